```python
import jax, jax.numpy as jnp
from jax import lax
import numpy as np

D_MODEL = 1024
BATCH = 8
SEQ = 4096
DEPTH = 4

GRID_W = 64
BLOCK_Q = 128
CHUNK = 128
ROPE_THETA = 10000.0
EPS = 1e-6
A_HEADS = 8
A_KV_HEADS = 2
A_HEAD_DIM = 64
A_WIDTH = A_HEADS * A_HEAD_DIM
A_KV_WIDTH = A_KV_HEADS * A_HEAD_DIM
B_GROUPS = 4
B_GROUP_DIM = 128
B_WIDTH = B_GROUPS * B_GROUP_DIM
M_HEADS = 4
M_HEAD_DIM = 128
M_WIDTH = M_HEADS * M_HEAD_DIM
MEM_LEN = 256
N_BRANCH = 3
BRANCH_WIDTH = 512
IN_SPLITS = (A_WIDTH, A_KV_WIDTH, A_KV_WIDTH, A_WIDTH,
             B_WIDTH, B_WIDTH, B_WIDTH,
             M_WIDTH, M_WIDTH,
             N_BRANCH * D_MODEL)
IN_WIDTH = sum(IN_SPLITS)

kernel_name = "hybrid_gqa_gmlp_memory_encoder"


def _split_points():
    pts, acc = [], 0
    for s in IN_SPLITS[:-1]:
        acc += s
        pts.append(acc)
    return pts


def rms_norm(x, g):
    xf = x.astype(jnp.float32)
    y = xf * lax.rsqrt(jnp.mean(xf * xf, axis=-1, keepdims=True) + EPS)
    return (y * g.astype(jnp.float32)).astype(x.dtype)


def layer_norm(x, g, b):
    xf = x.astype(jnp.float32)
    mu = jnp.mean(xf, axis=-1, keepdims=True)
    xc = xf - mu
    y = xc * lax.rsqrt(jnp.mean(xc * xc, axis=-1, keepdims=True) + EPS)
    return (y * g.astype(jnp.float32) + b.astype(jnp.float32)).astype(x.dtype)


def axial_rope_tables(seq):
    rows = seq // GRID_W
    row = jnp.repeat(jnp.arange(rows, dtype=jnp.float32), GRID_W)
    col = jnp.tile(jnp.arange(GRID_W, dtype=jnp.float32), rows)
    n_freq = A_HEAD_DIM // 4
    inv = ROPE_THETA ** (-jnp.arange(n_freq, dtype=jnp.float32) / n_freq)
    ang = jnp.stack([row[:, None] * inv, col[:, None] * inv], axis=1)
    return jnp.cos(ang), jnp.sin(ang)


def apply_axial_rope(x, cos, sin):
    b, s, h, d = x.shape
    x5 = x.reshape(b, s, h, 2, 2, d // 4)
    x1, x2 = x5[..., 0, :], x5[..., 1, :]
    c = cos[None, :, None].astype(x.dtype)
    sn = sin[None, :, None].astype(x.dtype)
    out = jnp.stack([x1 * c - x2 * sn, x2 * c + x1 * sn], axis=-2)
    return out.reshape(b, s, h, d)


def gqa_axial_attention(q, k, v, q_g, k_g, cos, sin):
    bsz, seq, _ = q.shape
    grp = A_HEADS // A_KV_HEADS
    q = apply_axial_rope(rms_norm(q.reshape(bsz, seq, A_HEADS, A_HEAD_DIM), q_g), cos, sin)
    k = apply_axial_rope(rms_norm(k.reshape(bsz, seq, A_KV_HEADS, A_HEAD_DIM), k_g), cos, sin)
    v = v.reshape(bsz, seq, A_KV_HEADS, A_HEAD_DIM)
    scale = A_HEAD_DIM ** -0.5
    nblk = seq // BLOCK_Q
    qb = q.reshape(bsz, nblk, BLOCK_Q, A_KV_HEADS, grp, A_HEAD_DIM).transpose(1, 0, 3, 4, 2, 5)
    kt = k.transpose(0, 2, 1, 3)
    vt = v.transpose(0, 2, 1, 3)

    def one_block(qblk):
        s = jnp.einsum('bkgqd,bksd->bkgqs', qblk, kt,
                       preferred_element_type=jnp.float32) * scale
        p = jax.nn.softmax(s, axis=-1)
        return jnp.einsum('bkgqs,bksd->bkgqd', p.astype(vt.dtype), vt)

    o = lax.map(one_block, qb)
    return o.transpose(1, 0, 4, 2, 3, 5).reshape(bsz, seq, A_WIDTH)


def chunked_spatial_gating(u, v, ln_g, ln_b, w_s, b_s):
    bsz, seq, _ = v.shape
    v = layer_norm(v, ln_g, ln_b)
    vc = v.reshape(bsz, seq // CHUNK, CHUNK, B_GROUPS, B_GROUP_DIM)
    mixed = jnp.einsum('gpq,bnqgc->bnpgc', w_s, vc) + b_s.T[None, None, :, :, None]
    return u * mixed.reshape(bsz, seq, B_WIDTH)


def memory_cross_attention(q, mem_n, w_kv):
    bsz, seq, _ = q.shape
    kv = mem_n @ w_kv
    k, v = jnp.split(kv, 2, axis=-1)
    k = k.reshape(bsz, -1, M_HEADS, M_HEAD_DIM)
    v = v.reshape(bsz, -1, M_HEADS, M_HEAD_DIM)
    q = q.reshape(bsz, seq, M_HEADS, M_HEAD_DIM)
    s = jnp.einsum('bshd,bmhd->bhsm', q, k,
                   preferred_element_type=jnp.float32) * (M_HEAD_DIM ** -0.5)
    p = jax.nn.softmax(s, axis=-1)
    o = jnp.einsum('bhsm,bmhd->bshd', p.astype(v.dtype), v)
    return o.reshape(bsz, seq, M_WIDTH)


def _fwd_setup_inputs(seed: int = 0) -> dict:
    key = jax.random.key(seed)
    ks = jax.random.split(key, 16)
    nrm = jax.random.normal
    f32 = jnp.float32
    return {
        "x": nrm(ks[0], (BATCH, SEQ, D_MODEL), f32),
        "mem": nrm(ks[1], (BATCH, MEM_LEN, D_MODEL), f32),
        "norm_g": 1.0 + 0.01 * nrm(ks[2], (DEPTH, D_MODEL), f32),
        "w_in": nrm(ks[3], (DEPTH, D_MODEL, IN_WIDTH), f32) * D_MODEL ** -0.5,
        "q_norm_g": 1.0 + 0.01 * nrm(ks[4], (DEPTH, A_HEAD_DIM), f32),
        "k_norm_g": 1.0 + 0.01 * nrm(ks[5], (DEPTH, A_HEAD_DIM), f32),
        "sg_ln_g": 1.0 + 0.01 * nrm(ks[6], (DEPTH, B_WIDTH), f32),
        "sg_ln_b": 0.01 * nrm(ks[7], (DEPTH, B_WIDTH), f32),
        "w_s": nrm(ks[8], (DEPTH, B_GROUPS, CHUNK, CHUNK), f32) * CHUNK ** -0.5,
        "b_s": 1.0 + 0.01 * nrm(ks[9], (DEPTH, B_GROUPS, CHUNK), f32),
        "mem_norm_g": 1.0 + 0.01 * nrm(ks[10], (DEPTH, D_MODEL), f32),
        "w_mem_kv": nrm(ks[11], (DEPTH, D_MODEL, 2 * M_WIDTH), f32) * D_MODEL ** -0.5,
        "w_br": nrm(ks[12], (DEPTH, N_BRANCH, BRANCH_WIDTH, D_MODEL), f32) * BRANCH_WIDTH ** -0.5,
        "w_out": nrm(ks[13], (DEPTH, D_MODEL, D_MODEL), f32) * D_MODEL ** -0.5,
        "final_g": 1.0 + 0.01 * nrm(ks[14], (D_MODEL,), f32),
    }


def _fwd_reference(x, mem, norm_g, w_in, q_norm_g, k_norm_g, sg_ln_g, sg_ln_b, w_s, b_s,
              mem_norm_g, w_mem_kv, w_br, w_out, final_g):
    bsz, seq, d = x.shape
    cos, sin = axial_rope_tables(seq)
    pts = _split_points()
    for l in range(DEPTH):
        h = rms_norm(x, norm_g[l])
        proj = h @ w_in[l]
        qA, kA, vA, zA, uB, vB, zB, qM, zM, g_logits = jnp.split(proj, pts, axis=-1)
        yA = gqa_axial_attention(qA, kA, vA, q_norm_g[l], k_norm_g[l], cos, sin) * jax.nn.silu(zA)
        yB = chunked_spatial_gating(uB, vB, sg_ln_g[l], sg_ln_b[l], w_s[l], b_s[l]) * jax.nn.silu(zB)
        mem_n = rms_norm(mem, mem_norm_g[l])
        yM = memory_cross_attention(qM, mem_n, w_mem_kv[l]) * jax.nn.silu(zM)
        branches = jnp.stack([yA, yB, yM], axis=2)
        up = jnp.einsum('bsnw,nwd->bsnd', branches, w_br[l])
        gates = jax.nn.sigmoid(g_logits.reshape(bsz, seq, N_BRANCH, d))
        merged = jnp.sum(gates * up, axis=2)
        x = x + merged @ w_out[l]
    return rms_norm(x, final_g)


import jax as _jax
import jax.numpy as _jnp

TWIN_FORMAT = 'train_step'
FWD_PARAMS = ['x', 'mem', 'norm_g', 'w_in', 'q_norm_g', 'k_norm_g', 'sg_ln_g', 'sg_ln_b', 'w_s', 'b_s', 'mem_norm_g', 'w_mem_kv', 'w_br', 'w_out', 'final_g']
TWIN_WEIGHTS = ['norm_g', 'w_in', 'q_norm_g', 'k_norm_g', 'sg_ln_g', 'sg_ln_b', 'w_s', 'b_s', 'mem_norm_g', 'w_mem_kv', 'w_br', 'w_out', 'final_g']
TWIN_DIFF_INPUT = 'x'
TWIN_INPUTS = ['x', 'mem', 'norm_g', 'w_in', 'q_norm_g', 'k_norm_g', 'sg_ln_g', 'sg_ln_b', 'w_s', 'b_s', 'mem_norm_g', 'w_mem_kv', 'w_br', 'w_out', 'final_g', 'loss_target', 'm_norm_g', 'm_w_in', 'm_q_norm_g', 'm_k_norm_g', 'm_sg_ln_g', 'm_sg_ln_b', 'm_w_s', 'm_b_s', 'm_mem_norm_g', 'm_w_mem_kv', 'm_w_br', 'm_w_out', 'm_final_g', 'v_norm_g', 'v_w_in', 'v_q_norm_g', 'v_k_norm_g', 'v_sg_ln_g', 'v_sg_ln_b', 'v_w_s', 'v_b_s', 'v_mem_norm_g', 'v_w_mem_kv', 'v_w_br', 'v_w_out', 'v_final_g']
TWIN_OUTPUTS = ['loss', 'grad_x', 'grad_norm_g', 'grad_w_in', 'grad_q_norm_g', 'grad_k_norm_g', 'grad_sg_ln_g', 'grad_sg_ln_b', 'grad_w_s', 'grad_b_s', 'grad_mem_norm_g', 'grad_w_mem_kv', 'grad_w_br', 'grad_w_out', 'grad_final_g', 'delta_norm_g', 'delta_w_in', 'delta_q_norm_g', 'delta_k_norm_g', 'delta_sg_ln_g', 'delta_sg_ln_b', 'delta_w_s', 'delta_b_s', 'delta_mem_norm_g', 'delta_w_mem_kv', 'delta_w_br', 'delta_w_out', 'delta_final_g', 'new_m_norm_g', 'new_m_w_in', 'new_m_q_norm_g', 'new_m_k_norm_g', 'new_m_sg_ln_g', 'new_m_sg_ln_b', 'new_m_w_s', 'new_m_b_s', 'new_m_mem_norm_g', 'new_m_w_mem_kv', 'new_m_w_br', 'new_m_w_out', 'new_m_final_g', 'new_v_norm_g', 'new_v_w_in', 'new_v_q_norm_g', 'new_v_k_norm_g', 'new_v_sg_ln_g', 'new_v_sg_ln_b', 'new_v_w_s', 'new_v_b_s', 'new_v_mem_norm_g', 'new_v_w_mem_kv', 'new_v_w_br', 'new_v_w_out', 'new_v_final_g']
TWIN_LEAF_KINDS = {'loss': 'loss', 'grad_x': 'grad_x', 'grad_norm_g': 'grad_w', 'grad_w_in': 'grad_w', 'grad_q_norm_g': 'grad_w', 'grad_k_norm_g': 'grad_w', 'grad_sg_ln_g': 'grad_w', 'grad_sg_ln_b': 'grad_w', 'grad_w_s': 'grad_w', 'grad_b_s': 'grad_w', 'grad_mem_norm_g': 'grad_w', 'grad_w_mem_kv': 'grad_w', 'grad_w_br': 'grad_w', 'grad_w_out': 'grad_w', 'grad_final_g': 'grad_w', 'delta_norm_g': 'delta_w', 'delta_w_in': 'delta_w', 'delta_q_norm_g': 'delta_w', 'delta_k_norm_g': 'delta_w', 'delta_sg_ln_g': 'delta_w', 'delta_sg_ln_b': 'delta_w', 'delta_w_s': 'delta_w', 'delta_b_s': 'delta_w', 'delta_mem_norm_g': 'delta_w', 'delta_w_mem_kv': 'delta_w', 'delta_w_br': 'delta_w', 'delta_w_out': 'delta_w', 'delta_final_g': 'delta_w', 'new_m_norm_g': 'new_m', 'new_m_w_in': 'new_m', 'new_m_q_norm_g': 'new_m', 'new_m_k_norm_g': 'new_m', 'new_m_sg_ln_g': 'new_m', 'new_m_sg_ln_b': 'new_m', 'new_m_w_s': 'new_m', 'new_m_b_s': 'new_m', 'new_m_mem_norm_g': 'new_m', 'new_m_w_mem_kv': 'new_m', 'new_m_w_br': 'new_m', 'new_m_w_out': 'new_m', 'new_m_final_g': 'new_m', 'new_v_norm_g': 'new_v', 'new_v_w_in': 'new_v', 'new_v_q_norm_g': 'new_v', 'new_v_k_norm_g': 'new_v', 'new_v_sg_ln_g': 'new_v', 'new_v_sg_ln_b': 'new_v', 'new_v_w_s': 'new_v', 'new_v_b_s': 'new_v', 'new_v_mem_norm_g': 'new_v', 'new_v_w_mem_kv': 'new_v', 'new_v_w_br': 'new_v', 'new_v_w_out': 'new_v', 'new_v_final_g': 'new_v'}


def _forward(args):
    return _fwd_reference(*[args[k] for k in FWD_PARAMS])


def _output_shape():
    out = _jax.eval_shape(lambda: _forward(_fwd_setup_inputs(0)))
    return out.shape, out.dtype

N_MICROBATCH = 1
ADAM_LR = 0.001
ADAM_B1 = 0.9
ADAM_B2 = 0.999
ADAM_EPS = 1e-08
ADAM_WD = 0.01
ADAM_STEP = 10
PER_EXAMPLE_BATCH_AXIS = {'x': 0, 'mem': 0, 'loss_target': 0}
SHARED_INPUTS = []
_WEIGHT_DTYPES = {'norm_g': _jnp.float32, 'w_in': _jnp.float32, 'q_norm_g': _jnp.float32, 'k_norm_g': _jnp.float32, 'sg_ln_g': _jnp.float32, 'sg_ln_b': _jnp.float32, 'w_s': _jnp.float32, 'b_s': _jnp.float32, 'mem_norm_g': _jnp.float32, 'w_mem_kv': _jnp.float32, 'w_br': _jnp.float32, 'w_out': _jnp.float32, 'final_g': _jnp.float32}
MOMENT_SCALE = {'norm_g': 1.325107e-01, 'w_in': 5.026566e-02, 'q_norm_g': 2.236514e-02, 'k_norm_g': 2.311822e-02, 'sg_ln_g': 8.103261e-02, 'sg_ln_b': 7.973431e-02, 'w_s': 7.934190e-02, 'b_s': 7.875775e-02, 'mem_norm_g': 8.904072e-03, 'w_mem_kv': 8.378338e-03, 'w_br': 4.550579e-02, 'w_out': 7.875527e-02, 'final_g': 3.198979e+01}


def _to_microbatches(a, axis):
    t = _jnp.moveaxis(a, axis, 0)
    t = t.reshape((N_MICROBATCH, t.shape[0] // N_MICROBATCH) + t.shape[1:])
    return _jnp.moveaxis(t, 1, axis + 1)


def setup_inputs(seed: int = 0) -> dict:
    inp = _fwd_setup_inputs(seed)
    key = _jax.random.fold_in(_jax.random.key(seed), 7919)
    shape, _ = _output_shape()
    out = dict(inp)
    out["loss_target"] = _jax.random.normal(_jax.random.fold_in(key, 0), shape, _jnp.float32)
    for i, name in enumerate(TWIN_WEIGHTS):
        w = inp[name].astype(_jnp.float32)
        if MOMENT_SCALE is None:
            s = _jnp.sqrt(_jnp.mean(_jnp.square(w)) + 1e-30)
        else:
            s = MOMENT_SCALE[name]
        km, kv = _jax.random.split(_jax.random.fold_in(key, i + 1))
        out[name] = w
        out["m_" + name] = s * _jax.random.normal(km, w.shape, _jnp.float32)
        out["v_" + name] = (s * s) * _jax.random.uniform(kv, w.shape, _jnp.float32, 0.5, 1.5)
    if N_MICROBATCH > 1:
        for name, axis in PER_EXAMPLE_BATCH_AXIS.items():
            out[name] = _to_microbatches(out[name], axis)
    return {'x': out['x'], 'mem': out['mem'], 'norm_g': out['norm_g'], 'w_in': out['w_in'], 'q_norm_g': out['q_norm_g'], 'k_norm_g': out['k_norm_g'], 'sg_ln_g': out['sg_ln_g'], 'sg_ln_b': out['sg_ln_b'], 'w_s': out['w_s'], 'b_s': out['b_s'], 'mem_norm_g': out['mem_norm_g'], 'w_mem_kv': out['w_mem_kv'], 'w_br': out['w_br'], 'w_out': out['w_out'], 'final_g': out['final_g'], 'loss_target': out['loss_target'], 'm_norm_g': out['m_norm_g'], 'm_w_in': out['m_w_in'], 'm_q_norm_g': out['m_q_norm_g'], 'm_k_norm_g': out['m_k_norm_g'], 'm_sg_ln_g': out['m_sg_ln_g'], 'm_sg_ln_b': out['m_sg_ln_b'], 'm_w_s': out['m_w_s'], 'm_b_s': out['m_b_s'], 'm_mem_norm_g': out['m_mem_norm_g'], 'm_w_mem_kv': out['m_w_mem_kv'], 'm_w_br': out['m_w_br'], 'm_w_out': out['m_w_out'], 'm_final_g': out['m_final_g'], 'v_norm_g': out['v_norm_g'], 'v_w_in': out['v_w_in'], 'v_q_norm_g': out['v_q_norm_g'], 'v_k_norm_g': out['v_k_norm_g'], 'v_sg_ln_g': out['v_sg_ln_g'], 'v_sg_ln_b': out['v_sg_ln_b'], 'v_w_s': out['v_w_s'], 'v_b_s': out['v_b_s'], 'v_mem_norm_g': out['v_mem_norm_g'], 'v_w_mem_kv': out['v_w_mem_kv'], 'v_w_br': out['v_w_br'], 'v_w_out': out['v_w_out'], 'v_final_g': out['v_final_g']}


def _loss(weights, diff, rest, loss_target):
    with _jax.named_scope("forward"):
        args = {**rest, TWIN_DIFF_INPUT: diff, **{k: w.astype(_WEIGHT_DTYPES[k]) for k, w in weights.items()}}
        y = _forward(args)
    with _jax.named_scope("loss_head"):
        err = _jnp.square(y.astype(_jnp.float32) - loss_target)
        return 0.5 * _jnp.sum(_jnp.mean(err, axis=-1)) if err.ndim else 0.5 * err


def _adamw(w, g, m, v):
    m = ADAM_B1 * m + (1.0 - ADAM_B1) * g
    v = ADAM_B2 * v + (1.0 - ADAM_B2) * _jnp.square(g)
    m_hat = m / (1.0 - ADAM_B1 ** ADAM_STEP)
    v_hat = v / (1.0 - ADAM_B2 ** ADAM_STEP)
    delta = -ADAM_LR * (m_hat / (_jnp.sqrt(v_hat) + ADAM_EPS) + ADAM_WD * w)
    return delta, m, v


def reference(x, mem, norm_g, w_in, q_norm_g, k_norm_g, sg_ln_g, sg_ln_b, w_s, b_s, mem_norm_g, w_mem_kv, w_br, w_out, final_g, loss_target, m_norm_g, m_w_in, m_q_norm_g, m_k_norm_g, m_sg_ln_g, m_sg_ln_b, m_w_s, m_b_s, m_mem_norm_g, m_w_mem_kv, m_w_br, m_w_out, m_final_g, v_norm_g, v_w_in, v_q_norm_g, v_k_norm_g, v_sg_ln_g, v_sg_ln_b, v_w_s, v_b_s, v_mem_norm_g, v_w_mem_kv, v_w_br, v_w_out, v_final_g):
    given = dict(x=x, mem=mem, norm_g=norm_g, w_in=w_in, q_norm_g=q_norm_g, k_norm_g=k_norm_g, sg_ln_g=sg_ln_g, sg_ln_b=sg_ln_b, w_s=w_s, b_s=b_s, mem_norm_g=mem_norm_g, w_mem_kv=w_mem_kv, w_br=w_br, w_out=w_out, final_g=final_g, loss_target=loss_target, m_norm_g=m_norm_g, m_w_in=m_w_in, m_q_norm_g=m_q_norm_g, m_k_norm_g=m_k_norm_g, m_sg_ln_g=m_sg_ln_g, m_sg_ln_b=m_sg_ln_b, m_w_s=m_w_s, m_b_s=m_b_s, m_mem_norm_g=m_mem_norm_g, m_w_mem_kv=m_w_mem_kv, m_w_br=m_w_br, m_w_out=m_w_out, m_final_g=m_final_g, v_norm_g=v_norm_g, v_w_in=v_w_in, v_q_norm_g=v_q_norm_g, v_k_norm_g=v_k_norm_g, v_sg_ln_g=v_sg_ln_g, v_sg_ln_b=v_sg_ln_b, v_w_s=v_w_s, v_b_s=v_b_s, v_mem_norm_g=v_mem_norm_g, v_w_mem_kv=v_w_mem_kv, v_w_br=v_w_br, v_w_out=v_w_out, v_final_g=v_final_g)
    weights = {n: given[n] for n in TWIN_WEIGHTS}
    shared = {n: given[n] for n in SHARED_INPUTS}
    per_example = {n: given[n] for n in ['x', 'mem']}
    grad_fn = _jax.value_and_grad(_loss, argnums=(0, 1))

    def one_microbatch(ex, loss_target):
        ex = dict(ex)
        diff = ex.pop(TWIN_DIFF_INPUT)
        return grad_fn(weights, diff, {**shared, **ex}, loss_target)

    if N_MICROBATCH == 1:
        loss, (grad_w, grad_x) = one_microbatch(per_example, given["loss_target"])
    else:
        def body(carry, xs):
            loss_sum, grad_sum = carry
            l_k, (gw_k, gx_k) = one_microbatch(xs[0], xs[1])
            with _jax.named_scope("update"):
                return (loss_sum + l_k, _jax.tree.map(_jnp.add, grad_sum, gw_k)), gx_k

        init = (_jnp.zeros((), _jnp.float32), _jax.tree.map(_jnp.zeros_like, weights))
        (loss, grad_w), grad_x = _jax.lax.scan(body, init, (per_example, given["loss_target"]))
    with _jax.named_scope("update"):
        delta_w, new_m, new_v = {}, {}, {}
        for n in TWIN_WEIGHTS:
            delta_w[n], new_m[n], new_v[n] = _adamw(weights[n], grad_w[n], given["m_" + n], given["v_" + n])
    return (loss, grad_x, *[grad_w[n] for n in TWIN_WEIGHTS], *[delta_w[n] for n in TWIN_WEIGHTS],
            *[new_m[n] for n in TWIN_WEIGHTS], *[new_v[n] for n in TWIN_WEIGHTS])
```

```python
import functools

import jax
import jax.numpy as jnp
from jax import lax
from jax.experimental import pallas as pl
from jax.experimental.pallas import tpu as pltpu

F32 = jnp.float32
BF16 = jnp.bfloat16

D_MODEL = 1024
GRID_W = 64
CHUNK = 128
ROPE_THETA = 10000.0
EPS = 1e-6
A_HEADS, A_KV_HEADS, A_HEAD_DIM = 8, 2, 64
A_WIDTH, A_KV_WIDTH = 512, 128
B_GROUPS, B_GROUP_DIM, B_WIDTH = 4, 128, 512
M_HEADS, M_HEAD_DIM, M_WIDTH = 4, 128, 512
N_BRANCH = 3
IN_WIDTH = 6912
O_QA, O_KA, O_VA, O_ZA, O_UB, O_VB, O_ZB, O_QM, O_ZM, O_LG = 0, 512, 640, 768, 1280, 1792, 2304, 2816, 3328, 3840
PBLK = 768
N_PBLK = IN_WIDTH // PBLK
MID_W = 3072
LG_W = 3072

ADAM_LR, ADAM_B1, ADAM_B2, ADAM_EPS, ADAM_WD, ADAM_STEP = 0.001, 0.9, 0.999, 1e-08, 0.01, 10

V7X_VMEM_BYTES = 64 * 2**20
VMEM_LIMIT = V7X_VMEM_BYTES - 8 * 2**20
LANES = 128
MESH = pl.DeviceIdType.MESH
N_CHIPS = 4


def _cp(*sem):
    return pltpu.CompilerParams(dimension_semantics=sem if sem else None, vmem_limit_bytes=VMEM_LIMIT)


def _dot(a, b):
    return jnp.dot(a, b, preferred_element_type=F32)


def _dot_nt(a, b):
    return lax.dot_general(a, b, (((1,), (1,)), ((), ())), preferred_element_type=F32)


def _dot_tn(a, b):
    return lax.dot_general(a, b, (((0,), (0,)), ((), ())), preferred_element_type=F32)


def _dot_hi(a, b):
    return jnp.dot(a, b, preferred_element_type=F32, precision=lax.Precision.HIGHEST)


def _dot_nt_hi(a, b):
    return lax.dot_general(a, b, (((1,), (1,)), ((), ())), preferred_element_type=F32, precision=lax.Precision.HIGHEST)


def _sig(z):
    return 1.0 / (1.0 + jnp.exp(-z))


def _full(shape):
    nd = len(shape)
    return pl.BlockSpec(shape, lambda *_: (0,) * nd)


def _rows(tm, width):
    return pl.BlockSpec((tm, width), lambda i: (i, 0))


def _sds(shape, dtype):
    return jax.ShapeDtypeStruct(shape, dtype)


def rms_fwd(x, g):
    s, d = x.shape
    tm = min(s, 512)

    def body(x_ref, g_ref, h_ref):
        xf = x_ref[...]
        r = lax.rsqrt(jnp.mean(xf * xf, axis=-1, keepdims=True) + EPS)
        h_ref[...] = ((xf * r) * g_ref[...]).astype(BF16)

    return pl.pallas_call(
        body, out_shape=_sds((s, d), BF16), grid=(s // tm,),
        in_specs=[_rows(tm, d), _full((1, d))], out_specs=_rows(tm, d),
        compiler_params=_cp("parallel"), name="rms_fwd")(x, g)


def proj_fwd(h, w_t):
    s, d = h.shape
    n = w_t.shape[0]
    tm = min(s, 1024)
    tn = 1152

    def body(h_ref, w_ref, o_ref):
        o_ref[...] = _dot_nt(h_ref[...], w_ref[...])

    return pl.pallas_call(
        body, out_shape=_sds((s, n), F32), grid=(n // tn, s // tm),
        in_specs=[pl.BlockSpec((tm, d), lambda j, i: (i, 0)), pl.BlockSpec((tn, d), lambda j, i: (j, 0))],
        out_specs=pl.BlockSpec((tm, tn), lambda j, i: (i, j)),
        compiler_params=_cp("parallel", "parallel"), name="proj_fwd")(h, w_t)


def rope_tables(seq):
    rows = seq // GRID_W
    row = jnp.repeat(jnp.arange(rows, dtype=F32), GRID_W)
    col = jnp.tile(jnp.arange(GRID_W, dtype=F32), rows)
    n_freq = A_HEAD_DIM // 4
    inv = ROPE_THETA ** (-jnp.arange(n_freq, dtype=F32) / n_freq)
    ang = jnp.stack([row[:, None] * inv, col[:, None] * inv], axis=1)
    cos, sin = jnp.cos(ang), jnp.sin(ang)
    zero = jnp.zeros_like(sin[:, 0])
    c64 = jnp.concatenate([cos[:, 0], cos[:, 0], cos[:, 1], cos[:, 1]], axis=1)
    sa64 = jnp.concatenate([zero, sin[:, 0], zero, sin[:, 1]], axis=1)
    sb64 = jnp.concatenate([-sin[:, 0], zero, -sin[:, 1], zero], axis=1)
    two = lambda t: jnp.concatenate([t, t], axis=1)
    return two(c64), two(sa64), two(sb64)


def _group_ones(width, group):
    i = jnp.arange(width)
    return (i[:, None] // group == i[None, :] // group).astype(F32)


def _rope(xn, c, sa, sb):
    w = xn.shape[1]
    return xn * c + pltpu.roll(xn, 16, 1) * sa + pltpu.roll(xn, w - 16, 1) * sb


def _rope_t(dy, c, sa, sb):
    w = dy.shape[1]
    return dy * c + pltpu.roll(dy * sa, w - 16, 1) + pltpu.roll(dy * sb, 16, 1)


def _tile4(t):
    return jnp.concatenate([t, t, t, t], axis=1)


def qk_prep(proj, tabs, qg, kg, gq, gk):
    s = proj.shape[0]
    tm = min(s, 512)
    c, sa, sb = tabs

    def body(p_ref, c_ref, sa_ref, sb_ref, qg_ref, kg_ref, gq_ref, gk_ref, qt_ref, kr_ref, krt_ref, vb_ref, vt_ref):
        xq = p_ref[:, O_QA:O_QA + A_WIDTH]
        xk = p_ref[:, O_KA:O_KA + A_KV_WIDTH]
        xv = p_ref[:, O_VA:O_VA + A_KV_WIDTH]
        cc, ssa, ssb = c_ref[...], sa_ref[...], sb_ref[...]
        msq = _dot_hi(xq * xq, gq_ref[...]) * (1.0 / A_HEAD_DIM)
        qn = (xq * lax.rsqrt(msq + EPS)) * qg_ref[...]
        qr = _rope(qn, _tile4(cc), _tile4(ssa), _tile4(ssb)) * (A_HEAD_DIM ** -0.5)
        qt_ref[...] = qr.T.astype(BF16)
        msk = _dot_hi(xk * xk, gk_ref[...]) * (1.0 / A_HEAD_DIM)
        kn = (xk * lax.rsqrt(msk + EPS)) * kg_ref[...]
        kr = _rope(kn, cc, ssa, ssb)
        kr_ref[...] = kr.astype(BF16)
        krt_ref[...] = kr.T.astype(BF16)
        vb_ref[...] = xv.astype(BF16)
        vt_ref[...] = xv.T.astype(BF16)

    tab = _rows(tm, LANES)
    colb = lambda w: pl.BlockSpec((w, tm), lambda i: (0, i))
    return pl.pallas_call(
        body,
        out_shape=(_sds((A_WIDTH, s), BF16), _sds((s, A_KV_WIDTH), BF16), _sds((A_KV_WIDTH, s), BF16),
                   _sds((s, A_KV_WIDTH), BF16), _sds((A_KV_WIDTH, s), BF16)),
        grid=(s // tm,),
        in_specs=[_rows(tm, PBLK), tab, tab, tab, _full((1, A_WIDTH)), _full((1, A_KV_WIDTH)),
                  _full((A_WIDTH, A_WIDTH)), _full((A_KV_WIDTH, A_KV_WIDTH))],
        out_specs=(colb(A_WIDTH), _rows(tm, A_KV_WIDTH), colb(A_KV_WIDTH), _rows(tm, A_KV_WIDTH), colb(A_KV_WIDTH)),
        compiler_params=_cp("parallel"), name="qk_prep")(proj, c, sa, sb, qg, kg, gq, gk)


def _pad_head(q_h, kv):
    z = jnp.zeros_like(q_h)
    return jnp.concatenate([q_h, z], axis=0) if kv == 0 else jnp.concatenate([z, q_h], axis=0)


def attn_fwd(q_t, kr, v_t):
    s = kr.shape[0]
    tq = min(s, 256)
    kc = min(s, 512)
    nkc = s // kc
    grp = A_HEADS // A_KV_HEADS

    def body(qt_ref, kr_ref, vt_ref, o_ref, lse_ref):
        outs, lses = [], []
        for h in range(A_HEADS):
            kv = h // grp
            qpad = _pad_head(qt_ref[A_HEAD_DIM * h:A_HEAD_DIM * (h + 1), :], kv)

            def step(ci, carry, qpad=qpad):
                m, l, acc = carry
                k0 = pl.multiple_of(ci * kc, kc)
                sc = _dot(kr_ref[pl.ds(k0, kc), :], qpad)
                m_new = jnp.maximum(m, jnp.max(sc, axis=0, keepdims=True))
                alpha = jnp.exp(m - m_new)
                p = jnp.exp(sc - m_new)
                l = l * alpha + jnp.sum(p, axis=0, keepdims=True)
                acc = acc * alpha + _dot(vt_ref[:, pl.ds(k0, kc)], p.astype(BF16))
                return m_new, l, acc

            m0 = jnp.full((1, tq), -1e30, F32)
            m, l, acc = lax.fori_loop(0, nkc, step, (m0, jnp.zeros((1, tq), F32), jnp.zeros((A_KV_WIDTH, tq), F32)))
            outs.append(acc[A_HEAD_DIM * kv:A_HEAD_DIM * (kv + 1), :] / l)
            lses.append(m + jnp.log(l))
        o_ref[...] = jnp.concatenate(outs, axis=0).T
        lse_ref[...] = jnp.concatenate(lses, axis=0)

    return pl.pallas_call(
        body, out_shape=(_sds((s, A_WIDTH), F32), _sds((A_HEADS, s), F32)), grid=(s // tq,),
        in_specs=[pl.BlockSpec((A_WIDTH, tq), lambda i: (0, i)), _full((s, A_KV_WIDTH)), _full((A_KV_WIDTH, s))],
        out_specs=(_rows(tq, A_WIDTH), pl.BlockSpec((A_HEADS, tq), lambda i: (0, i))),
        compiler_params=_cp("parallel"), name="attn_fwd")(q_t, kr, v_t)


def memkv_fwd(mem, g, w_kv):
    m, d = mem.shape

    def body(mem_ref, g_ref, w_ref, mn_ref, kv_ref):
        mf = mem_ref[...]
        r = lax.rsqrt(jnp.mean(mf * mf, axis=-1, keepdims=True) + EPS)
        mn = ((mf * r) * g_ref[...]).astype(BF16)
        mn_ref[...] = mn
        kv_ref[...] = _dot(mn, w_ref[...]).astype(BF16)

    return pl.pallas_call(
        body, out_shape=(_sds((m, d), BF16), _sds((m, 2 * M_WIDTH), BF16)),
        compiler_params=_cp(), name="memkv_fwd")(mem, g, w_kv)


def _layer_norm_stats(v):
    mu = jnp.mean(v, axis=-1, keepdims=True)
    xc = v - mu
    rstd = lax.rsqrt(jnp.mean(xc * xc, axis=-1, keepdims=True) + EPS)
    return xc * rstd, rstd


def _spatial_mix(vlb, ws_ref, bsb_ref, tm):
    rows = []
    for ci in range(tm // CHUNK):
        cols = []
        for g in range(B_GROUPS):
            blk = vlb[ci * CHUNK:(ci + 1) * CHUNK, g * B_GROUP_DIM:(g + 1) * B_GROUP_DIM]
            cols.append(_dot(ws_ref[g], blk) + bsb_ref[g])
        rows.append(jnp.concatenate(cols, axis=1))
    return jnp.concatenate(rows, axis=0)


def _mem_attn(qm, kv_ref):
    out = []
    for h in range(M_HEADS):
        qh = qm[:, h * M_HEAD_DIM:(h + 1) * M_HEAD_DIM].astype(BF16)
        kh = kv_ref[:, h * M_HEAD_DIM:(h + 1) * M_HEAD_DIM]
        vh = kv_ref[:, M_WIDTH + h * M_HEAD_DIM:M_WIDTH + (h + 1) * M_HEAD_DIM]
        sc = _dot_nt(qh, kh) * (M_HEAD_DIM ** -0.5)
        e = jnp.exp(sc - jnp.max(sc, axis=-1, keepdims=True))
        p = e / jnp.sum(e, axis=-1, keepdims=True)
        out.append((p, _dot(p.astype(BF16), vh)))
    return out


def branch_fwd(x, proj, o_a, kv, ws, bsb, ln_g, ln_b, w_br, w_out):
    s, d = x.shape
    tm = min(s, 256)

    def body(x_ref, p_ref, oa_ref, kv_ref, ws_ref, bsb_ref, lg_ref, lb_ref, wbr_ref, wo_ref,
             xn_ref, y_ref, up_ref, mg_ref):
        seg = lambda o, w: p_ref[:, o:o + w]
        z_a, u_b, v_b, z_b = seg(O_ZA, A_WIDTH), seg(O_UB, B_WIDTH), seg(O_VB, B_WIDTH), seg(O_ZB, B_WIDTH)
        q_m, z_m = seg(O_QM, M_WIDTH), seg(O_ZM, M_WIDTH)
        xhat, _ = _layer_norm_stats(v_b)
        vln = xhat * lg_ref[...] + lb_ref[...]
        mixed = _spatial_mix(vln.astype(BF16), ws_ref, bsb_ref, tm)
        y_b = (u_b * mixed) * (z_b * _sig(z_b))
        o_m = jnp.concatenate([o for _, o in _mem_attn(q_m, kv_ref)], axis=1)
        y_a = oa_ref[...] * (z_a * _sig(z_a))
        y_m = o_m * (z_m * _sig(z_m))
        merged = None
        for n, yy in enumerate((y_a, y_b, y_m)):
            yb = yy.astype(BF16)
            y_ref[n] = yb
            up = _dot(yb, wbr_ref[n])
            up_ref[n] = up.astype(BF16)
            t = _sig(seg(O_LG + n * d, d)) * up
            merged = t if merged is None else merged + t
        mb = merged.astype(BF16)
        mg_ref[...] = mb
        xn_ref[...] = x_ref[...] + _dot(mb, wo_ref[...])

    return pl.pallas_call(
        body,
        out_shape=(_sds((s, d), F32), _sds((N_BRANCH, s, A_WIDTH), BF16), _sds((N_BRANCH, s, d), BF16), _sds((s, d), BF16)),
        grid=(s // tm,),
        in_specs=[_rows(tm, d), _rows(tm, IN_WIDTH), _rows(tm, A_WIDTH), _full(kv.shape), _full(ws.shape), _full(bsb.shape),
                  _full((1, B_WIDTH)), _full((1, B_WIDTH)), _full(w_br.shape), _full(w_out.shape)],
        out_specs=(_rows(tm, d), pl.BlockSpec((N_BRANCH, tm, A_WIDTH), lambda i: (0, i, 0)),
                   pl.BlockSpec((N_BRANCH, tm, d), lambda i: (0, i, 0)), _rows(tm, d)),
        compiler_params=_cp("parallel"), name="branch_fwd")(x, proj, o_a, kv, ws, bsb, ln_g, ln_b, w_br, w_out)


def final_loss(x, fg, tgt):
    s, d = x.shape
    tm = min(s, 512)

    def body(x_ref, g_ref, t_ref, ls_ref, dx_ref, gg_ref):
        @pl.when(pl.program_id(0) == 0)
        def _():
            ls_ref[...] = jnp.zeros_like(ls_ref)
            gg_ref[...] = jnp.zeros_like(gg_ref)

        xf = x_ref[...]
        g = g_ref[...]
        r = lax.rsqrt(jnp.mean(xf * xf, axis=-1, keepdims=True) + EPS)
        xh = xf * r
        e = xh * g - t_ref[...]
        sq = jnp.sum(jnp.sum(e * e, axis=0, keepdims=True), axis=1, keepdims=True)
        ls_ref[...] += jnp.broadcast_to(sq, ls_ref.shape)
        dy = e * (1.0 / d)
        gg_ref[...] += jnp.sum(dy * xh, axis=0, keepdims=True)
        gy = dy * g
        dx_ref[...] = r * (gy - xh * jnp.mean(gy * xh, axis=-1, keepdims=True))

    return pl.pallas_call(
        body, out_shape=(_sds((1, LANES), F32), _sds((s, d), F32), _sds((1, d), F32)), grid=(s // tm,),
        in_specs=[_rows(tm, d), _full((1, d)), _rows(tm, d)],
        out_specs=(_full((1, LANES)), _rows(tm, d), _full((1, d))),
        compiler_params=_cp("arbitrary"), name="final_loss")(x, fg, tgt)


def _pblocks(tm, first, count):
    return [pl.BlockSpec((tm, PBLK), functools.partial(lambda i, b: (i, b), b=first + k)) for k in range(count)]


def merge_bwd(dx, proj, y, up, merged, w_br, w_out):
    s, d = dx.shape
    tm = min(s, 256)
    nlg = LG_W // PBLK

    def body(dx_ref, l0, l1, l2, l3, y_ref, up_ref, mg_ref, wbr_ref, wo_ref, dy_ref, dlg_ref, gwo_ref, gwb_ref):
        @pl.when(pl.program_id(0) == 0)
        def _():
            gwo_ref[...] = jnp.zeros_like(gwo_ref)
            gwb_ref[...] = jnp.zeros_like(gwb_ref)

        dxb = dx_ref[...].astype(BF16)
        dmg = _dot_nt(dxb, wo_ref[...])
        gwo_ref[...] += _dot_tn(mg_ref[...], dxb)
        lg = jnp.concatenate([l0[...], l1[...], l2[...], l3[...]], axis=1)
        for n in range(N_BRANCH):
            g = _sig(lg[:, n * d:(n + 1) * d])
            dup = dmg * g
            dlg_ref[:, n * d:(n + 1) * d] = ((dup * up_ref[n].astype(F32)) * (1.0 - g)).astype(BF16)
            dupb = dup.astype(BF16)
            gwb_ref[n] += _dot_tn(y_ref[n], dupb)
            dy_ref[n] = _dot_nt(dupb, wbr_ref[n])

    return pl.pallas_call(
        body,
        out_shape=(_sds((N_BRANCH, s, A_WIDTH), F32), _sds((s, LG_W), BF16), _sds((d, d), F32), _sds(w_br.shape, F32)),
        grid=(s // tm,),
        in_specs=[_rows(tm, d)] + _pblocks(tm, O_LG // PBLK, nlg) + [
            pl.BlockSpec((N_BRANCH, tm, A_WIDTH), lambda i: (0, i, 0)), pl.BlockSpec((N_BRANCH, tm, d), lambda i: (0, i, 0)),
            _rows(tm, d), _full(w_br.shape), _full(w_out.shape)],
        out_specs=(pl.BlockSpec((N_BRANCH, tm, A_WIDTH), lambda i: (0, i, 0)), _rows(tm, LG_W), _full((d, d)), _full(w_br.shape)),
        compiler_params=_cp("arbitrary"), name="merge_bwd")(dx, proj, proj, proj, proj, y, up, merged, w_br, w_out)


def _dsilu(z, sg):
    return sg * (1.0 + z * (1.0 - sg))


def branch_bwd(dy, proj, o_a, kv, ws, ws_t, bsb, ln_g, ln_b, head_sel):
    s = proj.shape[0]
    tm = min(s, 256)
    nmid = MID_W // PBLK

    def body(dy_ref, m0, m1, m2, m3, oa_ref, kv_ref, ws_ref, wst_ref, bsb_ref, lg_ref, lb_ref, sel_ref,
             dmid_ref, dot_ref, dl_ref, gws_ref, gbs_ref, glg_ref, glb_ref, dkv_ref):
        @pl.when(pl.program_id(0) == 0)
        def _():
            for r in (gws_ref, gbs_ref, glg_ref, glb_ref, dkv_ref):
                r[...] = jnp.zeros_like(r)

        mid = jnp.concatenate([m0[...], m1[...], m2[...], m3[...]], axis=1)
        seg = lambda o, w: mid[:, o - O_ZA:o - O_ZA + w]
        z_a, u_b, v_b, z_b = seg(O_ZA, A_WIDTH), seg(O_UB, B_WIDTH), seg(O_VB, B_WIDTH), seg(O_ZB, B_WIDTH)
        q_m, z_m = seg(O_QM, M_WIDTH), seg(O_ZM, M_WIDTH)

        def put(o, v):
            dmid_ref[:, o - O_ZA:o - O_ZA + v.shape[1]] = v.astype(BF16)

        dy_a, dy_b, dy_m = dy_ref[0], dy_ref[1], dy_ref[2]

        o_a_ = oa_ref[...]
        sg = _sig(z_a)
        do_a = dy_a * (z_a * sg)
        put(O_ZA, (dy_a * o_a_) * _dsilu(z_a, sg))
        dot_ref[...] = do_a.T.astype(BF16)
        dl_ref[...] = _dot_nt_hi(sel_ref[...], do_a * o_a_)

        xhat, rstd = _layer_norm_stats(v_b)
        lng = lg_ref[...]
        vln = xhat * lng + lb_ref[...]
        vlb = vln.astype(BF16)
        mixed = _spatial_mix(vlb, ws_ref, bsb_ref, tm)
        sg = _sig(z_b)
        sl = z_b * sg
        put(O_UB, (dy_b * mixed) * sl)
        put(O_ZB, ((dy_b * u_b) * mixed) * _dsilu(z_b, sg))
        dmix = (dy_b * u_b) * sl
        dmb = dmix.astype(BF16)
        rows = []
        for ci in range(tm // CHUNK):
            cols = []
            for g in range(B_GROUPS):
                rs, cs = slice(ci * CHUNK, (ci + 1) * CHUNK), slice(g * B_GROUP_DIM, (g + 1) * B_GROUP_DIM)
                gws_ref[g] += _dot_nt(dmb[rs, cs], vlb[rs, cs])
                gbs_ref[g] += jnp.broadcast_to(jnp.sum(dmix[rs, cs], axis=1, keepdims=True), (CHUNK, B_GROUP_DIM))
                cols.append(_dot(wst_ref[g], dmb[rs, cs]))
            rows.append(jnp.concatenate(cols, axis=1))
        dvln = jnp.concatenate(rows, axis=0)
        glg_ref[...] += jnp.sum(dvln * xhat, axis=0, keepdims=True)
        glb_ref[...] += jnp.sum(dvln, axis=0, keepdims=True)
        gy = dvln * lng
        put(O_VB, rstd * ((gy - jnp.mean(gy, axis=-1, keepdims=True)) - xhat * jnp.mean(gy * xhat, axis=-1, keepdims=True)))

        sg = _sig(z_m)
        sl = z_m * sg
        heads = _mem_attn(q_m, kv_ref)
        o_m = jnp.concatenate([o for _, o in heads], axis=1)
        put(O_ZM, (dy_m * o_m) * _dsilu(z_m, sg))
        do_m = dy_m * sl
        dqs = []
        for h, (p, o_h) in enumerate(heads):
            hs = slice(h * M_HEAD_DIM, (h + 1) * M_HEAD_DIM)
            vs = slice(M_WIDTH + h * M_HEAD_DIM, M_WIDTH + (h + 1) * M_HEAD_DIM)
            do_h = do_m[:, hs]
            dob = do_h.astype(BF16)
            dp = _dot_nt(dob, kv_ref[:, vs])
            dsc = (p * (dp - jnp.sum(do_h * o_h, axis=-1, keepdims=True))) * (M_HEAD_DIM ** -0.5)
            dsb = dsc.astype(BF16)
            dqs.append(_dot(dsb, kv_ref[:, hs]))
            dkv_ref[:, hs] += _dot_tn(dsb, q_m[:, hs].astype(BF16))
            dkv_ref[:, vs] += _dot_tn(p.astype(BF16), dob)
        put(O_QM, jnp.concatenate(dqs, axis=1))

    return pl.pallas_call(
        body,
        out_shape=(_sds((s, MID_W), BF16), _sds((A_WIDTH, s), BF16), _sds((A_HEADS, s), F32), _sds(ws.shape, F32),
                   _sds(ws.shape, F32), _sds((1, B_WIDTH), F32), _sds((1, B_WIDTH), F32), _sds(kv.shape, F32)),
        grid=(s // tm,),
        in_specs=[pl.BlockSpec((N_BRANCH, tm, A_WIDTH), lambda i: (0, i, 0))] + _pblocks(tm, O_ZA // PBLK, nmid) + [
            _rows(tm, A_WIDTH), _full(kv.shape), _full(ws.shape), _full(ws.shape), _full(bsb.shape),
            _full((1, B_WIDTH)), _full((1, B_WIDTH)), _full(head_sel.shape)],
        out_specs=(_rows(tm, MID_W), pl.BlockSpec((A_WIDTH, tm), lambda i: (0, i)), pl.BlockSpec((A_HEADS, tm), lambda i: (0, i)),
                   _full(ws.shape), _full(ws.shape), _full((1, B_WIDTH)), _full((1, B_WIDTH)), _full(kv.shape)),
        compiler_params=_cp("arbitrary"), name="branch_bwd")(dy, proj, proj, proj, proj, o_a, kv, ws, ws_t, bsb, ln_g, ln_b, head_sel)


def attn_bwd(q_t, do_t, kr, kr_t, vb, lse, delta):
    s = kr.shape[0]
    tq = min(s, 256)
    kc = min(s, 512)
    nkc = s // kc
    grp = A_HEADS // A_KV_HEADS

    def body(qt_ref, dot_ref, kr_ref, krt_ref, vb_ref, lse_ref, dl_ref, dqt_ref, dk_ref, dv_ref):
        @pl.when(pl.program_id(0) == 0)
        def _():
            dk_ref[...] = jnp.zeros_like(dk_ref)
            dv_ref[...] = jnp.zeros_like(dv_ref)

        dqs = []
        for h in range(A_HEADS):
            kv = h // grp
            hs = slice(A_HEAD_DIM * h, A_HEAD_DIM * (h + 1))
            qpad = _pad_head(qt_ref[hs, :], kv)
            dopad = _pad_head(dot_ref[hs, :], kv)
            lse_h = lse_ref[h:h + 1, :]
            dl_h = dl_ref[h:h + 1, :]

            def step(ci, dq, qpad=qpad, dopad=dopad, lse_h=lse_h, dl_h=dl_h):
                k0 = pl.multiple_of(ci * kc, kc)
                ks = pl.ds(k0, kc)
                p = jnp.exp(_dot(kr_ref[ks, :], qpad) - lse_h)
                dp = _dot(vb_ref[ks, :], dopad)
                dsb = (p * (dp - dl_h)).astype(BF16)
                dv_ref[ks, :] += _dot_nt(p.astype(BF16), dopad)
                dk_ref[ks, :] += _dot_nt(dsb, qpad)
                return dq + _dot(krt_ref[:, ks], dsb)

            dq = lax.fori_loop(0, nkc, step, jnp.zeros((A_KV_WIDTH, tq), F32))
            dqs.append(dq[A_HEAD_DIM * kv:A_HEAD_DIM * (kv + 1), :])
        dqt_ref[...] = jnp.concatenate(dqs, axis=0)

    colq = pl.BlockSpec((A_WIDTH, tq), lambda i: (0, i))
    colh = pl.BlockSpec((A_HEADS, tq), lambda i: (0, i))
    return pl.pallas_call(
        body, out_shape=(_sds((A_WIDTH, s), F32), _sds((s, A_KV_WIDTH), F32), _sds((s, A_KV_WIDTH), F32)), grid=(s // tq,),
        in_specs=[colq, colq, _full((s, A_KV_WIDTH)), _full((A_KV_WIDTH, s)), _full((s, A_KV_WIDTH)), colh, colh],
        out_specs=(colq, _full((s, A_KV_WIDTH)), _full((s, A_KV_WIDTH))),
        compiler_params=_cp("arbitrary"), name="attn_bwd")(q_t, do_t, kr, kr_t, vb, lse, delta)


def qk_prep_bwd(proj, dq_t, dkr, dvb, tabs, qg, kg, gq, gk, fold_q, fold_k):
    s = proj.shape[0]
    tm = min(s, 512)
    c, sa, sb = tabs

    def head_norm_bwd(x, dn, gain, gones, fold):
        ms = _dot_hi(x * x, gones) * (1.0 / A_HEAD_DIM)
        r = lax.rsqrt(ms + EPS)
        xh = x * r
        gg = _dot_hi(jnp.sum(dn * xh, axis=0, keepdims=True), fold)
        u = dn * gain
        mean_u = _dot_hi(u * xh, gones) * (1.0 / A_HEAD_DIM)
        return r * (u - xh * mean_u), gg

    def body(p_ref, dqt_ref, dk_ref, dv_ref, c_ref, sa_ref, sb_ref, qg_ref, kg_ref, gq_ref, gk_ref, fq_ref, fk_ref,
             dqkv_ref, gqg_ref, gkg_ref):
        @pl.when(pl.program_id(0) == 0)
        def _():
            gqg_ref[...] = jnp.zeros_like(gqg_ref)
            gkg_ref[...] = jnp.zeros_like(gkg_ref)

        cc, ssa, ssb = c_ref[...], sa_ref[...], sb_ref[...]
        dqr = dqt_ref[...].T * (A_HEAD_DIM ** -0.5)
        dqn = _rope_t(dqr, _tile4(cc), _tile4(ssa), _tile4(ssb))
        dxq, gq_ = head_norm_bwd(p_ref[:, O_QA:O_QA + A_WIDTH], dqn, qg_ref[...], gq_ref[...], fq_ref[...])
        dkn = _rope_t(dk_ref[...], cc, ssa, ssb)
        dxk, gk_ = head_norm_bwd(p_ref[:, O_KA:O_KA + A_KV_WIDTH], dkn, kg_ref[...], gk_ref[...], fk_ref[...])
        gqg_ref[...] += gq_
        gkg_ref[...] += gk_
        dqkv_ref[:, O_QA:O_QA + A_WIDTH] = dxq.astype(BF16)
        dqkv_ref[:, O_KA:O_KA + A_KV_WIDTH] = dxk.astype(BF16)
        dqkv_ref[:, O_VA:O_VA + A_KV_WIDTH] = dv_ref[...].astype(BF16)

    tab = _rows(tm, LANES)
    return pl.pallas_call(
        body, out_shape=(_sds((s, PBLK), BF16), _sds((1, LANES), F32), _sds((1, LANES), F32)), grid=(s // tm,),
        in_specs=[_rows(tm, PBLK), pl.BlockSpec((A_WIDTH, tm), lambda i: (0, i)), _rows(tm, A_KV_WIDTH), _rows(tm, A_KV_WIDTH),
                  tab, tab, tab, _full((1, A_WIDTH)), _full((1, A_KV_WIDTH)), _full((A_WIDTH, A_WIDTH)),
                  _full((A_KV_WIDTH, A_KV_WIDTH)), _full((A_WIDTH, LANES)), _full((A_KV_WIDTH, LANES))],
        out_specs=(_rows(tm, PBLK), _full((1, LANES)), _full((1, LANES))),
        compiler_params=_cp("arbitrary"), name="qk_prep_bwd")(proj, dq_t, dkr, dvb, c, sa, sb, qg, kg, gq, gk, fold_q, fold_k)


def _pick_dproj(b, d0, d1, d2, use):
    first_lg = 1 + MID_W // PBLK

    @pl.when(b == 0)
    def _():
        use(d0[...])

    @pl.when(jnp.logical_and(b >= 1, b < first_lg))
    def _():
        use(d1[...])

    @pl.when(b >= first_lg)
    def _():
        use(d2[...])


def win_grad(d0, d1, d2, h):
    s, d = h.shape
    tk = min(s, 1024)
    nk = s // tk

    def body(d0_ref, d1_ref, d2_ref, h_ref, o_ref):
        @pl.when(pl.program_id(1) == 0)
        def _():
            o_ref[...] = jnp.zeros_like(o_ref)

        def use(blk):
            o_ref[...] += _dot_tn(blk, h_ref[...])

        _pick_dproj(pl.program_id(0), d0_ref, d1_ref, d2_ref, use)

    def spec(first, count):
        return pl.BlockSpec((tk, PBLK), lambda j, k: (k, jnp.clip(j - first, 0, count - 1)))

    nm = MID_W // PBLK
    return pl.pallas_call(
        body, out_shape=_sds((IN_WIDTH, d), F32), grid=(N_PBLK, nk),
        in_specs=[spec(0, 1), spec(1, nm), spec(1 + nm, LG_W // PBLK), pl.BlockSpec((tk, d), lambda j, k: (k, 0))],
        out_specs=pl.BlockSpec((PBLK, d), lambda j, k: (j, 0)),
        compiler_params=_cp("parallel", "arbitrary"), name="win_grad")(d0, d1, d2, h)


def h_bwd(d0, d1, d2, w_t, x, dx_out, g):
    s, d = x.shape
    tm = min(s, 512)

    def body(d0_ref, d1_ref, d2_ref, w_ref, x_ref, dxo_ref, g_ref, dx_ref, gg_ref, acc_ref):
        i, k = pl.program_id(0), pl.program_id(1)

        @pl.when(jnp.logical_and(i == 0, k == 0))
        def _():
            gg_ref[...] = jnp.zeros_like(gg_ref)

        @pl.when(k == 0)
        def _():
            acc_ref[...] = jnp.zeros_like(acc_ref)

        def use(blk):
            acc_ref[...] += _dot(blk, w_ref[...])

        _pick_dproj(k, d0_ref, d1_ref, d2_ref, use)

        @pl.when(k == N_PBLK - 1)
        def _():
            xf = x_ref[...]
            r = lax.rsqrt(jnp.mean(xf * xf, axis=-1, keepdims=True) + EPS)
            xh = xf * r
            dh = acc_ref[...]
            gg_ref[...] += jnp.sum(dh * xh, axis=0, keepdims=True)
            u = dh * g_ref[...]
            dx_ref[...] = dxo_ref[...] + r * (u - xh * jnp.mean(u * xh, axis=-1, keepdims=True))

    def spec(first, count):
        return pl.BlockSpec((tm, PBLK), lambda i, k: (i, jnp.clip(k - first, 0, count - 1)))

    nm = MID_W // PBLK
    rowb = pl.BlockSpec((tm, d), lambda i, k: (i, 0))
    return pl.pallas_call(
        body, out_shape=(_sds((s, d), F32), _sds((1, d), F32)), grid=(s // tm, N_PBLK),
        in_specs=[spec(0, 1), spec(1, nm), spec(1 + nm, LG_W // PBLK), pl.BlockSpec((PBLK, d), lambda i, k: (k, 0)),
                  rowb, rowb, pl.BlockSpec((1, d), lambda i, k: (0, 0))],
        out_specs=(rowb, pl.BlockSpec((1, d), lambda i, k: (0, 0))),
        scratch_shapes=[pltpu.VMEM((tm, d), F32)],
        compiler_params=_cp("arbitrary", "arbitrary"), name="h_bwd")(d0, d1, d2, w_t, x, dx_out, g)


def memkv_bwd(mem, g, mem_n, w_kv, dkv):
    m, d = mem.shape

    def body(mem_ref, g_ref, mn_ref, w_ref, dkv_ref, gw_ref, gg_ref):
        dkb = dkv_ref[...].astype(BF16)
        gw_ref[...] = _dot_tn(mn_ref[...], dkb)
        dmn = _dot_nt(dkb, w_ref[...])
        mf = mem_ref[...]
        r = lax.rsqrt(jnp.mean(mf * mf, axis=-1, keepdims=True) + EPS)
        gg_ref[...] = jnp.sum(dmn * (mf * r), axis=0, keepdims=True)

    return pl.pallas_call(
        body, out_shape=(_sds(w_kv.shape, F32), _sds((1, d), F32)),
        compiler_params=_cp(), name="memkv_bwd")(mem, g, mem_n, w_kv, dkv)


def _layer_consts(seq):
    i = jnp.arange(A_WIDTH)
    return dict(
        tabs=rope_tables(seq),
        gq=_group_ones(A_WIDTH, A_HEAD_DIM), gk=_group_ones(A_KV_WIDTH, A_HEAD_DIM),
        fold_q=(i[:, None] % A_HEAD_DIM == jnp.arange(LANES)[None, :]).astype(F32),
        fold_k=(i[:A_KV_WIDTH, None] % A_HEAD_DIM == jnp.arange(LANES)[None, :]).astype(F32),
        head_sel=(jnp.arange(A_HEADS)[:, None] == i[None, :] // A_HEAD_DIM).astype(F32),
    )


def local_fwd_bwd(x, mem, tgt, small, big):
    s, d = x.shape
    depth = small["norm_g"].shape[0]
    k = _layer_consts(s)
    row = lambda v: v.reshape(1, -1)
    saved = []
    for l in range(depth):
        ng = row(small["norm_g"][l])
        qg = row(jnp.tile(small["q_norm_g"][l], A_HEADS))
        kg = row(jnp.tile(small["k_norm_g"][l], A_KV_HEADS))
        ws = small["w_s"][l].astype(BF16)
        ws_t = jnp.swapaxes(small["w_s"][l], 1, 2).astype(BF16)
        bsb = jnp.broadcast_to(small["b_s"][l][:, :, None], (B_GROUPS, CHUNK, B_GROUP_DIM))
        lng, lnb = row(small["sg_ln_g"][l]), row(small["sg_ln_b"][l])
        mg = row(small["mem_norm_g"][l])
        h = rms_fwd(x, ng)
        proj = proj_fwd(h, big["win_t"][l])
        q_t, kr, kr_t, vb, v_t = qk_prep(proj, k["tabs"], qg, kg, k["gq"], k["gk"])
        o_a, lse = attn_fwd(q_t, kr, v_t)
        mem_n, kv = memkv_fwd(mem, mg, big["wkv"][l])
        x_next, y, up, merged = branch_fwd(x, proj, o_a, kv, ws, bsb, lng, lnb, big["wbr"][l], big["wout"][l])
        saved.append(dict(x=x, ng=ng, qg=qg, kg=kg, ws=ws, ws_t=ws_t, bsb=bsb, lng=lng, lnb=lnb, mg=mg, h=h, proj=proj,
                          q_t=q_t, kr=kr, kr_t=kr_t, vb=vb, o_a=o_a, lse=lse, mem_n=mem_n, kv=kv, y=y, up=up, merged=merged))
        x = x_next

    sq, dx, g_final = final_loss(x, row(small["final_g"]), tgt)
    grads = {n: [None] * depth for n in ("norm_g", "q_norm_g", "k_norm_g", "sg_ln_g", "sg_ln_b", "w_s", "b_s", "mem_norm_g",
                                         "win_t", "wkv", "wbr", "wout")}
    for l in reversed(range(depth)):
        sv = saved[l]
        dy, dlg, g_wout, g_wbr = merge_bwd(dx, sv["proj"], sv["y"], sv["up"], sv["merged"], big["wbr"][l], big["wout"][l])
        dmid, do_t, delta, g_ws, g_bs, g_lng, g_lnb, dkv = branch_bwd(
            dy, sv["proj"], sv["o_a"], sv["kv"], sv["ws"], sv["ws_t"], sv["bsb"], sv["lng"], sv["lnb"], k["head_sel"])
        dq_t, dkr, dvb = attn_bwd(sv["q_t"], do_t, sv["kr"], sv["kr_t"], sv["vb"], sv["lse"], delta)
        dqkv, g_qg, g_kg = qk_prep_bwd(sv["proj"], dq_t, dkr, dvb, k["tabs"], sv["qg"], sv["kg"], k["gq"], k["gk"],
                                       k["fold_q"], k["fold_k"])
        g_wkv, g_mg = memkv_bwd(mem, sv["mg"], sv["mem_n"], big["wkv"][l], dkv)
        grads["win_t"][l] = win_grad(dqkv, dmid, dlg, sv["h"])
        dx, g_ng = h_bwd(dqkv, dmid, dlg, big["win_t"][l], sv["x"], dx, sv["ng"])
        grads["norm_g"][l] = g_ng[0]
        grads["q_norm_g"][l] = g_qg[0, :A_HEAD_DIM]
        grads["k_norm_g"][l] = g_kg[0, :A_HEAD_DIM]
        grads["sg_ln_g"][l] = g_lng[0]
        grads["sg_ln_b"][l] = g_lnb[0]
        grads["w_s"][l] = g_ws
        grads["b_s"][l] = g_bs[:, :, 0]
        grads["mem_norm_g"][l] = g_mg[0]
        grads["wkv"][l] = g_wkv
        grads["wbr"][l] = g_wbr
        grads["wout"][l] = g_wout
    grads = {n: jnp.stack(v) for n, v in grads.items()}
    grads["final_g"] = g_final[0]
    return sq[0, 0], dx, grads


def _row_block(rows, width, cap_bytes=2 * 2**20):
    best = None
    for br in range(8, rows + 1, 8):
        if rows % br == 0 and br * width * 4 <= cap_bytes:
            best = br
    return best if best is not None else rows


def adamw(w, g, m, v):
    r, c = w.shape
    br = _row_block(r, c)

    def body(w_ref, g_ref, m_ref, v_ref, d_ref, nm_ref, nv_ref):
        gg = g_ref[...]
        mm = ADAM_B1 * m_ref[...] + (1.0 - ADAM_B1) * gg
        vv = ADAM_B2 * v_ref[...] + (1.0 - ADAM_B2) * (gg * gg)
        m_hat = mm / (1.0 - ADAM_B1 ** ADAM_STEP)
        v_hat = vv / (1.0 - ADAM_B2 ** ADAM_STEP)
        d_ref[...] = -ADAM_LR * (m_hat / (jnp.sqrt(v_hat) + ADAM_EPS) + ADAM_WD * w_ref[...])
        nm_ref[...] = mm
        nv_ref[...] = vv

    blk = _rows(br, c)
    return pl.pallas_call(
        body, out_shape=(_sds((r, c), F32),) * 3, grid=(r // br,), in_specs=[blk] * 4, out_specs=(blk,) * 3,
        compiler_params=_cp("parallel"), name="adamw")(w, g, m, v)


def pair_sum(core, g, t1):
    depth, nsh, r, c = g.shape
    r2 = r // 2

    def body(core_ref, g_ref, t_ref, pf_ref, pb_ref):
        sm = g_ref[...] + t_ref[...]
        pf_ref[...] = sm
        pb_ref[...] = sm.astype(BF16)

    blk = pl.BlockSpec((None, None, r2, c), lambda l, s, core_ref: (l, s, 0, 0))
    return pl.pallas_call(
        body, out_shape=(_sds((depth, nsh, r2, c), F32), _sds((depth, nsh, r2, c), BF16)),
        grid_spec=pltpu.PrefetchScalarGridSpec(
            num_scalar_prefetch=1, grid=(depth, nsh),
            in_specs=[pl.BlockSpec((None, None, r2, c), lambda l, s, core_ref: (l, s, core_ref[0], 0)), blk],
            out_specs=(blk, blk)),
        compiler_params=_cp("parallel", "parallel"), name="pair_sum")(core, g, t1)


def chip_sum(own, t2):
    depth, r2, c = own.shape

    def body(o_ref, t_ref, f_ref):
        f_ref[...] = ((o_ref[...] + t_ref[0].astype(F32)) + t_ref[1].astype(F32)) + t_ref[2].astype(F32)

    blk = pl.BlockSpec((None, r2, c), lambda l: (l, 0, 0))
    return pl.pallas_call(
        body, out_shape=_sds((depth, r2, c), F32), grid=(depth,),
        in_specs=[blk, pl.BlockSpec((N_CHIPS - 1, None, r2, c), lambda l: (0, l, 0, 0))], out_specs=blk,
        compiler_params=_cp("parallel"), name="chip_sum")(own, t2)


_ANY = pl.BlockSpec(memory_space=pl.ANY)


def _place():
    x, y, c = lax.axis_index("x"), lax.axis_index("y"), lax.axis_index("c")
    chips = [(1 - x, y), (x, 1 - y), (1 - x, 1 - y)]
    return x, y, c, chips


def allgather_weights(shards):
    n = len(shards)
    depth = shards[0].shape[0]
    half = depth // 2

    def body(*refs):
        ins, outs = refs[:n], refs[n:2 * n]
        send, recv, loc = refs[2 * n:]
        x, y, c, chips = _place()
        me = 2 * x + y
        sib = (x, y, 1 - c)

        def part(a, chip, hl):
            return outs[a].at[pl.ds(hl * half, half), chip]

        def remote(a, k, src, dst, dev):
            return pltpu.make_async_remote_copy(src, dst, send.at[a, k], recv.at[a, k], device_id=dev, device_id_type=MESH)

        locals_ = [pltpu.make_async_copy(ins[a], outs[a].at[:, me], loc.at[a]) for a in range(n)]
        for cp in locals_:
            cp.start()
        first = []
        for a in range(n):
            for k, (cx, cy) in enumerate(chips):
                cp = remote(a, k, ins[a].at[pl.ds(c * half, half)], part(a, me, c), (cx, cy, c))
                cp.start()
                first.append(cp)
        passed = []
        for k, (cx, cy) in enumerate(chips):
            for a in range(n):
                got = part(a, 2 * cx + cy, c)
                remote(a, k, got, got, (cx, cy, c)).wait_recv()
                cp = remote(a, 3 + k, got, got, sib)
                cp.start()
                passed.append(cp)
        for k, (cx, cy) in enumerate(chips):
            for a in range(n):
                got = part(a, 2 * cx + cy, 1 - c)
                remote(a, 3 + k, got, got, sib).wait_recv()
        for cp in first + passed:
            cp.wait_send()
        for cp in locals_:
            cp.wait()

    return pl.pallas_call(
        body, out_shape=tuple(_sds((depth, N_CHIPS) + a.shape[1:], a.dtype) for a in shards),
        in_specs=[_ANY] * n, out_specs=(_ANY,) * n,
        scratch_shapes=[pltpu.SemaphoreType.DMA((n, 6)), pltpu.SemaphoreType.DMA((n, 6)), pltpu.SemaphoreType.DMA((n,))],
        name="allgather_weights")(*shards)


def exchange_halves(gs):
    n = len(gs)

    def body(*refs):
        g_refs, t_refs = refs[:n], refs[n:2 * n]
        send, recv = refs[2 * n:]
        x, y, c, _ = _place()
        sib = (x, y, 1 - c)
        for a in range(n):
            depth, _, r, _ = gs[a].shape
            r2 = r // 2
            for l in range(depth):
                pltpu.make_async_remote_copy(g_refs[a].at[l, :, pl.ds((1 - c) * r2, r2)], t_refs[a].at[l],
                                             send.at[a], recv.at[a], device_id=sib, device_id_type=MESH).start()
        for a in range(n):
            pltpu.make_async_remote_copy(t_refs[a], t_refs[a], send.at[a], recv.at[a], device_id=sib, device_id_type=MESH).wait()

    return pl.pallas_call(
        body, out_shape=tuple(_sds(g.shape[:2] + (g.shape[2] // 2, g.shape[3]), F32) for g in gs),
        in_specs=[_ANY] * n, out_specs=(_ANY,) * n,
        scratch_shapes=[pltpu.SemaphoreType.DMA((n,)), pltpu.SemaphoreType.DMA((n,))],
        name="exchange_halves")(*gs)


def send_partials(pbs, pfs):
    n = len(pbs)

    def body(*refs):
        pb, pf = refs[:n], refs[n:2 * n]
        t2, own = refs[2 * n:3 * n], refs[3 * n:4 * n]
        send, recv, loc = refs[4 * n:]
        x, y, c, chips = _place()
        locals_ = [pltpu.make_async_copy(pf[a].at[:, 2 * x + y], own[a], loc.at[a]) for a in range(n)]
        for cp in locals_:
            cp.start()
        copies = []
        for a in range(n):
            for k, (cx, cy) in enumerate(chips):
                cp = pltpu.make_async_remote_copy(pb[a].at[:, 2 * cx + cy], t2[a].at[k], send.at[a, k], recv.at[a, k],
                                                  device_id=(cx, cy, c), device_id_type=MESH)
                cp.start()
                copies.append(cp)
        for cp in copies:
            cp.wait()
        for cp in locals_:
            cp.wait()

    k3 = N_CHIPS - 1
    return pl.pallas_call(
        body,
        out_shape=tuple(_sds((k3, p.shape[0]) + p.shape[2:], BF16) for p in pbs) + tuple(_sds((p.shape[0],) + p.shape[2:], F32) for p in pfs),
        in_specs=[_ANY] * (2 * n), out_specs=(_ANY,) * (2 * n),
        scratch_shapes=[pltpu.SemaphoreType.DMA((n, k3)), pltpu.SemaphoreType.DMA((n, k3)), pltpu.SemaphoreType.DMA((n,))],
        name="send_partials")(*pbs, *pfs)


def share_final(fs):
    n = len(fs)

    def body(*refs):
        f, out = refs[:n], refs[n:2 * n]
        send, recv, loc = refs[2 * n:]
        x, y, c, _ = _place()
        sib = (x, y, 1 - c)
        cps = []
        for a in range(n):
            r2 = fs[a].shape[1]
            mine = out[a].at[:, pl.ds(c * r2, r2)]
            lc = pltpu.make_async_copy(f[a], mine, loc.at[a])
            rc = pltpu.make_async_remote_copy(f[a], mine, send.at[a], recv.at[a], device_id=sib, device_id_type=MESH)
            lc.start()
            rc.start()
            cps.append((lc, rc))
        for a, (lc, rc) in enumerate(cps):
            r2 = fs[a].shape[1]
            theirs = out[a].at[:, pl.ds((1 - c) * r2, r2)]
            rc.wait_send()
            pltpu.make_async_remote_copy(f[a], theirs, send.at[a], recv.at[a], device_id=sib, device_id_type=MESH).wait_recv()
            lc.wait()

    return pl.pallas_call(
        body, out_shape=tuple(_sds((f.shape[0], 2 * f.shape[1], f.shape[2]), F32) for f in fs),
        in_specs=[_ANY] * n, out_specs=(_ANY,) * n,
        scratch_shapes=[pltpu.SemaphoreType.DMA((n,)), pltpu.SemaphoreType.DMA((n,)), pltpu.SemaphoreType.DMA((n,))],
        name="share_final")(*fs)


def allreduce_small(v):
    r, w = v.shape
    ndev = 2 * N_CHIPS

    def body(v_ref, sum_ref, all_ref, send, recv, loc):
        x, y, c, chips = _place()
        me, sib = (x, y, c), (x, y, 1 - c)

        def slab(px, py, pc):
            return all_ref.at[4 * px + 2 * py + pc]

        def copy(k, block, to, src=None):
            return pltpu.make_async_remote_copy(slab(*block) if src is None else src, slab(*block), send.at[k], recv.at[k],
                                                device_id=to, device_id_type=MESH)

        mine = pltpu.make_async_copy(v_ref, slab(*me), loc)
        mine.start()
        first = [copy(0, me, sib, src=v_ref)] + [copy(1 + j, me, (*chip, c), src=v_ref) for j, chip in enumerate(chips)]
        for cp in first:
            cp.start()
        passed = [copy(4 + j, (*chip, c), sib) for j, chip in enumerate(chips)]
        for j, chip in enumerate(chips):
            copy(1 + j, (*chip, c), me).wait_recv()
            passed[j].start()
        copy(0, sib, me).wait_recv()
        for j, chip in enumerate(chips):
            copy(4 + j, (*chip, 1 - c), me).wait_recv()
        for cp in first + passed:
            cp.wait_send()
        mine.wait()
        acc = all_ref[0]
        for i in range(1, ndev):
            acc = acc + all_ref[i]
        sum_ref[...] = acc

    vm = pl.BlockSpec(memory_space=pltpu.VMEM)
    return pl.pallas_call(
        body, out_shape=_sds((r, w), F32), in_specs=[vm], out_specs=vm,
        scratch_shapes=[pltpu.VMEM((ndev, r, w), F32), pltpu.SemaphoreType.DMA((7,)), pltpu.SemaphoreType.DMA((7,)),
                        pltpu.SemaphoreType.DMA],
        compiler_params=pltpu.CompilerParams(vmem_limit_bytes=VMEM_LIMIT), name="allreduce_small")(v)


_SMALL = ("norm_g", "q_norm_g", "k_norm_g", "sg_ln_g", "sg_ln_b", "w_s", "b_s", "mem_norm_g", "final_g")
_WEIGHTS = ("norm_g", "w_in", "q_norm_g", "k_norm_g", "sg_ln_g", "sg_ln_b", "w_s", "b_s", "mem_norm_g", "w_mem_kv", "w_br",
            "w_out", "final_g")


def _pack(d):
    flat = jnp.concatenate([d[n].reshape(-1) for n in _SMALL])
    rows = -(-flat.shape[0] // (8 * LANES)) * 8
    return jnp.pad(flat, (0, rows * LANES - flat.shape[0])).reshape(rows, LANES)


def _unpack(p, like):
    flat, out, o = p.reshape(-1), {}, 0
    for n in _SMALL:
        out[n] = flat[o:o + like[n].size].reshape(like[n].shape)
        o += like[n].size
    return out


def kernel(x, mem, norm_g, w_in, q_norm_g, k_norm_g, sg_ln_g, sg_ln_b, w_s, b_s, mem_norm_g, w_mem_kv, w_br, w_out, final_g, loss_target, m_norm_g, m_w_in, m_q_norm_g, m_k_norm_g, m_sg_ln_g, m_sg_ln_b, m_w_s, m_b_s, m_mem_norm_g, m_w_mem_kv, m_w_br, m_w_out, m_final_g, v_norm_g, v_w_in, v_q_norm_g, v_k_norm_g, v_sg_ln_g, v_sg_ln_b, v_w_s, v_b_s, v_mem_norm_g, v_w_mem_kv, v_w_br, v_w_out, v_final_g):
    w = dict(norm_g=norm_g, w_in=w_in, q_norm_g=q_norm_g, k_norm_g=k_norm_g, sg_ln_g=sg_ln_g, sg_ln_b=sg_ln_b, w_s=w_s, b_s=b_s,
             mem_norm_g=mem_norm_g, w_mem_kv=w_mem_kv, w_br=w_br, w_out=w_out, final_g=final_g)
    m = dict(norm_g=m_norm_g, w_in=m_w_in, q_norm_g=m_q_norm_g, k_norm_g=m_k_norm_g, sg_ln_g=m_sg_ln_g, sg_ln_b=m_sg_ln_b,
             w_s=m_w_s, b_s=m_b_s, mem_norm_g=m_mem_norm_g, w_mem_kv=m_w_mem_kv, w_br=m_w_br, w_out=m_w_out, final_g=m_final_g)
    v = dict(norm_g=v_norm_g, w_in=v_w_in, q_norm_g=v_q_norm_g, k_norm_g=v_k_norm_g, sg_ln_g=v_sg_ln_g, sg_ln_b=v_sg_ln_b,
             w_s=v_w_s, b_s=v_b_s, mem_norm_g=v_mem_norm_g, w_mem_kv=v_w_mem_kv, w_br=v_w_br, w_out=v_w_out, final_g=v_final_g)
    depth, d = norm_g.shape
    nsh = N_CHIPS
    br_rows = N_BRANCH * A_WIDTH
    br_cols = d // nsh

    shards = [jnp.swapaxes(w_in, 1, 2).astype(BF16), w_mem_kv.astype(BF16), w_br.astype(BF16).reshape(depth, br_rows, br_cols),
              w_out.astype(BF16)]
    g_win, g_wkv, g_wbr, g_wout = allgather_weights(shards)
    big = dict(
        win_t=g_win.reshape(depth, IN_WIDTH, d),
        wkv=g_wkv.reshape(depth, d, 2 * M_WIDTH),
        wbr=g_wbr.reshape(depth, nsh, N_BRANCH, A_WIDTH, br_cols).transpose(0, 2, 3, 1, 4).reshape(depth, N_BRANCH, A_WIDTH, d),
        wout=g_wout.reshape(depth, d, d))
    small = {n: w[n] for n in _SMALL}

    sq, dx, grads = local_fwd_bwd(x[0], mem[0], loss_target[0], small, big)
    loss = (0.5 / d) * lax.psum(sq, ("x", "y", "c"))

    gs = [grads["win_t"].reshape(depth, nsh, IN_WIDTH // nsh, d),
          grads["wkv"].reshape(depth, nsh, d // nsh, 2 * M_WIDTH),
          grads["wbr"].reshape(depth, N_BRANCH, A_WIDTH, nsh, br_cols).transpose(0, 3, 1, 2, 4).reshape(depth, nsh, br_rows, br_cols),
          grads["wout"].reshape(depth, nsh, d // nsh, d)]
    core = lax.axis_index("c").astype(jnp.int32).reshape(1)
    t1 = exchange_halves(gs)
    pairs = [pair_sum(core, g, t) for g, t in zip(gs, t1)]
    got = send_partials([p[1] for p in pairs], [p[0] for p in pairs])
    t2, own = got[:len(gs)], got[len(gs):]
    finals = share_final([chip_sum(o, t) for o, t in zip(own, t2)])
    big_grads = dict(w_in=jnp.swapaxes(finals[0], 1, 2), w_mem_kv=finals[1],
                     w_br=finals[2].reshape(depth, N_BRANCH, A_WIDTH, br_cols), w_out=finals[3])

    small_grads = _unpack(allreduce_small(_pack(grads)), small)

    out_g, out_d, out_m, out_v = {}, {}, {}, {}
    sd, sm, sv = adamw(_pack(small), _pack(small_grads), _pack({n: m[n] for n in _SMALL}), _pack({n: v[n] for n in _SMALL}))
    sd, sm, sv = _unpack(sd, small), _unpack(sm, small), _unpack(sv, small)
    for n in _SMALL:
        out_g[n], out_d[n], out_m[n], out_v[n] = small_grads[n], sd[n], sm[n], sv[n]
    for n, g in big_grads.items():
        two_d = (-1, w[n].shape[-1])
        dd, mm, vv = adamw(w[n].reshape(two_d), g.reshape(two_d), m[n].reshape(two_d), v[n].reshape(two_d))
        out_g[n], out_d[n], out_m[n], out_v[n] = g, dd.reshape(w[n].shape), mm.reshape(w[n].shape), vv.reshape(w[n].shape)
    return (loss, dx[None], *[out_g[n] for n in _WEIGHTS], *[out_d[n] for n in _WEIGHTS], *[out_m[n] for n in _WEIGHTS],
            *[out_v[n] for n in _WEIGHTS])
```

```python
import functools

import jax
import jax.numpy as jnp
from jax import lax
from jax.experimental import pallas as pl
from jax.experimental.pallas import tpu as pltpu

F32 = jnp.float32
BF16 = jnp.bfloat16

D_MODEL = 1024
GRID_W = 64
CHUNK = 128
ROPE_THETA = 10000.0
EPS = 1e-6
A_HEADS, A_KV_HEADS, A_HEAD_DIM = 8, 2, 64
A_WIDTH, A_KV_WIDTH = 512, 128
B_GROUPS, B_GROUP_DIM, B_WIDTH = 4, 128, 512
M_HEADS, M_HEAD_DIM, M_WIDTH = 4, 128, 512
N_BRANCH = 3
IN_WIDTH = 6912
O_QA, O_KA, O_VA, O_ZA, O_UB, O_VB, O_ZB, O_QM, O_ZM, O_LG = 0, 512, 640, 768, 1280, 1792, 2304, 2816, 3328, 3840
PBLK = 768
N_PBLK = IN_WIDTH // PBLK
MID_W = 3072
LG_W = 3072

LN2 = 0.6931471805599453
Q_SCALE = A_HEAD_DIM ** -0.5 / LN2

ADAM_LR, ADAM_B1, ADAM_B2, ADAM_EPS, ADAM_WD, ADAM_STEP = 0.001, 0.9, 0.999, 1e-08, 0.01, 10

V7X_VMEM_BYTES = 64 * 2**20
VMEM_LIMIT = V7X_VMEM_BYTES - 8 * 2**20
LANES = 128
MESH = pl.DeviceIdType.MESH
N_CHIPS = 4


def _cp(*sem):
    return pltpu.CompilerParams(dimension_semantics=sem if sem else None, vmem_limit_bytes=VMEM_LIMIT)


def _dot(a, b):
    return jnp.dot(a, b, preferred_element_type=F32)


def _dot_nt(a, b):
    return lax.dot_general(a, b, (((1,), (1,)), ((), ())), preferred_element_type=F32)


def _dot_tn(a, b):
    return lax.dot_general(a, b, (((0,), (0,)), ((), ())), preferred_element_type=F32)


def _dot_hi(a, b):
    return jnp.dot(a, b, preferred_element_type=F32, precision=lax.Precision.HIGHEST)


def _dot_nt_hi(a, b):
    return lax.dot_general(a, b, (((1,), (1,)), ((), ())), preferred_element_type=F32, precision=lax.Precision.HIGHEST)


def _sig(z):
    return 1.0 / (1.0 + jnp.exp(-z))


def _full(shape):
    nd = len(shape)
    return pl.BlockSpec(shape, lambda *_: (0,) * nd)


def _rows(tm, width):
    return pl.BlockSpec((tm, width), lambda i: (i, 0))


def _sds(shape, dtype):
    return jax.ShapeDtypeStruct(shape, dtype)


def rms_fwd(x, g):
    s, d = x.shape
    tm = min(s, 512)

    def body(x_ref, g_ref, h_ref):
        xf = x_ref[...]
        r = lax.rsqrt(jnp.mean(xf * xf, axis=-1, keepdims=True) + EPS)
        h_ref[...] = ((xf * r) * g_ref[...]).astype(BF16)

    return pl.pallas_call(
        body, out_shape=_sds((s, d), BF16), grid=(s // tm,),
        in_specs=[_rows(tm, d), _full((1, d))], out_specs=_rows(tm, d),
        compiler_params=_cp("parallel"), name="rms_fwd")(x, g)


def proj_fwd(h, w_t):
    s, d = h.shape
    n = w_t.shape[0]
    tm = min(s, 1024)
    tn = 1152

    def body(h_ref, w_ref, o_ref):
        o_ref[...] = _dot_nt(h_ref[...], w_ref[...])

    return pl.pallas_call(
        body, out_shape=_sds((s, n), F32), grid=(n // tn, s // tm),
        in_specs=[pl.BlockSpec((tm, d), lambda j, i: (i, 0)), pl.BlockSpec((tn, d), lambda j, i: (j, 0))],
        out_specs=pl.BlockSpec((tm, tn), lambda j, i: (i, j)),
        compiler_params=_cp("parallel", "parallel"), name="proj_fwd")(h, w_t)


def rope_tables(seq):
    rows = seq // GRID_W
    row = jnp.repeat(jnp.arange(rows, dtype=F32), GRID_W)
    col = jnp.tile(jnp.arange(GRID_W, dtype=F32), rows)
    n_freq = A_HEAD_DIM // 4
    inv = ROPE_THETA ** (-jnp.arange(n_freq, dtype=F32) / n_freq)
    ang = jnp.stack([row[:, None] * inv, col[:, None] * inv], axis=1)
    cos, sin = jnp.cos(ang), jnp.sin(ang)
    zero = jnp.zeros_like(sin[:, 0])
    c64 = jnp.concatenate([cos[:, 0], cos[:, 0], cos[:, 1], cos[:, 1]], axis=1)
    sa64 = jnp.concatenate([zero, sin[:, 0], zero, sin[:, 1]], axis=1)
    sb64 = jnp.concatenate([-sin[:, 0], zero, -sin[:, 1], zero], axis=1)
    two = lambda t: jnp.concatenate([t, t], axis=1)
    return two(c64), two(sa64), two(sb64)


def _group_ones(width, group):
    i = jnp.arange(width)
    return (i[:, None] // group == i[None, :] // group).astype(F32)


def _rope(xn, c, sa, sb):
    w = xn.shape[1]
    return xn * c + pltpu.roll(xn, 16, 1) * sa + pltpu.roll(xn, w - 16, 1) * sb


def _rope_t(dy, c, sa, sb):
    w = dy.shape[1]
    return dy * c + pltpu.roll(dy * sa, w - 16, 1) + pltpu.roll(dy * sb, 16, 1)


def _tile4(t):
    return jnp.concatenate([t, t, t, t], axis=1)


def qk_prep(proj, tabs, qg, kg, gq, gk):
    s = proj.shape[0]
    tm = min(s, 512)
    c, sa, sb = tabs

    def body(p_ref, c_ref, sa_ref, sb_ref, qg_ref, kg_ref, gq_ref, gk_ref, qt_ref, kr_ref, krt_ref, vb_ref, v0_ref, v1_ref):
        xq = p_ref[:, O_QA:O_QA + A_WIDTH]
        xk = p_ref[:, O_KA:O_KA + A_KV_WIDTH]
        xv = p_ref[:, O_VA:O_VA + A_KV_WIDTH]
        cc, ssa, ssb = c_ref[...], sa_ref[...], sb_ref[...]
        msq = _dot_hi(xq * xq, gq_ref[...]) * (1.0 / A_HEAD_DIM)
        qn = (xq * lax.rsqrt(msq + EPS)) * qg_ref[...]
        qr = _rope(qn, _tile4(cc), _tile4(ssa), _tile4(ssb)) * Q_SCALE
        qt_ref[...] = qr.T.astype(BF16)
        msk = _dot_hi(xk * xk, gk_ref[...]) * (1.0 / A_HEAD_DIM)
        kn = (xk * lax.rsqrt(msk + EPS)) * kg_ref[...]
        kr = _rope(kn, cc, ssa, ssb)
        kr_ref[...] = kr.astype(BF16)
        krt_ref[...] = kr.T.astype(BF16)
        vb_ref[...] = xv.astype(BF16)
        vt = xv.T.astype(BF16)
        one = jnp.ones((A_HEAD_DIM, tm), BF16)
        v0_ref[...] = jnp.concatenate([vt[:A_HEAD_DIM], one], axis=0)
        v1_ref[...] = jnp.concatenate([one, vt[A_HEAD_DIM:]], axis=0)

    tab = _rows(tm, LANES)
    colb = lambda w: pl.BlockSpec((w, tm), lambda i: (0, i))
    return pl.pallas_call(
        body,
        out_shape=(_sds((A_WIDTH, s), BF16), _sds((s, A_KV_WIDTH), BF16), _sds((A_KV_WIDTH, s), BF16),
                   _sds((s, A_KV_WIDTH), BF16), _sds((A_KV_WIDTH, s), BF16), _sds((A_KV_WIDTH, s), BF16)),
        grid=(s // tm,),
        in_specs=[_rows(tm, PBLK), tab, tab, tab, _full((1, A_WIDTH)), _full((1, A_KV_WIDTH)),
                  _full((A_WIDTH, A_WIDTH)), _full((A_KV_WIDTH, A_KV_WIDTH))],
        out_specs=(colb(A_WIDTH), _rows(tm, A_KV_WIDTH), colb(A_KV_WIDTH), _rows(tm, A_KV_WIDTH), colb(A_KV_WIDTH),
                   colb(A_KV_WIDTH)),
        compiler_params=_cp("parallel"), name="qk_prep")(proj, c, sa, sb, qg, kg, gq, gk)


def _pad_head(q_h, kv):
    z = jnp.zeros_like(q_h)
    return jnp.concatenate([q_h, z], axis=0) if kv == 0 else jnp.concatenate([z, q_h], axis=0)


def attn_fwd(q_t, kr, vte0, vte1):
    s = kr.shape[0]
    tq = min(s, 256)
    kc = min(s, 512)
    nkc = s // kc
    grp = A_HEADS // A_KV_HEADS

    def body(qt_ref, kr_ref, v0_ref, v1_ref, o_ref, lse_ref, qp_ref, m_ref, acc_ref):
        for h in range(A_HEADS):
            qp_ref[h] = _pad_head(qt_ref[A_HEAD_DIM * h:A_HEAD_DIM * (h + 1), :], h // grp)
        m_ref[...] = jnp.full(m_ref.shape, -1e30, F32)
        acc_ref[...] = jnp.zeros_like(acc_ref)

        def step(ci, carry):
            ks = pl.ds(pl.multiple_of(ci * kc, kc), kc)
            kblk = kr_ref[ks, :]
            vts = (v0_ref[:, ks], v1_ref[:, ks])
            scs = [_dot(kblk, qp_ref[h]) for h in range(A_HEADS)]
            for h in range(A_HEADS):
                sc = scs[h]
                m_prev = m_ref[h:h + 1, :]
                m_new = jnp.maximum(m_prev, jnp.max(sc, axis=0, keepdims=True))
                p = jnp.exp2(sc - m_new)
                acc_ref[h] = acc_ref[h] * jnp.exp2(m_prev - m_new) + _dot(vts[h // grp], p.astype(BF16))
                m_ref[h:h + 1, :] = m_new
            return carry

        lax.fori_loop(0, nkc, step, 0)
        outs, lses = [], []
        for h in range(A_HEADS):
            kv = h // grp
            acc = acc_ref[h]
            l = acc[A_HEAD_DIM * (1 - kv):A_HEAD_DIM * (1 - kv) + 1, :]
            outs.append(acc[A_HEAD_DIM * kv:A_HEAD_DIM * (kv + 1), :] / l)
            lses.append(m_ref[h:h + 1, :] + jnp.log2(l))
        o_ref[...] = jnp.concatenate(outs, axis=0).T
        lse_ref[...] = jnp.concatenate(lses, axis=0)

    return pl.pallas_call(
        body, out_shape=(_sds((s, A_WIDTH), F32), _sds((A_HEADS, s), F32)), grid=(s // tq,),
        in_specs=[pl.BlockSpec((A_WIDTH, tq), lambda i: (0, i)), _full((s, A_KV_WIDTH)), _full((A_KV_WIDTH, s)),
                  _full((A_KV_WIDTH, s))],
        out_specs=(_rows(tq, A_WIDTH), pl.BlockSpec((A_HEADS, tq), lambda i: (0, i))),
        scratch_shapes=[pltpu.VMEM((A_HEADS, A_KV_WIDTH, tq), BF16), pltpu.VMEM((A_HEADS, tq), F32),
                        pltpu.VMEM((A_HEADS, A_KV_WIDTH, tq), F32)],
        compiler_params=_cp("parallel"), name="attn_fwd")(q_t, kr, vte0, vte1)


def memkv_fwd(mem, g, w_kv):
    m, d = mem.shape

    def body(mem_ref, g_ref, w_ref, mn_ref, kv_ref):
        mf = mem_ref[...]
        r = lax.rsqrt(jnp.mean(mf * mf, axis=-1, keepdims=True) + EPS)
        mn = ((mf * r) * g_ref[...]).astype(BF16)
        mn_ref[...] = mn
        kv_ref[...] = _dot(mn, w_ref[...]).astype(BF16)

    return pl.pallas_call(
        body, out_shape=(_sds((m, d), BF16), _sds((m, 2 * M_WIDTH), BF16)),
        compiler_params=_cp(), name="memkv_fwd")(mem, g, w_kv)


def _layer_norm_stats(v):
    mu = jnp.mean(v, axis=-1, keepdims=True)
    xc = v - mu
    rstd = lax.rsqrt(jnp.mean(xc * xc, axis=-1, keepdims=True) + EPS)
    return xc * rstd, rstd


def _spatial_mix(vlb, ws_ref, bsb_ref, tm):
    rows = []
    for ci in range(tm // CHUNK):
        cols = []
        for g in range(B_GROUPS):
            blk = vlb[ci * CHUNK:(ci + 1) * CHUNK, g * B_GROUP_DIM:(g + 1) * B_GROUP_DIM]
            cols.append(_dot(ws_ref[g], blk) + bsb_ref[g])
        rows.append(jnp.concatenate(cols, axis=1))
    return jnp.concatenate(rows, axis=0)


def _mem_attn(qm, kv_ref):
    out = []
    for h in range(M_HEADS):
        qh = qm[:, h * M_HEAD_DIM:(h + 1) * M_HEAD_DIM].astype(BF16)
        kh = kv_ref[:, h * M_HEAD_DIM:(h + 1) * M_HEAD_DIM]
        vh = kv_ref[:, M_WIDTH + h * M_HEAD_DIM:M_WIDTH + (h + 1) * M_HEAD_DIM]
        sc = _dot_nt(qh, kh) * (M_HEAD_DIM ** -0.5)
        e = jnp.exp(sc - jnp.max(sc, axis=-1, keepdims=True))
        p = e / jnp.sum(e, axis=-1, keepdims=True)
        out.append((p, _dot(p.astype(BF16), vh)))
    return out


def branch_fwd(x, proj, o_a, kv, ws, bsb, ln_g, ln_b, w_br, w_out):
    s, d = x.shape
    tm = min(s, 256)

    def body(x_ref, p_ref, oa_ref, kv_ref, ws_ref, bsb_ref, lg_ref, lb_ref, wbr_ref, wo_ref,
             xn_ref, y_ref, up_ref, mg_ref):
        seg = lambda o, w: p_ref[:, o:o + w]
        z_a, u_b, v_b, z_b = seg(O_ZA, A_WIDTH), seg(O_UB, B_WIDTH), seg(O_VB, B_WIDTH), seg(O_ZB, B_WIDTH)
        q_m, z_m = seg(O_QM, M_WIDTH), seg(O_ZM, M_WIDTH)
        xhat, _ = _layer_norm_stats(v_b)
        vln = xhat * lg_ref[...] + lb_ref[...]
        mixed = _spatial_mix(vln.astype(BF16), ws_ref, bsb_ref, tm)
        y_b = (u_b * mixed) * (z_b * _sig(z_b))
        o_m = jnp.concatenate([o for _, o in _mem_attn(q_m, kv_ref)], axis=1)
        y_a = oa_ref[...] * (z_a * _sig(z_a))
        y_m = o_m * (z_m * _sig(z_m))
        merged = None
        for n, yy in enumerate((y_a, y_b, y_m)):
            yb = yy.astype(BF16)
            y_ref[n] = yb
            up = _dot(yb, wbr_ref[n])
            up_ref[n] = up.astype(BF16)
            t = _sig(seg(O_LG + n * d, d)) * up
            merged = t if merged is None else merged + t
        mb = merged.astype(BF16)
        mg_ref[...] = mb
        xn_ref[...] = x_ref[...] + _dot(mb, wo_ref[...])

    return pl.pallas_call(
        body,
        out_shape=(_sds((s, d), F32), _sds((N_BRANCH, s, A_WIDTH), BF16), _sds((N_BRANCH, s, d), BF16), _sds((s, d), BF16)),
        grid=(s // tm,),
        in_specs=[_rows(tm, d), _rows(tm, IN_WIDTH), _rows(tm, A_WIDTH), _full(kv.shape), _full(ws.shape), _full(bsb.shape),
                  _full((1, B_WIDTH)), _full((1, B_WIDTH)), _full(w_br.shape), _full(w_out.shape)],
        out_specs=(_rows(tm, d), pl.BlockSpec((N_BRANCH, tm, A_WIDTH), lambda i: (0, i, 0)),
                   pl.BlockSpec((N_BRANCH, tm, d), lambda i: (0, i, 0)), _rows(tm, d)),
        compiler_params=_cp("parallel"), name="branch_fwd")(x, proj, o_a, kv, ws, bsb, ln_g, ln_b, w_br, w_out)


def final_loss(x, fg, tgt):
    s, d = x.shape
    tm = min(s, 512)

    def body(x_ref, g_ref, t_ref, ls_ref, dx_ref, gg_ref):
        @pl.when(pl.program_id(0) == 0)
        def _():
            ls_ref[...] = jnp.zeros_like(ls_ref)
            gg_ref[...] = jnp.zeros_like(gg_ref)

        xf = x_ref[...]
        g = g_ref[...]
        r = lax.rsqrt(jnp.mean(xf * xf, axis=-1, keepdims=True) + EPS)
        xh = xf * r
        e = xh * g - t_ref[...]
        sq = jnp.sum(jnp.sum(e * e, axis=0, keepdims=True), axis=1, keepdims=True)
        ls_ref[...] += jnp.broadcast_to(sq, ls_ref.shape)
        dy = e * (1.0 / d)
        gg_ref[...] += jnp.sum(dy * xh, axis=0, keepdims=True)
        gy = dy * g
        dx_ref[...] = r * (gy - xh * jnp.mean(gy * xh, axis=-1, keepdims=True))

    return pl.pallas_call(
        body, out_shape=(_sds((1, LANES), F32), _sds((s, d), F32), _sds((1, d), F32)), grid=(s // tm,),
        in_specs=[_rows(tm, d), _full((1, d)), _rows(tm, d)],
        out_specs=(_full((1, LANES)), _rows(tm, d), _full((1, d))),
        compiler_params=_cp("arbitrary"), name="final_loss")(x, fg, tgt)


def _pblocks(tm, first, count):
    return [pl.BlockSpec((tm, PBLK), functools.partial(lambda i, b: (i, b), b=first + k)) for k in range(count)]


def merge_bwd(dx, proj, y, up, merged, w_br, w_out):
    s, d = dx.shape
    tm = min(s, 256)
    nlg = LG_W // PBLK

    def body(dx_ref, l0, l1, l2, l3, y_ref, up_ref, mg_ref, wbr_ref, wo_ref, dy_ref, dlg_ref, gwo_ref, gwb_ref):
        @pl.when(pl.program_id(0) == 0)
        def _():
            gwo_ref[...] = jnp.zeros_like(gwo_ref)
            gwb_ref[...] = jnp.zeros_like(gwb_ref)

        dxb = dx_ref[...].astype(BF16)
        dmg = _dot_nt(dxb, wo_ref[...])
        gwo_ref[...] += _dot_tn(mg_ref[...], dxb)
        lg = jnp.concatenate([l0[...], l1[...], l2[...], l3[...]], axis=1)
        for n in range(N_BRANCH):
            g = _sig(lg[:, n * d:(n + 1) * d])
            dup = dmg * g
            dlg_ref[:, n * d:(n + 1) * d] = ((dup * up_ref[n].astype(F32)) * (1.0 - g)).astype(BF16)
            dupb = dup.astype(BF16)
            gwb_ref[n] += _dot_tn(y_ref[n], dupb)
            dy_ref[n] = _dot_nt(dupb, wbr_ref[n])

    return pl.pallas_call(
        body,
        out_shape=(_sds((N_BRANCH, s, A_WIDTH), F32), _sds((s, LG_W), BF16), _sds((d, d), F32), _sds(w_br.shape, F32)),
        grid=(s // tm,),
        in_specs=[_rows(tm, d)] + _pblocks(tm, O_LG // PBLK, nlg) + [
            pl.BlockSpec((N_BRANCH, tm, A_WIDTH), lambda i: (0, i, 0)), pl.BlockSpec((N_BRANCH, tm, d), lambda i: (0, i, 0)),
            _rows(tm, d), _full(w_br.shape), _full(w_out.shape)],
        out_specs=(pl.BlockSpec((N_BRANCH, tm, A_WIDTH), lambda i: (0, i, 0)), _rows(tm, LG_W), _full((d, d)), _full(w_br.shape)),
        compiler_params=_cp("arbitrary"), name="merge_bwd")(dx, proj, proj, proj, proj, y, up, merged, w_br, w_out)


def _dsilu(z, sg):
    return sg * (1.0 + z * (1.0 - sg))


def branch_bwd(dy, proj, o_a, kv, ws, ws_t, bsb, ln_g, ln_b, head_sel):
    s = proj.shape[0]
    tm = min(s, 256)
    nmid = MID_W // PBLK

    def body(dy_ref, m0, m1, m2, m3, oa_ref, kv_ref, ws_ref, wst_ref, bsb_ref, lg_ref, lb_ref, sel_ref,
             dmid_ref, dot_ref, dl_ref, gws_ref, gbs_ref, glg_ref, glb_ref, dkv_ref):
        @pl.when(pl.program_id(0) == 0)
        def _():
            for r in (gws_ref, gbs_ref, glg_ref, glb_ref, dkv_ref):
                r[...] = jnp.zeros_like(r)

        mid = jnp.concatenate([m0[...], m1[...], m2[...], m3[...]], axis=1)
        seg = lambda o, w: mid[:, o - O_ZA:o - O_ZA + w]
        z_a, u_b, v_b, z_b = seg(O_ZA, A_WIDTH), seg(O_UB, B_WIDTH), seg(O_VB, B_WIDTH), seg(O_ZB, B_WIDTH)
        q_m, z_m = seg(O_QM, M_WIDTH), seg(O_ZM, M_WIDTH)

        def put(o, v):
            dmid_ref[:, o - O_ZA:o - O_ZA + v.shape[1]] = v.astype(BF16)

        dy_a, dy_b, dy_m = dy_ref[0], dy_ref[1], dy_ref[2]

        o_a_ = oa_ref[...]
        sg = _sig(z_a)
        do_a = dy_a * (z_a * sg)
        put(O_ZA, (dy_a * o_a_) * _dsilu(z_a, sg))
        do_l = do_a * LN2
        dot_ref[...] = do_l.T.astype(BF16)
        dl_ref[...] = _dot_nt_hi(sel_ref[...], do_l * o_a_)

        xhat, rstd = _layer_norm_stats(v_b)
        lng = lg_ref[...]
        vln = xhat * lng + lb_ref[...]
        vlb = vln.astype(BF16)
        mixed = _spatial_mix(vlb, ws_ref, bsb_ref, tm)
        sg = _sig(z_b)
        sl = z_b * sg
        put(O_UB, (dy_b * mixed) * sl)
        put(O_ZB, ((dy_b * u_b) * mixed) * _dsilu(z_b, sg))
        dmix = (dy_b * u_b) * sl
        dmb = dmix.astype(BF16)
        rows = []
        for ci in range(tm // CHUNK):
            cols = []
            for g in range(B_GROUPS):
                rs, cs = slice(ci * CHUNK, (ci + 1) * CHUNK), slice(g * B_GROUP_DIM, (g + 1) * B_GROUP_DIM)
                gws_ref[g] += _dot_nt(dmb[rs, cs], vlb[rs, cs])
                gbs_ref[g] += jnp.broadcast_to(jnp.sum(dmix[rs, cs], axis=1, keepdims=True), (CHUNK, B_GROUP_DIM))
                cols.append(_dot(wst_ref[g], dmb[rs, cs]))
            rows.append(jnp.concatenate(cols, axis=1))
        dvln = jnp.concatenate(rows, axis=0)
        glg_ref[...] += jnp.sum(dvln * xhat, axis=0, keepdims=True)
        glb_ref[...] += jnp.sum(dvln, axis=0, keepdims=True)
        gy = dvln * lng
        put(O_VB, rstd * ((gy - jnp.mean(gy, axis=-1, keepdims=True)) - xhat * jnp.mean(gy * xhat, axis=-1, keepdims=True)))

        sg = _sig(z_m)
        sl = z_m * sg
        heads = _mem_attn(q_m, kv_ref)
        o_m = jnp.concatenate([o for _, o in heads], axis=1)
        put(O_ZM, (dy_m * o_m) * _dsilu(z_m, sg))
        do_m = dy_m * sl
        dqs = []
        for h, (p, o_h) in enumerate(heads):
            hs = slice(h * M_HEAD_DIM, (h + 1) * M_HEAD_DIM)
            vs = slice(M_WIDTH + h * M_HEAD_DIM, M_WIDTH + (h + 1) * M_HEAD_DIM)
            do_h = do_m[:, hs]
            dob = do_h.astype(BF16)
            dp = _dot_nt(dob, kv_ref[:, vs])
            dsc = (p * (dp - jnp.sum(do_h * o_h, axis=-1, keepdims=True))) * (M_HEAD_DIM ** -0.5)
            dsb = dsc.astype(BF16)
            dqs.append(_dot(dsb, kv_ref[:, hs]))
            dkv_ref[:, hs] += _dot_tn(dsb, q_m[:, hs].astype(BF16))
            dkv_ref[:, vs] += _dot_tn(p.astype(BF16), dob)
        put(O_QM, jnp.concatenate(dqs, axis=1))

    return pl.pallas_call(
        body,
        out_shape=(_sds((s, MID_W), BF16), _sds((A_WIDTH, s), BF16), _sds((A_HEADS, s), F32), _sds(ws.shape, F32),
                   _sds(ws.shape, F32), _sds((1, B_WIDTH), F32), _sds((1, B_WIDTH), F32), _sds(kv.shape, F32)),
        grid=(s // tm,),
        in_specs=[pl.BlockSpec((N_BRANCH, tm, A_WIDTH), lambda i: (0, i, 0))] + _pblocks(tm, O_ZA // PBLK, nmid) + [
            _rows(tm, A_WIDTH), _full(kv.shape), _full(ws.shape), _full(ws.shape), _full(bsb.shape),
            _full((1, B_WIDTH)), _full((1, B_WIDTH)), _full(head_sel.shape)],
        out_specs=(_rows(tm, MID_W), pl.BlockSpec((A_WIDTH, tm), lambda i: (0, i)), pl.BlockSpec((A_HEADS, tm), lambda i: (0, i)),
                   _full(ws.shape), _full(ws.shape), _full((1, B_WIDTH)), _full((1, B_WIDTH)), _full(kv.shape)),
        compiler_params=_cp("arbitrary"), name="branch_bwd")(dy, proj, proj, proj, proj, o_a, kv, ws, ws_t, bsb, ln_g, ln_b, head_sel)


def attn_bwd(q_t, do_t, kr, kr_t, vb, lse, delta):
    s = kr.shape[0]
    tq = min(s, 256)
    kc = min(s, 512)
    nkc = s // kc
    grp = A_HEADS // A_KV_HEADS

    def body(qt_ref, dot_ref, kr_ref, krt_ref, vb_ref, lse_ref, dl_ref, dqt_ref, dk_ref, dv_ref, qp_ref, dop_ref, dq_ref):
        @pl.when(pl.program_id(0) == 0)
        def _():
            dk_ref[...] = jnp.zeros_like(dk_ref)
            dv_ref[...] = jnp.zeros_like(dv_ref)

        for h in range(A_HEADS):
            hs = slice(A_HEAD_DIM * h, A_HEAD_DIM * (h + 1))
            qp_ref[h] = _pad_head(qt_ref[hs, :], h // grp)
            dop_ref[h] = _pad_head(dot_ref[hs, :], h // grp)
        dq_ref[...] = jnp.zeros_like(dq_ref)

        def step(ci, carry):
            ks = pl.ds(pl.multiple_of(ci * kc, kc), kc)
            kblk, vblk, ktb = kr_ref[ks, :], vb_ref[ks, :], krt_ref[:, ks]
            dv_acc = jnp.zeros((kc, A_KV_WIDTH), F32)
            dk_acc = jnp.zeros((kc, A_KV_WIDTH), F32)
            scs = [_dot(kblk, qp_ref[h]) for h in range(A_HEADS)]
            dps = [_dot(vblk, dop_ref[h]) for h in range(A_HEADS)]
            for h in range(A_HEADS):
                qpad, dopad = qp_ref[h], dop_ref[h]
                p = jnp.exp2(scs[h] - lse_ref[h:h + 1, :])
                dsb = (p * (dps[h] - dl_ref[h:h + 1, :])).astype(BF16)
                dv_acc = dv_acc + _dot_nt(p.astype(BF16), dopad)
                dk_acc = dk_acc + _dot_nt(dsb, qpad)
                dq_ref[h] += _dot(ktb, dsb)
            dv_ref[ks, :] += dv_acc
            dk_ref[ks, :] += dk_acc
            return carry

        lax.fori_loop(0, nkc, step, 0)
        dqt_ref[...] = jnp.concatenate(
            [dq_ref[h][A_HEAD_DIM * (h // grp):A_HEAD_DIM * (h // grp + 1), :] for h in range(A_HEADS)], axis=0)

    colq = pl.BlockSpec((A_WIDTH, tq), lambda i: (0, i))
    colh = pl.BlockSpec((A_HEADS, tq), lambda i: (0, i))
    return pl.pallas_call(
        body, out_shape=(_sds((A_WIDTH, s), F32), _sds((s, A_KV_WIDTH), F32), _sds((s, A_KV_WIDTH), F32)), grid=(s // tq,),
        in_specs=[colq, colq, _full((s, A_KV_WIDTH)), _full((A_KV_WIDTH, s)), _full((s, A_KV_WIDTH)), colh, colh],
        out_specs=(colq, _full((s, A_KV_WIDTH)), _full((s, A_KV_WIDTH))),
        scratch_shapes=[pltpu.VMEM((A_HEADS, A_KV_WIDTH, tq), BF16), pltpu.VMEM((A_HEADS, A_KV_WIDTH, tq), BF16),
                        pltpu.VMEM((A_HEADS, A_KV_WIDTH, tq), F32)],
        compiler_params=_cp("arbitrary"), name="attn_bwd")(q_t, do_t, kr, kr_t, vb, lse, delta)


def qk_prep_bwd(proj, dq_t, dkr, dvb, tabs, qg, kg, gq, gk, fold_q, fold_k):
    s = proj.shape[0]
    tm = min(s, 512)
    c, sa, sb = tabs

    def head_norm_bwd(x, dn, gain, gones, fold):
        ms = _dot_hi(x * x, gones) * (1.0 / A_HEAD_DIM)
        r = lax.rsqrt(ms + EPS)
        xh = x * r
        gg = _dot_hi(jnp.sum(dn * xh, axis=0, keepdims=True), fold)
        u = dn * gain
        mean_u = _dot_hi(u * xh, gones) * (1.0 / A_HEAD_DIM)
        return r * (u - xh * mean_u), gg

    def body(p_ref, dqt_ref, dk_ref, dv_ref, c_ref, sa_ref, sb_ref, qg_ref, kg_ref, gq_ref, gk_ref, fq_ref, fk_ref,
             dqkv_ref, gqg_ref, gkg_ref):
        @pl.when(pl.program_id(0) == 0)
        def _():
            gqg_ref[...] = jnp.zeros_like(gqg_ref)
            gkg_ref[...] = jnp.zeros_like(gkg_ref)

        cc, ssa, ssb = c_ref[...], sa_ref[...], sb_ref[...]
        dqr = dqt_ref[...].T * Q_SCALE
        dqn = _rope_t(dqr, _tile4(cc), _tile4(ssa), _tile4(ssb))
        dxq, gq_ = head_norm_bwd(p_ref[:, O_QA:O_QA + A_WIDTH], dqn, qg_ref[...], gq_ref[...], fq_ref[...])
        dkn = _rope_t(dk_ref[...], cc, ssa, ssb)
        dxk, gk_ = head_norm_bwd(p_ref[:, O_KA:O_KA + A_KV_WIDTH], dkn, kg_ref[...], gk_ref[...], fk_ref[...])
        gqg_ref[...] += gq_
        gkg_ref[...] += gk_
        dqkv_ref[:, O_QA:O_QA + A_WIDTH] = dxq.astype(BF16)
        dqkv_ref[:, O_KA:O_KA + A_KV_WIDTH] = dxk.astype(BF16)
        dqkv_ref[:, O_VA:O_VA + A_KV_WIDTH] = (dv_ref[...] * (1.0 / LN2)).astype(BF16)

    tab = _rows(tm, LANES)
    return pl.pallas_call(
        body, out_shape=(_sds((s, PBLK), BF16), _sds((1, LANES), F32), _sds((1, LANES), F32)), grid=(s // tm,),
        in_specs=[_rows(tm, PBLK), pl.BlockSpec((A_WIDTH, tm), lambda i: (0, i)), _rows(tm, A_KV_WIDTH), _rows(tm, A_KV_WIDTH),
                  tab, tab, tab, _full((1, A_WIDTH)), _full((1, A_KV_WIDTH)), _full((A_WIDTH, A_WIDTH)),
                  _full((A_KV_WIDTH, A_KV_WIDTH)), _full((A_WIDTH, LANES)), _full((A_KV_WIDTH, LANES))],
        out_specs=(_rows(tm, PBLK), _full((1, LANES)), _full((1, LANES))),
        compiler_params=_cp("arbitrary"), name="qk_prep_bwd")(proj, dq_t, dkr, dvb, c, sa, sb, qg, kg, gq, gk, fold_q, fold_k)


def _pick_dproj(b, d0, d1, d2, use):
    first_lg = 1 + MID_W // PBLK

    @pl.when(b == 0)
    def _():
        use(d0[...])

    @pl.when(jnp.logical_and(b >= 1, b < first_lg))
    def _():
        use(d1[...])

    @pl.when(b >= first_lg)
    def _():
        use(d2[...])


def win_grad(d0, d1, d2, h):
    s, d = h.shape
    tk = min(s, 1024)
    nk = s // tk

    def body(d0_ref, d1_ref, d2_ref, h_ref, o_ref):
        @pl.when(pl.program_id(1) == 0)
        def _():
            o_ref[...] = jnp.zeros_like(o_ref)

        def use(blk):
            o_ref[...] += _dot_tn(blk, h_ref[...])

        _pick_dproj(pl.program_id(0), d0_ref, d1_ref, d2_ref, use)

    def spec(first, count):
        return pl.BlockSpec((tk, PBLK), lambda j, k: (k, jnp.clip(j - first, 0, count - 1)))

    nm = MID_W // PBLK
    return pl.pallas_call(
        body, out_shape=_sds((IN_WIDTH, d), F32), grid=(N_PBLK, nk),
        in_specs=[spec(0, 1), spec(1, nm), spec(1 + nm, LG_W // PBLK), pl.BlockSpec((tk, d), lambda j, k: (k, 0))],
        out_specs=pl.BlockSpec((PBLK, d), lambda j, k: (j, 0)),
        compiler_params=_cp("parallel", "arbitrary"), name="win_grad")(d0, d1, d2, h)


def h_bwd(d0, d1, d2, w_t, x, dx_out, g):
    s, d = x.shape
    tm = min(s, 512)

    def body(d0_ref, d1_ref, d2_ref, w_ref, x_ref, dxo_ref, g_ref, dx_ref, gg_ref, acc_ref):
        i, k = pl.program_id(0), pl.program_id(1)

        @pl.when(jnp.logical_and(i == 0, k == 0))
        def _():
            gg_ref[...] = jnp.zeros_like(gg_ref)

        @pl.when(k == 0)
        def _():
            acc_ref[...] = jnp.zeros_like(acc_ref)

        def use(blk):
            acc_ref[...] += _dot(blk, w_ref[...])

        _pick_dproj(k, d0_ref, d1_ref, d2_ref, use)

        @pl.when(k == N_PBLK - 1)
        def _():
            xf = x_ref[...]
            r = lax.rsqrt(jnp.mean(xf * xf, axis=-1, keepdims=True) + EPS)
            xh = xf * r
            dh = acc_ref[...]
            gg_ref[...] += jnp.sum(dh * xh, axis=0, keepdims=True)
            u = dh * g_ref[...]
            dx_ref[...] = dxo_ref[...] + r * (u - xh * jnp.mean(u * xh, axis=-1, keepdims=True))

    def spec(first, count):
        return pl.BlockSpec((tm, PBLK), lambda i, k: (i, jnp.clip(k - first, 0, count - 1)))

    nm = MID_W // PBLK
    rowb = pl.BlockSpec((tm, d), lambda i, k: (i, 0))
    return pl.pallas_call(
        body, out_shape=(_sds((s, d), F32), _sds((1, d), F32)), grid=(s // tm, N_PBLK),
        in_specs=[spec(0, 1), spec(1, nm), spec(1 + nm, LG_W // PBLK), pl.BlockSpec((PBLK, d), lambda i, k: (k, 0)),
                  rowb, rowb, pl.BlockSpec((1, d), lambda i, k: (0, 0))],
        out_specs=(rowb, pl.BlockSpec((1, d), lambda i, k: (0, 0))),
        scratch_shapes=[pltpu.VMEM((tm, d), F32)],
        compiler_params=_cp("arbitrary", "arbitrary"), name="h_bwd")(d0, d1, d2, w_t, x, dx_out, g)


def memkv_bwd(mem, g, mem_n, w_kv, dkv):
    m, d = mem.shape

    def body(mem_ref, g_ref, mn_ref, w_ref, dkv_ref, gw_ref, gg_ref):
        dkb = dkv_ref[...].astype(BF16)
        gw_ref[...] = _dot_tn(mn_ref[...], dkb)
        dmn = _dot_nt(dkb, w_ref[...])
        mf = mem_ref[...]
        r = lax.rsqrt(jnp.mean(mf * mf, axis=-1, keepdims=True) + EPS)
        gg_ref[...] = jnp.sum(dmn * (mf * r), axis=0, keepdims=True)

    return pl.pallas_call(
        body, out_shape=(_sds(w_kv.shape, F32), _sds((1, d), F32)),
        compiler_params=_cp(), name="memkv_bwd")(mem, g, mem_n, w_kv, dkv)


def _layer_consts(seq):
    i = jnp.arange(A_WIDTH)
    return dict(
        tabs=rope_tables(seq),
        gq=_group_ones(A_WIDTH, A_HEAD_DIM), gk=_group_ones(A_KV_WIDTH, A_HEAD_DIM),
        fold_q=(i[:, None] % A_HEAD_DIM == jnp.arange(LANES)[None, :]).astype(F32),
        fold_k=(i[:A_KV_WIDTH, None] % A_HEAD_DIM == jnp.arange(LANES)[None, :]).astype(F32),
        head_sel=(jnp.arange(A_HEADS)[:, None] == i[None, :] // A_HEAD_DIM).astype(F32),
    )


def local_fwd_bwd(x, mem, tgt, small, big):
    s, d = x.shape
    depth = small["norm_g"].shape[0]
    k = _layer_consts(s)
    row = lambda v: v.reshape(1, -1)
    saved = []
    for l in range(depth):
        ng = row(small["norm_g"][l])
        qg = row(jnp.tile(small["q_norm_g"][l], A_HEADS))
        kg = row(jnp.tile(small["k_norm_g"][l], A_KV_HEADS))
        ws = small["w_s"][l].astype(BF16)
        ws_t = jnp.swapaxes(small["w_s"][l], 1, 2).astype(BF16)
        bsb = jnp.broadcast_to(small["b_s"][l][:, :, None], (B_GROUPS, CHUNK, B_GROUP_DIM))
        lng, lnb = row(small["sg_ln_g"][l]), row(small["sg_ln_b"][l])
        mg = row(small["mem_norm_g"][l])
        h = rms_fwd(x, ng)
        proj = proj_fwd(h, big["win_t"][l])
        q_t, kr, kr_t, vb, vte0, vte1 = qk_prep(proj, k["tabs"], qg, kg, k["gq"], k["gk"])
        o_a, lse = attn_fwd(q_t, kr, vte0, vte1)
        mem_n, kv = memkv_fwd(mem, mg, big["wkv"][l])
        x_next, y, up, merged = branch_fwd(x, proj, o_a, kv, ws, bsb, lng, lnb, big["wbr"][l], big["wout"][l])
        saved.append(dict(x=x, ng=ng, qg=qg, kg=kg, ws=ws, ws_t=ws_t, bsb=bsb, lng=lng, lnb=lnb, mg=mg, h=h, proj=proj,
                          q_t=q_t, kr=kr, kr_t=kr_t, vb=vb, o_a=o_a, lse=lse, mem_n=mem_n, kv=kv, y=y, up=up, merged=merged))
        x = x_next

    sq, dx, g_final = final_loss(x, row(small["final_g"]), tgt)
    grads = {n: [None] * depth for n in ("norm_g", "q_norm_g", "k_norm_g", "sg_ln_g", "sg_ln_b", "w_s", "b_s", "mem_norm_g",
                                         "win_t", "wkv", "wbr", "wout")}
    for l in reversed(range(depth)):
        sv = saved[l]
        dy, dlg, g_wout, g_wbr = merge_bwd(dx, sv["proj"], sv["y"], sv["up"], sv["merged"], big["wbr"][l], big["wout"][l])
        dmid, do_t, delta, g_ws, g_bs, g_lng, g_lnb, dkv = branch_bwd(
            dy, sv["proj"], sv["o_a"], sv["kv"], sv["ws"], sv["ws_t"], sv["bsb"], sv["lng"], sv["lnb"], k["head_sel"])
        dq_t, dkr, dvb = attn_bwd(sv["q_t"], do_t, sv["kr"], sv["kr_t"], sv["vb"], sv["lse"], delta)
        dqkv, g_qg, g_kg = qk_prep_bwd(sv["proj"], dq_t, dkr, dvb, k["tabs"], sv["qg"], sv["kg"], k["gq"], k["gk"],
                                       k["fold_q"], k["fold_k"])
        g_wkv, g_mg = memkv_bwd(mem, sv["mg"], sv["mem_n"], big["wkv"][l], dkv)
        grads["win_t"][l] = win_grad(dqkv, dmid, dlg, sv["h"])
        dx, g_ng = h_bwd(dqkv, dmid, dlg, big["win_t"][l], sv["x"], dx, sv["ng"])
        grads["norm_g"][l] = g_ng[0]
        grads["q_norm_g"][l] = g_qg[0, :A_HEAD_DIM]
        grads["k_norm_g"][l] = g_kg[0, :A_HEAD_DIM]
        grads["sg_ln_g"][l] = g_lng[0]
        grads["sg_ln_b"][l] = g_lnb[0]
        grads["w_s"][l] = g_ws
        grads["b_s"][l] = g_bs[:, :, 0]
        grads["mem_norm_g"][l] = g_mg[0]
        grads["wkv"][l] = g_wkv
        grads["wbr"][l] = g_wbr
        grads["wout"][l] = g_wout
    grads = {n: jnp.stack(v) for n, v in grads.items()}
    grads["final_g"] = g_final[0]
    return sq[0, 0], dx, grads


def _row_block(rows, width, cap_bytes=2 * 2**20):
    best = None
    for br in range(8, rows + 1, 8):
        if rows % br == 0 and br * width * 4 <= cap_bytes:
            best = br
    return best if best is not None else rows


def adamw(w, g, m, v):
    r, c = w.shape
    br = _row_block(r, c)

    def body(w_ref, g_ref, m_ref, v_ref, d_ref, nm_ref, nv_ref):
        gg = g_ref[...]
        mm = ADAM_B1 * m_ref[...] + (1.0 - ADAM_B1) * gg
        vv = ADAM_B2 * v_ref[...] + (1.0 - ADAM_B2) * (gg * gg)
        m_hat = mm / (1.0 - ADAM_B1 ** ADAM_STEP)
        v_hat = vv / (1.0 - ADAM_B2 ** ADAM_STEP)
        d_ref[...] = -ADAM_LR * (m_hat / (jnp.sqrt(v_hat) + ADAM_EPS) + ADAM_WD * w_ref[...])
        nm_ref[...] = mm
        nv_ref[...] = vv

    blk = _rows(br, c)
    return pl.pallas_call(
        body, out_shape=(_sds((r, c), F32),) * 3, grid=(r // br,), in_specs=[blk] * 4, out_specs=(blk,) * 3,
        compiler_params=_cp("parallel"), name="adamw")(w, g, m, v)


def pair_sum(core, g, t1):
    depth, nsh, r, c = g.shape
    r2 = r // 2

    def body(core_ref, g_ref, t_ref, pf_ref, pb_ref):
        sm = g_ref[...] + t_ref[...]
        pf_ref[...] = sm
        pb_ref[...] = sm.astype(BF16)

    blk = pl.BlockSpec((None, None, r2, c), lambda l, s, core_ref: (l, s, 0, 0))
    return pl.pallas_call(
        body, out_shape=(_sds((depth, nsh, r2, c), F32), _sds((depth, nsh, r2, c), BF16)),
        grid_spec=pltpu.PrefetchScalarGridSpec(
            num_scalar_prefetch=1, grid=(depth, nsh),
            in_specs=[pl.BlockSpec((None, None, r2, c), lambda l, s, core_ref: (l, s, core_ref[0], 0)), blk],
            out_specs=(blk, blk)),
        compiler_params=_cp("parallel", "parallel"), name="pair_sum")(core, g, t1)


def chip_sum(place, pf, t2):
    depth, _, r2, c = pf.shape

    def body(place_ref, o_ref, t_ref, f_ref):
        f_ref[...] = ((o_ref[...] + t_ref[0].astype(F32)) + t_ref[1].astype(F32)) + t_ref[2].astype(F32)

    return pl.pallas_call(
        body, out_shape=_sds((depth, 2 * r2, c), F32),
        grid_spec=pltpu.PrefetchScalarGridSpec(
            num_scalar_prefetch=1, grid=(depth,),
            in_specs=[pl.BlockSpec((None, None, r2, c), lambda l, place_ref: (l, place_ref[0], 0, 0)),
                      pl.BlockSpec((N_CHIPS - 1, None, r2, c), lambda l, place_ref: (0, l, 0, 0))],
            out_specs=pl.BlockSpec((None, r2, c), lambda l, place_ref: (l, place_ref[1], 0))),
        compiler_params=_cp("parallel"), name="chip_sum")(place, pf, t2)


_ANY = pl.BlockSpec(memory_space=pl.ANY)


def _place():
    x, y, c = lax.axis_index("x"), lax.axis_index("y"), lax.axis_index("c")
    chips = [(1 - x, y), (x, 1 - y), (1 - x, 1 - y)]
    return x, y, c, chips


def allgather_weights(shards):
    n = len(shards)
    depth = shards[0].shape[0]
    half = depth // 2

    def body(*refs):
        ins, outs = refs[:n], refs[n:2 * n]
        send, recv = refs[2 * n:]
        x, y, c, chips = _place()
        me = 2 * x + y
        sib = (x, y, 1 - c)

        def part(a, chip, hl):
            return outs[a].at[pl.ds(hl * half, half), chip]

        def remote(a, k, src, dst, dev):
            return pltpu.make_async_remote_copy(src, dst, send.at[a, k], recv.at[a, k], device_id=dev, device_id_type=MESH)

        first = []
        for a in range(n):
            for k, (cx, cy) in enumerate(chips):
                cp = remote(a, k, ins[a].at[pl.ds(c * half, half)], part(a, me, c), (cx, cy, c))
                cp.start()
                first.append(cp)
        passed = []
        for k, (cx, cy) in enumerate(chips):
            for a in range(n):
                got = part(a, 2 * cx + cy, c)
                remote(a, k, got, got, (cx, cy, c)).wait_recv()
                cp = remote(a, 3 + k, got, got, sib)
                cp.start()
                passed.append(cp)
        for k, (cx, cy) in enumerate(chips):
            for a in range(n):
                got = part(a, 2 * cx + cy, 1 - c)
                remote(a, 3 + k, got, got, sib).wait_recv()
        for cp in first + passed:
            cp.wait_send()

    return pl.pallas_call(
        body, out_shape=tuple(_sds((depth, N_CHIPS) + a.shape[1:], a.dtype) for a in shards),
        in_specs=[_ANY] * n, out_specs=(_ANY,) * n,
        scratch_shapes=[pltpu.SemaphoreType.DMA((n, 6)), pltpu.SemaphoreType.DMA((n, 6))],
        name="allgather_weights")(*shards)


def exchange_halves(gs):
    n = len(gs)

    def body(*refs):
        g_refs, t_refs = refs[:n], refs[n:2 * n]
        send, recv = refs[2 * n:]
        x, y, c, _ = _place()
        sib = (x, y, 1 - c)
        for a in range(n):
            depth, _, r, _ = gs[a].shape
            r2 = r // 2
            for l in range(depth):
                pltpu.make_async_remote_copy(g_refs[a].at[l, :, pl.ds((1 - c) * r2, r2)], t_refs[a].at[l],
                                             send.at[a], recv.at[a], device_id=sib, device_id_type=MESH).start()
        for a in range(n):
            pltpu.make_async_remote_copy(t_refs[a], t_refs[a], send.at[a], recv.at[a], device_id=sib, device_id_type=MESH).wait()

    return pl.pallas_call(
        body, out_shape=tuple(_sds(g.shape[:2] + (g.shape[2] // 2, g.shape[3]), F32) for g in gs),
        in_specs=[_ANY] * n, out_specs=(_ANY,) * n,
        scratch_shapes=[pltpu.SemaphoreType.DMA((n,)), pltpu.SemaphoreType.DMA((n,))],
        name="exchange_halves")(*gs)


def send_partials(pbs):
    n = len(pbs)

    def body(*refs):
        pb, t2 = refs[:n], refs[n:2 * n]
        send, recv = refs[2 * n:]
        _, _, c, chips = _place()
        copies = []
        for a in range(n):
            for k, (cx, cy) in enumerate(chips):
                cp = pltpu.make_async_remote_copy(pb[a].at[:, 2 * cx + cy], t2[a].at[k], send.at[a, k], recv.at[a, k],
                                                  device_id=(cx, cy, c), device_id_type=MESH)
                cp.start()
                copies.append(cp)
        for cp in copies:
            cp.wait()

    k3 = N_CHIPS - 1
    return pl.pallas_call(
        body, out_shape=tuple(_sds((k3, p.shape[0]) + p.shape[2:], BF16) for p in pbs),
        in_specs=[_ANY] * n, out_specs=(_ANY,) * n,
        scratch_shapes=[pltpu.SemaphoreType.DMA((n, k3)), pltpu.SemaphoreType.DMA((n, k3))],
        name="send_partials")(*pbs)


def share_final(fs):
    n = len(fs)

    def body(*refs):
        f, out = refs[:n], refs[n:2 * n]
        send, recv = refs[2 * n:]
        x, y, c, _ = _place()
        sib = (x, y, 1 - c)
        cps = []
        for a in range(n):
            r2 = fs[a].shape[1] // 2
            mine = out[a].at[:, pl.ds(c * r2, r2)]
            cp = pltpu.make_async_remote_copy(mine, mine, send.at[a], recv.at[a], device_id=sib, device_id_type=MESH)
            cp.start()
            cps.append(cp)
        for a, cp in enumerate(cps):
            r2 = fs[a].shape[1] // 2
            theirs = out[a].at[:, pl.ds((1 - c) * r2, r2)]
            cp.wait_send()
            pltpu.make_async_remote_copy(theirs, theirs, send.at[a], recv.at[a], device_id=sib, device_id_type=MESH).wait_recv()

    return pl.pallas_call(
        body, out_shape=tuple(_sds(f.shape, F32) for f in fs),
        in_specs=[_ANY] * n, out_specs=(_ANY,) * n, input_output_aliases={a: a for a in range(n)},
        scratch_shapes=[pltpu.SemaphoreType.DMA((n,)), pltpu.SemaphoreType.DMA((n,))],
        name="share_final")(*fs)


def allreduce_small(v):
    r, w = v.shape
    ndev = 2 * N_CHIPS

    def body(v_ref, sum_ref, all_ref, send, recv, loc):
        x, y, c, chips = _place()
        me, sib = (x, y, c), (x, y, 1 - c)

        def slab(px, py, pc):
            return all_ref.at[4 * px + 2 * py + pc]

        def copy(k, block, to, src=None):
            return pltpu.make_async_remote_copy(slab(*block) if src is None else src, slab(*block), send.at[k], recv.at[k],
                                                device_id=to, device_id_type=MESH)

        mine = pltpu.make_async_copy(v_ref, slab(*me), loc)
        mine.start()
        first = [copy(0, me, sib, src=v_ref)] + [copy(1 + j, me, (*chip, c), src=v_ref) for j, chip in enumerate(chips)]
        for cp in first:
            cp.start()
        passed = [copy(4 + j, (*chip, c), sib) for j, chip in enumerate(chips)]
        for j, chip in enumerate(chips):
            copy(1 + j, (*chip, c), me).wait_recv()
            passed[j].start()
        copy(0, sib, me).wait_recv()
        for j, chip in enumerate(chips):
            copy(4 + j, (*chip, 1 - c), me).wait_recv()
        for cp in first + passed:
            cp.wait_send()
        mine.wait()
        acc = all_ref[0]
        for i in range(1, ndev):
            acc = acc + all_ref[i]
        sum_ref[...] = acc

    vm = pl.BlockSpec(memory_space=pltpu.VMEM)
    return pl.pallas_call(
        body, out_shape=_sds((r, w), F32), in_specs=[vm], out_specs=vm,
        scratch_shapes=[pltpu.VMEM((ndev, r, w), F32), pltpu.SemaphoreType.DMA((7,)), pltpu.SemaphoreType.DMA((7,)),
                        pltpu.SemaphoreType.DMA],
        compiler_params=pltpu.CompilerParams(vmem_limit_bytes=VMEM_LIMIT), name="allreduce_small")(v)


_SMALL = ("norm_g", "q_norm_g", "k_norm_g", "sg_ln_g", "sg_ln_b", "w_s", "b_s", "mem_norm_g", "final_g")
_WEIGHTS = ("norm_g", "w_in", "q_norm_g", "k_norm_g", "sg_ln_g", "sg_ln_b", "w_s", "b_s", "mem_norm_g", "w_mem_kv", "w_br",
            "w_out", "final_g")


def _pack(d):
    flat = jnp.concatenate([d[n].reshape(-1) for n in _SMALL])
    rows = -(-flat.shape[0] // (8 * LANES)) * 8
    return jnp.pad(flat, (0, rows * LANES - flat.shape[0])).reshape(rows, LANES)


def _unpack(p, like):
    flat, out, o = p.reshape(-1), {}, 0
    for n in _SMALL:
        out[n] = flat[o:o + like[n].size].reshape(like[n].shape)
        o += like[n].size
    return out


def kernel(x, mem, norm_g, w_in, q_norm_g, k_norm_g, sg_ln_g, sg_ln_b, w_s, b_s, mem_norm_g, w_mem_kv, w_br, w_out, final_g, loss_target, m_norm_g, m_w_in, m_q_norm_g, m_k_norm_g, m_sg_ln_g, m_sg_ln_b, m_w_s, m_b_s, m_mem_norm_g, m_w_mem_kv, m_w_br, m_w_out, m_final_g, v_norm_g, v_w_in, v_q_norm_g, v_k_norm_g, v_sg_ln_g, v_sg_ln_b, v_w_s, v_b_s, v_mem_norm_g, v_w_mem_kv, v_w_br, v_w_out, v_final_g):
    w = dict(norm_g=norm_g, w_in=w_in, q_norm_g=q_norm_g, k_norm_g=k_norm_g, sg_ln_g=sg_ln_g, sg_ln_b=sg_ln_b, w_s=w_s, b_s=b_s,
             mem_norm_g=mem_norm_g, w_mem_kv=w_mem_kv, w_br=w_br, w_out=w_out, final_g=final_g)
    m = dict(norm_g=m_norm_g, w_in=m_w_in, q_norm_g=m_q_norm_g, k_norm_g=m_k_norm_g, sg_ln_g=m_sg_ln_g, sg_ln_b=m_sg_ln_b,
             w_s=m_w_s, b_s=m_b_s, mem_norm_g=m_mem_norm_g, w_mem_kv=m_w_mem_kv, w_br=m_w_br, w_out=m_w_out, final_g=m_final_g)
    v = dict(norm_g=v_norm_g, w_in=v_w_in, q_norm_g=v_q_norm_g, k_norm_g=v_k_norm_g, sg_ln_g=v_sg_ln_g, sg_ln_b=v_sg_ln_b,
             w_s=v_w_s, b_s=v_b_s, mem_norm_g=v_mem_norm_g, w_mem_kv=v_w_mem_kv, w_br=v_w_br, w_out=v_w_out, final_g=v_final_g)
    depth, d = norm_g.shape
    nsh = N_CHIPS
    br_rows = N_BRANCH * A_WIDTH
    br_cols = d // nsh

    shards = [jnp.swapaxes(w_in, 1, 2).astype(BF16), w_mem_kv.astype(BF16), w_br.astype(BF16).reshape(depth, br_rows, br_cols),
              w_out.astype(BF16)]
    chip = 2 * lax.axis_index("x") + lax.axis_index("y")
    core = lax.axis_index("c")
    g_win, g_wkv, g_wbr, g_wout = [lax.dynamic_update_slice(g, sh[:, None], (0, chip, 0, 0))
                                   for g, sh in zip(allgather_weights(shards), shards)]
    big = dict(
        win_t=g_win.reshape(depth, IN_WIDTH, d),
        wkv=g_wkv.reshape(depth, d, 2 * M_WIDTH),
        wbr=g_wbr.reshape(depth, nsh, N_BRANCH, A_WIDTH, br_cols).transpose(0, 2, 3, 1, 4).reshape(depth, N_BRANCH, A_WIDTH, d),
        wout=g_wout.reshape(depth, d, d))
    small = {n: w[n] for n in _SMALL}

    sq, dx, grads = local_fwd_bwd(x[0], mem[0], loss_target[0], small, big)
    loss = (0.5 / d) * lax.psum(sq, ("x", "y", "c"))

    gs = [grads["win_t"].reshape(depth, nsh, IN_WIDTH // nsh, d),
          grads["wkv"].reshape(depth, nsh, d // nsh, 2 * M_WIDTH),
          grads["wbr"].reshape(depth, N_BRANCH, A_WIDTH, nsh, br_cols).transpose(0, 3, 1, 2, 4).reshape(depth, nsh, br_rows, br_cols),
          grads["wout"].reshape(depth, nsh, d // nsh, d)]
    place = jnp.stack([chip, core]).astype(jnp.int32)
    t1 = exchange_halves(gs)
    pairs = [pair_sum(place[1:], g, t) for g, t in zip(gs, t1)]
    t2 = send_partials([p[1] for p in pairs])
    finals = share_final([chip_sum(place, p[0], t) for p, t in zip(pairs, t2)])
    big_grads = dict(w_in=jnp.swapaxes(finals[0], 1, 2), w_mem_kv=finals[1],
                     w_br=finals[2].reshape(depth, N_BRANCH, A_WIDTH, br_cols), w_out=finals[3])

    small_grads = _unpack(allreduce_small(_pack(grads)), small)

    out_g, out_d, out_m, out_v = {}, {}, {}, {}
    sd, sm, sv = adamw(_pack(small), _pack(small_grads), _pack({n: m[n] for n in _SMALL}), _pack({n: v[n] for n in _SMALL}))
    sd, sm, sv = _unpack(sd, small), _unpack(sm, small), _unpack(sv, small)
    for n in _SMALL:
        out_g[n], out_d[n], out_m[n], out_v[n] = small_grads[n], sd[n], sm[n], sv[n]
    for n, g in big_grads.items():
        two_d = (-1, w[n].shape[-1])
        dd, mm, vv = adamw(w[n].reshape(two_d), g.reshape(two_d), m[n].reshape(two_d), v[n].reshape(two_d))
        out_g[n], out_d[n], out_m[n], out_v[n] = g, dd.reshape(w[n].shape), mm.reshape(w[n].shape), vv.reshape(w[n].shape)
    return (loss, dx[None], *[out_g[n] for n in _WEIGHTS], *[out_d[n] for n in _WEIGHTS], *[out_m[n] for n in _WEIGHTS],
            *[out_v[n] for n in _WEIGHTS])
```

```python
import functools

import jax
import jax.numpy as jnp
from jax import lax
from jax.experimental import pallas as pl
from jax.experimental.pallas import tpu as pltpu

F32 = jnp.float32
BF16 = jnp.bfloat16

D_MODEL = 1024
GRID_W = 64
CHUNK = 128
ROPE_THETA = 10000.0
EPS = 1e-6
A_HEADS, A_KV_HEADS, A_HEAD_DIM = 8, 2, 64
A_WIDTH, A_KV_WIDTH = 512, 128
B_GROUPS, B_GROUP_DIM, B_WIDTH = 4, 128, 512
M_HEADS, M_HEAD_DIM, M_WIDTH = 4, 128, 512
N_BRANCH = 3
IN_WIDTH = 6912
O_QA, O_KA, O_VA, O_ZA, O_UB, O_VB, O_ZB, O_QM, O_ZM, O_LG = 0, 512, 640, 768, 1280, 1792, 2304, 2816, 3328, 3840
PBLK = 768
N_PBLK = IN_WIDTH // PBLK
MID_W = 3072
LG_W = 3072

LN2 = 0.6931471805599453
Q_SCALE = A_HEAD_DIM ** -0.5 / LN2

ADAM_LR, ADAM_B1, ADAM_B2, ADAM_EPS, ADAM_WD, ADAM_STEP = 0.001, 0.9, 0.999, 1e-08, 0.01, 10

V7X_VMEM_BYTES = 64 * 2**20
VMEM_LIMIT = V7X_VMEM_BYTES - 8 * 2**20
LANES = 128
MESH = pl.DeviceIdType.MESH
N_CHIPS = 4


def _cp(*sem):
    return pltpu.CompilerParams(dimension_semantics=sem if sem else None, vmem_limit_bytes=VMEM_LIMIT)


def _dot(a, b):
    return jnp.dot(a, b, preferred_element_type=F32)


def _dot_nt(a, b):
    return lax.dot_general(a, b, (((1,), (1,)), ((), ())), preferred_element_type=F32)


def _dot_tn(a, b):
    return lax.dot_general(a, b, (((0,), (0,)), ((), ())), preferred_element_type=F32)


def _dot_hi(a, b):
    return jnp.dot(a, b, preferred_element_type=F32, precision=lax.Precision.HIGHEST)


def _dot_nt_hi(a, b):
    return lax.dot_general(a, b, (((1,), (1,)), ((), ())), preferred_element_type=F32, precision=lax.Precision.HIGHEST)


def _sig(z):
    return 1.0 / (1.0 + jnp.exp(-z))


def _full(shape):
    nd = len(shape)
    return pl.BlockSpec(shape, lambda *_: (0,) * nd)


def _rows(tm, width):
    return pl.BlockSpec((tm, width), lambda i: (i, 0))


def _sds(shape, dtype):
    return jax.ShapeDtypeStruct(shape, dtype)


def rms_fwd(x, g):
    s, d = x.shape
    tm = min(s, 512)

    def body(x_ref, g_ref, h_ref):
        xf = x_ref[...]
        r = lax.rsqrt(jnp.mean(xf * xf, axis=-1, keepdims=True) + EPS)
        h_ref[...] = ((xf * r) * g_ref[...]).astype(BF16)

    return pl.pallas_call(
        body, out_shape=_sds((s, d), BF16), grid=(s // tm,),
        in_specs=[_rows(tm, d), _full((1, d))], out_specs=_rows(tm, d),
        compiler_params=_cp("parallel"), name="rms_fwd")(x, g)


def proj_fwd(h, w_t):
    s, d = h.shape
    n = w_t.shape[0]
    tm = min(s, 1024)
    tn = 1152

    def body(h_ref, w_ref, o_ref):
        o_ref[...] = _dot_nt(h_ref[...], w_ref[...])

    return pl.pallas_call(
        body, out_shape=_sds((s, n), F32), grid=(n // tn, s // tm),
        in_specs=[pl.BlockSpec((tm, d), lambda j, i: (i, 0)), pl.BlockSpec((tn, d), lambda j, i: (j, 0))],
        out_specs=pl.BlockSpec((tm, tn), lambda j, i: (i, j)),
        compiler_params=_cp("parallel", "parallel"), name="proj_fwd")(h, w_t)


def rope_tables(seq):
    rows = seq // GRID_W
    row = jnp.repeat(jnp.arange(rows, dtype=F32), GRID_W)
    col = jnp.tile(jnp.arange(GRID_W, dtype=F32), rows)
    n_freq = A_HEAD_DIM // 4
    inv = ROPE_THETA ** (-jnp.arange(n_freq, dtype=F32) / n_freq)
    ang = jnp.stack([row[:, None] * inv, col[:, None] * inv], axis=1)
    cos, sin = jnp.cos(ang), jnp.sin(ang)
    zero = jnp.zeros_like(sin[:, 0])
    c64 = jnp.concatenate([cos[:, 0], cos[:, 0], cos[:, 1], cos[:, 1]], axis=1)
    sa64 = jnp.concatenate([zero, sin[:, 0], zero, sin[:, 1]], axis=1)
    sb64 = jnp.concatenate([-sin[:, 0], zero, -sin[:, 1], zero], axis=1)
    two = lambda t: jnp.concatenate([t, t], axis=1)
    return two(c64), two(sa64), two(sb64)


def _group_ones(width, group):
    i = jnp.arange(width)
    return (i[:, None] // group == i[None, :] // group).astype(F32)


def _rope(xn, c, sa, sb):
    w = xn.shape[1]
    return xn * c + pltpu.roll(xn, 16, 1) * sa + pltpu.roll(xn, w - 16, 1) * sb


def _rope_t(dy, c, sa, sb):
    w = dy.shape[1]
    return dy * c + pltpu.roll(dy * sa, w - 16, 1) + pltpu.roll(dy * sb, 16, 1)


def _tile4(t):
    return jnp.concatenate([t, t, t, t], axis=1)


def qk_prep(proj, tabs, qg, kg, gq, gk):
    s = proj.shape[0]
    tm = min(s, 512)
    c, sa, sb = tabs

    def body(p_ref, c_ref, sa_ref, sb_ref, qg_ref, kg_ref, gq_ref, gk_ref, qt_ref, kr_ref, krt_ref, vb_ref, v0_ref, v1_ref):
        xq = p_ref[:, O_QA:O_QA + A_WIDTH]
        xk = p_ref[:, O_KA:O_KA + A_KV_WIDTH]
        xv = p_ref[:, O_VA:O_VA + A_KV_WIDTH]
        cc, ssa, ssb = c_ref[...], sa_ref[...], sb_ref[...]
        msq = _dot_hi(xq * xq, gq_ref[...]) * (1.0 / A_HEAD_DIM)
        qn = (xq * lax.rsqrt(msq + EPS)) * qg_ref[...]
        qr = _rope(qn, _tile4(cc), _tile4(ssa), _tile4(ssb)) * Q_SCALE
        qt_ref[...] = qr.T.astype(BF16)
        msk = _dot_hi(xk * xk, gk_ref[...]) * (1.0 / A_HEAD_DIM)
        kn = (xk * lax.rsqrt(msk + EPS)) * kg_ref[...]
        kr = _rope(kn, cc, ssa, ssb)
        kr_ref[...] = kr.astype(BF16)
        krt_ref[...] = kr.T.astype(BF16)
        vb_ref[...] = xv.astype(BF16)
        vt = xv.T.astype(BF16)
        one = jnp.ones((A_HEAD_DIM, tm), BF16)
        v0_ref[...] = jnp.concatenate([vt[:A_HEAD_DIM], one], axis=0)
        v1_ref[...] = jnp.concatenate([one, vt[A_HEAD_DIM:]], axis=0)

    tab = _rows(tm, LANES)
    colb = lambda w: pl.BlockSpec((w, tm), lambda i: (0, i))
    return pl.pallas_call(
        body,
        out_shape=(_sds((A_WIDTH, s), BF16), _sds((s, A_KV_WIDTH), BF16), _sds((A_KV_WIDTH, s), BF16),
                   _sds((s, A_KV_WIDTH), BF16), _sds((A_KV_WIDTH, s), BF16), _sds((A_KV_WIDTH, s), BF16)),
        grid=(s // tm,),
        in_specs=[_rows(tm, PBLK), tab, tab, tab, _full((1, A_WIDTH)), _full((1, A_KV_WIDTH)),
                  _full((A_WIDTH, A_WIDTH)), _full((A_KV_WIDTH, A_KV_WIDTH))],
        out_specs=(colb(A_WIDTH), _rows(tm, A_KV_WIDTH), colb(A_KV_WIDTH), _rows(tm, A_KV_WIDTH), colb(A_KV_WIDTH),
                   colb(A_KV_WIDTH)),
        compiler_params=_cp("parallel"), name="qk_prep")(proj, c, sa, sb, qg, kg, gq, gk)


def _pad_head(q_h, kv):
    z = jnp.zeros_like(q_h)
    return jnp.concatenate([q_h, z], axis=0) if kv == 0 else jnp.concatenate([z, q_h], axis=0)


def attn_fwd(q_t, kr, vte0, vte1, gather=()):
    s = kr.shape[0]
    tq = min(s, 256)
    kc = min(s, 512)
    nkc = s // kc
    nq = s // tq
    grp = A_HEADS // A_KV_HEADS
    ng = len(gather)

    def body(qt_ref, kr_ref, v0_ref, v1_ref, *rest):
        g_in, (o_ref, lse_ref), g_out = rest[:ng], rest[ng:ng + 2], rest[ng + 2:2 * ng + 2]
        qp_ref, m_ref, acc_ref = rest[2 * ng + 2:2 * ng + 5]
        if ng:
            start, forward, finish = gather_stages([g.shape for g in gather], g_in, g_out, *rest[2 * ng + 5:])
            pl.when(pl.program_id(0) == 0)(start)
            pl.when(pl.program_id(0) == nq // 2)(forward)

        for h in range(A_HEADS):
            qp_ref[h] = _pad_head(qt_ref[A_HEAD_DIM * h:A_HEAD_DIM * (h + 1), :], h // grp)
        m_ref[...] = jnp.full(m_ref.shape, -1e30, F32)
        acc_ref[...] = jnp.zeros_like(acc_ref)

        def step(ci, carry):
            ks = pl.ds(pl.multiple_of(ci * kc, kc), kc)
            kblk = kr_ref[ks, :]
            vts = (v0_ref[:, ks], v1_ref[:, ks])
            scs = [_dot(kblk, qp_ref[h]) for h in range(A_HEADS)]
            for h in range(A_HEADS):
                sc = scs[h]
                m_prev = m_ref[h:h + 1, :]
                m_new = jnp.maximum(m_prev, jnp.max(sc, axis=0, keepdims=True))
                p = jnp.exp2(sc - m_new)
                acc_ref[h] = acc_ref[h] * jnp.exp2(m_prev - m_new) + _dot(vts[h // grp], p.astype(BF16))
                m_ref[h:h + 1, :] = m_new
            return carry

        lax.fori_loop(0, nkc, step, 0)
        outs, lses = [], []
        for h in range(A_HEADS):
            kv = h // grp
            acc = acc_ref[h]
            l = acc[A_HEAD_DIM * (1 - kv):A_HEAD_DIM * (1 - kv) + 1, :]
            outs.append(acc[A_HEAD_DIM * kv:A_HEAD_DIM * (kv + 1), :] / l)
            lses.append(m_ref[h:h + 1, :] + jnp.log2(l))
        o_ref[...] = jnp.concatenate(outs, axis=0).T
        lse_ref[...] = jnp.concatenate(lses, axis=0)
        if ng:
            pl.when(pl.program_id(0) == nq - 1)(finish)

    out = pl.pallas_call(
        body,
        out_shape=(_sds((s, A_WIDTH), F32), _sds((A_HEADS, s), F32)) + tuple(_sds((N_CHIPS,) + g.shape, g.dtype) for g in gather),
        grid=(nq,),
        in_specs=[pl.BlockSpec((A_WIDTH, tq), lambda i: (0, i)), _full((s, A_KV_WIDTH)), _full((A_KV_WIDTH, s)),
                  _full((A_KV_WIDTH, s))] + [_ANY] * ng,
        out_specs=(_rows(tq, A_WIDTH), pl.BlockSpec((A_HEADS, tq), lambda i: (0, i))) + (_ANY,) * ng,
        scratch_shapes=[pltpu.VMEM((A_HEADS, A_KV_WIDTH, tq), BF16), pltpu.VMEM((A_HEADS, tq), F32),
                        pltpu.VMEM((A_HEADS, A_KV_WIDTH, tq), F32)] + (gather_sems(ng) if ng else []),
        compiler_params=_cp("arbitrary"), name="attn_fwd_gather" if ng else "attn_fwd")(q_t, kr, vte0, vte1, *gather)
    return out[0], out[1], list(out[2:])


def memkv_fwd(mem, g, w_kv):
    m, d = mem.shape

    def body(mem_ref, g_ref, w_ref, mn_ref, kv_ref):
        mf = mem_ref[...]
        r = lax.rsqrt(jnp.mean(mf * mf, axis=-1, keepdims=True) + EPS)
        mn = ((mf * r) * g_ref[...]).astype(BF16)
        mn_ref[...] = mn
        kv_ref[...] = _dot(mn, w_ref[...]).astype(BF16)

    return pl.pallas_call(
        body, out_shape=(_sds((m, d), BF16), _sds((m, 2 * M_WIDTH), BF16)),
        compiler_params=_cp(), name="memkv_fwd")(mem, g, w_kv)


def _layer_norm_stats(v):
    mu = jnp.mean(v, axis=-1, keepdims=True)
    xc = v - mu
    rstd = lax.rsqrt(jnp.mean(xc * xc, axis=-1, keepdims=True) + EPS)
    return xc * rstd, rstd


def _spatial_mix(vlb, ws_ref, bsb_ref, tm):
    rows = []
    for ci in range(tm // CHUNK):
        cols = []
        for g in range(B_GROUPS):
            blk = vlb[ci * CHUNK:(ci + 1) * CHUNK, g * B_GROUP_DIM:(g + 1) * B_GROUP_DIM]
            cols.append(_dot(ws_ref[g], blk) + bsb_ref[g])
        rows.append(jnp.concatenate(cols, axis=1))
    return jnp.concatenate(rows, axis=0)


def _mem_attn(qm, kv_ref):
    out = []
    for h in range(M_HEADS):
        qh = qm[:, h * M_HEAD_DIM:(h + 1) * M_HEAD_DIM].astype(BF16)
        kh = kv_ref[:, h * M_HEAD_DIM:(h + 1) * M_HEAD_DIM]
        vh = kv_ref[:, M_WIDTH + h * M_HEAD_DIM:M_WIDTH + (h + 1) * M_HEAD_DIM]
        sc = _dot_nt(qh, kh) * (M_HEAD_DIM ** -0.5)
        e = jnp.exp(sc - jnp.max(sc, axis=-1, keepdims=True))
        p = e / jnp.sum(e, axis=-1, keepdims=True)
        out.append((p, _dot(p.astype(BF16), vh)))
    return out


def branch_fwd(x, proj, o_a, kv, ws, bsb, ln_g, ln_b, w_br, w_out):
    s, d = x.shape
    tm = min(s, 256)

    def body(x_ref, p_ref, oa_ref, kv_ref, ws_ref, bsb_ref, lg_ref, lb_ref, wbr_ref, wo_ref,
             xn_ref, y_ref, up_ref, mg_ref):
        seg = lambda o, w: p_ref[:, o:o + w]
        z_a, u_b, v_b, z_b = seg(O_ZA, A_WIDTH), seg(O_UB, B_WIDTH), seg(O_VB, B_WIDTH), seg(O_ZB, B_WIDTH)
        q_m, z_m = seg(O_QM, M_WIDTH), seg(O_ZM, M_WIDTH)
        xhat, _ = _layer_norm_stats(v_b)
        vln = xhat * lg_ref[...] + lb_ref[...]
        mixed = _spatial_mix(vln.astype(BF16), ws_ref, bsb_ref, tm)
        y_b = (u_b * mixed) * (z_b * _sig(z_b))
        o_m = jnp.concatenate([o for _, o in _mem_attn(q_m, kv_ref)], axis=1)
        y_a = oa_ref[...] * (z_a * _sig(z_a))
        y_m = o_m * (z_m * _sig(z_m))
        merged = None
        for n, yy in enumerate((y_a, y_b, y_m)):
            yb = yy.astype(BF16)
            y_ref[n] = yb
            up = jnp.concatenate([_dot(yb, wbr_ref[c, n]) for c in range(N_CHIPS)], axis=1)
            up_ref[n] = up.astype(BF16)
            t = _sig(seg(O_LG + n * d, d)) * up
            merged = t if merged is None else merged + t
        mb = merged.astype(BF16)
        mg_ref[...] = mb
        xn_ref[...] = x_ref[...] + _dot(mb, wo_ref[...])

    return pl.pallas_call(
        body,
        out_shape=(_sds((s, d), F32), _sds((N_BRANCH, s, A_WIDTH), BF16), _sds((N_BRANCH, s, d), BF16), _sds((s, d), BF16)),
        grid=(s // tm,),
        in_specs=[_rows(tm, d), _rows(tm, IN_WIDTH), _rows(tm, A_WIDTH), _full(kv.shape), _full(ws.shape), _full(bsb.shape),
                  _full((1, B_WIDTH)), _full((1, B_WIDTH)), _full(w_br.shape), _full(w_out.shape)],
        out_specs=(_rows(tm, d), pl.BlockSpec((N_BRANCH, tm, A_WIDTH), lambda i: (0, i, 0)),
                   pl.BlockSpec((N_BRANCH, tm, d), lambda i: (0, i, 0)), _rows(tm, d)),
        compiler_params=_cp("parallel"), name="branch_fwd")(x, proj, o_a, kv, ws, bsb, ln_g, ln_b, w_br, w_out)


def final_loss(x, fg, tgt):
    s, d = x.shape
    tm = min(s, 512)

    def body(x_ref, g_ref, t_ref, ls_ref, dx_ref, gg_ref):
        @pl.when(pl.program_id(0) == 0)
        def _():
            ls_ref[...] = jnp.zeros_like(ls_ref)
            gg_ref[...] = jnp.zeros_like(gg_ref)

        xf = x_ref[...]
        g = g_ref[...]
        r = lax.rsqrt(jnp.mean(xf * xf, axis=-1, keepdims=True) + EPS)
        xh = xf * r
        e = xh * g - t_ref[...]
        sq = jnp.sum(jnp.sum(e * e, axis=0, keepdims=True), axis=1, keepdims=True)
        ls_ref[...] += jnp.broadcast_to(sq, ls_ref.shape)
        dy = e * (1.0 / d)
        gg_ref[...] += jnp.sum(dy * xh, axis=0, keepdims=True)
        gy = dy * g
        dx_ref[...] = r * (gy - xh * jnp.mean(gy * xh, axis=-1, keepdims=True))

    return pl.pallas_call(
        body, out_shape=(_sds((1, LANES), F32), _sds((s, d), F32), _sds((1, d), F32)), grid=(s // tm,),
        in_specs=[_rows(tm, d), _full((1, d)), _rows(tm, d)],
        out_specs=(_full((1, LANES)), _rows(tm, d), _full((1, d))),
        compiler_params=_cp("arbitrary"), name="final_loss")(x, fg, tgt)


def _pblocks(tm, first, count):
    return [pl.BlockSpec((tm, PBLK), functools.partial(lambda i, b: (i, b), b=first + k)) for k in range(count)]


def merge_bwd(dx, proj, y, up, merged, w_br, w_out):
    s, d = dx.shape
    tm = min(s, 256)
    nlg = LG_W // PBLK
    cw = d // N_CHIPS

    def body(dx_ref, l0, l1, l2, l3, y_ref, up_ref, mg_ref, wbr_ref, wo_ref, dy_ref, dlg_ref, gwo_ref, gwb_ref, gwo16_ref, gwb16_ref):
        @pl.when(pl.program_id(0) == 0)
        def _():
            gwo_ref[...] = jnp.zeros_like(gwo_ref)
            gwb_ref[...] = jnp.zeros_like(gwb_ref)

        dxb = dx_ref[...].astype(BF16)
        dmg = _dot_nt(dxb, wo_ref[...])
        gwo_ref[...] += _dot_tn(mg_ref[...], dxb)
        lg = jnp.concatenate([l0[...], l1[...], l2[...], l3[...]], axis=1)
        for n in range(N_BRANCH):
            g = _sig(lg[:, n * d:(n + 1) * d])
            dup = dmg * g
            dlg_ref[:, n * d:(n + 1) * d] = ((dup * up_ref[n].astype(F32)) * (1.0 - g)).astype(BF16)
            dupb = dup.astype(BF16)
            dyn = None
            for c in range(N_CHIPS):
                blk = dupb[:, c * cw:(c + 1) * cw]
                gwb_ref[c, n] += _dot_tn(y_ref[n], blk)
                t = _dot_nt(blk, wbr_ref[c, n])
                dyn = t if dyn is None else dyn + t
            dy_ref[n] = dyn

        @pl.when(pl.program_id(0) == pl.num_programs(0) - 1)
        def _():
            gwo16_ref[...] = gwo_ref[...].astype(BF16)
            gwb16_ref[...] = gwb_ref[...].astype(BF16)

    return pl.pallas_call(
        body,
        out_shape=(_sds((N_BRANCH, s, A_WIDTH), F32), _sds((s, LG_W), BF16), _sds((d, d), F32), _sds(w_br.shape, F32),
                   _sds((d, d), BF16), _sds(w_br.shape, BF16)),
        grid=(s // tm,),
        in_specs=[_rows(tm, d)] + _pblocks(tm, O_LG // PBLK, nlg) + [
            pl.BlockSpec((N_BRANCH, tm, A_WIDTH), lambda i: (0, i, 0)), pl.BlockSpec((N_BRANCH, tm, d), lambda i: (0, i, 0)),
            _rows(tm, d), _full(w_br.shape), _full(w_out.shape)],
        out_specs=(pl.BlockSpec((N_BRANCH, tm, A_WIDTH), lambda i: (0, i, 0)), _rows(tm, LG_W), _full((d, d)), _full(w_br.shape),
                   _full((d, d)), _full(w_br.shape)),
        compiler_params=_cp("arbitrary"), name="merge_bwd")(dx, proj, proj, proj, proj, y, up, merged, w_br, w_out)


def _dsilu(z, sg):
    return sg * (1.0 + z * (1.0 - sg))


def branch_bwd(dy, proj, o_a, kv, ws, ws_t, bsb, ln_g, ln_b, head_sel):
    s = proj.shape[0]
    tm = min(s, 256)
    nmid = MID_W // PBLK

    def body(dy_ref, m0, m1, m2, m3, oa_ref, kv_ref, ws_ref, wst_ref, bsb_ref, lg_ref, lb_ref, sel_ref,
             dmid_ref, dot_ref, dl_ref, gws_ref, gbs_ref, glg_ref, glb_ref, dkv_ref):
        @pl.when(pl.program_id(0) == 0)
        def _():
            for r in (gws_ref, gbs_ref, glg_ref, glb_ref, dkv_ref):
                r[...] = jnp.zeros_like(r)

        mid = jnp.concatenate([m0[...], m1[...], m2[...], m3[...]], axis=1)
        seg = lambda o, w: mid[:, o - O_ZA:o - O_ZA + w]
        z_a, u_b, v_b, z_b = seg(O_ZA, A_WIDTH), seg(O_UB, B_WIDTH), seg(O_VB, B_WIDTH), seg(O_ZB, B_WIDTH)
        q_m, z_m = seg(O_QM, M_WIDTH), seg(O_ZM, M_WIDTH)

        def put(o, v):
            dmid_ref[:, o - O_ZA:o - O_ZA + v.shape[1]] = v.astype(BF16)

        dy_a, dy_b, dy_m = dy_ref[0], dy_ref[1], dy_ref[2]

        o_a_ = oa_ref[...]
        sg = _sig(z_a)
        do_a = dy_a * (z_a * sg)
        put(O_ZA, (dy_a * o_a_) * _dsilu(z_a, sg))
        do_l = do_a * LN2
        dot_ref[...] = do_l.T.astype(BF16)
        dl_ref[...] = _dot_nt_hi(sel_ref[...], do_l * o_a_)

        xhat, rstd = _layer_norm_stats(v_b)
        lng = lg_ref[...]
        vln = xhat * lng + lb_ref[...]
        vlb = vln.astype(BF16)
        mixed = _spatial_mix(vlb, ws_ref, bsb_ref, tm)
        sg = _sig(z_b)
        sl = z_b * sg
        put(O_UB, (dy_b * mixed) * sl)
        put(O_ZB, ((dy_b * u_b) * mixed) * _dsilu(z_b, sg))
        dmix = (dy_b * u_b) * sl
        dmb = dmix.astype(BF16)
        rows = []
        for ci in range(tm // CHUNK):
            cols = []
            for g in range(B_GROUPS):
                rs, cs = slice(ci * CHUNK, (ci + 1) * CHUNK), slice(g * B_GROUP_DIM, (g + 1) * B_GROUP_DIM)
                gws_ref[g] += _dot_nt(dmb[rs, cs], vlb[rs, cs])
                gbs_ref[g] += jnp.broadcast_to(jnp.sum(dmix[rs, cs], axis=1, keepdims=True), (CHUNK, B_GROUP_DIM))
                cols.append(_dot(wst_ref[g], dmb[rs, cs]))
            rows.append(jnp.concatenate(cols, axis=1))
        dvln = jnp.concatenate(rows, axis=0)
        glg_ref[...] += jnp.sum(dvln * xhat, axis=0, keepdims=True)
        glb_ref[...] += jnp.sum(dvln, axis=0, keepdims=True)
        gy = dvln * lng
        put(O_VB, rstd * ((gy - jnp.mean(gy, axis=-1, keepdims=True)) - xhat * jnp.mean(gy * xhat, axis=-1, keepdims=True)))

        sg = _sig(z_m)
        sl = z_m * sg
        heads = _mem_attn(q_m, kv_ref)
        o_m = jnp.concatenate([o for _, o in heads], axis=1)
        put(O_ZM, (dy_m * o_m) * _dsilu(z_m, sg))
        do_m = dy_m * sl
        dqs = []
        for h, (p, o_h) in enumerate(heads):
            hs = slice(h * M_HEAD_DIM, (h + 1) * M_HEAD_DIM)
            vs = slice(M_WIDTH + h * M_HEAD_DIM, M_WIDTH + (h + 1) * M_HEAD_DIM)
            do_h = do_m[:, hs]
            dob = do_h.astype(BF16)
            dp = _dot_nt(dob, kv_ref[:, vs])
            dsc = (p * (dp - jnp.sum(do_h * o_h, axis=-1, keepdims=True))) * (M_HEAD_DIM ** -0.5)
            dsb = dsc.astype(BF16)
            dqs.append(_dot(dsb, kv_ref[:, hs]))
            dkv_ref[:, hs] += _dot_tn(dsb, q_m[:, hs].astype(BF16))
            dkv_ref[:, vs] += _dot_tn(p.astype(BF16), dob)
        put(O_QM, jnp.concatenate(dqs, axis=1))

    return pl.pallas_call(
        body,
        out_shape=(_sds((s, MID_W), BF16), _sds((A_WIDTH, s), BF16), _sds((A_HEADS, s), F32), _sds(ws.shape, F32),
                   _sds(ws.shape, F32), _sds((1, B_WIDTH), F32), _sds((1, B_WIDTH), F32), _sds(kv.shape, F32)),
        grid=(s // tm,),
        in_specs=[pl.BlockSpec((N_BRANCH, tm, A_WIDTH), lambda i: (0, i, 0))] + _pblocks(tm, O_ZA // PBLK, nmid) + [
            _rows(tm, A_WIDTH), _full(kv.shape), _full(ws.shape), _full(ws.shape), _full(bsb.shape),
            _full((1, B_WIDTH)), _full((1, B_WIDTH)), _full(head_sel.shape)],
        out_specs=(_rows(tm, MID_W), pl.BlockSpec((A_WIDTH, tm), lambda i: (0, i)), pl.BlockSpec((A_HEADS, tm), lambda i: (0, i)),
                   _full(ws.shape), _full(ws.shape), _full((1, B_WIDTH)), _full((1, B_WIDTH)), _full(kv.shape)),
        compiler_params=_cp("arbitrary"), name="branch_bwd")(dy, proj, proj, proj, proj, o_a, kv, ws, ws_t, bsb, ln_g, ln_b, head_sel)


def attn_bwd(q_t, do_t, kr, kr_t, vb, lse, delta, scatter=()):
    s = kr.shape[0]
    tq = min(s, 256)
    kc = min(s, 512)
    nkc = s // kc
    nq = s // tq
    grp = A_HEADS // A_KV_HEADS
    ns = len(scatter)
    na = ns // 2

    def body(qt_ref, dot_ref, kr_ref, krt_ref, vb_ref, lse_ref, dl_ref, *rest):
        s_in, (dqt_ref, dk_ref, dv_ref), s_out = rest[:ns], rest[ns:ns + 3], rest[ns + 3:2 * ns + 3]
        qp_ref, dop_ref, dq_ref = rest[2 * ns + 3:2 * ns + 6]
        if ns:
            start, finish = scatter_stages([g.shape[1:] for g in scatter[:na]], s_in[:na], s_in[na:], s_out[:na], s_out[na:],
                                           *rest[2 * ns + 6:])
            pl.when(pl.program_id(0) == 0)(start)

        @pl.when(pl.program_id(0) == 0)
        def _():
            dk_ref[...] = jnp.zeros_like(dk_ref)
            dv_ref[...] = jnp.zeros_like(dv_ref)

        for h in range(A_HEADS):
            hs = slice(A_HEAD_DIM * h, A_HEAD_DIM * (h + 1))
            qp_ref[h] = _pad_head(qt_ref[hs, :], h // grp)
            dop_ref[h] = _pad_head(dot_ref[hs, :], h // grp)
        dq_ref[...] = jnp.zeros_like(dq_ref)

        def step(ci, carry):
            ks = pl.ds(pl.multiple_of(ci * kc, kc), kc)
            kblk, vblk, ktb = kr_ref[ks, :], vb_ref[ks, :], krt_ref[:, ks]
            dv_acc = jnp.zeros((kc, A_KV_WIDTH), F32)
            dk_acc = jnp.zeros((kc, A_KV_WIDTH), F32)
            scs = [_dot(kblk, qp_ref[h]) for h in range(A_HEADS)]
            dps = [_dot(vblk, dop_ref[h]) for h in range(A_HEADS)]
            for h in range(A_HEADS):
                qpad, dopad = qp_ref[h], dop_ref[h]
                p = jnp.exp2(scs[h] - lse_ref[h:h + 1, :])
                dsb = (p * (dps[h] - dl_ref[h:h + 1, :])).astype(BF16)
                dv_acc = dv_acc + _dot_nt(p.astype(BF16), dopad)
                dk_acc = dk_acc + _dot_nt(dsb, qpad)
                dq_ref[h] += _dot(ktb, dsb)
            dv_ref[ks, :] += dv_acc
            dk_ref[ks, :] += dk_acc
            return carry

        lax.fori_loop(0, nkc, step, 0)
        dqt_ref[...] = jnp.concatenate(
            [dq_ref[h][A_HEAD_DIM * (h // grp):A_HEAD_DIM * (h // grp + 1), :] for h in range(A_HEADS)], axis=0)
        if ns:
            pl.when(pl.program_id(0) == nq - 1)(finish)

    colq = pl.BlockSpec((A_WIDTH, tq), lambda i: (0, i))
    colh = pl.BlockSpec((A_HEADS, tq), lambda i: (0, i))
    out = pl.pallas_call(
        body,
        out_shape=(_sds((A_WIDTH, s), F32), _sds((s, A_KV_WIDTH), F32), _sds((s, A_KV_WIDTH), F32)) + scatter_out_shapes(scatter[:na]),
        grid=(nq,),
        in_specs=[colq, colq, _full((s, A_KV_WIDTH)), _full((A_KV_WIDTH, s)), _full((s, A_KV_WIDTH)), colh, colh] + [_ANY] * ns,
        out_specs=(colq, _full((s, A_KV_WIDTH)), _full((s, A_KV_WIDTH))) + (_ANY,) * ns,
        scratch_shapes=[pltpu.VMEM((A_HEADS, A_KV_WIDTH, tq), BF16), pltpu.VMEM((A_HEADS, A_KV_WIDTH, tq), BF16),
                        pltpu.VMEM((A_HEADS, A_KV_WIDTH, tq), F32)] + (scatter_sems(na) if ns else []),
        compiler_params=_cp("arbitrary"), name="attn_bwd_scatter" if ns else "attn_bwd")(q_t, do_t, kr, kr_t, vb, lse, delta, *scatter)
    return out[0], out[1], out[2], list(out[3:3 + na]), list(out[3 + na:])


def qk_prep_bwd(proj, dq_t, dkr, dvb, tabs, qg, kg, gq, gk, fold_q, fold_k):
    s = proj.shape[0]
    tm = min(s, 512)
    c, sa, sb = tabs

    def head_norm_bwd(x, dn, gain, gones, fold):
        ms = _dot_hi(x * x, gones) * (1.0 / A_HEAD_DIM)
        r = lax.rsqrt(ms + EPS)
        xh = x * r
        gg = _dot_hi(jnp.sum(dn * xh, axis=0, keepdims=True), fold)
        u = dn * gain
        mean_u = _dot_hi(u * xh, gones) * (1.0 / A_HEAD_DIM)
        return r * (u - xh * mean_u), gg

    def body(p_ref, dqt_ref, dk_ref, dv_ref, c_ref, sa_ref, sb_ref, qg_ref, kg_ref, gq_ref, gk_ref, fq_ref, fk_ref,
             dqkv_ref, gqg_ref, gkg_ref):
        @pl.when(pl.program_id(0) == 0)
        def _():
            gqg_ref[...] = jnp.zeros_like(gqg_ref)
            gkg_ref[...] = jnp.zeros_like(gkg_ref)

        cc, ssa, ssb = c_ref[...], sa_ref[...], sb_ref[...]
        dqr = dqt_ref[...].T * Q_SCALE
        dqn = _rope_t(dqr, _tile4(cc), _tile4(ssa), _tile4(ssb))
        dxq, gq_ = head_norm_bwd(p_ref[:, O_QA:O_QA + A_WIDTH], dqn, qg_ref[...], gq_ref[...], fq_ref[...])
        dkn = _rope_t(dk_ref[...], cc, ssa, ssb)
        dxk, gk_ = head_norm_bwd(p_ref[:, O_KA:O_KA + A_KV_WIDTH], dkn, kg_ref[...], gk_ref[...], fk_ref[...])
        gqg_ref[...] += gq_
        gkg_ref[...] += gk_
        dqkv_ref[:, O_QA:O_QA + A_WIDTH] = dxq.astype(BF16)
        dqkv_ref[:, O_KA:O_KA + A_KV_WIDTH] = dxk.astype(BF16)
        dqkv_ref[:, O_VA:O_VA + A_KV_WIDTH] = (dv_ref[...] * (1.0 / LN2)).astype(BF16)

    tab = _rows(tm, LANES)
    return pl.pallas_call(
        body, out_shape=(_sds((s, PBLK), BF16), _sds((1, LANES), F32), _sds((1, LANES), F32)), grid=(s // tm,),
        in_specs=[_rows(tm, PBLK), pl.BlockSpec((A_WIDTH, tm), lambda i: (0, i)), _rows(tm, A_KV_WIDTH), _rows(tm, A_KV_WIDTH),
                  tab, tab, tab, _full((1, A_WIDTH)), _full((1, A_KV_WIDTH)), _full((A_WIDTH, A_WIDTH)),
                  _full((A_KV_WIDTH, A_KV_WIDTH)), _full((A_WIDTH, LANES)), _full((A_KV_WIDTH, LANES))],
        out_specs=(_rows(tm, PBLK), _full((1, LANES)), _full((1, LANES))),
        compiler_params=_cp("arbitrary"), name="qk_prep_bwd")(proj, dq_t, dkr, dvb, c, sa, sb, qg, kg, gq, gk, fold_q, fold_k)


def _pick_dproj(b, d0, d1, d2, use):
    first_lg = 1 + MID_W // PBLK

    @pl.when(b == 0)
    def _():
        use(d0[...])

    @pl.when(jnp.logical_and(b >= 1, b < first_lg))
    def _():
        use(d1[...])

    @pl.when(b >= first_lg)
    def _():
        use(d2[...])


def win_grad(d0, d1, d2, h):
    s, d = h.shape
    tk = min(s, 1024)
    nk = s // tk

    def body(d0_ref, d1_ref, d2_ref, h_ref, o_ref, o16_ref):
        @pl.when(pl.program_id(1) == 0)
        def _():
            o_ref[...] = jnp.zeros_like(o_ref)

        def use(blk):
            o_ref[...] += _dot_tn(blk, h_ref[...])

        _pick_dproj(pl.program_id(0), d0_ref, d1_ref, d2_ref, use)

        @pl.when(pl.program_id(1) == nk - 1)
        def _():
            o16_ref[...] = o_ref[...].astype(BF16)

    def spec(first, count):
        return pl.BlockSpec((tk, PBLK), lambda j, k: (k, jnp.clip(j - first, 0, count - 1)))

    nm = MID_W // PBLK
    oblk = pl.BlockSpec((PBLK, d), lambda j, k: (j, 0))
    return pl.pallas_call(
        body, out_shape=(_sds((IN_WIDTH, d), F32), _sds((IN_WIDTH, d), BF16)), grid=(N_PBLK, nk),
        in_specs=[spec(0, 1), spec(1, nm), spec(1 + nm, LG_W // PBLK), pl.BlockSpec((tk, d), lambda j, k: (k, 0))],
        out_specs=(oblk, oblk),
        compiler_params=_cp("parallel", "arbitrary"), name="win_grad")(d0, d1, d2, h)


def h_bwd(d0, d1, d2, w_t, x, dx_out, g):
    s, d = x.shape
    tm = min(s, 512)

    def body(d0_ref, d1_ref, d2_ref, w_ref, x_ref, dxo_ref, g_ref, dx_ref, gg_ref, acc_ref):
        i, k = pl.program_id(0), pl.program_id(1)

        @pl.when(jnp.logical_and(i == 0, k == 0))
        def _():
            gg_ref[...] = jnp.zeros_like(gg_ref)

        @pl.when(k == 0)
        def _():
            acc_ref[...] = jnp.zeros_like(acc_ref)

        def use(blk):
            acc_ref[...] += _dot(blk, w_ref[...])

        _pick_dproj(k, d0_ref, d1_ref, d2_ref, use)

        @pl.when(k == N_PBLK - 1)
        def _():
            xf = x_ref[...]
            r = lax.rsqrt(jnp.mean(xf * xf, axis=-1, keepdims=True) + EPS)
            xh = xf * r
            dh = acc_ref[...]
            gg_ref[...] += jnp.sum(dh * xh, axis=0, keepdims=True)
            u = dh * g_ref[...]
            dx_ref[...] = dxo_ref[...] + r * (u - xh * jnp.mean(u * xh, axis=-1, keepdims=True))

    def spec(first, count):
        return pl.BlockSpec((tm, PBLK), lambda i, k: (i, jnp.clip(k - first, 0, count - 1)))

    nm = MID_W // PBLK
    rowb = pl.BlockSpec((tm, d), lambda i, k: (i, 0))
    return pl.pallas_call(
        body, out_shape=(_sds((s, d), F32), _sds((1, d), F32)), grid=(s // tm, N_PBLK),
        in_specs=[spec(0, 1), spec(1, nm), spec(1 + nm, LG_W // PBLK), pl.BlockSpec((PBLK, d), lambda i, k: (k, 0)),
                  rowb, rowb, pl.BlockSpec((1, d), lambda i, k: (0, 0))],
        out_specs=(rowb, pl.BlockSpec((1, d), lambda i, k: (0, 0))),
        scratch_shapes=[pltpu.VMEM((tm, d), F32)],
        compiler_params=_cp("arbitrary", "arbitrary"), name="h_bwd")(d0, d1, d2, w_t, x, dx_out, g)


def memkv_bwd(mem, g, mem_n, w_kv, dkv):
    m, d = mem.shape

    def body(mem_ref, g_ref, mn_ref, w_ref, dkv_ref, gw_ref, gw16_ref, gg_ref):
        dkb = dkv_ref[...].astype(BF16)
        gw = _dot_tn(mn_ref[...], dkb)
        gw_ref[...] = gw
        gw16_ref[...] = gw.astype(BF16)
        dmn = _dot_nt(dkb, w_ref[...])
        mf = mem_ref[...]
        r = lax.rsqrt(jnp.mean(mf * mf, axis=-1, keepdims=True) + EPS)
        gg_ref[...] = jnp.sum(dmn * (mf * r), axis=0, keepdims=True)

    return pl.pallas_call(
        body, out_shape=(_sds(w_kv.shape, F32), _sds(w_kv.shape, BF16), _sds((1, d), F32)),
        compiler_params=_cp(), name="memkv_bwd")(mem, g, mem_n, w_kv, dkv)


def _layer_consts(seq):
    i = jnp.arange(A_WIDTH)
    return dict(
        tabs=rope_tables(seq),
        gq=_group_ones(A_WIDTH, A_HEAD_DIM), gk=_group_ones(A_KV_WIDTH, A_HEAD_DIM),
        fold_q=(i[:, None] % A_HEAD_DIM == jnp.arange(LANES)[None, :]).astype(F32),
        fold_k=(i[:A_KV_WIDTH, None] % A_HEAD_DIM == jnp.arange(LANES)[None, :]).astype(F32),
        head_sel=(jnp.arange(A_HEADS)[:, None] == i[None, :] // A_HEAD_DIM).astype(F32),
    )


_BIG = ("win_t", "wkv", "wbr", "wout")


def _with_own_part(gathered, shards, chip):
    win_t, wkv, wbr, wout = [lax.dynamic_update_slice(g, sh[None], (chip, 0, 0)) for g, sh in zip(gathered, shards)]
    d = wout.shape[-1]
    return dict(win_t=win_t.reshape(IN_WIDTH, d), wkv=wkv.reshape(d, 2 * M_WIDTH),
                wbr=wbr.reshape(N_CHIPS, N_BRANCH, A_WIDTH, d // N_CHIPS), wout=wout.reshape(d, d))


def local_fwd_bwd(x, mem, tgt, small, big=None, shards=None, place=None):
    s, d = x.shape
    depth = small["norm_g"].shape[0]
    k = _layer_consts(s)
    row = lambda v: v.reshape(1, -1)
    dist = shards is not None
    if dist:
        big = [_with_own_part(allgather_layer(shards[0]), shards[0], place[0])] + [None] * (depth - 1)
    saved = []
    for l in range(depth):
        ng = row(small["norm_g"][l])
        qg = row(jnp.tile(small["q_norm_g"][l], A_HEADS))
        kg = row(jnp.tile(small["k_norm_g"][l], A_KV_HEADS))
        ws = small["w_s"][l].astype(BF16)
        ws_t = jnp.swapaxes(small["w_s"][l], 1, 2).astype(BF16)
        bsb = jnp.broadcast_to(small["b_s"][l][:, :, None], (B_GROUPS, CHUNK, B_GROUP_DIM))
        lng, lnb = row(small["sg_ln_g"][l]), row(small["sg_ln_b"][l])
        mg = row(small["mem_norm_g"][l])
        w = big[l]
        h = rms_fwd(x, ng)
        proj = proj_fwd(h, w["win_t"])
        q_t, kr, kr_t, vb, vte0, vte1 = qk_prep(proj, k["tabs"], qg, kg, k["gq"], k["gk"])
        nxt = shards[l + 1] if dist and l + 1 < depth else ()
        o_a, lse, gathered = attn_fwd(q_t, kr, vte0, vte1, gather=tuple(nxt))
        if nxt:
            big[l + 1] = _with_own_part(gathered, nxt, place[0])
        mem_n, kv = memkv_fwd(mem, mg, w["wkv"])
        x_next, y, up, merged = branch_fwd(x, proj, o_a, kv, ws, bsb, lng, lnb, w["wbr"], w["wout"])
        saved.append(dict(x=x, ng=ng, qg=qg, kg=kg, ws=ws, ws_t=ws_t, bsb=bsb, lng=lng, lnb=lnb, mg=mg, h=h, proj=proj,
                          q_t=q_t, kr=kr, kr_t=kr_t, vb=vb, o_a=o_a, lse=lse, mem_n=mem_n, kv=kv, y=y, up=up, merged=merged))
        x = x_next

    sq, dx, g_final = final_loss(x, row(small["final_g"]), tgt)
    grads = {n: [None] * depth for n in ("norm_g", "q_norm_g", "k_norm_g", "sg_ln_g", "sg_ln_b", "w_s", "b_s", "mem_norm_g", "big")}
    pending = ()
    for l in reversed(range(depth)):
        sv, w = saved[l], big[l]
        dy, dlg, g_wout, g_wbr, g_wout16, g_wbr16 = merge_bwd(dx, sv["proj"], sv["y"], sv["up"], sv["merged"], w["wbr"], w["wout"])
        dmid, do_t, delta, g_ws, g_bs, g_lng, g_lnb, dkv = branch_bwd(
            dy, sv["proj"], sv["o_a"], sv["kv"], sv["ws"], sv["ws_t"], sv["bsb"], sv["lng"], sv["lnb"], k["head_sel"])
        dq_t, dkr, dvb, t_sib, t_rem = attn_bwd(sv["q_t"], do_t, sv["kr"], sv["kr_t"], sv["vb"], sv["lse"], delta, scatter=pending)
        if pending:
            grads["big"][l + 1] = [reduce_rows(place, g, ts, tr) for g, ts, tr in zip(pending[:len(_BIG)], t_sib, t_rem)]
        dqkv, g_qg, g_kg = qk_prep_bwd(sv["proj"], dq_t, dkr, dvb, k["tabs"], sv["qg"], sv["kg"], k["gq"], k["gk"],
                                       k["fold_q"], k["fold_k"])
        g_wkv, g_wkv16, g_mg = memkv_bwd(mem, sv["mg"], sv["mem_n"], w["wkv"], dkv)
        g_win, g_win16 = win_grad(dqkv, dmid, dlg, sv["h"])
        dx, g_ng = h_bwd(dqkv, dmid, dlg, w["win_t"], sv["x"], dx, sv["ng"])
        grads["norm_g"][l] = g_ng[0]
        grads["q_norm_g"][l] = g_qg[0, :A_HEAD_DIM]
        grads["k_norm_g"][l] = g_kg[0, :A_HEAD_DIM]
        grads["sg_ln_g"][l] = g_lng[0]
        grads["sg_ln_b"][l] = g_lnb[0]
        grads["w_s"][l] = g_ws
        grads["b_s"][l] = g_bs[:, :, 0]
        grads["mem_norm_g"][l] = g_mg[0]
        parts = lambda gs: tuple(g.reshape(N_CHIPS, -1, g.shape[-1]) for g in gs)
        full, half = parts((g_win, g_wkv, g_wbr, g_wout)), parts((g_win16, g_wkv16, g_wbr16, g_wout16))
        if dist:
            pending = full + half
        else:
            grads["big"][l] = dict(zip(_BIG, full))
    if dist:
        t_sib, t_rem = scatter_layer(pending)
        grads["big"][0] = [reduce_rows(place, g, ts, tr) for g, ts, tr in zip(pending[:len(_BIG)], t_sib, t_rem)]
    big_grads = grads.pop("big")
    grads = {n: jnp.stack(v) for n, v in grads.items()}
    grads["final_g"] = g_final[0]
    return sq[0, 0], dx, grads, big_grads


def _row_block(rows, width, cap_bytes=2 * 2**20):
    best = None
    for br in range(8, rows + 1, 8):
        if rows % br == 0 and br * width * 4 <= cap_bytes:
            best = br
    return best if best is not None else rows


def adamw(w, g, m, v):
    r, c = w.shape
    br = _row_block(r, c)

    def body(w_ref, g_ref, m_ref, v_ref, d_ref, nm_ref, nv_ref):
        gg = g_ref[...]
        mm = ADAM_B1 * m_ref[...] + (1.0 - ADAM_B1) * gg
        vv = ADAM_B2 * v_ref[...] + (1.0 - ADAM_B2) * (gg * gg)
        m_hat = mm / (1.0 - ADAM_B1 ** ADAM_STEP)
        v_hat = vv / (1.0 - ADAM_B2 ** ADAM_STEP)
        d_ref[...] = -ADAM_LR * (m_hat / (jnp.sqrt(v_hat) + ADAM_EPS) + ADAM_WD * w_ref[...])
        nm_ref[...] = mm
        nv_ref[...] = vv

    blk = _rows(br, c)
    return pl.pallas_call(
        body, out_shape=(_sds((r, c), F32),) * 3, grid=(r // br,), in_specs=[blk] * 4, out_specs=(blk,) * 3,
        compiler_params=_cp("parallel"), name="adamw")(w, g, m, v)


N_REMOTE = 2 * (N_CHIPS - 1)


def reduce_rows(place, g, t_sib, t_rem):
    _, r, c = g.shape
    r2 = r // 2
    nt = 2 if r2 * c * 4 > 2**20 else 1
    tr = r2 // nt

    def body(place_ref, g_ref, s_ref, t_ref, f_ref):
        acc = g_ref[...] + s_ref[...]
        for j in range(N_REMOTE):
            acc = acc + t_ref[j].astype(F32)
        f_ref[...] = acc

    return pl.pallas_call(
        body, out_shape=_sds((r, c), F32),
        grid_spec=pltpu.PrefetchScalarGridSpec(
            num_scalar_prefetch=1, grid=(nt,),
            in_specs=[pl.BlockSpec((None, tr, c), lambda i, p: (p[0], p[1] * nt + i, 0)),
                      pl.BlockSpec((tr, c), lambda i, p: (i, 0)),
                      pl.BlockSpec((N_REMOTE, tr, c), lambda i, p: (0, i, 0))],
            out_specs=pl.BlockSpec((tr, c), lambda i, p: (p[1] * nt + i, 0))),
        compiler_params=_cp("parallel"), name="reduce_rows")(place, g, t_sib, t_rem)


_ANY = pl.BlockSpec(memory_space=pl.ANY)


def _place():
    x, y, c = lax.axis_index("x"), lax.axis_index("y"), lax.axis_index("c")
    chips = [(1 - x, y), (x, 1 - y), (1 - x, 1 - y)]
    return x, y, c, chips


def gather_sems(n):
    return [pltpu.SemaphoreType.DMA((n, N_REMOTE)), pltpu.SemaphoreType.DMA((n, N_REMOTE))]


def gather_stages(shapes, ins, outs, send, recv):
    n = len(shapes)
    x, y, c, chips = _place()
    me = 2 * x + y
    sib = (x, y, 1 - c)

    def rows(a, hl):
        r2 = shapes[a][0] // 2
        return pl.ds(hl * r2, r2)

    def remote(a, k, src, dst, dev):
        return pltpu.make_async_remote_copy(src, dst, send.at[a, k], recv.at[a, k], device_id=dev, device_id_type=MESH)

    def sent(a, k):
        cx, cy = chips[k]
        return remote(a, k, ins[a].at[rows(a, c)], outs[a].at[me, rows(a, c)], (cx, cy, c))

    def passed(a, k, hl):
        cx, cy = chips[k]
        got = outs[a].at[2 * cx + cy, rows(a, hl)]
        return remote(a, k, got, got, (cx, cy, c)), remote(a, 3 + k, got, got, sib)

    def start():
        for a in range(n):
            for k in range(3):
                sent(a, k).start()

    def forward():
        for k in range(3):
            for a in range(n):
                arrived, on = passed(a, k, c)
                arrived.wait_recv()
                on.start()

    def finish():
        for k in range(3):
            for a in range(n):
                passed(a, k, 1 - c)[1].wait_recv()
        for k in range(3):
            for a in range(n):
                sent(a, k).wait_send()
                passed(a, k, c)[1].wait_send()

    return start, forward, finish


def allgather_layer(shards):
    n = len(shards)

    def body(*refs):
        for stage in gather_stages([a.shape for a in shards], refs[:n], refs[n:2 * n], *refs[2 * n:]):
            stage()

    return pl.pallas_call(
        body, out_shape=tuple(_sds((N_CHIPS,) + a.shape, a.dtype) for a in shards),
        in_specs=[_ANY] * n, out_specs=(_ANY,) * n, scratch_shapes=gather_sems(n), name="allgather_layer")(*shards)


def scatter_sems(n):
    return [pltpu.SemaphoreType.DMA((n, N_REMOTE + 1)), pltpu.SemaphoreType.DMA((n, N_REMOTE + 1))]


def scatter_out_shapes(gs):
    return (tuple(_sds((g.shape[1] // 2, g.shape[2]), F32) for g in gs)
            + tuple(_sds((N_REMOTE, g.shape[1] // 2, g.shape[2]), BF16) for g in gs))


def scatter_stages(shapes, gf, gb, t_sib, t_rem, send, recv):
    n = len(shapes)
    x, y, c, chips = _place()
    me = 2 * x + y

    def copies():
        out = []
        for a in range(n):
            r2 = shapes[a][0] // 2
            out.append(pltpu.make_async_remote_copy(gf[a].at[me, pl.ds((1 - c) * r2, r2)], t_sib[a], send.at[a, N_REMOTE],
                                                    recv.at[a, N_REMOTE], device_id=(x, y, 1 - c), device_id_type=MESH))
            for k, (cx, cy) in enumerate(chips):
                for o in range(2):
                    tc = c if o == 0 else 1 - c
                    out.append(pltpu.make_async_remote_copy(gb[a].at[2 * cx + cy, pl.ds(tc * r2, r2)], t_rem[a].at[2 * k + o],
                                                            send.at[a, 2 * k + o], recv.at[a, 2 * k + o],
                                                            device_id=(cx, cy, tc), device_id_type=MESH))
        return out

    def start():
        for cp in copies():
            cp.start()

    def finish():
        for cp in copies():
            cp.wait()

    return start, finish


def scatter_layer(grads):
    n = len(grads) // 2

    def body(*refs):
        start, finish = scatter_stages([g.shape[1:] for g in grads[:n]], refs[:n], refs[n:2 * n], refs[2 * n:3 * n],
                                       refs[3 * n:4 * n], *refs[4 * n:])
        start()
        finish()

    out = pl.pallas_call(
        body, out_shape=scatter_out_shapes(grads[:n]), in_specs=[_ANY] * (2 * n), out_specs=(_ANY,) * (2 * n),
        scratch_shapes=scatter_sems(n), name="scatter_layer")(*grads)
    return list(out[:n]), list(out[n:])


def share_final(fs):
    n = len(fs)

    def body(*refs):
        out = refs[n:2 * n]
        send, recv = refs[2 * n:]
        x, y, c, _ = _place()
        sib = (x, y, 1 - c)
        cps = []
        for a in range(n):
            r2 = fs[a].shape[0] // 2
            mine = out[a].at[pl.ds(c * r2, r2)]
            cp = pltpu.make_async_remote_copy(mine, mine, send.at[a], recv.at[a], device_id=sib, device_id_type=MESH)
            cp.start()
            cps.append(cp)
        for a, cp in enumerate(cps):
            r2 = fs[a].shape[0] // 2
            theirs = out[a].at[pl.ds((1 - c) * r2, r2)]
            cp.wait_send()
            pltpu.make_async_remote_copy(theirs, theirs, send.at[a], recv.at[a], device_id=sib, device_id_type=MESH).wait_recv()

    return pl.pallas_call(
        body, out_shape=tuple(_sds(f.shape, F32) for f in fs),
        in_specs=[_ANY] * n, out_specs=(_ANY,) * n, input_output_aliases={a: a for a in range(n)},
        scratch_shapes=[pltpu.SemaphoreType.DMA((n,)), pltpu.SemaphoreType.DMA((n,))],
        name="share_final")(*fs)


def allreduce_small(v):
    r, w = v.shape
    ndev = 2 * N_CHIPS

    def body(v_ref, sum_ref, all_ref, send, recv, loc):
        x, y, c, chips = _place()
        me, sib = (x, y, c), (x, y, 1 - c)

        def slab(px, py, pc):
            return all_ref.at[4 * px + 2 * py + pc]

        def copy(k, block, to, src=None):
            return pltpu.make_async_remote_copy(slab(*block) if src is None else src, slab(*block), send.at[k], recv.at[k],
                                                device_id=to, device_id_type=MESH)

        mine = pltpu.make_async_copy(v_ref, slab(*me), loc)
        mine.start()
        first = [copy(0, me, sib, src=v_ref)] + [copy(1 + j, me, (*chip, c), src=v_ref) for j, chip in enumerate(chips)]
        for cp in first:
            cp.start()
        passed = [copy(4 + j, (*chip, c), sib) for j, chip in enumerate(chips)]
        for j, chip in enumerate(chips):
            copy(1 + j, (*chip, c), me).wait_recv()
            passed[j].start()
        copy(0, sib, me).wait_recv()
        for j, chip in enumerate(chips):
            copy(4 + j, (*chip, 1 - c), me).wait_recv()
        for cp in first + passed:
            cp.wait_send()
        mine.wait()
        acc = all_ref[0]
        for i in range(1, ndev):
            acc = acc + all_ref[i]
        sum_ref[...] = acc

    vm = pl.BlockSpec(memory_space=pltpu.VMEM)
    return pl.pallas_call(
        body, out_shape=_sds((r, w), F32), in_specs=[vm], out_specs=vm,
        scratch_shapes=[pltpu.VMEM((ndev, r, w), F32), pltpu.SemaphoreType.DMA((7,)), pltpu.SemaphoreType.DMA((7,)),
                        pltpu.SemaphoreType.DMA],
        compiler_params=pltpu.CompilerParams(vmem_limit_bytes=VMEM_LIMIT), name="allreduce_small")(v)


_SMALL = ("norm_g", "q_norm_g", "k_norm_g", "sg_ln_g", "sg_ln_b", "w_s", "b_s", "mem_norm_g", "final_g")
_WEIGHTS = ("norm_g", "w_in", "q_norm_g", "k_norm_g", "sg_ln_g", "sg_ln_b", "w_s", "b_s", "mem_norm_g", "w_mem_kv", "w_br",
            "w_out", "final_g")


def _pack(d):
    flat = jnp.concatenate([d[n].reshape(-1) for n in _SMALL])
    rows = -(-flat.shape[0] // (8 * LANES)) * 8
    return jnp.pad(flat, (0, rows * LANES - flat.shape[0])).reshape(rows, LANES)


def _unpack(p, like):
    flat, out, o = p.reshape(-1), {}, 0
    for n in _SMALL:
        out[n] = flat[o:o + like[n].size].reshape(like[n].shape)
        o += like[n].size
    return out


def kernel(x, mem, norm_g, w_in, q_norm_g, k_norm_g, sg_ln_g, sg_ln_b, w_s, b_s, mem_norm_g, w_mem_kv, w_br, w_out, final_g, loss_target, m_norm_g, m_w_in, m_q_norm_g, m_k_norm_g, m_sg_ln_g, m_sg_ln_b, m_w_s, m_b_s, m_mem_norm_g, m_w_mem_kv, m_w_br, m_w_out, m_final_g, v_norm_g, v_w_in, v_q_norm_g, v_k_norm_g, v_sg_ln_g, v_sg_ln_b, v_w_s, v_b_s, v_mem_norm_g, v_w_mem_kv, v_w_br, v_w_out, v_final_g):
    w = dict(norm_g=norm_g, w_in=w_in, q_norm_g=q_norm_g, k_norm_g=k_norm_g, sg_ln_g=sg_ln_g, sg_ln_b=sg_ln_b, w_s=w_s, b_s=b_s,
             mem_norm_g=mem_norm_g, w_mem_kv=w_mem_kv, w_br=w_br, w_out=w_out, final_g=final_g)
    m = dict(norm_g=m_norm_g, w_in=m_w_in, q_norm_g=m_q_norm_g, k_norm_g=m_k_norm_g, sg_ln_g=m_sg_ln_g, sg_ln_b=m_sg_ln_b,
             w_s=m_w_s, b_s=m_b_s, mem_norm_g=m_mem_norm_g, w_mem_kv=m_w_mem_kv, w_br=m_w_br, w_out=m_w_out, final_g=m_final_g)
    v = dict(norm_g=v_norm_g, w_in=v_w_in, q_norm_g=v_q_norm_g, k_norm_g=v_k_norm_g, sg_ln_g=v_sg_ln_g, sg_ln_b=v_sg_ln_b,
             w_s=v_w_s, b_s=v_b_s, mem_norm_g=v_mem_norm_g, w_mem_kv=v_w_mem_kv, w_br=v_w_br, w_out=v_w_out, final_g=v_final_g)
    depth, d = norm_g.shape
    nsh = N_CHIPS
    br_rows = N_BRANCH * A_WIDTH
    br_cols = d // nsh

    shards = [[jnp.swapaxes(w_in[l], 0, 1).astype(BF16), w_mem_kv[l].astype(BF16), w_br[l].astype(BF16).reshape(br_rows, br_cols),
               w_out[l].astype(BF16)] for l in range(depth)]
    place = jnp.stack([2 * lax.axis_index("x") + lax.axis_index("y"), lax.axis_index("c")]).astype(jnp.int32)
    small = {n: w[n] for n in _SMALL}

    sq, dx, grads, reduced = local_fwd_bwd(x[0], mem[0], loss_target[0], small, shards=shards, place=place)
    loss = (0.5 / d) * lax.psum(sq, ("x", "y", "c"))

    finals = share_final([g for layer in reduced for g in layer])
    finals = [jnp.stack(finals[a::len(_BIG)]) for a in range(len(_BIG))]
    big_grads = dict(w_in=jnp.swapaxes(finals[0], 1, 2), w_mem_kv=finals[1],
                     w_br=finals[2].reshape(depth, N_BRANCH, A_WIDTH, br_cols), w_out=finals[3])

    small_grads = _unpack(allreduce_small(_pack(grads)), small)

    out_g, out_d, out_m, out_v = {}, {}, {}, {}
    sd, sm, sv = adamw(_pack(small), _pack(small_grads), _pack({n: m[n] for n in _SMALL}), _pack({n: v[n] for n in _SMALL}))
    sd, sm, sv = _unpack(sd, small), _unpack(sm, small), _unpack(sv, small)
    for n in _SMALL:
        out_g[n], out_d[n], out_m[n], out_v[n] = small_grads[n], sd[n], sm[n], sv[n]
    for n, g in big_grads.items():
        two_d = (-1, w[n].shape[-1])
        dd, mm, vv = adamw(w[n].reshape(two_d), g.reshape(two_d), m[n].reshape(two_d), v[n].reshape(two_d))
        out_g[n], out_d[n], out_m[n], out_v[n] = g, dd.reshape(w[n].shape), mm.reshape(w[n].shape), vv.reshape(w[n].shape)
    return (loss, dx[None], *[out_g[n] for n in _WEIGHTS], *[out_d[n] for n in _WEIGHTS], *[out_m[n] for n in _WEIGHTS],
            *[out_v[n] for n in _WEIGHTS])
```

```python
import functools

import jax
import jax.numpy as jnp
from jax import lax
from jax.experimental import pallas as pl
from jax.experimental.pallas import tpu as pltpu

F32 = jnp.float32
BF16 = jnp.bfloat16

D_MODEL = 1024
GRID_W = 64
CHUNK = 128
ROPE_THETA = 10000.0
EPS = 1e-6
A_HEADS, A_KV_HEADS, A_HEAD_DIM = 8, 2, 64
A_WIDTH, A_KV_WIDTH = 512, 128
B_GROUPS, B_GROUP_DIM, B_WIDTH = 4, 128, 512
M_HEADS, M_HEAD_DIM, M_WIDTH = 4, 128, 512
N_BRANCH = 3
IN_WIDTH = 6912
O_QA, O_KA, O_VA, O_ZA, O_UB, O_VB, O_ZB, O_QM, O_ZM, O_LG = 0, 512, 640, 768, 1280, 1792, 2304, 2816, 3328, 3840
PBLK = 768
N_PBLK = IN_WIDTH // PBLK
MID_W = 3072
LG_W = 3072

LN2 = 0.6931471805599453
Q_SCALE = A_HEAD_DIM ** -0.5 / LN2

ADAM_LR, ADAM_B1, ADAM_B2, ADAM_EPS, ADAM_WD, ADAM_STEP = 0.001, 0.9, 0.999, 1e-08, 0.01, 10

V7X_VMEM_BYTES = 64 * 2**20
VMEM_LIMIT = V7X_VMEM_BYTES - 8 * 2**20
LANES = 128
MESH = pl.DeviceIdType.MESH
N_CHIPS = 4


def _cp(*sem):
    return pltpu.CompilerParams(dimension_semantics=sem if sem else None, vmem_limit_bytes=VMEM_LIMIT)


def _dot(a, b):
    return jnp.dot(a, b, preferred_element_type=F32)


def _dot_nt(a, b):
    return lax.dot_general(a, b, (((1,), (1,)), ((), ())), preferred_element_type=F32)


def _dot_tn(a, b):
    return lax.dot_general(a, b, (((0,), (0,)), ((), ())), preferred_element_type=F32)


def _dot_hi(a, b):
    return jnp.dot(a, b, preferred_element_type=F32, precision=lax.Precision.HIGHEST)


def _group_sum(a, ones):
    hi = a.astype(BF16)
    lo = (a - hi.astype(F32)).astype(BF16)
    return _dot(hi, ones) + _dot(lo, ones)


def _dot_nt_hi(a, b):
    return lax.dot_general(a, b, (((1,), (1,)), ((), ())), preferred_element_type=F32, precision=lax.Precision.HIGHEST)


def _sig(z):
    return 1.0 / (1.0 + jnp.exp(-z))


def _full(shape):
    nd = len(shape)
    return pl.BlockSpec(shape, lambda *_: (0,) * nd)


def _rows(tm, width):
    return pl.BlockSpec((tm, width), lambda i: (i, 0))


def _sds(shape, dtype):
    return jax.ShapeDtypeStruct(shape, dtype)


def rms_fwd(x, g):
    s, d = x.shape
    tm = min(s, 512)

    def body(x_ref, g_ref, h_ref):
        xf = x_ref[...]
        r = lax.rsqrt(jnp.mean(xf * xf, axis=-1, keepdims=True) + EPS)
        h_ref[...] = ((xf * r) * g_ref[...]).astype(BF16)

    return pl.pallas_call(
        body, out_shape=_sds((s, d), BF16), grid=(s // tm,),
        in_specs=[_rows(tm, d), _full((1, d))], out_specs=_rows(tm, d),
        compiler_params=_cp("parallel"), name="rms_fwd")(x, g)


def proj_fwd(h, w_t):
    s, d = h.shape
    n = w_t.shape[0]
    tm = min(s, 512)
    tn = 2304

    def body(h_ref, w_ref, o_ref):
        o_ref[...] = _dot_nt(h_ref[...], w_ref[...])

    return pl.pallas_call(
        body, out_shape=_sds((s, n), F32), grid=(n // tn, s // tm),
        in_specs=[pl.BlockSpec((tm, d), lambda j, i: (i, 0)), pl.BlockSpec((tn, d), lambda j, i: (j, 0))],
        out_specs=pl.BlockSpec((tm, tn), lambda j, i: (i, j)),
        compiler_params=_cp("parallel", "parallel"), name="proj_fwd")(h, w_t)


def rope_tables(seq):
    rows = seq // GRID_W
    row = jnp.repeat(jnp.arange(rows, dtype=F32), GRID_W)
    col = jnp.tile(jnp.arange(GRID_W, dtype=F32), rows)
    n_freq = A_HEAD_DIM // 4
    inv = ROPE_THETA ** (-jnp.arange(n_freq, dtype=F32) / n_freq)
    ang = jnp.stack([row[:, None] * inv, col[:, None] * inv], axis=1)
    cos, sin = jnp.cos(ang), jnp.sin(ang)
    zero = jnp.zeros_like(sin[:, 0])
    c64 = jnp.concatenate([cos[:, 0], cos[:, 0], cos[:, 1], cos[:, 1]], axis=1)
    sa64 = jnp.concatenate([zero, sin[:, 0], zero, sin[:, 1]], axis=1)
    sb64 = jnp.concatenate([-sin[:, 0], zero, -sin[:, 1], zero], axis=1)
    two = lambda t: jnp.concatenate([t, t], axis=1)
    return two(c64), two(sa64), two(sb64)


def _group_ones(width, group):
    i = jnp.arange(width)
    return (i[:, None] // group == i[None, :] // group).astype(F32)


def _rope(xn, c, sa, sb):
    w = xn.shape[1]
    return xn * c + pltpu.roll(xn, 16, 1) * sa + pltpu.roll(xn, w - 16, 1) * sb


def _rope_t(dy, c, sa, sb):
    w = dy.shape[1]
    return dy * c + pltpu.roll(dy * sa, w - 16, 1) + pltpu.roll(dy * sb, 16, 1)


def _tile4(t):
    return jnp.concatenate([t, t, t, t], axis=1)


def qk_prep(proj, tabs, qg, kg, gq, gk):
    s = proj.shape[0]
    tm = min(s, 512)
    c, sa, sb = tabs

    def body(p_ref, c_ref, sa_ref, sb_ref, qg_ref, kg_ref, gq_ref, gk_ref, qt_ref, kr_ref, krt_ref, vb_ref, v0_ref, v1_ref):
        xq = p_ref[:, O_QA:O_QA + A_WIDTH]
        xk = p_ref[:, O_KA:O_KA + A_KV_WIDTH]
        xv = p_ref[:, O_VA:O_VA + A_KV_WIDTH]
        cc, ssa, ssb = c_ref[...], sa_ref[...], sb_ref[...]
        msq = _group_sum(xq * xq, gq_ref[...]) * (1.0 / A_HEAD_DIM)
        qn = (xq * lax.rsqrt(msq + EPS)) * qg_ref[...]
        qr = _rope(qn, _tile4(cc), _tile4(ssa), _tile4(ssb)) * Q_SCALE
        qt_ref[...] = qr.T.astype(BF16)
        msk = _group_sum(xk * xk, gk_ref[...]) * (1.0 / A_HEAD_DIM)
        kn = (xk * lax.rsqrt(msk + EPS)) * kg_ref[...]
        kr = _rope(kn, cc, ssa, ssb)
        kr_ref[...] = kr.astype(BF16)
        krt_ref[...] = kr.T.astype(BF16)
        vb_ref[...] = xv.astype(BF16)
        vt = xv.T.astype(BF16)
        one = jnp.ones((A_HEAD_DIM, tm), BF16)
        v0_ref[...] = jnp.concatenate([vt[:A_HEAD_DIM], one], axis=0)
        v1_ref[...] = jnp.concatenate([one, vt[A_HEAD_DIM:]], axis=0)

    tab = _rows(tm, LANES)
    colb = lambda w: pl.BlockSpec((w, tm), lambda i: (0, i))
    return pl.pallas_call(
        body,
        out_shape=(_sds((A_WIDTH, s), BF16), _sds((s, A_KV_WIDTH), BF16), _sds((A_KV_WIDTH, s), BF16),
                   _sds((s, A_KV_WIDTH), BF16), _sds((A_KV_WIDTH, s), BF16), _sds((A_KV_WIDTH, s), BF16)),
        grid=(s // tm,),
        in_specs=[_rows(tm, PBLK), tab, tab, tab, _full((1, A_WIDTH)), _full((1, A_KV_WIDTH)),
                  _full((A_WIDTH, A_WIDTH)), _full((A_KV_WIDTH, A_KV_WIDTH))],
        out_specs=(colb(A_WIDTH), _rows(tm, A_KV_WIDTH), colb(A_KV_WIDTH), _rows(tm, A_KV_WIDTH), colb(A_KV_WIDTH),
                   colb(A_KV_WIDTH)),
        compiler_params=_cp("parallel"), name="qk_prep")(proj, c, sa, sb, qg, kg, gq, gk)


def _pad_head(q_h, kv):
    z = jnp.zeros_like(q_h)
    return jnp.concatenate([q_h, z], axis=0) if kv == 0 else jnp.concatenate([z, q_h], axis=0)


def attn_fwd(q_t, kr, vte0, vte1, gather=()):
    s = kr.shape[0]
    tq = min(s, 256)
    kc = min(s, 512)
    nkc = s // kc
    nq = s // tq
    grp = A_HEADS // A_KV_HEADS
    ng = len(gather)

    def body(qt_ref, kr_ref, v0_ref, v1_ref, *rest):
        g_in, (o_ref, lse_ref), g_out = rest[:ng], rest[ng:ng + 2], rest[ng + 2:2 * ng + 2]
        qp_ref, m_ref, acc_ref = rest[2 * ng + 2:2 * ng + 5]
        if ng:
            start, forward, finish = gather_stages([g.shape for g in gather], g_in, g_out, *rest[2 * ng + 5:])
            pl.when(pl.program_id(0) == 0)(start)
            pl.when(pl.program_id(0) == (3 * nq) // 4)(forward)

        for h in range(A_HEADS):
            qp_ref[h] = _pad_head(qt_ref[A_HEAD_DIM * h:A_HEAD_DIM * (h + 1), :], h // grp)
        m_ref[...] = jnp.full(m_ref.shape, -1e30, F32)
        acc_ref[...] = jnp.zeros_like(acc_ref)

        def step(ci, carry):
            ks = pl.ds(pl.multiple_of(ci * kc, kc), kc)
            kblk = kr_ref[ks, :]
            vts = (v0_ref[:, ks], v1_ref[:, ks])
            scs = [_dot(kblk, qp_ref[h]) for h in range(A_HEADS)]
            for h in range(A_HEADS):
                sc = scs[h]
                m_prev = m_ref[h:h + 1, :]
                m_new = jnp.maximum(m_prev, jnp.max(sc, axis=0, keepdims=True))
                p = jnp.exp2(sc - m_new)
                acc_ref[h] = acc_ref[h] * jnp.exp2(m_prev - m_new) + _dot(vts[h // grp], p.astype(BF16))
                m_ref[h:h + 1, :] = m_new
            return carry

        lax.fori_loop(0, nkc, step, 0)
        outs, lses = [], []
        for h in range(A_HEADS):
            kv = h // grp
            acc = acc_ref[h]
            l = acc[A_HEAD_DIM * (1 - kv):A_HEAD_DIM * (1 - kv) + 1, :]
            outs.append(acc[A_HEAD_DIM * kv:A_HEAD_DIM * (kv + 1), :] / l)
            lses.append(m_ref[h:h + 1, :] + jnp.log2(l))
        o_ref[...] = jnp.concatenate(outs, axis=0).T
        lse_ref[...] = jnp.concatenate(lses, axis=0)
        if ng:
            pl.when(pl.program_id(0) == nq - 1)(finish)

    out = pl.pallas_call(
        body,
        out_shape=(_sds((s, A_WIDTH), F32), _sds((A_HEADS, s), F32)) + tuple(_sds((N_CHIPS,) + g.shape, g.dtype) for g in gather),
        grid=(nq,),
        in_specs=[pl.BlockSpec((A_WIDTH, tq), lambda i: (0, i)), _full((s, A_KV_WIDTH)), _full((A_KV_WIDTH, s)),
                  _full((A_KV_WIDTH, s))] + [_ANY] * ng,
        out_specs=(_rows(tq, A_WIDTH), pl.BlockSpec((A_HEADS, tq), lambda i: (0, i))) + (_ANY,) * ng,
        scratch_shapes=[pltpu.VMEM((A_HEADS, A_KV_WIDTH, tq), BF16), pltpu.VMEM((A_HEADS, tq), F32),
                        pltpu.VMEM((A_HEADS, A_KV_WIDTH, tq), F32)] + (gather_sems(ng) if ng else []),
        compiler_params=_cp("arbitrary"), name="attn_fwd_gather" if ng else "attn_fwd")(q_t, kr, vte0, vte1, *gather)
    return out[0], out[1], list(out[2:])


def memkv_fwd(mem, g, w_kv):
    m, d = mem.shape

    def body(mem_ref, g_ref, w_ref, mn_ref, kv_ref):
        mf = mem_ref[...]
        r = lax.rsqrt(jnp.mean(mf * mf, axis=-1, keepdims=True) + EPS)
        mn = ((mf * r) * g_ref[...]).astype(BF16)
        mn_ref[...] = mn
        kv_ref[...] = _dot(mn, w_ref[...]).astype(BF16)

    return pl.pallas_call(
        body, out_shape=(_sds((m, d), BF16), _sds((m, 2 * M_WIDTH), BF16)),
        compiler_params=_cp(), name="memkv_fwd")(mem, g, w_kv)


def _layer_norm_stats(v):
    mu = jnp.mean(v, axis=-1, keepdims=True)
    xc = v - mu
    rstd = lax.rsqrt(jnp.mean(xc * xc, axis=-1, keepdims=True) + EPS)
    return xc * rstd, rstd


def _spatial_mix(vlb, ws_ref, bsb_ref, tm):
    rows = []
    for ci in range(tm // CHUNK):
        cols = []
        for g in range(B_GROUPS):
            blk = vlb[ci * CHUNK:(ci + 1) * CHUNK, g * B_GROUP_DIM:(g + 1) * B_GROUP_DIM]
            cols.append(_dot(ws_ref[g], blk) + bsb_ref[g])
        rows.append(jnp.concatenate(cols, axis=1))
    return jnp.concatenate(rows, axis=0)


def _mem_attn(qm, kv_ref):
    out = []
    for h in range(M_HEADS):
        qh = qm[:, h * M_HEAD_DIM:(h + 1) * M_HEAD_DIM].astype(BF16)
        kh = kv_ref[:, h * M_HEAD_DIM:(h + 1) * M_HEAD_DIM]
        vh = kv_ref[:, M_WIDTH + h * M_HEAD_DIM:M_WIDTH + (h + 1) * M_HEAD_DIM]
        sc = _dot_nt(qh, kh) * (M_HEAD_DIM ** -0.5)
        e = jnp.exp(sc - jnp.max(sc, axis=-1, keepdims=True))
        p = e / jnp.sum(e, axis=-1, keepdims=True)
        out.append((p, _dot(p.astype(BF16), vh)))
    return out


def branch_fwd(x, proj, o_a, kv, ws, bsb, ln_g, ln_b, w_br, w_out):
    s, d = x.shape
    tm = min(s, 256)

    def body(x_ref, p_ref, oa_ref, kv_ref, ws_ref, bsb_ref, lg_ref, lb_ref, wbr_ref, wo_ref,
             xn_ref, y_ref, up_ref, mg_ref):
        seg = lambda o, w: p_ref[:, o:o + w]
        z_a, u_b, v_b, z_b = seg(O_ZA, A_WIDTH), seg(O_UB, B_WIDTH), seg(O_VB, B_WIDTH), seg(O_ZB, B_WIDTH)
        q_m, z_m = seg(O_QM, M_WIDTH), seg(O_ZM, M_WIDTH)
        xhat, _ = _layer_norm_stats(v_b)
        vln = xhat * lg_ref[...] + lb_ref[...]
        mixed = _spatial_mix(vln.astype(BF16), ws_ref, bsb_ref, tm)
        y_b = (u_b * mixed) * (z_b * _sig(z_b))
        o_m = jnp.concatenate([o for _, o in _mem_attn(q_m, kv_ref)], axis=1)
        y_a = oa_ref[...] * (z_a * _sig(z_a))
        y_m = o_m * (z_m * _sig(z_m))
        merged = None
        for n, yy in enumerate((y_a, y_b, y_m)):
            yb = yy.astype(BF16)
            y_ref[n] = yb
            up = jnp.concatenate([_dot(yb, wbr_ref[c, n]) for c in range(N_CHIPS)], axis=1)
            up_ref[n] = up.astype(BF16)
            t = _sig(seg(O_LG + n * d, d)) * up
            merged = t if merged is None else merged + t
        mb = merged.astype(BF16)
        mg_ref[...] = mb
        xn_ref[...] = x_ref[...] + _dot(mb, wo_ref[...])

    return pl.pallas_call(
        body,
        out_shape=(_sds((s, d), F32), _sds((N_BRANCH, s, A_WIDTH), BF16), _sds((N_BRANCH, s, d), BF16), _sds((s, d), BF16)),
        grid=(s // tm,),
        in_specs=[_rows(tm, d), _rows(tm, IN_WIDTH), _rows(tm, A_WIDTH), _full(kv.shape), _full(ws.shape), _full(bsb.shape),
                  _full((1, B_WIDTH)), _full((1, B_WIDTH)), _full(w_br.shape), _full(w_out.shape)],
        out_specs=(_rows(tm, d), pl.BlockSpec((N_BRANCH, tm, A_WIDTH), lambda i: (0, i, 0)),
                   pl.BlockSpec((N_BRANCH, tm, d), lambda i: (0, i, 0)), _rows(tm, d)),
        compiler_params=_cp("parallel"), name="branch_fwd")(x, proj, o_a, kv, ws, bsb, ln_g, ln_b, w_br, w_out)


def final_loss(x, fg, tgt):
    s, d = x.shape
    tm = min(s, 512)

    def body(x_ref, g_ref, t_ref, ls_ref, dx_ref, gg_ref):
        @pl.when(pl.program_id(0) == 0)
        def _():
            ls_ref[...] = jnp.zeros_like(ls_ref)
            gg_ref[...] = jnp.zeros_like(gg_ref)

        xf = x_ref[...]
        g = g_ref[...]
        r = lax.rsqrt(jnp.mean(xf * xf, axis=-1, keepdims=True) + EPS)
        xh = xf * r
        e = xh * g - t_ref[...]
        sq = jnp.sum(jnp.sum(e * e, axis=0, keepdims=True), axis=1, keepdims=True)
        ls_ref[...] += jnp.broadcast_to(sq, ls_ref.shape)
        dy = e * (1.0 / d)
        gg_ref[...] += jnp.sum(dy * xh, axis=0, keepdims=True)
        gy = dy * g
        dx_ref[...] = r * (gy - xh * jnp.mean(gy * xh, axis=-1, keepdims=True))

    return pl.pallas_call(
        body, out_shape=(_sds((1, LANES), F32), _sds((s, d), F32), _sds((1, d), F32)), grid=(s // tm,),
        in_specs=[_rows(tm, d), _full((1, d)), _rows(tm, d)],
        out_specs=(_full((1, LANES)), _rows(tm, d), _full((1, d))),
        compiler_params=_cp("arbitrary"), name="final_loss")(x, fg, tgt)


def _pblocks(tm, first, count):
    return [pl.BlockSpec((tm, PBLK), functools.partial(lambda i, b: (i, b), b=first + k)) for k in range(count)]


def merge_bwd(dx, proj, y, up, merged, w_br, w_out):
    s, d = dx.shape
    tm = min(s, 256)
    nlg = LG_W // PBLK
    cw = d // N_CHIPS

    def body(dx_ref, l0, l1, l2, l3, y_ref, up_ref, mg_ref, wbr_ref, wo_ref, dy_ref, dlg_ref, gwo_ref, gwb_ref, gwo16_ref, gwb16_ref):
        @pl.when(pl.program_id(0) == 0)
        def _():
            gwo_ref[...] = jnp.zeros_like(gwo_ref)
            gwb_ref[...] = jnp.zeros_like(gwb_ref)

        dxb = dx_ref[...].astype(BF16)
        dmg = _dot_nt(dxb, wo_ref[...])
        gwo_ref[...] += _dot_tn(mg_ref[...], dxb)
        lg = jnp.concatenate([l0[...], l1[...], l2[...], l3[...]], axis=1)
        for n in range(N_BRANCH):
            g = _sig(lg[:, n * d:(n + 1) * d])
            dup = dmg * g
            dlg_ref[:, n * d:(n + 1) * d] = ((dup * up_ref[n].astype(F32)) * (1.0 - g)).astype(BF16)
            dupb = dup.astype(BF16)
            dyn = None
            for c in range(N_CHIPS):
                blk = dupb[:, c * cw:(c + 1) * cw]
                gwb_ref[c, n] += _dot_tn(y_ref[n], blk)
                t = _dot_nt(blk, wbr_ref[c, n])
                dyn = t if dyn is None else dyn + t
            dy_ref[n] = dyn

        @pl.when(pl.program_id(0) == pl.num_programs(0) - 1)
        def _():
            gwo16_ref[...] = gwo_ref[...].astype(BF16)
            gwb16_ref[...] = gwb_ref[...].astype(BF16)

    return pl.pallas_call(
        body,
        out_shape=(_sds((N_BRANCH, s, A_WIDTH), F32), _sds((s, LG_W), BF16), _sds((d, d), F32), _sds(w_br.shape, F32),
                   _sds((d, d), BF16), _sds(w_br.shape, BF16)),
        grid=(s // tm,),
        in_specs=[_rows(tm, d)] + _pblocks(tm, O_LG // PBLK, nlg) + [
            pl.BlockSpec((N_BRANCH, tm, A_WIDTH), lambda i: (0, i, 0)), pl.BlockSpec((N_BRANCH, tm, d), lambda i: (0, i, 0)),
            _rows(tm, d), _full(w_br.shape), _full(w_out.shape)],
        out_specs=(pl.BlockSpec((N_BRANCH, tm, A_WIDTH), lambda i: (0, i, 0)), _rows(tm, LG_W), _full((d, d)), _full(w_br.shape),
                   _full((d, d)), _full(w_br.shape)),
        compiler_params=_cp("arbitrary"), name="merge_bwd")(dx, proj, proj, proj, proj, y, up, merged, w_br, w_out)


def _dsilu(z, sg):
    return sg * (1.0 + z * (1.0 - sg))


def branch_bwd(dy, proj, o_a, kv, ws, ws_t, bsb, ln_g, ln_b, head_sel):
    s = proj.shape[0]
    tm = min(s, 256)
    nmid = MID_W // PBLK

    def body(dy_ref, m0, m1, m2, m3, oa_ref, kv_ref, ws_ref, wst_ref, bsb_ref, lg_ref, lb_ref, sel_ref,
             dmid_ref, dot_ref, dl_ref, gws_ref, gbs_ref, glg_ref, glb_ref, dkv_ref):
        @pl.when(pl.program_id(0) == 0)
        def _():
            for r in (gws_ref, gbs_ref, glg_ref, glb_ref, dkv_ref):
                r[...] = jnp.zeros_like(r)

        mid = jnp.concatenate([m0[...], m1[...], m2[...], m3[...]], axis=1)
        seg = lambda o, w: mid[:, o - O_ZA:o - O_ZA + w]
        z_a, u_b, v_b, z_b = seg(O_ZA, A_WIDTH), seg(O_UB, B_WIDTH), seg(O_VB, B_WIDTH), seg(O_ZB, B_WIDTH)
        q_m, z_m = seg(O_QM, M_WIDTH), seg(O_ZM, M_WIDTH)

        def put(o, v):
            dmid_ref[:, o - O_ZA:o - O_ZA + v.shape[1]] = v.astype(BF16)

        dy_a, dy_b, dy_m = dy_ref[0], dy_ref[1], dy_ref[2]

        o_a_ = oa_ref[...]
        sg = _sig(z_a)
        do_a = dy_a * (z_a * sg)
        put(O_ZA, (dy_a * o_a_) * _dsilu(z_a, sg))
        do_l = do_a * LN2
        dot_ref[...] = do_l.T.astype(BF16)
        dl_ref[...] = _dot_nt_hi(sel_ref[...], do_l * o_a_)

        xhat, rstd = _layer_norm_stats(v_b)
        lng = lg_ref[...]
        vln = xhat * lng + lb_ref[...]
        vlb = vln.astype(BF16)
        mixed = _spatial_mix(vlb, ws_ref, bsb_ref, tm)
        sg = _sig(z_b)
        sl = z_b * sg
        put(O_UB, (dy_b * mixed) * sl)
        put(O_ZB, ((dy_b * u_b) * mixed) * _dsilu(z_b, sg))
        dmix = (dy_b * u_b) * sl
        dmb = dmix.astype(BF16)
        rows = []
        for ci in range(tm // CHUNK):
            cols = []
            for g in range(B_GROUPS):
                rs, cs = slice(ci * CHUNK, (ci + 1) * CHUNK), slice(g * B_GROUP_DIM, (g + 1) * B_GROUP_DIM)
                gws_ref[g] += _dot_nt(dmb[rs, cs], vlb[rs, cs])
                gbs_ref[g] += jnp.broadcast_to(jnp.sum(dmix[rs, cs], axis=1, keepdims=True), (CHUNK, B_GROUP_DIM))
                cols.append(_dot(wst_ref[g], dmb[rs, cs]))
            rows.append(jnp.concatenate(cols, axis=1))
        dvln = jnp.concatenate(rows, axis=0)
        glg_ref[...] += jnp.sum(dvln * xhat, axis=0, keepdims=True)
        glb_ref[...] += jnp.sum(dvln, axis=0, keepdims=True)
        gy = dvln * lng
        put(O_VB, rstd * ((gy - jnp.mean(gy, axis=-1, keepdims=True)) - xhat * jnp.mean(gy * xhat, axis=-1, keepdims=True)))

        sg = _sig(z_m)
        sl = z_m * sg
        heads = _mem_attn(q_m, kv_ref)
        o_m = jnp.concatenate([o for _, o in heads], axis=1)
        put(O_ZM, (dy_m * o_m) * _dsilu(z_m, sg))
        do_m = dy_m * sl
        dqs = []
        for h, (p, o_h) in enumerate(heads):
            hs = slice(h * M_HEAD_DIM, (h + 1) * M_HEAD_DIM)
            vs = slice(M_WIDTH + h * M_HEAD_DIM, M_WIDTH + (h + 1) * M_HEAD_DIM)
            do_h = do_m[:, hs]
            dob = do_h.astype(BF16)
            dp = _dot_nt(dob, kv_ref[:, vs])
            dsc = (p * (dp - jnp.sum(do_h * o_h, axis=-1, keepdims=True))) * (M_HEAD_DIM ** -0.5)
            dsb = dsc.astype(BF16)
            dqs.append(_dot(dsb, kv_ref[:, hs]))
            dkv_ref[:, hs] += _dot_tn(dsb, q_m[:, hs].astype(BF16))
            dkv_ref[:, vs] += _dot_tn(p.astype(BF16), dob)
        put(O_QM, jnp.concatenate(dqs, axis=1))

    return pl.pallas_call(
        body,
        out_shape=(_sds((s, MID_W), BF16), _sds((A_WIDTH, s), BF16), _sds((A_HEADS, s), F32), _sds(ws.shape, F32),
                   _sds(ws.shape, F32), _sds((1, B_WIDTH), F32), _sds((1, B_WIDTH), F32), _sds(kv.shape, F32)),
        grid=(s // tm,),
        in_specs=[pl.BlockSpec((N_BRANCH, tm, A_WIDTH), lambda i: (0, i, 0))] + _pblocks(tm, O_ZA // PBLK, nmid) + [
            _rows(tm, A_WIDTH), _full(kv.shape), _full(ws.shape), _full(ws.shape), _full(bsb.shape),
            _full((1, B_WIDTH)), _full((1, B_WIDTH)), _full(head_sel.shape)],
        out_specs=(_rows(tm, MID_W), pl.BlockSpec((A_WIDTH, tm), lambda i: (0, i)), pl.BlockSpec((A_HEADS, tm), lambda i: (0, i)),
                   _full(ws.shape), _full(ws.shape), _full((1, B_WIDTH)), _full((1, B_WIDTH)), _full(kv.shape)),
        compiler_params=_cp("arbitrary"), name="branch_bwd")(dy, proj, proj, proj, proj, o_a, kv, ws, ws_t, bsb, ln_g, ln_b, head_sel)


def attn_bwd(q_t, do_t, kr, kr_t, vb, lse, delta, scatter=()):
    s = kr.shape[0]
    tq = min(s, 256)
    kc = min(s, 512)
    nkc = s // kc
    nq = s // tq
    grp = A_HEADS // A_KV_HEADS
    ns = len(scatter)
    na = ns // 2

    def body(qt_ref, dot_ref, kr_ref, krt_ref, vb_ref, lse_ref, dl_ref, *rest):
        s_in, (dqt_ref, dk_ref, dv_ref), s_out = rest[:ns], rest[ns:ns + 3], rest[ns + 3:2 * ns + 3]
        qp_ref, dop_ref, dq_ref = rest[2 * ns + 3:2 * ns + 6]
        if ns:
            start, finish = scatter_stages([g.shape[1:] for g in scatter[:na]], s_in[:na], s_in[na:], s_out[:na], s_out[na:],
                                           *rest[2 * ns + 6:])
            pl.when(pl.program_id(0) == 0)(start)

        @pl.when(pl.program_id(0) == 0)
        def _():
            dk_ref[...] = jnp.zeros_like(dk_ref)
            dv_ref[...] = jnp.zeros_like(dv_ref)

        for h in range(A_HEADS):
            hs = slice(A_HEAD_DIM * h, A_HEAD_DIM * (h + 1))
            qp_ref[h] = _pad_head(qt_ref[hs, :], h // grp)
            dop_ref[h] = _pad_head(dot_ref[hs, :], h // grp)
        dq_ref[...] = jnp.zeros_like(dq_ref)

        def step(ci, carry):
            ks = pl.ds(pl.multiple_of(ci * kc, kc), kc)
            kblk, vblk, ktb = kr_ref[ks, :], vb_ref[ks, :], krt_ref[:, ks]
            dv_acc = jnp.zeros((kc, A_KV_WIDTH), F32)
            dk_acc = jnp.zeros((kc, A_KV_WIDTH), F32)
            scs = [_dot(kblk, qp_ref[h]) for h in range(A_HEADS)]
            dps = [_dot(vblk, dop_ref[h]) for h in range(A_HEADS)]
            for h in range(A_HEADS):
                qpad, dopad = qp_ref[h], dop_ref[h]
                p = jnp.exp2(scs[h] - lse_ref[h:h + 1, :])
                dsb = (p * (dps[h] - dl_ref[h:h + 1, :])).astype(BF16)
                dv_acc = dv_acc + _dot_nt(p.astype(BF16), dopad)
                dk_acc = dk_acc + _dot_nt(dsb, qpad)
                dq_ref[h] += _dot(ktb, dsb)
            dv_ref[ks, :] += dv_acc
            dk_ref[ks, :] += dk_acc
            return carry

        lax.fori_loop(0, nkc, step, 0)
        dqt_ref[...] = jnp.concatenate(
            [dq_ref[h][A_HEAD_DIM * (h // grp):A_HEAD_DIM * (h // grp + 1), :] for h in range(A_HEADS)], axis=0)
        if ns:
            pl.when(pl.program_id(0) == nq - 1)(finish)

    colq = pl.BlockSpec((A_WIDTH, tq), lambda i: (0, i))
    colh = pl.BlockSpec((A_HEADS, tq), lambda i: (0, i))
    out = pl.pallas_call(
        body,
        out_shape=(_sds((A_WIDTH, s), F32), _sds((s, A_KV_WIDTH), F32), _sds((s, A_KV_WIDTH), F32)) + scatter_out_shapes(scatter[:na]),
        grid=(nq,),
        in_specs=[colq, colq, _full((s, A_KV_WIDTH)), _full((A_KV_WIDTH, s)), _full((s, A_KV_WIDTH)), colh, colh] + [_ANY] * ns,
        out_specs=(colq, _full((s, A_KV_WIDTH)), _full((s, A_KV_WIDTH))) + (_ANY,) * ns,
        scratch_shapes=[pltpu.VMEM((A_HEADS, A_KV_WIDTH, tq), BF16), pltpu.VMEM((A_HEADS, A_KV_WIDTH, tq), BF16),
                        pltpu.VMEM((A_HEADS, A_KV_WIDTH, tq), F32)] + (scatter_sems(na) if ns else []),
        compiler_params=_cp("arbitrary"), name="attn_bwd_scatter" if ns else "attn_bwd")(q_t, do_t, kr, kr_t, vb, lse, delta, *scatter)
    return out[0], out[1], out[2], list(out[3:3 + na]), list(out[3 + na:])


def qk_prep_bwd(proj, dq_t, dkr, dvb, tabs, qg, kg, gq, gk, fold_q, fold_k):
    s = proj.shape[0]
    tm = min(s, 512)
    c, sa, sb = tabs

    def head_norm_bwd(x, dn, gain, gones, fold):
        ms = _group_sum(x * x, gones) * (1.0 / A_HEAD_DIM)
        r = lax.rsqrt(ms + EPS)
        xh = x * r
        gg = _dot_hi(jnp.sum(dn * xh, axis=0, keepdims=True), fold)
        u = dn * gain
        mean_u = _group_sum(u * xh, gones) * (1.0 / A_HEAD_DIM)
        return r * (u - xh * mean_u), gg

    def body(p_ref, dqt_ref, dk_ref, dv_ref, c_ref, sa_ref, sb_ref, qg_ref, kg_ref, gq_ref, gk_ref, fq_ref, fk_ref,
             dqkv_ref, gqg_ref, gkg_ref):
        @pl.when(pl.program_id(0) == 0)
        def _():
            gqg_ref[...] = jnp.zeros_like(gqg_ref)
            gkg_ref[...] = jnp.zeros_like(gkg_ref)

        cc, ssa, ssb = c_ref[...], sa_ref[...], sb_ref[...]
        dqr = dqt_ref[...].T * Q_SCALE
        dqn = _rope_t(dqr, _tile4(cc), _tile4(ssa), _tile4(ssb))
        dxq, gq_ = head_norm_bwd(p_ref[:, O_QA:O_QA + A_WIDTH], dqn, qg_ref[...], gq_ref[...], fq_ref[...])
        dkn = _rope_t(dk_ref[...], cc, ssa, ssb)
        dxk, gk_ = head_norm_bwd(p_ref[:, O_KA:O_KA + A_KV_WIDTH], dkn, kg_ref[...], gk_ref[...], fk_ref[...])
        gqg_ref[...] += gq_
        gkg_ref[...] += gk_
        dqkv_ref[:, O_QA:O_QA + A_WIDTH] = dxq.astype(BF16)
        dqkv_ref[:, O_KA:O_KA + A_KV_WIDTH] = dxk.astype(BF16)
        dqkv_ref[:, O_VA:O_VA + A_KV_WIDTH] = (dv_ref[...] * (1.0 / LN2)).astype(BF16)

    tab = _rows(tm, LANES)
    return pl.pallas_call(
        body, out_shape=(_sds((s, PBLK), BF16), _sds((1, LANES), F32), _sds((1, LANES), F32)), grid=(s // tm,),
        in_specs=[_rows(tm, PBLK), pl.BlockSpec((A_WIDTH, tm), lambda i: (0, i)), _rows(tm, A_KV_WIDTH), _rows(tm, A_KV_WIDTH),
                  tab, tab, tab, _full((1, A_WIDTH)), _full((1, A_KV_WIDTH)), _full((A_WIDTH, A_WIDTH)),
                  _full((A_KV_WIDTH, A_KV_WIDTH)), _full((A_WIDTH, LANES)), _full((A_KV_WIDTH, LANES))],
        out_specs=(_rows(tm, PBLK), _full((1, LANES)), _full((1, LANES))),
        compiler_params=_cp("arbitrary"), name="qk_prep_bwd")(proj, dq_t, dkr, dvb, c, sa, sb, qg, kg, gq, gk, fold_q, fold_k)


def _pick_dproj(b, d0, d1, d2, use):
    first_lg = 1 + MID_W // PBLK

    @pl.when(b == 0)
    def _():
        use(d0[...])

    @pl.when(jnp.logical_and(b >= 1, b < first_lg))
    def _():
        use(d1[...])

    @pl.when(b >= first_lg)
    def _():
        use(d2[...])


def win_grad(d0, d1, d2, h):
    s, d = h.shape
    tk = min(s, 1024)
    nk = s // tk

    def body(d0_ref, d1_ref, d2_ref, h_ref, o_ref, o16_ref):
        @pl.when(pl.program_id(1) == 0)
        def _():
            o_ref[...] = jnp.zeros_like(o_ref)

        def use(blk):
            o_ref[...] += _dot_tn(blk, h_ref[...])

        _pick_dproj(pl.program_id(0), d0_ref, d1_ref, d2_ref, use)

        @pl.when(pl.program_id(1) == nk - 1)
        def _():
            o16_ref[...] = o_ref[...].astype(BF16)

    def spec(first, count):
        def imap(j, k):
            used = jnp.logical_and(j >= first, j < first + count)
            return (jnp.where(used, k, 0), jnp.clip(j - first, 0, count - 1))
        return pl.BlockSpec((tk, PBLK), imap)

    nm = MID_W // PBLK
    oblk = pl.BlockSpec((PBLK, d), lambda j, k: (j, 0))
    return pl.pallas_call(
        body, out_shape=(_sds((IN_WIDTH, d), F32), _sds((IN_WIDTH, d), BF16)), grid=(N_PBLK, nk),
        in_specs=[spec(0, 1), spec(1, nm), spec(1 + nm, LG_W // PBLK), pl.BlockSpec((tk, d), lambda j, k: (k, 0))],
        out_specs=(oblk, oblk),
        compiler_params=_cp("parallel", "arbitrary"), name="win_grad")(d0, d1, d2, h)


def h_bwd(d0, d1, d2, w_t, x, dx_out, g, scatter=()):
    s, d = x.shape
    tm = min(s, 512)
    nt = s // tm
    ns = len(scatter)
    na = ns // 2

    def body(d0_ref, d1_ref, d2_ref, w_ref, x_ref, dxo_ref, g_ref, *rest):
        s_in, (dx_ref, gg_ref), s_out = rest[:ns], rest[ns:ns + 2], rest[ns + 2:2 * ns + 2]
        if ns:
            start, finish = scatter_stages([a.shape[1:] for a in scatter[:na]], s_in[:na], s_in[na:], s_out[:na], s_out[na:],
                                           *rest[2 * ns + 2:])
            pl.when(pl.program_id(0) == 0)(start)

        @pl.when(pl.program_id(0) == 0)
        def _():
            gg_ref[...] = jnp.zeros_like(gg_ref)

        dh = (_dot(d0_ref[...], w_ref[0:PBLK, :]) + _dot(d1_ref[...], w_ref[PBLK:PBLK + MID_W, :])
              + _dot(d2_ref[...], w_ref[PBLK + MID_W:, :]))
        xf = x_ref[...]
        r = lax.rsqrt(jnp.mean(xf * xf, axis=-1, keepdims=True) + EPS)
        xh = xf * r
        gg_ref[...] += jnp.sum(dh * xh, axis=0, keepdims=True)
        u = dh * g_ref[...]
        dx_ref[...] = dxo_ref[...] + r * (u - xh * jnp.mean(u * xh, axis=-1, keepdims=True))
        if ns:
            pl.when(pl.program_id(0) == nt - 1)(finish)

    rowb = _rows(tm, d)
    out = pl.pallas_call(
        body, out_shape=(_sds((s, d), F32), _sds((1, d), F32)) + scatter_out_shapes(scatter[:na]), grid=(nt,),
        in_specs=[_rows(tm, PBLK), _rows(tm, MID_W), _rows(tm, LG_W),
                  pl.BlockSpec(w_t.shape, lambda i: (0, 0), pipeline_mode=pl.Buffered(1)), rowb, rowb, _full((1, d))] + [_ANY] * ns,
        out_specs=(rowb, _full((1, d))) + (_ANY,) * ns,
        scratch_shapes=scatter_sems(na) if ns else [],
        compiler_params=_cp("arbitrary"), name="h_bwd_scatter" if ns else "h_bwd")(d0, d1, d2, w_t, x, dx_out, g, *scatter)
    return out[0], out[1], list(out[2:2 + na]), list(out[2 + na:])


def memkv_bwd(mem, g, mem_n, w_kv, dkv):
    m, d = mem.shape

    def body(mem_ref, g_ref, mn_ref, w_ref, dkv_ref, gw_ref, gw16_ref, gg_ref):
        dkb = dkv_ref[...].astype(BF16)
        gw = _dot_tn(mn_ref[...], dkb)
        gw_ref[...] = gw
        gw16_ref[...] = gw.astype(BF16)
        dmn = _dot_nt(dkb, w_ref[...])
        mf = mem_ref[...]
        r = lax.rsqrt(jnp.mean(mf * mf, axis=-1, keepdims=True) + EPS)
        gg_ref[...] = jnp.sum(dmn * (mf * r), axis=0, keepdims=True)

    return pl.pallas_call(
        body, out_shape=(_sds(w_kv.shape, F32), _sds(w_kv.shape, BF16), _sds((1, d), F32)),
        compiler_params=_cp(), name="memkv_bwd")(mem, g, mem_n, w_kv, dkv)


def _layer_consts(seq):
    i = jnp.arange(A_WIDTH)
    return dict(
        tabs=rope_tables(seq),
        gq=_group_ones(A_WIDTH, A_HEAD_DIM).astype(BF16), gk=_group_ones(A_KV_WIDTH, A_HEAD_DIM).astype(BF16),
        fold_q=(i[:, None] % A_HEAD_DIM == jnp.arange(LANES)[None, :]).astype(F32),
        fold_k=(i[:A_KV_WIDTH, None] % A_HEAD_DIM == jnp.arange(LANES)[None, :]).astype(F32),
        head_sel=(jnp.arange(A_HEADS)[:, None] == i[None, :] // A_HEAD_DIM).astype(F32),
    )


_BIG = ("win_t", "wkv", "wbr", "wout")


def _with_own_part(gathered, shards, chip):
    win_t, wkv, wbr, wout = [lax.dynamic_update_slice(g, sh[None], (chip, 0, 0)) for g, sh in zip(gathered, shards)]
    d = wout.shape[-1]
    return dict(win_t=win_t.reshape(IN_WIDTH, d), wkv=wkv.reshape(d, 2 * M_WIDTH),
                wbr=wbr.reshape(N_CHIPS, N_BRANCH, A_WIDTH, d // N_CHIPS), wout=wout.reshape(d, d))


def local_fwd_bwd(x, mem, tgt, small, big=None, shards=None, place=None):
    s, d = x.shape
    depth = small["norm_g"].shape[0]
    k = _layer_consts(s)
    row = lambda v: v.reshape(1, -1)
    dist = shards is not None
    if dist:
        big = [_with_own_part(allgather_layer(shards[0]), shards[0], place[0])] + [None] * (depth - 1)
    saved = []
    for l in range(depth):
        ng = row(small["norm_g"][l])
        qg = row(jnp.tile(small["q_norm_g"][l], A_HEADS))
        kg = row(jnp.tile(small["k_norm_g"][l], A_KV_HEADS))
        ws = small["w_s"][l].astype(BF16)
        ws_t = jnp.swapaxes(small["w_s"][l], 1, 2).astype(BF16)
        bsb = jnp.broadcast_to(small["b_s"][l][:, :, None], (B_GROUPS, CHUNK, B_GROUP_DIM))
        lng, lnb = row(small["sg_ln_g"][l]), row(small["sg_ln_b"][l])
        mg = row(small["mem_norm_g"][l])
        w = big[l]
        h = rms_fwd(x, ng)
        proj = proj_fwd(h, w["win_t"])
        q_t, kr, kr_t, vb, vte0, vte1 = qk_prep(proj, k["tabs"], qg, kg, k["gq"], k["gk"])
        nxt = shards[l + 1] if dist and l + 1 < depth else ()
        o_a, lse, gathered = attn_fwd(q_t, kr, vte0, vte1, gather=tuple(nxt))
        if nxt:
            big[l + 1] = _with_own_part(gathered, nxt, place[0])
        mem_n, kv = memkv_fwd(mem, mg, w["wkv"])
        x_next, y, up, merged = branch_fwd(x, proj, o_a, kv, ws, bsb, lng, lnb, w["wbr"], w["wout"])
        saved.append(dict(x=x, ng=ng, qg=qg, kg=kg, ws=ws, ws_t=ws_t, bsb=bsb, lng=lng, lnb=lnb, mg=mg, h=h, proj=proj,
                          q_t=q_t, kr=kr, kr_t=kr_t, vb=vb, o_a=o_a, lse=lse, mem_n=mem_n, kv=kv, y=y, up=up, merged=merged))
        x = x_next

    sq, dx, g_final = final_loss(x, row(small["final_g"]), tgt)
    grads = {n: [None] * depth for n in ("norm_g", "q_norm_g", "k_norm_g", "sg_ln_g", "sg_ln_b", "w_s", "b_s", "mem_norm_g")}
    parts = lambda g: g.reshape(N_CHIPS, -1, g.shape[-1])
    reduced = [[None] * len(_BIG) for _ in range(depth)]

    def reduce_all(items, t_sib, t_rem):
        for (ll, a, g, _), ts, tr in zip(items, t_sib, t_rem):
            reduced[ll][a] = reduce_rows(place, g, ts, tr)

    as_scatter = lambda items: tuple(i[2] for i in items) + tuple(i[3] for i in items)
    pending = []
    for l in reversed(range(depth)):
        sv, w = saved[l], big[l]
        dy, dlg, g_wout, g_wbr, g_wout16, g_wbr16 = merge_bwd(dx, sv["proj"], sv["y"], sv["up"], sv["merged"], w["wbr"], w["wout"])
        dmid, do_t, delta, g_ws, g_bs, g_lng, g_lnb, dkv = branch_bwd(
            dy, sv["proj"], sv["o_a"], sv["kv"], sv["ws"], sv["ws_t"], sv["bsb"], sv["lng"], sv["lnb"], k["head_sel"])
        g_wkv, g_wkv16, g_mg = memkv_bwd(mem, sv["mg"], sv["mem_n"], w["wkv"], dkv)
        if dist:
            pending += [(l, 1, parts(g_wkv), parts(g_wkv16)), (l, 2, parts(g_wbr), parts(g_wbr16)), (l, 3, parts(g_wout), parts(g_wout16))]
        dq_t, dkr, dvb, t_sib, t_rem = attn_bwd(sv["q_t"], do_t, sv["kr"], sv["kr_t"], sv["vb"], sv["lse"], delta,
                                                scatter=as_scatter(pending))
        reduce_all(pending, t_sib, t_rem)
        dqkv, g_qg, g_kg = qk_prep_bwd(sv["proj"], dq_t, dkr, dvb, k["tabs"], sv["qg"], sv["kg"], k["gq"], k["gk"],
                                       k["fold_q"], k["fold_k"])
        g_win, g_win16 = win_grad(dqkv, dmid, dlg, sv["h"])
        pending = [(l, 0, parts(g_win), parts(g_win16))] if dist else []
        last = as_scatter(pending) if l == 0 else ()
        dx, g_ng, t_sib, t_rem = h_bwd(dqkv, dmid, dlg, w["win_t"], sv["x"], dx, sv["ng"], scatter=last)
        if last:
            reduce_all(pending, t_sib, t_rem)
        grads["norm_g"][l] = g_ng[0]
        grads["q_norm_g"][l] = g_qg[0, :A_HEAD_DIM]
        grads["k_norm_g"][l] = g_kg[0, :A_HEAD_DIM]
        grads["sg_ln_g"][l] = g_lng[0]
        grads["sg_ln_b"][l] = g_lnb[0]
        grads["w_s"][l] = g_ws
        grads["b_s"][l] = g_bs[:, :, 0]
        grads["mem_norm_g"][l] = g_mg[0]
        if not dist:
            reduced[l] = dict(zip(_BIG, (parts(g_win), parts(g_wkv), parts(g_wbr), parts(g_wout))))
    grads = {n: jnp.stack(v) for n, v in grads.items()}
    grads["final_g"] = g_final[0]
    return sq[0, 0], dx, grads, reduced


def _row_block(rows, width, cap_bytes=2 * 2**20):
    best = None
    for br in range(8, rows + 1, 8):
        if rows % br == 0 and br * width * 4 <= cap_bytes:
            best = br
    return best if best is not None else rows


def adamw(w, g, m, v):
    r, c = w.shape
    br = _row_block(r, c)

    def body(w_ref, g_ref, m_ref, v_ref, d_ref, nm_ref, nv_ref):
        gg = g_ref[...]
        mm = ADAM_B1 * m_ref[...] + (1.0 - ADAM_B1) * gg
        vv = ADAM_B2 * v_ref[...] + (1.0 - ADAM_B2) * (gg * gg)
        m_hat = mm / (1.0 - ADAM_B1 ** ADAM_STEP)
        v_hat = vv / (1.0 - ADAM_B2 ** ADAM_STEP)
        d_ref[...] = -ADAM_LR * (m_hat / (jnp.sqrt(v_hat) + ADAM_EPS) + ADAM_WD * w_ref[...])
        nm_ref[...] = mm
        nv_ref[...] = vv

    blk = _rows(br, c)
    return pl.pallas_call(
        body, out_shape=(_sds((r, c), F32),) * 3, grid=(r // br,), in_specs=[blk] * 4, out_specs=(blk,) * 3,
        compiler_params=_cp("parallel"), name="adamw")(w, g, m, v)


N_REMOTE = 2 * (N_CHIPS - 1)


def reduce_rows(place, g, t_sib, t_rem):
    _, r, c = g.shape
    r2 = r // 2
    nt = 2 if r2 * c * 4 > 2**20 else 1
    tr = r2 // nt

    def body(place_ref, g_ref, s_ref, t_ref, f_ref):
        acc = g_ref[...] + s_ref[...]
        for j in range(N_REMOTE):
            acc = acc + t_ref[j].astype(F32)
        f_ref[...] = acc

    return pl.pallas_call(
        body, out_shape=_sds((r, c), F32),
        grid_spec=pltpu.PrefetchScalarGridSpec(
            num_scalar_prefetch=1, grid=(nt,),
            in_specs=[pl.BlockSpec((None, tr, c), lambda i, p: (p[0], p[1] * nt + i, 0)),
                      pl.BlockSpec((tr, c), lambda i, p: (i, 0)),
                      pl.BlockSpec((N_REMOTE, tr, c), lambda i, p: (0, i, 0))],
            out_specs=pl.BlockSpec((tr, c), lambda i, p: (p[1] * nt + i, 0))),
        compiler_params=_cp("parallel"), name="reduce_rows")(place, g, t_sib, t_rem)


_ANY = pl.BlockSpec(memory_space=pl.ANY)


def _place():
    x, y, c = lax.axis_index("x"), lax.axis_index("y"), lax.axis_index("c")
    chips = [(1 - x, y), (x, 1 - y), (1 - x, 1 - y)]
    return x, y, c, chips


def gather_sems(n):
    return [pltpu.SemaphoreType.DMA((n, N_REMOTE)), pltpu.SemaphoreType.DMA((n, N_REMOTE))]


def gather_stages(shapes, ins, outs, send, recv):
    n = len(shapes)
    x, y, c, chips = _place()
    me = 2 * x + y
    sib = (x, y, 1 - c)

    def rows(a, hl):
        r2 = shapes[a][0] // 2
        return pl.ds(hl * r2, r2)

    def remote(a, k, src, dst, dev):
        return pltpu.make_async_remote_copy(src, dst, send.at[a, k], recv.at[a, k], device_id=dev, device_id_type=MESH)

    def sent(a, k):
        cx, cy = chips[k]
        return remote(a, k, ins[a].at[rows(a, c)], outs[a].at[me, rows(a, c)], (cx, cy, c))

    def passed(a, k, hl):
        cx, cy = chips[k]
        got = outs[a].at[2 * cx + cy, rows(a, hl)]
        return remote(a, k, got, got, (cx, cy, c)), remote(a, 3 + k, got, got, sib)

    def start():
        for a in range(n):
            for k in range(3):
                sent(a, k).start()

    def forward():
        for k in range(3):
            for a in range(n):
                arrived, on = passed(a, k, c)
                arrived.wait_recv()
                on.start()

    def finish():
        for k in range(3):
            for a in range(n):
                passed(a, k, 1 - c)[1].wait_recv()
        for k in range(3):
            for a in range(n):
                sent(a, k).wait_send()
                passed(a, k, c)[1].wait_send()

    return start, forward, finish


def allgather_layer(shards):
    n = len(shards)

    def body(*refs):
        for stage in gather_stages([a.shape for a in shards], refs[:n], refs[n:2 * n], *refs[2 * n:]):
            stage()

    return pl.pallas_call(
        body, out_shape=tuple(_sds((N_CHIPS,) + a.shape, a.dtype) for a in shards),
        in_specs=[_ANY] * n, out_specs=(_ANY,) * n, scratch_shapes=gather_sems(n), name="allgather_layer")(*shards)


def scatter_sems(n):
    return [pltpu.SemaphoreType.DMA((n, N_REMOTE + 1)), pltpu.SemaphoreType.DMA((n, N_REMOTE + 1))]


def scatter_out_shapes(gs):
    return (tuple(_sds((g.shape[1] // 2, g.shape[2]), F32) for g in gs)
            + tuple(_sds((N_REMOTE, g.shape[1] // 2, g.shape[2]), BF16) for g in gs))


def scatter_stages(shapes, gf, gb, t_sib, t_rem, send, recv):
    n = len(shapes)
    x, y, c, chips = _place()
    me = 2 * x + y

    def copies():
        out = []
        for a in range(n):
            r2 = shapes[a][0] // 2
            out.append(pltpu.make_async_remote_copy(gf[a].at[me, pl.ds((1 - c) * r2, r2)], t_sib[a], send.at[a, N_REMOTE],
                                                    recv.at[a, N_REMOTE], device_id=(x, y, 1 - c), device_id_type=MESH))
            for k, (cx, cy) in enumerate(chips):
                for o in range(2):
                    tc = c if o == 0 else 1 - c
                    out.append(pltpu.make_async_remote_copy(gb[a].at[2 * cx + cy, pl.ds(tc * r2, r2)], t_rem[a].at[2 * k + o],
                                                            send.at[a, 2 * k + o], recv.at[a, 2 * k + o],
                                                            device_id=(cx, cy, tc), device_id_type=MESH))
        return out

    def start():
        for cp in copies():
            cp.start()

    def finish():
        for cp in copies():
            cp.wait()

    return start, finish


def share_final(fs):
    n = len(fs)

    def body(*refs):
        out = refs[n:2 * n]
        send, recv = refs[2 * n:]
        x, y, c, _ = _place()
        sib = (x, y, 1 - c)
        cps = []
        for a in range(n):
            r2 = fs[a].shape[0] // 2
            mine = out[a].at[pl.ds(c * r2, r2)]
            cp = pltpu.make_async_remote_copy(mine, mine, send.at[a], recv.at[a], device_id=sib, device_id_type=MESH)
            cp.start()
            cps.append(cp)
        for a, cp in enumerate(cps):
            r2 = fs[a].shape[0] // 2
            theirs = out[a].at[pl.ds((1 - c) * r2, r2)]
            cp.wait_send()
            pltpu.make_async_remote_copy(theirs, theirs, send.at[a], recv.at[a], device_id=sib, device_id_type=MESH).wait_recv()

    return pl.pallas_call(
        body, out_shape=tuple(_sds(f.shape, F32) for f in fs),
        in_specs=[_ANY] * n, out_specs=(_ANY,) * n, input_output_aliases={a: a for a in range(n)},
        scratch_shapes=[pltpu.SemaphoreType.DMA((n,)), pltpu.SemaphoreType.DMA((n,))],
        name="share_final")(*fs)


def allreduce_small(v):
    r, w = v.shape
    ndev = 2 * N_CHIPS

    def body(v_ref, sum_ref, all_ref, send, recv, loc):
        x, y, c, chips = _place()
        me, sib = (x, y, c), (x, y, 1 - c)

        def slab(px, py, pc):
            return all_ref.at[4 * px + 2 * py + pc]

        def copy(k, block, to, src=None):
            return pltpu.make_async_remote_copy(slab(*block) if src is None else src, slab(*block), send.at[k], recv.at[k],
                                                device_id=to, device_id_type=MESH)

        mine = pltpu.make_async_copy(v_ref, slab(*me), loc)
        mine.start()
        first = [copy(0, me, sib, src=v_ref)] + [copy(1 + j, me, (*chip, c), src=v_ref) for j, chip in enumerate(chips)]
        for cp in first:
            cp.start()
        passed = [copy(4 + j, (*chip, c), sib) for j, chip in enumerate(chips)]
        for j, chip in enumerate(chips):
            copy(1 + j, (*chip, c), me).wait_recv()
            passed[j].start()
        copy(0, sib, me).wait_recv()
        for j, chip in enumerate(chips):
            copy(4 + j, (*chip, 1 - c), me).wait_recv()
        for cp in first + passed:
            cp.wait_send()
        mine.wait()
        acc = all_ref[0]
        for i in range(1, ndev):
            acc = acc + all_ref[i]
        sum_ref[...] = acc

    vm = pl.BlockSpec(memory_space=pltpu.VMEM)
    return pl.pallas_call(
        body, out_shape=_sds((r, w), F32), in_specs=[vm], out_specs=vm,
        scratch_shapes=[pltpu.VMEM((ndev, r, w), F32), pltpu.SemaphoreType.DMA((7,)), pltpu.SemaphoreType.DMA((7,)),
                        pltpu.SemaphoreType.DMA],
        compiler_params=pltpu.CompilerParams(vmem_limit_bytes=VMEM_LIMIT), name="allreduce_small")(v)


_SMALL = ("norm_g", "q_norm_g", "k_norm_g", "sg_ln_g", "sg_ln_b", "w_s", "b_s", "mem_norm_g", "final_g")
_WEIGHTS = ("norm_g", "w_in", "q_norm_g", "k_norm_g", "sg_ln_g", "sg_ln_b", "w_s", "b_s", "mem_norm_g", "w_mem_kv", "w_br",
            "w_out", "final_g")


def _pack(d):
    flat = jnp.concatenate([d[n].reshape(-1) for n in _SMALL])
    rows = -(-flat.shape[0] // (8 * LANES)) * 8
    return jnp.pad(flat, (0, rows * LANES - flat.shape[0])).reshape(rows, LANES)


def _unpack(p, like):
    flat, out, o = p.reshape(-1), {}, 0
    for n in _SMALL:
        out[n] = flat[o:o + like[n].size].reshape(like[n].shape)
        o += like[n].size
    return out


def kernel(x, mem, norm_g, w_in, q_norm_g, k_norm_g, sg_ln_g, sg_ln_b, w_s, b_s, mem_norm_g, w_mem_kv, w_br, w_out, final_g, loss_target, m_norm_g, m_w_in, m_q_norm_g, m_k_norm_g, m_sg_ln_g, m_sg_ln_b, m_w_s, m_b_s, m_mem_norm_g, m_w_mem_kv, m_w_br, m_w_out, m_final_g, v_norm_g, v_w_in, v_q_norm_g, v_k_norm_g, v_sg_ln_g, v_sg_ln_b, v_w_s, v_b_s, v_mem_norm_g, v_w_mem_kv, v_w_br, v_w_out, v_final_g):
    w = dict(norm_g=norm_g, w_in=w_in, q_norm_g=q_norm_g, k_norm_g=k_norm_g, sg_ln_g=sg_ln_g, sg_ln_b=sg_ln_b, w_s=w_s, b_s=b_s,
             mem_norm_g=mem_norm_g, w_mem_kv=w_mem_kv, w_br=w_br, w_out=w_out, final_g=final_g)
    m = dict(norm_g=m_norm_g, w_in=m_w_in, q_norm_g=m_q_norm_g, k_norm_g=m_k_norm_g, sg_ln_g=m_sg_ln_g, sg_ln_b=m_sg_ln_b,
             w_s=m_w_s, b_s=m_b_s, mem_norm_g=m_mem_norm_g, w_mem_kv=m_w_mem_kv, w_br=m_w_br, w_out=m_w_out, final_g=m_final_g)
    v = dict(norm_g=v_norm_g, w_in=v_w_in, q_norm_g=v_q_norm_g, k_norm_g=v_k_norm_g, sg_ln_g=v_sg_ln_g, sg_ln_b=v_sg_ln_b,
             w_s=v_w_s, b_s=v_b_s, mem_norm_g=v_mem_norm_g, w_mem_kv=v_w_mem_kv, w_br=v_w_br, w_out=v_w_out, final_g=v_final_g)
    depth, d = norm_g.shape
    nsh = N_CHIPS
    br_rows = N_BRANCH * A_WIDTH
    br_cols = d // nsh

    shards = [[jnp.swapaxes(w_in[l], 0, 1).astype(BF16), w_mem_kv[l].astype(BF16), w_br[l].astype(BF16).reshape(br_rows, br_cols),
               w_out[l].astype(BF16)] for l in range(depth)]
    place = jnp.stack([2 * lax.axis_index("x") + lax.axis_index("y"), lax.axis_index("c")]).astype(jnp.int32)
    small = {n: w[n] for n in _SMALL}

    sq, dx, grads, reduced = local_fwd_bwd(x[0], mem[0], loss_target[0], small, shards=shards, place=place)
    loss = (0.5 / d) * lax.psum(sq, ("x", "y", "c"))

    finals = share_final([g for layer in reduced for g in layer])
    finals = [jnp.stack(finals[a::len(_BIG)]) for a in range(len(_BIG))]
    big_grads = dict(w_in=jnp.swapaxes(finals[0], 1, 2), w_mem_kv=finals[1],
                     w_br=finals[2].reshape(depth, N_BRANCH, A_WIDTH, br_cols), w_out=finals[3])

    small_grads = _unpack(allreduce_small(_pack(grads)), small)

    out_g, out_d, out_m, out_v = {}, {}, {}, {}
    sd, sm, sv = adamw(_pack(small), _pack(small_grads), _pack({n: m[n] for n in _SMALL}), _pack({n: v[n] for n in _SMALL}))
    sd, sm, sv = _unpack(sd, small), _unpack(sm, small), _unpack(sv, small)
    for n in _SMALL:
        out_g[n], out_d[n], out_m[n], out_v[n] = small_grads[n], sd[n], sm[n], sv[n]
    for n, g in big_grads.items():
        two_d = (-1, w[n].shape[-1])
        dd, mm, vv = adamw(w[n].reshape(two_d), g.reshape(two_d), m[n].reshape(two_d), v[n].reshape(two_d))
        out_g[n], out_d[n], out_m[n], out_v[n] = g, dd.reshape(w[n].shape), mm.reshape(w[n].shape), vv.reshape(w[n].shape)
    return (loss, dx[None], *[out_g[n] for n in _WEIGHTS], *[out_d[n] for n in _WEIGHTS], *[out_m[n] for n in _WEIGHTS],
            *[out_v[n] for n in _WEIGHTS])
```

```python
import functools

import jax
import jax.numpy as jnp
from jax import lax
from jax.experimental import pallas as pl
from jax.experimental.pallas import tpu as pltpu

F32 = jnp.float32
BF16 = jnp.bfloat16

D_MODEL = 1024
GRID_W = 64
CHUNK = 128
ROPE_THETA = 10000.0
EPS = 1e-6
A_HEADS, A_KV_HEADS, A_HEAD_DIM = 8, 2, 64
A_WIDTH, A_KV_WIDTH = 512, 128
B_GROUPS, B_GROUP_DIM, B_WIDTH = 4, 128, 512
M_HEADS, M_HEAD_DIM, M_WIDTH = 4, 128, 512
N_BRANCH = 3
IN_WIDTH = 6912
O_QA, O_KA, O_VA, O_ZA, O_UB, O_VB, O_ZB, O_QM, O_ZM, O_LG = 0, 512, 640, 768, 1280, 1792, 2304, 2816, 3328, 3840
PBLK = 768
N_PBLK = IN_WIDTH // PBLK
MID_W = 3072
LG_W = 3072

LN2 = 0.6931471805599453
Q_SCALE = A_HEAD_DIM ** -0.5 / LN2

ADAM_LR, ADAM_B1, ADAM_B2, ADAM_EPS, ADAM_WD, ADAM_STEP = 0.001, 0.9, 0.999, 1e-08, 0.01, 10

V7X_VMEM_BYTES = 64 * 2**20
VMEM_LIMIT = V7X_VMEM_BYTES - 8 * 2**20
LANES = 128
MESH = pl.DeviceIdType.MESH
N_CHIPS = 4


def _cp(*sem):
    return pltpu.CompilerParams(dimension_semantics=sem if sem else None, vmem_limit_bytes=VMEM_LIMIT)


def _dot(a, b):
    return jnp.dot(a, b, preferred_element_type=F32)


def _dot_nt(a, b):
    return lax.dot_general(a, b, (((1,), (1,)), ((), ())), preferred_element_type=F32)


def _dot_tn(a, b):
    return lax.dot_general(a, b, (((0,), (0,)), ((), ())), preferred_element_type=F32)


def _dot_hi(a, b):
    return jnp.dot(a, b, preferred_element_type=F32, precision=lax.Precision.HIGHEST)


def _group_sum(a, ones):
    hi = a.astype(BF16)
    lo = (a - hi.astype(F32)).astype(BF16)
    return _dot(hi, ones) + _dot(lo, ones)


def _dot_nt_hi(a, b):
    return lax.dot_general(a, b, (((1,), (1,)), ((), ())), preferred_element_type=F32, precision=lax.Precision.HIGHEST)


def _sig(z):
    return 1.0 / (1.0 + jnp.exp(-z))


def _full(shape):
    nd = len(shape)
    return pl.BlockSpec(shape, lambda *_: (0,) * nd)


def _rows(tm, width):
    return pl.BlockSpec((tm, width), lambda i: (i, 0))


def _sds(shape, dtype):
    return jax.ShapeDtypeStruct(shape, dtype)


def rms_fwd(x, g):
    s, d = x.shape
    tm = min(s, 512)

    def body(x_ref, g_ref, h_ref):
        xf = x_ref[...]
        r = lax.rsqrt(jnp.mean(xf * xf, axis=-1, keepdims=True) + EPS)
        h_ref[...] = ((xf * r) * g_ref[...]).astype(BF16)

    return pl.pallas_call(
        body, out_shape=_sds((s, d), BF16), grid=(s // tm,),
        in_specs=[_rows(tm, d), _full((1, d))], out_specs=_rows(tm, d),
        compiler_params=_cp("parallel"), name="rms_fwd")(x, g)


def proj_fwd(h, w_t):
    s, d = h.shape
    n = w_t.shape[0]
    tm = min(s, 512)
    tn = 2304

    def body(h_ref, w_ref, o_ref):
        o_ref[...] = _dot_nt(h_ref[...], w_ref[...]).astype(BF16)

    return pl.pallas_call(
        body, out_shape=_sds((s, n), BF16), grid=(n // tn, s // tm),
        in_specs=[pl.BlockSpec((tm, d), lambda j, i: (i, 0)), pl.BlockSpec((tn, d), lambda j, i: (j, 0))],
        out_specs=pl.BlockSpec((tm, tn), lambda j, i: (i, j)),
        compiler_params=_cp("parallel", "parallel"), name="proj_fwd")(h, w_t)


def rope_tables(seq):
    rows = seq // GRID_W
    row = jnp.repeat(jnp.arange(rows, dtype=F32), GRID_W)
    col = jnp.tile(jnp.arange(GRID_W, dtype=F32), rows)
    n_freq = A_HEAD_DIM // 4
    inv = ROPE_THETA ** (-jnp.arange(n_freq, dtype=F32) / n_freq)
    ang = jnp.stack([row[:, None] * inv, col[:, None] * inv], axis=1)
    cos, sin = jnp.cos(ang), jnp.sin(ang)
    zero = jnp.zeros_like(sin[:, 0])
    c64 = jnp.concatenate([cos[:, 0], cos[:, 0], cos[:, 1], cos[:, 1]], axis=1)
    sa64 = jnp.concatenate([zero, sin[:, 0], zero, sin[:, 1]], axis=1)
    sb64 = jnp.concatenate([-sin[:, 0], zero, -sin[:, 1], zero], axis=1)
    two = lambda t: jnp.concatenate([t, t], axis=1)
    return two(c64), two(sa64), two(sb64)


def _group_ones(width, group):
    i = jnp.arange(width)
    return (i[:, None] // group == i[None, :] // group).astype(F32)


def _rope(xn, c, sa, sb):
    w = xn.shape[1]
    return xn * c + pltpu.roll(xn, 16, 1) * sa + pltpu.roll(xn, w - 16, 1) * sb


def _rope_t(dy, c, sa, sb):
    w = dy.shape[1]
    return dy * c + pltpu.roll(dy * sa, w - 16, 1) + pltpu.roll(dy * sb, 16, 1)


def _tile4(t):
    return jnp.concatenate([t, t, t, t], axis=1)


def qk_prep(proj, tabs, qg, kg, gq, gk):
    s = proj.shape[0]
    tm = min(s, 512)
    c, sa, sb = tabs

    def body(p_ref, c_ref, sa_ref, sb_ref, qg_ref, kg_ref, gq_ref, gk_ref, qt_ref, kr_ref, krt_ref, vb_ref, v0_ref, v1_ref):
        xq = p_ref[:, O_QA:O_QA + A_WIDTH].astype(F32)
        xk = p_ref[:, O_KA:O_KA + A_KV_WIDTH].astype(F32)
        xv = p_ref[:, O_VA:O_VA + A_KV_WIDTH].astype(F32)
        cc, ssa, ssb = c_ref[...], sa_ref[...], sb_ref[...]
        msq = _group_sum(xq * xq, gq_ref[...]) * (1.0 / A_HEAD_DIM)
        qn = (xq * lax.rsqrt(msq + EPS)) * qg_ref[...]
        qr = _rope(qn, _tile4(cc), _tile4(ssa), _tile4(ssb)) * Q_SCALE
        qt_ref[...] = qr.T.astype(BF16)
        msk = _group_sum(xk * xk, gk_ref[...]) * (1.0 / A_HEAD_DIM)
        kn = (xk * lax.rsqrt(msk + EPS)) * kg_ref[...]
        kr = _rope(kn, cc, ssa, ssb)
        kr_ref[...] = kr.astype(BF16)
        krt_ref[...] = kr.T.astype(BF16)
        vb_ref[...] = xv.astype(BF16)
        vt = xv.T.astype(BF16)
        one = jnp.ones((A_HEAD_DIM, tm), BF16)
        v0_ref[...] = jnp.concatenate([vt[:A_HEAD_DIM], one], axis=0)
        v1_ref[...] = jnp.concatenate([one, vt[A_HEAD_DIM:]], axis=0)

    tab = _rows(tm, LANES)
    colb = lambda w: pl.BlockSpec((w, tm), lambda i: (0, i))
    return pl.pallas_call(
        body,
        out_shape=(_sds((A_WIDTH, s), BF16), _sds((s, A_KV_WIDTH), BF16), _sds((A_KV_WIDTH, s), BF16),
                   _sds((s, A_KV_WIDTH), BF16), _sds((A_KV_WIDTH, s), BF16), _sds((A_KV_WIDTH, s), BF16)),
        grid=(s // tm,),
        in_specs=[_rows(tm, PBLK), tab, tab, tab, _full((1, A_WIDTH)), _full((1, A_KV_WIDTH)),
                  _full((A_WIDTH, A_WIDTH)), _full((A_KV_WIDTH, A_KV_WIDTH))],
        out_specs=(colb(A_WIDTH), _rows(tm, A_KV_WIDTH), colb(A_KV_WIDTH), _rows(tm, A_KV_WIDTH), colb(A_KV_WIDTH),
                   colb(A_KV_WIDTH)),
        compiler_params=_cp("parallel"), name="qk_prep")(proj, c, sa, sb, qg, kg, gq, gk)


def _pad_head(q_h, kv):
    z = jnp.zeros_like(q_h)
    return jnp.concatenate([q_h, z], axis=0) if kv == 0 else jnp.concatenate([z, q_h], axis=0)


def attn_fwd(q_t, kr, vte0, vte1, gather=()):
    s = kr.shape[0]
    tq = min(s, 256)
    kc = min(s, 512)
    nkc = s // kc
    nq = s // tq
    grp = A_HEADS // A_KV_HEADS
    ng = len(gather)

    def body(qt_ref, kr_ref, v0_ref, v1_ref, *rest):
        g_in, (o_ref, lse_ref), g_out = rest[:ng], rest[ng:ng + 2], rest[ng + 2:2 * ng + 2]
        qp_ref, m_ref, acc_ref = rest[2 * ng + 2:2 * ng + 5]
        if ng:
            start, forward, finish = gather_stages([g.shape for g in gather], g_in, g_out, *rest[2 * ng + 5:])
            pl.when(pl.program_id(0) == 0)(start)
            pl.when(pl.program_id(0) == (3 * nq) // 4)(forward)

        for h in range(A_HEADS):
            qp_ref[h] = _pad_head(qt_ref[A_HEAD_DIM * h:A_HEAD_DIM * (h + 1), :], h // grp)
        m_ref[...] = jnp.full(m_ref.shape, -1e30, F32)
        acc_ref[...] = jnp.zeros_like(acc_ref)

        def step(ci, carry):
            ks = pl.ds(pl.multiple_of(ci * kc, kc), kc)
            kblk = kr_ref[ks, :]
            vts = (v0_ref[:, ks], v1_ref[:, ks])
            scs = [_dot(kblk, qp_ref[h]) for h in range(A_HEADS)]
            for h in range(A_HEADS):
                sc = scs[h]
                m_prev = m_ref[h:h + 1, :]
                m_new = jnp.maximum(m_prev, jnp.max(sc, axis=0, keepdims=True))
                p = jnp.exp2(sc - m_new)
                acc_ref[h] = acc_ref[h] * jnp.exp2(m_prev - m_new) + _dot(vts[h // grp], p.astype(BF16))
                m_ref[h:h + 1, :] = m_new
            return carry

        lax.fori_loop(0, nkc, step, 0)
        outs, lses = [], []
        for h in range(A_HEADS):
            kv = h // grp
            acc = acc_ref[h]
            l = acc[A_HEAD_DIM * (1 - kv):A_HEAD_DIM * (1 - kv) + 1, :]
            outs.append(acc[A_HEAD_DIM * kv:A_HEAD_DIM * (kv + 1), :] / l)
            lses.append(m_ref[h:h + 1, :] + jnp.log2(l))
        o_ref[...] = jnp.concatenate(outs, axis=0).T
        lse_ref[...] = jnp.concatenate(lses, axis=0)
        if ng:
            pl.when(pl.program_id(0) == nq - 1)(finish)

    out = pl.pallas_call(
        body,
        out_shape=(_sds((s, A_WIDTH), F32), _sds((A_HEADS, s), F32)) + tuple(_sds((N_CHIPS,) + g.shape, g.dtype) for g in gather),
        grid=(nq,),
        in_specs=[pl.BlockSpec((A_WIDTH, tq), lambda i: (0, i)), _full((s, A_KV_WIDTH)), _full((A_KV_WIDTH, s)),
                  _full((A_KV_WIDTH, s))] + [_ANY] * ng,
        out_specs=(_rows(tq, A_WIDTH), pl.BlockSpec((A_HEADS, tq), lambda i: (0, i))) + (_ANY,) * ng,
        scratch_shapes=[pltpu.VMEM((A_HEADS, A_KV_WIDTH, tq), BF16), pltpu.VMEM((A_HEADS, tq), F32),
                        pltpu.VMEM((A_HEADS, A_KV_WIDTH, tq), F32)] + (gather_sems(ng) if ng else []),
        compiler_params=_cp("arbitrary"), name="attn_fwd_gather" if ng else "attn_fwd")(q_t, kr, vte0, vte1, *gather)
    return out[0], out[1], list(out[2:])


def memkv_fwd(mem, g, w_kv):
    m, d = mem.shape

    def body(mem_ref, g_ref, w_ref, mn_ref, kv_ref):
        mf = mem_ref[...]
        r = lax.rsqrt(jnp.mean(mf * mf, axis=-1, keepdims=True) + EPS)
        mn = ((mf * r) * g_ref[...]).astype(BF16)
        mn_ref[...] = mn
        kv_ref[...] = _dot(mn, w_ref[...]).astype(BF16)

    return pl.pallas_call(
        body, out_shape=(_sds((m, d), BF16), _sds((m, 2 * M_WIDTH), BF16)),
        compiler_params=_cp(), name="memkv_fwd")(mem, g, w_kv)


def _layer_norm_stats(v):
    mu = jnp.mean(v, axis=-1, keepdims=True)
    xc = v - mu
    rstd = lax.rsqrt(jnp.mean(xc * xc, axis=-1, keepdims=True) + EPS)
    return xc * rstd, rstd


def _spatial_mix(vlb, ws_ref, bsb_ref, tm):
    rows = []
    for ci in range(tm // CHUNK):
        cols = []
        for g in range(B_GROUPS):
            blk = vlb[ci * CHUNK:(ci + 1) * CHUNK, g * B_GROUP_DIM:(g + 1) * B_GROUP_DIM]
            cols.append(_dot(ws_ref[g], blk) + bsb_ref[g])
        rows.append(jnp.concatenate(cols, axis=1))
    return jnp.concatenate(rows, axis=0)


def _mem_attn(qm, kv_ref):
    out = []
    for h in range(M_HEADS):
        qh = qm[:, h * M_HEAD_DIM:(h + 1) * M_HEAD_DIM].astype(BF16)
        kh = kv_ref[:, h * M_HEAD_DIM:(h + 1) * M_HEAD_DIM]
        vh = kv_ref[:, M_WIDTH + h * M_HEAD_DIM:M_WIDTH + (h + 1) * M_HEAD_DIM]
        sc = _dot_nt(qh, kh) * (M_HEAD_DIM ** -0.5)
        e = jnp.exp(sc - jnp.max(sc, axis=-1, keepdims=True))
        p = e / jnp.sum(e, axis=-1, keepdims=True)
        out.append((p, _dot(p.astype(BF16), vh)))
    return out


def branch_fwd(x, proj, o_a, kv, ws, bsb, ln_g, ln_b, w_br, w_out):
    s, d = x.shape
    tm = min(s, 256)

    def body(x_ref, p_ref, oa_ref, kv_ref, ws_ref, bsb_ref, lg_ref, lb_ref, wbr_ref, wo_ref,
             xn_ref, y_ref, up_ref, mg_ref):
        seg = lambda o, w: p_ref[:, o:o + w].astype(F32)
        z_a, u_b, v_b, z_b = seg(O_ZA, A_WIDTH), seg(O_UB, B_WIDTH), seg(O_VB, B_WIDTH), seg(O_ZB, B_WIDTH)
        q_m, z_m = seg(O_QM, M_WIDTH), seg(O_ZM, M_WIDTH)
        xhat, _ = _layer_norm_stats(v_b)
        vln = xhat * lg_ref[...] + lb_ref[...]
        mixed = _spatial_mix(vln.astype(BF16), ws_ref, bsb_ref, tm)
        y_b = (u_b * mixed) * (z_b * _sig(z_b))
        o_m = jnp.concatenate([o for _, o in _mem_attn(q_m, kv_ref)], axis=1)
        y_a = oa_ref[...] * (z_a * _sig(z_a))
        y_m = o_m * (z_m * _sig(z_m))
        merged = None
        for n, yy in enumerate((y_a, y_b, y_m)):
            yb = yy.astype(BF16)
            y_ref[n] = yb
            up = jnp.concatenate([_dot(yb, wbr_ref[c, n]) for c in range(N_CHIPS)], axis=1)
            up_ref[n] = up.astype(BF16)
            t = _sig(seg(O_LG + n * d, d)) * up
            merged = t if merged is None else merged + t
        mb = merged.astype(BF16)
        mg_ref[...] = mb
        xn_ref[...] = x_ref[...] + _dot(mb, wo_ref[...])

    return pl.pallas_call(
        body,
        out_shape=(_sds((s, d), F32), _sds((N_BRANCH, s, A_WIDTH), BF16), _sds((N_BRANCH, s, d), BF16), _sds((s, d), BF16)),
        grid=(s // tm,),
        in_specs=[_rows(tm, d), _rows(tm, IN_WIDTH), _rows(tm, A_WIDTH), _full(kv.shape), _full(ws.shape), _full(bsb.shape),
                  _full((1, B_WIDTH)), _full((1, B_WIDTH)), _full(w_br.shape), _full(w_out.shape)],
        out_specs=(_rows(tm, d), pl.BlockSpec((N_BRANCH, tm, A_WIDTH), lambda i: (0, i, 0)),
                   pl.BlockSpec((N_BRANCH, tm, d), lambda i: (0, i, 0)), _rows(tm, d)),
        compiler_params=_cp("parallel"), name="branch_fwd")(x, proj, o_a, kv, ws, bsb, ln_g, ln_b, w_br, w_out)


def final_loss(x, fg, tgt):
    s, d = x.shape
    tm = min(s, 512)

    def body(x_ref, g_ref, t_ref, ls_ref, dx_ref, gg_ref):
        @pl.when(pl.program_id(0) == 0)
        def _():
            ls_ref[...] = jnp.zeros_like(ls_ref)
            gg_ref[...] = jnp.zeros_like(gg_ref)

        xf = x_ref[...]
        g = g_ref[...]
        r = lax.rsqrt(jnp.mean(xf * xf, axis=-1, keepdims=True) + EPS)
        xh = xf * r
        e = xh * g - t_ref[...]
        sq = jnp.sum(jnp.sum(e * e, axis=0, keepdims=True), axis=1, keepdims=True)
        ls_ref[...] += jnp.broadcast_to(sq, ls_ref.shape)
        dy = e * (1.0 / d)
        gg_ref[...] += jnp.sum(dy * xh, axis=0, keepdims=True)
        gy = dy * g
        dx_ref[...] = r * (gy - xh * jnp.mean(gy * xh, axis=-1, keepdims=True))

    return pl.pallas_call(
        body, out_shape=(_sds((1, LANES), F32), _sds((s, d), F32), _sds((1, d), F32)), grid=(s // tm,),
        in_specs=[_rows(tm, d), _full((1, d)), _rows(tm, d)],
        out_specs=(_full((1, LANES)), _rows(tm, d), _full((1, d))),
        compiler_params=_cp("arbitrary"), name="final_loss")(x, fg, tgt)


def _pblocks(tm, first, count):
    return [pl.BlockSpec((tm, PBLK), functools.partial(lambda i, b: (i, b), b=first + k)) for k in range(count)]


def merge_bwd(dx, proj, y, up, merged, w_br, w_out):
    s, d = dx.shape
    tm = min(s, 256)
    nlg = LG_W // PBLK
    cw = d // N_CHIPS

    def body(dx_ref, l0, l1, l2, l3, y_ref, up_ref, mg_ref, wbr_ref, wo_ref, dy_ref, dlg_ref, gwo_ref, gwb_ref, gwo16_ref, gwb16_ref):
        @pl.when(pl.program_id(0) == 0)
        def _():
            gwo_ref[...] = jnp.zeros_like(gwo_ref)
            gwb_ref[...] = jnp.zeros_like(gwb_ref)

        dxb = dx_ref[...].astype(BF16)
        dmg = _dot_nt(dxb, wo_ref[...])
        gwo_ref[...] += _dot_tn(mg_ref[...], dxb)
        lg = jnp.concatenate([l0[...], l1[...], l2[...], l3[...]], axis=1).astype(F32)
        for n in range(N_BRANCH):
            g = _sig(lg[:, n * d:(n + 1) * d])
            dup = dmg * g
            dlg_ref[:, n * d:(n + 1) * d] = ((dup * up_ref[n].astype(F32)) * (1.0 - g)).astype(BF16)
            dupb = dup.astype(BF16)
            dyn = None
            for c in range(N_CHIPS):
                blk = dupb[:, c * cw:(c + 1) * cw]
                gwb_ref[c, n] += _dot_tn(y_ref[n], blk)
                t = _dot_nt(blk, wbr_ref[c, n])
                dyn = t if dyn is None else dyn + t
            dy_ref[n] = dyn

        @pl.when(pl.program_id(0) == pl.num_programs(0) - 1)
        def _():
            gwo16_ref[...] = gwo_ref[...].astype(BF16)
            gwb16_ref[...] = gwb_ref[...].astype(BF16)

    return pl.pallas_call(
        body,
        out_shape=(_sds((N_BRANCH, s, A_WIDTH), F32), _sds((s, LG_W), BF16), _sds((d, d), F32), _sds(w_br.shape, F32),
                   _sds((d, d), BF16), _sds(w_br.shape, BF16)),
        grid=(s // tm,),
        in_specs=[_rows(tm, d)] + _pblocks(tm, O_LG // PBLK, nlg) + [
            pl.BlockSpec((N_BRANCH, tm, A_WIDTH), lambda i: (0, i, 0)), pl.BlockSpec((N_BRANCH, tm, d), lambda i: (0, i, 0)),
            _rows(tm, d), _full(w_br.shape), _full(w_out.shape)],
        out_specs=(pl.BlockSpec((N_BRANCH, tm, A_WIDTH), lambda i: (0, i, 0)), _rows(tm, LG_W), _full((d, d)), _full(w_br.shape),
                   _full((d, d)), _full(w_br.shape)),
        compiler_params=_cp("arbitrary"), name="merge_bwd")(dx, proj, proj, proj, proj, y, up, merged, w_br, w_out)


def _dsilu(z, sg):
    return sg * (1.0 + z * (1.0 - sg))


def branch_bwd(dy, proj, o_a, kv, ws, ws_t, bsb, ln_g, ln_b, head_sel):
    s = proj.shape[0]
    tm = min(s, 256)
    nmid = MID_W // PBLK

    def body(dy_ref, m0, m1, m2, m3, oa_ref, kv_ref, ws_ref, wst_ref, bsb_ref, lg_ref, lb_ref, sel_ref,
             dmid_ref, dot_ref, dl_ref, gws_ref, gbs_ref, glg_ref, glb_ref, dkv_ref):
        @pl.when(pl.program_id(0) == 0)
        def _():
            for r in (gws_ref, gbs_ref, glg_ref, glb_ref, dkv_ref):
                r[...] = jnp.zeros_like(r)

        mid = jnp.concatenate([m0[...], m1[...], m2[...], m3[...]], axis=1).astype(F32)
        seg = lambda o, w: mid[:, o - O_ZA:o - O_ZA + w]
        z_a, u_b, v_b, z_b = seg(O_ZA, A_WIDTH), seg(O_UB, B_WIDTH), seg(O_VB, B_WIDTH), seg(O_ZB, B_WIDTH)
        q_m, z_m = seg(O_QM, M_WIDTH), seg(O_ZM, M_WIDTH)

        def put(o, v):
            dmid_ref[:, o - O_ZA:o - O_ZA + v.shape[1]] = v.astype(BF16)

        dy_a, dy_b, dy_m = dy_ref[0], dy_ref[1], dy_ref[2]

        o_a_ = oa_ref[...]
        sg = _sig(z_a)
        do_a = dy_a * (z_a * sg)
        put(O_ZA, (dy_a * o_a_) * _dsilu(z_a, sg))
        do_l = do_a * LN2
        dot_ref[...] = do_l.T.astype(BF16)
        dl_ref[...] = _dot_nt_hi(sel_ref[...], do_l * o_a_)

        xhat, rstd = _layer_norm_stats(v_b)
        lng = lg_ref[...]
        vln = xhat * lng + lb_ref[...]
        vlb = vln.astype(BF16)
        mixed = _spatial_mix(vlb, ws_ref, bsb_ref, tm)
        sg = _sig(z_b)
        sl = z_b * sg
        put(O_UB, (dy_b * mixed) * sl)
        put(O_ZB, ((dy_b * u_b) * mixed) * _dsilu(z_b, sg))
        dmix = (dy_b * u_b) * sl
        dmb = dmix.astype(BF16)
        rows = []
        for ci in range(tm // CHUNK):
            cols = []
            for g in range(B_GROUPS):
                rs, cs = slice(ci * CHUNK, (ci + 1) * CHUNK), slice(g * B_GROUP_DIM, (g + 1) * B_GROUP_DIM)
                gws_ref[g] += _dot_nt(dmb[rs, cs], vlb[rs, cs])
                gbs_ref[g] += jnp.broadcast_to(jnp.sum(dmix[rs, cs], axis=1, keepdims=True), (CHUNK, B_GROUP_DIM))
                cols.append(_dot(wst_ref[g], dmb[rs, cs]))
            rows.append(jnp.concatenate(cols, axis=1))
        dvln = jnp.concatenate(rows, axis=0)
        glg_ref[...] += jnp.sum(dvln * xhat, axis=0, keepdims=True)
        glb_ref[...] += jnp.sum(dvln, axis=0, keepdims=True)
        gy = dvln * lng
        put(O_VB, rstd * ((gy - jnp.mean(gy, axis=-1, keepdims=True)) - xhat * jnp.mean(gy * xhat, axis=-1, keepdims=True)))

        sg = _sig(z_m)
        sl = z_m * sg
        heads = _mem_attn(q_m, kv_ref)
        o_m = jnp.concatenate([o for _, o in heads], axis=1)
        put(O_ZM, (dy_m * o_m) * _dsilu(z_m, sg))
        do_m = dy_m * sl
        dqs = []
        for h, (p, o_h) in enumerate(heads):
            hs = slice(h * M_HEAD_DIM, (h + 1) * M_HEAD_DIM)
            vs = slice(M_WIDTH + h * M_HEAD_DIM, M_WIDTH + (h + 1) * M_HEAD_DIM)
            do_h = do_m[:, hs]
            dob = do_h.astype(BF16)
            dp = _dot_nt(dob, kv_ref[:, vs])
            dsc = (p * (dp - jnp.sum(do_h * o_h, axis=-1, keepdims=True))) * (M_HEAD_DIM ** -0.5)
            dsb = dsc.astype(BF16)
            dqs.append(_dot(dsb, kv_ref[:, hs]))
            dkv_ref[:, hs] += _dot_tn(dsb, q_m[:, hs].astype(BF16))
            dkv_ref[:, vs] += _dot_tn(p.astype(BF16), dob)
        put(O_QM, jnp.concatenate(dqs, axis=1))

    return pl.pallas_call(
        body,
        out_shape=(_sds((s, MID_W), BF16), _sds((A_WIDTH, s), BF16), _sds((A_HEADS, s), F32), _sds(ws.shape, F32),
                   _sds(ws.shape, F32), _sds((1, B_WIDTH), F32), _sds((1, B_WIDTH), F32), _sds(kv.shape, F32)),
        grid=(s // tm,),
        in_specs=[pl.BlockSpec((N_BRANCH, tm, A_WIDTH), lambda i: (0, i, 0))] + _pblocks(tm, O_ZA // PBLK, nmid) + [
            _rows(tm, A_WIDTH), _full(kv.shape), _full(ws.shape), _full(ws.shape), _full(bsb.shape),
            _full((1, B_WIDTH)), _full((1, B_WIDTH)), _full(head_sel.shape)],
        out_specs=(_rows(tm, MID_W), pl.BlockSpec((A_WIDTH, tm), lambda i: (0, i)), pl.BlockSpec((A_HEADS, tm), lambda i: (0, i)),
                   _full(ws.shape), _full(ws.shape), _full((1, B_WIDTH)), _full((1, B_WIDTH)), _full(kv.shape)),
        compiler_params=_cp("arbitrary"), name="branch_bwd")(dy, proj, proj, proj, proj, o_a, kv, ws, ws_t, bsb, ln_g, ln_b, head_sel)


def attn_bwd(q_t, do_t, kr, kr_t, vb, lse, delta, scatter=()):
    s = kr.shape[0]
    tq = min(s, 256)
    kc = min(s, 512)
    nkc = s // kc
    nq = s // tq
    grp = A_HEADS // A_KV_HEADS
    ns = len(scatter)
    na = ns // 2

    def body(qt_ref, dot_ref, kr_ref, krt_ref, vb_ref, lse_ref, dl_ref, *rest):
        s_in, (dqt_ref, dk_ref, dv_ref), s_out = rest[:ns], rest[ns:ns + 3], rest[ns + 3:2 * ns + 3]
        qp_ref, dop_ref, dq_ref = rest[2 * ns + 3:2 * ns + 6]
        if ns:
            start, finish = scatter_stages([g.shape[1:] for g in scatter[:na]], s_in[:na], s_in[na:], s_out[:na], s_out[na:],
                                           *rest[2 * ns + 6:])
            pl.when(pl.program_id(0) == 0)(start)

        @pl.when(pl.program_id(0) == 0)
        def _():
            dk_ref[...] = jnp.zeros_like(dk_ref)
            dv_ref[...] = jnp.zeros_like(dv_ref)

        for h in range(A_HEADS):
            hs = slice(A_HEAD_DIM * h, A_HEAD_DIM * (h + 1))
            qp_ref[h] = _pad_head(qt_ref[hs, :], h // grp)
            dop_ref[h] = _pad_head(dot_ref[hs, :], h // grp)
        dq_ref[...] = jnp.zeros_like(dq_ref)

        def step(ci, carry):
            ks = pl.ds(pl.multiple_of(ci * kc, kc), kc)
            kblk, vblk, ktb = kr_ref[ks, :], vb_ref[ks, :], krt_ref[:, ks]
            dv_acc = jnp.zeros((kc, A_KV_WIDTH), F32)
            dk_acc = jnp.zeros((kc, A_KV_WIDTH), F32)
            scs = [_dot(kblk, qp_ref[h]) for h in range(A_HEADS)]
            dps = [_dot(vblk, dop_ref[h]) for h in range(A_HEADS)]
            for h in range(A_HEADS):
                qpad, dopad = qp_ref[h], dop_ref[h]
                p = jnp.exp2(scs[h] - lse_ref[h:h + 1, :])
                dsb = (p * (dps[h] - dl_ref[h:h + 1, :])).astype(BF16)
                dv_acc = dv_acc + _dot_nt(p.astype(BF16), dopad)
                dk_acc = dk_acc + _dot_nt(dsb, qpad)
                dq_ref[h] += _dot(ktb, dsb)
            dv_ref[ks, :] += dv_acc
            dk_ref[ks, :] += dk_acc
            return carry

        lax.fori_loop(0, nkc, step, 0)
        dqt_ref[...] = jnp.concatenate(
            [dq_ref[h][A_HEAD_DIM * (h // grp):A_HEAD_DIM * (h // grp + 1), :] for h in range(A_HEADS)], axis=0)
        if ns:
            pl.when(pl.program_id(0) == nq - 1)(finish)

    colq = pl.BlockSpec((A_WIDTH, tq), lambda i: (0, i))
    colh = pl.BlockSpec((A_HEADS, tq), lambda i: (0, i))
    out = pl.pallas_call(
        body,
        out_shape=(_sds((A_WIDTH, s), F32), _sds((s, A_KV_WIDTH), F32), _sds((s, A_KV_WIDTH), F32)) + scatter_out_shapes(scatter[:na]),
        grid=(nq,),
        in_specs=[colq, colq, _full((s, A_KV_WIDTH)), _full((A_KV_WIDTH, s)), _full((s, A_KV_WIDTH)), colh, colh] + [_ANY] * ns,
        out_specs=(colq, _full((s, A_KV_WIDTH)), _full((s, A_KV_WIDTH))) + (_ANY,) * ns,
        scratch_shapes=[pltpu.VMEM((A_HEADS, A_KV_WIDTH, tq), BF16), pltpu.VMEM((A_HEADS, A_KV_WIDTH, tq), BF16),
                        pltpu.VMEM((A_HEADS, A_KV_WIDTH, tq), F32)] + (scatter_sems(na) if ns else []),
        compiler_params=_cp("arbitrary"), name="attn_bwd_scatter" if ns else "attn_bwd")(q_t, do_t, kr, kr_t, vb, lse, delta, *scatter)
    return out[0], out[1], out[2], list(out[3:3 + na]), list(out[3 + na:])


def qk_prep_bwd(proj, dq_t, dkr, dvb, tabs, qg, kg, gq, gk, fold_q, fold_k):
    s = proj.shape[0]
    tm = min(s, 512)
    c, sa, sb = tabs

    def head_norm_bwd(x, dn, gain, gones, fold):
        ms = _group_sum(x * x, gones) * (1.0 / A_HEAD_DIM)
        r = lax.rsqrt(ms + EPS)
        xh = x * r
        gg = _dot_hi(jnp.sum(dn * xh, axis=0, keepdims=True), fold)
        u = dn * gain
        mean_u = _group_sum(u * xh, gones) * (1.0 / A_HEAD_DIM)
        return r * (u - xh * mean_u), gg

    def body(p_ref, dqt_ref, dk_ref, dv_ref, c_ref, sa_ref, sb_ref, qg_ref, kg_ref, gq_ref, gk_ref, fq_ref, fk_ref,
             dqkv_ref, gqg_ref, gkg_ref):
        @pl.when(pl.program_id(0) == 0)
        def _():
            gqg_ref[...] = jnp.zeros_like(gqg_ref)
            gkg_ref[...] = jnp.zeros_like(gkg_ref)

        cc, ssa, ssb = c_ref[...], sa_ref[...], sb_ref[...]
        dqr = dqt_ref[...].T * Q_SCALE
        dqn = _rope_t(dqr, _tile4(cc), _tile4(ssa), _tile4(ssb))
        dxq, gq_ = head_norm_bwd(p_ref[:, O_QA:O_QA + A_WIDTH].astype(F32), dqn, qg_ref[...], gq_ref[...], fq_ref[...])
        dkn = _rope_t(dk_ref[...], cc, ssa, ssb)
        dxk, gk_ = head_norm_bwd(p_ref[:, O_KA:O_KA + A_KV_WIDTH].astype(F32), dkn, kg_ref[...], gk_ref[...], fk_ref[...])
        gqg_ref[...] += gq_
        gkg_ref[...] += gk_
        dqkv_ref[:, O_QA:O_QA + A_WIDTH] = dxq.astype(BF16)
        dqkv_ref[:, O_KA:O_KA + A_KV_WIDTH] = dxk.astype(BF16)
        dqkv_ref[:, O_VA:O_VA + A_KV_WIDTH] = (dv_ref[...] * (1.0 / LN2)).astype(BF16)

    tab = _rows(tm, LANES)
    return pl.pallas_call(
        body, out_shape=(_sds((s, PBLK), BF16), _sds((1, LANES), F32), _sds((1, LANES), F32)), grid=(s // tm,),
        in_specs=[_rows(tm, PBLK), pl.BlockSpec((A_WIDTH, tm), lambda i: (0, i)), _rows(tm, A_KV_WIDTH), _rows(tm, A_KV_WIDTH),
                  tab, tab, tab, _full((1, A_WIDTH)), _full((1, A_KV_WIDTH)), _full((A_WIDTH, A_WIDTH)),
                  _full((A_KV_WIDTH, A_KV_WIDTH)), _full((A_WIDTH, LANES)), _full((A_KV_WIDTH, LANES))],
        out_specs=(_rows(tm, PBLK), _full((1, LANES)), _full((1, LANES))),
        compiler_params=_cp("arbitrary"), name="qk_prep_bwd")(proj, dq_t, dkr, dvb, c, sa, sb, qg, kg, gq, gk, fold_q, fold_k)


def _pick_dproj(b, d0, d1, d2, use):
    first_lg = 1 + MID_W // PBLK

    @pl.when(b == 0)
    def _():
        use(d0[...])

    @pl.when(jnp.logical_and(b >= 1, b < first_lg))
    def _():
        use(d1[...])

    @pl.when(b >= first_lg)
    def _():
        use(d2[...])


def win_grad(d0, d1, d2, h):
    s, d = h.shape
    tk = min(s, 1024)
    nk = s // tk

    def body(d0_ref, d1_ref, d2_ref, h_ref, o_ref, o16_ref):
        @pl.when(pl.program_id(1) == 0)
        def _():
            o_ref[...] = jnp.zeros_like(o_ref)

        def use(blk):
            o_ref[...] += _dot_tn(blk, h_ref[...])

        _pick_dproj(pl.program_id(0), d0_ref, d1_ref, d2_ref, use)

        @pl.when(pl.program_id(1) == nk - 1)
        def _():
            o16_ref[...] = o_ref[...].astype(BF16)

    def spec(first, count):
        def imap(j, k):
            used = jnp.logical_and(j >= first, j < first + count)
            return (jnp.where(used, k, 0), jnp.clip(j - first, 0, count - 1))
        return pl.BlockSpec((tk, PBLK), imap)

    nm = MID_W // PBLK
    oblk = pl.BlockSpec((PBLK, d), lambda j, k: (j, 0))
    return pl.pallas_call(
        body, out_shape=(_sds((IN_WIDTH, d), F32), _sds((IN_WIDTH, d), BF16)), grid=(N_PBLK, nk),
        in_specs=[spec(0, 1), spec(1, nm), spec(1 + nm, LG_W // PBLK), pl.BlockSpec((tk, d), lambda j, k: (k, 0))],
        out_specs=(oblk, oblk),
        compiler_params=_cp("parallel", "arbitrary"), name="win_grad")(d0, d1, d2, h)


def h_bwd(d0, d1, d2, w_t, x, dx_out, g, scatter=()):
    s, d = x.shape
    tm = min(s, 512)
    nt = s // tm
    ns = len(scatter)
    na = ns // 2

    def body(d0_ref, d1_ref, d2_ref, w_ref, x_ref, dxo_ref, g_ref, *rest):
        s_in, (dx_ref, gg_ref), s_out = rest[:ns], rest[ns:ns + 2], rest[ns + 2:2 * ns + 2]
        if ns:
            start, finish = scatter_stages([a.shape[1:] for a in scatter[:na]], s_in[:na], s_in[na:], s_out[:na], s_out[na:],
                                           *rest[2 * ns + 2:])
            pl.when(pl.program_id(0) == 0)(start)

        @pl.when(pl.program_id(0) == 0)
        def _():
            gg_ref[...] = jnp.zeros_like(gg_ref)

        dh = (_dot(d0_ref[...], w_ref[0:PBLK, :]) + _dot(d1_ref[...], w_ref[PBLK:PBLK + MID_W, :])
              + _dot(d2_ref[...], w_ref[PBLK + MID_W:, :]))
        xf = x_ref[...]
        r = lax.rsqrt(jnp.mean(xf * xf, axis=-1, keepdims=True) + EPS)
        xh = xf * r
        gg_ref[...] += jnp.sum(dh * xh, axis=0, keepdims=True)
        u = dh * g_ref[...]
        dx_ref[...] = dxo_ref[...] + r * (u - xh * jnp.mean(u * xh, axis=-1, keepdims=True))
        if ns:
            pl.when(pl.program_id(0) == nt - 1)(finish)

    rowb = _rows(tm, d)
    out = pl.pallas_call(
        body, out_shape=(_sds((s, d), F32), _sds((1, d), F32)) + scatter_out_shapes(scatter[:na]), grid=(nt,),
        in_specs=[_rows(tm, PBLK), _rows(tm, MID_W), _rows(tm, LG_W),
                  pl.BlockSpec(w_t.shape, lambda i: (0, 0), pipeline_mode=pl.Buffered(1)), rowb, rowb, _full((1, d))] + [_ANY] * ns,
        out_specs=(rowb, _full((1, d))) + (_ANY,) * ns,
        scratch_shapes=scatter_sems(na) if ns else [],
        compiler_params=_cp("arbitrary"), name="h_bwd_scatter" if ns else "h_bwd")(d0, d1, d2, w_t, x, dx_out, g, *scatter)
    return out[0], out[1], list(out[2:2 + na]), list(out[2 + na:])


def memkv_bwd(mem, g, mem_n, w_kv, dkv):
    m, d = mem.shape

    def body(mem_ref, g_ref, mn_ref, w_ref, dkv_ref, gw_ref, gw16_ref, gg_ref):
        dkb = dkv_ref[...].astype(BF16)
        gw = _dot_tn(mn_ref[...], dkb)
        gw_ref[...] = gw
        gw16_ref[...] = gw.astype(BF16)
        dmn = _dot_nt(dkb, w_ref[...])
        mf = mem_ref[...]
        r = lax.rsqrt(jnp.mean(mf * mf, axis=-1, keepdims=True) + EPS)
        gg_ref[...] = jnp.sum(dmn * (mf * r), axis=0, keepdims=True)

    return pl.pallas_call(
        body, out_shape=(_sds(w_kv.shape, F32), _sds(w_kv.shape, BF16), _sds((1, d), F32)),
        compiler_params=_cp(), name="memkv_bwd")(mem, g, mem_n, w_kv, dkv)


def _layer_consts(seq):
    i = jnp.arange(A_WIDTH)
    return dict(
        tabs=rope_tables(seq),
        gq=_group_ones(A_WIDTH, A_HEAD_DIM).astype(BF16), gk=_group_ones(A_KV_WIDTH, A_HEAD_DIM).astype(BF16),
        fold_q=(i[:, None] % A_HEAD_DIM == jnp.arange(LANES)[None, :]).astype(F32),
        fold_k=(i[:A_KV_WIDTH, None] % A_HEAD_DIM == jnp.arange(LANES)[None, :]).astype(F32),
        head_sel=(jnp.arange(A_HEADS)[:, None] == i[None, :] // A_HEAD_DIM).astype(F32),
    )


_BIG = ("win_t", "wkv", "wbr", "wout")


def _with_own_part(gathered, shards, chip):
    win_t, wkv, wbr, wout = [lax.dynamic_update_slice(g, sh[None], (chip, 0, 0)) for g, sh in zip(gathered, shards)]
    d = wout.shape[-1]
    return dict(win_t=win_t.reshape(IN_WIDTH, d), wkv=wkv.reshape(d, 2 * M_WIDTH),
                wbr=wbr.reshape(N_CHIPS, N_BRANCH, A_WIDTH, d // N_CHIPS), wout=wout.reshape(d, d))


def local_fwd_bwd(x, mem, tgt, small, big=None, shards=None, place=None):
    s, d = x.shape
    depth = small["norm_g"].shape[0]
    k = _layer_consts(s)
    row = lambda v: v.reshape(1, -1)
    dist = shards is not None
    if dist:
        big = [_with_own_part(allgather_layer(shards[0]), shards[0], place[0])] + [None] * (depth - 1)
    saved = []
    for l in range(depth):
        ng = row(small["norm_g"][l])
        qg = row(jnp.tile(small["q_norm_g"][l], A_HEADS))
        kg = row(jnp.tile(small["k_norm_g"][l], A_KV_HEADS))
        ws = small["w_s"][l].astype(BF16)
        ws_t = jnp.swapaxes(small["w_s"][l], 1, 2).astype(BF16)
        bsb = jnp.broadcast_to(small["b_s"][l][:, :, None], (B_GROUPS, CHUNK, B_GROUP_DIM))
        lng, lnb = row(small["sg_ln_g"][l]), row(small["sg_ln_b"][l])
        mg = row(small["mem_norm_g"][l])
        w = big[l]
        h = rms_fwd(x, ng)
        proj = proj_fwd(h, w["win_t"])
        q_t, kr, kr_t, vb, vte0, vte1 = qk_prep(proj, k["tabs"], qg, kg, k["gq"], k["gk"])
        nxt = shards[l + 1] if dist and l + 1 < depth else ()
        o_a, lse, gathered = attn_fwd(q_t, kr, vte0, vte1, gather=tuple(nxt))
        if nxt:
            big[l + 1] = _with_own_part(gathered, nxt, place[0])
        mem_n, kv = memkv_fwd(mem, mg, w["wkv"])
        x_next, y, up, merged = branch_fwd(x, proj, o_a, kv, ws, bsb, lng, lnb, w["wbr"], w["wout"])
        saved.append(dict(x=x, ng=ng, qg=qg, kg=kg, ws=ws, ws_t=ws_t, bsb=bsb, lng=lng, lnb=lnb, mg=mg, h=h, proj=proj,
                          q_t=q_t, kr=kr, kr_t=kr_t, vb=vb, o_a=o_a, lse=lse, mem_n=mem_n, kv=kv, y=y, up=up, merged=merged))
        x = x_next

    sq, dx, g_final = final_loss(x, row(small["final_g"]), tgt)
    grads = {n: [None] * depth for n in ("norm_g", "q_norm_g", "k_norm_g", "sg_ln_g", "sg_ln_b", "w_s", "b_s", "mem_norm_g")}
    parts = lambda g: g.reshape(N_CHIPS, -1, g.shape[-1])
    reduced = [[None] * len(_BIG) for _ in range(depth)]

    def reduce_all(items, t_sib, t_rem):
        for (ll, a, g, _), ts, tr in zip(items, t_sib, t_rem):
            reduced[ll][a] = reduce_rows(place, g, ts, tr)

    as_scatter = lambda items: tuple(i[2] for i in items) + tuple(i[3] for i in items)
    pending = []
    for l in reversed(range(depth)):
        sv, w = saved[l], big[l]
        dy, dlg, g_wout, g_wbr, g_wout16, g_wbr16 = merge_bwd(dx, sv["proj"], sv["y"], sv["up"], sv["merged"], w["wbr"], w["wout"])
        dmid, do_t, delta, g_ws, g_bs, g_lng, g_lnb, dkv = branch_bwd(
            dy, sv["proj"], sv["o_a"], sv["kv"], sv["ws"], sv["ws_t"], sv["bsb"], sv["lng"], sv["lnb"], k["head_sel"])
        g_wkv, g_wkv16, g_mg = memkv_bwd(mem, sv["mg"], sv["mem_n"], w["wkv"], dkv)
        if dist:
            pending += [(l, 1, parts(g_wkv), parts(g_wkv16)), (l, 2, parts(g_wbr), parts(g_wbr16)), (l, 3, parts(g_wout), parts(g_wout16))]
        dq_t, dkr, dvb, t_sib, t_rem = attn_bwd(sv["q_t"], do_t, sv["kr"], sv["kr_t"], sv["vb"], sv["lse"], delta,
                                                scatter=as_scatter(pending))
        reduce_all(pending, t_sib, t_rem)
        dqkv, g_qg, g_kg = qk_prep_bwd(sv["proj"], dq_t, dkr, dvb, k["tabs"], sv["qg"], sv["kg"], k["gq"], k["gk"],
                                       k["fold_q"], k["fold_k"])
        g_win, g_win16 = win_grad(dqkv, dmid, dlg, sv["h"])
        pending = [(l, 0, parts(g_win), parts(g_win16))] if dist else []
        last = as_scatter(pending) if l == 0 else ()
        dx, g_ng, t_sib, t_rem = h_bwd(dqkv, dmid, dlg, w["win_t"], sv["x"], dx, sv["ng"], scatter=last)
        if last:
            reduce_all(pending, t_sib, t_rem)
        grads["norm_g"][l] = g_ng[0]
        grads["q_norm_g"][l] = g_qg[0, :A_HEAD_DIM]
        grads["k_norm_g"][l] = g_kg[0, :A_HEAD_DIM]
        grads["sg_ln_g"][l] = g_lng[0]
        grads["sg_ln_b"][l] = g_lnb[0]
        grads["w_s"][l] = g_ws
        grads["b_s"][l] = g_bs[:, :, 0]
        grads["mem_norm_g"][l] = g_mg[0]
        if not dist:
            reduced[l] = dict(zip(_BIG, (parts(g_win), parts(g_wkv), parts(g_wbr), parts(g_wout))))
    grads = {n: jnp.stack(v) for n, v in grads.items()}
    grads["final_g"] = g_final[0]
    return sq[0, 0], dx, grads, reduced


def _row_block(rows, width, cap_bytes=2 * 2**20):
    best = None
    for br in range(8, rows + 1, 8):
        if rows % br == 0 and br * width * 4 <= cap_bytes:
            best = br
    return best if best is not None else rows


def adamw(w, g, m, v):
    r, c = w.shape
    br = _row_block(r, c)

    def body(w_ref, g_ref, m_ref, v_ref, d_ref, nm_ref, nv_ref):
        gg = g_ref[...]
        mm = ADAM_B1 * m_ref[...] + (1.0 - ADAM_B1) * gg
        vv = ADAM_B2 * v_ref[...] + (1.0 - ADAM_B2) * (gg * gg)
        m_hat = mm / (1.0 - ADAM_B1 ** ADAM_STEP)
        v_hat = vv / (1.0 - ADAM_B2 ** ADAM_STEP)
        d_ref[...] = -ADAM_LR * (m_hat / (jnp.sqrt(v_hat) + ADAM_EPS) + ADAM_WD * w_ref[...])
        nm_ref[...] = mm
        nv_ref[...] = vv

    blk = _rows(br, c)
    return pl.pallas_call(
        body, out_shape=(_sds((r, c), F32),) * 3, grid=(r // br,), in_specs=[blk] * 4, out_specs=(blk,) * 3,
        compiler_params=_cp("parallel"), name="adamw")(w, g, m, v)


N_REMOTE = 2 * (N_CHIPS - 1)


def reduce_rows(place, g, t_sib, t_rem):
    _, r, c = g.shape
    r2 = r // 2
    nt = 2 if r2 * c * 4 > 2**20 else 1
    tr = r2 // nt

    def body(place_ref, g_ref, s_ref, t_ref, f_ref):
        acc = g_ref[...] + s_ref[...]
        for j in range(N_REMOTE):
            acc = acc + t_ref[j].astype(F32)
        f_ref[...] = acc

    return pl.pallas_call(
        body, out_shape=_sds((r, c), F32),
        grid_spec=pltpu.PrefetchScalarGridSpec(
            num_scalar_prefetch=1, grid=(nt,),
            in_specs=[pl.BlockSpec((None, tr, c), lambda i, p: (p[0], p[1] * nt + i, 0)),
                      pl.BlockSpec((tr, c), lambda i, p: (i, 0)),
                      pl.BlockSpec((N_REMOTE, tr, c), lambda i, p: (0, i, 0))],
            out_specs=pl.BlockSpec((tr, c), lambda i, p: (p[1] * nt + i, 0))),
        compiler_params=_cp("parallel"), name="reduce_rows")(place, g, t_sib, t_rem)


_ANY = pl.BlockSpec(memory_space=pl.ANY)


def _place():
    x, y, c = lax.axis_index("x"), lax.axis_index("y"), lax.axis_index("c")
    chips = [(1 - x, y), (x, 1 - y), (1 - x, 1 - y)]
    return x, y, c, chips


def gather_sems(n):
    return [pltpu.SemaphoreType.DMA((n, N_REMOTE)), pltpu.SemaphoreType.DMA((n, N_REMOTE))]


def gather_stages(shapes, ins, outs, send, recv):
    n = len(shapes)
    x, y, c, chips = _place()
    me = 2 * x + y
    sib = (x, y, 1 - c)

    def rows(a, hl):
        r2 = shapes[a][0] // 2
        return pl.ds(hl * r2, r2)

    def remote(a, k, src, dst, dev):
        return pltpu.make_async_remote_copy(src, dst, send.at[a, k], recv.at[a, k], device_id=dev, device_id_type=MESH)

    def sent(a, k):
        cx, cy = chips[k]
        return remote(a, k, ins[a].at[rows(a, c)], outs[a].at[me, rows(a, c)], (cx, cy, c))

    def passed(a, k, hl):
        cx, cy = chips[k]
        got = outs[a].at[2 * cx + cy, rows(a, hl)]
        return remote(a, k, got, got, (cx, cy, c)), remote(a, 3 + k, got, got, sib)

    def start():
        for a in range(n):
            for k in range(3):
                sent(a, k).start()

    def forward():
        for k in range(3):
            for a in range(n):
                arrived, on = passed(a, k, c)
                arrived.wait_recv()
                on.start()

    def finish():
        for k in range(3):
            for a in range(n):
                passed(a, k, 1 - c)[1].wait_recv()
        for k in range(3):
            for a in range(n):
                sent(a, k).wait_send()
                passed(a, k, c)[1].wait_send()

    return start, forward, finish


def allgather_layer(shards):
    n = len(shards)

    def body(*refs):
        for stage in gather_stages([a.shape for a in shards], refs[:n], refs[n:2 * n], *refs[2 * n:]):
            stage()

    return pl.pallas_call(
        body, out_shape=tuple(_sds((N_CHIPS,) + a.shape, a.dtype) for a in shards),
        in_specs=[_ANY] * n, out_specs=(_ANY,) * n, scratch_shapes=gather_sems(n), name="allgather_layer")(*shards)


def scatter_sems(n):
    return [pltpu.SemaphoreType.DMA((n, N_REMOTE + 1)), pltpu.SemaphoreType.DMA((n, N_REMOTE + 1))]


def scatter_out_shapes(gs):
    return (tuple(_sds((g.shape[1] // 2, g.shape[2]), F32) for g in gs)
            + tuple(_sds((N_REMOTE, g.shape[1] // 2, g.shape[2]), BF16) for g in gs))


def scatter_stages(shapes, gf, gb, t_sib, t_rem, send, recv):
    n = len(shapes)
    x, y, c, chips = _place()
    me = 2 * x + y

    def copies():
        out = []
        for a in range(n):
            r2 = shapes[a][0] // 2
            out.append(pltpu.make_async_remote_copy(gf[a].at[me, pl.ds((1 - c) * r2, r2)], t_sib[a], send.at[a, N_REMOTE],
                                                    recv.at[a, N_REMOTE], device_id=(x, y, 1 - c), device_id_type=MESH))
            for k, (cx, cy) in enumerate(chips):
                for o in range(2):
                    tc = c if o == 0 else 1 - c
                    out.append(pltpu.make_async_remote_copy(gb[a].at[2 * cx + cy, pl.ds(tc * r2, r2)], t_rem[a].at[2 * k + o],
                                                            send.at[a, 2 * k + o], recv.at[a, 2 * k + o],
                                                            device_id=(cx, cy, tc), device_id_type=MESH))
        return out

    def start():
        for cp in copies():
            cp.start()

    def finish():
        for cp in copies():
            cp.wait()

    return start, finish


def share_final(fs):
    n = len(fs)

    def body(*refs):
        out = refs[n:2 * n]
        send, recv = refs[2 * n:]
        x, y, c, _ = _place()
        sib = (x, y, 1 - c)
        cps = []
        for a in range(n):
            r2 = fs[a].shape[0] // 2
            mine = out[a].at[pl.ds(c * r2, r2)]
            cp = pltpu.make_async_remote_copy(mine, mine, send.at[a], recv.at[a], device_id=sib, device_id_type=MESH)
            cp.start()
            cps.append(cp)
        for a, cp in enumerate(cps):
            r2 = fs[a].shape[0] // 2
            theirs = out[a].at[pl.ds((1 - c) * r2, r2)]
            cp.wait_send()
            pltpu.make_async_remote_copy(theirs, theirs, send.at[a], recv.at[a], device_id=sib, device_id_type=MESH).wait_recv()

    return pl.pallas_call(
        body, out_shape=tuple(_sds(f.shape, F32) for f in fs),
        in_specs=[_ANY] * n, out_specs=(_ANY,) * n, input_output_aliases={a: a for a in range(n)},
        scratch_shapes=[pltpu.SemaphoreType.DMA((n,)), pltpu.SemaphoreType.DMA((n,))],
        name="share_final")(*fs)


def allreduce_small(v):
    r, w = v.shape
    ndev = 2 * N_CHIPS

    def body(v_ref, sum_ref, all_ref, send, recv, loc):
        x, y, c, chips = _place()
        me, sib = (x, y, c), (x, y, 1 - c)

        def slab(px, py, pc):
            return all_ref.at[4 * px + 2 * py + pc]

        def copy(k, block, to, src=None):
            return pltpu.make_async_remote_copy(slab(*block) if src is None else src, slab(*block), send.at[k], recv.at[k],
                                                device_id=to, device_id_type=MESH)

        mine = pltpu.make_async_copy(v_ref, slab(*me), loc)
        mine.start()
        first = [copy(0, me, sib, src=v_ref)] + [copy(1 + j, me, (*chip, c), src=v_ref) for j, chip in enumerate(chips)]
        for cp in first:
            cp.start()
        passed = [copy(4 + j, (*chip, c), sib) for j, chip in enumerate(chips)]
        for j, chip in enumerate(chips):
            copy(1 + j, (*chip, c), me).wait_recv()
            passed[j].start()
        copy(0, sib, me).wait_recv()
        for j, chip in enumerate(chips):
            copy(4 + j, (*chip, 1 - c), me).wait_recv()
        for cp in first + passed:
            cp.wait_send()
        mine.wait()
        acc = all_ref[0]
        for i in range(1, ndev):
            acc = acc + all_ref[i]
        sum_ref[...] = acc

    vm = pl.BlockSpec(memory_space=pltpu.VMEM)
    return pl.pallas_call(
        body, out_shape=_sds((r, w), F32), in_specs=[vm], out_specs=vm,
        scratch_shapes=[pltpu.VMEM((ndev, r, w), F32), pltpu.SemaphoreType.DMA((7,)), pltpu.SemaphoreType.DMA((7,)),
                        pltpu.SemaphoreType.DMA],
        compiler_params=pltpu.CompilerParams(vmem_limit_bytes=VMEM_LIMIT), name="allreduce_small")(v)


_SMALL = ("norm_g", "q_norm_g", "k_norm_g", "sg_ln_g", "sg_ln_b", "w_s", "b_s", "mem_norm_g", "final_g")
_WEIGHTS = ("norm_g", "w_in", "q_norm_g", "k_norm_g", "sg_ln_g", "sg_ln_b", "w_s", "b_s", "mem_norm_g", "w_mem_kv", "w_br",
            "w_out", "final_g")


def _pack(d):
    flat = jnp.concatenate([d[n].reshape(-1) for n in _SMALL])
    rows = -(-flat.shape[0] // (8 * LANES)) * 8
    return jnp.pad(flat, (0, rows * LANES - flat.shape[0])).reshape(rows, LANES)


def _unpack(p, like):
    flat, out, o = p.reshape(-1), {}, 0
    for n in _SMALL:
        out[n] = flat[o:o + like[n].size].reshape(like[n].shape)
        o += like[n].size
    return out


def kernel(x, mem, norm_g, w_in, q_norm_g, k_norm_g, sg_ln_g, sg_ln_b, w_s, b_s, mem_norm_g, w_mem_kv, w_br, w_out, final_g, loss_target, m_norm_g, m_w_in, m_q_norm_g, m_k_norm_g, m_sg_ln_g, m_sg_ln_b, m_w_s, m_b_s, m_mem_norm_g, m_w_mem_kv, m_w_br, m_w_out, m_final_g, v_norm_g, v_w_in, v_q_norm_g, v_k_norm_g, v_sg_ln_g, v_sg_ln_b, v_w_s, v_b_s, v_mem_norm_g, v_w_mem_kv, v_w_br, v_w_out, v_final_g):
    w = dict(norm_g=norm_g, w_in=w_in, q_norm_g=q_norm_g, k_norm_g=k_norm_g, sg_ln_g=sg_ln_g, sg_ln_b=sg_ln_b, w_s=w_s, b_s=b_s,
             mem_norm_g=mem_norm_g, w_mem_kv=w_mem_kv, w_br=w_br, w_out=w_out, final_g=final_g)
    m = dict(norm_g=m_norm_g, w_in=m_w_in, q_norm_g=m_q_norm_g, k_norm_g=m_k_norm_g, sg_ln_g=m_sg_ln_g, sg_ln_b=m_sg_ln_b,
             w_s=m_w_s, b_s=m_b_s, mem_norm_g=m_mem_norm_g, w_mem_kv=m_w_mem_kv, w_br=m_w_br, w_out=m_w_out, final_g=m_final_g)
    v = dict(norm_g=v_norm_g, w_in=v_w_in, q_norm_g=v_q_norm_g, k_norm_g=v_k_norm_g, sg_ln_g=v_sg_ln_g, sg_ln_b=v_sg_ln_b,
             w_s=v_w_s, b_s=v_b_s, mem_norm_g=v_mem_norm_g, w_mem_kv=v_w_mem_kv, w_br=v_w_br, w_out=v_w_out, final_g=v_final_g)
    depth, d = norm_g.shape
    nsh = N_CHIPS
    br_rows = N_BRANCH * A_WIDTH
    br_cols = d // nsh

    shards = [[jnp.swapaxes(w_in[l], 0, 1).astype(BF16), w_mem_kv[l].astype(BF16), w_br[l].astype(BF16).reshape(br_rows, br_cols),
               w_out[l].astype(BF16)] for l in range(depth)]
    place = jnp.stack([2 * lax.axis_index("x") + lax.axis_index("y"), lax.axis_index("c")]).astype(jnp.int32)
    small = {n: w[n] for n in _SMALL}

    sq, dx, grads, reduced = local_fwd_bwd(x[0], mem[0], loss_target[0], small, shards=shards, place=place)
    loss = (0.5 / d) * lax.psum(sq, ("x", "y", "c"))

    finals = share_final([g for layer in reduced for g in layer])
    finals = [jnp.stack(finals[a::len(_BIG)]) for a in range(len(_BIG))]
    big_grads = dict(w_in=finals[0], w_mem_kv=finals[1], w_br=finals[2].reshape(depth, N_BRANCH, A_WIDTH, br_cols), w_out=finals[3])

    small_grads = _unpack(allreduce_small(_pack(grads)), small)

    out_g, out_d, out_m, out_v = {}, {}, {}, {}
    sd, sm, sv = adamw(_pack(small), _pack(small_grads), _pack({n: m[n] for n in _SMALL}), _pack({n: v[n] for n in _SMALL}))
    sd, sm, sv = _unpack(sd, small), _unpack(sm, small), _unpack(sv, small)
    for n in _SMALL:
        out_g[n], out_d[n], out_m[n], out_v[n] = small_grads[n], sd[n], sm[n], sv[n]
    for n, g in big_grads.items():
        into = (lambda a: jnp.swapaxes(a, 1, 2)) if n == "w_in" else (lambda a: a)
        two_d = lambda a: a.reshape(-1, a.shape[-1])
        res = adamw(two_d(into(w[n])), two_d(g), two_d(into(m[n])), two_d(into(v[n])))
        out_g[n], out_d[n], out_m[n], out_v[n] = [into(t.reshape(g.shape)) for t in (two_d(g),) + tuple(res)]
    return (loss, dx[None], *[out_g[n] for n in _WEIGHTS], *[out_d[n] for n in _WEIGHTS], *[out_m[n] for n in _WEIGHTS],
            *[out_v[n] for n in _WEIGHTS])
```

```python
import functools

import jax
import jax.numpy as jnp
from jax import lax
from jax.experimental import pallas as pl
from jax.experimental.pallas import tpu as pltpu

F32 = jnp.float32
BF16 = jnp.bfloat16

D_MODEL = 1024
GRID_W = 64
CHUNK = 128
ROPE_THETA = 10000.0
EPS = 1e-6
A_HEADS, A_KV_HEADS, A_HEAD_DIM = 8, 2, 64
A_WIDTH, A_KV_WIDTH = 512, 128
B_GROUPS, B_GROUP_DIM, B_WIDTH = 4, 128, 512
M_HEADS, M_HEAD_DIM, M_WIDTH = 4, 128, 512
N_BRANCH = 3
IN_WIDTH = 6912
O_QA, O_KA, O_VA, O_ZA, O_UB, O_VB, O_ZB, O_QM, O_ZM, O_LG = 0, 512, 640, 768, 1280, 1792, 2304, 2816, 3328, 3840
PBLK = 768
N_PBLK = IN_WIDTH // PBLK
MID_W = 3072
LG_W = 3072

LN2 = 0.6931471805599453
Q_SCALE = A_HEAD_DIM ** -0.5 / LN2

ADAM_LR, ADAM_B1, ADAM_B2, ADAM_EPS, ADAM_WD, ADAM_STEP = 0.001, 0.9, 0.999, 1e-08, 0.01, 10

V7X_VMEM_BYTES = 64 * 2**20
VMEM_LIMIT = V7X_VMEM_BYTES - 8 * 2**20
LANES = 128
MESH = pl.DeviceIdType.MESH
N_CHIPS = 4


def _cp(*sem):
    return pltpu.CompilerParams(dimension_semantics=sem if sem else None, vmem_limit_bytes=VMEM_LIMIT)


def _dot(a, b):
    return jnp.dot(a, b, preferred_element_type=F32)


def _dot_nt(a, b):
    return lax.dot_general(a, b, (((1,), (1,)), ((), ())), preferred_element_type=F32)


def _dot_tn(a, b):
    return lax.dot_general(a, b, (((0,), (0,)), ((), ())), preferred_element_type=F32)


def _dot_hi(a, b):
    return jnp.dot(a, b, preferred_element_type=F32, precision=lax.Precision.HIGHEST)


def _group_sum(a, ones):
    hi = a.astype(BF16)
    lo = (a - hi.astype(F32)).astype(BF16)
    return _dot(hi, ones) + _dot(lo, ones)


def _dot_nt_hi(a, b):
    return lax.dot_general(a, b, (((1,), (1,)), ((), ())), preferred_element_type=F32, precision=lax.Precision.HIGHEST)


def _sig(z):
    return 1.0 / (1.0 + jnp.exp(-z))


def _full(shape):
    nd = len(shape)
    return pl.BlockSpec(shape, lambda *_: (0,) * nd)


def _rows(tm, width):
    return pl.BlockSpec((tm, width), lambda i: (i, 0))


def _sds(shape, dtype):
    return jax.ShapeDtypeStruct(shape, dtype)


def rms_fwd(x, g):
    s, d = x.shape
    tm = min(s, 512)

    def body(x_ref, g_ref, h_ref):
        xf = x_ref[...]
        r = lax.rsqrt(jnp.mean(xf * xf, axis=-1, keepdims=True) + EPS)
        h_ref[...] = ((xf * r) * g_ref[...]).astype(BF16)

    return pl.pallas_call(
        body, out_shape=_sds((s, d), BF16), grid=(s // tm,),
        in_specs=[_rows(tm, d), _full((1, d))], out_specs=_rows(tm, d),
        compiler_params=_cp("parallel"), name="rms_fwd")(x, g)


def proj_fwd(h, w_t):
    s, d = h.shape
    n = w_t.shape[0]
    tm = min(s, 512)
    tn = 2304

    def body(h_ref, w_ref, o_ref):
        o_ref[...] = _dot_nt(h_ref[...], w_ref[...]).astype(BF16)

    return pl.pallas_call(
        body, out_shape=_sds((s, n), BF16), grid=(n // tn, s // tm),
        in_specs=[pl.BlockSpec((tm, d), lambda j, i: (i, 0)), pl.BlockSpec((tn, d), lambda j, i: (j, 0))],
        out_specs=pl.BlockSpec((tm, tn), lambda j, i: (i, j)),
        compiler_params=_cp("parallel", "parallel"), name="proj_fwd")(h, w_t)


def rope_tables(seq):
    rows = seq // GRID_W
    row = jnp.repeat(jnp.arange(rows, dtype=F32), GRID_W)
    col = jnp.tile(jnp.arange(GRID_W, dtype=F32), rows)
    n_freq = A_HEAD_DIM // 4
    inv = ROPE_THETA ** (-jnp.arange(n_freq, dtype=F32) / n_freq)
    ang = jnp.stack([row[:, None] * inv, col[:, None] * inv], axis=1)
    cos, sin = jnp.cos(ang), jnp.sin(ang)
    zero = jnp.zeros_like(sin[:, 0])
    c64 = jnp.concatenate([cos[:, 0], cos[:, 0], cos[:, 1], cos[:, 1]], axis=1)
    sa64 = jnp.concatenate([zero, sin[:, 0], zero, sin[:, 1]], axis=1)
    sb64 = jnp.concatenate([-sin[:, 0], zero, -sin[:, 1], zero], axis=1)
    two = lambda t: jnp.concatenate([t, t], axis=1)
    return two(c64), two(sa64), two(sb64)


def _group_ones(width, group):
    i = jnp.arange(width)
    return (i[:, None] // group == i[None, :] // group).astype(F32)


def _rope(xn, c, sa, sb):
    w = xn.shape[1]
    return xn * c + pltpu.roll(xn, 16, 1) * sa + pltpu.roll(xn, w - 16, 1) * sb


def _rope_t(dy, c, sa, sb):
    w = dy.shape[1]
    return dy * c + pltpu.roll(dy * sa, w - 16, 1) + pltpu.roll(dy * sb, 16, 1)


def _tile4(t):
    return jnp.concatenate([t, t, t, t], axis=1)


def qk_prep(proj, tabs, qg, kg, gq, gk):
    s = proj.shape[0]
    tm = min(s, 512)
    c, sa, sb = tabs

    def body(p_ref, c_ref, sa_ref, sb_ref, qg_ref, kg_ref, gq_ref, gk_ref, qt_ref, kr_ref, krt_ref, vb_ref, v0_ref, v1_ref):
        xq = p_ref[:, O_QA:O_QA + A_WIDTH].astype(F32)
        xk = p_ref[:, O_KA:O_KA + A_KV_WIDTH].astype(F32)
        xv = p_ref[:, O_VA:O_VA + A_KV_WIDTH].astype(F32)
        cc, ssa, ssb = c_ref[...], sa_ref[...], sb_ref[...]
        msq = _group_sum(xq * xq, gq_ref[...]) * (1.0 / A_HEAD_DIM)
        qn = (xq * lax.rsqrt(msq + EPS)) * qg_ref[...]
        qr = _rope(qn, _tile4(cc), _tile4(ssa), _tile4(ssb)) * Q_SCALE
        qt_ref[...] = qr.T.astype(BF16)
        msk = _group_sum(xk * xk, gk_ref[...]) * (1.0 / A_HEAD_DIM)
        kn = (xk * lax.rsqrt(msk + EPS)) * kg_ref[...]
        kr = _rope(kn, cc, ssa, ssb)
        kr_ref[...] = kr.astype(BF16)
        krt_ref[...] = kr.T.astype(BF16)
        vb_ref[...] = xv.astype(BF16)
        vt = xv.T.astype(BF16)
        one = jnp.ones((A_HEAD_DIM, tm), BF16)
        v0_ref[...] = jnp.concatenate([vt[:A_HEAD_DIM], one], axis=0)
        v1_ref[...] = jnp.concatenate([one, vt[A_HEAD_DIM:]], axis=0)

    tab = _rows(tm, LANES)
    colb = lambda w: pl.BlockSpec((w, tm), lambda i: (0, i))
    return pl.pallas_call(
        body,
        out_shape=(_sds((A_WIDTH, s), BF16), _sds((s, A_KV_WIDTH), BF16), _sds((A_KV_WIDTH, s), BF16),
                   _sds((s, A_KV_WIDTH), BF16), _sds((A_KV_WIDTH, s), BF16), _sds((A_KV_WIDTH, s), BF16)),
        grid=(s // tm,),
        in_specs=[_rows(tm, PBLK), tab, tab, tab, _full((1, A_WIDTH)), _full((1, A_KV_WIDTH)),
                  _full((A_WIDTH, A_WIDTH)), _full((A_KV_WIDTH, A_KV_WIDTH))],
        out_specs=(colb(A_WIDTH), _rows(tm, A_KV_WIDTH), colb(A_KV_WIDTH), _rows(tm, A_KV_WIDTH), colb(A_KV_WIDTH),
                   colb(A_KV_WIDTH)),
        compiler_params=_cp("parallel"), name="qk_prep")(proj, c, sa, sb, qg, kg, gq, gk)


def _pad_head(q_h, kv):
    z = jnp.zeros_like(q_h)
    return jnp.concatenate([q_h, z], axis=0) if kv == 0 else jnp.concatenate([z, q_h], axis=0)


def attn_fwd(q_t, kr, vte0, vte1, gather=()):
    s = kr.shape[0]
    tq = min(s, 256)
    kc = min(s, 512)
    nkc = s // kc
    nq = s // tq
    grp = A_HEADS // A_KV_HEADS
    ng = len(gather)

    def body(qt_ref, kr_ref, v0_ref, v1_ref, *rest):
        g_in, (o_ref, lse_ref), g_out = rest[:ng], rest[ng:ng + 2], rest[ng + 2:2 * ng + 2]
        qp_ref, m_ref, acc_ref = rest[2 * ng + 2:2 * ng + 5]
        if ng:
            start, forward, finish = gather_stages([g.shape for g in gather], g_in, g_out, *rest[2 * ng + 5:])
            pl.when(pl.program_id(0) == 0)(start)
            pl.when(pl.program_id(0) == (3 * nq) // 4)(forward)

        for h in range(A_HEADS):
            qp_ref[h] = _pad_head(qt_ref[A_HEAD_DIM * h:A_HEAD_DIM * (h + 1), :], h // grp)
        m_ref[...] = jnp.full(m_ref.shape, -1e30, F32)
        acc_ref[...] = jnp.zeros_like(acc_ref)

        def step(ci, carry):
            ks = pl.ds(pl.multiple_of(ci * kc, kc), kc)
            kblk = kr_ref[ks, :]
            vts = (v0_ref[:, ks], v1_ref[:, ks])
            scs = [_dot(kblk, qp_ref[h]) for h in range(A_HEADS)]
            for h in range(A_HEADS):
                sc = scs[h]
                m_prev = m_ref[h:h + 1, :]
                m_new = jnp.maximum(m_prev, jnp.max(sc, axis=0, keepdims=True))
                p = jnp.exp2(sc - m_new)
                acc_ref[h] = acc_ref[h] * jnp.exp2(m_prev - m_new) + _dot(vts[h // grp], p.astype(BF16))
                m_ref[h:h + 1, :] = m_new
            return carry

        lax.fori_loop(0, nkc, step, 0)
        outs, lses = [], []
        for h in range(A_HEADS):
            kv = h // grp
            acc = acc_ref[h]
            l = acc[A_HEAD_DIM * (1 - kv):A_HEAD_DIM * (1 - kv) + 1, :]
            outs.append(acc[A_HEAD_DIM * kv:A_HEAD_DIM * (kv + 1), :] / l)
            lses.append(m_ref[h:h + 1, :] + jnp.log2(l))
        o_ref[...] = jnp.concatenate(outs, axis=0).T
        lse_ref[...] = jnp.concatenate(lses, axis=0)
        if ng:
            pl.when(pl.program_id(0) == nq - 1)(finish)

    out = pl.pallas_call(
        body,
        out_shape=(_sds((s, A_WIDTH), F32), _sds((A_HEADS, s), F32)) + tuple(_sds((N_CHIPS,) + g.shape, g.dtype) for g in gather),
        grid=(nq,),
        in_specs=[pl.BlockSpec((A_WIDTH, tq), lambda i: (0, i)), _full((s, A_KV_WIDTH)), _full((A_KV_WIDTH, s)),
                  _full((A_KV_WIDTH, s))] + [_ANY] * ng,
        out_specs=(_rows(tq, A_WIDTH), pl.BlockSpec((A_HEADS, tq), lambda i: (0, i))) + (_ANY,) * ng,
        scratch_shapes=[pltpu.VMEM((A_HEADS, A_KV_WIDTH, tq), BF16), pltpu.VMEM((A_HEADS, tq), F32),
                        pltpu.VMEM((A_HEADS, A_KV_WIDTH, tq), F32)] + (gather_sems(ng) if ng else []),
        compiler_params=_cp("arbitrary"), name="attn_fwd_gather" if ng else "attn_fwd")(q_t, kr, vte0, vte1, *gather)
    return out[0], out[1], list(out[2:])


def memkv_fwd(mem, g, w_kv):
    m, d = mem.shape

    def body(mem_ref, g_ref, w_ref, mn_ref, kv_ref):
        mf = mem_ref[...]
        r = lax.rsqrt(jnp.mean(mf * mf, axis=-1, keepdims=True) + EPS)
        mn = ((mf * r) * g_ref[...]).astype(BF16)
        mn_ref[...] = mn
        kv_ref[...] = _dot(mn, w_ref[...]).astype(BF16)

    return pl.pallas_call(
        body, out_shape=(_sds((m, d), BF16), _sds((m, 2 * M_WIDTH), BF16)),
        compiler_params=_cp(), name="memkv_fwd")(mem, g, w_kv)


def _layer_norm_stats(v):
    mu = jnp.mean(v, axis=-1, keepdims=True)
    xc = v - mu
    rstd = lax.rsqrt(jnp.mean(xc * xc, axis=-1, keepdims=True) + EPS)
    return xc * rstd, rstd


def _spatial_mix(vlb, ws_ref, bsb_ref, tm):
    rows = []
    for ci in range(tm // CHUNK):
        cols = []
        for g in range(B_GROUPS):
            blk = vlb[ci * CHUNK:(ci + 1) * CHUNK, g * B_GROUP_DIM:(g + 1) * B_GROUP_DIM]
            cols.append(_dot(ws_ref[g], blk) + bsb_ref[g])
        rows.append(jnp.concatenate(cols, axis=1))
    return jnp.concatenate(rows, axis=0)


def _mem_attn(qm, kv_ref):
    out = []
    for h in range(M_HEADS):
        qh = qm[:, h * M_HEAD_DIM:(h + 1) * M_HEAD_DIM].astype(BF16)
        kh = kv_ref[:, h * M_HEAD_DIM:(h + 1) * M_HEAD_DIM]
        vh = kv_ref[:, M_WIDTH + h * M_HEAD_DIM:M_WIDTH + (h + 1) * M_HEAD_DIM]
        sc = _dot_nt(qh, kh) * (M_HEAD_DIM ** -0.5)
        e = jnp.exp(sc - jnp.max(sc, axis=-1, keepdims=True))
        p = e / jnp.sum(e, axis=-1, keepdims=True)
        out.append((p, _dot(p.astype(BF16), vh)))
    return out


def branch_fwd(x, proj, o_a, kv, ws, bsb, ln_g, ln_b, w_br, w_out, next_g):
    s, d = x.shape
    tm = min(s, 256)

    def body(x_ref, p_ref, oa_ref, kv_ref, ws_ref, bsb_ref, lg_ref, lb_ref, wbr_ref, wo_ref, ng_ref,
             xn_ref, y_ref, up_ref, mg_ref, hn_ref):
        seg = lambda o, w: p_ref[:, o:o + w].astype(F32)
        z_a, u_b, v_b, z_b = seg(O_ZA, A_WIDTH), seg(O_UB, B_WIDTH), seg(O_VB, B_WIDTH), seg(O_ZB, B_WIDTH)
        q_m, z_m = seg(O_QM, M_WIDTH), seg(O_ZM, M_WIDTH)
        xhat, _ = _layer_norm_stats(v_b)
        vln = xhat * lg_ref[...] + lb_ref[...]
        mixed = _spatial_mix(vln.astype(BF16), ws_ref, bsb_ref, tm)
        y_b = (u_b * mixed) * (z_b * _sig(z_b))
        o_m = jnp.concatenate([o for _, o in _mem_attn(q_m, kv_ref)], axis=1)
        y_a = oa_ref[...] * (z_a * _sig(z_a))
        y_m = o_m * (z_m * _sig(z_m))
        merged = None
        for n, yy in enumerate((y_a, y_b, y_m)):
            yb = yy.astype(BF16)
            y_ref[n] = yb
            up = jnp.concatenate([_dot(yb, wbr_ref[c, n]) for c in range(N_CHIPS)], axis=1)
            up_ref[n] = up.astype(BF16)
            t = _sig(seg(O_LG + n * d, d)) * up
            merged = t if merged is None else merged + t
        mb = merged.astype(BF16)
        mg_ref[...] = mb
        xn = x_ref[...] + _dot(mb, wo_ref[...])
        xn_ref[...] = xn
        r = lax.rsqrt(jnp.mean(xn * xn, axis=-1, keepdims=True) + EPS)
        hn_ref[...] = ((xn * r) * ng_ref[...]).astype(BF16)

    return pl.pallas_call(
        body,
        out_shape=(_sds((s, d), F32), _sds((N_BRANCH, s, A_WIDTH), BF16), _sds((N_BRANCH, s, d), BF16), _sds((s, d), BF16),
                   _sds((s, d), BF16)),
        grid=(s // tm,),
        in_specs=[_rows(tm, d), _rows(tm, IN_WIDTH), _rows(tm, A_WIDTH), _full(kv.shape), _full(ws.shape), _full(bsb.shape),
                  _full((1, B_WIDTH)), _full((1, B_WIDTH)), _full(w_br.shape), _full(w_out.shape), _full((1, d))],
        out_specs=(_rows(tm, d), pl.BlockSpec((N_BRANCH, tm, A_WIDTH), lambda i: (0, i, 0)),
                   pl.BlockSpec((N_BRANCH, tm, d), lambda i: (0, i, 0)), _rows(tm, d), _rows(tm, d)),
        compiler_params=_cp("parallel"), name="branch_fwd")(x, proj, o_a, kv, ws, bsb, ln_g, ln_b, w_br, w_out, next_g)


def final_loss(x, fg, tgt):
    s, d = x.shape
    tm = min(s, 512)

    def body(x_ref, g_ref, t_ref, ls_ref, dx_ref, gg_ref):
        @pl.when(pl.program_id(0) == 0)
        def _():
            ls_ref[...] = jnp.zeros_like(ls_ref)
            gg_ref[...] = jnp.zeros_like(gg_ref)

        xf = x_ref[...]
        g = g_ref[...]
        r = lax.rsqrt(jnp.mean(xf * xf, axis=-1, keepdims=True) + EPS)
        xh = xf * r
        e = xh * g - t_ref[...]
        sq = jnp.sum(jnp.sum(e * e, axis=0, keepdims=True), axis=1, keepdims=True)
        ls_ref[...] += jnp.broadcast_to(sq, ls_ref.shape)
        dy = e * (1.0 / d)
        gg_ref[...] += jnp.sum(dy * xh, axis=0, keepdims=True)
        gy = dy * g
        dx_ref[...] = r * (gy - xh * jnp.mean(gy * xh, axis=-1, keepdims=True))

    return pl.pallas_call(
        body, out_shape=(_sds((1, LANES), F32), _sds((s, d), F32), _sds((1, d), F32)), grid=(s // tm,),
        in_specs=[_rows(tm, d), _full((1, d)), _rows(tm, d)],
        out_specs=(_full((1, LANES)), _rows(tm, d), _full((1, d))),
        compiler_params=_cp("arbitrary"), name="final_loss")(x, fg, tgt)


def _pblocks(tm, first, count):
    return [pl.BlockSpec((tm, PBLK), functools.partial(lambda i, b: (i, b), b=first + k)) for k in range(count)]


def merge_bwd(dx, proj, y, up, merged, w_br, w_out):
    s, d = dx.shape
    tm = min(s, 256)
    nlg = LG_W // PBLK
    cw = d // N_CHIPS

    def body(dx_ref, l0, l1, l2, l3, y_ref, up_ref, mg_ref, wbr_ref, wo_ref, dy_ref, dlg_ref, gwo_ref, gwb_ref, gwo16_ref, gwb16_ref):
        @pl.when(pl.program_id(0) == 0)
        def _():
            gwo_ref[...] = jnp.zeros_like(gwo_ref)
            gwb_ref[...] = jnp.zeros_like(gwb_ref)

        dxb = dx_ref[...].astype(BF16)
        dmg = _dot_nt(dxb, wo_ref[...])
        gwo_ref[...] += _dot_tn(mg_ref[...], dxb)
        lg = jnp.concatenate([l0[...], l1[...], l2[...], l3[...]], axis=1).astype(F32)
        for n in range(N_BRANCH):
            g = _sig(lg[:, n * d:(n + 1) * d])
            dup = dmg * g
            dlg_ref[:, n * d:(n + 1) * d] = ((dup * up_ref[n].astype(F32)) * (1.0 - g)).astype(BF16)
            dupb = dup.astype(BF16)
            dyn = None
            for c in range(N_CHIPS):
                blk = dupb[:, c * cw:(c + 1) * cw]
                gwb_ref[c, n] += _dot_tn(y_ref[n], blk)
                t = _dot_nt(blk, wbr_ref[c, n])
                dyn = t if dyn is None else dyn + t
            dy_ref[n] = dyn

        @pl.when(pl.program_id(0) == pl.num_programs(0) - 1)
        def _():
            gwo16_ref[...] = gwo_ref[...].astype(BF16)
            gwb16_ref[...] = gwb_ref[...].astype(BF16)

    return pl.pallas_call(
        body,
        out_shape=(_sds((N_BRANCH, s, A_WIDTH), F32), _sds((s, LG_W), BF16), _sds((d, d), F32), _sds(w_br.shape, F32),
                   _sds((d, d), BF16), _sds(w_br.shape, BF16)),
        grid=(s // tm,),
        in_specs=[_rows(tm, d)] + _pblocks(tm, O_LG // PBLK, nlg) + [
            pl.BlockSpec((N_BRANCH, tm, A_WIDTH), lambda i: (0, i, 0)), pl.BlockSpec((N_BRANCH, tm, d), lambda i: (0, i, 0)),
            _rows(tm, d), _full(w_br.shape), _full(w_out.shape)],
        out_specs=(pl.BlockSpec((N_BRANCH, tm, A_WIDTH), lambda i: (0, i, 0)), _rows(tm, LG_W), _full((d, d)), _full(w_br.shape),
                   _full((d, d)), _full(w_br.shape)),
        compiler_params=_cp("arbitrary"), name="merge_bwd")(dx, proj, proj, proj, proj, y, up, merged, w_br, w_out)


def _dsilu(z, sg):
    return sg * (1.0 + z * (1.0 - sg))


def branch_bwd(dy, proj, o_a, kv, ws, ws_t, bsb, ln_g, ln_b, head_sel):
    s = proj.shape[0]
    tm = min(s, 256)
    nmid = MID_W // PBLK

    def body(dy_ref, m0, m1, m2, m3, oa_ref, kv_ref, ws_ref, wst_ref, bsb_ref, lg_ref, lb_ref, sel_ref,
             dmid_ref, dot_ref, dl_ref, gws_ref, gbs_ref, glg_ref, glb_ref, dkv_ref):
        @pl.when(pl.program_id(0) == 0)
        def _():
            for r in (gws_ref, gbs_ref, glg_ref, glb_ref, dkv_ref):
                r[...] = jnp.zeros_like(r)

        mid = jnp.concatenate([m0[...], m1[...], m2[...], m3[...]], axis=1).astype(F32)
        seg = lambda o, w: mid[:, o - O_ZA:o - O_ZA + w]
        z_a, u_b, v_b, z_b = seg(O_ZA, A_WIDTH), seg(O_UB, B_WIDTH), seg(O_VB, B_WIDTH), seg(O_ZB, B_WIDTH)
        q_m, z_m = seg(O_QM, M_WIDTH), seg(O_ZM, M_WIDTH)

        def put(o, v):
            dmid_ref[:, o - O_ZA:o - O_ZA + v.shape[1]] = v.astype(BF16)

        dy_a, dy_b, dy_m = dy_ref[0], dy_ref[1], dy_ref[2]

        o_a_ = oa_ref[...]
        sg = _sig(z_a)
        do_a = dy_a * (z_a * sg)
        put(O_ZA, (dy_a * o_a_) * _dsilu(z_a, sg))
        do_l = do_a * LN2
        dot_ref[...] = do_l.T.astype(BF16)
        dl_ref[...] = _dot_nt_hi(sel_ref[...], do_l * o_a_)

        xhat, rstd = _layer_norm_stats(v_b)
        lng = lg_ref[...]
        vln = xhat * lng + lb_ref[...]
        vlb = vln.astype(BF16)
        mixed = _spatial_mix(vlb, ws_ref, bsb_ref, tm)
        sg = _sig(z_b)
        sl = z_b * sg
        put(O_UB, (dy_b * mixed) * sl)
        put(O_ZB, ((dy_b * u_b) * mixed) * _dsilu(z_b, sg))
        dmix = (dy_b * u_b) * sl
        dmb = dmix.astype(BF16)
        rows = []
        for ci in range(tm // CHUNK):
            cols = []
            for g in range(B_GROUPS):
                rs, cs = slice(ci * CHUNK, (ci + 1) * CHUNK), slice(g * B_GROUP_DIM, (g + 1) * B_GROUP_DIM)
                gws_ref[g] += _dot_nt(dmb[rs, cs], vlb[rs, cs])
                gbs_ref[g] += jnp.broadcast_to(jnp.sum(dmix[rs, cs], axis=1, keepdims=True), (CHUNK, B_GROUP_DIM))
                cols.append(_dot(wst_ref[g], dmb[rs, cs]))
            rows.append(jnp.concatenate(cols, axis=1))
        dvln = jnp.concatenate(rows, axis=0)
        glg_ref[...] += jnp.sum(dvln * xhat, axis=0, keepdims=True)
        glb_ref[...] += jnp.sum(dvln, axis=0, keepdims=True)
        gy = dvln * lng
        put(O_VB, rstd * ((gy - jnp.mean(gy, axis=-1, keepdims=True)) - xhat * jnp.mean(gy * xhat, axis=-1, keepdims=True)))

        sg = _sig(z_m)
        sl = z_m * sg
        heads = _mem_attn(q_m, kv_ref)
        o_m = jnp.concatenate([o for _, o in heads], axis=1)
        put(O_ZM, (dy_m * o_m) * _dsilu(z_m, sg))
        do_m = dy_m * sl
        dqs = []
        for h, (p, o_h) in enumerate(heads):
            hs = slice(h * M_HEAD_DIM, (h + 1) * M_HEAD_DIM)
            vs = slice(M_WIDTH + h * M_HEAD_DIM, M_WIDTH + (h + 1) * M_HEAD_DIM)
            do_h = do_m[:, hs]
            dob = do_h.astype(BF16)
            dp = _dot_nt(dob, kv_ref[:, vs])
            dsc = (p * (dp - jnp.sum(do_h * o_h, axis=-1, keepdims=True))) * (M_HEAD_DIM ** -0.5)
            dsb = dsc.astype(BF16)
            dqs.append(_dot(dsb, kv_ref[:, hs]))
            dkv_ref[:, hs] += _dot_tn(dsb, q_m[:, hs].astype(BF16))
            dkv_ref[:, vs] += _dot_tn(p.astype(BF16), dob)
        put(O_QM, jnp.concatenate(dqs, axis=1))

    return pl.pallas_call(
        body,
        out_shape=(_sds((s, MID_W), BF16), _sds((A_WIDTH, s), BF16), _sds((A_HEADS, s), F32), _sds(ws.shape, F32),
                   _sds(ws.shape, F32), _sds((1, B_WIDTH), F32), _sds((1, B_WIDTH), F32), _sds(kv.shape, F32)),
        grid=(s // tm,),
        in_specs=[pl.BlockSpec((N_BRANCH, tm, A_WIDTH), lambda i: (0, i, 0))] + _pblocks(tm, O_ZA // PBLK, nmid) + [
            _rows(tm, A_WIDTH), _full(kv.shape), _full(ws.shape), _full(ws.shape), _full(bsb.shape),
            _full((1, B_WIDTH)), _full((1, B_WIDTH)), _full(head_sel.shape)],
        out_specs=(_rows(tm, MID_W), pl.BlockSpec((A_WIDTH, tm), lambda i: (0, i)), pl.BlockSpec((A_HEADS, tm), lambda i: (0, i)),
                   _full(ws.shape), _full(ws.shape), _full((1, B_WIDTH)), _full((1, B_WIDTH)), _full(kv.shape)),
        compiler_params=_cp("arbitrary"), name="branch_bwd")(dy, proj, proj, proj, proj, o_a, kv, ws, ws_t, bsb, ln_g, ln_b, head_sel)


def attn_bwd(q_t, do_t, kr, kr_t, vb, lse, delta, scatter=()):
    s = kr.shape[0]
    tq = min(s, 256)
    kc = min(s, 512)
    nkc = s // kc
    nq = s // tq
    grp = A_HEADS // A_KV_HEADS
    ns = len(scatter)
    na = ns // 2

    def body(qt_ref, dot_ref, kr_ref, krt_ref, vb_ref, lse_ref, dl_ref, *rest):
        s_in, (dqt_ref, dk_ref, dv_ref), s_out = rest[:ns], rest[ns:ns + 3], rest[ns + 3:2 * ns + 3]
        qp_ref, dop_ref, dq_ref = rest[2 * ns + 3:2 * ns + 6]
        if ns:
            start, finish = scatter_stages([g.shape[1:] for g in scatter[:na]], s_in[:na], s_in[na:], s_out[:na], s_out[na:],
                                           *rest[2 * ns + 6:])
            pl.when(pl.program_id(0) == 0)(start)

        @pl.when(pl.program_id(0) == 0)
        def _():
            dk_ref[...] = jnp.zeros_like(dk_ref)
            dv_ref[...] = jnp.zeros_like(dv_ref)

        for h in range(A_HEADS):
            hs = slice(A_HEAD_DIM * h, A_HEAD_DIM * (h + 1))
            qp_ref[h] = _pad_head(qt_ref[hs, :], h // grp)
            dop_ref[h] = _pad_head(dot_ref[hs, :], h // grp)
        dq_ref[...] = jnp.zeros_like(dq_ref)

        def step(ci, carry):
            ks = pl.ds(pl.multiple_of(ci * kc, kc), kc)
            kblk, vblk, ktb = kr_ref[ks, :], vb_ref[ks, :], krt_ref[:, ks]
            dv_acc = jnp.zeros((kc, A_KV_WIDTH), F32)
            dk_acc = jnp.zeros((kc, A_KV_WIDTH), F32)
            scs = [_dot(kblk, qp_ref[h]) for h in range(A_HEADS)]
            dps = [_dot(vblk, dop_ref[h]) for h in range(A_HEADS)]
            for h in range(A_HEADS):
                qpad, dopad = qp_ref[h], dop_ref[h]
                p = jnp.exp2(scs[h] - lse_ref[h:h + 1, :])
                dsb = (p * (dps[h] - dl_ref[h:h + 1, :])).astype(BF16)
                dv_acc = dv_acc + _dot_nt(p.astype(BF16), dopad)
                dk_acc = dk_acc + _dot_nt(dsb, qpad)
                dq_ref[h] += _dot(ktb, dsb)
            dv_ref[ks, :] += dv_acc
            dk_ref[ks, :] += dk_acc
            return carry

        lax.fori_loop(0, nkc, step, 0)
        dqt_ref[...] = jnp.concatenate(
            [dq_ref[h][A_HEAD_DIM * (h // grp):A_HEAD_DIM * (h // grp + 1), :] for h in range(A_HEADS)], axis=0)
        if ns:
            pl.when(pl.program_id(0) == nq - 1)(finish)

    colq = pl.BlockSpec((A_WIDTH, tq), lambda i: (0, i))
    colh = pl.BlockSpec((A_HEADS, tq), lambda i: (0, i))
    out = pl.pallas_call(
        body,
        out_shape=(_sds((A_WIDTH, s), F32), _sds((s, A_KV_WIDTH), F32), _sds((s, A_KV_WIDTH), F32)) + scatter_out_shapes(scatter[:na]),
        grid=(nq,),
        in_specs=[colq, colq, _full((s, A_KV_WIDTH)), _full((A_KV_WIDTH, s)), _full((s, A_KV_WIDTH)), colh, colh] + [_ANY] * ns,
        out_specs=(colq, _full((s, A_KV_WIDTH)), _full((s, A_KV_WIDTH))) + (_ANY,) * ns,
        scratch_shapes=[pltpu.VMEM((A_HEADS, A_KV_WIDTH, tq), BF16), pltpu.VMEM((A_HEADS, A_KV_WIDTH, tq), BF16),
                        pltpu.VMEM((A_HEADS, A_KV_WIDTH, tq), F32)] + (scatter_sems(na) if ns else []),
        compiler_params=_cp("arbitrary"), name="attn_bwd_scatter" if ns else "attn_bwd")(q_t, do_t, kr, kr_t, vb, lse, delta, *scatter)
    return out[0], out[1], out[2], list(out[3:3 + na]), list(out[3 + na:])


def qk_prep_bwd(proj, dq_t, dkr, dvb, tabs, qg, kg, gq, gk, fold_q, fold_k):
    s = proj.shape[0]
    tm = min(s, 512)
    c, sa, sb = tabs

    def head_norm_bwd(x, dn, gain, gones, fold):
        ms = _group_sum(x * x, gones) * (1.0 / A_HEAD_DIM)
        r = lax.rsqrt(ms + EPS)
        xh = x * r
        gg = _dot_hi(jnp.sum(dn * xh, axis=0, keepdims=True), fold)
        u = dn * gain
        mean_u = _group_sum(u * xh, gones) * (1.0 / A_HEAD_DIM)
        return r * (u - xh * mean_u), gg

    def body(p_ref, dqt_ref, dk_ref, dv_ref, c_ref, sa_ref, sb_ref, qg_ref, kg_ref, gq_ref, gk_ref, fq_ref, fk_ref,
             dqkv_ref, gqg_ref, gkg_ref):
        @pl.when(pl.program_id(0) == 0)
        def _():
            gqg_ref[...] = jnp.zeros_like(gqg_ref)
            gkg_ref[...] = jnp.zeros_like(gkg_ref)

        cc, ssa, ssb = c_ref[...], sa_ref[...], sb_ref[...]
        dqr = dqt_ref[...].T * Q_SCALE
        dqn = _rope_t(dqr, _tile4(cc), _tile4(ssa), _tile4(ssb))
        dxq, gq_ = head_norm_bwd(p_ref[:, O_QA:O_QA + A_WIDTH].astype(F32), dqn, qg_ref[...], gq_ref[...], fq_ref[...])
        dkn = _rope_t(dk_ref[...], cc, ssa, ssb)
        dxk, gk_ = head_norm_bwd(p_ref[:, O_KA:O_KA + A_KV_WIDTH].astype(F32), dkn, kg_ref[...], gk_ref[...], fk_ref[...])
        gqg_ref[...] += gq_
        gkg_ref[...] += gk_
        dqkv_ref[:, O_QA:O_QA + A_WIDTH] = dxq.astype(BF16)
        dqkv_ref[:, O_KA:O_KA + A_KV_WIDTH] = dxk.astype(BF16)
        dqkv_ref[:, O_VA:O_VA + A_KV_WIDTH] = (dv_ref[...] * (1.0 / LN2)).astype(BF16)

    tab = _rows(tm, LANES)
    return pl.pallas_call(
        body, out_shape=(_sds((s, PBLK), BF16), _sds((1, LANES), F32), _sds((1, LANES), F32)), grid=(s // tm,),
        in_specs=[_rows(tm, PBLK), pl.BlockSpec((A_WIDTH, tm), lambda i: (0, i)), _rows(tm, A_KV_WIDTH), _rows(tm, A_KV_WIDTH),
                  tab, tab, tab, _full((1, A_WIDTH)), _full((1, A_KV_WIDTH)), _full((A_WIDTH, A_WIDTH)),
                  _full((A_KV_WIDTH, A_KV_WIDTH)), _full((A_WIDTH, LANES)), _full((A_KV_WIDTH, LANES))],
        out_specs=(_rows(tm, PBLK), _full((1, LANES)), _full((1, LANES))),
        compiler_params=_cp("arbitrary"), name="qk_prep_bwd")(proj, dq_t, dkr, dvb, c, sa, sb, qg, kg, gq, gk, fold_q, fold_k)


def _pick_dproj(b, d0, d1, d2, use):
    first_lg = 1 + MID_W // PBLK

    @pl.when(b == 0)
    def _():
        use(d0[...])

    @pl.when(jnp.logical_and(b >= 1, b < first_lg))
    def _():
        use(d1[...])

    @pl.when(b >= first_lg)
    def _():
        use(d2[...])


def win_grad(d0, d1, d2, h):
    s, d = h.shape
    tk = min(s, 2048)
    nk = s // tk

    def body(d0_ref, d1_ref, d2_ref, h_ref, o_ref, o16_ref):
        @pl.when(pl.program_id(1) == 0)
        def _():
            o_ref[...] = jnp.zeros_like(o_ref)

        def use(blk):
            o_ref[...] += _dot_tn(blk, h_ref[...])

        _pick_dproj(pl.program_id(0), d0_ref, d1_ref, d2_ref, use)

        @pl.when(pl.program_id(1) == nk - 1)
        def _():
            o16_ref[...] = o_ref[...].astype(BF16)

    def spec(first, count):
        def imap(j, k):
            used = jnp.logical_and(j >= first, j < first + count)
            return (jnp.where(used, k, 0), jnp.clip(j - first, 0, count - 1))
        return pl.BlockSpec((tk, PBLK), imap)

    nm = MID_W // PBLK
    oblk = pl.BlockSpec((PBLK, d), lambda j, k: (j, 0))
    return pl.pallas_call(
        body, out_shape=(_sds((IN_WIDTH, d), F32), _sds((IN_WIDTH, d), BF16)), grid=(N_PBLK, nk),
        in_specs=[spec(0, 1), spec(1, nm), spec(1 + nm, LG_W // PBLK), pl.BlockSpec((tk, d), lambda j, k: (k, 0))],
        out_specs=(oblk, oblk),
        compiler_params=_cp("parallel", "arbitrary"), name="win_grad")(d0, d1, d2, h)


def h_bwd(d0, d1, d2, w_t, x, dx_out, g, scatter=()):
    s, d = x.shape
    tm = min(s, 512)
    nt = s // tm
    ns = len(scatter)
    na = ns // 2

    def body(d0_ref, d1_ref, d2_ref, w_ref, x_ref, dxo_ref, g_ref, *rest):
        s_in, (dx_ref, gg_ref), s_out = rest[:ns], rest[ns:ns + 2], rest[ns + 2:2 * ns + 2]
        if ns:
            start, finish = scatter_stages([a.shape[1:] for a in scatter[:na]], s_in[:na], s_in[na:], s_out[:na], s_out[na:],
                                           *rest[2 * ns + 2:])
            pl.when(pl.program_id(0) == 0)(start)

        @pl.when(pl.program_id(0) == 0)
        def _():
            gg_ref[...] = jnp.zeros_like(gg_ref)

        dh = (_dot(d0_ref[...], w_ref[0:PBLK, :]) + _dot(d1_ref[...], w_ref[PBLK:PBLK + MID_W, :])
              + _dot(d2_ref[...], w_ref[PBLK + MID_W:, :]))
        xf = x_ref[...]
        r = lax.rsqrt(jnp.mean(xf * xf, axis=-1, keepdims=True) + EPS)
        xh = xf * r
        gg_ref[...] += jnp.sum(dh * xh, axis=0, keepdims=True)
        u = dh * g_ref[...]
        dx_ref[...] = dxo_ref[...] + r * (u - xh * jnp.mean(u * xh, axis=-1, keepdims=True))
        if ns:
            pl.when(pl.program_id(0) == nt - 1)(finish)

    rowb = _rows(tm, d)
    out = pl.pallas_call(
        body, out_shape=(_sds((s, d), F32), _sds((1, d), F32)) + scatter_out_shapes(scatter[:na]), grid=(nt,),
        in_specs=[_rows(tm, PBLK), _rows(tm, MID_W), _rows(tm, LG_W),
                  pl.BlockSpec(w_t.shape, lambda i: (0, 0), pipeline_mode=pl.Buffered(1)), rowb, rowb, _full((1, d))] + [_ANY] * ns,
        out_specs=(rowb, _full((1, d))) + (_ANY,) * ns,
        scratch_shapes=scatter_sems(na) if ns else [],
        compiler_params=_cp("arbitrary"), name="h_bwd_scatter" if ns else "h_bwd")(d0, d1, d2, w_t, x, dx_out, g, *scatter)
    return out[0], out[1], list(out[2:2 + na]), list(out[2 + na:])


def memkv_bwd(mem, g, mem_n, w_kv, dkv):
    m, d = mem.shape

    def body(mem_ref, g_ref, mn_ref, w_ref, dkv_ref, gw_ref, gw16_ref, gg_ref):
        dkb = dkv_ref[...].astype(BF16)
        gw = _dot_tn(mn_ref[...], dkb)
        gw_ref[...] = gw
        gw16_ref[...] = gw.astype(BF16)
        dmn = _dot_nt(dkb, w_ref[...])
        mf = mem_ref[...]
        r = lax.rsqrt(jnp.mean(mf * mf, axis=-1, keepdims=True) + EPS)
        gg_ref[...] = jnp.sum(dmn * (mf * r), axis=0, keepdims=True)

    return pl.pallas_call(
        body, out_shape=(_sds(w_kv.shape, F32), _sds(w_kv.shape, BF16), _sds((1, d), F32)),
        compiler_params=_cp(), name="memkv_bwd")(mem, g, mem_n, w_kv, dkv)


def _layer_consts(seq):
    i = jnp.arange(A_WIDTH)
    return dict(
        tabs=rope_tables(seq),
        gq=_group_ones(A_WIDTH, A_HEAD_DIM).astype(BF16), gk=_group_ones(A_KV_WIDTH, A_HEAD_DIM).astype(BF16),
        fold_q=(i[:, None] % A_HEAD_DIM == jnp.arange(LANES)[None, :]).astype(F32),
        fold_k=(i[:A_KV_WIDTH, None] % A_HEAD_DIM == jnp.arange(LANES)[None, :]).astype(F32),
        head_sel=(jnp.arange(A_HEADS)[:, None] == i[None, :] // A_HEAD_DIM).astype(F32),
    )


_BIG = ("win_t", "wkv", "wbr", "wout")


def _with_own_part(gathered, shards, chip):
    win_t, wkv, wbr, wout = [lax.dynamic_update_slice(g, sh[None], (chip, 0, 0)) for g, sh in zip(gathered, shards)]
    d = wout.shape[-1]
    return dict(win_t=win_t.reshape(IN_WIDTH, d), wkv=wkv.reshape(d, 2 * M_WIDTH),
                wbr=wbr.reshape(N_CHIPS, N_BRANCH, A_WIDTH, d // N_CHIPS), wout=wout.reshape(d, d))


def local_fwd_bwd(x, mem, tgt, small, big=None, shards=None, place=None):
    s, d = x.shape
    depth = small["norm_g"].shape[0]
    k = _layer_consts(s)
    row = lambda v: v.reshape(1, -1)
    dist = shards is not None
    if dist:
        big = [_with_own_part(allgather_layer(shards[0]), shards[0], place[0])] + [None] * (depth - 1)
    saved = []
    for l in range(depth):
        ng = row(small["norm_g"][l])
        qg = row(jnp.tile(small["q_norm_g"][l], A_HEADS))
        kg = row(jnp.tile(small["k_norm_g"][l], A_KV_HEADS))
        ws = small["w_s"][l].astype(BF16)
        ws_t = jnp.swapaxes(small["w_s"][l], 1, 2).astype(BF16)
        bsb = jnp.broadcast_to(small["b_s"][l][:, :, None], (B_GROUPS, CHUNK, B_GROUP_DIM))
        lng, lnb = row(small["sg_ln_g"][l]), row(small["sg_ln_b"][l])
        mg = row(small["mem_norm_g"][l])
        w = big[l]
        h = rms_fwd(x, ng) if l == 0 else h_next
        proj = proj_fwd(h, w["win_t"])
        q_t, kr, kr_t, vb, vte0, vte1 = qk_prep(proj, k["tabs"], qg, kg, k["gq"], k["gk"])
        nxt = shards[l + 1] if dist and l + 1 < depth else ()
        o_a, lse, gathered = attn_fwd(q_t, kr, vte0, vte1, gather=tuple(nxt))
        if nxt:
            big[l + 1] = _with_own_part(gathered, nxt, place[0])
        mem_n, kv = memkv_fwd(mem, mg, w["wkv"])
        next_g = row(small["norm_g"][l + 1]) if l + 1 < depth else row(small["final_g"])
        x_next, y, up, merged, h_next = branch_fwd(x, proj, o_a, kv, ws, bsb, lng, lnb, w["wbr"], w["wout"], next_g)
        saved.append(dict(x=x, ng=ng, qg=qg, kg=kg, ws=ws, ws_t=ws_t, bsb=bsb, lng=lng, lnb=lnb, mg=mg, h=h, proj=proj,
                          q_t=q_t, kr=kr, kr_t=kr_t, vb=vb, o_a=o_a, lse=lse, mem_n=mem_n, kv=kv, y=y, up=up, merged=merged))
        x = x_next

    sq, dx, g_final = final_loss(x, row(small["final_g"]), tgt)
    grads = {n: [None] * depth for n in ("norm_g", "q_norm_g", "k_norm_g", "sg_ln_g", "sg_ln_b", "w_s", "b_s", "mem_norm_g")}
    parts = lambda g: g.reshape(N_CHIPS, -1, g.shape[-1])
    reduced = [[None] * len(_BIG) for _ in range(depth)]

    def reduce_all(items, t_sib, t_rem):
        for (ll, a, g, _), ts, tr in zip(items, t_sib, t_rem):
            reduced[ll][a] = reduce_rows(place, g, ts, tr)

    as_scatter = lambda items: tuple(i[2] for i in items) + tuple(i[3] for i in items)
    pending = []
    for l in reversed(range(depth)):
        sv, w = saved[l], big[l]
        dy, dlg, g_wout, g_wbr, g_wout16, g_wbr16 = merge_bwd(dx, sv["proj"], sv["y"], sv["up"], sv["merged"], w["wbr"], w["wout"])
        dmid, do_t, delta, g_ws, g_bs, g_lng, g_lnb, dkv = branch_bwd(
            dy, sv["proj"], sv["o_a"], sv["kv"], sv["ws"], sv["ws_t"], sv["bsb"], sv["lng"], sv["lnb"], k["head_sel"])
        g_wkv, g_wkv16, g_mg = memkv_bwd(mem, sv["mg"], sv["mem_n"], w["wkv"], dkv)
        if dist:
            pending += [(l, 1, parts(g_wkv), parts(g_wkv16)), (l, 2, parts(g_wbr), parts(g_wbr16)), (l, 3, parts(g_wout), parts(g_wout16))]
        dq_t, dkr, dvb, t_sib, t_rem = attn_bwd(sv["q_t"], do_t, sv["kr"], sv["kr_t"], sv["vb"], sv["lse"], delta,
                                                scatter=as_scatter(pending))
        reduce_all(pending, t_sib, t_rem)
        dqkv, g_qg, g_kg = qk_prep_bwd(sv["proj"], dq_t, dkr, dvb, k["tabs"], sv["qg"], sv["kg"], k["gq"], k["gk"],
                                       k["fold_q"], k["fold_k"])
        g_win, g_win16 = win_grad(dqkv, dmid, dlg, sv["h"])
        pending = [(l, 0, parts(g_win), parts(g_win16))] if dist else []
        last = as_scatter(pending) if l == 0 else ()
        dx, g_ng, t_sib, t_rem = h_bwd(dqkv, dmid, dlg, w["win_t"], sv["x"], dx, sv["ng"], scatter=last)
        if last:
            reduce_all(pending, t_sib, t_rem)
        grads["norm_g"][l] = g_ng[0]
        grads["q_norm_g"][l] = g_qg[0, :A_HEAD_DIM]
        grads["k_norm_g"][l] = g_kg[0, :A_HEAD_DIM]
        grads["sg_ln_g"][l] = g_lng[0]
        grads["sg_ln_b"][l] = g_lnb[0]
        grads["w_s"][l] = g_ws
        grads["b_s"][l] = g_bs[:, :, 0]
        grads["mem_norm_g"][l] = g_mg[0]
        if not dist:
            reduced[l] = dict(zip(_BIG, (parts(g_win), parts(g_wkv), parts(g_wbr), parts(g_wout))))
    grads = {n: jnp.stack(v) for n, v in grads.items()}
    grads["final_g"] = g_final[0]
    return sq[0, 0], dx, grads, reduced


def _row_block(rows, width, cap_bytes=2 * 2**20):
    best = None
    for br in range(8, rows + 1, 8):
        if rows % br == 0 and br * width * 4 <= cap_bytes:
            best = br
    return best if best is not None else rows


def adamw(w, g, m, v):
    r, c = w.shape
    br = _row_block(r, c)

    def body(w_ref, g_ref, m_ref, v_ref, d_ref, nm_ref, nv_ref):
        gg = g_ref[...]
        mm = ADAM_B1 * m_ref[...] + (1.0 - ADAM_B1) * gg
        vv = ADAM_B2 * v_ref[...] + (1.0 - ADAM_B2) * (gg * gg)
        m_hat = mm / (1.0 - ADAM_B1 ** ADAM_STEP)
        v_hat = vv / (1.0 - ADAM_B2 ** ADAM_STEP)
        d_ref[...] = -ADAM_LR * (m_hat / (jnp.sqrt(v_hat) + ADAM_EPS) + ADAM_WD * w_ref[...])
        nm_ref[...] = mm
        nv_ref[...] = vv

    blk = _rows(br, c)
    return pl.pallas_call(
        body, out_shape=(_sds((r, c), F32),) * 3, grid=(r // br,), in_specs=[blk] * 4, out_specs=(blk,) * 3,
        compiler_params=_cp("parallel"), name="adamw")(w, g, m, v)


N_REMOTE = 2 * (N_CHIPS - 1)


def reduce_rows(place, g, t_sib, t_rem):
    _, r, c = g.shape
    r2 = r // 2
    nt = 2 if r2 * c * 4 > 2**20 else 1
    tr = r2 // nt

    def body(place_ref, g_ref, s_ref, t_ref, f_ref):
        acc = g_ref[...] + s_ref[...]
        for j in range(N_REMOTE):
            acc = acc + t_ref[j].astype(F32)
        f_ref[...] = acc

    return pl.pallas_call(
        body, out_shape=_sds((r, c), F32),
        grid_spec=pltpu.PrefetchScalarGridSpec(
            num_scalar_prefetch=1, grid=(nt,),
            in_specs=[pl.BlockSpec((None, tr, c), lambda i, p: (p[0], p[1] * nt + i, 0)),
                      pl.BlockSpec((tr, c), lambda i, p: (i, 0)),
                      pl.BlockSpec((N_REMOTE, tr, c), lambda i, p: (0, i, 0))],
            out_specs=pl.BlockSpec((tr, c), lambda i, p: (p[1] * nt + i, 0))),
        compiler_params=_cp("parallel"), name="reduce_rows")(place, g, t_sib, t_rem)


_ANY = pl.BlockSpec(memory_space=pl.ANY)


def _place():
    x, y, c = lax.axis_index("x"), lax.axis_index("y"), lax.axis_index("c")
    chips = [(1 - x, y), (x, 1 - y), (1 - x, 1 - y)]
    return x, y, c, chips


def gather_sems(n):
    return [pltpu.SemaphoreType.DMA((n, N_REMOTE)), pltpu.SemaphoreType.DMA((n, N_REMOTE))]


def gather_stages(shapes, ins, outs, send, recv):
    n = len(shapes)
    x, y, c, chips = _place()
    me = 2 * x + y
    sib = (x, y, 1 - c)

    def rows(a, hl):
        r2 = shapes[a][0] // 2
        return pl.ds(hl * r2, r2)

    def remote(a, k, src, dst, dev):
        return pltpu.make_async_remote_copy(src, dst, send.at[a, k], recv.at[a, k], device_id=dev, device_id_type=MESH)

    def sent(a, k):
        cx, cy = chips[k]
        return remote(a, k, ins[a].at[rows(a, c)], outs[a].at[me, rows(a, c)], (cx, cy, c))

    def passed(a, k, hl):
        cx, cy = chips[k]
        got = outs[a].at[2 * cx + cy, rows(a, hl)]
        return remote(a, k, got, got, (cx, cy, c)), remote(a, 3 + k, got, got, sib)

    def start():
        for a in range(n):
            for k in range(3):
                sent(a, k).start()

    def forward():
        for k in range(3):
            for a in range(n):
                arrived, on = passed(a, k, c)
                arrived.wait_recv()
                on.start()

    def finish():
        for k in range(3):
            for a in range(n):
                passed(a, k, 1 - c)[1].wait_recv()
        for k in range(3):
            for a in range(n):
                sent(a, k).wait_send()
                passed(a, k, c)[1].wait_send()

    return start, forward, finish


def allgather_layer(shards):
    n = len(shards)

    def body(*refs):
        for stage in gather_stages([a.shape for a in shards], refs[:n], refs[n:2 * n], *refs[2 * n:]):
            stage()

    return pl.pallas_call(
        body, out_shape=tuple(_sds((N_CHIPS,) + a.shape, a.dtype) for a in shards),
        in_specs=[_ANY] * n, out_specs=(_ANY,) * n, scratch_shapes=gather_sems(n), name="allgather_layer")(*shards)


def scatter_sems(n):
    return [pltpu.SemaphoreType.DMA((n, N_REMOTE + 1)), pltpu.SemaphoreType.DMA((n, N_REMOTE + 1))]


def scatter_out_shapes(gs):
    return (tuple(_sds((g.shape[1] // 2, g.shape[2]), F32) for g in gs)
            + tuple(_sds((N_REMOTE, g.shape[1] // 2, g.shape[2]), BF16) for g in gs))


def scatter_stages(shapes, gf, gb, t_sib, t_rem, send, recv):
    n = len(shapes)
    x, y, c, chips = _place()
    me = 2 * x + y

    def copies():
        out = []
        for a in range(n):
            r2 = shapes[a][0] // 2
            out.append(pltpu.make_async_remote_copy(gf[a].at[me, pl.ds((1 - c) * r2, r2)], t_sib[a], send.at[a, N_REMOTE],
                                                    recv.at[a, N_REMOTE], device_id=(x, y, 1 - c), device_id_type=MESH))
            for k, (cx, cy) in enumerate(chips):
                for o in range(2):
                    tc = c if o == 0 else 1 - c
                    out.append(pltpu.make_async_remote_copy(gb[a].at[2 * cx + cy, pl.ds(tc * r2, r2)], t_rem[a].at[2 * k + o],
                                                            send.at[a, 2 * k + o], recv.at[a, 2 * k + o],
                                                            device_id=(cx, cy, tc), device_id_type=MESH))
        return out

    def start():
        for cp in copies():
            cp.start()

    def finish():
        for cp in copies():
            cp.wait()

    return start, finish


def finish_exchange(v, fs):
    n = len(fs)
    r, w = v.shape
    ndev = 2 * N_CHIPS

    def body(v_ref, *refs):
        out, sum_ref = refs[n:2 * n], refs[2 * n]
        all_ref, send, recv, loc, fsend, frecv = refs[2 * n + 1:]
        x, y, c, chips = _place()
        me, sib = (x, y, c), (x, y, 1 - c)
        swaps = []
        for a in range(n):
            r2 = fs[a].shape[0] // 2
            mine = out[a].at[pl.ds(c * r2, r2)]
            cp = pltpu.make_async_remote_copy(mine, mine, fsend.at[a], frecv.at[a], device_id=sib, device_id_type=MESH)
            cp.start()
            swaps.append(cp)

        def slab(px, py, pc):
            return all_ref.at[4 * px + 2 * py + pc]

        def copy(k, block, to, src=None):
            return pltpu.make_async_remote_copy(slab(*block) if src is None else src, slab(*block), send.at[k], recv.at[k],
                                                device_id=to, device_id_type=MESH)

        mine = pltpu.make_async_copy(v_ref, slab(*me), loc)
        mine.start()
        first = [copy(0, me, sib, src=v_ref)] + [copy(1 + j, me, (*chip, c), src=v_ref) for j, chip in enumerate(chips)]
        for cp in first:
            cp.start()
        passed = [copy(4 + j, (*chip, c), sib) for j, chip in enumerate(chips)]
        for j, chip in enumerate(chips):
            copy(1 + j, (*chip, c), me).wait_recv()
            passed[j].start()
        copy(0, sib, me).wait_recv()
        for j, chip in enumerate(chips):
            copy(4 + j, (*chip, 1 - c), me).wait_recv()
        for cp in first + passed:
            cp.wait_send()
        mine.wait()
        acc = all_ref[0]
        for i in range(1, ndev):
            acc = acc + all_ref[i]
        sum_ref[...] = acc
        for a, cp in enumerate(swaps):
            r2 = fs[a].shape[0] // 2
            theirs = out[a].at[pl.ds((1 - c) * r2, r2)]
            cp.wait_send()
            pltpu.make_async_remote_copy(theirs, theirs, fsend.at[a], frecv.at[a], device_id=sib, device_id_type=MESH).wait_recv()

    vm = pl.BlockSpec(memory_space=pltpu.VMEM)
    res = pl.pallas_call(
        body, out_shape=tuple(_sds(f.shape, F32) for f in fs) + (_sds((r, w), F32),),
        in_specs=[vm] + [_ANY] * n, out_specs=(_ANY,) * n + (vm,), input_output_aliases={a + 1: a for a in range(n)},
        scratch_shapes=[pltpu.VMEM((ndev, r, w), F32), pltpu.SemaphoreType.DMA((7,)), pltpu.SemaphoreType.DMA((7,)),
                        pltpu.SemaphoreType.DMA, pltpu.SemaphoreType.DMA((n,)), pltpu.SemaphoreType.DMA((n,))],
        compiler_params=pltpu.CompilerParams(vmem_limit_bytes=VMEM_LIMIT), name="finish_exchange")(v, *fs)
    return res[n], list(res[:n])


_SMALL = ("norm_g", "q_norm_g", "k_norm_g", "sg_ln_g", "sg_ln_b", "w_s", "b_s", "mem_norm_g", "final_g")
_WEIGHTS = ("norm_g", "w_in", "q_norm_g", "k_norm_g", "sg_ln_g", "sg_ln_b", "w_s", "b_s", "mem_norm_g", "w_mem_kv", "w_br",
            "w_out", "final_g")


def _pack(d):
    flat = jnp.concatenate([d[n].reshape(-1) for n in _SMALL])
    rows = -(-flat.shape[0] // (8 * LANES)) * 8
    return jnp.pad(flat, (0, rows * LANES - flat.shape[0])).reshape(rows, LANES)


def _unpack(p, like):
    flat, out, o = p.reshape(-1), {}, 0
    for n in _SMALL:
        out[n] = flat[o:o + like[n].size].reshape(like[n].shape)
        o += like[n].size
    return out


def kernel(x, mem, norm_g, w_in, q_norm_g, k_norm_g, sg_ln_g, sg_ln_b, w_s, b_s, mem_norm_g, w_mem_kv, w_br, w_out, final_g, loss_target, m_norm_g, m_w_in, m_q_norm_g, m_k_norm_g, m_sg_ln_g, m_sg_ln_b, m_w_s, m_b_s, m_mem_norm_g, m_w_mem_kv, m_w_br, m_w_out, m_final_g, v_norm_g, v_w_in, v_q_norm_g, v_k_norm_g, v_sg_ln_g, v_sg_ln_b, v_w_s, v_b_s, v_mem_norm_g, v_w_mem_kv, v_w_br, v_w_out, v_final_g):
    w = dict(norm_g=norm_g, w_in=w_in, q_norm_g=q_norm_g, k_norm_g=k_norm_g, sg_ln_g=sg_ln_g, sg_ln_b=sg_ln_b, w_s=w_s, b_s=b_s,
             mem_norm_g=mem_norm_g, w_mem_kv=w_mem_kv, w_br=w_br, w_out=w_out, final_g=final_g)
    m = dict(norm_g=m_norm_g, w_in=m_w_in, q_norm_g=m_q_norm_g, k_norm_g=m_k_norm_g, sg_ln_g=m_sg_ln_g, sg_ln_b=m_sg_ln_b,
             w_s=m_w_s, b_s=m_b_s, mem_norm_g=m_mem_norm_g, w_mem_kv=m_w_mem_kv, w_br=m_w_br, w_out=m_w_out, final_g=m_final_g)
    v = dict(norm_g=v_norm_g, w_in=v_w_in, q_norm_g=v_q_norm_g, k_norm_g=v_k_norm_g, sg_ln_g=v_sg_ln_g, sg_ln_b=v_sg_ln_b,
             w_s=v_w_s, b_s=v_b_s, mem_norm_g=v_mem_norm_g, w_mem_kv=v_w_mem_kv, w_br=v_w_br, w_out=v_w_out, final_g=v_final_g)
    depth, d = norm_g.shape
    nsh = N_CHIPS
    br_rows = N_BRANCH * A_WIDTH
    br_cols = d // nsh

    shards = [[jnp.swapaxes(w_in[l], 0, 1).astype(BF16), w_mem_kv[l].astype(BF16), w_br[l].astype(BF16).reshape(br_rows, br_cols),
               w_out[l].astype(BF16)] for l in range(depth)]
    place = jnp.stack([2 * lax.axis_index("x") + lax.axis_index("y"), lax.axis_index("c")]).astype(jnp.int32)
    small = {n: w[n] for n in _SMALL}

    sq, dx, grads, reduced = local_fwd_bwd(x[0], mem[0], loss_target[0], small, shards=shards, place=place)
    loss = (0.5 / d) * lax.psum(sq, ("x", "y", "c"))

    small_sum, finals = finish_exchange(_pack(grads), [g for layer in reduced for g in layer])
    finals = [jnp.stack(finals[a::len(_BIG)]) for a in range(len(_BIG))]
    big_grads = dict(w_in=finals[0], w_mem_kv=finals[1], w_br=finals[2].reshape(depth, N_BRANCH, A_WIDTH, br_cols), w_out=finals[3])
    small_grads = _unpack(small_sum, small)

    out_g, out_d, out_m, out_v = {}, {}, {}, {}
    sd, sm, sv = adamw(_pack(small), _pack(small_grads), _pack({n: m[n] for n in _SMALL}), _pack({n: v[n] for n in _SMALL}))
    sd, sm, sv = _unpack(sd, small), _unpack(sm, small), _unpack(sv, small)
    for n in _SMALL:
        out_g[n], out_d[n], out_m[n], out_v[n] = small_grads[n], sd[n], sm[n], sv[n]
    for n, g in big_grads.items():
        into = (lambda a: jnp.swapaxes(a, 1, 2)) if n == "w_in" else (lambda a: a)
        two_d = lambda a: a.reshape(-1, a.shape[-1])
        res = adamw(two_d(into(w[n])), two_d(g), two_d(into(m[n])), two_d(into(v[n])))
        out_g[n], out_d[n], out_m[n], out_v[n] = [into(t.reshape(g.shape)) for t in (two_d(g),) + tuple(res)]
    return (loss, dx[None], *[out_g[n] for n in _WEIGHTS], *[out_d[n] for n in _WEIGHTS], *[out_m[n] for n in _WEIGHTS],
            *[out_v[n] for n in _WEIGHTS])
```

```python
import functools

import jax
import jax.numpy as jnp
from jax import lax
from jax.experimental import pallas as pl
from jax.experimental.pallas import tpu as pltpu

F32 = jnp.float32
BF16 = jnp.bfloat16

D_MODEL = 1024
GRID_W = 64
CHUNK = 128
ROPE_THETA = 10000.0
EPS = 1e-6
A_HEADS, A_KV_HEADS, A_HEAD_DIM = 8, 2, 64
A_WIDTH, A_KV_WIDTH = 512, 128
B_GROUPS, B_GROUP_DIM, B_WIDTH = 4, 128, 512
M_HEADS, M_HEAD_DIM, M_WIDTH = 4, 128, 512
N_BRANCH = 3
IN_WIDTH = 6912
O_QA, O_KA, O_VA, O_ZA, O_UB, O_VB, O_ZB, O_QM, O_ZM, O_LG = 0, 512, 640, 768, 1280, 1792, 2304, 2816, 3328, 3840
PBLK = 768
N_PBLK = IN_WIDTH // PBLK
MID_W = 3072
LG_W = 3072

LN2 = 0.6931471805599453
Q_SCALE = A_HEAD_DIM ** -0.5 / LN2
VTE_ROWS = A_HEAD_DIM + 16

ADAM_LR, ADAM_B1, ADAM_B2, ADAM_EPS, ADAM_WD, ADAM_STEP = 0.001, 0.9, 0.999, 1e-08, 0.01, 10

V7X_VMEM_BYTES = 64 * 2**20
VMEM_LIMIT = V7X_VMEM_BYTES - 8 * 2**20
LANES = 128
MESH = pl.DeviceIdType.MESH
N_CHIPS = 4


def _cp(*sem):
    return pltpu.CompilerParams(dimension_semantics=sem if sem else None, vmem_limit_bytes=VMEM_LIMIT)


def _dot(a, b):
    return jnp.dot(a, b, preferred_element_type=F32)


def _dot_nt(a, b):
    return lax.dot_general(a, b, (((1,), (1,)), ((), ())), preferred_element_type=F32)


def _dot_tn(a, b):
    return lax.dot_general(a, b, (((0,), (0,)), ((), ())), preferred_element_type=F32)


def _dot_hi(a, b):
    return jnp.dot(a, b, preferred_element_type=F32, precision=lax.Precision.HIGHEST)


def _group_sum(a, ones):
    hi = a.astype(BF16)
    lo = (a - hi.astype(F32)).astype(BF16)
    return _dot(hi, ones) + _dot(lo, ones)


def _dot_nt_hi(a, b):
    return lax.dot_general(a, b, (((1,), (1,)), ((), ())), preferred_element_type=F32, precision=lax.Precision.HIGHEST)


def _sig(z):
    return 1.0 / (1.0 + jnp.exp(-z))


def _full(shape, once=False):
    nd = len(shape)
    return pl.BlockSpec(shape, lambda *_: (0,) * nd, pipeline_mode=pl.Buffered(1) if once else None)


def _rows(tm, width):
    return pl.BlockSpec((tm, width), lambda i: (i, 0))


def _sds(shape, dtype):
    return jax.ShapeDtypeStruct(shape, dtype)


def rms_fwd(x, g):
    s, d = x.shape
    tm = min(s, 512)

    def body(x_ref, g_ref, h_ref):
        xf = x_ref[...]
        r = lax.rsqrt(jnp.mean(xf * xf, axis=-1, keepdims=True) + EPS)
        h_ref[...] = ((xf * r) * g_ref[...]).astype(BF16)

    return pl.pallas_call(
        body, out_shape=_sds((s, d), BF16), grid=(s // tm,),
        in_specs=[_rows(tm, d), _full((1, d))], out_specs=_rows(tm, d),
        compiler_params=_cp("parallel"), name="rms_fwd")(x, g)


def proj_fwd(h, w_t):
    s, d = h.shape
    n = w_t.shape[0]
    tm = min(s, 512)
    tn = 2304

    def body(h_ref, w_ref, o_ref):
        o_ref[...] = _dot_nt(h_ref[...], w_ref[...]).astype(BF16)

    return pl.pallas_call(
        body, out_shape=_sds((s, n), BF16), grid=(n // tn, s // tm),
        in_specs=[pl.BlockSpec((tm, d), lambda j, i: (i, 0)), pl.BlockSpec((tn, d), lambda j, i: (j, 0))],
        out_specs=pl.BlockSpec((tm, tn), lambda j, i: (i, j)),
        compiler_params=_cp("parallel", "parallel"), name="proj_fwd")(h, w_t)


def rope_tables(seq):
    rows = seq // GRID_W
    row = jnp.repeat(jnp.arange(rows, dtype=F32), GRID_W)
    col = jnp.tile(jnp.arange(GRID_W, dtype=F32), rows)
    n_freq = A_HEAD_DIM // 4
    inv = ROPE_THETA ** (-jnp.arange(n_freq, dtype=F32) / n_freq)
    ang = jnp.stack([row[:, None] * inv, col[:, None] * inv], axis=1)
    cos, sin = jnp.cos(ang), jnp.sin(ang)
    zero = jnp.zeros_like(sin[:, 0])
    c64 = jnp.concatenate([cos[:, 0], cos[:, 0], cos[:, 1], cos[:, 1]], axis=1)
    sa64 = jnp.concatenate([zero, sin[:, 0], zero, sin[:, 1]], axis=1)
    sb64 = jnp.concatenate([-sin[:, 0], zero, -sin[:, 1], zero], axis=1)
    two = lambda t: jnp.concatenate([t, t], axis=1)
    return two(c64), two(sa64), two(sb64)


def _group_ones(width, group):
    i = jnp.arange(width)
    return (i[:, None] // group == i[None, :] // group).astype(F32)


def _rope(xn, c, sa, sb):
    w = xn.shape[1]
    return xn * c + pltpu.roll(xn, 16, 1) * sa + pltpu.roll(xn, w - 16, 1) * sb


def _rope_t(dy, c, sa, sb):
    w = dy.shape[1]
    return dy * c + pltpu.roll(dy * sa, w - 16, 1) + pltpu.roll(dy * sb, 16, 1)


def _tile4(t):
    return jnp.concatenate([t, t, t, t], axis=1)


def qk_prep(proj, tabs, qg, kg, gq, gk):
    s = proj.shape[0]
    tm = min(s, 512)
    c, sa, sb = tabs

    def body(p_ref, c_ref, sa_ref, sb_ref, qg_ref, kg_ref, gq_ref, gk_ref, qt_ref, kr_ref, vb_ref, kt0_ref, kt1_ref, v0_ref, v1_ref):
        xq = p_ref[:, O_QA:O_QA + A_WIDTH].astype(F32)
        xk = p_ref[:, O_KA:O_KA + A_KV_WIDTH].astype(F32)
        xv = p_ref[:, O_VA:O_VA + A_KV_WIDTH].astype(F32)
        cc, ssa, ssb = c_ref[...], sa_ref[...], sb_ref[...]
        msq = _group_sum(xq * xq, gq_ref[...]) * (1.0 / A_HEAD_DIM)
        qn = (xq * lax.rsqrt(msq + EPS)) * qg_ref[...]
        qr = _rope(qn, _tile4(cc), _tile4(ssa), _tile4(ssb)) * Q_SCALE
        qt_ref[...] = qr.T.astype(BF16)
        msk = _group_sum(xk * xk, gk_ref[...]) * (1.0 / A_HEAD_DIM)
        kn = (xk * lax.rsqrt(msk + EPS)) * kg_ref[...]
        kr = _rope(kn, cc, ssa, ssb)
        kr_ref[...] = kr.astype(BF16)
        vb_ref[...] = xv.astype(BF16)
        kt = kr.T.astype(BF16)
        kt0_ref[...] = kt[:A_HEAD_DIM]
        kt1_ref[...] = kt[A_HEAD_DIM:]
        vt = xv.T.astype(BF16)
        one = jnp.ones((VTE_ROWS - A_HEAD_DIM, tm), BF16)
        v0_ref[...] = jnp.concatenate([vt[:A_HEAD_DIM], one], axis=0)
        v1_ref[...] = jnp.concatenate([vt[A_HEAD_DIM:], one], axis=0)

    tab = _rows(tm, LANES)
    colb = lambda w: pl.BlockSpec((w, tm), lambda i: (0, i))
    return pl.pallas_call(
        body,
        out_shape=(_sds((A_WIDTH, s), BF16), _sds((s, A_KV_WIDTH), BF16), _sds((s, A_KV_WIDTH), BF16),
                   _sds((A_HEAD_DIM, s), BF16), _sds((A_HEAD_DIM, s), BF16), _sds((VTE_ROWS, s), BF16), _sds((VTE_ROWS, s), BF16)),
        grid=(s // tm,),
        in_specs=[_rows(tm, PBLK), tab, tab, tab, _full((1, A_WIDTH)), _full((1, A_KV_WIDTH)),
                  _full((A_WIDTH, A_WIDTH)), _full((A_KV_WIDTH, A_KV_WIDTH))],
        out_specs=(colb(A_WIDTH), _rows(tm, A_KV_WIDTH), _rows(tm, A_KV_WIDTH), colb(A_HEAD_DIM), colb(A_HEAD_DIM),
                   colb(VTE_ROWS), colb(VTE_ROWS)),
        compiler_params=_cp("parallel"), name="qk_prep")(proj, c, sa, sb, qg, kg, gq, gk)


def _pad_head(q_h, kv):
    z = jnp.zeros_like(q_h)
    return jnp.concatenate([q_h, z], axis=0) if kv == 0 else jnp.concatenate([z, q_h], axis=0)


def attn_fwd(q_t, kr, vte0, vte1, gather=()):
    s = kr.shape[0]
    tq = min(s, 256)
    kc = min(s, 512)
    nkc = s // kc
    nq = s // tq
    grp = A_HEADS // A_KV_HEADS
    ng = len(gather)

    def body(qt_ref, kr_ref, v0_ref, v1_ref, *rest):
        g_in, (o_ref, lse_ref), g_out = rest[:ng], rest[ng:ng + 2], rest[ng + 2:2 * ng + 2]
        qp_ref, m_ref, acc_ref = rest[2 * ng + 2:2 * ng + 5]
        if ng:
            start, forward, finish = gather_stages([g.shape for g in gather], g_in, g_out, *rest[2 * ng + 5:])
            pl.when(pl.program_id(0) == 0)(start)
            pl.when(pl.program_id(0) == (3 * nq) // 4)(forward)

        for h in range(A_HEADS):
            qp_ref[h] = _pad_head(qt_ref[A_HEAD_DIM * h:A_HEAD_DIM * (h + 1), :], h // grp)
        m_ref[...] = jnp.full(m_ref.shape, -1e30, F32)
        acc_ref[...] = jnp.zeros_like(acc_ref)

        def step(ci, carry):
            ks = pl.ds(pl.multiple_of(ci * kc, kc), kc)
            kblk = kr_ref[ks, :]
            vts = (v0_ref[:, ks], v1_ref[:, ks])
            scs = [_dot(kblk, qp_ref[h]) for h in range(A_HEADS)]
            for h in range(A_HEADS):
                sc = scs[h]
                m_prev = m_ref[h:h + 1, :]
                m_new = jnp.maximum(m_prev, jnp.max(sc, axis=0, keepdims=True))
                p = jnp.exp2(sc - m_new)
                acc_ref[h] = acc_ref[h] * jnp.exp2(m_prev - m_new) + _dot(vts[h // grp], p.astype(BF16))
                m_ref[h:h + 1, :] = m_new
            return carry

        lax.fori_loop(0, nkc, step, 0)
        outs, lses = [], []
        for h in range(A_HEADS):
            acc = acc_ref[h]
            l = acc[A_HEAD_DIM:A_HEAD_DIM + 1, :]
            outs.append(acc[:A_HEAD_DIM, :] / l)
            lses.append(m_ref[h:h + 1, :] + jnp.log2(l))
        o_ref[...] = jnp.concatenate(outs, axis=0).T
        lse_ref[...] = jnp.concatenate(lses, axis=0)
        if ng:
            pl.when(pl.program_id(0) == nq - 1)(finish)

    out = pl.pallas_call(
        body,
        out_shape=(_sds((s, A_WIDTH), F32), _sds((A_HEADS, s), F32)) + tuple(_sds((N_CHIPS,) + g.shape, g.dtype) for g in gather),
        grid=(nq,),
        in_specs=[pl.BlockSpec((A_WIDTH, tq), lambda i: (0, i)), _full((s, A_KV_WIDTH)), _full((VTE_ROWS, s)),
                  _full((VTE_ROWS, s))] + [_ANY] * ng,
        out_specs=(_rows(tq, A_WIDTH), pl.BlockSpec((A_HEADS, tq), lambda i: (0, i))) + (_ANY,) * ng,
        scratch_shapes=[pltpu.VMEM((A_HEADS, A_KV_WIDTH, tq), BF16), pltpu.VMEM((A_HEADS, tq), F32),
                        pltpu.VMEM((A_HEADS, VTE_ROWS, tq), F32)] + (gather_sems(ng) if ng else []),
        compiler_params=_cp("arbitrary"), name="attn_fwd_gather" if ng else "attn_fwd")(q_t, kr, vte0, vte1, *gather)
    return out[0], out[1], list(out[2:])


def memkv_fwd(mem, g, w_kv):
    m, d = mem.shape

    def body(mem_ref, g_ref, w_ref, mn_ref, kv_ref):
        mf = mem_ref[...]
        r = lax.rsqrt(jnp.mean(mf * mf, axis=-1, keepdims=True) + EPS)
        mn = ((mf * r) * g_ref[...]).astype(BF16)
        mn_ref[...] = mn
        kv_ref[...] = _dot(mn, w_ref[...]).astype(BF16)

    return pl.pallas_call(
        body, out_shape=(_sds((m, d), BF16), _sds((m, 2 * M_WIDTH), BF16)),
        compiler_params=_cp(), name="memkv_fwd")(mem, g, w_kv)


def _layer_norm_stats(v):
    mu = jnp.mean(v, axis=-1, keepdims=True)
    xc = v - mu
    rstd = lax.rsqrt(jnp.mean(xc * xc, axis=-1, keepdims=True) + EPS)
    return xc * rstd, rstd


def _spatial_mix(vlb, ws_ref, bsb_ref, tm):
    rows = []
    for ci in range(tm // CHUNK):
        cols = []
        for g in range(B_GROUPS):
            blk = vlb[ci * CHUNK:(ci + 1) * CHUNK, g * B_GROUP_DIM:(g + 1) * B_GROUP_DIM]
            cols.append(_dot(ws_ref[g], blk) + bsb_ref[g])
        rows.append(jnp.concatenate(cols, axis=1))
    return jnp.concatenate(rows, axis=0)


def _mem_attn(qm, kv_ref):
    out = []
    for h in range(M_HEADS):
        qh = qm[:, h * M_HEAD_DIM:(h + 1) * M_HEAD_DIM].astype(BF16)
        kh = kv_ref[:, h * M_HEAD_DIM:(h + 1) * M_HEAD_DIM]
        vh = kv_ref[:, M_WIDTH + h * M_HEAD_DIM:M_WIDTH + (h + 1) * M_HEAD_DIM]
        sc = _dot_nt(qh, kh) * (M_HEAD_DIM ** -0.5)
        e = jnp.exp(sc - jnp.max(sc, axis=-1, keepdims=True))
        p = e / jnp.sum(e, axis=-1, keepdims=True)
        out.append((p, _dot(p.astype(BF16), vh)))
    return out


def branch_fwd(x, proj, o_a, kv, ws, bsb, ln_g, ln_b, w_br, w_out, next_g):
    s, d = x.shape
    tm = min(s, 256)

    def body(x_ref, p_ref, oa_ref, kv_ref, ws_ref, bsb_ref, lg_ref, lb_ref, wbr_ref, wo_ref, ng_ref,
             xn_ref, y_ref, up_ref, mg_ref, hn_ref):
        seg = lambda o, w: p_ref[:, o:o + w].astype(F32)
        z_a, u_b, v_b, z_b = seg(O_ZA, A_WIDTH), seg(O_UB, B_WIDTH), seg(O_VB, B_WIDTH), seg(O_ZB, B_WIDTH)
        q_m, z_m = seg(O_QM, M_WIDTH), seg(O_ZM, M_WIDTH)
        xhat, _ = _layer_norm_stats(v_b)
        vln = xhat * lg_ref[...] + lb_ref[...]
        mixed = _spatial_mix(vln.astype(BF16), ws_ref, bsb_ref, tm)
        y_b = (u_b * mixed) * (z_b * _sig(z_b))
        o_m = jnp.concatenate([o for _, o in _mem_attn(q_m, kv_ref)], axis=1)
        y_a = oa_ref[...] * (z_a * _sig(z_a))
        y_m = o_m * (z_m * _sig(z_m))
        merged = None
        for n, yy in enumerate((y_a, y_b, y_m)):
            yb = yy.astype(BF16)
            y_ref[n] = yb
            up = jnp.concatenate([_dot(yb, wbr_ref[c, n]) for c in range(N_CHIPS)], axis=1)
            up_ref[n] = up.astype(BF16)
            t = _sig(seg(O_LG + n * d, d)) * up
            merged = t if merged is None else merged + t
        mb = merged.astype(BF16)
        mg_ref[...] = mb
        xn = x_ref[...] + _dot(mb, wo_ref[...])
        xn_ref[...] = xn
        r = lax.rsqrt(jnp.mean(xn * xn, axis=-1, keepdims=True) + EPS)
        hn_ref[...] = ((xn * r) * ng_ref[...]).astype(BF16)

    return pl.pallas_call(
        body,
        out_shape=(_sds((s, d), F32), _sds((N_BRANCH, s, A_WIDTH), BF16), _sds((N_BRANCH, s, d), BF16), _sds((s, d), BF16),
                   _sds((s, d), BF16)),
        grid=(s // tm,),
        in_specs=[_rows(tm, d), _rows(tm, IN_WIDTH), _rows(tm, A_WIDTH), _full(kv.shape), _full(ws.shape), _full(bsb.shape),
                  _full((1, B_WIDTH)), _full((1, B_WIDTH)), _full(w_br.shape), _full(w_out.shape), _full((1, d))],
        out_specs=(_rows(tm, d), pl.BlockSpec((N_BRANCH, tm, A_WIDTH), lambda i: (0, i, 0)),
                   pl.BlockSpec((N_BRANCH, tm, d), lambda i: (0, i, 0)), _rows(tm, d), _rows(tm, d)),
        compiler_params=_cp("parallel"), name="branch_fwd")(x, proj, o_a, kv, ws, bsb, ln_g, ln_b, w_br, w_out, next_g)


def final_loss(x, fg, tgt):
    s, d = x.shape
    tm = min(s, 512)

    def body(x_ref, g_ref, t_ref, ls_ref, dx_ref, gg_ref):
        @pl.when(pl.program_id(0) == 0)
        def _():
            ls_ref[...] = jnp.zeros_like(ls_ref)
            gg_ref[...] = jnp.zeros_like(gg_ref)

        xf = x_ref[...]
        g = g_ref[...]
        r = lax.rsqrt(jnp.mean(xf * xf, axis=-1, keepdims=True) + EPS)
        xh = xf * r
        e = xh * g - t_ref[...]
        sq = jnp.sum(jnp.sum(e * e, axis=0, keepdims=True), axis=1, keepdims=True)
        ls_ref[...] += jnp.broadcast_to(sq, ls_ref.shape)
        dy = e * (1.0 / d)
        gg_ref[...] += jnp.sum(dy * xh, axis=0, keepdims=True)
        gy = dy * g
        dx_ref[...] = r * (gy - xh * jnp.mean(gy * xh, axis=-1, keepdims=True))

    return pl.pallas_call(
        body, out_shape=(_sds((1, LANES), F32), _sds((s, d), F32), _sds((1, d), F32)), grid=(s // tm,),
        in_specs=[_rows(tm, d), _full((1, d)), _rows(tm, d)],
        out_specs=(_full((1, LANES)), _rows(tm, d), _full((1, d))),
        compiler_params=_cp("arbitrary"), name="final_loss")(x, fg, tgt)


def _pblocks(tm, first, count):
    return [pl.BlockSpec((tm, PBLK), functools.partial(lambda i, b: (i, b), b=first + k)) for k in range(count)]


def merge_bwd(dx, proj, y, up, merged, w_br, w_out):
    s, d = dx.shape
    tm = min(s, 256)
    nlg = LG_W // PBLK
    cw = d // N_CHIPS

    def body(dx_ref, l0, l1, l2, l3, y_ref, up_ref, mg_ref, wbr_ref, wo_ref, dy_ref, dlg_ref, gwo_ref, gwb_ref, gwo16_ref, gwb16_ref):
        @pl.when(pl.program_id(0) == 0)
        def _():
            gwo_ref[...] = jnp.zeros_like(gwo_ref)
            gwb_ref[...] = jnp.zeros_like(gwb_ref)

        dxb = dx_ref[...].astype(BF16)
        dmg = _dot_nt(dxb, wo_ref[...])
        gwo_ref[...] += _dot_tn(mg_ref[...], dxb)
        lg = jnp.concatenate([l0[...], l1[...], l2[...], l3[...]], axis=1).astype(F32)
        for n in range(N_BRANCH):
            g = _sig(lg[:, n * d:(n + 1) * d])
            dup = dmg * g
            dlg_ref[:, n * d:(n + 1) * d] = ((dup * up_ref[n].astype(F32)) * (1.0 - g)).astype(BF16)
            dupb = dup.astype(BF16)
            dyn = None
            for c in range(N_CHIPS):
                blk = dupb[:, c * cw:(c + 1) * cw]
                gwb_ref[c, n] += _dot_tn(y_ref[n], blk)
                t = _dot_nt(blk, wbr_ref[c, n])
                dyn = t if dyn is None else dyn + t
            dy_ref[n] = dyn

        @pl.when(pl.program_id(0) == pl.num_programs(0) - 1)
        def _():
            gwo16_ref[...] = gwo_ref[...].astype(BF16)
            gwb16_ref[...] = gwb_ref[...].astype(BF16)

    return pl.pallas_call(
        body,
        out_shape=(_sds((N_BRANCH, s, A_WIDTH), F32), _sds((s, LG_W), BF16), _sds((d, d), F32), _sds(w_br.shape, F32),
                   _sds((d, d), BF16), _sds(w_br.shape, BF16)),
        grid=(s // tm,),
        in_specs=[_rows(tm, d)] + _pblocks(tm, O_LG // PBLK, nlg) + [
            pl.BlockSpec((N_BRANCH, tm, A_WIDTH), lambda i: (0, i, 0)), pl.BlockSpec((N_BRANCH, tm, d), lambda i: (0, i, 0)),
            _rows(tm, d), _full(w_br.shape, once=True), _full(w_out.shape, once=True)],
        out_specs=(pl.BlockSpec((N_BRANCH, tm, A_WIDTH), lambda i: (0, i, 0)), _rows(tm, LG_W), _full((d, d)), _full(w_br.shape),
                   _full((d, d)), _full(w_br.shape)),
        compiler_params=_cp("arbitrary"), name="merge_bwd")(dx, proj, proj, proj, proj, y, up, merged, w_br, w_out)


def _dsilu(z, sg):
    return sg * (1.0 + z * (1.0 - sg))


def branch_bwd(dy, proj, o_a, kv, ws, ws_t, bsb, ln_g, ln_b, head_sel):
    s = proj.shape[0]
    tm = min(s, 256)
    nmid = MID_W // PBLK

    def body(dy_ref, m0, m1, m2, m3, oa_ref, kv_ref, ws_ref, wst_ref, bsb_ref, lg_ref, lb_ref, sel_ref,
             dmid_ref, dot_ref, dl_ref, gws_ref, gbs_ref, glg_ref, glb_ref, dkv_ref):
        @pl.when(pl.program_id(0) == 0)
        def _():
            for r in (gws_ref, gbs_ref, glg_ref, glb_ref, dkv_ref):
                r[...] = jnp.zeros_like(r)

        mid = jnp.concatenate([m0[...], m1[...], m2[...], m3[...]], axis=1).astype(F32)
        seg = lambda o, w: mid[:, o - O_ZA:o - O_ZA + w]
        z_a, u_b, v_b, z_b = seg(O_ZA, A_WIDTH), seg(O_UB, B_WIDTH), seg(O_VB, B_WIDTH), seg(O_ZB, B_WIDTH)
        q_m, z_m = seg(O_QM, M_WIDTH), seg(O_ZM, M_WIDTH)

        def put(o, v):
            dmid_ref[:, o - O_ZA:o - O_ZA + v.shape[1]] = v.astype(BF16)

        dy_a, dy_b, dy_m = dy_ref[0], dy_ref[1], dy_ref[2]

        o_a_ = oa_ref[...]
        sg = _sig(z_a)
        do_a = dy_a * (z_a * sg)
        put(O_ZA, (dy_a * o_a_) * _dsilu(z_a, sg))
        do_l = do_a * LN2
        dot_ref[...] = do_l.T.astype(BF16)
        dl_ref[...] = _dot_nt_hi(sel_ref[...], do_l * o_a_)

        xhat, rstd = _layer_norm_stats(v_b)
        lng = lg_ref[...]
        vln = xhat * lng + lb_ref[...]
        vlb = vln.astype(BF16)
        mixed = _spatial_mix(vlb, ws_ref, bsb_ref, tm)
        sg = _sig(z_b)
        sl = z_b * sg
        put(O_UB, (dy_b * mixed) * sl)
        put(O_ZB, ((dy_b * u_b) * mixed) * _dsilu(z_b, sg))
        dmix = (dy_b * u_b) * sl
        dmb = dmix.astype(BF16)
        rows = []
        for ci in range(tm // CHUNK):
            cols = []
            for g in range(B_GROUPS):
                rs, cs = slice(ci * CHUNK, (ci + 1) * CHUNK), slice(g * B_GROUP_DIM, (g + 1) * B_GROUP_DIM)
                gws_ref[g] += _dot_nt(dmb[rs, cs], vlb[rs, cs])
                gbs_ref[g] += jnp.broadcast_to(jnp.sum(dmix[rs, cs], axis=1, keepdims=True), (CHUNK, B_GROUP_DIM))
                cols.append(_dot(wst_ref[g], dmb[rs, cs]))
            rows.append(jnp.concatenate(cols, axis=1))
        dvln = jnp.concatenate(rows, axis=0)
        glg_ref[...] += jnp.sum(dvln * xhat, axis=0, keepdims=True)
        glb_ref[...] += jnp.sum(dvln, axis=0, keepdims=True)
        gy = dvln * lng
        put(O_VB, rstd * ((gy - jnp.mean(gy, axis=-1, keepdims=True)) - xhat * jnp.mean(gy * xhat, axis=-1, keepdims=True)))

        sg = _sig(z_m)
        sl = z_m * sg
        heads = _mem_attn(q_m, kv_ref)
        o_m = jnp.concatenate([o for _, o in heads], axis=1)
        put(O_ZM, (dy_m * o_m) * _dsilu(z_m, sg))
        do_m = dy_m * sl
        dqs = []
        for h, (p, o_h) in enumerate(heads):
            hs = slice(h * M_HEAD_DIM, (h + 1) * M_HEAD_DIM)
            vs = slice(M_WIDTH + h * M_HEAD_DIM, M_WIDTH + (h + 1) * M_HEAD_DIM)
            do_h = do_m[:, hs]
            dob = do_h.astype(BF16)
            dp = _dot_nt(dob, kv_ref[:, vs])
            dsc = (p * (dp - jnp.sum(do_h * o_h, axis=-1, keepdims=True))) * (M_HEAD_DIM ** -0.5)
            dsb = dsc.astype(BF16)
            dqs.append(_dot(dsb, kv_ref[:, hs]))
            dkv_ref[:, hs] += _dot_tn(dsb, q_m[:, hs].astype(BF16))
            dkv_ref[:, vs] += _dot_tn(p.astype(BF16), dob)
        put(O_QM, jnp.concatenate(dqs, axis=1))

    return pl.pallas_call(
        body,
        out_shape=(_sds((s, MID_W), BF16), _sds((A_WIDTH, s), BF16), _sds((A_HEADS, s), F32), _sds(ws.shape, F32),
                   _sds(ws.shape, F32), _sds((1, B_WIDTH), F32), _sds((1, B_WIDTH), F32), _sds(kv.shape, F32)),
        grid=(s // tm,),
        in_specs=[pl.BlockSpec((N_BRANCH, tm, A_WIDTH), lambda i: (0, i, 0))] + _pblocks(tm, O_ZA // PBLK, nmid) + [
            _rows(tm, A_WIDTH), _full(kv.shape), _full(ws.shape), _full(ws.shape), _full(bsb.shape),
            _full((1, B_WIDTH)), _full((1, B_WIDTH)), _full(head_sel.shape)],
        out_specs=(_rows(tm, MID_W), pl.BlockSpec((A_WIDTH, tm), lambda i: (0, i)), pl.BlockSpec((A_HEADS, tm), lambda i: (0, i)),
                   _full(ws.shape), _full(ws.shape), _full((1, B_WIDTH)), _full((1, B_WIDTH)), _full(kv.shape)),
        compiler_params=_cp("arbitrary"), name="branch_bwd")(dy, proj, proj, proj, proj, o_a, kv, ws, ws_t, bsb, ln_g, ln_b, head_sel)


def attn_bwd(q_t, do_t, kr, kt0, kt1, vb, lse, delta, scatter=()):
    s = kr.shape[0]
    tq = min(s, 256)
    kc = min(s, 512)
    nkc = s // kc
    nq = s // tq
    grp = A_HEADS // A_KV_HEADS
    ns = len(scatter)
    na = ns // 2

    def body(qt_ref, dot_ref, kr_ref, kt0_ref, kt1_ref, vb_ref, lse_ref, dl_ref, *rest):
        s_in, (dqt_ref, dk_ref, dv_ref), s_out = rest[:ns], rest[ns:ns + 3], rest[ns + 3:2 * ns + 3]
        qp_ref, dop_ref, dq_ref = rest[2 * ns + 3:2 * ns + 6]
        if ns:
            start, finish = scatter_stages([g.shape[1:] for g in scatter[:na]], s_in[:na], s_in[na:], s_out[:na], s_out[na:],
                                           *rest[2 * ns + 6:])
            pl.when(pl.program_id(0) == 0)(start)

        @pl.when(pl.program_id(0) == 0)
        def _():
            dk_ref[...] = jnp.zeros_like(dk_ref)
            dv_ref[...] = jnp.zeros_like(dv_ref)

        for h in range(A_HEADS):
            hs = slice(A_HEAD_DIM * h, A_HEAD_DIM * (h + 1))
            qp_ref[h] = _pad_head(qt_ref[hs, :], h // grp)
            dop_ref[h] = _pad_head(dot_ref[hs, :], h // grp)
        dq_ref[...] = jnp.zeros_like(dq_ref)

        def step(ci, carry):
            ks = pl.ds(pl.multiple_of(ci * kc, kc), kc)
            kblk, vblk, kts = kr_ref[ks, :], vb_ref[ks, :], (kt0_ref[:, ks], kt1_ref[:, ks])
            dv_acc = jnp.zeros((kc, A_KV_WIDTH), F32)
            dk_acc = jnp.zeros((kc, A_KV_WIDTH), F32)
            scs = [_dot(kblk, qp_ref[h]) for h in range(A_HEADS)]
            dps = [_dot(vblk, dop_ref[h]) for h in range(A_HEADS)]
            for h in range(A_HEADS):
                qpad, dopad = qp_ref[h], dop_ref[h]
                p = jnp.exp2(scs[h] - lse_ref[h:h + 1, :])
                dsb = (p * (dps[h] - dl_ref[h:h + 1, :])).astype(BF16)
                dv_acc = dv_acc + _dot_nt(p.astype(BF16), dopad)
                dk_acc = dk_acc + _dot_nt(dsb, qpad)
                dq_ref[h] += _dot(kts[h // grp], dsb)
            dv_ref[ks, :] += dv_acc
            dk_ref[ks, :] += dk_acc
            return carry

        lax.fori_loop(0, nkc, step, 0)
        dqt_ref[...] = dq_ref[...].reshape(A_WIDTH, tq)
        if ns:
            pl.when(pl.program_id(0) == nq - 1)(finish)

    colq = pl.BlockSpec((A_WIDTH, tq), lambda i: (0, i))
    colh = pl.BlockSpec((A_HEADS, tq), lambda i: (0, i))
    out = pl.pallas_call(
        body,
        out_shape=(_sds((A_WIDTH, s), F32), _sds((s, A_KV_WIDTH), F32), _sds((s, A_KV_WIDTH), F32)) + scatter_out_shapes(scatter[:na]),
        grid=(nq,),
        in_specs=[colq, colq, _full((s, A_KV_WIDTH)), _full((A_HEAD_DIM, s)), _full((A_HEAD_DIM, s)), _full((s, A_KV_WIDTH)),
                  colh, colh] + [_ANY] * ns,
        out_specs=(colq, _full((s, A_KV_WIDTH)), _full((s, A_KV_WIDTH))) + (_ANY,) * ns,
        scratch_shapes=[pltpu.VMEM((A_HEADS, A_KV_WIDTH, tq), BF16), pltpu.VMEM((A_HEADS, A_KV_WIDTH, tq), BF16),
                        pltpu.VMEM((A_HEADS, A_HEAD_DIM, tq), F32)] + (scatter_sems(na) if ns else []),
        compiler_params=_cp("arbitrary"), name="attn_bwd_scatter" if ns else "attn_bwd")(
            q_t, do_t, kr, kt0, kt1, vb, lse, delta, *scatter)
    return out[0], out[1], out[2], list(out[3:3 + na]), list(out[3 + na:])


def qk_prep_bwd(proj, dq_t, dkr, dvb, tabs, qg, kg, gq, gk, fold_q, fold_k):
    s = proj.shape[0]
    tm = min(s, 512)
    c, sa, sb = tabs

    def head_norm_bwd(x, dn, gain, gones, fold):
        ms = _group_sum(x * x, gones) * (1.0 / A_HEAD_DIM)
        r = lax.rsqrt(ms + EPS)
        xh = x * r
        gg = _dot_hi(jnp.sum(dn * xh, axis=0, keepdims=True), fold)
        u = dn * gain
        mean_u = _group_sum(u * xh, gones) * (1.0 / A_HEAD_DIM)
        return r * (u - xh * mean_u), gg

    def body(p_ref, dqt_ref, dk_ref, dv_ref, c_ref, sa_ref, sb_ref, qg_ref, kg_ref, gq_ref, gk_ref, fq_ref, fk_ref,
             dqkv_ref, gqg_ref, gkg_ref):
        @pl.when(pl.program_id(0) == 0)
        def _():
            gqg_ref[...] = jnp.zeros_like(gqg_ref)
            gkg_ref[...] = jnp.zeros_like(gkg_ref)

        cc, ssa, ssb = c_ref[...], sa_ref[...], sb_ref[...]
        dqr = dqt_ref[...].T * Q_SCALE
        dqn = _rope_t(dqr, _tile4(cc), _tile4(ssa), _tile4(ssb))
        dxq, gq_ = head_norm_bwd(p_ref[:, O_QA:O_QA + A_WIDTH].astype(F32), dqn, qg_ref[...], gq_ref[...], fq_ref[...])
        dkn = _rope_t(dk_ref[...], cc, ssa, ssb)
        dxk, gk_ = head_norm_bwd(p_ref[:, O_KA:O_KA + A_KV_WIDTH].astype(F32), dkn, kg_ref[...], gk_ref[...], fk_ref[...])
        gqg_ref[...] += gq_
        gkg_ref[...] += gk_
        dqkv_ref[:, O_QA:O_QA + A_WIDTH] = dxq.astype(BF16)
        dqkv_ref[:, O_KA:O_KA + A_KV_WIDTH] = dxk.astype(BF16)
        dqkv_ref[:, O_VA:O_VA + A_KV_WIDTH] = (dv_ref[...] * (1.0 / LN2)).astype(BF16)

    tab = _rows(tm, LANES)
    return pl.pallas_call(
        body, out_shape=(_sds((s, PBLK), BF16), _sds((1, LANES), F32), _sds((1, LANES), F32)), grid=(s // tm,),
        in_specs=[_rows(tm, PBLK), pl.BlockSpec((A_WIDTH, tm), lambda i: (0, i)), _rows(tm, A_KV_WIDTH), _rows(tm, A_KV_WIDTH),
                  tab, tab, tab, _full((1, A_WIDTH)), _full((1, A_KV_WIDTH)), _full((A_WIDTH, A_WIDTH)),
                  _full((A_KV_WIDTH, A_KV_WIDTH)), _full((A_WIDTH, LANES)), _full((A_KV_WIDTH, LANES))],
        out_specs=(_rows(tm, PBLK), _full((1, LANES)), _full((1, LANES))),
        compiler_params=_cp("arbitrary"), name="qk_prep_bwd")(proj, dq_t, dkr, dvb, c, sa, sb, qg, kg, gq, gk, fold_q, fold_k)


def _pick_dproj(b, d0, d1, d2, use):
    first_lg = 1 + MID_W // PBLK

    @pl.when(b == 0)
    def _():
        use(d0[...])

    @pl.when(jnp.logical_and(b >= 1, b < first_lg))
    def _():
        use(d1[...])

    @pl.when(b >= first_lg)
    def _():
        use(d2[...])


def win_grad(d0, d1, d2, h):
    s, d = h.shape
    tk = min(s, 2048)
    nk = s // tk

    def body(d0_ref, d1_ref, d2_ref, h_ref, o_ref, o16_ref):
        @pl.when(pl.program_id(1) == 0)
        def _():
            o_ref[...] = jnp.zeros_like(o_ref)

        def use(blk):
            o_ref[...] += _dot_tn(blk, h_ref[...])

        _pick_dproj(pl.program_id(0), d0_ref, d1_ref, d2_ref, use)

        @pl.when(pl.program_id(1) == nk - 1)
        def _():
            o16_ref[...] = o_ref[...].astype(BF16)

    def spec(first, count):
        def imap(j, k):
            used = jnp.logical_and(j >= first, j < first + count)
            return (jnp.where(used, k, 0), jnp.clip(j - first, 0, count - 1))
        return pl.BlockSpec((tk, PBLK), imap)

    nm = MID_W // PBLK
    oblk = pl.BlockSpec((PBLK, d), lambda j, k: (j, 0))
    return pl.pallas_call(
        body, out_shape=(_sds((IN_WIDTH, d), F32), _sds((IN_WIDTH, d), BF16)), grid=(N_PBLK, nk),
        in_specs=[spec(0, 1), spec(1, nm), spec(1 + nm, LG_W // PBLK), pl.BlockSpec((tk, d), lambda j, k: (k, 0))],
        out_specs=(oblk, oblk),
        compiler_params=_cp("parallel", "arbitrary"), name="win_grad")(d0, d1, d2, h)


def h_bwd(d0, d1, d2, w_t, x, dx_out, g, scatter=()):
    s, d = x.shape
    tm = min(s, 512)
    nt = s // tm
    ns = len(scatter)
    na = ns // 2

    def body(d0_ref, d1_ref, d2_ref, w_ref, x_ref, dxo_ref, g_ref, *rest):
        s_in, (dx_ref, gg_ref), s_out = rest[:ns], rest[ns:ns + 2], rest[ns + 2:2 * ns + 2]
        if ns:
            start, finish = scatter_stages([a.shape[1:] for a in scatter[:na]], s_in[:na], s_in[na:], s_out[:na], s_out[na:],
                                           *rest[2 * ns + 2:])
            pl.when(pl.program_id(0) == 0)(start)

        @pl.when(pl.program_id(0) == 0)
        def _():
            gg_ref[...] = jnp.zeros_like(gg_ref)

        dh = (_dot(d0_ref[...], w_ref[0:PBLK, :]) + _dot(d1_ref[...], w_ref[PBLK:PBLK + MID_W, :])
              + _dot(d2_ref[...], w_ref[PBLK + MID_W:, :]))
        xf = x_ref[...]
        r = lax.rsqrt(jnp.mean(xf * xf, axis=-1, keepdims=True) + EPS)
        xh = xf * r
        gg_ref[...] += jnp.sum(dh * xh, axis=0, keepdims=True)
        u = dh * g_ref[...]
        dx_ref[...] = dxo_ref[...] + r * (u - xh * jnp.mean(u * xh, axis=-1, keepdims=True))
        if ns:
            pl.when(pl.program_id(0) == nt - 1)(finish)

    rowb = _rows(tm, d)
    out = pl.pallas_call(
        body, out_shape=(_sds((s, d), F32), _sds((1, d), F32)) + scatter_out_shapes(scatter[:na]), grid=(nt,),
        in_specs=[_rows(tm, PBLK), _rows(tm, MID_W), _rows(tm, LG_W),
                  pl.BlockSpec(w_t.shape, lambda i: (0, 0), pipeline_mode=pl.Buffered(1)), rowb, rowb, _full((1, d))] + [_ANY] * ns,
        out_specs=(rowb, _full((1, d))) + (_ANY,) * ns,
        scratch_shapes=scatter_sems(na) if ns else [],
        compiler_params=_cp("arbitrary"), name="h_bwd_scatter" if ns else "h_bwd")(d0, d1, d2, w_t, x, dx_out, g, *scatter)
    return out[0], out[1], list(out[2:2 + na]), list(out[2 + na:])


def memkv_bwd(mem, g, mem_n, w_kv, dkv):
    m, d = mem.shape

    def body(mem_ref, g_ref, mn_ref, w_ref, dkv_ref, gw_ref, gw16_ref, gg_ref):
        dkb = dkv_ref[...].astype(BF16)
        gw = _dot_tn(mn_ref[...], dkb)
        gw_ref[...] = gw
        gw16_ref[...] = gw.astype(BF16)
        dmn = _dot_nt(dkb, w_ref[...])
        mf = mem_ref[...]
        r = lax.rsqrt(jnp.mean(mf * mf, axis=-1, keepdims=True) + EPS)
        gg_ref[...] = jnp.sum(dmn * (mf * r), axis=0, keepdims=True)

    return pl.pallas_call(
        body, out_shape=(_sds(w_kv.shape, F32), _sds(w_kv.shape, BF16), _sds((1, d), F32)),
        compiler_params=_cp(), name="memkv_bwd")(mem, g, mem_n, w_kv, dkv)


def _layer_consts(seq):
    i = jnp.arange(A_WIDTH)
    return dict(
        tabs=rope_tables(seq),
        gq=_group_ones(A_WIDTH, A_HEAD_DIM).astype(BF16), gk=_group_ones(A_KV_WIDTH, A_HEAD_DIM).astype(BF16),
        fold_q=(i[:, None] % A_HEAD_DIM == jnp.arange(LANES)[None, :]).astype(F32),
        fold_k=(i[:A_KV_WIDTH, None] % A_HEAD_DIM == jnp.arange(LANES)[None, :]).astype(F32),
        head_sel=(jnp.arange(A_HEADS)[:, None] == i[None, :] // A_HEAD_DIM).astype(F32),
    )


_BIG = ("win_t", "wkv", "wbr", "wout")


def _with_own_part(names, gathered, shards, chip, d):
    shape = dict(win_t=(IN_WIDTH, d), wkv=(d, 2 * M_WIDTH), wbr=(N_CHIPS, N_BRANCH, A_WIDTH, d // N_CHIPS), wout=(d, d))
    return {n: lax.dynamic_update_slice(g, sh[None], (chip, 0, 0)).reshape(shape[n]) for n, g, sh in zip(names, gathered, shards)}


def local_fwd_bwd(x, mem, tgt, small, big=None, shards=None, place=None):
    s, d = x.shape
    depth = small["norm_g"].shape[0]
    k = _layer_consts(s)
    row = lambda v: v.reshape(1, -1)
    dist = shards is not None
    if dist:
        big = [_with_own_part(_BIG[:1], allgather_layer(shards[0][:1]), shards[0][:1], place[0], d)] + [None] * (depth - 1)
    saved = []
    for l in range(depth):
        ng = row(small["norm_g"][l])
        qg = row(jnp.tile(small["q_norm_g"][l], A_HEADS))
        kg = row(jnp.tile(small["k_norm_g"][l], A_KV_HEADS))
        ws = small["w_s"][l].astype(BF16)
        ws_t = jnp.swapaxes(small["w_s"][l], 1, 2).astype(BF16)
        bsb = jnp.broadcast_to(small["b_s"][l][:, :, None], (B_GROUPS, CHUNK, B_GROUP_DIM))
        lng, lnb = row(small["sg_ln_g"][l]), row(small["sg_ln_b"][l])
        mg = row(small["mem_norm_g"][l])
        w = big[l]
        h = rms_fwd(x, ng) if l == 0 else h_next
        proj = proj_fwd(h, w["win_t"])
        q_t, kr, vb, kt0, kt1, vte0, vte1 = qk_prep(proj, k["tabs"], qg, kg, k["gq"], k["gk"])
        late = list(shards[0][1:]) if dist and l == 0 else []
        nxt = list(shards[l + 1]) if dist and l + 1 < depth else []
        o_a, lse, gathered = attn_fwd(q_t, kr, vte0, vte1, gather=tuple(late + nxt))
        if late:
            w.update(_with_own_part(_BIG[1:], gathered[:len(late)], late, place[0], d))
        if nxt:
            big[l + 1] = _with_own_part(_BIG, gathered[len(late):], nxt, place[0], d)
        mem_n, kv = memkv_fwd(mem, mg, w["wkv"])
        next_g = row(small["norm_g"][l + 1]) if l + 1 < depth else row(small["final_g"])
        x_next, y, up, merged, h_next = branch_fwd(x, proj, o_a, kv, ws, bsb, lng, lnb, w["wbr"], w["wout"], next_g)
        saved.append(dict(x=x, ng=ng, qg=qg, kg=kg, ws=ws, ws_t=ws_t, bsb=bsb, lng=lng, lnb=lnb, mg=mg, h=h, proj=proj,
                          q_t=q_t, kr=kr, kt0=kt0, kt1=kt1, vb=vb, o_a=o_a, lse=lse, mem_n=mem_n, kv=kv, y=y, up=up, merged=merged))
        x = x_next

    sq, dx, g_final = final_loss(x, row(small["final_g"]), tgt)
    grads = {n: [None] * depth for n in ("norm_g", "q_norm_g", "k_norm_g", "sg_ln_g", "sg_ln_b", "w_s", "b_s", "mem_norm_g")}
    parts = lambda g: g.reshape(N_CHIPS, -1, g.shape[-1])
    reduced = [[None] * len(_BIG) for _ in range(depth)]

    def reduce_all(items, t_sib, t_rem):
        if items:
            for (ll, a, _, _), f in zip(items, reduce_rows(place, [i[2] for i in items], t_sib, t_rem)):
                reduced[ll][a] = f

    as_scatter = lambda items: tuple(i[2] for i in items) + tuple(i[3] for i in items)
    pending = []
    for l in reversed(range(depth)):
        sv, w = saved[l], big[l]
        dy, dlg, g_wout, g_wbr, g_wout16, g_wbr16 = merge_bwd(dx, sv["proj"], sv["y"], sv["up"], sv["merged"], w["wbr"], w["wout"])
        dmid, do_t, delta, g_ws, g_bs, g_lng, g_lnb, dkv = branch_bwd(
            dy, sv["proj"], sv["o_a"], sv["kv"], sv["ws"], sv["ws_t"], sv["bsb"], sv["lng"], sv["lnb"], k["head_sel"])
        g_wkv, g_wkv16, g_mg = memkv_bwd(mem, sv["mg"], sv["mem_n"], w["wkv"], dkv)
        if dist:
            pending += [(l, 1, parts(g_wkv), parts(g_wkv16)), (l, 2, parts(g_wbr), parts(g_wbr16)), (l, 3, parts(g_wout), parts(g_wout16))]
        dq_t, dkr, dvb, t_sib, t_rem = attn_bwd(sv["q_t"], do_t, sv["kr"], sv["kt0"], sv["kt1"], sv["vb"], sv["lse"], delta,
                                                scatter=as_scatter(pending))
        reduce_all(pending, t_sib, t_rem)
        dqkv, g_qg, g_kg = qk_prep_bwd(sv["proj"], dq_t, dkr, dvb, k["tabs"], sv["qg"], sv["kg"], k["gq"], k["gk"],
                                       k["fold_q"], k["fold_k"])
        g_win, g_win16 = win_grad(dqkv, dmid, dlg, sv["h"])
        pending = [(l, 0, parts(g_win), parts(g_win16))] if dist else []
        last = as_scatter(pending) if l == 0 else ()
        dx, g_ng, t_sib, t_rem = h_bwd(dqkv, dmid, dlg, w["win_t"], sv["x"], dx, sv["ng"], scatter=last)
        if last:
            reduce_all(pending, t_sib, t_rem)
        grads["norm_g"][l] = g_ng[0]
        grads["q_norm_g"][l] = g_qg[0, :A_HEAD_DIM]
        grads["k_norm_g"][l] = g_kg[0, :A_HEAD_DIM]
        grads["sg_ln_g"][l] = g_lng[0]
        grads["sg_ln_b"][l] = g_lnb[0]
        grads["w_s"][l] = g_ws
        grads["b_s"][l] = g_bs[:, :, 0]
        grads["mem_norm_g"][l] = g_mg[0]
        if not dist:
            reduced[l] = dict(zip(_BIG, (parts(g_win), parts(g_wkv), parts(g_wbr), parts(g_wout))))
    grads = {n: jnp.stack(v) for n, v in grads.items()}
    grads["final_g"] = g_final[0]
    return sq[0, 0], dx, grads, reduced


def _row_block(rows, width, cap_bytes=2 * 2**20):
    best = None
    for br in range(8, rows + 1, 8):
        if rows % br == 0 and br * width * 4 <= cap_bytes:
            best = br
    return best if best is not None else rows


def adamw(w, g, m, v):
    r, c = w.shape
    br = _row_block(r, c)

    def body(w_ref, g_ref, m_ref, v_ref, d_ref, nm_ref, nv_ref):
        gg = g_ref[...]
        mm = ADAM_B1 * m_ref[...] + (1.0 - ADAM_B1) * gg
        vv = ADAM_B2 * v_ref[...] + (1.0 - ADAM_B2) * (gg * gg)
        m_hat = mm / (1.0 - ADAM_B1 ** ADAM_STEP)
        v_hat = vv / (1.0 - ADAM_B2 ** ADAM_STEP)
        d_ref[...] = -ADAM_LR * (m_hat / (jnp.sqrt(v_hat) + ADAM_EPS) + ADAM_WD * w_ref[...])
        nm_ref[...] = mm
        nv_ref[...] = vv

    blk = _rows(br, c)
    return pl.pallas_call(
        body, out_shape=(_sds((r, c), F32),) * 3, grid=(r // br,), in_specs=[blk] * 4, out_specs=(blk,) * 3,
        compiler_params=_cp("parallel"), name="adamw")(w, g, m, v)


N_REMOTE = 2 * (N_CHIPS - 1)


def reduce_rows(place, gs, t_sibs, t_rems):
    n = len(gs)
    nt = 2

    def body(place_ref, *refs):
        for a in range(n):
            g_ref, s_ref, t_ref, f_ref = refs[a], refs[n + a], refs[2 * n + a], refs[3 * n + a]
            acc = g_ref[...] + s_ref[...]
            for j in range(N_REMOTE):
                acc = acc + t_ref[j].astype(F32)
            f_ref[...] = acc

    tiles = [(g.shape[1] // 2 // nt, g.shape[2]) for g in gs]
    return pl.pallas_call(
        body, out_shape=tuple(_sds(g.shape[1:], F32) for g in gs),
        grid_spec=pltpu.PrefetchScalarGridSpec(
            num_scalar_prefetch=1, grid=(nt,),
            in_specs=[pl.BlockSpec((None, tr, c), lambda i, p: (p[0], p[1] * nt + i, 0)) for tr, c in tiles]
            + [pl.BlockSpec((tr, c), lambda i, p: (i, 0)) for tr, c in tiles]
            + [pl.BlockSpec((N_REMOTE, tr, c), lambda i, p: (0, i, 0)) for tr, c in tiles],
            out_specs=tuple(pl.BlockSpec((tr, c), lambda i, p: (p[1] * nt + i, 0)) for tr, c in tiles)),
        compiler_params=_cp("parallel"), name="reduce_rows")(place, *gs, *t_sibs, *t_rems)


_ANY = pl.BlockSpec(memory_space=pl.ANY)


def _place():
    x, y, c = lax.axis_index("x"), lax.axis_index("y"), lax.axis_index("c")
    chips = [(1 - x, y), (x, 1 - y), (1 - x, 1 - y)]
    return x, y, c, chips


def gather_sems(n):
    return [pltpu.SemaphoreType.DMA((n, N_REMOTE)), pltpu.SemaphoreType.DMA((n, N_REMOTE))]


def gather_stages(shapes, ins, outs, send, recv):
    n = len(shapes)
    x, y, c, chips = _place()
    me = 2 * x + y
    sib = (x, y, 1 - c)

    def rows(a, hl):
        r2 = shapes[a][0] // 2
        return pl.ds(hl * r2, r2)

    def remote(a, k, src, dst, dev):
        return pltpu.make_async_remote_copy(src, dst, send.at[a, k], recv.at[a, k], device_id=dev, device_id_type=MESH)

    def sent(a, k):
        cx, cy = chips[k]
        return remote(a, k, ins[a].at[rows(a, c)], outs[a].at[me, rows(a, c)], (cx, cy, c))

    def passed(a, k, hl):
        cx, cy = chips[k]
        got = outs[a].at[2 * cx + cy, rows(a, hl)]
        return remote(a, k, got, got, (cx, cy, c)), remote(a, 3 + k, got, got, sib)

    def start():
        for a in range(n):
            for k in range(3):
                sent(a, k).start()

    def forward():
        for k in range(3):
            for a in range(n):
                arrived, on = passed(a, k, c)
                arrived.wait_recv()
                on.start()

    def finish():
        for k in range(3):
            for a in range(n):
                passed(a, k, 1 - c)[1].wait_recv()
        for k in range(3):
            for a in range(n):
                sent(a, k).wait_send()
                passed(a, k, c)[1].wait_send()

    return start, forward, finish


def allgather_layer(shards):
    n = len(shards)

    def body(*refs):
        for stage in gather_stages([a.shape for a in shards], refs[:n], refs[n:2 * n], *refs[2 * n:]):
            stage()

    return pl.pallas_call(
        body, out_shape=tuple(_sds((N_CHIPS,) + a.shape, a.dtype) for a in shards),
        in_specs=[_ANY] * n, out_specs=(_ANY,) * n, scratch_shapes=gather_sems(n), name="allgather_layer")(*shards)


def scatter_sems(n):
    return [pltpu.SemaphoreType.DMA((n, N_REMOTE + 1)), pltpu.SemaphoreType.DMA((n, N_REMOTE + 1))]


def scatter_out_shapes(gs):
    return (tuple(_sds((g.shape[1] // 2, g.shape[2]), F32) for g in gs)
            + tuple(_sds((N_REMOTE, g.shape[1] // 2, g.shape[2]), BF16) for g in gs))


def scatter_stages(shapes, gf, gb, t_sib, t_rem, send, recv):
    n = len(shapes)
    x, y, c, chips = _place()
    me = 2 * x + y

    def copies():
        out = []
        for a in range(n):
            r2 = shapes[a][0] // 2
            out.append(pltpu.make_async_remote_copy(gf[a].at[me, pl.ds((1 - c) * r2, r2)], t_sib[a], send.at[a, N_REMOTE],
                                                    recv.at[a, N_REMOTE], device_id=(x, y, 1 - c), device_id_type=MESH))
            for k, (cx, cy) in enumerate(chips):
                for o in range(2):
                    tc = c if o == 0 else 1 - c
                    out.append(pltpu.make_async_remote_copy(gb[a].at[2 * cx + cy, pl.ds(tc * r2, r2)], t_rem[a].at[2 * k + o],
                                                            send.at[a, 2 * k + o], recv.at[a, 2 * k + o],
                                                            device_id=(cx, cy, tc), device_id_type=MESH))
        return out

    def start():
        for cp in copies():
            cp.start()

    def finish():
        for cp in copies():
            cp.wait()

    return start, finish


def finish_exchange(v, fs):
    n = len(fs)
    r, w = v.shape
    ndev = 2 * N_CHIPS

    def body(v_ref, *refs):
        out, sum_ref = refs[n:2 * n], refs[2 * n]
        all_ref, send, recv, loc, fsend, frecv = refs[2 * n + 1:]
        x, y, c, chips = _place()
        me, sib = (x, y, c), (x, y, 1 - c)
        swaps = []
        for a in range(n):
            r2 = fs[a].shape[0] // 2
            mine = out[a].at[pl.ds(c * r2, r2)]
            cp = pltpu.make_async_remote_copy(mine, mine, fsend.at[a], frecv.at[a], device_id=sib, device_id_type=MESH)
            cp.start()
            swaps.append(cp)

        def slab(px, py, pc):
            return all_ref.at[4 * px + 2 * py + pc]

        def copy(k, block, to, src=None):
            return pltpu.make_async_remote_copy(slab(*block) if src is None else src, slab(*block), send.at[k], recv.at[k],
                                                device_id=to, device_id_type=MESH)

        mine = pltpu.make_async_copy(v_ref, slab(*me), loc)
        mine.start()
        first = [copy(0, me, sib, src=v_ref)] + [copy(1 + j, me, (*chip, c), src=v_ref) for j, chip in enumerate(chips)]
        for cp in first:
            cp.start()
        passed = [copy(4 + j, (*chip, c), sib) for j, chip in enumerate(chips)]
        for j, chip in enumerate(chips):
            copy(1 + j, (*chip, c), me).wait_recv()
            passed[j].start()
        copy(0, sib, me).wait_recv()
        for j, chip in enumerate(chips):
            copy(4 + j, (*chip, 1 - c), me).wait_recv()
        for cp in first + passed:
            cp.wait_send()
        mine.wait()
        acc = all_ref[0]
        for i in range(1, ndev):
            acc = acc + all_ref[i]
        sum_ref[...] = acc
        for a, cp in enumerate(swaps):
            r2 = fs[a].shape[0] // 2
            theirs = out[a].at[pl.ds((1 - c) * r2, r2)]
            cp.wait_send()
            pltpu.make_async_remote_copy(theirs, theirs, fsend.at[a], frecv.at[a], device_id=sib, device_id_type=MESH).wait_recv()

    vm = pl.BlockSpec(memory_space=pltpu.VMEM)
    res = pl.pallas_call(
        body, out_shape=tuple(_sds(f.shape, F32) for f in fs) + (_sds((r, w), F32),),
        in_specs=[vm] + [_ANY] * n, out_specs=(_ANY,) * n + (vm,), input_output_aliases={a + 1: a for a in range(n)},
        scratch_shapes=[pltpu.VMEM((ndev, r, w), F32), pltpu.SemaphoreType.DMA((7,)), pltpu.SemaphoreType.DMA((7,)),
                        pltpu.SemaphoreType.DMA, pltpu.SemaphoreType.DMA((n,)), pltpu.SemaphoreType.DMA((n,))],
        compiler_params=pltpu.CompilerParams(vmem_limit_bytes=VMEM_LIMIT), name="finish_exchange")(v, *fs)
    return res[n], list(res[:n])


_SMALL = ("norm_g", "q_norm_g", "k_norm_g", "sg_ln_g", "sg_ln_b", "w_s", "b_s", "mem_norm_g", "final_g")
_WEIGHTS = ("norm_g", "w_in", "q_norm_g", "k_norm_g", "sg_ln_g", "sg_ln_b", "w_s", "b_s", "mem_norm_g", "w_mem_kv", "w_br",
            "w_out", "final_g")


def _pack(d):
    flat = jnp.concatenate([d[n].reshape(-1) for n in _SMALL])
    rows = -(-flat.shape[0] // (8 * LANES)) * 8
    return jnp.pad(flat, (0, rows * LANES - flat.shape[0])).reshape(rows, LANES)


def _unpack(p, like):
    flat, out, o = p.reshape(-1), {}, 0
    for n in _SMALL:
        out[n] = flat[o:o + like[n].size].reshape(like[n].shape)
        o += like[n].size
    return out


def kernel(x, mem, norm_g, w_in, q_norm_g, k_norm_g, sg_ln_g, sg_ln_b, w_s, b_s, mem_norm_g, w_mem_kv, w_br, w_out, final_g, loss_target, m_norm_g, m_w_in, m_q_norm_g, m_k_norm_g, m_sg_ln_g, m_sg_ln_b, m_w_s, m_b_s, m_mem_norm_g, m_w_mem_kv, m_w_br, m_w_out, m_final_g, v_norm_g, v_w_in, v_q_norm_g, v_k_norm_g, v_sg_ln_g, v_sg_ln_b, v_w_s, v_b_s, v_mem_norm_g, v_w_mem_kv, v_w_br, v_w_out, v_final_g):
    w = dict(norm_g=norm_g, w_in=w_in, q_norm_g=q_norm_g, k_norm_g=k_norm_g, sg_ln_g=sg_ln_g, sg_ln_b=sg_ln_b, w_s=w_s, b_s=b_s,
             mem_norm_g=mem_norm_g, w_mem_kv=w_mem_kv, w_br=w_br, w_out=w_out, final_g=final_g)
    m = dict(norm_g=m_norm_g, w_in=m_w_in, q_norm_g=m_q_norm_g, k_norm_g=m_k_norm_g, sg_ln_g=m_sg_ln_g, sg_ln_b=m_sg_ln_b,
             w_s=m_w_s, b_s=m_b_s, mem_norm_g=m_mem_norm_g, w_mem_kv=m_w_mem_kv, w_br=m_w_br, w_out=m_w_out, final_g=m_final_g)
    v = dict(norm_g=v_norm_g, w_in=v_w_in, q_norm_g=v_q_norm_g, k_norm_g=v_k_norm_g, sg_ln_g=v_sg_ln_g, sg_ln_b=v_sg_ln_b,
             w_s=v_w_s, b_s=v_b_s, mem_norm_g=v_mem_norm_g, w_mem_kv=v_w_mem_kv, w_br=v_w_br, w_out=v_w_out, final_g=v_final_g)
    depth, d = norm_g.shape
    nsh = N_CHIPS
    br_rows = N_BRANCH * A_WIDTH
    br_cols = d // nsh

    shards = [[jnp.swapaxes(w_in[l], 0, 1).astype(BF16), w_mem_kv[l].astype(BF16), w_br[l].astype(BF16).reshape(br_rows, br_cols),
               w_out[l].astype(BF16)] for l in range(depth)]
    place = jnp.stack([2 * lax.axis_index("x") + lax.axis_index("y"), lax.axis_index("c")]).astype(jnp.int32)
    small = {n: w[n] for n in _SMALL}

    sq, dx, grads, reduced = local_fwd_bwd(x[0], mem[0], loss_target[0], small, shards=shards, place=place)
    loss = (0.5 / d) * lax.psum(sq, ("x", "y", "c"))

    small_sum, finals = finish_exchange(_pack(grads), [g for layer in reduced for g in layer])
    finals = [jnp.stack(finals[a::len(_BIG)]) for a in range(len(_BIG))]
    big_grads = dict(w_in=finals[0], w_mem_kv=finals[1], w_br=finals[2].reshape(depth, N_BRANCH, A_WIDTH, br_cols), w_out=finals[3])
    small_grads = _unpack(small_sum, small)

    out_g, out_d, out_m, out_v = {}, {}, {}, {}
    sd, sm, sv = adamw(_pack(small), _pack(small_grads), _pack({n: m[n] for n in _SMALL}), _pack({n: v[n] for n in _SMALL}))
    sd, sm, sv = _unpack(sd, small), _unpack(sm, small), _unpack(sv, small)
    for n in _SMALL:
        out_g[n], out_d[n], out_m[n], out_v[n] = small_grads[n], sd[n], sm[n], sv[n]
    for n, g in big_grads.items():
        into = (lambda a: jnp.swapaxes(a, 1, 2)) if n == "w_in" else (lambda a: a)
        two_d = lambda a: a.reshape(-1, a.shape[-1])
        res = adamw(two_d(into(w[n])), two_d(g), two_d(into(m[n])), two_d(into(v[n])))
        out_g[n], out_d[n], out_m[n], out_v[n] = [into(t.reshape(g.shape)) for t in (two_d(g),) + tuple(res)]
    return (loss, dx[None], *[out_g[n] for n in _WEIGHTS], *[out_d[n] for n in _WEIGHTS], *[out_m[n] for n in _WEIGHTS],
            *[out_v[n] for n in _WEIGHTS])
```

```python
import functools

import jax
import jax.numpy as jnp
from jax import lax
from jax.experimental import pallas as pl
from jax.experimental.pallas import tpu as pltpu

F32 = jnp.float32
BF16 = jnp.bfloat16

D_MODEL = 1024
GRID_W = 64
CHUNK = 128
ROPE_THETA = 10000.0
EPS = 1e-6
A_HEADS, A_KV_HEADS, A_HEAD_DIM = 8, 2, 64
A_WIDTH, A_KV_WIDTH = 512, 128
B_GROUPS, B_GROUP_DIM, B_WIDTH = 4, 128, 512
M_HEADS, M_HEAD_DIM, M_WIDTH = 4, 128, 512
N_BRANCH = 3
IN_WIDTH = 6912
O_QA, O_KA, O_VA, O_ZA, O_UB, O_VB, O_ZB, O_QM, O_ZM, O_LG = 0, 512, 640, 768, 1280, 1792, 2304, 2816, 3328, 3840
PBLK = 768
N_PBLK = IN_WIDTH // PBLK
MID_W = 3072
LG_W = 3072

LN2 = 0.6931471805599453
Q_SCALE = A_HEAD_DIM ** -0.5 / LN2
VTE_ROWS = A_HEAD_DIM + 16

ADAM_LR, ADAM_B1, ADAM_B2, ADAM_EPS, ADAM_WD, ADAM_STEP = 0.001, 0.9, 0.999, 1e-08, 0.01, 10

V7X_VMEM_BYTES = 64 * 2**20
VMEM_LIMIT = V7X_VMEM_BYTES - 8 * 2**20
LANES = 128
MESH = pl.DeviceIdType.MESH
N_CHIPS = 4


def _cp(*sem):
    return pltpu.CompilerParams(dimension_semantics=sem if sem else None, vmem_limit_bytes=VMEM_LIMIT)


def _dot(a, b):
    return jnp.dot(a, b, preferred_element_type=F32)


def _dot_nt(a, b):
    return lax.dot_general(a, b, (((1,), (1,)), ((), ())), preferred_element_type=F32)


def _dot_tn(a, b):
    return lax.dot_general(a, b, (((0,), (0,)), ((), ())), preferred_element_type=F32)


def _dot_hi(a, b):
    return jnp.dot(a, b, preferred_element_type=F32, precision=lax.Precision.HIGHEST)


def _group_sum(a, ones):
    hi = a.astype(BF16)
    lo = (a - hi.astype(F32)).astype(BF16)
    return _dot(hi, ones) + _dot(lo, ones)


def _dot_nt_hi(a, b):
    return lax.dot_general(a, b, (((1,), (1,)), ((), ())), preferred_element_type=F32, precision=lax.Precision.HIGHEST)


def _sig(z):
    return 1.0 / (1.0 + jnp.exp(-z))


def _full(shape, once=False):
    nd = len(shape)
    return pl.BlockSpec(shape, lambda *_: (0,) * nd, pipeline_mode=pl.Buffered(1) if once else None)


def _rows(tm, width):
    return pl.BlockSpec((tm, width), lambda i: (i, 0))


def _sds(shape, dtype):
    return jax.ShapeDtypeStruct(shape, dtype)


def rms_fwd(x, g):
    s, d = x.shape
    tm = min(s, 512)

    def body(x_ref, g_ref, h_ref):
        xf = x_ref[...]
        r = lax.rsqrt(jnp.mean(xf * xf, axis=-1, keepdims=True) + EPS)
        h_ref[...] = ((xf * r) * g_ref[...]).astype(BF16)

    return pl.pallas_call(
        body, out_shape=_sds((s, d), BF16), grid=(s // tm,),
        in_specs=[_rows(tm, d), _full((1, d))], out_specs=_rows(tm, d),
        compiler_params=_cp("parallel"), name="rms_fwd")(x, g)


def proj_fwd(h, w_t):
    s, d = h.shape
    n = w_t.shape[0]
    tm = min(s, 512)
    tn = 2304

    def body(h_ref, w_ref, o_ref):
        o_ref[...] = _dot_nt(h_ref[...], w_ref[...]).astype(BF16)

    return pl.pallas_call(
        body, out_shape=_sds((s, n), BF16), grid=(n // tn, s // tm),
        in_specs=[pl.BlockSpec((tm, d), lambda j, i: (i, 0)), pl.BlockSpec((tn, d), lambda j, i: (j, 0))],
        out_specs=pl.BlockSpec((tm, tn), lambda j, i: (i, j)),
        compiler_params=_cp("parallel", "parallel"), name="proj_fwd")(h, w_t)


def rope_tables(seq):
    n_freq = A_HEAD_DIM // 4
    d = jnp.arange(LANES) % A_HEAD_DIM
    seg, half, freq = d // (2 * n_freq), (d % (2 * n_freq)) // n_freq, d % n_freq
    inv = ROPE_THETA ** (-freq.astype(F32) / n_freq)
    t = jnp.arange(seq)
    pos = jnp.where(seg[None, :] == 0, (t // GRID_W)[:, None], (t % GRID_W)[:, None]).astype(F32)
    ang = pos * inv[None, :]
    cos, sin = jnp.cos(ang), jnp.sin(ang)
    return cos, jnp.where(half[None, :] == 1, sin, 0.0), jnp.where(half[None, :] == 0, -sin, 0.0)


def _group_ones(width, group):
    i = jnp.arange(width)
    return (i[:, None] // group == i[None, :] // group).astype(F32)


def _rope(xn, c, sa, sb):
    w = xn.shape[1]
    return xn * c + pltpu.roll(xn, 16, 1) * sa + pltpu.roll(xn, w - 16, 1) * sb


def _rope_t(dy, c, sa, sb):
    w = dy.shape[1]
    return dy * c + pltpu.roll(dy * sa, w - 16, 1) + pltpu.roll(dy * sb, 16, 1)


def _tile4(t):
    return jnp.concatenate([t, t, t, t], axis=1)


def qk_prep(proj, tabs, qg, kg, gq, gk):
    s = proj.shape[0]
    tm = min(s, 512)
    c, sa, sb = tabs

    def body(p_ref, c_ref, sa_ref, sb_ref, qg_ref, kg_ref, gq_ref, gk_ref, qt_ref, kr_ref, vb_ref, kt_ref, v0_ref, v1_ref):
        xq = p_ref[:, O_QA:O_QA + A_WIDTH].astype(F32)
        xk = p_ref[:, O_KA:O_KA + A_KV_WIDTH].astype(F32)
        xv = p_ref[:, O_VA:O_VA + A_KV_WIDTH].astype(F32)
        cc, ssa, ssb = c_ref[...], sa_ref[...], sb_ref[...]
        msq = _group_sum(xq * xq, gq_ref[...]) * (1.0 / A_HEAD_DIM)
        qn = (xq * lax.rsqrt(msq + EPS)) * qg_ref[...]
        qr = _rope(qn, _tile4(cc), _tile4(ssa), _tile4(ssb)) * Q_SCALE
        qt_ref[...] = qr.T.astype(BF16)
        msk = _group_sum(xk * xk, gk_ref[...]) * (1.0 / A_HEAD_DIM)
        kn = (xk * lax.rsqrt(msk + EPS)) * kg_ref[...]
        kr = _rope(kn, cc, ssa, ssb)
        kr_ref[...] = kr.astype(BF16)
        vb_ref[...] = xv.astype(BF16)
        kt_ref[...] = kr.T.astype(BF16)
        vt = xv.T.astype(BF16)
        one = jnp.ones((VTE_ROWS - A_HEAD_DIM, tm), BF16)
        v0_ref[...] = jnp.concatenate([vt[:A_HEAD_DIM], one], axis=0)
        v1_ref[...] = jnp.concatenate([vt[A_HEAD_DIM:], one], axis=0)

    tab = _rows(tm, LANES)
    colb = lambda w: pl.BlockSpec((w, tm), lambda i: (0, i))
    return pl.pallas_call(
        body,
        out_shape=(_sds((A_WIDTH, s), BF16), _sds((s, A_KV_WIDTH), BF16), _sds((s, A_KV_WIDTH), BF16),
                   _sds((A_KV_WIDTH, s), BF16), _sds((VTE_ROWS, s), BF16), _sds((VTE_ROWS, s), BF16)),
        grid=(s // tm,),
        in_specs=[_rows(tm, PBLK), tab, tab, tab, _full((1, A_WIDTH)), _full((1, A_KV_WIDTH)),
                  _full((A_WIDTH, A_WIDTH)), _full((A_KV_WIDTH, A_KV_WIDTH))],
        out_specs=(colb(A_WIDTH), _rows(tm, A_KV_WIDTH), _rows(tm, A_KV_WIDTH), colb(A_KV_WIDTH), colb(VTE_ROWS), colb(VTE_ROWS)),
        compiler_params=_cp("parallel"), name="qk_prep")(proj, c, sa, sb, qg, kg, gq, gk)


def _pad_head(q_h, kv):
    z = jnp.zeros_like(q_h)
    return jnp.concatenate([q_h, z], axis=0) if kv == 0 else jnp.concatenate([z, q_h], axis=0)


def attn_fwd(q_t, kr, vte0, vte1, gather=()):
    s = kr.shape[0]
    tq = min(s, 256)
    kc = min(s, 512)
    nkc = s // kc
    nq = s // tq
    grp = A_HEADS // A_KV_HEADS
    ng = len(gather)

    def body(qt_ref, kr_ref, v0_ref, v1_ref, *rest):
        g_in, (o_ref, lse_ref), g_out = rest[:ng], rest[ng:ng + 2], rest[ng + 2:2 * ng + 2]
        qp_ref, m_ref, acc_ref = rest[2 * ng + 2:2 * ng + 5]
        if ng:
            start, forward, finish = gather_stages([g.shape for g in gather], g_in, g_out, *rest[2 * ng + 5:])
            pl.when(pl.program_id(0) == 0)(start)
            pl.when(pl.program_id(0) == (3 * nq) // 4)(forward)

        for h in range(A_HEADS):
            qp_ref[h] = _pad_head(qt_ref[A_HEAD_DIM * h:A_HEAD_DIM * (h + 1), :], h // grp)
        m_ref[...] = jnp.full(m_ref.shape, -1e30, F32)
        acc_ref[...] = jnp.zeros_like(acc_ref)

        def step(ci, carry):
            ks = pl.ds(pl.multiple_of(ci * kc, kc), kc)
            kblk = kr_ref[ks, :]
            vts = (v0_ref[:, ks], v1_ref[:, ks])
            scs = [_dot(kblk, qp_ref[h]) for h in range(A_HEADS)]
            for h in range(A_HEADS):
                sc = scs[h]
                m_prev = m_ref[h:h + 1, :]
                m_new = jnp.maximum(m_prev, jnp.max(sc, axis=0, keepdims=True))
                p = jnp.exp2(sc - m_new)
                acc_ref[h] = acc_ref[h] * jnp.exp2(m_prev - m_new) + _dot(vts[h // grp], p.astype(BF16))
                m_ref[h:h + 1, :] = m_new
            return carry

        lax.fori_loop(0, nkc, step, 0)
        outs, lses = [], []
        for h in range(A_HEADS):
            acc = acc_ref[h]
            l = acc[A_HEAD_DIM:A_HEAD_DIM + 1, :]
            outs.append(acc[:A_HEAD_DIM, :] / l)
            lses.append(m_ref[h:h + 1, :] + jnp.log2(l))
        o_ref[...] = jnp.concatenate(outs, axis=0).T
        lse_ref[...] = jnp.concatenate(lses, axis=0)
        if ng:
            pl.when(pl.program_id(0) == nq - 1)(finish)

    out = pl.pallas_call(
        body,
        out_shape=(_sds((s, A_WIDTH), F32), _sds((A_HEADS, s), F32)) + tuple(_sds((N_CHIPS,) + g.shape, g.dtype) for g in gather),
        grid=(nq,),
        in_specs=[pl.BlockSpec((A_WIDTH, tq), lambda i: (0, i)), _full((s, A_KV_WIDTH)), _full((VTE_ROWS, s)),
                  _full((VTE_ROWS, s))] + [_ANY] * ng,
        out_specs=(_rows(tq, A_WIDTH), pl.BlockSpec((A_HEADS, tq), lambda i: (0, i))) + (_ANY,) * ng,
        scratch_shapes=[pltpu.VMEM((A_HEADS, A_KV_WIDTH, tq), BF16), pltpu.VMEM((A_HEADS, tq), F32),
                        pltpu.VMEM((A_HEADS, VTE_ROWS, tq), F32)] + (gather_sems(ng) if ng else []),
        compiler_params=_cp("arbitrary"), name="attn_fwd_gather" if ng else "attn_fwd")(q_t, kr, vte0, vte1, *gather)
    return out[0], out[1], list(out[2:])


def memkv_fwd(mem, g, w_kv):
    m, d = mem.shape

    def body(mem_ref, g_ref, w_ref, mn_ref, kv_ref):
        mf = mem_ref[...]
        r = lax.rsqrt(jnp.mean(mf * mf, axis=-1, keepdims=True) + EPS)
        mn = ((mf * r) * g_ref[...]).astype(BF16)
        mn_ref[...] = mn
        kv_ref[...] = _dot(mn, w_ref[...]).astype(BF16)

    return pl.pallas_call(
        body, out_shape=(_sds((m, d), BF16), _sds((m, 2 * M_WIDTH), BF16)),
        compiler_params=_cp(), name="memkv_fwd")(mem, g, w_kv)


def _layer_norm_stats(v):
    mu = jnp.mean(v, axis=-1, keepdims=True)
    xc = v - mu
    rstd = lax.rsqrt(jnp.mean(xc * xc, axis=-1, keepdims=True) + EPS)
    return xc * rstd, rstd


def _spatial_mix(vlb, ws_ref, bsb_ref, tm):
    rows = []
    for ci in range(tm // CHUNK):
        cols = []
        for g in range(B_GROUPS):
            blk = vlb[ci * CHUNK:(ci + 1) * CHUNK, g * B_GROUP_DIM:(g + 1) * B_GROUP_DIM]
            cols.append(_dot(ws_ref[g], blk) + bsb_ref[g])
        rows.append(jnp.concatenate(cols, axis=1))
    return jnp.concatenate(rows, axis=0)


def _mem_attn(qm, kv_ref):
    out = []
    for h in range(M_HEADS):
        qh = qm[:, h * M_HEAD_DIM:(h + 1) * M_HEAD_DIM].astype(BF16)
        kh = kv_ref[:, h * M_HEAD_DIM:(h + 1) * M_HEAD_DIM]
        vh = kv_ref[:, M_WIDTH + h * M_HEAD_DIM:M_WIDTH + (h + 1) * M_HEAD_DIM]
        sc = _dot_nt(qh, kh) * (M_HEAD_DIM ** -0.5)
        e = jnp.exp(sc - jnp.max(sc, axis=-1, keepdims=True))
        p = e / jnp.sum(e, axis=-1, keepdims=True)
        out.append((p, _dot(p.astype(BF16), vh)))
    return out


def branch_fwd(x, proj, o_a, kv, ws, bsb, ln_g, ln_b, w_br, w_out, next_g):
    s, d = x.shape
    tm = min(s, 256)

    def body(x_ref, p_ref, oa_ref, kv_ref, ws_ref, bsb_ref, lg_ref, lb_ref, wbr_ref, wo_ref, ng_ref,
             xn_ref, y_ref, up_ref, mg_ref, hn_ref):
        seg = lambda o, w: p_ref[:, o:o + w].astype(F32)
        z_a, u_b, v_b, z_b = seg(O_ZA, A_WIDTH), seg(O_UB, B_WIDTH), seg(O_VB, B_WIDTH), seg(O_ZB, B_WIDTH)
        q_m, z_m = seg(O_QM, M_WIDTH), seg(O_ZM, M_WIDTH)
        xhat, _ = _layer_norm_stats(v_b)
        vln = xhat * lg_ref[...] + lb_ref[...]
        mixed = _spatial_mix(vln.astype(BF16), ws_ref, bsb_ref, tm)
        y_b = (u_b * mixed) * (z_b * _sig(z_b))
        o_m = jnp.concatenate([o for _, o in _mem_attn(q_m, kv_ref)], axis=1)
        y_a = oa_ref[...] * (z_a * _sig(z_a))
        y_m = o_m * (z_m * _sig(z_m))
        merged = None
        for n, yy in enumerate((y_a, y_b, y_m)):
            yb = yy.astype(BF16)
            y_ref[n] = yb
            up = jnp.concatenate([_dot(yb, wbr_ref[c, n]) for c in range(N_CHIPS)], axis=1)
            up_ref[n] = up.astype(BF16)
            t = _sig(seg(O_LG + n * d, d)) * up
            merged = t if merged is None else merged + t
        mb = merged.astype(BF16)
        mg_ref[...] = mb
        xn = x_ref[...] + _dot(mb, wo_ref[...])
        xn_ref[...] = xn
        r = lax.rsqrt(jnp.mean(xn * xn, axis=-1, keepdims=True) + EPS)
        hn_ref[...] = ((xn * r) * ng_ref[...]).astype(BF16)

    return pl.pallas_call(
        body,
        out_shape=(_sds((s, d), F32), _sds((N_BRANCH, s, A_WIDTH), BF16), _sds((N_BRANCH, s, d), BF16), _sds((s, d), BF16),
                   _sds((s, d), BF16)),
        grid=(s // tm,),
        in_specs=[_rows(tm, d), _rows(tm, IN_WIDTH), _rows(tm, A_WIDTH), _full(kv.shape), _full(ws.shape), _full(bsb.shape),
                  _full((1, B_WIDTH)), _full((1, B_WIDTH)), _full(w_br.shape), _full(w_out.shape), _full((1, d))],
        out_specs=(_rows(tm, d), pl.BlockSpec((N_BRANCH, tm, A_WIDTH), lambda i: (0, i, 0)),
                   pl.BlockSpec((N_BRANCH, tm, d), lambda i: (0, i, 0)), _rows(tm, d), _rows(tm, d)),
        compiler_params=_cp("parallel"), name="branch_fwd")(x, proj, o_a, kv, ws, bsb, ln_g, ln_b, w_br, w_out, next_g)


def final_loss(x, fg, tgt):
    s, d = x.shape
    tm = min(s, 512)

    def body(x_ref, g_ref, t_ref, ls_ref, dx_ref, gg_ref):
        @pl.when(pl.program_id(0) == 0)
        def _():
            ls_ref[...] = jnp.zeros_like(ls_ref)
            gg_ref[...] = jnp.zeros_like(gg_ref)

        xf = x_ref[...]
        g = g_ref[...]
        r = lax.rsqrt(jnp.mean(xf * xf, axis=-1, keepdims=True) + EPS)
        xh = xf * r
        e = xh * g - t_ref[...]
        sq = jnp.sum(jnp.sum(e * e, axis=0, keepdims=True), axis=1, keepdims=True)
        ls_ref[...] += jnp.broadcast_to(sq, ls_ref.shape)
        dy = e * (1.0 / d)
        gg_ref[...] += jnp.sum(dy * xh, axis=0, keepdims=True)
        gy = dy * g
        dx_ref[...] = r * (gy - xh * jnp.mean(gy * xh, axis=-1, keepdims=True))

    return pl.pallas_call(
        body, out_shape=(_sds((1, LANES), F32), _sds((s, d), F32), _sds((1, d), F32)), grid=(s // tm,),
        in_specs=[_rows(tm, d), _full((1, d)), _rows(tm, d)],
        out_specs=(_full((1, LANES)), _rows(tm, d), _full((1, d))),
        compiler_params=_cp("arbitrary"), name="final_loss")(x, fg, tgt)


def _pblocks(tm, first, count):
    return [pl.BlockSpec((tm, PBLK), functools.partial(lambda i, b: (i, b), b=first + k)) for k in range(count)]


def merge_bwd(dx, proj, y, up, merged, w_br, w_out):
    s, d = dx.shape
    tm = min(s, 256)
    nlg = LG_W // PBLK
    cw = d // N_CHIPS

    def body(dx_ref, l0, l1, l2, l3, y_ref, up_ref, mg_ref, wbr_ref, wo_ref, dy_ref, dlg_ref, gwo_ref, gwb_ref, gwo16_ref, gwb16_ref):
        @pl.when(pl.program_id(0) == 0)
        def _():
            gwo_ref[...] = jnp.zeros_like(gwo_ref)
            gwb_ref[...] = jnp.zeros_like(gwb_ref)

        dxb = dx_ref[...].astype(BF16)
        dmg = _dot_nt(dxb, wo_ref[...])
        gwo_ref[...] += _dot_tn(mg_ref[...], dxb)
        lg = jnp.concatenate([l0[...], l1[...], l2[...], l3[...]], axis=1).astype(F32)
        for n in range(N_BRANCH):
            g = _sig(lg[:, n * d:(n + 1) * d])
            dup = dmg * g
            dlg_ref[:, n * d:(n + 1) * d] = ((dup * up_ref[n].astype(F32)) * (1.0 - g)).astype(BF16)
            dupb = dup.astype(BF16)
            dyn = None
            for c in range(N_CHIPS):
                blk = dupb[:, c * cw:(c + 1) * cw]
                gwb_ref[c, n] += _dot_tn(y_ref[n], blk)
                t = _dot_nt(blk, wbr_ref[c, n])
                dyn = t if dyn is None else dyn + t
            dy_ref[n] = dyn

        @pl.when(pl.program_id(0) == pl.num_programs(0) - 1)
        def _():
            gwo16_ref[...] = gwo_ref[...].astype(BF16)
            gwb16_ref[...] = gwb_ref[...].astype(BF16)

    return pl.pallas_call(
        body,
        out_shape=(_sds((N_BRANCH, s, A_WIDTH), F32), _sds((s, LG_W), BF16), _sds((d, d), F32), _sds(w_br.shape, F32),
                   _sds((d, d), BF16), _sds(w_br.shape, BF16)),
        grid=(s // tm,),
        in_specs=[_rows(tm, d)] + _pblocks(tm, O_LG // PBLK, nlg) + [
            pl.BlockSpec((N_BRANCH, tm, A_WIDTH), lambda i: (0, i, 0)), pl.BlockSpec((N_BRANCH, tm, d), lambda i: (0, i, 0)),
            _rows(tm, d), _full(w_br.shape, once=True), _full(w_out.shape, once=True)],
        out_specs=(pl.BlockSpec((N_BRANCH, tm, A_WIDTH), lambda i: (0, i, 0)), _rows(tm, LG_W), _full((d, d)), _full(w_br.shape),
                   _full((d, d)), _full(w_br.shape)),
        compiler_params=_cp("arbitrary"), name="merge_bwd")(dx, proj, proj, proj, proj, y, up, merged, w_br, w_out)


def _dsilu(z, sg):
    return sg * (1.0 + z * (1.0 - sg))


def branch_bwd(dy, proj, o_a, kv, ws, ws_t, bsb, ln_g, ln_b, head_sel):
    s = proj.shape[0]
    tm = min(s, 256)
    nmid = MID_W // PBLK

    def body(dy_ref, m0, m1, m2, m3, oa_ref, kv_ref, ws_ref, wst_ref, bsb_ref, lg_ref, lb_ref, sel_ref,
             dmid_ref, dot_ref, dl_ref, gws_ref, gbs_ref, glg_ref, glb_ref, dkv_ref):
        @pl.when(pl.program_id(0) == 0)
        def _():
            for r in (gws_ref, gbs_ref, glg_ref, glb_ref, dkv_ref):
                r[...] = jnp.zeros_like(r)

        mid = jnp.concatenate([m0[...], m1[...], m2[...], m3[...]], axis=1).astype(F32)
        seg = lambda o, w: mid[:, o - O_ZA:o - O_ZA + w]
        z_a, u_b, v_b, z_b = seg(O_ZA, A_WIDTH), seg(O_UB, B_WIDTH), seg(O_VB, B_WIDTH), seg(O_ZB, B_WIDTH)
        q_m, z_m = seg(O_QM, M_WIDTH), seg(O_ZM, M_WIDTH)

        def put(o, v):
            dmid_ref[:, o - O_ZA:o - O_ZA + v.shape[1]] = v.astype(BF16)

        dy_a, dy_b, dy_m = dy_ref[0], dy_ref[1], dy_ref[2]

        o_a_ = oa_ref[...]
        sg = _sig(z_a)
        do_a = dy_a * (z_a * sg)
        put(O_ZA, (dy_a * o_a_) * _dsilu(z_a, sg))
        do_l = do_a * LN2
        dot_ref[...] = do_l.T.astype(BF16)
        dl_ref[...] = _dot_nt_hi(sel_ref[...], do_l * o_a_)

        xhat, rstd = _layer_norm_stats(v_b)
        lng = lg_ref[...]
        vln = xhat * lng + lb_ref[...]
        vlb = vln.astype(BF16)
        mixed = _spatial_mix(vlb, ws_ref, bsb_ref, tm)
        sg = _sig(z_b)
        sl = z_b * sg
        put(O_UB, (dy_b * mixed) * sl)
        put(O_ZB, ((dy_b * u_b) * mixed) * _dsilu(z_b, sg))
        dmix = (dy_b * u_b) * sl
        dmb = dmix.astype(BF16)
        rows = []
        for ci in range(tm // CHUNK):
            cols = []
            for g in range(B_GROUPS):
                rs, cs = slice(ci * CHUNK, (ci + 1) * CHUNK), slice(g * B_GROUP_DIM, (g + 1) * B_GROUP_DIM)
                gws_ref[g] += _dot_nt(dmb[rs, cs], vlb[rs, cs])
                gbs_ref[g] += jnp.broadcast_to(jnp.sum(dmix[rs, cs], axis=1, keepdims=True), (CHUNK, B_GROUP_DIM))
                cols.append(_dot(wst_ref[g], dmb[rs, cs]))
            rows.append(jnp.concatenate(cols, axis=1))
        dvln = jnp.concatenate(rows, axis=0)
        glg_ref[...] += jnp.sum(dvln * xhat, axis=0, keepdims=True)
        glb_ref[...] += jnp.sum(dvln, axis=0, keepdims=True)
        gy = dvln * lng
        put(O_VB, rstd * ((gy - jnp.mean(gy, axis=-1, keepdims=True)) - xhat * jnp.mean(gy * xhat, axis=-1, keepdims=True)))

        sg = _sig(z_m)
        sl = z_m * sg
        heads = _mem_attn(q_m, kv_ref)
        o_m = jnp.concatenate([o for _, o in heads], axis=1)
        put(O_ZM, (dy_m * o_m) * _dsilu(z_m, sg))
        do_m = dy_m * sl
        dqs = []
        for h, (p, o_h) in enumerate(heads):
            hs = slice(h * M_HEAD_DIM, (h + 1) * M_HEAD_DIM)
            vs = slice(M_WIDTH + h * M_HEAD_DIM, M_WIDTH + (h + 1) * M_HEAD_DIM)
            do_h = do_m[:, hs]
            dob = do_h.astype(BF16)
            dp = _dot_nt(dob, kv_ref[:, vs])
            dsc = (p * (dp - jnp.sum(do_h * o_h, axis=-1, keepdims=True))) * (M_HEAD_DIM ** -0.5)
            dsb = dsc.astype(BF16)
            dqs.append(_dot(dsb, kv_ref[:, hs]))
            dkv_ref[:, hs] += _dot_tn(dsb, q_m[:, hs].astype(BF16))
            dkv_ref[:, vs] += _dot_tn(p.astype(BF16), dob)
        put(O_QM, jnp.concatenate(dqs, axis=1))

    return pl.pallas_call(
        body,
        out_shape=(_sds((s, MID_W), BF16), _sds((A_WIDTH, s), BF16), _sds((A_HEADS, s), F32), _sds(ws.shape, F32),
                   _sds(ws.shape, F32), _sds((1, B_WIDTH), F32), _sds((1, B_WIDTH), F32), _sds(kv.shape, F32)),
        grid=(s // tm,),
        in_specs=[pl.BlockSpec((N_BRANCH, tm, A_WIDTH), lambda i: (0, i, 0))] + _pblocks(tm, O_ZA // PBLK, nmid) + [
            _rows(tm, A_WIDTH), _full(kv.shape), _full(ws.shape), _full(ws.shape), _full(bsb.shape),
            _full((1, B_WIDTH)), _full((1, B_WIDTH)), _full(head_sel.shape)],
        out_specs=(_rows(tm, MID_W), pl.BlockSpec((A_WIDTH, tm), lambda i: (0, i)), pl.BlockSpec((A_HEADS, tm), lambda i: (0, i)),
                   _full(ws.shape), _full(ws.shape), _full((1, B_WIDTH)), _full((1, B_WIDTH)), _full(kv.shape)),
        compiler_params=_cp("arbitrary"), name="branch_bwd")(dy, proj, proj, proj, proj, o_a, kv, ws, ws_t, bsb, ln_g, ln_b, head_sel)


def attn_bwd(q_t, do_t, kr, kr_t, vb, lse, delta, scatter=()):
    s = kr.shape[0]
    tq = min(s, 256)
    kc = min(s, 512)
    nkc = s // kc
    nq = s // tq
    grp = A_HEADS // A_KV_HEADS
    ns = len(scatter)
    na = ns // 2

    def body(qt_ref, dot_ref, kr_ref, krt_ref, vb_ref, lse_ref, dl_ref, *rest):
        s_in, (dqt_ref, dk_ref, dv_ref), s_out = rest[:ns], rest[ns:ns + 3], rest[ns + 3:2 * ns + 3]
        qp_ref, dop_ref, dq_ref = rest[2 * ns + 3:2 * ns + 6]
        if ns:
            start, finish = scatter_stages([g.shape[1:] for g in scatter[:na]], s_in[:na], s_in[na:], s_out[:na], s_out[na:],
                                           *rest[2 * ns + 6:])
            pl.when(pl.program_id(0) == 0)(start)

        @pl.when(pl.program_id(0) == 0)
        def _():
            dk_ref[...] = jnp.zeros_like(dk_ref)
            dv_ref[...] = jnp.zeros_like(dv_ref)

        for h in range(A_HEADS):
            hs = slice(A_HEAD_DIM * h, A_HEAD_DIM * (h + 1))
            qp_ref[h] = _pad_head(qt_ref[hs, :], h // grp)
            dop_ref[h] = _pad_head(dot_ref[hs, :], h // grp)
        dq_ref[...] = jnp.zeros_like(dq_ref)

        def step(ci, carry):
            ks = pl.ds(pl.multiple_of(ci * kc, kc), kc)
            kblk, vblk, ktb = kr_ref[ks, :], vb_ref[ks, :], krt_ref[:, ks]
            dv_acc = jnp.zeros((kc, A_KV_WIDTH), F32)
            dk_acc = jnp.zeros((kc, A_KV_WIDTH), F32)
            scs = [_dot(kblk, qp_ref[h]) for h in range(A_HEADS)]
            dps = [_dot(vblk, dop_ref[h]) for h in range(A_HEADS)]
            for h in range(A_HEADS):
                qpad, dopad = qp_ref[h], dop_ref[h]
                p = jnp.exp2(scs[h] - lse_ref[h:h + 1, :])
                dsb = (p * (dps[h] - dl_ref[h:h + 1, :])).astype(BF16)
                dv_acc = dv_acc + _dot_nt(p.astype(BF16), dopad)
                dk_acc = dk_acc + _dot_nt(dsb, qpad)
                dq_ref[h] += _dot(ktb, dsb)
            dv_ref[ks, :] += dv_acc
            dk_ref[ks, :] += dk_acc
            return carry

        lax.fori_loop(0, nkc, step, 0)
        dqt_ref[...] = jnp.concatenate(
            [dq_ref[h][A_HEAD_DIM * (h // grp):A_HEAD_DIM * (h // grp + 1), :] for h in range(A_HEADS)], axis=0)
        if ns:
            pl.when(pl.program_id(0) == nq - 1)(finish)

    colq = pl.BlockSpec((A_WIDTH, tq), lambda i: (0, i))
    colh = pl.BlockSpec((A_HEADS, tq), lambda i: (0, i))
    out = pl.pallas_call(
        body,
        out_shape=(_sds((A_WIDTH, s), F32), _sds((s, A_KV_WIDTH), F32), _sds((s, A_KV_WIDTH), F32)) + scatter_out_shapes(scatter[:na]),
        grid=(nq,),
        in_specs=[colq, colq, _full((s, A_KV_WIDTH)), _full((A_KV_WIDTH, s)), _full((s, A_KV_WIDTH)), colh, colh] + [_ANY] * ns,
        out_specs=(colq, _full((s, A_KV_WIDTH)), _full((s, A_KV_WIDTH))) + (_ANY,) * ns,
        scratch_shapes=[pltpu.VMEM((A_HEADS, A_KV_WIDTH, tq), BF16), pltpu.VMEM((A_HEADS, A_KV_WIDTH, tq), BF16),
                        pltpu.VMEM((A_HEADS, A_KV_WIDTH, tq), F32)] + (scatter_sems(na) if ns else []),
        compiler_params=_cp("arbitrary"), name="attn_bwd_scatter" if ns else "attn_bwd")(
            q_t, do_t, kr, kr_t, vb, lse, delta, *scatter)
    return out[0], out[1], out[2], list(out[3:3 + na]), list(out[3 + na:])


def qk_prep_bwd(proj, dq_t, dkr, dvb, tabs, qg, kg, gq, gk, fold_q, fold_k):
    s = proj.shape[0]
    tm = min(s, 512)
    c, sa, sb = tabs

    def head_norm_bwd(x, dn, gain, gones, fold):
        ms = _group_sum(x * x, gones) * (1.0 / A_HEAD_DIM)
        r = lax.rsqrt(ms + EPS)
        xh = x * r
        gg = _dot_hi(jnp.sum(dn * xh, axis=0, keepdims=True), fold)
        u = dn * gain
        mean_u = _group_sum(u * xh, gones) * (1.0 / A_HEAD_DIM)
        return r * (u - xh * mean_u), gg

    def body(p_ref, dqt_ref, dk_ref, dv_ref, c_ref, sa_ref, sb_ref, qg_ref, kg_ref, gq_ref, gk_ref, fq_ref, fk_ref,
             dqkv_ref, gqg_ref, gkg_ref):
        @pl.when(pl.program_id(0) == 0)
        def _():
            gqg_ref[...] = jnp.zeros_like(gqg_ref)
            gkg_ref[...] = jnp.zeros_like(gkg_ref)

        cc, ssa, ssb = c_ref[...], sa_ref[...], sb_ref[...]
        dqr = dqt_ref[...].T * Q_SCALE
        dqn = _rope_t(dqr, _tile4(cc), _tile4(ssa), _tile4(ssb))
        dxq, gq_ = head_norm_bwd(p_ref[:, O_QA:O_QA + A_WIDTH].astype(F32), dqn, qg_ref[...], gq_ref[...], fq_ref[...])
        dkn = _rope_t(dk_ref[...], cc, ssa, ssb)
        dxk, gk_ = head_norm_bwd(p_ref[:, O_KA:O_KA + A_KV_WIDTH].astype(F32), dkn, kg_ref[...], gk_ref[...], fk_ref[...])
        gqg_ref[...] += gq_
        gkg_ref[...] += gk_
        dqkv_ref[:, O_QA:O_QA + A_WIDTH] = dxq.astype(BF16)
        dqkv_ref[:, O_KA:O_KA + A_KV_WIDTH] = dxk.astype(BF16)
        dqkv_ref[:, O_VA:O_VA + A_KV_WIDTH] = (dv_ref[...] * (1.0 / LN2)).astype(BF16)

    tab = _rows(tm, LANES)
    return pl.pallas_call(
        body, out_shape=(_sds((s, PBLK), BF16), _sds((1, LANES), F32), _sds((1, LANES), F32)), grid=(s // tm,),
        in_specs=[_rows(tm, PBLK), pl.BlockSpec((A_WIDTH, tm), lambda i: (0, i)), _rows(tm, A_KV_WIDTH), _rows(tm, A_KV_WIDTH),
                  tab, tab, tab, _full((1, A_WIDTH)), _full((1, A_KV_WIDTH)), _full((A_WIDTH, A_WIDTH)),
                  _full((A_KV_WIDTH, A_KV_WIDTH)), _full((A_WIDTH, LANES)), _full((A_KV_WIDTH, LANES))],
        out_specs=(_rows(tm, PBLK), _full((1, LANES)), _full((1, LANES))),
        compiler_params=_cp("arbitrary"), name="qk_prep_bwd")(proj, dq_t, dkr, dvb, c, sa, sb, qg, kg, gq, gk, fold_q, fold_k)


def _pick_dproj(b, d0, d1, d2, use):
    first_lg = 1 + MID_W // PBLK

    @pl.when(b == 0)
    def _():
        use(d0[...])

    @pl.when(jnp.logical_and(b >= 1, b < first_lg))
    def _():
        use(d1[...])

    @pl.when(b >= first_lg)
    def _():
        use(d2[...])


def win_grad(d0, d1, d2, h):
    s, d = h.shape
    tk = min(s, 2048)
    nk = s // tk

    def body(d0_ref, d1_ref, d2_ref, h_ref, o_ref, o16_ref):
        @pl.when(pl.program_id(1) == 0)
        def _():
            o_ref[...] = jnp.zeros_like(o_ref)

        def use(blk):
            o_ref[...] += _dot_tn(blk, h_ref[...])

        _pick_dproj(pl.program_id(0), d0_ref, d1_ref, d2_ref, use)

        @pl.when(pl.program_id(1) == nk - 1)
        def _():
            o16_ref[...] = o_ref[...].astype(BF16)

    def spec(first, count):
        def imap(j, k):
            used = jnp.logical_and(j >= first, j < first + count)
            return (jnp.where(used, k, 0), jnp.clip(j - first, 0, count - 1))
        return pl.BlockSpec((tk, PBLK), imap)

    nm = MID_W // PBLK
    oblk = pl.BlockSpec((PBLK, d), lambda j, k: (j, 0))
    return pl.pallas_call(
        body, out_shape=(_sds((IN_WIDTH, d), F32), _sds((IN_WIDTH, d), BF16)), grid=(N_PBLK, nk),
        in_specs=[spec(0, 1), spec(1, nm), spec(1 + nm, LG_W // PBLK), pl.BlockSpec((tk, d), lambda j, k: (k, 0))],
        out_specs=(oblk, oblk),
        compiler_params=_cp("parallel", "arbitrary"), name="win_grad")(d0, d1, d2, h)


def h_bwd(d0, d1, d2, w_t, x, dx_out, g, scatter=()):
    s, d = x.shape
    tm = min(s, 512)
    nt = s // tm
    ns = len(scatter)
    na = ns // 2

    def body(d0_ref, d1_ref, d2_ref, w_ref, x_ref, dxo_ref, g_ref, *rest):
        s_in, (dx_ref, gg_ref), s_out = rest[:ns], rest[ns:ns + 2], rest[ns + 2:2 * ns + 2]
        if ns:
            start, finish = scatter_stages([a.shape[1:] for a in scatter[:na]], s_in[:na], s_in[na:], s_out[:na], s_out[na:],
                                           *rest[2 * ns + 2:])
            pl.when(pl.program_id(0) == 0)(start)

        @pl.when(pl.program_id(0) == 0)
        def _():
            gg_ref[...] = jnp.zeros_like(gg_ref)

        dh = (_dot(d0_ref[...], w_ref[0:PBLK, :]) + _dot(d1_ref[...], w_ref[PBLK:PBLK + MID_W, :])
              + _dot(d2_ref[...], w_ref[PBLK + MID_W:, :]))
        xf = x_ref[...]
        r = lax.rsqrt(jnp.mean(xf * xf, axis=-1, keepdims=True) + EPS)
        xh = xf * r
        gg_ref[...] += jnp.sum(dh * xh, axis=0, keepdims=True)
        u = dh * g_ref[...]
        dx_ref[...] = dxo_ref[...] + r * (u - xh * jnp.mean(u * xh, axis=-1, keepdims=True))
        if ns:
            pl.when(pl.program_id(0) == nt - 1)(finish)

    rowb = _rows(tm, d)
    out = pl.pallas_call(
        body, out_shape=(_sds((s, d), F32), _sds((1, d), F32)) + scatter_out_shapes(scatter[:na]), grid=(nt,),
        in_specs=[_rows(tm, PBLK), _rows(tm, MID_W), _rows(tm, LG_W),
                  pl.BlockSpec(w_t.shape, lambda i: (0, 0), pipeline_mode=pl.Buffered(1)), rowb, rowb, _full((1, d))] + [_ANY] * ns,
        out_specs=(rowb, _full((1, d))) + (_ANY,) * ns,
        scratch_shapes=scatter_sems(na) if ns else [],
        compiler_params=_cp("arbitrary"), name="h_bwd_scatter" if ns else "h_bwd")(d0, d1, d2, w_t, x, dx_out, g, *scatter)
    return out[0], out[1], list(out[2:2 + na]), list(out[2 + na:])


def memkv_bwd(mem, g, mem_n, w_kv, dkv):
    m, d = mem.shape

    def body(mem_ref, g_ref, mn_ref, w_ref, dkv_ref, gw_ref, gw16_ref, gg_ref):
        dkb = dkv_ref[...].astype(BF16)
        gw = _dot_tn(mn_ref[...], dkb)
        gw_ref[...] = gw
        gw16_ref[...] = gw.astype(BF16)
        dmn = _dot_nt(dkb, w_ref[...])
        mf = mem_ref[...]
        r = lax.rsqrt(jnp.mean(mf * mf, axis=-1, keepdims=True) + EPS)
        gg_ref[...] = jnp.sum(dmn * (mf * r), axis=0, keepdims=True)

    return pl.pallas_call(
        body, out_shape=(_sds(w_kv.shape, F32), _sds(w_kv.shape, BF16), _sds((1, d), F32)),
        compiler_params=_cp(), name="memkv_bwd")(mem, g, mem_n, w_kv, dkv)


def _layer_consts(seq):
    i = jnp.arange(A_WIDTH)
    return dict(
        tabs=rope_tables(seq),
        gq=_group_ones(A_WIDTH, A_HEAD_DIM).astype(BF16), gk=_group_ones(A_KV_WIDTH, A_HEAD_DIM).astype(BF16),
        fold_q=(i[:, None] % A_HEAD_DIM == jnp.arange(LANES)[None, :]).astype(F32),
        fold_k=(i[:A_KV_WIDTH, None] % A_HEAD_DIM == jnp.arange(LANES)[None, :]).astype(F32),
        head_sel=(jnp.arange(A_HEADS)[:, None] == i[None, :] // A_HEAD_DIM).astype(F32),
    )


_BIG = ("win_t", "wkv", "wbr", "wout")


def _with_own_part(names, gathered, shards, chip, d):
    shape = dict(win_t=(IN_WIDTH, d), wkv=(d, 2 * M_WIDTH), wbr=(N_CHIPS, N_BRANCH, A_WIDTH, d // N_CHIPS), wout=(d, d))
    return {n: lax.dynamic_update_slice(g, sh[None], (chip, 0, 0)).reshape(shape[n]) for n, g, sh in zip(names, gathered, shards)}


def local_fwd_bwd(x, mem, tgt, small, big=None, shards=None, place=None):
    s, d = x.shape
    depth = small["norm_g"].shape[0]
    k = _layer_consts(s)
    row = lambda v: v.reshape(1, -1)
    dist = shards is not None
    if dist:
        big = [_with_own_part(_BIG[:1], allgather_layer(shards[0][:1]), shards[0][:1], place[0], d)] + [None] * (depth - 1)
    saved = []
    for l in range(depth):
        ng = row(small["norm_g"][l])
        qg = row(jnp.tile(small["q_norm_g"][l], A_HEADS))
        kg = row(jnp.tile(small["k_norm_g"][l], A_KV_HEADS))
        ws = small["w_s"][l].astype(BF16)
        ws_t = jnp.swapaxes(small["w_s"][l], 1, 2).astype(BF16)
        bsb = jnp.broadcast_to(small["b_s"][l][:, :, None], (B_GROUPS, CHUNK, B_GROUP_DIM))
        lng, lnb = row(small["sg_ln_g"][l]), row(small["sg_ln_b"][l])
        mg = row(small["mem_norm_g"][l])
        w = big[l]
        h = rms_fwd(x, ng) if l == 0 else h_next
        proj = proj_fwd(h, w["win_t"])
        q_t, kr, vb, kr_t, vte0, vte1 = qk_prep(proj, k["tabs"], qg, kg, k["gq"], k["gk"])
        late = list(shards[0][1:]) if dist and l == 0 else []
        nxt = list(shards[l + 1]) if dist and l + 1 < depth else []
        o_a, lse, gathered = attn_fwd(q_t, kr, vte0, vte1, gather=tuple(late + nxt))
        if late:
            w.update(_with_own_part(_BIG[1:], gathered[:len(late)], late, place[0], d))
        if nxt:
            big[l + 1] = _with_own_part(_BIG, gathered[len(late):], nxt, place[0], d)
        mem_n, kv = memkv_fwd(mem, mg, w["wkv"])
        next_g = row(small["norm_g"][l + 1]) if l + 1 < depth else row(small["final_g"])
        x_next, y, up, merged, h_next = branch_fwd(x, proj, o_a, kv, ws, bsb, lng, lnb, w["wbr"], w["wout"], next_g)
        saved.append(dict(x=x, ng=ng, qg=qg, kg=kg, ws=ws, ws_t=ws_t, bsb=bsb, lng=lng, lnb=lnb, mg=mg, h=h, proj=proj,
                          q_t=q_t, kr=kr, kr_t=kr_t, vb=vb, o_a=o_a, lse=lse, mem_n=mem_n, kv=kv, y=y, up=up, merged=merged))
        x = x_next

    sq, dx, g_final = final_loss(x, row(small["final_g"]), tgt)
    grads = {n: [None] * depth for n in ("norm_g", "q_norm_g", "k_norm_g", "sg_ln_g", "sg_ln_b", "w_s", "b_s", "mem_norm_g")}
    parts = lambda g: g.reshape(N_CHIPS, -1, g.shape[-1])
    reduced = [[None] * len(_BIG) for _ in range(depth)]

    def reduce_all(items, t_sib, t_rem):
        if items:
            for (ll, a, _, _), f in zip(items, reduce_rows(place, [i[2] for i in items], t_sib, t_rem)):
                reduced[ll][a] = f

    as_scatter = lambda items: tuple(i[2] for i in items) + tuple(i[3] for i in items)
    pending = []
    for l in reversed(range(depth)):
        sv, w = saved[l], big[l]
        dy, dlg, g_wout, g_wbr, g_wout16, g_wbr16 = merge_bwd(dx, sv["proj"], sv["y"], sv["up"], sv["merged"], w["wbr"], w["wout"])
        dmid, do_t, delta, g_ws, g_bs, g_lng, g_lnb, dkv = branch_bwd(
            dy, sv["proj"], sv["o_a"], sv["kv"], sv["ws"], sv["ws_t"], sv["bsb"], sv["lng"], sv["lnb"], k["head_sel"])
        g_wkv, g_wkv16, g_mg = memkv_bwd(mem, sv["mg"], sv["mem_n"], w["wkv"], dkv)
        if dist:
            pending += [(l, 1, parts(g_wkv), parts(g_wkv16)), (l, 2, parts(g_wbr), parts(g_wbr16)), (l, 3, parts(g_wout), parts(g_wout16))]
        dq_t, dkr, dvb, t_sib, t_rem = attn_bwd(sv["q_t"], do_t, sv["kr"], sv["kr_t"], sv["vb"], sv["lse"], delta,
                                                scatter=as_scatter(pending))
        reduce_all(pending, t_sib, t_rem)
        dqkv, g_qg, g_kg = qk_prep_bwd(sv["proj"], dq_t, dkr, dvb, k["tabs"], sv["qg"], sv["kg"], k["gq"], k["gk"],
                                       k["fold_q"], k["fold_k"])
        g_win, g_win16 = win_grad(dqkv, dmid, dlg, sv["h"])
        pending = [(l, 0, parts(g_win), parts(g_win16))] if dist else []
        last = as_scatter(pending) if l == 0 else ()
        dx, g_ng, t_sib, t_rem = h_bwd(dqkv, dmid, dlg, w["win_t"], sv["x"], dx, sv["ng"], scatter=last)
        if last:
            reduce_all(pending, t_sib, t_rem)
        grads["norm_g"][l] = g_ng[0]
        grads["q_norm_g"][l] = g_qg[0, :A_HEAD_DIM]
        grads["k_norm_g"][l] = g_kg[0, :A_HEAD_DIM]
        grads["sg_ln_g"][l] = g_lng[0]
        grads["sg_ln_b"][l] = g_lnb[0]
        grads["w_s"][l] = g_ws
        grads["b_s"][l] = g_bs[:, :, 0]
        grads["mem_norm_g"][l] = g_mg[0]
        if not dist:
            reduced[l] = dict(zip(_BIG, (parts(g_win), parts(g_wkv), parts(g_wbr), parts(g_wout))))
    grads = {n: jnp.stack(v) for n, v in grads.items()}
    grads["final_g"] = g_final[0]
    return sq[0, 0], dx, grads, reduced


def _row_block(rows, width, cap_bytes=2 * 2**20):
    best = None
    for br in range(8, rows + 1, 8):
        if rows % br == 0 and br * width * 4 <= cap_bytes:
            best = br
    return best if best is not None else rows


def adamw(w, gs, m, v):
    r, c = w.shape
    n = len(gs)
    rs = r // n
    br = _row_block(rs, c)
    nb = rs // br

    def body(w_ref, *refs):
        g_refs, (m_ref, v_ref, og_ref, d_ref, nm_ref, nv_ref) = refs[:n], refs[n:]

        def update(gg):
            mm = ADAM_B1 * m_ref[...] + (1.0 - ADAM_B1) * gg
            vv = ADAM_B2 * v_ref[...] + (1.0 - ADAM_B2) * (gg * gg)
            m_hat = mm / (1.0 - ADAM_B1 ** ADAM_STEP)
            v_hat = vv / (1.0 - ADAM_B2 ** ADAM_STEP)
            og_ref[...] = gg
            d_ref[...] = -ADAM_LR * (m_hat / (jnp.sqrt(v_hat) + ADAM_EPS) + ADAM_WD * w_ref[...])
            nm_ref[...] = mm
            nv_ref[...] = vv

        for k in range(n):
            pl.when(pl.program_id(0) == k)(functools.partial(lambda k: update(g_refs[k][...]), k))

    blk = pl.BlockSpec((br, c), lambda l, i: (l * nb + i, 0))
    g_specs = [pl.BlockSpec((br, c), functools.partial(lambda l, i, k: (jnp.where(l == k, i, 0), 0), k=k)) for k in range(n)]
    return pl.pallas_call(
        body, out_shape=(_sds((r, c), F32),) * 4, grid=(n, nb), in_specs=[blk] + g_specs + [blk, blk], out_specs=(blk,) * 4,
        compiler_params=_cp("arbitrary", "arbitrary"), name="adamw")(w, *gs, m, v)


N_REMOTE = 2 * (N_CHIPS - 1)


def reduce_rows(place, gs, t_sibs, t_rems):
    n = len(gs)
    nt = 2

    def body(place_ref, *refs):
        for a in range(n):
            g_ref, s_ref, t_ref, f_ref = refs[a], refs[n + a], refs[2 * n + a], refs[3 * n + a]
            acc = g_ref[...] + s_ref[...]
            for j in range(N_REMOTE):
                acc = acc + t_ref[j].astype(F32)
            f_ref[...] = acc

    tiles = [(g.shape[1] // 2 // nt, g.shape[2]) for g in gs]
    return pl.pallas_call(
        body, out_shape=tuple(_sds(g.shape[1:], F32) for g in gs),
        grid_spec=pltpu.PrefetchScalarGridSpec(
            num_scalar_prefetch=1, grid=(nt,),
            in_specs=[pl.BlockSpec((None, tr, c), lambda i, p: (p[0], p[1] * nt + i, 0)) for tr, c in tiles]
            + [pl.BlockSpec((tr, c), lambda i, p: (i, 0)) for tr, c in tiles]
            + [pl.BlockSpec((N_REMOTE, tr, c), lambda i, p: (0, i, 0)) for tr, c in tiles],
            out_specs=tuple(pl.BlockSpec((tr, c), lambda i, p: (p[1] * nt + i, 0)) for tr, c in tiles)),
        compiler_params=_cp("parallel"), name="reduce_rows")(place, *gs, *t_sibs, *t_rems)


_ANY = pl.BlockSpec(memory_space=pl.ANY)


def _place():
    x, y, c = lax.axis_index("x"), lax.axis_index("y"), lax.axis_index("c")
    chips = [(1 - x, y), (x, 1 - y), (1 - x, 1 - y)]
    return x, y, c, chips


def gather_sems(n):
    return [pltpu.SemaphoreType.DMA((n, N_REMOTE)), pltpu.SemaphoreType.DMA((n, N_REMOTE))]


def gather_stages(shapes, ins, outs, send, recv):
    n = len(shapes)
    x, y, c, chips = _place()
    me = 2 * x + y
    sib = (x, y, 1 - c)

    def rows(a, hl):
        r2 = shapes[a][0] // 2
        return pl.ds(hl * r2, r2)

    def remote(a, k, src, dst, dev):
        return pltpu.make_async_remote_copy(src, dst, send.at[a, k], recv.at[a, k], device_id=dev, device_id_type=MESH)

    def sent(a, k):
        cx, cy = chips[k]
        return remote(a, k, ins[a].at[rows(a, c)], outs[a].at[me, rows(a, c)], (cx, cy, c))

    def passed(a, k, hl):
        cx, cy = chips[k]
        got = outs[a].at[2 * cx + cy, rows(a, hl)]
        return remote(a, k, got, got, (cx, cy, c)), remote(a, 3 + k, got, got, sib)

    def start():
        for a in range(n):
            for k in range(3):
                sent(a, k).start()

    def forward():
        for k in range(3):
            for a in range(n):
                arrived, on = passed(a, k, c)
                arrived.wait_recv()
                on.start()

    def finish():
        for k in range(3):
            for a in range(n):
                passed(a, k, 1 - c)[1].wait_recv()
        for k in range(3):
            for a in range(n):
                sent(a, k).wait_send()
                passed(a, k, c)[1].wait_send()

    return start, forward, finish


def allgather_layer(shards):
    n = len(shards)

    def body(*refs):
        for stage in gather_stages([a.shape for a in shards], refs[:n], refs[n:2 * n], *refs[2 * n:]):
            stage()

    return pl.pallas_call(
        body, out_shape=tuple(_sds((N_CHIPS,) + a.shape, a.dtype) for a in shards),
        in_specs=[_ANY] * n, out_specs=(_ANY,) * n, scratch_shapes=gather_sems(n), name="allgather_layer")(*shards)


def scatter_sems(n):
    return [pltpu.SemaphoreType.DMA((n, N_REMOTE + 1)), pltpu.SemaphoreType.DMA((n, N_REMOTE + 1))]


def scatter_out_shapes(gs):
    return (tuple(_sds((g.shape[1] // 2, g.shape[2]), F32) for g in gs)
            + tuple(_sds((N_REMOTE, g.shape[1] // 2, g.shape[2]), BF16) for g in gs))


def scatter_stages(shapes, gf, gb, t_sib, t_rem, send, recv):
    n = len(shapes)
    x, y, c, chips = _place()
    me = 2 * x + y

    def copies():
        out = []
        for a in range(n):
            r2 = shapes[a][0] // 2
            out.append(pltpu.make_async_remote_copy(gf[a].at[me, pl.ds((1 - c) * r2, r2)], t_sib[a], send.at[a, N_REMOTE],
                                                    recv.at[a, N_REMOTE], device_id=(x, y, 1 - c), device_id_type=MESH))
            for k, (cx, cy) in enumerate(chips):
                for o in range(2):
                    tc = c if o == 0 else 1 - c
                    out.append(pltpu.make_async_remote_copy(gb[a].at[2 * cx + cy, pl.ds(tc * r2, r2)], t_rem[a].at[2 * k + o],
                                                            send.at[a, 2 * k + o], recv.at[a, 2 * k + o],
                                                            device_id=(cx, cy, tc), device_id_type=MESH))
        return out

    def start():
        for cp in copies():
            cp.start()

    def finish():
        for cp in copies():
            cp.wait()

    return start, finish


def finish_exchange(v, fs):
    n = len(fs)
    r, w = v.shape
    ndev = 2 * N_CHIPS

    def body(v_ref, *refs):
        out, sum_ref = refs[n:2 * n], refs[2 * n]
        all_ref, send, recv, loc, fsend, frecv = refs[2 * n + 1:]
        x, y, c, chips = _place()
        me, sib = (x, y, c), (x, y, 1 - c)
        swaps = []
        for a in range(n):
            r2 = fs[a].shape[0] // 2
            mine = out[a].at[pl.ds(c * r2, r2)]
            cp = pltpu.make_async_remote_copy(mine, mine, fsend.at[a], frecv.at[a], device_id=sib, device_id_type=MESH)
            cp.start()
            swaps.append(cp)

        def slab(px, py, pc):
            return all_ref.at[4 * px + 2 * py + pc]

        def copy(k, block, to, src=None):
            return pltpu.make_async_remote_copy(slab(*block) if src is None else src, slab(*block), send.at[k], recv.at[k],
                                                device_id=to, device_id_type=MESH)

        mine = pltpu.make_async_copy(v_ref, slab(*me), loc)
        mine.start()
        first = [copy(0, me, sib, src=v_ref)] + [copy(1 + j, me, (*chip, c), src=v_ref) for j, chip in enumerate(chips)]
        for cp in first:
            cp.start()
        passed = [copy(4 + j, (*chip, c), sib) for j, chip in enumerate(chips)]
        for j, chip in enumerate(chips):
            copy(1 + j, (*chip, c), me).wait_recv()
            passed[j].start()
        copy(0, sib, me).wait_recv()
        for j, chip in enumerate(chips):
            copy(4 + j, (*chip, 1 - c), me).wait_recv()
        for cp in first + passed:
            cp.wait_send()
        mine.wait()
        acc = all_ref[0]
        for i in range(1, ndev):
            acc = acc + all_ref[i]
        sum_ref[...] = acc
        for a, cp in enumerate(swaps):
            r2 = fs[a].shape[0] // 2
            theirs = out[a].at[pl.ds((1 - c) * r2, r2)]
            cp.wait_send()
            pltpu.make_async_remote_copy(theirs, theirs, fsend.at[a], frecv.at[a], device_id=sib, device_id_type=MESH).wait_recv()

    vm = pl.BlockSpec(memory_space=pltpu.VMEM)
    res = pl.pallas_call(
        body, out_shape=tuple(_sds(f.shape, F32) for f in fs) + (_sds((r, w), F32),),
        in_specs=[vm] + [_ANY] * n, out_specs=(_ANY,) * n + (vm,), input_output_aliases={a + 1: a for a in range(n)},
        scratch_shapes=[pltpu.VMEM((ndev, r, w), F32), pltpu.SemaphoreType.DMA((7,)), pltpu.SemaphoreType.DMA((7,)),
                        pltpu.SemaphoreType.DMA, pltpu.SemaphoreType.DMA((n,)), pltpu.SemaphoreType.DMA((n,))],
        compiler_params=pltpu.CompilerParams(vmem_limit_bytes=VMEM_LIMIT), name="finish_exchange")(v, *fs)
    return res[n], list(res[:n])


_SMALL = ("norm_g", "q_norm_g", "k_norm_g", "sg_ln_g", "sg_ln_b", "w_s", "b_s", "mem_norm_g", "final_g")
_WEIGHTS = ("norm_g", "w_in", "q_norm_g", "k_norm_g", "sg_ln_g", "sg_ln_b", "w_s", "b_s", "mem_norm_g", "w_mem_kv", "w_br",
            "w_out", "final_g")


def _pack(d):
    flat = jnp.concatenate([d[n].reshape(-1) for n in _SMALL])
    rows = -(-flat.shape[0] // (8 * LANES)) * 8
    return jnp.pad(flat, (0, rows * LANES - flat.shape[0])).reshape(rows, LANES)


def _unpack(p, like):
    flat, out, o = p.reshape(-1), {}, 0
    for n in _SMALL:
        out[n] = flat[o:o + like[n].size].reshape(like[n].shape)
        o += like[n].size
    return out


def kernel(x, mem, norm_g, w_in, q_norm_g, k_norm_g, sg_ln_g, sg_ln_b, w_s, b_s, mem_norm_g, w_mem_kv, w_br, w_out, final_g, loss_target, m_norm_g, m_w_in, m_q_norm_g, m_k_norm_g, m_sg_ln_g, m_sg_ln_b, m_w_s, m_b_s, m_mem_norm_g, m_w_mem_kv, m_w_br, m_w_out, m_final_g, v_norm_g, v_w_in, v_q_norm_g, v_k_norm_g, v_sg_ln_g, v_sg_ln_b, v_w_s, v_b_s, v_mem_norm_g, v_w_mem_kv, v_w_br, v_w_out, v_final_g):
    w = dict(norm_g=norm_g, w_in=w_in, q_norm_g=q_norm_g, k_norm_g=k_norm_g, sg_ln_g=sg_ln_g, sg_ln_b=sg_ln_b, w_s=w_s, b_s=b_s,
             mem_norm_g=mem_norm_g, w_mem_kv=w_mem_kv, w_br=w_br, w_out=w_out, final_g=final_g)
    m = dict(norm_g=m_norm_g, w_in=m_w_in, q_norm_g=m_q_norm_g, k_norm_g=m_k_norm_g, sg_ln_g=m_sg_ln_g, sg_ln_b=m_sg_ln_b,
             w_s=m_w_s, b_s=m_b_s, mem_norm_g=m_mem_norm_g, w_mem_kv=m_w_mem_kv, w_br=m_w_br, w_out=m_w_out, final_g=m_final_g)
    v = dict(norm_g=v_norm_g, w_in=v_w_in, q_norm_g=v_q_norm_g, k_norm_g=v_k_norm_g, sg_ln_g=v_sg_ln_g, sg_ln_b=v_sg_ln_b,
             w_s=v_w_s, b_s=v_b_s, mem_norm_g=v_mem_norm_g, w_mem_kv=v_w_mem_kv, w_br=v_w_br, w_out=v_w_out, final_g=v_final_g)
    depth, d = norm_g.shape
    nsh = N_CHIPS
    br_rows = N_BRANCH * A_WIDTH
    br_cols = d // nsh

    shards = [[jnp.swapaxes(w_in[l], 0, 1).astype(BF16), w_mem_kv[l].astype(BF16), w_br[l].astype(BF16).reshape(br_rows, br_cols),
               w_out[l].astype(BF16)] for l in range(depth)]
    place = jnp.stack([2 * lax.axis_index("x") + lax.axis_index("y"), lax.axis_index("c")]).astype(jnp.int32)
    small = {n: w[n] for n in _SMALL}

    sq, dx, grads, reduced = local_fwd_bwd(x[0], mem[0], loss_target[0], small, shards=shards, place=place)
    loss = (0.5 / d) * lax.psum(sq, ("x", "y", "c"))

    small_sum, finals = finish_exchange(_pack(grads), [g for layer in reduced for g in layer])
    big_grads = dict(zip(("w_in", "w_mem_kv", "w_br", "w_out"), [finals[a::len(_BIG)] for a in range(len(_BIG))]))
    small_grads = _unpack(small_sum, small)

    out_g, out_d, out_m, out_v = {}, {}, {}, {}
    _, sd, sm, sv = adamw(_pack(small), [small_sum], _pack({n: m[n] for n in _SMALL}), _pack({n: v[n] for n in _SMALL}))
    sd, sm, sv = _unpack(sd, small), _unpack(sm, small), _unpack(sv, small)
    for n in _SMALL:
        out_g[n], out_d[n], out_m[n], out_v[n] = small_grads[n], sd[n], sm[n], sv[n]
    for n, gs in big_grads.items():
        into = (lambda a: jnp.swapaxes(a, 1, 2)) if n == "w_in" else (lambda a: a)
        two_d = lambda a: a.reshape(-1, gs[0].shape[-1])
        res = adamw(two_d(into(w[n])), gs, two_d(into(m[n])), two_d(into(v[n])))
        out_g[n], out_d[n], out_m[n], out_v[n] = [into(t.reshape(into(w[n]).shape)) for t in res]
    return (loss, dx[None], *[out_g[n] for n in _WEIGHTS], *[out_d[n] for n in _WEIGHTS], *[out_m[n] for n in _WEIGHTS],
            *[out_v[n] for n in _WEIGHTS])
```

```python
import functools

import jax
import jax.numpy as jnp
from jax import lax
from jax.experimental import pallas as pl
from jax.experimental.pallas import tpu as pltpu

F32 = jnp.float32
BF16 = jnp.bfloat16

D_MODEL = 1024
GRID_W = 64
CHUNK = 128
ROPE_THETA = 10000.0
EPS = 1e-6
A_HEADS, A_KV_HEADS, A_HEAD_DIM = 8, 2, 64
A_WIDTH, A_KV_WIDTH = 512, 128
B_GROUPS, B_GROUP_DIM, B_WIDTH = 4, 128, 512
M_HEADS, M_HEAD_DIM, M_WIDTH = 4, 128, 512
N_BRANCH = 3
IN_WIDTH = 6912
O_QA, O_KA, O_VA, O_ZA, O_UB, O_VB, O_ZB, O_QM, O_ZM, O_LG = 0, 512, 640, 768, 1280, 1792, 2304, 2816, 3328, 3840
PBLK = 768
N_PBLK = IN_WIDTH // PBLK
MID_W = 3072
LG_W = 3072

LN2 = 0.6931471805599453
Q_SCALE = A_HEAD_DIM ** -0.5 / LN2
VTE_ROWS = A_HEAD_DIM + 16

ADAM_LR, ADAM_B1, ADAM_B2, ADAM_EPS, ADAM_WD, ADAM_STEP = 0.001, 0.9, 0.999, 1e-08, 0.01, 10

V7X_VMEM_BYTES = 64 * 2**20
VMEM_LIMIT = V7X_VMEM_BYTES - 8 * 2**20
LANES = 128
MESH = pl.DeviceIdType.MESH
N_CHIPS = 4


def _cp(*sem):
    return pltpu.CompilerParams(dimension_semantics=sem if sem else None, vmem_limit_bytes=VMEM_LIMIT)


def _dot(a, b):
    return jnp.dot(a, b, preferred_element_type=F32)


def _dot_nt(a, b):
    return lax.dot_general(a, b, (((1,), (1,)), ((), ())), preferred_element_type=F32)


def _dot_tn(a, b):
    return lax.dot_general(a, b, (((0,), (0,)), ((), ())), preferred_element_type=F32)


def _dot_hi(a, b):
    return jnp.dot(a, b, preferred_element_type=F32, precision=lax.Precision.HIGHEST)


def _group_sum(a, ones):
    hi = a.astype(BF16)
    lo = (a - hi.astype(F32)).astype(BF16)
    return _dot(hi, ones) + _dot(lo, ones)


def _dot_nt_hi(a, b):
    return lax.dot_general(a, b, (((1,), (1,)), ((), ())), preferred_element_type=F32, precision=lax.Precision.HIGHEST)


def _sig(z):
    return 1.0 / (1.0 + jnp.exp(-z))


def _full(shape, once=False):
    nd = len(shape)
    return pl.BlockSpec(shape, lambda *_: (0,) * nd, pipeline_mode=pl.Buffered(1) if once else None)


def _rows(tm, width):
    return pl.BlockSpec((tm, width), lambda i: (i, 0))


def _sds(shape, dtype):
    return jax.ShapeDtypeStruct(shape, dtype)


def rms_fwd(x, g):
    s, d = x.shape
    tm = min(s, 512)

    def body(x_ref, g_ref, h_ref):
        xf = x_ref[...]
        r = lax.rsqrt(jnp.mean(xf * xf, axis=-1, keepdims=True) + EPS)
        h_ref[...] = ((xf * r) * g_ref[...]).astype(BF16)

    return pl.pallas_call(
        body, out_shape=_sds((s, d), BF16), grid=(s // tm,),
        in_specs=[_rows(tm, d), _full((1, d))], out_specs=_rows(tm, d),
        compiler_params=_cp("parallel"), name="rms_fwd")(x, g)


def proj_fwd(h, w_t):
    s, d = h.shape
    n = w_t.shape[0]
    tm = min(s, 512)
    tn = 2304

    def body(h_ref, w_ref, o_ref):
        o_ref[...] = _dot_nt(h_ref[...], w_ref[...]).astype(BF16)

    return pl.pallas_call(
        body, out_shape=_sds((s, n), BF16), grid=(n // tn, s // tm),
        in_specs=[pl.BlockSpec((tm, d), lambda j, i: (i, 0)), pl.BlockSpec((tn, d), lambda j, i: (j, 0))],
        out_specs=pl.BlockSpec((tm, tn), lambda j, i: (i, j)),
        compiler_params=_cp("parallel", "parallel"), name="proj_fwd")(h, w_t)


def rope_tables(seq):
    n_freq = A_HEAD_DIM // 4
    d = jnp.arange(LANES) % A_HEAD_DIM
    seg, half, freq = d // (2 * n_freq), (d % (2 * n_freq)) // n_freq, d % n_freq
    inv = ROPE_THETA ** (-freq.astype(F32) / n_freq)
    t = jnp.arange(seq)
    pos = jnp.where(seg[None, :] == 0, (t // GRID_W)[:, None], (t % GRID_W)[:, None]).astype(F32)
    ang = pos * inv[None, :]
    cos, sin = jnp.cos(ang), jnp.sin(ang)
    return cos, jnp.where(half[None, :] == 1, sin, 0.0), jnp.where(half[None, :] == 0, -sin, 0.0)


def _group_ones(width, group):
    i = jnp.arange(width)
    return (i[:, None] // group == i[None, :] // group).astype(F32)


def _rope(xn, c, sa, sb):
    w = xn.shape[1]
    return xn * c + pltpu.roll(xn, 16, 1) * sa + pltpu.roll(xn, w - 16, 1) * sb


def _rope_t(dy, c, sa, sb):
    w = dy.shape[1]
    return dy * c + pltpu.roll(dy * sa, w - 16, 1) + pltpu.roll(dy * sb, 16, 1)


def _tile4(t):
    return jnp.concatenate([t, t, t, t], axis=1)


def qk_prep(proj, tabs, qg, kg, gq, gk):
    s = proj.shape[0]
    tm = min(s, 512)
    c, sa, sb = tabs

    def body(p_ref, c_ref, sa_ref, sb_ref, qg_ref, kg_ref, gq_ref, gk_ref, qt_ref, kr_ref, vb_ref, kt_ref, v0_ref, v1_ref):
        xq = p_ref[:, O_QA:O_QA + A_WIDTH].astype(F32)
        xk = p_ref[:, O_KA:O_KA + A_KV_WIDTH].astype(F32)
        xv = p_ref[:, O_VA:O_VA + A_KV_WIDTH].astype(F32)
        cc, ssa, ssb = c_ref[...], sa_ref[...], sb_ref[...]
        msq = _group_sum(xq * xq, gq_ref[...]) * (1.0 / A_HEAD_DIM)
        qn = (xq * lax.rsqrt(msq + EPS)) * qg_ref[...]
        qr = _rope(qn, _tile4(cc), _tile4(ssa), _tile4(ssb)) * Q_SCALE
        qt_ref[...] = qr.T.astype(BF16)
        msk = _group_sum(xk * xk, gk_ref[...]) * (1.0 / A_HEAD_DIM)
        kn = (xk * lax.rsqrt(msk + EPS)) * kg_ref[...]
        kr = _rope(kn, cc, ssa, ssb)
        kr_ref[...] = kr.astype(BF16)
        vb_ref[...] = xv.astype(BF16)
        kt_ref[...] = kr.T.astype(BF16)
        vt = xv.T.astype(BF16)
        one = jnp.ones((VTE_ROWS - A_HEAD_DIM, tm), BF16)
        v0_ref[...] = jnp.concatenate([vt[:A_HEAD_DIM], one], axis=0)
        v1_ref[...] = jnp.concatenate([vt[A_HEAD_DIM:], one], axis=0)

    tab = _rows(tm, LANES)
    colb = lambda w: pl.BlockSpec((w, tm), lambda i: (0, i))
    return pl.pallas_call(
        body,
        out_shape=(_sds((A_WIDTH, s), BF16), _sds((s, A_KV_WIDTH), BF16), _sds((s, A_KV_WIDTH), BF16),
                   _sds((A_KV_WIDTH, s), BF16), _sds((VTE_ROWS, s), BF16), _sds((VTE_ROWS, s), BF16)),
        grid=(s // tm,),
        in_specs=[_rows(tm, PBLK), tab, tab, tab, _full((1, A_WIDTH)), _full((1, A_KV_WIDTH)),
                  _full((A_WIDTH, A_WIDTH)), _full((A_KV_WIDTH, A_KV_WIDTH))],
        out_specs=(colb(A_WIDTH), _rows(tm, A_KV_WIDTH), _rows(tm, A_KV_WIDTH), colb(A_KV_WIDTH), colb(VTE_ROWS), colb(VTE_ROWS)),
        compiler_params=_cp("parallel"), name="qk_prep")(proj, c, sa, sb, qg, kg, gq, gk)


def _pad_head(q_h, kv):
    z = jnp.zeros_like(q_h)
    return jnp.concatenate([q_h, z], axis=0) if kv == 0 else jnp.concatenate([z, q_h], axis=0)


def attn_fwd(q_t, kr, vte0, vte1, gather=()):
    s = kr.shape[0]
    tq = min(s, 512)
    kc = min(s, 256)
    nkc = s // kc
    nq = s // tq
    grp = A_HEADS // A_KV_HEADS
    ng = len(gather)

    def body(qt_ref, kr_ref, v0_ref, v1_ref, *rest):
        g_in, (o_ref, lse_ref), g_out = rest[:ng], rest[ng:ng + 2], rest[ng + 2:2 * ng + 2]
        qp_ref, m_ref, acc_ref = rest[2 * ng + 2:2 * ng + 5]
        if ng:
            start, forward, finish = gather_stages([g.shape for g in gather], g_in, g_out, *rest[2 * ng + 5:])
            pl.when(pl.program_id(0) == 0)(start)
            pl.when(pl.program_id(0) == (3 * nq) // 4)(forward)

        for h in range(A_HEADS):
            qp_ref[h] = _pad_head(qt_ref[A_HEAD_DIM * h:A_HEAD_DIM * (h + 1), :], h // grp)
        m_ref[...] = jnp.full(m_ref.shape, -1e30, F32)
        acc_ref[...] = jnp.zeros_like(acc_ref)

        def step(ci, carry):
            ks = pl.ds(pl.multiple_of(ci * kc, kc), kc)
            kblk = kr_ref[ks, :]
            vts = (v0_ref[:, ks], v1_ref[:, ks])
            scs = [_dot(kblk, qp_ref[h]) for h in range(A_HEADS)]
            for h in range(A_HEADS):
                sc = scs[h]
                m_prev = m_ref[h:h + 1, :]
                m_new = jnp.maximum(m_prev, jnp.max(sc, axis=0, keepdims=True))
                p = jnp.exp2(sc - m_new)
                acc_ref[h] = acc_ref[h] * jnp.exp2(m_prev - m_new) + _dot(vts[h // grp], p.astype(BF16))
                m_ref[h:h + 1, :] = m_new
            return carry

        lax.fori_loop(0, nkc, step, 0)
        outs, lses = [], []
        for h in range(A_HEADS):
            acc = acc_ref[h]
            l = acc[A_HEAD_DIM:A_HEAD_DIM + 1, :]
            outs.append(acc[:A_HEAD_DIM, :] / l)
            lses.append(m_ref[h:h + 1, :] + jnp.log2(l))
        o_ref[...] = jnp.concatenate(outs, axis=0).T
        lse_ref[...] = jnp.concatenate(lses, axis=0)
        if ng:
            pl.when(pl.program_id(0) == nq - 1)(finish)

    out = pl.pallas_call(
        body,
        out_shape=(_sds((s, A_WIDTH), F32), _sds((A_HEADS, s), F32)) + tuple(_sds((N_CHIPS,) + g.shape, g.dtype) for g in gather),
        grid=(nq,),
        in_specs=[pl.BlockSpec((A_WIDTH, tq), lambda i: (0, i)), _full((s, A_KV_WIDTH)), _full((VTE_ROWS, s)),
                  _full((VTE_ROWS, s))] + [_ANY] * ng,
        out_specs=(_rows(tq, A_WIDTH), pl.BlockSpec((A_HEADS, tq), lambda i: (0, i))) + (_ANY,) * ng,
        scratch_shapes=[pltpu.VMEM((A_HEADS, A_KV_WIDTH, tq), BF16), pltpu.VMEM((A_HEADS, tq), F32),
                        pltpu.VMEM((A_HEADS, VTE_ROWS, tq), F32)] + (gather_sems(ng) if ng else []),
        compiler_params=_cp("arbitrary"), name="attn_fwd_gather" if ng else "attn_fwd")(q_t, kr, vte0, vte1, *gather)
    return out[0], out[1], list(out[2:])


def memkv_fwd(mem, g, w_kv):
    m, d = mem.shape

    def body(mem_ref, g_ref, w_ref, mn_ref, kv_ref):
        mf = mem_ref[...]
        r = lax.rsqrt(jnp.mean(mf * mf, axis=-1, keepdims=True) + EPS)
        mn = ((mf * r) * g_ref[...]).astype(BF16)
        mn_ref[...] = mn
        kv_ref[...] = _dot(mn, w_ref[...]).astype(BF16)

    return pl.pallas_call(
        body, out_shape=(_sds((m, d), BF16), _sds((m, 2 * M_WIDTH), BF16)),
        compiler_params=_cp(), name="memkv_fwd")(mem, g, w_kv)


def _layer_norm_stats(v):
    mu = jnp.mean(v, axis=-1, keepdims=True)
    xc = v - mu
    rstd = lax.rsqrt(jnp.mean(xc * xc, axis=-1, keepdims=True) + EPS)
    return xc * rstd, rstd


def _spatial_mix(vlb, ws_ref, bsb_ref, tm):
    rows = []
    for ci in range(tm // CHUNK):
        cols = []
        for g in range(B_GROUPS):
            blk = vlb[ci * CHUNK:(ci + 1) * CHUNK, g * B_GROUP_DIM:(g + 1) * B_GROUP_DIM]
            cols.append(_dot(ws_ref[g], blk) + bsb_ref[g])
        rows.append(jnp.concatenate(cols, axis=1))
    return jnp.concatenate(rows, axis=0)


def _mem_attn(qm, kv_ref):
    out = []
    for h in range(M_HEADS):
        qh = qm[:, h * M_HEAD_DIM:(h + 1) * M_HEAD_DIM].astype(BF16)
        kh = kv_ref[:, h * M_HEAD_DIM:(h + 1) * M_HEAD_DIM]
        vh = kv_ref[:, M_WIDTH + h * M_HEAD_DIM:M_WIDTH + (h + 1) * M_HEAD_DIM]
        sc = _dot_nt(qh, kh) * (M_HEAD_DIM ** -0.5)
        e = jnp.exp(sc - jnp.max(sc, axis=-1, keepdims=True))
        p = e / jnp.sum(e, axis=-1, keepdims=True)
        out.append((p, _dot(p.astype(BF16), vh)))
    return out


def branch_fwd(x, proj, o_a, kv, ws, bsb, ln_g, ln_b, w_br, w_out, next_g):
    s, d = x.shape
    tm = min(s, 512)

    def body(x_ref, p_ref, oa_ref, kv_ref, ws_ref, bsb_ref, lg_ref, lb_ref, wbr_ref, wo_ref, ng_ref,
             xn_ref, y_ref, up_ref, mg_ref, hn_ref):
        seg = lambda o, w: p_ref[:, o:o + w].astype(F32)
        z_a, u_b, v_b, z_b = seg(O_ZA, A_WIDTH), seg(O_UB, B_WIDTH), seg(O_VB, B_WIDTH), seg(O_ZB, B_WIDTH)
        q_m, z_m = seg(O_QM, M_WIDTH), seg(O_ZM, M_WIDTH)
        xhat, _ = _layer_norm_stats(v_b)
        vln = xhat * lg_ref[...] + lb_ref[...]
        mixed = _spatial_mix(vln.astype(BF16), ws_ref, bsb_ref, tm)
        y_b = (u_b * mixed) * (z_b * _sig(z_b))
        o_m = jnp.concatenate([o for _, o in _mem_attn(q_m, kv_ref)], axis=1)
        y_a = oa_ref[...] * (z_a * _sig(z_a))
        y_m = o_m * (z_m * _sig(z_m))
        merged = None
        for n, yy in enumerate((y_a, y_b, y_m)):
            yb = yy.astype(BF16)
            y_ref[n] = yb
            up = jnp.concatenate([_dot(yb, wbr_ref[c, n]) for c in range(N_CHIPS)], axis=1)
            up_ref[n] = up.astype(BF16)
            t = _sig(seg(O_LG + n * d, d)) * up
            merged = t if merged is None else merged + t
        mb = merged.astype(BF16)
        mg_ref[...] = mb
        xn = x_ref[...] + _dot(mb, wo_ref[...])
        xn_ref[...] = xn
        r = lax.rsqrt(jnp.mean(xn * xn, axis=-1, keepdims=True) + EPS)
        hn_ref[...] = ((xn * r) * ng_ref[...]).astype(BF16)

    return pl.pallas_call(
        body,
        out_shape=(_sds((s, d), F32), _sds((N_BRANCH, s, A_WIDTH), BF16), _sds((N_BRANCH, s, d), BF16), _sds((s, d), BF16),
                   _sds((s, d), BF16)),
        grid=(s // tm,),
        in_specs=[_rows(tm, d), _rows(tm, IN_WIDTH), _rows(tm, A_WIDTH), _full(kv.shape), _full(ws.shape), _full(bsb.shape),
                  _full((1, B_WIDTH)), _full((1, B_WIDTH)), _full(w_br.shape), _full(w_out.shape), _full((1, d))],
        out_specs=(_rows(tm, d), pl.BlockSpec((N_BRANCH, tm, A_WIDTH), lambda i: (0, i, 0)),
                   pl.BlockSpec((N_BRANCH, tm, d), lambda i: (0, i, 0)), _rows(tm, d), _rows(tm, d)),
        compiler_params=_cp("parallel"), name="branch_fwd")(x, proj, o_a, kv, ws, bsb, ln_g, ln_b, w_br, w_out, next_g)


def final_loss(x, fg, tgt):
    s, d = x.shape
    tm = min(s, 512)

    def body(x_ref, g_ref, t_ref, ls_ref, dx_ref, gg_ref):
        @pl.when(pl.program_id(0) == 0)
        def _():
            ls_ref[...] = jnp.zeros_like(ls_ref)
            gg_ref[...] = jnp.zeros_like(gg_ref)

        xf = x_ref[...]
        g = g_ref[...]
        r = lax.rsqrt(jnp.mean(xf * xf, axis=-1, keepdims=True) + EPS)
        xh = xf * r
        e = xh * g - t_ref[...]
        sq = jnp.sum(jnp.sum(e * e, axis=0, keepdims=True), axis=1, keepdims=True)
        ls_ref[...] += jnp.broadcast_to(sq, ls_ref.shape)
        dy = e * (1.0 / d)
        gg_ref[...] += jnp.sum(dy * xh, axis=0, keepdims=True)
        gy = dy * g
        dx_ref[...] = r * (gy - xh * jnp.mean(gy * xh, axis=-1, keepdims=True))

    return pl.pallas_call(
        body, out_shape=(_sds((1, LANES), F32), _sds((s, d), F32), _sds((1, d), F32)), grid=(s // tm,),
        in_specs=[_rows(tm, d), _full((1, d)), _rows(tm, d)],
        out_specs=(_full((1, LANES)), _rows(tm, d), _full((1, d))),
        compiler_params=_cp("arbitrary"), name="final_loss")(x, fg, tgt)


def _pblocks(tm, first, count):
    return [pl.BlockSpec((tm, PBLK), functools.partial(lambda i, b: (i, b), b=first + k)) for k in range(count)]


def merge_bwd(dx, proj, y, up, merged, w_br, w_out):
    s, d = dx.shape
    tm = min(s, 256)
    nlg = LG_W // PBLK
    cw = d // N_CHIPS

    def body(dx_ref, l0, l1, l2, l3, y_ref, up_ref, mg_ref, wbr_ref, wo_ref, dy_ref, dlg_ref, gwo_ref, gwb_ref, gwo16_ref, gwb16_ref):
        @pl.when(pl.program_id(0) == 0)
        def _():
            gwo_ref[...] = jnp.zeros_like(gwo_ref)
            gwb_ref[...] = jnp.zeros_like(gwb_ref)

        dxb = dx_ref[...].astype(BF16)
        dmg = _dot_nt(dxb, wo_ref[...])
        gwo_ref[...] += _dot_tn(mg_ref[...], dxb)
        lg = jnp.concatenate([l0[...], l1[...], l2[...], l3[...]], axis=1).astype(F32)
        for n in range(N_BRANCH):
            g = _sig(lg[:, n * d:(n + 1) * d])
            dup = dmg * g
            dlg_ref[:, n * d:(n + 1) * d] = ((dup * up_ref[n].astype(F32)) * (1.0 - g)).astype(BF16)
            dupb = dup.astype(BF16)
            dyn = None
            for c in range(N_CHIPS):
                blk = dupb[:, c * cw:(c + 1) * cw]
                gwb_ref[c, n] += _dot_tn(y_ref[n], blk)
                t = _dot_nt(blk, wbr_ref[c, n])
                dyn = t if dyn is None else dyn + t
            dy_ref[n] = dyn.astype(BF16)

        @pl.when(pl.program_id(0) == pl.num_programs(0) - 1)
        def _():
            gwo16_ref[...] = gwo_ref[...].astype(BF16)
            gwb16_ref[...] = gwb_ref[...].astype(BF16)

    return pl.pallas_call(
        body,
        out_shape=(_sds((N_BRANCH, s, A_WIDTH), BF16), _sds((s, LG_W), BF16), _sds((d, d), F32), _sds(w_br.shape, F32),
                   _sds((d, d), BF16), _sds(w_br.shape, BF16)),
        grid=(s // tm,),
        in_specs=[_rows(tm, d)] + _pblocks(tm, O_LG // PBLK, nlg) + [
            pl.BlockSpec((N_BRANCH, tm, A_WIDTH), lambda i: (0, i, 0)), pl.BlockSpec((N_BRANCH, tm, d), lambda i: (0, i, 0)),
            _rows(tm, d), _full(w_br.shape, once=True), _full(w_out.shape, once=True)],
        out_specs=(pl.BlockSpec((N_BRANCH, tm, A_WIDTH), lambda i: (0, i, 0)), _rows(tm, LG_W), _full((d, d)), _full(w_br.shape),
                   _full((d, d)), _full(w_br.shape)),
        compiler_params=_cp("arbitrary"), name="merge_bwd")(dx, proj, proj, proj, proj, y, up, merged, w_br, w_out)


def _dsilu(z, sg):
    return sg * (1.0 + z * (1.0 - sg))


def branch_bwd(dy, proj, o_a, kv, ws, ws_t, bsb, ln_g, ln_b, head_sel):
    s = proj.shape[0]
    tm = min(s, 256)
    nmid = MID_W // PBLK

    def body(dy_ref, m0, m1, m2, m3, oa_ref, kv_ref, ws_ref, wst_ref, bsb_ref, lg_ref, lb_ref, sel_ref,
             dmid_ref, dot_ref, dl_ref, gws_ref, gbs_ref, glg_ref, glb_ref, dkv_ref):
        @pl.when(pl.program_id(0) == 0)
        def _():
            for r in (gws_ref, gbs_ref, glg_ref, glb_ref, dkv_ref):
                r[...] = jnp.zeros_like(r)

        mid = jnp.concatenate([m0[...], m1[...], m2[...], m3[...]], axis=1).astype(F32)
        seg = lambda o, w: mid[:, o - O_ZA:o - O_ZA + w]
        z_a, u_b, v_b, z_b = seg(O_ZA, A_WIDTH), seg(O_UB, B_WIDTH), seg(O_VB, B_WIDTH), seg(O_ZB, B_WIDTH)
        q_m, z_m = seg(O_QM, M_WIDTH), seg(O_ZM, M_WIDTH)

        def put(o, v):
            dmid_ref[:, o - O_ZA:o - O_ZA + v.shape[1]] = v.astype(BF16)

        dy_a, dy_b, dy_m = dy_ref[0].astype(F32), dy_ref[1].astype(F32), dy_ref[2].astype(F32)

        o_a_ = oa_ref[...]
        sg = _sig(z_a)
        do_a = dy_a * (z_a * sg)
        put(O_ZA, (dy_a * o_a_) * _dsilu(z_a, sg))
        do_l = do_a * LN2
        dot_ref[...] = do_l.T.astype(BF16)
        dl_ref[...] = _dot_nt_hi(sel_ref[...], do_l * o_a_)

        xhat, rstd = _layer_norm_stats(v_b)
        lng = lg_ref[...]
        vln = xhat * lng + lb_ref[...]
        vlb = vln.astype(BF16)
        mixed = _spatial_mix(vlb, ws_ref, bsb_ref, tm)
        sg = _sig(z_b)
        sl = z_b * sg
        put(O_UB, (dy_b * mixed) * sl)
        put(O_ZB, ((dy_b * u_b) * mixed) * _dsilu(z_b, sg))
        dmix = (dy_b * u_b) * sl
        dmb = dmix.astype(BF16)
        rows = []
        for ci in range(tm // CHUNK):
            cols = []
            for g in range(B_GROUPS):
                rs, cs = slice(ci * CHUNK, (ci + 1) * CHUNK), slice(g * B_GROUP_DIM, (g + 1) * B_GROUP_DIM)
                gws_ref[g] += _dot_nt(dmb[rs, cs], vlb[rs, cs])
                gbs_ref[g] += jnp.broadcast_to(jnp.sum(dmix[rs, cs], axis=1, keepdims=True), (CHUNK, B_GROUP_DIM))
                cols.append(_dot(wst_ref[g], dmb[rs, cs]))
            rows.append(jnp.concatenate(cols, axis=1))
        dvln = jnp.concatenate(rows, axis=0)
        glg_ref[...] += jnp.sum(dvln * xhat, axis=0, keepdims=True)
        glb_ref[...] += jnp.sum(dvln, axis=0, keepdims=True)
        gy = dvln * lng
        put(O_VB, rstd * ((gy - jnp.mean(gy, axis=-1, keepdims=True)) - xhat * jnp.mean(gy * xhat, axis=-1, keepdims=True)))

        sg = _sig(z_m)
        sl = z_m * sg
        heads = _mem_attn(q_m, kv_ref)
        o_m = jnp.concatenate([o for _, o in heads], axis=1)
        put(O_ZM, (dy_m * o_m) * _dsilu(z_m, sg))
        do_m = dy_m * sl
        dqs = []
        for h, (p, o_h) in enumerate(heads):
            hs = slice(h * M_HEAD_DIM, (h + 1) * M_HEAD_DIM)
            vs = slice(M_WIDTH + h * M_HEAD_DIM, M_WIDTH + (h + 1) * M_HEAD_DIM)
            do_h = do_m[:, hs]
            dob = do_h.astype(BF16)
            dp = _dot_nt(dob, kv_ref[:, vs])
            dsc = (p * (dp - jnp.sum(do_h * o_h, axis=-1, keepdims=True))) * (M_HEAD_DIM ** -0.5)
            dsb = dsc.astype(BF16)
            dqs.append(_dot(dsb, kv_ref[:, hs]))
            dkv_ref[:, hs] += _dot_tn(dsb, q_m[:, hs].astype(BF16))
            dkv_ref[:, vs] += _dot_tn(p.astype(BF16), dob)
        put(O_QM, jnp.concatenate(dqs, axis=1))

    return pl.pallas_call(
        body,
        out_shape=(_sds((s, MID_W), BF16), _sds((A_WIDTH, s), BF16), _sds((A_HEADS, s), F32), _sds(ws.shape, F32),
                   _sds(ws.shape, F32), _sds((1, B_WIDTH), F32), _sds((1, B_WIDTH), F32), _sds(kv.shape, F32)),
        grid=(s // tm,),
        in_specs=[pl.BlockSpec((N_BRANCH, tm, A_WIDTH), lambda i: (0, i, 0))] + _pblocks(tm, O_ZA // PBLK, nmid) + [
            _rows(tm, A_WIDTH), _full(kv.shape), _full(ws.shape), _full(ws.shape), _full(bsb.shape),
            _full((1, B_WIDTH)), _full((1, B_WIDTH)), _full(head_sel.shape)],
        out_specs=(_rows(tm, MID_W), pl.BlockSpec((A_WIDTH, tm), lambda i: (0, i)), pl.BlockSpec((A_HEADS, tm), lambda i: (0, i)),
                   _full(ws.shape), _full(ws.shape), _full((1, B_WIDTH)), _full((1, B_WIDTH)), _full(kv.shape)),
        compiler_params=_cp("arbitrary"), name="branch_bwd")(dy, proj, proj, proj, proj, o_a, kv, ws, ws_t, bsb, ln_g, ln_b, head_sel)


def attn_bwd(q_t, do_t, kr, kr_t, vb, lse, delta, scatter=()):
    s = kr.shape[0]
    tq = min(s, 256)
    kc = min(s, 512)
    nkc = s // kc
    nq = s // tq
    grp = A_HEADS // A_KV_HEADS
    ns = len(scatter)
    na = ns // 2

    def body(qt_ref, dot_ref, kr_ref, krt_ref, vb_ref, lse_ref, dl_ref, *rest):
        s_in, (dqt_ref, dk_ref, dv_ref), s_out = rest[:ns], rest[ns:ns + 3], rest[ns + 3:2 * ns + 3]
        qp_ref, dop_ref, dq_ref = rest[2 * ns + 3:2 * ns + 6]
        if ns:
            start, finish = scatter_stages([g.shape[1:] for g in scatter[:na]], s_in[:na], s_in[na:], s_out[:na], s_out[na:],
                                           *rest[2 * ns + 6:])
            pl.when(pl.program_id(0) == 0)(start)

        @pl.when(pl.program_id(0) == 0)
        def _():
            dk_ref[...] = jnp.zeros_like(dk_ref)
            dv_ref[...] = jnp.zeros_like(dv_ref)

        for h in range(A_HEADS):
            hs = slice(A_HEAD_DIM * h, A_HEAD_DIM * (h + 1))
            qp_ref[h] = _pad_head(qt_ref[hs, :], h // grp)
            dop_ref[h] = _pad_head(dot_ref[hs, :], h // grp)
        dq_ref[...] = jnp.zeros_like(dq_ref)

        def step(ci, carry):
            ks = pl.ds(pl.multiple_of(ci * kc, kc), kc)
            kblk, vblk, ktb = kr_ref[ks, :], vb_ref[ks, :], krt_ref[:, ks]
            dv_acc = jnp.zeros((kc, A_KV_WIDTH), F32)
            dk_acc = jnp.zeros((kc, A_KV_WIDTH), F32)
            scs = [_dot(kblk, qp_ref[h]) for h in range(A_HEADS)]
            dps = [_dot(vblk, dop_ref[h]) for h in range(A_HEADS)]
            for h in range(A_HEADS):
                qpad, dopad = qp_ref[h], dop_ref[h]
                p = jnp.exp2(scs[h] - lse_ref[h:h + 1, :])
                dsb = (p * (dps[h] - dl_ref[h:h + 1, :])).astype(BF16)
                dv_acc = dv_acc + _dot_nt(p.astype(BF16), dopad)
                dk_acc = dk_acc + _dot_nt(dsb, qpad)
                dq_ref[h] += _dot(ktb, dsb)
            dv_ref[ks, :] += dv_acc
            dk_ref[ks, :] += dk_acc
            return carry

        lax.fori_loop(0, nkc, step, 0)
        dqt_ref[...] = jnp.concatenate(
            [dq_ref[h][A_HEAD_DIM * (h // grp):A_HEAD_DIM * (h // grp + 1), :] for h in range(A_HEADS)], axis=0)
        if ns:
            pl.when(pl.program_id(0) == nq - 1)(finish)

    colq = pl.BlockSpec((A_WIDTH, tq), lambda i: (0, i))
    colh = pl.BlockSpec((A_HEADS, tq), lambda i: (0, i))
    out = pl.pallas_call(
        body,
        out_shape=(_sds((A_WIDTH, s), F32), _sds((s, A_KV_WIDTH), F32), _sds((s, A_KV_WIDTH), F32)) + scatter_out_shapes(scatter[:na]),
        grid=(nq,),
        in_specs=[colq, colq, _full((s, A_KV_WIDTH)), _full((A_KV_WIDTH, s)), _full((s, A_KV_WIDTH)), colh, colh] + [_ANY] * ns,
        out_specs=(colq, _full((s, A_KV_WIDTH)), _full((s, A_KV_WIDTH))) + (_ANY,) * ns,
        scratch_shapes=[pltpu.VMEM((A_HEADS, A_KV_WIDTH, tq), BF16), pltpu.VMEM((A_HEADS, A_KV_WIDTH, tq), BF16),
                        pltpu.VMEM((A_HEADS, A_KV_WIDTH, tq), F32)] + (scatter_sems(na) if ns else []),
        compiler_params=_cp("arbitrary"), name="attn_bwd_scatter" if ns else "attn_bwd")(
            q_t, do_t, kr, kr_t, vb, lse, delta, *scatter)
    return out[0], out[1], out[2], list(out[3:3 + na]), list(out[3 + na:])


def qk_prep_bwd(proj, dq_t, dkr, dvb, tabs, qg, kg, gq, gk, fold_q, fold_k):
    s = proj.shape[0]
    tm = min(s, 512)
    c, sa, sb = tabs

    def head_norm_bwd(x, dn, gain, gones, fold):
        ms = _group_sum(x * x, gones) * (1.0 / A_HEAD_DIM)
        r = lax.rsqrt(ms + EPS)
        xh = x * r
        gg = _dot_hi(jnp.sum(dn * xh, axis=0, keepdims=True), fold)
        u = dn * gain
        mean_u = _group_sum(u * xh, gones) * (1.0 / A_HEAD_DIM)
        return r * (u - xh * mean_u), gg

    def body(p_ref, dqt_ref, dk_ref, dv_ref, c_ref, sa_ref, sb_ref, qg_ref, kg_ref, gq_ref, gk_ref, fq_ref, fk_ref,
             dqkv_ref, gqg_ref, gkg_ref):
        @pl.when(pl.program_id(0) == 0)
        def _():
            gqg_ref[...] = jnp.zeros_like(gqg_ref)
            gkg_ref[...] = jnp.zeros_like(gkg_ref)

        cc, ssa, ssb = c_ref[...], sa_ref[...], sb_ref[...]
        dqr = dqt_ref[...].T * Q_SCALE
        dqn = _rope_t(dqr, _tile4(cc), _tile4(ssa), _tile4(ssb))
        dxq, gq_ = head_norm_bwd(p_ref[:, O_QA:O_QA + A_WIDTH].astype(F32), dqn, qg_ref[...], gq_ref[...], fq_ref[...])
        dkn = _rope_t(dk_ref[...], cc, ssa, ssb)
        dxk, gk_ = head_norm_bwd(p_ref[:, O_KA:O_KA + A_KV_WIDTH].astype(F32), dkn, kg_ref[...], gk_ref[...], fk_ref[...])
        gqg_ref[...] += gq_
        gkg_ref[...] += gk_
        dqkv_ref[:, O_QA:O_QA + A_WIDTH] = dxq.astype(BF16)
        dqkv_ref[:, O_KA:O_KA + A_KV_WIDTH] = dxk.astype(BF16)
        dqkv_ref[:, O_VA:O_VA + A_KV_WIDTH] = (dv_ref[...] * (1.0 / LN2)).astype(BF16)

    tab = _rows(tm, LANES)
    return pl.pallas_call(
        body, out_shape=(_sds((s, PBLK), BF16), _sds((1, LANES), F32), _sds((1, LANES), F32)), grid=(s // tm,),
        in_specs=[_rows(tm, PBLK), pl.BlockSpec((A_WIDTH, tm), lambda i: (0, i)), _rows(tm, A_KV_WIDTH), _rows(tm, A_KV_WIDTH),
                  tab, tab, tab, _full((1, A_WIDTH)), _full((1, A_KV_WIDTH)), _full((A_WIDTH, A_WIDTH)),
                  _full((A_KV_WIDTH, A_KV_WIDTH)), _full((A_WIDTH, LANES)), _full((A_KV_WIDTH, LANES))],
        out_specs=(_rows(tm, PBLK), _full((1, LANES)), _full((1, LANES))),
        compiler_params=_cp("arbitrary"), name="qk_prep_bwd")(proj, dq_t, dkr, dvb, c, sa, sb, qg, kg, gq, gk, fold_q, fold_k)


def _pick_dproj(b, d0, d1, d2, use):
    first_lg = 1 + MID_W // PBLK

    @pl.when(b == 0)
    def _():
        use(d0[...])

    @pl.when(jnp.logical_and(b >= 1, b < first_lg))
    def _():
        use(d1[...])

    @pl.when(b >= first_lg)
    def _():
        use(d2[...])


def win_grad(d0, d1, d2, h):
    s, d = h.shape
    tk = min(s, 2048)
    nk = s // tk

    def body(d0_ref, d1_ref, d2_ref, h_ref, o_ref, o16_ref):
        @pl.when(pl.program_id(1) == 0)
        def _():
            o_ref[...] = jnp.zeros_like(o_ref)

        def use(blk):
            o_ref[...] += _dot_tn(blk, h_ref[...])

        _pick_dproj(pl.program_id(0), d0_ref, d1_ref, d2_ref, use)

        @pl.when(pl.program_id(1) == nk - 1)
        def _():
            o16_ref[...] = o_ref[...].astype(BF16)

    def spec(first, count):
        def imap(j, k):
            used = jnp.logical_and(j >= first, j < first + count)
            return (jnp.where(used, k, 0), jnp.clip(j - first, 0, count - 1))
        return pl.BlockSpec((tk, PBLK), imap)

    nm = MID_W // PBLK
    oblk = pl.BlockSpec((PBLK, d), lambda j, k: (j, 0))
    return pl.pallas_call(
        body, out_shape=(_sds((IN_WIDTH, d), F32), _sds((IN_WIDTH, d), BF16)), grid=(N_PBLK, nk),
        in_specs=[spec(0, 1), spec(1, nm), spec(1 + nm, LG_W // PBLK), pl.BlockSpec((tk, d), lambda j, k: (k, 0))],
        out_specs=(oblk, oblk),
        compiler_params=_cp("parallel", "arbitrary"), name="win_grad")(d0, d1, d2, h)


def h_bwd(d0, d1, d2, w_t, x, dx_out, g, scatter=()):
    s, d = x.shape
    tm = min(s, 512)
    nt = s // tm
    ns = len(scatter)
    na = ns // 2

    def body(d0_ref, d1_ref, d2_ref, w_ref, x_ref, dxo_ref, g_ref, *rest):
        s_in, (dx_ref, gg_ref), s_out = rest[:ns], rest[ns:ns + 2], rest[ns + 2:2 * ns + 2]
        if ns:
            start, finish = scatter_stages([a.shape[1:] for a in scatter[:na]], s_in[:na], s_in[na:], s_out[:na], s_out[na:],
                                           *rest[2 * ns + 2:])
            pl.when(pl.program_id(0) == 0)(start)

        @pl.when(pl.program_id(0) == 0)
        def _():
            gg_ref[...] = jnp.zeros_like(gg_ref)

        dh = (_dot(d0_ref[...], w_ref[0:PBLK, :]) + _dot(d1_ref[...], w_ref[PBLK:PBLK + MID_W, :])
              + _dot(d2_ref[...], w_ref[PBLK + MID_W:, :]))
        xf = x_ref[...]
        r = lax.rsqrt(jnp.mean(xf * xf, axis=-1, keepdims=True) + EPS)
        xh = xf * r
        gg_ref[...] += jnp.sum(dh * xh, axis=0, keepdims=True)
        u = dh * g_ref[...]
        dx_ref[...] = dxo_ref[...] + r * (u - xh * jnp.mean(u * xh, axis=-1, keepdims=True))
        if ns:
            pl.when(pl.program_id(0) == nt - 1)(finish)

    rowb = _rows(tm, d)
    out = pl.pallas_call(
        body, out_shape=(_sds((s, d), F32), _sds((1, d), F32)) + scatter_out_shapes(scatter[:na]), grid=(nt,),
        in_specs=[_rows(tm, PBLK), _rows(tm, MID_W), _rows(tm, LG_W),
                  pl.BlockSpec(w_t.shape, lambda i: (0, 0), pipeline_mode=pl.Buffered(1)), rowb, rowb, _full((1, d))] + [_ANY] * ns,
        out_specs=(rowb, _full((1, d))) + (_ANY,) * ns,
        scratch_shapes=scatter_sems(na) if ns else [],
        compiler_params=_cp("arbitrary"), name="h_bwd_scatter" if ns else "h_bwd")(d0, d1, d2, w_t, x, dx_out, g, *scatter)
    return out[0], out[1], list(out[2:2 + na]), list(out[2 + na:])


def memkv_bwd(mem, g, mem_n, w_kv, dkv):
    m, d = mem.shape

    def body(mem_ref, g_ref, mn_ref, w_ref, dkv_ref, gw_ref, gw16_ref, gg_ref):
        dkb = dkv_ref[...].astype(BF16)
        gw = _dot_tn(mn_ref[...], dkb)
        gw_ref[...] = gw
        gw16_ref[...] = gw.astype(BF16)
        dmn = _dot_nt(dkb, w_ref[...])
        mf = mem_ref[...]
        r = lax.rsqrt(jnp.mean(mf * mf, axis=-1, keepdims=True) + EPS)
        gg_ref[...] = jnp.sum(dmn * (mf * r), axis=0, keepdims=True)

    return pl.pallas_call(
        body, out_shape=(_sds(w_kv.shape, F32), _sds(w_kv.shape, BF16), _sds((1, d), F32)),
        compiler_params=_cp(), name="memkv_bwd")(mem, g, mem_n, w_kv, dkv)


def _layer_consts(seq):
    i = jnp.arange(A_WIDTH)
    return dict(
        tabs=rope_tables(seq),
        gq=_group_ones(A_WIDTH, A_HEAD_DIM).astype(BF16), gk=_group_ones(A_KV_WIDTH, A_HEAD_DIM).astype(BF16),
        fold_q=(i[:, None] % A_HEAD_DIM == jnp.arange(LANES)[None, :]).astype(F32),
        fold_k=(i[:A_KV_WIDTH, None] % A_HEAD_DIM == jnp.arange(LANES)[None, :]).astype(F32),
        head_sel=(jnp.arange(A_HEADS)[:, None] == i[None, :] // A_HEAD_DIM).astype(F32),
    )


_BIG = ("win_t", "wkv", "wbr", "wout")


def _with_own_part(names, gathered, shards, chip, d):
    shape = dict(win_t=(IN_WIDTH, d), wkv=(d, 2 * M_WIDTH), wbr=(N_CHIPS, N_BRANCH, A_WIDTH, d // N_CHIPS), wout=(d, d))
    return {n: lax.dynamic_update_slice(g, sh[None], (chip, 0, 0)).reshape(shape[n]) for n, g, sh in zip(names, gathered, shards)}


def local_fwd_bwd(x, mem, tgt, small, big=None, shards=None, place=None):
    s, d = x.shape
    depth = small["norm_g"].shape[0]
    k = _layer_consts(s)
    row = lambda v: v.reshape(1, -1)
    dist = shards is not None
    if dist:
        big = [_with_own_part(_BIG[:1], allgather_layer(shards[0][:1]), shards[0][:1], place[0], d)] + [None] * (depth - 1)
    saved = []
    for l in range(depth):
        ng = row(small["norm_g"][l])
        qg = row(jnp.tile(small["q_norm_g"][l], A_HEADS))
        kg = row(jnp.tile(small["k_norm_g"][l], A_KV_HEADS))
        ws = small["w_s"][l].astype(BF16)
        ws_t = jnp.swapaxes(small["w_s"][l], 1, 2).astype(BF16)
        bsb = jnp.broadcast_to(small["b_s"][l][:, :, None], (B_GROUPS, CHUNK, B_GROUP_DIM))
        lng, lnb = row(small["sg_ln_g"][l]), row(small["sg_ln_b"][l])
        mg = row(small["mem_norm_g"][l])
        w = big[l]
        h = rms_fwd(x, ng) if l == 0 else h_next
        proj = proj_fwd(h, w["win_t"])
        q_t, kr, vb, kr_t, vte0, vte1 = qk_prep(proj, k["tabs"], qg, kg, k["gq"], k["gk"])
        late = list(shards[0][1:]) if dist and l == 0 else []
        nxt = list(shards[l + 1]) if dist and l + 1 < depth else []
        o_a, lse, gathered = attn_fwd(q_t, kr, vte0, vte1, gather=tuple(late + nxt))
        if late:
            w.update(_with_own_part(_BIG[1:], gathered[:len(late)], late, place[0], d))
        if nxt:
            big[l + 1] = _with_own_part(_BIG, gathered[len(late):], nxt, place[0], d)
        mem_n, kv = memkv_fwd(mem, mg, w["wkv"])
        next_g = row(small["norm_g"][l + 1]) if l + 1 < depth else row(small["final_g"])
        x_next, y, up, merged, h_next = branch_fwd(x, proj, o_a, kv, ws, bsb, lng, lnb, w["wbr"], w["wout"], next_g)
        saved.append(dict(x=x, ng=ng, qg=qg, kg=kg, ws=ws, ws_t=ws_t, bsb=bsb, lng=lng, lnb=lnb, mg=mg, h=h, proj=proj,
                          q_t=q_t, kr=kr, kr_t=kr_t, vb=vb, o_a=o_a, lse=lse, mem_n=mem_n, kv=kv, y=y, up=up, merged=merged))
        x = x_next

    sq, dx, g_final = final_loss(x, row(small["final_g"]), tgt)
    grads = {n: [None] * depth for n in ("norm_g", "q_norm_g", "k_norm_g", "sg_ln_g", "sg_ln_b", "w_s", "b_s", "mem_norm_g")}
    parts = lambda g: g.reshape(N_CHIPS, -1, g.shape[-1])
    reduced = [[None] * len(_BIG) for _ in range(depth)]

    def reduce_all(items, t_sib, t_rem):
        if items:
            for (ll, a, _, _), f in zip(items, reduce_rows(place, [i[2] for i in items], t_sib, t_rem)):
                reduced[ll][a] = f

    as_scatter = lambda items: tuple(i[2] for i in items) + tuple(i[3] for i in items)
    pending = []
    for l in reversed(range(depth)):
        sv, w = saved[l], big[l]
        dy, dlg, g_wout, g_wbr, g_wout16, g_wbr16 = merge_bwd(dx, sv["proj"], sv["y"], sv["up"], sv["merged"], w["wbr"], w["wout"])
        dmid, do_t, delta, g_ws, g_bs, g_lng, g_lnb, dkv = branch_bwd(
            dy, sv["proj"], sv["o_a"], sv["kv"], sv["ws"], sv["ws_t"], sv["bsb"], sv["lng"], sv["lnb"], k["head_sel"])
        g_wkv, g_wkv16, g_mg = memkv_bwd(mem, sv["mg"], sv["mem_n"], w["wkv"], dkv)
        if dist:
            pending += [(l, 1, parts(g_wkv), parts(g_wkv16)), (l, 2, parts(g_wbr), parts(g_wbr16)), (l, 3, parts(g_wout), parts(g_wout16))]
        dq_t, dkr, dvb, t_sib, t_rem = attn_bwd(sv["q_t"], do_t, sv["kr"], sv["kr_t"], sv["vb"], sv["lse"], delta,
                                                scatter=as_scatter(pending))
        reduce_all(pending, t_sib, t_rem)
        dqkv, g_qg, g_kg = qk_prep_bwd(sv["proj"], dq_t, dkr, dvb, k["tabs"], sv["qg"], sv["kg"], k["gq"], k["gk"],
                                       k["fold_q"], k["fold_k"])
        g_win, g_win16 = win_grad(dqkv, dmid, dlg, sv["h"])
        pending = [(l, 0, parts(g_win), parts(g_win16))] if dist else []
        last = as_scatter(pending) if l == 0 else ()
        dx, g_ng, t_sib, t_rem = h_bwd(dqkv, dmid, dlg, w["win_t"], sv["x"], dx, sv["ng"], scatter=last)
        if last:
            reduce_all(pending, t_sib, t_rem)
        grads["norm_g"][l] = g_ng[0]
        grads["q_norm_g"][l] = g_qg[0, :A_HEAD_DIM]
        grads["k_norm_g"][l] = g_kg[0, :A_HEAD_DIM]
        grads["sg_ln_g"][l] = g_lng[0]
        grads["sg_ln_b"][l] = g_lnb[0]
        grads["w_s"][l] = g_ws
        grads["b_s"][l] = g_bs[:, :, 0]
        grads["mem_norm_g"][l] = g_mg[0]
        if not dist:
            reduced[l] = dict(zip(_BIG, (parts(g_win), parts(g_wkv), parts(g_wbr), parts(g_wout))))
    grads = {n: jnp.stack(v) for n, v in grads.items()}
    grads["final_g"] = g_final[0]
    return sq[0, 0], dx, grads, reduced


def _row_block(rows, width, cap_bytes=2 * 2**20):
    best = None
    for br in range(8, rows + 1, 8):
        if rows % br == 0 and br * width * 4 <= cap_bytes:
            best = br
    return best if best is not None else rows


def adamw(w, gs, m, v):
    r, c = w.shape
    n = len(gs)
    rs = r // n
    br = _row_block(rs, c)
    nb = rs // br

    def body(w_ref, *refs):
        g_refs, (m_ref, v_ref, og_ref, d_ref, nm_ref, nv_ref) = refs[:n], refs[n:]

        def update(gg):
            mm = ADAM_B1 * m_ref[...] + (1.0 - ADAM_B1) * gg
            vv = ADAM_B2 * v_ref[...] + (1.0 - ADAM_B2) * (gg * gg)
            m_hat = mm / (1.0 - ADAM_B1 ** ADAM_STEP)
            v_hat = vv / (1.0 - ADAM_B2 ** ADAM_STEP)
            og_ref[...] = gg
            d_ref[...] = -ADAM_LR * (m_hat / (jnp.sqrt(v_hat) + ADAM_EPS) + ADAM_WD * w_ref[...])
            nm_ref[...] = mm
            nv_ref[...] = vv

        for k in range(n):
            pl.when(pl.program_id(0) == k)(functools.partial(lambda k: update(g_refs[k][...]), k))

    blk = pl.BlockSpec((br, c), lambda l, i: (l * nb + i, 0))
    g_specs = [pl.BlockSpec((br, c), functools.partial(lambda l, i, k: (jnp.where(l == k, i, 0), 0), k=k)) for k in range(n)]
    return pl.pallas_call(
        body, out_shape=(_sds((r, c), F32),) * 4, grid=(n, nb), in_specs=[blk] + g_specs + [blk, blk], out_specs=(blk,) * 4,
        compiler_params=_cp("arbitrary", "arbitrary"), name="adamw")(w, *gs, m, v)


N_REMOTE = 2 * (N_CHIPS - 1)


def reduce_rows(place, gs, t_sibs, t_rems):
    n = len(gs)
    nt = 2

    def body(place_ref, *refs):
        for a in range(n):
            g_ref, s_ref, t_ref, f_ref = refs[a], refs[n + a], refs[2 * n + a], refs[3 * n + a]
            acc = g_ref[...] + s_ref[...]
            for j in range(N_REMOTE):
                acc = acc + t_ref[j].astype(F32)
            f_ref[...] = acc

    tiles = [(g.shape[1] // 2 // nt, g.shape[2]) for g in gs]
    return pl.pallas_call(
        body, out_shape=tuple(_sds(g.shape[1:], F32) for g in gs),
        grid_spec=pltpu.PrefetchScalarGridSpec(
            num_scalar_prefetch=1, grid=(nt,),
            in_specs=[pl.BlockSpec((None, tr, c), lambda i, p: (p[0], p[1] * nt + i, 0)) for tr, c in tiles]
            + [pl.BlockSpec((tr, c), lambda i, p: (i, 0)) for tr, c in tiles]
            + [pl.BlockSpec((N_REMOTE, tr, c), lambda i, p: (0, i, 0)) for tr, c in tiles],
            out_specs=tuple(pl.BlockSpec((tr, c), lambda i, p: (p[1] * nt + i, 0)) for tr, c in tiles)),
        compiler_params=_cp("parallel"), name="reduce_rows")(place, *gs, *t_sibs, *t_rems)


_ANY = pl.BlockSpec(memory_space=pl.ANY)


def _place():
    x, y, c = lax.axis_index("x"), lax.axis_index("y"), lax.axis_index("c")
    chips = [(1 - x, y), (x, 1 - y), (1 - x, 1 - y)]
    return x, y, c, chips


def gather_sems(n):
    return [pltpu.SemaphoreType.DMA((n, N_REMOTE)), pltpu.SemaphoreType.DMA((n, N_REMOTE))]


def gather_stages(shapes, ins, outs, send, recv):
    n = len(shapes)
    x, y, c, chips = _place()
    me = 2 * x + y
    sib = (x, y, 1 - c)

    def rows(a, hl):
        r2 = shapes[a][0] // 2
        return pl.ds(hl * r2, r2)

    def remote(a, k, src, dst, dev):
        return pltpu.make_async_remote_copy(src, dst, send.at[a, k], recv.at[a, k], device_id=dev, device_id_type=MESH)

    def sent(a, k):
        cx, cy = chips[k]
        return remote(a, k, ins[a].at[rows(a, c)], outs[a].at[me, rows(a, c)], (cx, cy, c))

    def passed(a, k, hl):
        cx, cy = chips[k]
        got = outs[a].at[2 * cx + cy, rows(a, hl)]
        return remote(a, k, got, got, (cx, cy, c)), remote(a, 3 + k, got, got, sib)

    def start():
        for a in range(n):
            for k in range(3):
                sent(a, k).start()

    def forward():
        for k in range(3):
            for a in range(n):
                arrived, on = passed(a, k, c)
                arrived.wait_recv()
                on.start()

    def finish():
        for k in range(3):
            for a in range(n):
                passed(a, k, 1 - c)[1].wait_recv()
        for k in range(3):
            for a in range(n):
                sent(a, k).wait_send()
                passed(a, k, c)[1].wait_send()

    return start, forward, finish


def allgather_layer(shards):
    n = len(shards)

    def body(*refs):
        for stage in gather_stages([a.shape for a in shards], refs[:n], refs[n:2 * n], *refs[2 * n:]):
            stage()

    return pl.pallas_call(
        body, out_shape=tuple(_sds((N_CHIPS,) + a.shape, a.dtype) for a in shards),
        in_specs=[_ANY] * n, out_specs=(_ANY,) * n, scratch_shapes=gather_sems(n), name="allgather_layer")(*shards)


def scatter_sems(n):
    return [pltpu.SemaphoreType.DMA((n, N_REMOTE + 1)), pltpu.SemaphoreType.DMA((n, N_REMOTE + 1))]


def scatter_out_shapes(gs):
    return (tuple(_sds((g.shape[1] // 2, g.shape[2]), F32) for g in gs)
            + tuple(_sds((N_REMOTE, g.shape[1] // 2, g.shape[2]), BF16) for g in gs))


def scatter_stages(shapes, gf, gb, t_sib, t_rem, send, recv):
    n = len(shapes)
    x, y, c, chips = _place()
    me = 2 * x + y

    def copies():
        out = []
        for a in range(n):
            r2 = shapes[a][0] // 2
            out.append(pltpu.make_async_remote_copy(gf[a].at[me, pl.ds((1 - c) * r2, r2)], t_sib[a], send.at[a, N_REMOTE],
                                                    recv.at[a, N_REMOTE], device_id=(x, y, 1 - c), device_id_type=MESH))
            for k, (cx, cy) in enumerate(chips):
                for o in range(2):
                    tc = c if o == 0 else 1 - c
                    out.append(pltpu.make_async_remote_copy(gb[a].at[2 * cx + cy, pl.ds(tc * r2, r2)], t_rem[a].at[2 * k + o],
                                                            send.at[a, 2 * k + o], recv.at[a, 2 * k + o],
                                                            device_id=(cx, cy, tc), device_id_type=MESH))
        return out

    def start():
        for cp in copies():
            cp.start()

    def finish():
        for cp in copies():
            cp.wait()

    return start, finish


def finish_exchange(v, fs):
    n = len(fs)
    r, w = v.shape
    ndev = 2 * N_CHIPS

    def body(v_ref, *refs):
        out, sum_ref = refs[n:2 * n], refs[2 * n]
        all_ref, send, recv, loc, fsend, frecv = refs[2 * n + 1:]
        x, y, c, chips = _place()
        me, sib = (x, y, c), (x, y, 1 - c)
        swaps = []
        for a in range(n):
            r2 = fs[a].shape[0] // 2
            mine = out[a].at[pl.ds(c * r2, r2)]
            cp = pltpu.make_async_remote_copy(mine, mine, fsend.at[a], frecv.at[a], device_id=sib, device_id_type=MESH)
            cp.start()
            swaps.append(cp)

        def slab(px, py, pc):
            return all_ref.at[4 * px + 2 * py + pc]

        def copy(k, block, to, src=None):
            return pltpu.make_async_remote_copy(slab(*block) if src is None else src, slab(*block), send.at[k], recv.at[k],
                                                device_id=to, device_id_type=MESH)

        mine = pltpu.make_async_copy(v_ref, slab(*me), loc)
        mine.start()
        first = [copy(0, me, sib, src=v_ref)] + [copy(1 + j, me, (*chip, c), src=v_ref) for j, chip in enumerate(chips)]
        for cp in first:
            cp.start()
        passed = [copy(4 + j, (*chip, c), sib) for j, chip in enumerate(chips)]
        for j, chip in enumerate(chips):
            copy(1 + j, (*chip, c), me).wait_recv()
            passed[j].start()
        copy(0, sib, me).wait_recv()
        for j, chip in enumerate(chips):
            copy(4 + j, (*chip, 1 - c), me).wait_recv()
        for cp in first + passed:
            cp.wait_send()
        mine.wait()
        acc = all_ref[0]
        for i in range(1, ndev):
            acc = acc + all_ref[i]
        sum_ref[...] = acc
        for a, cp in enumerate(swaps):
            r2 = fs[a].shape[0] // 2
            theirs = out[a].at[pl.ds((1 - c) * r2, r2)]
            cp.wait_send()
            pltpu.make_async_remote_copy(theirs, theirs, fsend.at[a], frecv.at[a], device_id=sib, device_id_type=MESH).wait_recv()

    vm = pl.BlockSpec(memory_space=pltpu.VMEM)
    res = pl.pallas_call(
        body, out_shape=tuple(_sds(f.shape, F32) for f in fs) + (_sds((r, w), F32),),
        in_specs=[vm] + [_ANY] * n, out_specs=(_ANY,) * n + (vm,), input_output_aliases={a + 1: a for a in range(n)},
        scratch_shapes=[pltpu.VMEM((ndev, r, w), F32), pltpu.SemaphoreType.DMA((7,)), pltpu.SemaphoreType.DMA((7,)),
                        pltpu.SemaphoreType.DMA, pltpu.SemaphoreType.DMA((n,)), pltpu.SemaphoreType.DMA((n,))],
        compiler_params=pltpu.CompilerParams(vmem_limit_bytes=VMEM_LIMIT), name="finish_exchange")(v, *fs)
    return res[n], list(res[:n])


_SMALL = ("norm_g", "q_norm_g", "k_norm_g", "sg_ln_g", "sg_ln_b", "w_s", "b_s", "mem_norm_g", "final_g")
_WEIGHTS = ("norm_g", "w_in", "q_norm_g", "k_norm_g", "sg_ln_g", "sg_ln_b", "w_s", "b_s", "mem_norm_g", "w_mem_kv", "w_br",
            "w_out", "final_g")


def _pack(d):
    flat = jnp.concatenate([d[n].reshape(-1) for n in _SMALL])
    rows = -(-flat.shape[0] // (8 * LANES)) * 8
    return jnp.pad(flat, (0, rows * LANES - flat.shape[0])).reshape(rows, LANES)


def _unpack(p, like):
    flat, out, o = p.reshape(-1), {}, 0
    for n in _SMALL:
        out[n] = flat[o:o + like[n].size].reshape(like[n].shape)
        o += like[n].size
    return out


def kernel(x, mem, norm_g, w_in, q_norm_g, k_norm_g, sg_ln_g, sg_ln_b, w_s, b_s, mem_norm_g, w_mem_kv, w_br, w_out, final_g, loss_target, m_norm_g, m_w_in, m_q_norm_g, m_k_norm_g, m_sg_ln_g, m_sg_ln_b, m_w_s, m_b_s, m_mem_norm_g, m_w_mem_kv, m_w_br, m_w_out, m_final_g, v_norm_g, v_w_in, v_q_norm_g, v_k_norm_g, v_sg_ln_g, v_sg_ln_b, v_w_s, v_b_s, v_mem_norm_g, v_w_mem_kv, v_w_br, v_w_out, v_final_g):
    w = dict(norm_g=norm_g, w_in=w_in, q_norm_g=q_norm_g, k_norm_g=k_norm_g, sg_ln_g=sg_ln_g, sg_ln_b=sg_ln_b, w_s=w_s, b_s=b_s,
             mem_norm_g=mem_norm_g, w_mem_kv=w_mem_kv, w_br=w_br, w_out=w_out, final_g=final_g)
    m = dict(norm_g=m_norm_g, w_in=m_w_in, q_norm_g=m_q_norm_g, k_norm_g=m_k_norm_g, sg_ln_g=m_sg_ln_g, sg_ln_b=m_sg_ln_b,
             w_s=m_w_s, b_s=m_b_s, mem_norm_g=m_mem_norm_g, w_mem_kv=m_w_mem_kv, w_br=m_w_br, w_out=m_w_out, final_g=m_final_g)
    v = dict(norm_g=v_norm_g, w_in=v_w_in, q_norm_g=v_q_norm_g, k_norm_g=v_k_norm_g, sg_ln_g=v_sg_ln_g, sg_ln_b=v_sg_ln_b,
             w_s=v_w_s, b_s=v_b_s, mem_norm_g=v_mem_norm_g, w_mem_kv=v_w_mem_kv, w_br=v_w_br, w_out=v_w_out, final_g=v_final_g)
    depth, d = norm_g.shape
    nsh = N_CHIPS
    br_rows = N_BRANCH * A_WIDTH
    br_cols = d // nsh

    shards = [[jnp.swapaxes(w_in[l], 0, 1).astype(BF16), w_mem_kv[l].astype(BF16), w_br[l].astype(BF16).reshape(br_rows, br_cols),
               w_out[l].astype(BF16)] for l in range(depth)]
    place = jnp.stack([2 * lax.axis_index("x") + lax.axis_index("y"), lax.axis_index("c")]).astype(jnp.int32)
    small = {n: w[n] for n in _SMALL}

    sq, dx, grads, reduced = local_fwd_bwd(x[0], mem[0], loss_target[0], small, shards=shards, place=place)
    loss = (0.5 / d) * lax.psum(sq, ("x", "y", "c"))

    small_sum, finals = finish_exchange(_pack(grads), [g for layer in reduced for g in layer])
    big_grads = dict(zip(("w_in", "w_mem_kv", "w_br", "w_out"), [finals[a::len(_BIG)] for a in range(len(_BIG))]))
    small_grads = _unpack(small_sum, small)

    out_g, out_d, out_m, out_v = {}, {}, {}, {}
    _, sd, sm, sv = adamw(_pack(small), [small_sum], _pack({n: m[n] for n in _SMALL}), _pack({n: v[n] for n in _SMALL}))
    sd, sm, sv = _unpack(sd, small), _unpack(sm, small), _unpack(sv, small)
    for n in _SMALL:
        out_g[n], out_d[n], out_m[n], out_v[n] = small_grads[n], sd[n], sm[n], sv[n]
    for n, gs in big_grads.items():
        into = (lambda a: jnp.swapaxes(a, 1, 2)) if n == "w_in" else (lambda a: a)
        two_d = lambda a: a.reshape(-1, gs[0].shape[-1])
        res = adamw(two_d(into(w[n])), gs, two_d(into(m[n])), two_d(into(v[n])))
        out_g[n], out_d[n], out_m[n], out_v[n] = [into(t.reshape(into(w[n]).shape)) for t in res]
    return (loss, dx[None], *[out_g[n] for n in _WEIGHTS], *[out_d[n] for n in _WEIGHTS], *[out_m[n] for n in _WEIGHTS],
            *[out_v[n] for n in _WEIGHTS])
```

```python
import functools

import jax
import jax.numpy as jnp
from jax import lax
from jax.experimental import pallas as pl
from jax.experimental.pallas import tpu as pltpu

F32 = jnp.float32
BF16 = jnp.bfloat16

D_MODEL = 1024
GRID_W = 64
CHUNK = 128
ROPE_THETA = 10000.0
EPS = 1e-6
A_HEADS, A_KV_HEADS, A_HEAD_DIM = 8, 2, 64
A_WIDTH, A_KV_WIDTH = 512, 128
B_GROUPS, B_GROUP_DIM, B_WIDTH = 4, 128, 512
M_HEADS, M_HEAD_DIM, M_WIDTH = 4, 128, 512
N_BRANCH = 3
IN_WIDTH = 6912
O_QA, O_KA, O_VA, O_ZA, O_UB, O_VB, O_ZB, O_QM, O_ZM, O_LG = 0, 512, 640, 768, 1280, 1792, 2304, 2816, 3328, 3840
PBLK = 768
N_PBLK = IN_WIDTH // PBLK
MID_W = 3072
LG_W = 3072

LN2 = 0.6931471805599453
Q_SCALE = A_HEAD_DIM ** -0.5 / LN2
VTE_ROWS = A_HEAD_DIM + 16

ADAM_LR, ADAM_B1, ADAM_B2, ADAM_EPS, ADAM_WD, ADAM_STEP = 0.001, 0.9, 0.999, 1e-08, 0.01, 10

V7X_VMEM_BYTES = 64 * 2**20
VMEM_LIMIT = V7X_VMEM_BYTES - 4 * 2**20
LANES = 128
MESH = pl.DeviceIdType.MESH
N_CHIPS = 4


def _cp(*sem):
    return pltpu.CompilerParams(dimension_semantics=sem if sem else None, vmem_limit_bytes=VMEM_LIMIT)


def _dot(a, b):
    return jnp.dot(a, b, preferred_element_type=F32)


def _dot_nt(a, b):
    return lax.dot_general(a, b, (((1,), (1,)), ((), ())), preferred_element_type=F32)


def _dot_tn(a, b):
    return lax.dot_general(a, b, (((0,), (0,)), ((), ())), preferred_element_type=F32)


def _dot_hi(a, b):
    return jnp.dot(a, b, preferred_element_type=F32, precision=lax.Precision.HIGHEST)


def _group_sum(a, ones):
    hi = a.astype(BF16)
    lo = (a - hi.astype(F32)).astype(BF16)
    return _dot(hi, ones) + _dot(lo, ones)


def _dot_nt_hi(a, b):
    return lax.dot_general(a, b, (((1,), (1,)), ((), ())), preferred_element_type=F32, precision=lax.Precision.HIGHEST)


def _sig(z):
    return 1.0 / (1.0 + jnp.exp(-z))


def _full(shape, once=False):
    nd = len(shape)
    return pl.BlockSpec(shape, lambda *_: (0,) * nd, pipeline_mode=pl.Buffered(1) if once else None)


def _rows(tm, width):
    return pl.BlockSpec((tm, width), lambda i: (i, 0))


def _sds(shape, dtype):
    return jax.ShapeDtypeStruct(shape, dtype)


def rms_fwd(x, g):
    s, d = x.shape
    tm = min(s, 512)

    def body(x_ref, g_ref, h_ref):
        xf = x_ref[...]
        r = lax.rsqrt(jnp.mean(xf * xf, axis=-1, keepdims=True) + EPS)
        h_ref[...] = ((xf * r) * g_ref[...]).astype(BF16)

    return pl.pallas_call(
        body, out_shape=_sds((s, d), BF16), grid=(s // tm,),
        in_specs=[_rows(tm, d), _full((1, d))], out_specs=_rows(tm, d),
        compiler_params=_cp("parallel"), name="rms_fwd")(x, g)


def proj_fwd(h, w_t):
    s, d = h.shape
    n = w_t.shape[0]
    tm = min(s, 512)
    tn = 2304

    def body(h_ref, w_ref, o_ref):
        o_ref[...] = _dot_nt(h_ref[...], w_ref[...]).astype(BF16)

    return pl.pallas_call(
        body, out_shape=_sds((s, n), BF16), grid=(n // tn, s // tm),
        in_specs=[pl.BlockSpec((tm, d), lambda j, i: (i, 0)), pl.BlockSpec((tn, d), lambda j, i: (j, 0))],
        out_specs=pl.BlockSpec((tm, tn), lambda j, i: (i, j)),
        compiler_params=_cp("parallel", "parallel"), name="proj_fwd")(h, w_t)


def rope_tables(seq):
    n_freq = A_HEAD_DIM // 4
    d = jnp.arange(LANES) % A_HEAD_DIM
    seg, half, freq = d // (2 * n_freq), (d % (2 * n_freq)) // n_freq, d % n_freq
    inv = ROPE_THETA ** (-freq.astype(F32) / n_freq)
    t = jnp.arange(seq)
    pos = jnp.where(seg[None, :] == 0, (t // GRID_W)[:, None], (t % GRID_W)[:, None]).astype(F32)
    ang = pos * inv[None, :]
    cos, sin = jnp.cos(ang), jnp.sin(ang)
    return cos, jnp.where(half[None, :] == 1, sin, 0.0), jnp.where(half[None, :] == 0, -sin, 0.0)


def _group_ones(width, group):
    i = jnp.arange(width)
    return (i[:, None] // group == i[None, :] // group).astype(F32)


def _rope(xn, c, sa, sb):
    w = xn.shape[1]
    return xn * c + pltpu.roll(xn, 16, 1) * sa + pltpu.roll(xn, w - 16, 1) * sb


def _rope_t(dy, c, sa, sb):
    w = dy.shape[1]
    return dy * c + pltpu.roll(dy * sa, w - 16, 1) + pltpu.roll(dy * sb, 16, 1)


def _tile4(t):
    return jnp.concatenate([t, t, t, t], axis=1)


def qk_prep(proj, tabs, qg, kg, gq, gk):
    s = proj.shape[0]
    tm = min(s, 512)
    c, sa, sb = tabs

    def body(p_ref, c_ref, sa_ref, sb_ref, qg_ref, kg_ref, gq_ref, gk_ref, qt_ref, kr_ref, vb_ref, kt_ref, v0_ref, v1_ref):
        xq = p_ref[:, O_QA:O_QA + A_WIDTH].astype(F32)
        xk = p_ref[:, O_KA:O_KA + A_KV_WIDTH].astype(F32)
        xv = p_ref[:, O_VA:O_VA + A_KV_WIDTH].astype(F32)
        cc, ssa, ssb = c_ref[...], sa_ref[...], sb_ref[...]
        msq = _group_sum(xq * xq, gq_ref[...]) * (1.0 / A_HEAD_DIM)
        qn = (xq * lax.rsqrt(msq + EPS)) * qg_ref[...]
        qr = _rope(qn, _tile4(cc), _tile4(ssa), _tile4(ssb)) * Q_SCALE
        qt_ref[...] = qr.T.astype(BF16)
        msk = _group_sum(xk * xk, gk_ref[...]) * (1.0 / A_HEAD_DIM)
        kn = (xk * lax.rsqrt(msk + EPS)) * kg_ref[...]
        kr = _rope(kn, cc, ssa, ssb)
        kr_ref[...] = kr.astype(BF16)
        vb_ref[...] = xv.astype(BF16)
        kt_ref[...] = kr.T.astype(BF16)
        vt = xv.T.astype(BF16)
        one = jnp.ones((VTE_ROWS - A_HEAD_DIM, tm), BF16)
        v0_ref[...] = jnp.concatenate([vt[:A_HEAD_DIM], one], axis=0)
        v1_ref[...] = jnp.concatenate([vt[A_HEAD_DIM:], one], axis=0)

    tab = _rows(tm, LANES)
    colb = lambda w: pl.BlockSpec((w, tm), lambda i: (0, i))
    return pl.pallas_call(
        body,
        out_shape=(_sds((A_WIDTH, s), BF16), _sds((s, A_KV_WIDTH), BF16), _sds((s, A_KV_WIDTH), BF16),
                   _sds((A_KV_WIDTH, s), BF16), _sds((VTE_ROWS, s), BF16), _sds((VTE_ROWS, s), BF16)),
        grid=(s // tm,),
        in_specs=[_rows(tm, PBLK), tab, tab, tab, _full((1, A_WIDTH)), _full((1, A_KV_WIDTH)),
                  _full((A_WIDTH, A_WIDTH)), _full((A_KV_WIDTH, A_KV_WIDTH))],
        out_specs=(colb(A_WIDTH), _rows(tm, A_KV_WIDTH), _rows(tm, A_KV_WIDTH), colb(A_KV_WIDTH), colb(VTE_ROWS), colb(VTE_ROWS)),
        compiler_params=_cp("parallel"), name="qk_prep")(proj, c, sa, sb, qg, kg, gq, gk)


def _pad_head(q_h, kv):
    z = jnp.zeros_like(q_h)
    return jnp.concatenate([q_h, z], axis=0) if kv == 0 else jnp.concatenate([z, q_h], axis=0)


def attn_fwd(q_t, kr, vte0, vte1, gather=()):
    s = kr.shape[0]
    tq = min(s, 512)
    kc = min(s, 256)
    nkc = s // kc
    nq = s // tq
    grp = A_HEADS // A_KV_HEADS
    ng = len(gather)

    def body(qt_ref, kr_ref, v0_ref, v1_ref, *rest):
        g_in, (o_ref, lse_ref), g_out = rest[:ng], rest[ng:ng + 2], rest[ng + 2:2 * ng + 2]
        qp_ref, m_ref, acc_ref = rest[2 * ng + 2:2 * ng + 5]
        if ng:
            start, forward, finish = gather_stages([g.shape for g in gather], g_in, g_out, *rest[2 * ng + 5:])
            pl.when(pl.program_id(0) == 0)(start)
            pl.when(pl.program_id(0) == (3 * nq) // 4)(forward)

        for h in range(A_HEADS):
            qp_ref[h] = _pad_head(qt_ref[A_HEAD_DIM * h:A_HEAD_DIM * (h + 1), :], h // grp)
        m_ref[...] = jnp.full(m_ref.shape, -1e30, F32)
        acc_ref[...] = jnp.zeros_like(acc_ref)

        def step(ci, carry):
            ks = pl.ds(pl.multiple_of(ci * kc, kc), kc)
            kblk = kr_ref[ks, :]
            vts = (v0_ref[:, ks], v1_ref[:, ks])
            scs = [_dot(kblk, qp_ref[h]) for h in range(A_HEADS)]
            for h in range(A_HEADS):
                sc = scs[h]
                m_prev = m_ref[h:h + 1, :]
                m_new = jnp.maximum(m_prev, jnp.max(sc, axis=0, keepdims=True))
                p = jnp.exp2(sc - m_new)
                acc_ref[h] = acc_ref[h] * jnp.exp2(m_prev - m_new) + _dot(vts[h // grp], p.astype(BF16))
                m_ref[h:h + 1, :] = m_new
            return carry

        lax.fori_loop(0, nkc, step, 0)
        outs, lses = [], []
        for h in range(A_HEADS):
            acc = acc_ref[h]
            l = acc[A_HEAD_DIM:A_HEAD_DIM + 1, :]
            outs.append(acc[:A_HEAD_DIM, :] / l)
            lses.append(m_ref[h:h + 1, :] + jnp.log2(l))
        o_ref[...] = jnp.concatenate(outs, axis=0).T
        lse_ref[...] = jnp.concatenate(lses, axis=0)
        if ng:
            pl.when(pl.program_id(0) == nq - 1)(finish)

    out = pl.pallas_call(
        body,
        out_shape=(_sds((s, A_WIDTH), F32), _sds((A_HEADS, s), F32)) + tuple(_sds((N_CHIPS,) + g.shape, g.dtype) for g in gather),
        grid=(nq,),
        in_specs=[pl.BlockSpec((A_WIDTH, tq), lambda i: (0, i)), _full((s, A_KV_WIDTH)), _full((VTE_ROWS, s)),
                  _full((VTE_ROWS, s))] + [_ANY] * ng,
        out_specs=(_rows(tq, A_WIDTH), pl.BlockSpec((A_HEADS, tq), lambda i: (0, i))) + (_ANY,) * ng,
        scratch_shapes=[pltpu.VMEM((A_HEADS, A_KV_WIDTH, tq), BF16), pltpu.VMEM((A_HEADS, tq), F32),
                        pltpu.VMEM((A_HEADS, VTE_ROWS, tq), F32)] + (gather_sems(ng) if ng else []),
        compiler_params=_cp("arbitrary"), name="attn_fwd_gather" if ng else "attn_fwd")(q_t, kr, vte0, vte1, *gather)
    return out[0], out[1], list(out[2:])


def memkv_fwd(mem, g, w_kv):
    m, d = mem.shape

    def body(mem_ref, g_ref, w_ref, mn_ref, kv_ref):
        mf = mem_ref[...]
        r = lax.rsqrt(jnp.mean(mf * mf, axis=-1, keepdims=True) + EPS)
        mn = ((mf * r) * g_ref[...]).astype(BF16)
        mn_ref[...] = mn
        kv_ref[...] = _dot(mn, w_ref[...]).astype(BF16)

    return pl.pallas_call(
        body, out_shape=(_sds((m, d), BF16), _sds((m, 2 * M_WIDTH), BF16)),
        compiler_params=_cp(), name="memkv_fwd")(mem, g, w_kv)


def _layer_norm_stats(v):
    mu = jnp.mean(v, axis=-1, keepdims=True)
    xc = v - mu
    rstd = lax.rsqrt(jnp.mean(xc * xc, axis=-1, keepdims=True) + EPS)
    return xc * rstd, rstd


def _spatial_mix(vlb, ws_ref, bsb_ref, tm):
    rows = []
    for ci in range(tm // CHUNK):
        cols = []
        for g in range(B_GROUPS):
            blk = vlb[ci * CHUNK:(ci + 1) * CHUNK, g * B_GROUP_DIM:(g + 1) * B_GROUP_DIM]
            cols.append(_dot(ws_ref[g], blk) + bsb_ref[g])
        rows.append(jnp.concatenate(cols, axis=1))
    return jnp.concatenate(rows, axis=0)


def _mem_attn(qm, kv_ref):
    out = []
    for h in range(M_HEADS):
        qh = qm[:, h * M_HEAD_DIM:(h + 1) * M_HEAD_DIM].astype(BF16)
        kh = kv_ref[:, h * M_HEAD_DIM:(h + 1) * M_HEAD_DIM]
        vh = kv_ref[:, M_WIDTH + h * M_HEAD_DIM:M_WIDTH + (h + 1) * M_HEAD_DIM]
        sc = _dot_nt(qh, kh) * (M_HEAD_DIM ** -0.5)
        e = jnp.exp(sc - jnp.max(sc, axis=-1, keepdims=True))
        p = e / jnp.sum(e, axis=-1, keepdims=True)
        out.append((p, _dot(p.astype(BF16), vh)))
    return out


def branch_fwd(x, proj, o_a, kv, ws, bsb, ln_g, ln_b, w_br, w_out, next_g):
    s, d = x.shape
    tm = min(s, 512)

    def body(x_ref, p_ref, oa_ref, kv_ref, ws_ref, bsb_ref, lg_ref, lb_ref, wbr_ref, wo_ref, ng_ref,
             xn_ref, y_ref, up_ref, mg_ref, hn_ref):
        seg = lambda o, w: p_ref[:, o:o + w].astype(F32)
        z_a, u_b, v_b, z_b = seg(O_ZA, A_WIDTH), seg(O_UB, B_WIDTH), seg(O_VB, B_WIDTH), seg(O_ZB, B_WIDTH)
        q_m, z_m = seg(O_QM, M_WIDTH), seg(O_ZM, M_WIDTH)
        xhat, _ = _layer_norm_stats(v_b)
        vln = xhat * lg_ref[...] + lb_ref[...]
        mixed = _spatial_mix(vln.astype(BF16), ws_ref, bsb_ref, tm)
        y_b = (u_b * mixed) * (z_b * _sig(z_b))
        o_m = jnp.concatenate([o for _, o in _mem_attn(q_m, kv_ref)], axis=1)
        y_a = oa_ref[...] * (z_a * _sig(z_a))
        y_m = o_m * (z_m * _sig(z_m))
        merged = None
        for n, yy in enumerate((y_a, y_b, y_m)):
            yb = yy.astype(BF16)
            y_ref[n] = yb
            up = jnp.concatenate([_dot(yb, wbr_ref[c, n]) for c in range(N_CHIPS)], axis=1)
            up_ref[n] = up.astype(BF16)
            t = _sig(seg(O_LG + n * d, d)) * up
            merged = t if merged is None else merged + t
        mb = merged.astype(BF16)
        mg_ref[...] = mb
        xn = x_ref[...] + _dot(mb, wo_ref[...])
        xn_ref[...] = xn
        r = lax.rsqrt(jnp.mean(xn * xn, axis=-1, keepdims=True) + EPS)
        hn_ref[...] = ((xn * r) * ng_ref[...]).astype(BF16)

    return pl.pallas_call(
        body,
        out_shape=(_sds((s, d), F32), _sds((N_BRANCH, s, A_WIDTH), BF16), _sds((N_BRANCH, s, d), BF16), _sds((s, d), BF16),
                   _sds((s, d), BF16)),
        grid=(s // tm,),
        in_specs=[_rows(tm, d), _rows(tm, IN_WIDTH), _rows(tm, A_WIDTH), _full(kv.shape), _full(ws.shape), _full(bsb.shape),
                  _full((1, B_WIDTH)), _full((1, B_WIDTH)), _full(w_br.shape), _full(w_out.shape), _full((1, d))],
        out_specs=(_rows(tm, d), pl.BlockSpec((N_BRANCH, tm, A_WIDTH), lambda i: (0, i, 0)),
                   pl.BlockSpec((N_BRANCH, tm, d), lambda i: (0, i, 0)), _rows(tm, d), _rows(tm, d)),
        compiler_params=_cp("parallel"), name="branch_fwd")(x, proj, o_a, kv, ws, bsb, ln_g, ln_b, w_br, w_out, next_g)


def final_loss(x, fg, tgt):
    s, d = x.shape
    tm = min(s, 512)

    def body(x_ref, g_ref, t_ref, ls_ref, dx_ref, gg_ref):
        @pl.when(pl.program_id(0) == 0)
        def _():
            ls_ref[...] = jnp.zeros_like(ls_ref)
            gg_ref[...] = jnp.zeros_like(gg_ref)

        xf = x_ref[...]
        g = g_ref[...]
        r = lax.rsqrt(jnp.mean(xf * xf, axis=-1, keepdims=True) + EPS)
        xh = xf * r
        e = xh * g - t_ref[...]
        sq = jnp.sum(jnp.sum(e * e, axis=0, keepdims=True), axis=1, keepdims=True)
        ls_ref[...] += jnp.broadcast_to(sq, ls_ref.shape)
        dy = e * (1.0 / d)
        gg_ref[...] += jnp.sum(dy * xh, axis=0, keepdims=True)
        gy = dy * g
        dx_ref[...] = r * (gy - xh * jnp.mean(gy * xh, axis=-1, keepdims=True))

    return pl.pallas_call(
        body, out_shape=(_sds((1, LANES), F32), _sds((s, d), F32), _sds((1, d), F32)), grid=(s // tm,),
        in_specs=[_rows(tm, d), _full((1, d)), _rows(tm, d)],
        out_specs=(_full((1, LANES)), _rows(tm, d), _full((1, d))),
        compiler_params=_cp("arbitrary"), name="final_loss")(x, fg, tgt)


def _pblocks(tm, first, count):
    return [pl.BlockSpec((tm, PBLK), functools.partial(lambda i, b: (i, b), b=first + k)) for k in range(count)]


def merge_bwd(dx, proj, y, up, merged, w_br, w_out):
    s, d = dx.shape
    tm = min(s, 512)
    nlg = LG_W // PBLK
    cw = d // N_CHIPS

    def body(dx_ref, l0, l1, l2, l3, y_ref, up_ref, mg_ref, wbr_ref, wo_ref, dy_ref, dlg_ref, gwo_ref, gwb_ref, gwo16_ref, gwb16_ref):
        @pl.when(pl.program_id(0) == 0)
        def _():
            gwo_ref[...] = jnp.zeros_like(gwo_ref)
            gwb_ref[...] = jnp.zeros_like(gwb_ref)

        dxb = dx_ref[...].astype(BF16)
        dmg = _dot_nt(dxb, wo_ref[...])
        gwo_ref[...] += _dot_tn(mg_ref[...], dxb)
        lg = jnp.concatenate([l0[...], l1[...], l2[...], l3[...]], axis=1).astype(F32)
        for n in range(N_BRANCH):
            g = _sig(lg[:, n * d:(n + 1) * d])
            dup = dmg * g
            dlg_ref[:, n * d:(n + 1) * d] = ((dup * up_ref[n].astype(F32)) * (1.0 - g)).astype(BF16)
            dupb = dup.astype(BF16)
            dyn = None
            for c in range(N_CHIPS):
                blk = dupb[:, c * cw:(c + 1) * cw]
                gwb_ref[c, n] += _dot_tn(y_ref[n], blk)
                t = _dot_nt(blk, wbr_ref[c, n])
                dyn = t if dyn is None else dyn + t
            dy_ref[n] = dyn.astype(BF16)

        @pl.when(pl.program_id(0) == pl.num_programs(0) - 1)
        def _():
            gwo16_ref[...] = gwo_ref[...].astype(BF16)
            gwb16_ref[...] = gwb_ref[...].astype(BF16)

    return pl.pallas_call(
        body,
        out_shape=(_sds((N_BRANCH, s, A_WIDTH), BF16), _sds((s, LG_W), BF16), _sds((d, d), F32), _sds(w_br.shape, F32),
                   _sds((d, d), BF16), _sds(w_br.shape, BF16)),
        grid=(s // tm,),
        in_specs=[_rows(tm, d)] + _pblocks(tm, O_LG // PBLK, nlg) + [
            pl.BlockSpec((N_BRANCH, tm, A_WIDTH), lambda i: (0, i, 0)), pl.BlockSpec((N_BRANCH, tm, d), lambda i: (0, i, 0)),
            _rows(tm, d), _full(w_br.shape, once=True), _full(w_out.shape, once=True)],
        out_specs=(pl.BlockSpec((N_BRANCH, tm, A_WIDTH), lambda i: (0, i, 0)), _rows(tm, LG_W), _full((d, d)), _full(w_br.shape),
                   _full((d, d)), _full(w_br.shape)),
        compiler_params=_cp("arbitrary"), name="merge_bwd")(dx, proj, proj, proj, proj, y, up, merged, w_br, w_out)


def _dsilu(z, sg):
    return sg * (1.0 + z * (1.0 - sg))


def branch_bwd(dy, proj, o_a, kv, ws, ws_t, bsb, ln_g, ln_b, head_sel):
    s = proj.shape[0]
    tm = min(s, 512)
    nmid = MID_W // PBLK

    def body(dy_ref, m0, m1, m2, m3, oa_ref, kv_ref, ws_ref, wst_ref, bsb_ref, lg_ref, lb_ref, sel_ref,
             dmid_ref, dot_ref, dl_ref, gws_ref, gbs_ref, glg_ref, glb_ref, dkv_ref):
        @pl.when(pl.program_id(0) == 0)
        def _():
            for r in (gws_ref, gbs_ref, glg_ref, glb_ref, dkv_ref):
                r[...] = jnp.zeros_like(r)

        mid = jnp.concatenate([m0[...], m1[...], m2[...], m3[...]], axis=1).astype(F32)
        seg = lambda o, w: mid[:, o - O_ZA:o - O_ZA + w]
        z_a, u_b, v_b, z_b = seg(O_ZA, A_WIDTH), seg(O_UB, B_WIDTH), seg(O_VB, B_WIDTH), seg(O_ZB, B_WIDTH)
        q_m, z_m = seg(O_QM, M_WIDTH), seg(O_ZM, M_WIDTH)

        def put(o, v):
            dmid_ref[:, o - O_ZA:o - O_ZA + v.shape[1]] = v.astype(BF16)

        dy_a, dy_b, dy_m = dy_ref[0].astype(F32), dy_ref[1].astype(F32), dy_ref[2].astype(F32)

        o_a_ = oa_ref[...]
        sg = _sig(z_a)
        do_a = dy_a * (z_a * sg)
        put(O_ZA, (dy_a * o_a_) * _dsilu(z_a, sg))
        do_l = do_a * LN2
        dot_ref[...] = do_l.T.astype(BF16)
        dl_ref[...] = _dot_nt_hi(sel_ref[...], do_l * o_a_)

        xhat, rstd = _layer_norm_stats(v_b)
        lng = lg_ref[...]
        vln = xhat * lng + lb_ref[...]
        vlb = vln.astype(BF16)
        mixed = _spatial_mix(vlb, ws_ref, bsb_ref, tm)
        sg = _sig(z_b)
        sl = z_b * sg
        put(O_UB, (dy_b * mixed) * sl)
        put(O_ZB, ((dy_b * u_b) * mixed) * _dsilu(z_b, sg))
        dmix = (dy_b * u_b) * sl
        dmb = dmix.astype(BF16)
        rows = []
        for ci in range(tm // CHUNK):
            cols = []
            for g in range(B_GROUPS):
                rs, cs = slice(ci * CHUNK, (ci + 1) * CHUNK), slice(g * B_GROUP_DIM, (g + 1) * B_GROUP_DIM)
                gws_ref[g] += _dot_nt(dmb[rs, cs], vlb[rs, cs])
                gbs_ref[g] += jnp.broadcast_to(jnp.sum(dmix[rs, cs], axis=1, keepdims=True), (CHUNK, B_GROUP_DIM))
                cols.append(_dot(wst_ref[g], dmb[rs, cs]))
            rows.append(jnp.concatenate(cols, axis=1))
        dvln = jnp.concatenate(rows, axis=0)
        glg_ref[...] += jnp.sum(dvln * xhat, axis=0, keepdims=True)
        glb_ref[...] += jnp.sum(dvln, axis=0, keepdims=True)
        gy = dvln * lng
        put(O_VB, rstd * ((gy - jnp.mean(gy, axis=-1, keepdims=True)) - xhat * jnp.mean(gy * xhat, axis=-1, keepdims=True)))

        sg = _sig(z_m)
        sl = z_m * sg
        heads = _mem_attn(q_m, kv_ref)
        o_m = jnp.concatenate([o for _, o in heads], axis=1)
        put(O_ZM, (dy_m * o_m) * _dsilu(z_m, sg))
        do_m = dy_m * sl
        dqs = []
        for h, (p, o_h) in enumerate(heads):
            hs = slice(h * M_HEAD_DIM, (h + 1) * M_HEAD_DIM)
            vs = slice(M_WIDTH + h * M_HEAD_DIM, M_WIDTH + (h + 1) * M_HEAD_DIM)
            do_h = do_m[:, hs]
            dob = do_h.astype(BF16)
            dp = _dot_nt(dob, kv_ref[:, vs])
            dsc = (p * (dp - jnp.sum(do_h * o_h, axis=-1, keepdims=True))) * (M_HEAD_DIM ** -0.5)
            dsb = dsc.astype(BF16)
            dqs.append(_dot(dsb, kv_ref[:, hs]))
            dkv_ref[:, hs] += _dot_tn(dsb, q_m[:, hs].astype(BF16))
            dkv_ref[:, vs] += _dot_tn(p.astype(BF16), dob)
        put(O_QM, jnp.concatenate(dqs, axis=1))

    return pl.pallas_call(
        body,
        out_shape=(_sds((s, MID_W), BF16), _sds((A_WIDTH, s), BF16), _sds((A_HEADS, s), F32), _sds(ws.shape, F32),
                   _sds(ws.shape, F32), _sds((1, B_WIDTH), F32), _sds((1, B_WIDTH), F32), _sds(kv.shape, F32)),
        grid=(s // tm,),
        in_specs=[pl.BlockSpec((N_BRANCH, tm, A_WIDTH), lambda i: (0, i, 0))] + _pblocks(tm, O_ZA // PBLK, nmid) + [
            _rows(tm, A_WIDTH), _full(kv.shape), _full(ws.shape), _full(ws.shape), _full(bsb.shape),
            _full((1, B_WIDTH)), _full((1, B_WIDTH)), _full(head_sel.shape)],
        out_specs=(_rows(tm, MID_W), pl.BlockSpec((A_WIDTH, tm), lambda i: (0, i)), pl.BlockSpec((A_HEADS, tm), lambda i: (0, i)),
                   _full(ws.shape), _full(ws.shape), _full((1, B_WIDTH)), _full((1, B_WIDTH)), _full(kv.shape)),
        compiler_params=_cp("arbitrary"), name="branch_bwd")(dy, proj, proj, proj, proj, o_a, kv, ws, ws_t, bsb, ln_g, ln_b, head_sel)


def attn_bwd(q_t, do_t, kr, kr_t, vb, lse, delta, scatter=()):
    s = kr.shape[0]
    tq = min(s, 512)
    kc = min(s, 256)
    nkc = s // kc
    nq = s // tq
    grp = A_HEADS // A_KV_HEADS
    ns = len(scatter)
    na = ns // 2

    def body(qt_ref, dot_ref, kr_ref, krt_ref, vb_ref, lse_ref, dl_ref, *rest):
        s_in, (dqt_ref, dk_ref, dv_ref), s_out = rest[:ns], rest[ns:ns + 3], rest[ns + 3:2 * ns + 3]
        qp_ref, dop_ref, dq_ref = rest[2 * ns + 3:2 * ns + 6]
        if ns:
            start, finish = scatter_stages([g.shape[1:] for g in scatter[:na]], s_in[:na], s_in[na:], s_out[:na], s_out[na:],
                                           *rest[2 * ns + 6:])
            pl.when(pl.program_id(0) == 0)(start)

        @pl.when(pl.program_id(0) == 0)
        def _():
            dk_ref[...] = jnp.zeros_like(dk_ref)
            dv_ref[...] = jnp.zeros_like(dv_ref)

        for h in range(A_HEADS):
            hs = slice(A_HEAD_DIM * h, A_HEAD_DIM * (h + 1))
            qp_ref[h] = _pad_head(qt_ref[hs, :], h // grp)
            dop_ref[h] = _pad_head(dot_ref[hs, :], h // grp)
        dq_ref[...] = jnp.zeros_like(dq_ref)

        def step(ci, carry):
            ks = pl.ds(pl.multiple_of(ci * kc, kc), kc)
            kblk, vblk, ktb = kr_ref[ks, :], vb_ref[ks, :], krt_ref[:, ks]
            dv_acc = jnp.zeros((kc, A_KV_WIDTH), F32)
            dk_acc = jnp.zeros((kc, A_KV_WIDTH), F32)
            scs = [_dot(kblk, qp_ref[h]) for h in range(A_HEADS)]
            dps = [_dot(vblk, dop_ref[h]) for h in range(A_HEADS)]
            for h in range(A_HEADS):
                qpad, dopad = qp_ref[h], dop_ref[h]
                p = jnp.exp2(scs[h] - lse_ref[h:h + 1, :])
                dsb = (p * (dps[h] - dl_ref[h:h + 1, :])).astype(BF16)
                dv_acc = dv_acc + _dot_nt(p.astype(BF16), dopad)
                dk_acc = dk_acc + _dot_nt(dsb, qpad)
                dq_ref[h] += _dot(ktb, dsb)
            dv_ref[ks, :] += dv_acc
            dk_ref[ks, :] += dk_acc
            return carry

        lax.fori_loop(0, nkc, step, 0)
        dqt_ref[...] = jnp.concatenate(
            [dq_ref[h][A_HEAD_DIM * (h // grp):A_HEAD_DIM * (h // grp + 1), :] for h in range(A_HEADS)], axis=0)
        if ns:
            pl.when(pl.program_id(0) == nq - 1)(finish)

    colq = pl.BlockSpec((A_WIDTH, tq), lambda i: (0, i))
    colh = pl.BlockSpec((A_HEADS, tq), lambda i: (0, i))
    out = pl.pallas_call(
        body,
        out_shape=(_sds((A_WIDTH, s), F32), _sds((s, A_KV_WIDTH), F32), _sds((s, A_KV_WIDTH), F32)) + scatter_out_shapes(scatter[:na]),
        grid=(nq,),
        in_specs=[colq, colq, _full((s, A_KV_WIDTH)), _full((A_KV_WIDTH, s)), _full((s, A_KV_WIDTH)), colh, colh] + [_ANY] * ns,
        out_specs=(colq, _full((s, A_KV_WIDTH)), _full((s, A_KV_WIDTH))) + (_ANY,) * ns,
        scratch_shapes=[pltpu.VMEM((A_HEADS, A_KV_WIDTH, tq), BF16), pltpu.VMEM((A_HEADS, A_KV_WIDTH, tq), BF16),
                        pltpu.VMEM((A_HEADS, A_KV_WIDTH, tq), F32)] + (scatter_sems(na) if ns else []),
        compiler_params=_cp("arbitrary"), name="attn_bwd_scatter" if ns else "attn_bwd")(
            q_t, do_t, kr, kr_t, vb, lse, delta, *scatter)
    return out[0], out[1], out[2], list(out[3:3 + na]), list(out[3 + na:])


def qk_prep_bwd(proj, dq_t, dkr, dvb, tabs, qg, kg, gq, gk, fold_q, fold_k):
    s = proj.shape[0]
    tm = min(s, 512)
    c, sa, sb = tabs

    def head_norm_bwd(x, dn, gain, gones, fold):
        ms = _group_sum(x * x, gones) * (1.0 / A_HEAD_DIM)
        r = lax.rsqrt(ms + EPS)
        xh = x * r
        gg = _dot_hi(jnp.sum(dn * xh, axis=0, keepdims=True), fold)
        u = dn * gain
        mean_u = _group_sum(u * xh, gones) * (1.0 / A_HEAD_DIM)
        return r * (u - xh * mean_u), gg

    def body(p_ref, dqt_ref, dk_ref, dv_ref, c_ref, sa_ref, sb_ref, qg_ref, kg_ref, gq_ref, gk_ref, fq_ref, fk_ref,
             dqkv_ref, gqg_ref, gkg_ref):
        @pl.when(pl.program_id(0) == 0)
        def _():
            gqg_ref[...] = jnp.zeros_like(gqg_ref)
            gkg_ref[...] = jnp.zeros_like(gkg_ref)

        cc, ssa, ssb = c_ref[...], sa_ref[...], sb_ref[...]
        dqr = dqt_ref[...].T * Q_SCALE
        dqn = _rope_t(dqr, _tile4(cc), _tile4(ssa), _tile4(ssb))
        dxq, gq_ = head_norm_bwd(p_ref[:, O_QA:O_QA + A_WIDTH].astype(F32), dqn, qg_ref[...], gq_ref[...], fq_ref[...])
        dkn = _rope_t(dk_ref[...], cc, ssa, ssb)
        dxk, gk_ = head_norm_bwd(p_ref[:, O_KA:O_KA + A_KV_WIDTH].astype(F32), dkn, kg_ref[...], gk_ref[...], fk_ref[...])
        gqg_ref[...] += gq_
        gkg_ref[...] += gk_
        dqkv_ref[:, O_QA:O_QA + A_WIDTH] = dxq.astype(BF16)
        dqkv_ref[:, O_KA:O_KA + A_KV_WIDTH] = dxk.astype(BF16)
        dqkv_ref[:, O_VA:O_VA + A_KV_WIDTH] = (dv_ref[...] * (1.0 / LN2)).astype(BF16)

    tab = _rows(tm, LANES)
    return pl.pallas_call(
        body, out_shape=(_sds((s, PBLK), BF16), _sds((1, LANES), F32), _sds((1, LANES), F32)), grid=(s // tm,),
        in_specs=[_rows(tm, PBLK), pl.BlockSpec((A_WIDTH, tm), lambda i: (0, i)), _rows(tm, A_KV_WIDTH), _rows(tm, A_KV_WIDTH),
                  tab, tab, tab, _full((1, A_WIDTH)), _full((1, A_KV_WIDTH)), _full((A_WIDTH, A_WIDTH)),
                  _full((A_KV_WIDTH, A_KV_WIDTH)), _full((A_WIDTH, LANES)), _full((A_KV_WIDTH, LANES))],
        out_specs=(_rows(tm, PBLK), _full((1, LANES)), _full((1, LANES))),
        compiler_params=_cp("arbitrary"), name="qk_prep_bwd")(proj, dq_t, dkr, dvb, c, sa, sb, qg, kg, gq, gk, fold_q, fold_k)


def _pick_dproj(b, d0, d1, d2, use):
    first_lg = 1 + MID_W // PBLK

    @pl.when(b == 0)
    def _():
        use(d0[...])

    @pl.when(jnp.logical_and(b >= 1, b < first_lg))
    def _():
        use(d1[...])

    @pl.when(b >= first_lg)
    def _():
        use(d2[...])


def win_grad(d0, d1, d2, h):
    s, d = h.shape
    tk = min(s, 2048)
    nk = s // tk

    def body(d0_ref, d1_ref, d2_ref, h_ref, o_ref, o16_ref):
        @pl.when(pl.program_id(1) == 0)
        def _():
            o_ref[...] = jnp.zeros_like(o_ref)

        def use(blk):
            o_ref[...] += _dot_tn(blk, h_ref[...])

        _pick_dproj(pl.program_id(0), d0_ref, d1_ref, d2_ref, use)

        @pl.when(pl.program_id(1) == nk - 1)
        def _():
            o16_ref[...] = o_ref[...].astype(BF16)

    def spec(first, count):
        def imap(j, k):
            used = jnp.logical_and(j >= first, j < first + count)
            return (jnp.where(used, k, 0), jnp.clip(j - first, 0, count - 1))
        return pl.BlockSpec((tk, PBLK), imap)

    nm = MID_W // PBLK
    oblk = pl.BlockSpec((PBLK, d), lambda j, k: (j, 0))
    return pl.pallas_call(
        body, out_shape=(_sds((IN_WIDTH, d), F32), _sds((IN_WIDTH, d), BF16)), grid=(N_PBLK, nk),
        in_specs=[spec(0, 1), spec(1, nm), spec(1 + nm, LG_W // PBLK), pl.BlockSpec((tk, d), lambda j, k: (k, 0))],
        out_specs=(oblk, oblk),
        compiler_params=_cp("parallel", "arbitrary"), name="win_grad")(d0, d1, d2, h)


def h_bwd(d0, d1, d2, w_t, x, dx_out, g, scatter=()):
    s, d = x.shape
    tm = min(s, 512)
    nt = s // tm
    ns = len(scatter)
    na = ns // 2

    def body(d0_ref, d1_ref, d2_ref, w_ref, x_ref, dxo_ref, g_ref, *rest):
        s_in, (dx_ref, gg_ref), s_out = rest[:ns], rest[ns:ns + 2], rest[ns + 2:2 * ns + 2]
        if ns:
            start, finish = scatter_stages([a.shape[1:] for a in scatter[:na]], s_in[:na], s_in[na:], s_out[:na], s_out[na:],
                                           *rest[2 * ns + 2:])
            pl.when(pl.program_id(0) == 0)(start)

        @pl.when(pl.program_id(0) == 0)
        def _():
            gg_ref[...] = jnp.zeros_like(gg_ref)

        dh = (_dot(d0_ref[...], w_ref[0:PBLK, :]) + _dot(d1_ref[...], w_ref[PBLK:PBLK + MID_W, :])
              + _dot(d2_ref[...], w_ref[PBLK + MID_W:, :]))
        xf = x_ref[...]
        r = lax.rsqrt(jnp.mean(xf * xf, axis=-1, keepdims=True) + EPS)
        xh = xf * r
        gg_ref[...] += jnp.sum(dh * xh, axis=0, keepdims=True)
        u = dh * g_ref[...]
        dx_ref[...] = dxo_ref[...] + r * (u - xh * jnp.mean(u * xh, axis=-1, keepdims=True))
        if ns:
            pl.when(pl.program_id(0) == nt - 1)(finish)

    rowb = _rows(tm, d)
    out = pl.pallas_call(
        body, out_shape=(_sds((s, d), F32), _sds((1, d), F32)) + scatter_out_shapes(scatter[:na]), grid=(nt,),
        in_specs=[_rows(tm, PBLK), _rows(tm, MID_W), _rows(tm, LG_W),
                  pl.BlockSpec(w_t.shape, lambda i: (0, 0), pipeline_mode=pl.Buffered(1)), rowb, rowb, _full((1, d))] + [_ANY] * ns,
        out_specs=(rowb, _full((1, d))) + (_ANY,) * ns,
        scratch_shapes=scatter_sems(na) if ns else [],
        compiler_params=_cp("arbitrary"), name="h_bwd_scatter" if ns else "h_bwd")(d0, d1, d2, w_t, x, dx_out, g, *scatter)
    return out[0], out[1], list(out[2:2 + na]), list(out[2 + na:])


def memkv_bwd(mem, g, mem_n, w_kv, dkv):
    m, d = mem.shape

    def body(mem_ref, g_ref, mn_ref, w_ref, dkv_ref, gw_ref, gw16_ref, gg_ref):
        dkb = dkv_ref[...].astype(BF16)
        gw = _dot_tn(mn_ref[...], dkb)
        gw_ref[...] = gw
        gw16_ref[...] = gw.astype(BF16)
        dmn = _dot_nt(dkb, w_ref[...])
        mf = mem_ref[...]
        r = lax.rsqrt(jnp.mean(mf * mf, axis=-1, keepdims=True) + EPS)
        gg_ref[...] = jnp.sum(dmn * (mf * r), axis=0, keepdims=True)

    return pl.pallas_call(
        body, out_shape=(_sds(w_kv.shape, F32), _sds(w_kv.shape, BF16), _sds((1, d), F32)),
        compiler_params=_cp(), name="memkv_bwd")(mem, g, mem_n, w_kv, dkv)


def _layer_consts(seq):
    i = jnp.arange(A_WIDTH)
    return dict(
        tabs=rope_tables(seq),
        gq=_group_ones(A_WIDTH, A_HEAD_DIM).astype(BF16), gk=_group_ones(A_KV_WIDTH, A_HEAD_DIM).astype(BF16),
        fold_q=(i[:, None] % A_HEAD_DIM == jnp.arange(LANES)[None, :]).astype(F32),
        fold_k=(i[:A_KV_WIDTH, None] % A_HEAD_DIM == jnp.arange(LANES)[None, :]).astype(F32),
        head_sel=(jnp.arange(A_HEADS)[:, None] == i[None, :] // A_HEAD_DIM).astype(F32),
    )


_BIG = ("win_t", "wkv", "wbr", "wout")


def _with_own_part(names, gathered, shards, chip, d):
    shape = dict(win_t=(IN_WIDTH, d), wkv=(d, 2 * M_WIDTH), wbr=(N_CHIPS, N_BRANCH, A_WIDTH, d // N_CHIPS), wout=(d, d))
    return {n: lax.dynamic_update_slice(g, sh[None], (chip, 0, 0)).reshape(shape[n]) for n, g, sh in zip(names, gathered, shards)}


def local_fwd_bwd(x, mem, tgt, small, big=None, shards=None, place=None):
    s, d = x.shape
    depth = small["norm_g"].shape[0]
    k = _layer_consts(s)
    row = lambda v: v.reshape(1, -1)
    dist = shards is not None
    if dist:
        big = [_with_own_part(_BIG[:1], allgather_layer(shards[0][:1]), shards[0][:1], place[0], d)] + [None] * (depth - 1)
    saved = []
    for l in range(depth):
        ng = row(small["norm_g"][l])
        qg = row(jnp.tile(small["q_norm_g"][l], A_HEADS))
        kg = row(jnp.tile(small["k_norm_g"][l], A_KV_HEADS))
        ws = small["w_s"][l].astype(BF16)
        ws_t = jnp.swapaxes(small["w_s"][l], 1, 2).astype(BF16)
        bsb = jnp.broadcast_to(small["b_s"][l][:, :, None], (B_GROUPS, CHUNK, B_GROUP_DIM))
        lng, lnb = row(small["sg_ln_g"][l]), row(small["sg_ln_b"][l])
        mg = row(small["mem_norm_g"][l])
        w = big[l]
        h = rms_fwd(x, ng) if l == 0 else h_next
        proj = proj_fwd(h, w["win_t"])
        q_t, kr, vb, kr_t, vte0, vte1 = qk_prep(proj, k["tabs"], qg, kg, k["gq"], k["gk"])
        late = list(shards[0][1:]) if dist and l == 0 else []
        nxt = list(shards[l + 1]) if dist and l + 1 < depth else []
        o_a, lse, gathered = attn_fwd(q_t, kr, vte0, vte1, gather=tuple(late + nxt))
        if late:
            w.update(_with_own_part(_BIG[1:], gathered[:len(late)], late, place[0], d))
        if nxt:
            big[l + 1] = _with_own_part(_BIG, gathered[len(late):], nxt, place[0], d)
        mem_n, kv = memkv_fwd(mem, mg, w["wkv"])
        next_g = row(small["norm_g"][l + 1]) if l + 1 < depth else row(small["final_g"])
        x_next, y, up, merged, h_next = branch_fwd(x, proj, o_a, kv, ws, bsb, lng, lnb, w["wbr"], w["wout"], next_g)
        saved.append(dict(x=x, ng=ng, qg=qg, kg=kg, ws=ws, ws_t=ws_t, bsb=bsb, lng=lng, lnb=lnb, mg=mg, h=h, proj=proj,
                          q_t=q_t, kr=kr, kr_t=kr_t, vb=vb, o_a=o_a, lse=lse, mem_n=mem_n, kv=kv, y=y, up=up, merged=merged))
        x = x_next

    sq, dx, g_final = final_loss(x, row(small["final_g"]), tgt)
    grads = {n: [None] * depth for n in ("norm_g", "q_norm_g", "k_norm_g", "sg_ln_g", "sg_ln_b", "w_s", "b_s", "mem_norm_g")}
    parts = lambda g: g.reshape(N_CHIPS, -1, g.shape[-1])
    reduced = [[None] * len(_BIG) for _ in range(depth)]

    def reduce_all(items, t_sib, t_rem):
        if items:
            for (ll, a, _, _), f in zip(items, reduce_rows(place, [i[2] for i in items], t_sib, t_rem)):
                reduced[ll][a] = f

    as_scatter = lambda items: tuple(i[2] for i in items) + tuple(i[3] for i in items)
    pending = []
    for l in reversed(range(depth)):
        sv, w = saved[l], big[l]
        dy, dlg, g_wout, g_wbr, g_wout16, g_wbr16 = merge_bwd(dx, sv["proj"], sv["y"], sv["up"], sv["merged"], w["wbr"], w["wout"])
        dmid, do_t, delta, g_ws, g_bs, g_lng, g_lnb, dkv = branch_bwd(
            dy, sv["proj"], sv["o_a"], sv["kv"], sv["ws"], sv["ws_t"], sv["bsb"], sv["lng"], sv["lnb"], k["head_sel"])
        g_wkv, g_wkv16, g_mg = memkv_bwd(mem, sv["mg"], sv["mem_n"], w["wkv"], dkv)
        if dist:
            pending += [(l, 1, parts(g_wkv), parts(g_wkv16)), (l, 2, parts(g_wbr), parts(g_wbr16)), (l, 3, parts(g_wout), parts(g_wout16))]
        dq_t, dkr, dvb, t_sib, t_rem = attn_bwd(sv["q_t"], do_t, sv["kr"], sv["kr_t"], sv["vb"], sv["lse"], delta,
                                                scatter=as_scatter(pending))
        reduce_all(pending, t_sib, t_rem)
        dqkv, g_qg, g_kg = qk_prep_bwd(sv["proj"], dq_t, dkr, dvb, k["tabs"], sv["qg"], sv["kg"], k["gq"], k["gk"],
                                       k["fold_q"], k["fold_k"])
        g_win, g_win16 = win_grad(dqkv, dmid, dlg, sv["h"])
        pending = [(l, 0, parts(g_win), parts(g_win16))] if dist else []
        last = as_scatter(pending) if l == 0 else ()
        dx, g_ng, t_sib, t_rem = h_bwd(dqkv, dmid, dlg, w["win_t"], sv["x"], dx, sv["ng"], scatter=last)
        if last:
            reduce_all(pending, t_sib, t_rem)
        grads["norm_g"][l] = g_ng[0]
        grads["q_norm_g"][l] = g_qg[0, :A_HEAD_DIM]
        grads["k_norm_g"][l] = g_kg[0, :A_HEAD_DIM]
        grads["sg_ln_g"][l] = g_lng[0]
        grads["sg_ln_b"][l] = g_lnb[0]
        grads["w_s"][l] = g_ws
        grads["b_s"][l] = g_bs[:, :, 0]
        grads["mem_norm_g"][l] = g_mg[0]
        if not dist:
            reduced[l] = dict(zip(_BIG, (parts(g_win), parts(g_wkv), parts(g_wbr), parts(g_wout))))
    grads = {n: jnp.stack(v) for n, v in grads.items()}
    grads["final_g"] = g_final[0]
    return sq[0, 0], dx, grads, reduced


def _row_block(rows, width, cap_bytes=2 * 2**20):
    best = None
    for br in range(8, rows + 1, 8):
        if rows % br == 0 and br * width * 4 <= cap_bytes:
            best = br
    return best if best is not None else rows


def adamw(w, gs, m, v):
    r, c = w.shape
    n = len(gs)
    rs = r // n
    br = _row_block(rs, c)
    nb = rs // br

    def body(w_ref, *refs):
        g_refs, (m_ref, v_ref, og_ref, d_ref, nm_ref, nv_ref) = refs[:n], refs[n:]

        def update(gg):
            mm = ADAM_B1 * m_ref[...] + (1.0 - ADAM_B1) * gg
            vv = ADAM_B2 * v_ref[...] + (1.0 - ADAM_B2) * (gg * gg)
            m_hat = mm / (1.0 - ADAM_B1 ** ADAM_STEP)
            v_hat = vv / (1.0 - ADAM_B2 ** ADAM_STEP)
            og_ref[...] = gg
            d_ref[...] = -ADAM_LR * (m_hat / (jnp.sqrt(v_hat) + ADAM_EPS) + ADAM_WD * w_ref[...])
            nm_ref[...] = mm
            nv_ref[...] = vv

        for k in range(n):
            pl.when(pl.program_id(0) == k)(functools.partial(lambda k: update(g_refs[k][...]), k))

    blk = pl.BlockSpec((br, c), lambda l, i: (l * nb + i, 0))
    g_specs = [pl.BlockSpec((br, c), functools.partial(lambda l, i, k: (jnp.where(l == k, i, 0), 0), k=k)) for k in range(n)]
    return pl.pallas_call(
        body, out_shape=(_sds((r, c), F32),) * 4, grid=(n, nb), in_specs=[blk] + g_specs + [blk, blk], out_specs=(blk,) * 4,
        compiler_params=_cp("arbitrary", "arbitrary"), name="adamw")(w, *gs, m, v)


N_REMOTE = 2 * (N_CHIPS - 1)


def reduce_rows(place, gs, t_sibs, t_rems):
    n = len(gs)
    nt = 2

    def body(place_ref, *refs):
        for a in range(n):
            g_ref, s_ref, t_ref, f_ref = refs[a], refs[n + a], refs[2 * n + a], refs[3 * n + a]
            acc = g_ref[...] + s_ref[...]
            for j in range(N_REMOTE):
                acc = acc + t_ref[j].astype(F32)
            f_ref[...] = acc

    tiles = [(g.shape[1] // 2 // nt, g.shape[2]) for g in gs]
    return pl.pallas_call(
        body, out_shape=tuple(_sds(g.shape[1:], F32) for g in gs),
        grid_spec=pltpu.PrefetchScalarGridSpec(
            num_scalar_prefetch=1, grid=(nt,),
            in_specs=[pl.BlockSpec((None, tr, c), lambda i, p: (p[0], p[1] * nt + i, 0)) for tr, c in tiles]
            + [pl.BlockSpec((tr, c), lambda i, p: (i, 0)) for tr, c in tiles]
            + [pl.BlockSpec((N_REMOTE, tr, c), lambda i, p: (0, i, 0)) for tr, c in tiles],
            out_specs=tuple(pl.BlockSpec((tr, c), lambda i, p: (p[1] * nt + i, 0)) for tr, c in tiles)),
        compiler_params=_cp("parallel"), name="reduce_rows")(place, *gs, *t_sibs, *t_rems)


_ANY = pl.BlockSpec(memory_space=pl.ANY)


def _place():
    x, y, c = lax.axis_index("x"), lax.axis_index("y"), lax.axis_index("c")
    chips = [(1 - x, y), (x, 1 - y), (1 - x, 1 - y)]
    return x, y, c, chips


def gather_sems(n):
    return [pltpu.SemaphoreType.DMA((n, N_REMOTE)), pltpu.SemaphoreType.DMA((n, N_REMOTE))]


def gather_stages(shapes, ins, outs, send, recv):
    n = len(shapes)
    x, y, c, chips = _place()
    me = 2 * x + y
    sib = (x, y, 1 - c)

    def rows(a, hl):
        r2 = shapes[a][0] // 2
        return pl.ds(hl * r2, r2)

    def remote(a, k, src, dst, dev):
        return pltpu.make_async_remote_copy(src, dst, send.at[a, k], recv.at[a, k], device_id=dev, device_id_type=MESH)

    def sent(a, k):
        cx, cy = chips[k]
        return remote(a, k, ins[a].at[rows(a, c)], outs[a].at[me, rows(a, c)], (cx, cy, c))

    def passed(a, k, hl):
        cx, cy = chips[k]
        got = outs[a].at[2 * cx + cy, rows(a, hl)]
        return remote(a, k, got, got, (cx, cy, c)), remote(a, 3 + k, got, got, sib)

    def start():
        for a in range(n):
            for k in range(3):
                sent(a, k).start()

    def forward():
        for k in range(3):
            for a in range(n):
                arrived, on = passed(a, k, c)
                arrived.wait_recv()
                on.start()

    def finish():
        for k in range(3):
            for a in range(n):
                passed(a, k, 1 - c)[1].wait_recv()
        for k in range(3):
            for a in range(n):
                sent(a, k).wait_send()
                passed(a, k, c)[1].wait_send()

    return start, forward, finish


def allgather_layer(shards):
    n = len(shards)

    def body(*refs):
        for stage in gather_stages([a.shape for a in shards], refs[:n], refs[n:2 * n], *refs[2 * n:]):
            stage()

    return pl.pallas_call(
        body, out_shape=tuple(_sds((N_CHIPS,) + a.shape, a.dtype) for a in shards),
        in_specs=[_ANY] * n, out_specs=(_ANY,) * n, scratch_shapes=gather_sems(n), name="allgather_layer")(*shards)


def scatter_sems(n):
    return [pltpu.SemaphoreType.DMA((n, N_REMOTE + 1)), pltpu.SemaphoreType.DMA((n, N_REMOTE + 1))]


def scatter_out_shapes(gs):
    return (tuple(_sds((g.shape[1] // 2, g.shape[2]), F32) for g in gs)
            + tuple(_sds((N_REMOTE, g.shape[1] // 2, g.shape[2]), BF16) for g in gs))


def scatter_stages(shapes, gf, gb, t_sib, t_rem, send, recv):
    n = len(shapes)
    x, y, c, chips = _place()
    me = 2 * x + y

    def copies():
        out = []
        for a in range(n):
            r2 = shapes[a][0] // 2
            out.append(pltpu.make_async_remote_copy(gf[a].at[me, pl.ds((1 - c) * r2, r2)], t_sib[a], send.at[a, N_REMOTE],
                                                    recv.at[a, N_REMOTE], device_id=(x, y, 1 - c), device_id_type=MESH))
            for k, (cx, cy) in enumerate(chips):
                for o in range(2):
                    tc = c if o == 0 else 1 - c
                    out.append(pltpu.make_async_remote_copy(gb[a].at[2 * cx + cy, pl.ds(tc * r2, r2)], t_rem[a].at[2 * k + o],
                                                            send.at[a, 2 * k + o], recv.at[a, 2 * k + o],
                                                            device_id=(cx, cy, tc), device_id_type=MESH))
        return out

    def start():
        for cp in copies():
            cp.start()

    def finish():
        for cp in copies():
            cp.wait()

    return start, finish


def finish_exchange(v, fs):
    n = len(fs)
    r, w = v.shape
    ndev = 2 * N_CHIPS

    def body(v_ref, *refs):
        out, sum_ref = refs[n:2 * n], refs[2 * n]
        all_ref, send, recv, loc, fsend, frecv = refs[2 * n + 1:]
        x, y, c, chips = _place()
        me, sib = (x, y, c), (x, y, 1 - c)
        swaps = []
        for a in range(n):
            r2 = fs[a].shape[0] // 2
            mine = out[a].at[pl.ds(c * r2, r2)]
            cp = pltpu.make_async_remote_copy(mine, mine, fsend.at[a], frecv.at[a], device_id=sib, device_id_type=MESH)
            cp.start()
            swaps.append(cp)

        def slab(px, py, pc):
            return all_ref.at[4 * px + 2 * py + pc]

        def copy(k, block, to, src=None):
            return pltpu.make_async_remote_copy(slab(*block) if src is None else src, slab(*block), send.at[k], recv.at[k],
                                                device_id=to, device_id_type=MESH)

        mine = pltpu.make_async_copy(v_ref, slab(*me), loc)
        mine.start()
        first = [copy(0, me, sib, src=v_ref)] + [copy(1 + j, me, (*chip, c), src=v_ref) for j, chip in enumerate(chips)]
        for cp in first:
            cp.start()
        passed = [copy(4 + j, (*chip, c), sib) for j, chip in enumerate(chips)]
        for j, chip in enumerate(chips):
            copy(1 + j, (*chip, c), me).wait_recv()
            passed[j].start()
        copy(0, sib, me).wait_recv()
        for j, chip in enumerate(chips):
            copy(4 + j, (*chip, 1 - c), me).wait_recv()
        for cp in first + passed:
            cp.wait_send()
        mine.wait()
        acc = all_ref[0]
        for i in range(1, ndev):
            acc = acc + all_ref[i]
        sum_ref[...] = acc
        for a, cp in enumerate(swaps):
            r2 = fs[a].shape[0] // 2
            theirs = out[a].at[pl.ds((1 - c) * r2, r2)]
            cp.wait_send()
            pltpu.make_async_remote_copy(theirs, theirs, fsend.at[a], frecv.at[a], device_id=sib, device_id_type=MESH).wait_recv()

    vm = pl.BlockSpec(memory_space=pltpu.VMEM)
    res = pl.pallas_call(
        body, out_shape=tuple(_sds(f.shape, F32) for f in fs) + (_sds((r, w), F32),),
        in_specs=[vm] + [_ANY] * n, out_specs=(_ANY,) * n + (vm,), input_output_aliases={a + 1: a for a in range(n)},
        scratch_shapes=[pltpu.VMEM((ndev, r, w), F32), pltpu.SemaphoreType.DMA((7,)), pltpu.SemaphoreType.DMA((7,)),
                        pltpu.SemaphoreType.DMA, pltpu.SemaphoreType.DMA((n,)), pltpu.SemaphoreType.DMA((n,))],
        compiler_params=pltpu.CompilerParams(vmem_limit_bytes=VMEM_LIMIT), name="finish_exchange")(v, *fs)
    return res[n], list(res[:n])


_SMALL = ("norm_g", "q_norm_g", "k_norm_g", "sg_ln_g", "sg_ln_b", "w_s", "b_s", "mem_norm_g", "final_g")
_WEIGHTS = ("norm_g", "w_in", "q_norm_g", "k_norm_g", "sg_ln_g", "sg_ln_b", "w_s", "b_s", "mem_norm_g", "w_mem_kv", "w_br",
            "w_out", "final_g")


def _pack(d):
    flat = jnp.concatenate([d[n].reshape(-1) for n in _SMALL])
    rows = -(-flat.shape[0] // (8 * LANES)) * 8
    return jnp.pad(flat, (0, rows * LANES - flat.shape[0])).reshape(rows, LANES)


def _unpack(p, like):
    flat, out, o = p.reshape(-1), {}, 0
    for n in _SMALL:
        out[n] = flat[o:o + like[n].size].reshape(like[n].shape)
        o += like[n].size
    return out


def kernel(x, mem, norm_g, w_in, q_norm_g, k_norm_g, sg_ln_g, sg_ln_b, w_s, b_s, mem_norm_g, w_mem_kv, w_br, w_out, final_g, loss_target, m_norm_g, m_w_in, m_q_norm_g, m_k_norm_g, m_sg_ln_g, m_sg_ln_b, m_w_s, m_b_s, m_mem_norm_g, m_w_mem_kv, m_w_br, m_w_out, m_final_g, v_norm_g, v_w_in, v_q_norm_g, v_k_norm_g, v_sg_ln_g, v_sg_ln_b, v_w_s, v_b_s, v_mem_norm_g, v_w_mem_kv, v_w_br, v_w_out, v_final_g):
    w = dict(norm_g=norm_g, w_in=w_in, q_norm_g=q_norm_g, k_norm_g=k_norm_g, sg_ln_g=sg_ln_g, sg_ln_b=sg_ln_b, w_s=w_s, b_s=b_s,
             mem_norm_g=mem_norm_g, w_mem_kv=w_mem_kv, w_br=w_br, w_out=w_out, final_g=final_g)
    m = dict(norm_g=m_norm_g, w_in=m_w_in, q_norm_g=m_q_norm_g, k_norm_g=m_k_norm_g, sg_ln_g=m_sg_ln_g, sg_ln_b=m_sg_ln_b,
             w_s=m_w_s, b_s=m_b_s, mem_norm_g=m_mem_norm_g, w_mem_kv=m_w_mem_kv, w_br=m_w_br, w_out=m_w_out, final_g=m_final_g)
    v = dict(norm_g=v_norm_g, w_in=v_w_in, q_norm_g=v_q_norm_g, k_norm_g=v_k_norm_g, sg_ln_g=v_sg_ln_g, sg_ln_b=v_sg_ln_b,
             w_s=v_w_s, b_s=v_b_s, mem_norm_g=v_mem_norm_g, w_mem_kv=v_w_mem_kv, w_br=v_w_br, w_out=v_w_out, final_g=v_final_g)
    depth, d = norm_g.shape
    nsh = N_CHIPS
    br_rows = N_BRANCH * A_WIDTH
    br_cols = d // nsh

    shards = [[jnp.swapaxes(w_in[l], 0, 1).astype(BF16), w_mem_kv[l].astype(BF16), w_br[l].astype(BF16).reshape(br_rows, br_cols),
               w_out[l].astype(BF16)] for l in range(depth)]
    place = jnp.stack([2 * lax.axis_index("x") + lax.axis_index("y"), lax.axis_index("c")]).astype(jnp.int32)
    small = {n: w[n] for n in _SMALL}

    sq, dx, grads, reduced = local_fwd_bwd(x[0], mem[0], loss_target[0], small, shards=shards, place=place)
    loss = (0.5 / d) * lax.psum(sq, ("x", "y", "c"))

    small_sum, finals = finish_exchange(_pack(grads), [g for layer in reduced for g in layer])
    big_grads = dict(zip(("w_in", "w_mem_kv", "w_br", "w_out"), [finals[a::len(_BIG)] for a in range(len(_BIG))]))
    small_grads = _unpack(small_sum, small)

    out_g, out_d, out_m, out_v = {}, {}, {}, {}
    _, sd, sm, sv = adamw(_pack(small), [small_sum], _pack({n: m[n] for n in _SMALL}), _pack({n: v[n] for n in _SMALL}))
    sd, sm, sv = _unpack(sd, small), _unpack(sm, small), _unpack(sv, small)
    for n in _SMALL:
        out_g[n], out_d[n], out_m[n], out_v[n] = small_grads[n], sd[n], sm[n], sv[n]
    for n, gs in big_grads.items():
        into = (lambda a: jnp.swapaxes(a, 1, 2)) if n == "w_in" else (lambda a: a)
        two_d = lambda a: a.reshape(-1, gs[0].shape[-1])
        res = adamw(two_d(into(w[n])), gs, two_d(into(m[n])), two_d(into(v[n])))
        out_g[n], out_d[n], out_m[n], out_v[n] = [into(t.reshape(into(w[n]).shape)) for t in res]
    return (loss, dx[None], *[out_g[n] for n in _WEIGHTS], *[out_d[n] for n in _WEIGHTS], *[out_m[n] for n in _WEIGHTS],
            *[out_v[n] for n in _WEIGHTS])
```

```python
import functools

import jax
import jax.numpy as jnp
from jax import lax
from jax.experimental import pallas as pl
from jax.experimental.pallas import tpu as pltpu

F32 = jnp.float32
BF16 = jnp.bfloat16

D_MODEL = 1024
GRID_W = 64
CHUNK = 128
ROPE_THETA = 10000.0
EPS = 1e-6
A_HEADS, A_KV_HEADS, A_HEAD_DIM = 8, 2, 64
A_WIDTH, A_KV_WIDTH = 512, 128
B_GROUPS, B_GROUP_DIM, B_WIDTH = 4, 128, 512
M_HEADS, M_HEAD_DIM, M_WIDTH = 4, 128, 512
N_BRANCH = 3
IN_WIDTH = 6912
O_QA, O_KA, O_VA, O_ZA, O_UB, O_VB, O_ZB, O_QM, O_ZM, O_LG = 0, 512, 640, 768, 1280, 1792, 2304, 2816, 3328, 3840
PBLK = 768
N_PBLK = IN_WIDTH // PBLK
MID_W = 3072
LG_W = 3072

LN2 = 0.6931471805599453
Q_SCALE = A_HEAD_DIM ** -0.5 / LN2
VTE_ROWS = A_HEAD_DIM + 16

ADAM_LR, ADAM_B1, ADAM_B2, ADAM_EPS, ADAM_WD, ADAM_STEP = 0.001, 0.9, 0.999, 1e-08, 0.01, 10

V7X_VMEM_BYTES = 64 * 2**20
VMEM_LIMIT = V7X_VMEM_BYTES - 4 * 2**20
LANES = 128
MESH = pl.DeviceIdType.MESH
N_CHIPS = 4


def _cp(*sem):
    return pltpu.CompilerParams(dimension_semantics=sem if sem else None, vmem_limit_bytes=VMEM_LIMIT)


def _dot(a, b):
    return jnp.dot(a, b, preferred_element_type=F32)


def _dot_nt(a, b):
    return lax.dot_general(a, b, (((1,), (1,)), ((), ())), preferred_element_type=F32)


def _dot_tn(a, b):
    return lax.dot_general(a, b, (((0,), (0,)), ((), ())), preferred_element_type=F32)


def _dot_hi(a, b):
    return jnp.dot(a, b, preferred_element_type=F32, precision=lax.Precision.HIGHEST)


def _group_sum(a, ones):
    hi = a.astype(BF16)
    lo = (a - hi.astype(F32)).astype(BF16)
    return _dot(hi, ones) + _dot(lo, ones)


def _dot_nt_hi(a, b):
    return lax.dot_general(a, b, (((1,), (1,)), ((), ())), preferred_element_type=F32, precision=lax.Precision.HIGHEST)


def _sig(z):
    return 1.0 / (1.0 + jnp.exp(-z))


def _full(shape, once=False):
    nd = len(shape)
    return pl.BlockSpec(shape, lambda *_: (0,) * nd, pipeline_mode=pl.Buffered(1) if once else None)


def _rows(tm, width):
    return pl.BlockSpec((tm, width), lambda i: (i, 0))


def _sds(shape, dtype):
    return jax.ShapeDtypeStruct(shape, dtype)


def rms_fwd(x, g):
    s, d = x.shape
    tm = min(s, 512)

    def body(x_ref, g_ref, h_ref):
        xf = x_ref[...]
        r = lax.rsqrt(jnp.mean(xf * xf, axis=-1, keepdims=True) + EPS)
        h_ref[...] = ((xf * r) * g_ref[...]).astype(BF16)

    return pl.pallas_call(
        body, out_shape=_sds((s, d), BF16), grid=(s // tm,),
        in_specs=[_rows(tm, d), _full((1, d))], out_specs=_rows(tm, d),
        compiler_params=_cp("parallel"), name="rms_fwd")(x, g)


def proj_fwd(h, w_t):
    s, d = h.shape
    n = w_t.shape[0]
    tm = min(s, 1024)
    tn = 2304

    def body(h_ref, w_ref, o_ref):
        o_ref[...] = _dot_nt(h_ref[...], w_ref[...]).astype(BF16)

    return pl.pallas_call(
        body, out_shape=_sds((s, n), BF16), grid=(n // tn, s // tm),
        in_specs=[pl.BlockSpec((tm, d), lambda j, i: (i, 0)), pl.BlockSpec((tn, d), lambda j, i: (j, 0))],
        out_specs=pl.BlockSpec((tm, tn), lambda j, i: (i, j)),
        compiler_params=_cp("parallel", "parallel"), name="proj_fwd")(h, w_t)


def rope_tables(seq):
    n_freq = A_HEAD_DIM // 4
    d = jnp.arange(LANES) % A_HEAD_DIM
    seg, half, freq = d // (2 * n_freq), (d % (2 * n_freq)) // n_freq, d % n_freq
    inv = ROPE_THETA ** (-freq.astype(F32) / n_freq)
    t = jnp.arange(seq)
    pos = jnp.where(seg[None, :] == 0, (t // GRID_W)[:, None], (t % GRID_W)[:, None]).astype(F32)
    ang = pos * inv[None, :]
    cos, sin = jnp.cos(ang), jnp.sin(ang)
    return cos, jnp.where(half[None, :] == 1, sin, 0.0), jnp.where(half[None, :] == 0, -sin, 0.0)


def _group_ones(width, group):
    i = jnp.arange(width)
    return (i[:, None] // group == i[None, :] // group).astype(F32)


def _rope(xn, c, sa, sb):
    w = xn.shape[1]
    return xn * c + pltpu.roll(xn, 16, 1) * sa + pltpu.roll(xn, w - 16, 1) * sb


def _rope_t(dy, c, sa, sb):
    w = dy.shape[1]
    return dy * c + pltpu.roll(dy * sa, w - 16, 1) + pltpu.roll(dy * sb, 16, 1)


def _tile4(t):
    return jnp.concatenate([t, t, t, t], axis=1)


def qk_prep(proj, tabs, qg, kg, gq, gk):
    s = proj.shape[0]
    tm = min(s, 1024)
    c, sa, sb = tabs

    def body(p_ref, c_ref, sa_ref, sb_ref, qg_ref, kg_ref, gq_ref, gk_ref, qt_ref, kr_ref, vb_ref, kt_ref, v0_ref, v1_ref):
        xq = p_ref[:, O_QA:O_QA + A_WIDTH].astype(F32)
        xk = p_ref[:, O_KA:O_KA + A_KV_WIDTH].astype(F32)
        xv = p_ref[:, O_VA:O_VA + A_KV_WIDTH].astype(F32)
        cc, ssa, ssb = c_ref[...], sa_ref[...], sb_ref[...]
        msq = _group_sum(xq * xq, gq_ref[...]) * (1.0 / A_HEAD_DIM)
        qn = (xq * lax.rsqrt(msq + EPS)) * qg_ref[...]
        qr = _rope(qn, _tile4(cc), _tile4(ssa), _tile4(ssb)) * Q_SCALE
        qt_ref[...] = qr.T.astype(BF16)
        msk = _group_sum(xk * xk, gk_ref[...]) * (1.0 / A_HEAD_DIM)
        kn = (xk * lax.rsqrt(msk + EPS)) * kg_ref[...]
        kr = _rope(kn, cc, ssa, ssb)
        kr_ref[...] = kr.astype(BF16)
        vb_ref[...] = xv.astype(BF16)
        kt_ref[...] = kr.T.astype(BF16)
        vt = xv.T.astype(BF16)
        one = jnp.ones((VTE_ROWS - A_HEAD_DIM, tm), BF16)
        v0_ref[...] = jnp.concatenate([vt[:A_HEAD_DIM], one], axis=0)
        v1_ref[...] = jnp.concatenate([vt[A_HEAD_DIM:], one], axis=0)

    tab = _rows(tm, LANES)
    colb = lambda w: pl.BlockSpec((w, tm), lambda i: (0, i))
    return pl.pallas_call(
        body,
        out_shape=(_sds((A_WIDTH, s), BF16), _sds((s, A_KV_WIDTH), BF16), _sds((s, A_KV_WIDTH), BF16),
                   _sds((A_KV_WIDTH, s), BF16), _sds((VTE_ROWS, s), BF16), _sds((VTE_ROWS, s), BF16)),
        grid=(s // tm,),
        in_specs=[_rows(tm, PBLK), tab, tab, tab, _full((1, A_WIDTH)), _full((1, A_KV_WIDTH)),
                  _full((A_WIDTH, A_WIDTH)), _full((A_KV_WIDTH, A_KV_WIDTH))],
        out_specs=(colb(A_WIDTH), _rows(tm, A_KV_WIDTH), _rows(tm, A_KV_WIDTH), colb(A_KV_WIDTH), colb(VTE_ROWS), colb(VTE_ROWS)),
        compiler_params=_cp("parallel"), name="qk_prep")(proj, c, sa, sb, qg, kg, gq, gk)


def _pad_head(q_h, kv):
    z = jnp.zeros_like(q_h)
    return jnp.concatenate([q_h, z], axis=0) if kv == 0 else jnp.concatenate([z, q_h], axis=0)


def attn_fwd(q_t, kr, vte0, vte1, gather=()):
    s = kr.shape[0]
    tq = min(s, 512)
    kc = min(s, 256)
    nkc = s // kc
    nq = s // tq
    grp = A_HEADS // A_KV_HEADS
    ng = len(gather)

    def body(qt_ref, kr_ref, v0_ref, v1_ref, *rest):
        g_in, (o_ref, lse_ref), g_out = rest[:ng], rest[ng:ng + 2], rest[ng + 2:2 * ng + 2]
        qp_ref, m_ref, acc_ref = rest[2 * ng + 2:2 * ng + 5]
        if ng:
            start, forward, finish = gather_stages([g.shape for g in gather], g_in, g_out, *rest[2 * ng + 5:])
            pl.when(pl.program_id(0) == 0)(start)
            pl.when(pl.program_id(0) == (3 * nq) // 4)(forward)

        for h in range(A_HEADS):
            qp_ref[h] = _pad_head(qt_ref[A_HEAD_DIM * h:A_HEAD_DIM * (h + 1), :], h // grp)
        m_ref[...] = jnp.full(m_ref.shape, -1e30, F32)
        acc_ref[...] = jnp.zeros_like(acc_ref)

        def step(ci, carry):
            ks = pl.ds(pl.multiple_of(ci * kc, kc), kc)
            kblk = kr_ref[ks, :]
            vts = (v0_ref[:, ks], v1_ref[:, ks])
            scs = [_dot(kblk, qp_ref[h]) for h in range(A_HEADS)]
            for h in range(A_HEADS):
                sc = scs[h]
                m_prev = m_ref[h:h + 1, :]
                m_new = jnp.maximum(m_prev, jnp.max(sc, axis=0, keepdims=True))
                p = jnp.exp2(sc - m_new)
                acc_ref[h] = acc_ref[h] * jnp.exp2(m_prev - m_new) + _dot(vts[h // grp], p.astype(BF16))
                m_ref[h:h + 1, :] = m_new
            return carry

        lax.fori_loop(0, nkc, step, 0)
        outs, lses = [], []
        for h in range(A_HEADS):
            acc = acc_ref[h]
            l = acc[A_HEAD_DIM:A_HEAD_DIM + 1, :]
            outs.append(acc[:A_HEAD_DIM, :] / l)
            lses.append(m_ref[h:h + 1, :] + jnp.log2(l))
        o_ref[...] = jnp.concatenate(outs, axis=0).T
        lse_ref[...] = jnp.concatenate(lses, axis=0)
        if ng:
            pl.when(pl.program_id(0) == nq - 1)(finish)

    out = pl.pallas_call(
        body,
        out_shape=(_sds((s, A_WIDTH), F32), _sds((A_HEADS, s), F32)) + tuple(_sds((N_CHIPS,) + g.shape, g.dtype) for g in gather),
        grid=(nq,),
        in_specs=[pl.BlockSpec((A_WIDTH, tq), lambda i: (0, i)), _full((s, A_KV_WIDTH)), _full((VTE_ROWS, s)),
                  _full((VTE_ROWS, s))] + [_ANY] * ng,
        out_specs=(_rows(tq, A_WIDTH), pl.BlockSpec((A_HEADS, tq), lambda i: (0, i))) + (_ANY,) * ng,
        scratch_shapes=[pltpu.VMEM((A_HEADS, A_KV_WIDTH, tq), BF16), pltpu.VMEM((A_HEADS, tq), F32),
                        pltpu.VMEM((A_HEADS, VTE_ROWS, tq), F32)] + (gather_sems(ng) if ng else []),
        compiler_params=_cp("arbitrary"), name="attn_fwd_gather" if ng else "attn_fwd")(q_t, kr, vte0, vte1, *gather)
    return out[0], out[1], list(out[2:])


def memkv_fwd(mem, g, w_kv):
    m, d = mem.shape

    def body(mem_ref, g_ref, w_ref, mn_ref, kv_ref):
        mf = mem_ref[...]
        r = lax.rsqrt(jnp.mean(mf * mf, axis=-1, keepdims=True) + EPS)
        mn = ((mf * r) * g_ref[...]).astype(BF16)
        mn_ref[...] = mn
        kv_ref[...] = _dot(mn, w_ref[...]).astype(BF16)

    return pl.pallas_call(
        body, out_shape=(_sds((m, d), BF16), _sds((m, 2 * M_WIDTH), BF16)),
        compiler_params=_cp(), name="memkv_fwd")(mem, g, w_kv)


def _layer_norm_stats(v):
    mu = jnp.mean(v, axis=-1, keepdims=True)
    xc = v - mu
    rstd = lax.rsqrt(jnp.mean(xc * xc, axis=-1, keepdims=True) + EPS)
    return xc * rstd, rstd


def _spatial_mix(vlb, ws_ref, bsb_ref, tm):
    rows = []
    for ci in range(tm // CHUNK):
        cols = []
        for g in range(B_GROUPS):
            blk = vlb[ci * CHUNK:(ci + 1) * CHUNK, g * B_GROUP_DIM:(g + 1) * B_GROUP_DIM]
            cols.append(_dot(ws_ref[g], blk) + bsb_ref[g])
        rows.append(jnp.concatenate(cols, axis=1))
    return jnp.concatenate(rows, axis=0)


def _mem_attn(qm, kv_ref):
    out = []
    for h in range(M_HEADS):
        qh = qm[:, h * M_HEAD_DIM:(h + 1) * M_HEAD_DIM].astype(BF16)
        kh = kv_ref[:, h * M_HEAD_DIM:(h + 1) * M_HEAD_DIM]
        vh = kv_ref[:, M_WIDTH + h * M_HEAD_DIM:M_WIDTH + (h + 1) * M_HEAD_DIM]
        sc = _dot_nt(qh, kh) * (M_HEAD_DIM ** -0.5)
        e = jnp.exp(sc - jnp.max(sc, axis=-1, keepdims=True))
        p = e / jnp.sum(e, axis=-1, keepdims=True)
        out.append((p, _dot(p.astype(BF16), vh)))
    return out


def branch_fwd(x, proj, o_a, kv, ws, bsb, ln_g, ln_b, w_br, w_out, next_g):
    s, d = x.shape
    tm = min(s, 512)

    def body(x_ref, p_ref, oa_ref, kv_ref, ws_ref, bsb_ref, lg_ref, lb_ref, wbr_ref, wo_ref, ng_ref,
             xn_ref, y_ref, up_ref, mg_ref, hn_ref):
        seg = lambda o, w: p_ref[:, o:o + w].astype(F32)
        z_a, u_b, v_b, z_b = seg(O_ZA, A_WIDTH), seg(O_UB, B_WIDTH), seg(O_VB, B_WIDTH), seg(O_ZB, B_WIDTH)
        q_m, z_m = seg(O_QM, M_WIDTH), seg(O_ZM, M_WIDTH)
        xhat, _ = _layer_norm_stats(v_b)
        vln = xhat * lg_ref[...] + lb_ref[...]
        mixed = _spatial_mix(vln.astype(BF16), ws_ref, bsb_ref, tm)
        y_b = (u_b * mixed) * (z_b * _sig(z_b))
        o_m = jnp.concatenate([o for _, o in _mem_attn(q_m, kv_ref)], axis=1)
        y_a = oa_ref[...] * (z_a * _sig(z_a))
        y_m = o_m * (z_m * _sig(z_m))
        merged = None
        for n, yy in enumerate((y_a, y_b, y_m)):
            yb = yy.astype(BF16)
            y_ref[n] = yb
            up = jnp.concatenate([_dot(yb, wbr_ref[c, n]) for c in range(N_CHIPS)], axis=1)
            up_ref[n] = up.astype(BF16)
            t = _sig(seg(O_LG + n * d, d)) * up
            merged = t if merged is None else merged + t
        mb = merged.astype(BF16)
        mg_ref[...] = mb
        xn = x_ref[...] + _dot(mb, wo_ref[...])
        xn_ref[...] = xn
        r = lax.rsqrt(jnp.mean(xn * xn, axis=-1, keepdims=True) + EPS)
        hn_ref[...] = ((xn * r) * ng_ref[...]).astype(BF16)

    return pl.pallas_call(
        body,
        out_shape=(_sds((s, d), F32), _sds((N_BRANCH, s, A_WIDTH), BF16), _sds((N_BRANCH, s, d), BF16), _sds((s, d), BF16),
                   _sds((s, d), BF16)),
        grid=(s // tm,),
        in_specs=[_rows(tm, d), _rows(tm, IN_WIDTH), _rows(tm, A_WIDTH), _full(kv.shape), _full(ws.shape), _full(bsb.shape),
                  _full((1, B_WIDTH)), _full((1, B_WIDTH)), _full(w_br.shape), _full(w_out.shape), _full((1, d))],
        out_specs=(_rows(tm, d), pl.BlockSpec((N_BRANCH, tm, A_WIDTH), lambda i: (0, i, 0)),
                   pl.BlockSpec((N_BRANCH, tm, d), lambda i: (0, i, 0)), _rows(tm, d), _rows(tm, d)),
        compiler_params=_cp("parallel"), name="branch_fwd")(x, proj, o_a, kv, ws, bsb, ln_g, ln_b, w_br, w_out, next_g)


def final_loss(x, fg, tgt):
    s, d = x.shape
    tm = min(s, 512)

    def body(x_ref, g_ref, t_ref, ls_ref, dx_ref, gg_ref):
        @pl.when(pl.program_id(0) == 0)
        def _():
            ls_ref[...] = jnp.zeros_like(ls_ref)
            gg_ref[...] = jnp.zeros_like(gg_ref)

        xf = x_ref[...]
        g = g_ref[...]
        r = lax.rsqrt(jnp.mean(xf * xf, axis=-1, keepdims=True) + EPS)
        xh = xf * r
        e = xh * g - t_ref[...]
        sq = jnp.sum(jnp.sum(e * e, axis=0, keepdims=True), axis=1, keepdims=True)
        ls_ref[...] += jnp.broadcast_to(sq, ls_ref.shape)
        dy = e * (1.0 / d)
        gg_ref[...] += jnp.sum(dy * xh, axis=0, keepdims=True)
        gy = dy * g
        dx_ref[...] = r * (gy - xh * jnp.mean(gy * xh, axis=-1, keepdims=True))

    return pl.pallas_call(
        body, out_shape=(_sds((1, LANES), F32), _sds((s, d), F32), _sds((1, d), F32)), grid=(s // tm,),
        in_specs=[_rows(tm, d), _full((1, d)), _rows(tm, d)],
        out_specs=(_full((1, LANES)), _rows(tm, d), _full((1, d))),
        compiler_params=_cp("arbitrary"), name="final_loss")(x, fg, tgt)


def _pblocks(tm, first, count):
    return [pl.BlockSpec((tm, PBLK), functools.partial(lambda i, b: (i, b), b=first + k)) for k in range(count)]


def merge_bwd(dx, proj, y, up, merged, w_br, w_out):
    s, d = dx.shape
    tm = min(s, 512)
    nlg = LG_W // PBLK
    cw = d // N_CHIPS

    def body(dx_ref, l0, l1, l2, l3, y_ref, up_ref, mg_ref, wbr_ref, wo_ref, dy_ref, dlg_ref, gwo_ref, gwb_ref, gwo16_ref, gwb16_ref):
        @pl.when(pl.program_id(0) == 0)
        def _():
            gwo_ref[...] = jnp.zeros_like(gwo_ref)
            gwb_ref[...] = jnp.zeros_like(gwb_ref)

        dxb = dx_ref[...].astype(BF16)
        dmg = _dot_nt(dxb, wo_ref[...])
        gwo_ref[...] += _dot_tn(mg_ref[...], dxb)
        lg = jnp.concatenate([l0[...], l1[...], l2[...], l3[...]], axis=1).astype(F32)
        for n in range(N_BRANCH):
            g = _sig(lg[:, n * d:(n + 1) * d])
            dup = dmg * g
            dlg_ref[:, n * d:(n + 1) * d] = ((dup * up_ref[n].astype(F32)) * (1.0 - g)).astype(BF16)
            dupb = dup.astype(BF16)
            dyn = None
            for c in range(N_CHIPS):
                blk = dupb[:, c * cw:(c + 1) * cw]
                gwb_ref[c, n] += _dot_tn(y_ref[n], blk)
                t = _dot_nt(blk, wbr_ref[c, n])
                dyn = t if dyn is None else dyn + t
            dy_ref[n] = dyn.astype(BF16)

        @pl.when(pl.program_id(0) == pl.num_programs(0) - 1)
        def _():
            gwo16_ref[...] = gwo_ref[...].astype(BF16)
            gwb16_ref[...] = gwb_ref[...].astype(BF16)

    return pl.pallas_call(
        body,
        out_shape=(_sds((N_BRANCH, s, A_WIDTH), BF16), _sds((s, LG_W), BF16), _sds((d, d), F32), _sds(w_br.shape, F32),
                   _sds((d, d), BF16), _sds(w_br.shape, BF16)),
        grid=(s // tm,),
        in_specs=[_rows(tm, d)] + _pblocks(tm, O_LG // PBLK, nlg) + [
            pl.BlockSpec((N_BRANCH, tm, A_WIDTH), lambda i: (0, i, 0)), pl.BlockSpec((N_BRANCH, tm, d), lambda i: (0, i, 0)),
            _rows(tm, d), _full(w_br.shape, once=True), _full(w_out.shape, once=True)],
        out_specs=(pl.BlockSpec((N_BRANCH, tm, A_WIDTH), lambda i: (0, i, 0)), _rows(tm, LG_W), _full((d, d)), _full(w_br.shape),
                   _full((d, d)), _full(w_br.shape)),
        compiler_params=_cp("arbitrary"), name="merge_bwd")(dx, proj, proj, proj, proj, y, up, merged, w_br, w_out)


def _dsilu(z, sg):
    return sg * (1.0 + z * (1.0 - sg))


def branch_bwd(dy, proj, o_a, kv, ws, ws_t, bsb, ln_g, ln_b, head_sel):
    s = proj.shape[0]
    tm = min(s, 512)
    nmid = MID_W // PBLK

    def body(dy_ref, m0, m1, m2, m3, oa_ref, kv_ref, ws_ref, wst_ref, bsb_ref, lg_ref, lb_ref, sel_ref,
             dmid_ref, dot_ref, dl_ref, gws_ref, gbs_ref, glg_ref, glb_ref, dkv_ref):
        @pl.when(pl.program_id(0) == 0)
        def _():
            for r in (gws_ref, gbs_ref, glg_ref, glb_ref, dkv_ref):
                r[...] = jnp.zeros_like(r)

        mid = jnp.concatenate([m0[...], m1[...], m2[...], m3[...]], axis=1).astype(F32)
        seg = lambda o, w: mid[:, o - O_ZA:o - O_ZA + w]
        z_a, u_b, v_b, z_b = seg(O_ZA, A_WIDTH), seg(O_UB, B_WIDTH), seg(O_VB, B_WIDTH), seg(O_ZB, B_WIDTH)
        q_m, z_m = seg(O_QM, M_WIDTH), seg(O_ZM, M_WIDTH)

        def put(o, v):
            dmid_ref[:, o - O_ZA:o - O_ZA + v.shape[1]] = v.astype(BF16)

        dy_a, dy_b, dy_m = dy_ref[0].astype(F32), dy_ref[1].astype(F32), dy_ref[2].astype(F32)

        o_a_ = oa_ref[...]
        sg = _sig(z_a)
        do_a = dy_a * (z_a * sg)
        put(O_ZA, (dy_a * o_a_) * _dsilu(z_a, sg))
        do_l = do_a * LN2
        dot_ref[...] = do_l.T.astype(BF16)
        dl_ref[...] = _dot_nt_hi(sel_ref[...], do_l * o_a_)

        xhat, rstd = _layer_norm_stats(v_b)
        lng = lg_ref[...]
        vln = xhat * lng + lb_ref[...]
        vlb = vln.astype(BF16)
        mixed = _spatial_mix(vlb, ws_ref, bsb_ref, tm)
        sg = _sig(z_b)
        sl = z_b * sg
        put(O_UB, (dy_b * mixed) * sl)
        put(O_ZB, ((dy_b * u_b) * mixed) * _dsilu(z_b, sg))
        dmix = (dy_b * u_b) * sl
        dmb = dmix.astype(BF16)
        rows = []
        for ci in range(tm // CHUNK):
            cols = []
            for g in range(B_GROUPS):
                rs, cs = slice(ci * CHUNK, (ci + 1) * CHUNK), slice(g * B_GROUP_DIM, (g + 1) * B_GROUP_DIM)
                gws_ref[g] += _dot_nt(dmb[rs, cs], vlb[rs, cs])
                gbs_ref[g] += jnp.broadcast_to(jnp.sum(dmix[rs, cs], axis=1, keepdims=True), (CHUNK, B_GROUP_DIM))
                cols.append(_dot(wst_ref[g], dmb[rs, cs]))
            rows.append(jnp.concatenate(cols, axis=1))
        dvln = jnp.concatenate(rows, axis=0)
        glg_ref[...] += jnp.sum(dvln * xhat, axis=0, keepdims=True)
        glb_ref[...] += jnp.sum(dvln, axis=0, keepdims=True)
        gy = dvln * lng
        put(O_VB, rstd * ((gy - jnp.mean(gy, axis=-1, keepdims=True)) - xhat * jnp.mean(gy * xhat, axis=-1, keepdims=True)))

        sg = _sig(z_m)
        sl = z_m * sg
        heads = _mem_attn(q_m, kv_ref)
        o_m = jnp.concatenate([o for _, o in heads], axis=1)
        put(O_ZM, (dy_m * o_m) * _dsilu(z_m, sg))
        do_m = dy_m * sl
        dqs = []
        for h, (p, o_h) in enumerate(heads):
            hs = slice(h * M_HEAD_DIM, (h + 1) * M_HEAD_DIM)
            vs = slice(M_WIDTH + h * M_HEAD_DIM, M_WIDTH + (h + 1) * M_HEAD_DIM)
            do_h = do_m[:, hs]
            dob = do_h.astype(BF16)
            dp = _dot_nt(dob, kv_ref[:, vs])
            dsc = (p * (dp - jnp.sum(do_h * o_h, axis=-1, keepdims=True))) * (M_HEAD_DIM ** -0.5)
            dsb = dsc.astype(BF16)
            dqs.append(_dot(dsb, kv_ref[:, hs]))
            dkv_ref[:, hs] += _dot_tn(dsb, q_m[:, hs].astype(BF16))
            dkv_ref[:, vs] += _dot_tn(p.astype(BF16), dob)
        put(O_QM, jnp.concatenate(dqs, axis=1))

    return pl.pallas_call(
        body,
        out_shape=(_sds((s, MID_W), BF16), _sds((A_WIDTH, s), BF16), _sds((A_HEADS, s), F32), _sds(ws.shape, F32),
                   _sds(ws.shape, F32), _sds((1, B_WIDTH), F32), _sds((1, B_WIDTH), F32), _sds(kv.shape, F32)),
        grid=(s // tm,),
        in_specs=[pl.BlockSpec((N_BRANCH, tm, A_WIDTH), lambda i: (0, i, 0))] + _pblocks(tm, O_ZA // PBLK, nmid) + [
            _rows(tm, A_WIDTH), _full(kv.shape), _full(ws.shape), _full(ws.shape), _full(bsb.shape),
            _full((1, B_WIDTH)), _full((1, B_WIDTH)), _full(head_sel.shape)],
        out_specs=(_rows(tm, MID_W), pl.BlockSpec((A_WIDTH, tm), lambda i: (0, i)), pl.BlockSpec((A_HEADS, tm), lambda i: (0, i)),
                   _full(ws.shape), _full(ws.shape), _full((1, B_WIDTH)), _full((1, B_WIDTH)), _full(kv.shape)),
        compiler_params=_cp("arbitrary"), name="branch_bwd")(dy, proj, proj, proj, proj, o_a, kv, ws, ws_t, bsb, ln_g, ln_b, head_sel)


def attn_bwd(q_t, do_t, kr, kr_t, vb, lse, delta, scatter=()):
    s = kr.shape[0]
    tq = min(s, 256)
    kc = min(s, 512)
    nkc = s // kc
    nq = s // tq
    grp = A_HEADS // A_KV_HEADS
    ns = len(scatter)
    na = ns // 2

    def body(qt_ref, dot_ref, kr_ref, krt_ref, vb_ref, lse_ref, dl_ref, *rest):
        s_in, (dqt_ref, dk_ref, dv_ref), s_out = rest[:ns], rest[ns:ns + 3], rest[ns + 3:2 * ns + 3]
        qp_ref, dop_ref, dq_ref = rest[2 * ns + 3:2 * ns + 6]
        if ns:
            start, finish = scatter_stages([g.shape[1:] for g in scatter[:na]], s_in[:na], s_in[na:], s_out[:na], s_out[na:],
                                           *rest[2 * ns + 6:])
            pl.when(pl.program_id(0) == 0)(start)

        @pl.when(pl.program_id(0) == 0)
        def _():
            dk_ref[...] = jnp.zeros_like(dk_ref)
            dv_ref[...] = jnp.zeros_like(dv_ref)

        for h in range(A_HEADS):
            hs = slice(A_HEAD_DIM * h, A_HEAD_DIM * (h + 1))
            qp_ref[h] = _pad_head(qt_ref[hs, :], h // grp)
            dop_ref[h] = _pad_head(dot_ref[hs, :], h // grp)
        dq_ref[...] = jnp.zeros_like(dq_ref)

        def step(ci, carry):
            ks = pl.ds(pl.multiple_of(ci * kc, kc), kc)
            kblk, vblk, ktb = kr_ref[ks, :], vb_ref[ks, :], krt_ref[:, ks]
            dv_acc = jnp.zeros((kc, A_KV_WIDTH), F32)
            dk_acc = jnp.zeros((kc, A_KV_WIDTH), F32)
            scs = [_dot(kblk, qp_ref[h]) for h in range(A_HEADS)]
            dps = [_dot(vblk, dop_ref[h]) for h in range(A_HEADS)]
            for h in range(A_HEADS):
                qpad, dopad = qp_ref[h], dop_ref[h]
                p = jnp.exp2(scs[h] - lse_ref[h:h + 1, :])
                dsb = (p * (dps[h] - dl_ref[h:h + 1, :])).astype(BF16)
                dv_acc = dv_acc + _dot_nt(p.astype(BF16), dopad)
                dk_acc = dk_acc + _dot_nt(dsb, qpad)
                dq_ref[h] += _dot(ktb, dsb)
            dv_ref[ks, :] += dv_acc
            dk_ref[ks, :] += dk_acc
            return carry

        lax.fori_loop(0, nkc, step, 0)
        dqt_ref[...] = jnp.concatenate(
            [dq_ref[h][A_HEAD_DIM * (h // grp):A_HEAD_DIM * (h // grp + 1), :] for h in range(A_HEADS)], axis=0)
        if ns:
            pl.when(pl.program_id(0) == nq - 1)(finish)

    colq = pl.BlockSpec((A_WIDTH, tq), lambda i: (0, i))
    colh = pl.BlockSpec((A_HEADS, tq), lambda i: (0, i))
    out = pl.pallas_call(
        body,
        out_shape=(_sds((A_WIDTH, s), F32), _sds((s, A_KV_WIDTH), F32), _sds((s, A_KV_WIDTH), F32)) + scatter_out_shapes(scatter[:na]),
        grid=(nq,),
        in_specs=[colq, colq, _full((s, A_KV_WIDTH)), _full((A_KV_WIDTH, s)), _full((s, A_KV_WIDTH)), colh, colh] + [_ANY] * ns,
        out_specs=(colq, _full((s, A_KV_WIDTH)), _full((s, A_KV_WIDTH))) + (_ANY,) * ns,
        scratch_shapes=[pltpu.VMEM((A_HEADS, A_KV_WIDTH, tq), BF16), pltpu.VMEM((A_HEADS, A_KV_WIDTH, tq), BF16),
                        pltpu.VMEM((A_HEADS, A_KV_WIDTH, tq), F32)] + (scatter_sems(na) if ns else []),
        compiler_params=_cp("arbitrary"), name="attn_bwd_scatter" if ns else "attn_bwd")(
            q_t, do_t, kr, kr_t, vb, lse, delta, *scatter)
    return out[0], out[1], out[2], list(out[3:3 + na]), list(out[3 + na:])


def qk_prep_bwd(proj, dq_t, dkr, dvb, tabs, qg, kg, gq, gk, fold_q, fold_k):
    s = proj.shape[0]
    tm = min(s, 1024)
    c, sa, sb = tabs

    def head_norm_bwd(x, dn, gain, gones, fold):
        ms = _group_sum(x * x, gones) * (1.0 / A_HEAD_DIM)
        r = lax.rsqrt(ms + EPS)
        xh = x * r
        gg = _dot_hi(jnp.sum(dn * xh, axis=0, keepdims=True), fold)
        u = dn * gain
        mean_u = _group_sum(u * xh, gones) * (1.0 / A_HEAD_DIM)
        return r * (u - xh * mean_u), gg

    def body(p_ref, dqt_ref, dk_ref, dv_ref, c_ref, sa_ref, sb_ref, qg_ref, kg_ref, gq_ref, gk_ref, fq_ref, fk_ref,
             dqkv_ref, gqg_ref, gkg_ref):
        @pl.when(pl.program_id(0) == 0)
        def _():
            gqg_ref[...] = jnp.zeros_like(gqg_ref)
            gkg_ref[...] = jnp.zeros_like(gkg_ref)

        cc, ssa, ssb = c_ref[...], sa_ref[...], sb_ref[...]
        dqr = dqt_ref[...].T * Q_SCALE
        dqn = _rope_t(dqr, _tile4(cc), _tile4(ssa), _tile4(ssb))
        dxq, gq_ = head_norm_bwd(p_ref[:, O_QA:O_QA + A_WIDTH].astype(F32), dqn, qg_ref[...], gq_ref[...], fq_ref[...])
        dkn = _rope_t(dk_ref[...], cc, ssa, ssb)
        dxk, gk_ = head_norm_bwd(p_ref[:, O_KA:O_KA + A_KV_WIDTH].astype(F32), dkn, kg_ref[...], gk_ref[...], fk_ref[...])
        gqg_ref[...] += gq_
        gkg_ref[...] += gk_
        dqkv_ref[:, O_QA:O_QA + A_WIDTH] = dxq.astype(BF16)
        dqkv_ref[:, O_KA:O_KA + A_KV_WIDTH] = dxk.astype(BF16)
        dqkv_ref[:, O_VA:O_VA + A_KV_WIDTH] = (dv_ref[...] * (1.0 / LN2)).astype(BF16)

    tab = _rows(tm, LANES)
    return pl.pallas_call(
        body, out_shape=(_sds((s, PBLK), BF16), _sds((1, LANES), F32), _sds((1, LANES), F32)), grid=(s // tm,),
        in_specs=[_rows(tm, PBLK), pl.BlockSpec((A_WIDTH, tm), lambda i: (0, i)), _rows(tm, A_KV_WIDTH), _rows(tm, A_KV_WIDTH),
                  tab, tab, tab, _full((1, A_WIDTH)), _full((1, A_KV_WIDTH)), _full((A_WIDTH, A_WIDTH)),
                  _full((A_KV_WIDTH, A_KV_WIDTH)), _full((A_WIDTH, LANES)), _full((A_KV_WIDTH, LANES))],
        out_specs=(_rows(tm, PBLK), _full((1, LANES)), _full((1, LANES))),
        compiler_params=_cp("arbitrary"), name="qk_prep_bwd")(proj, dq_t, dkr, dvb, c, sa, sb, qg, kg, gq, gk, fold_q, fold_k)


def _pick_dproj(b, d0, d1, d2, use):
    first_lg = 1 + MID_W // PBLK

    @pl.when(b == 0)
    def _():
        use(d0[...])

    @pl.when(jnp.logical_and(b >= 1, b < first_lg))
    def _():
        use(d1[...])

    @pl.when(b >= first_lg)
    def _():
        use(d2[...])


def win_grad(d0, d1, d2, h):
    s, d = h.shape
    tk = min(s, 2048)
    nk = s // tk

    def body(d0_ref, d1_ref, d2_ref, h_ref, o_ref, o16_ref):
        @pl.when(pl.program_id(1) == 0)
        def _():
            o_ref[...] = jnp.zeros_like(o_ref)

        def use(blk):
            o_ref[...] += _dot_tn(blk, h_ref[...])

        _pick_dproj(pl.program_id(0), d0_ref, d1_ref, d2_ref, use)

        @pl.when(pl.program_id(1) == nk - 1)
        def _():
            o16_ref[...] = o_ref[...].astype(BF16)

    def spec(first, count):
        def imap(j, k):
            used = jnp.logical_and(j >= first, j < first + count)
            return (jnp.where(used, k, 0), jnp.clip(j - first, 0, count - 1))
        return pl.BlockSpec((tk, PBLK), imap)

    nm = MID_W // PBLK
    oblk = pl.BlockSpec((PBLK, d), lambda j, k: (j, 0))
    return pl.pallas_call(
        body, out_shape=(_sds((IN_WIDTH, d), F32), _sds((IN_WIDTH, d), BF16)), grid=(N_PBLK, nk),
        in_specs=[spec(0, 1), spec(1, nm), spec(1 + nm, LG_W // PBLK), pl.BlockSpec((tk, d), lambda j, k: (k, 0))],
        out_specs=(oblk, oblk),
        compiler_params=_cp("parallel", "arbitrary"), name="win_grad")(d0, d1, d2, h)


def h_bwd(d0, d1, d2, w_t, x, dx_out, g, scatter=()):
    s, d = x.shape
    tm = min(s, 512)
    nt = s // tm
    ns = len(scatter)
    na = ns // 2

    def body(d0_ref, d1_ref, d2_ref, w_ref, x_ref, dxo_ref, g_ref, *rest):
        s_in, (dx_ref, gg_ref), s_out = rest[:ns], rest[ns:ns + 2], rest[ns + 2:2 * ns + 2]
        if ns:
            start, finish = scatter_stages([a.shape[1:] for a in scatter[:na]], s_in[:na], s_in[na:], s_out[:na], s_out[na:],
                                           *rest[2 * ns + 2:])
            pl.when(pl.program_id(0) == 0)(start)

        @pl.when(pl.program_id(0) == 0)
        def _():
            gg_ref[...] = jnp.zeros_like(gg_ref)

        dh = (_dot(d0_ref[...], w_ref[0:PBLK, :]) + _dot(d1_ref[...], w_ref[PBLK:PBLK + MID_W, :])
              + _dot(d2_ref[...], w_ref[PBLK + MID_W:, :]))
        xf = x_ref[...]
        r = lax.rsqrt(jnp.mean(xf * xf, axis=-1, keepdims=True) + EPS)
        xh = xf * r
        gg_ref[...] += jnp.sum(dh * xh, axis=0, keepdims=True)
        u = dh * g_ref[...]
        dx_ref[...] = dxo_ref[...] + r * (u - xh * jnp.mean(u * xh, axis=-1, keepdims=True))
        if ns:
            pl.when(pl.program_id(0) == nt - 1)(finish)

    rowb = _rows(tm, d)
    out = pl.pallas_call(
        body, out_shape=(_sds((s, d), F32), _sds((1, d), F32)) + scatter_out_shapes(scatter[:na]), grid=(nt,),
        in_specs=[_rows(tm, PBLK), _rows(tm, MID_W), _rows(tm, LG_W),
                  pl.BlockSpec(w_t.shape, lambda i: (0, 0), pipeline_mode=pl.Buffered(1)), rowb, rowb, _full((1, d))] + [_ANY] * ns,
        out_specs=(rowb, _full((1, d))) + (_ANY,) * ns,
        scratch_shapes=scatter_sems(na) if ns else [],
        compiler_params=_cp("arbitrary"), name="h_bwd_scatter" if ns else "h_bwd")(d0, d1, d2, w_t, x, dx_out, g, *scatter)
    return out[0], out[1], list(out[2:2 + na]), list(out[2 + na:])


def memkv_bwd(mem, g, mem_n, w_kv, dkv):
    m, d = mem.shape

    def body(mem_ref, g_ref, mn_ref, w_ref, dkv_ref, gw_ref, gw16_ref, gg_ref):
        dkb = dkv_ref[...].astype(BF16)
        gw = _dot_tn(mn_ref[...], dkb)
        gw_ref[...] = gw
        gw16_ref[...] = gw.astype(BF16)
        dmn = _dot_nt(dkb, w_ref[...])
        mf = mem_ref[...]
        r = lax.rsqrt(jnp.mean(mf * mf, axis=-1, keepdims=True) + EPS)
        gg_ref[...] = jnp.sum(dmn * (mf * r), axis=0, keepdims=True)

    return pl.pallas_call(
        body, out_shape=(_sds(w_kv.shape, F32), _sds(w_kv.shape, BF16), _sds((1, d), F32)),
        compiler_params=_cp(), name="memkv_bwd")(mem, g, mem_n, w_kv, dkv)


def _layer_consts(seq):
    i = jnp.arange(A_WIDTH)
    return dict(
        tabs=rope_tables(seq),
        gq=_group_ones(A_WIDTH, A_HEAD_DIM).astype(BF16), gk=_group_ones(A_KV_WIDTH, A_HEAD_DIM).astype(BF16),
        fold_q=(i[:, None] % A_HEAD_DIM == jnp.arange(LANES)[None, :]).astype(F32),
        fold_k=(i[:A_KV_WIDTH, None] % A_HEAD_DIM == jnp.arange(LANES)[None, :]).astype(F32),
        head_sel=(jnp.arange(A_HEADS)[:, None] == i[None, :] // A_HEAD_DIM).astype(F32),
    )


_BIG = ("win_t", "wkv", "wbr", "wout")


def _with_own_part(names, gathered, shards, chip, d):
    shape = dict(win_t=(IN_WIDTH, d), wkv=(d, 2 * M_WIDTH), wbr=(N_CHIPS, N_BRANCH, A_WIDTH, d // N_CHIPS), wout=(d, d))
    return {n: lax.dynamic_update_slice(g, sh[None], (chip, 0, 0)).reshape(shape[n]) for n, g, sh in zip(names, gathered, shards)}


def local_fwd_bwd(x, mem, tgt, small, big=None, shards=None, place=None):
    s, d = x.shape
    depth = small["norm_g"].shape[0]
    k = _layer_consts(s)
    row = lambda v: v.reshape(1, -1)
    dist = shards is not None
    if dist:
        big = [_with_own_part(_BIG[:1], allgather_layer(shards[0][:1]), shards[0][:1], place[0], d)] + [None] * (depth - 1)
    saved = []
    for l in range(depth):
        ng = row(small["norm_g"][l])
        qg = row(jnp.tile(small["q_norm_g"][l], A_HEADS))
        kg = row(jnp.tile(small["k_norm_g"][l], A_KV_HEADS))
        ws = small["w_s"][l].astype(BF16)
        ws_t = jnp.swapaxes(small["w_s"][l], 1, 2).astype(BF16)
        bsb = jnp.broadcast_to(small["b_s"][l][:, :, None], (B_GROUPS, CHUNK, B_GROUP_DIM))
        lng, lnb = row(small["sg_ln_g"][l]), row(small["sg_ln_b"][l])
        mg = row(small["mem_norm_g"][l])
        w = big[l]
        h = rms_fwd(x, ng) if l == 0 else h_next
        proj = proj_fwd(h, w["win_t"])
        q_t, kr, vb, kr_t, vte0, vte1 = qk_prep(proj, k["tabs"], qg, kg, k["gq"], k["gk"])
        late = list(shards[0][1:]) if dist and l == 0 else []
        nxt = list(shards[l + 1]) if dist and l + 1 < depth else []
        o_a, lse, gathered = attn_fwd(q_t, kr, vte0, vte1, gather=tuple(late + nxt))
        if late:
            w.update(_with_own_part(_BIG[1:], gathered[:len(late)], late, place[0], d))
        if nxt:
            big[l + 1] = _with_own_part(_BIG, gathered[len(late):], nxt, place[0], d)
        mem_n, kv = memkv_fwd(mem, mg, w["wkv"])
        next_g = row(small["norm_g"][l + 1]) if l + 1 < depth else row(small["final_g"])
        x_next, y, up, merged, h_next = branch_fwd(x, proj, o_a, kv, ws, bsb, lng, lnb, w["wbr"], w["wout"], next_g)
        saved.append(dict(x=x, ng=ng, qg=qg, kg=kg, ws=ws, ws_t=ws_t, bsb=bsb, lng=lng, lnb=lnb, mg=mg, h=h, proj=proj,
                          q_t=q_t, kr=kr, kr_t=kr_t, vb=vb, o_a=o_a, lse=lse, mem_n=mem_n, kv=kv, y=y, up=up, merged=merged))
        x = x_next

    sq, dx, g_final = final_loss(x, row(small["final_g"]), tgt)
    grads = {n: [None] * depth for n in ("norm_g", "q_norm_g", "k_norm_g", "sg_ln_g", "sg_ln_b", "w_s", "b_s", "mem_norm_g")}
    parts = lambda g: g.reshape(N_CHIPS, -1, g.shape[-1])
    reduced = [[None] * len(_BIG) for _ in range(depth)]

    def reduce_all(items, t_sib, t_rem):
        if items:
            for (ll, a, _, _), f in zip(items, reduce_rows(place, [i[2] for i in items], t_sib, t_rem)):
                reduced[ll][a] = f

    as_scatter = lambda items: tuple(i[2] for i in items) + tuple(i[3] for i in items)
    pending = []
    for l in reversed(range(depth)):
        sv, w = saved[l], big[l]
        dy, dlg, g_wout, g_wbr, g_wout16, g_wbr16 = merge_bwd(dx, sv["proj"], sv["y"], sv["up"], sv["merged"], w["wbr"], w["wout"])
        dmid, do_t, delta, g_ws, g_bs, g_lng, g_lnb, dkv = branch_bwd(
            dy, sv["proj"], sv["o_a"], sv["kv"], sv["ws"], sv["ws_t"], sv["bsb"], sv["lng"], sv["lnb"], k["head_sel"])
        g_wkv, g_wkv16, g_mg = memkv_bwd(mem, sv["mg"], sv["mem_n"], w["wkv"], dkv)
        if dist:
            pending += [(l, 1, parts(g_wkv), parts(g_wkv16)), (l, 2, parts(g_wbr), parts(g_wbr16)), (l, 3, parts(g_wout), parts(g_wout16))]
        dq_t, dkr, dvb, t_sib, t_rem = attn_bwd(sv["q_t"], do_t, sv["kr"], sv["kr_t"], sv["vb"], sv["lse"], delta,
                                                scatter=as_scatter(pending))
        reduce_all(pending, t_sib, t_rem)
        dqkv, g_qg, g_kg = qk_prep_bwd(sv["proj"], dq_t, dkr, dvb, k["tabs"], sv["qg"], sv["kg"], k["gq"], k["gk"],
                                       k["fold_q"], k["fold_k"])
        g_win, g_win16 = win_grad(dqkv, dmid, dlg, sv["h"])
        pending = [(l, 0, parts(g_win), parts(g_win16))] if dist else []
        last = as_scatter(pending) if l == 0 else ()
        dx, g_ng, t_sib, t_rem = h_bwd(dqkv, dmid, dlg, w["win_t"], sv["x"], dx, sv["ng"], scatter=last)
        if last:
            reduce_all(pending, t_sib, t_rem)
        grads["norm_g"][l] = g_ng[0]
        grads["q_norm_g"][l] = g_qg[0, :A_HEAD_DIM]
        grads["k_norm_g"][l] = g_kg[0, :A_HEAD_DIM]
        grads["sg_ln_g"][l] = g_lng[0]
        grads["sg_ln_b"][l] = g_lnb[0]
        grads["w_s"][l] = g_ws
        grads["b_s"][l] = g_bs[:, :, 0]
        grads["mem_norm_g"][l] = g_mg[0]
        if not dist:
            reduced[l] = dict(zip(_BIG, (parts(g_win), parts(g_wkv), parts(g_wbr), parts(g_wout))))
    grads = {n: jnp.stack(v) for n, v in grads.items()}
    grads["final_g"] = g_final[0]
    return sq[0, 0], dx, grads, reduced


def _row_block(rows, width, cap_bytes=2 * 2**20):
    best = None
    for br in range(8, rows + 1, 8):
        if rows % br == 0 and br * width * 4 <= cap_bytes:
            best = br
    return best if best is not None else rows


def adamw(w, gs, m, v):
    r, c = w.shape
    n = len(gs)
    rs = r // n
    br = _row_block(rs, c)
    nb = rs // br

    def body(w_ref, *refs):
        g_refs, (m_ref, v_ref, og_ref, d_ref, nm_ref, nv_ref) = refs[:n], refs[n:]

        def update(gg):
            mm = ADAM_B1 * m_ref[...] + (1.0 - ADAM_B1) * gg
            vv = ADAM_B2 * v_ref[...] + (1.0 - ADAM_B2) * (gg * gg)
            m_hat = mm / (1.0 - ADAM_B1 ** ADAM_STEP)
            v_hat = vv / (1.0 - ADAM_B2 ** ADAM_STEP)
            og_ref[...] = gg
            d_ref[...] = -ADAM_LR * (m_hat / (jnp.sqrt(v_hat) + ADAM_EPS) + ADAM_WD * w_ref[...])
            nm_ref[...] = mm
            nv_ref[...] = vv

        for k in range(n):
            pl.when(pl.program_id(0) == k)(functools.partial(lambda k: update(g_refs[k][...]), k))

    blk = pl.BlockSpec((br, c), lambda l, i: (l * nb + i, 0))
    g_specs = [pl.BlockSpec((br, c), functools.partial(lambda l, i, k: (jnp.where(l == k, i, 0), 0), k=k)) for k in range(n)]
    return pl.pallas_call(
        body, out_shape=(_sds((r, c), F32),) * 4, grid=(n, nb), in_specs=[blk] + g_specs + [blk, blk], out_specs=(blk,) * 4,
        compiler_params=_cp("arbitrary", "arbitrary"), name="adamw")(w, *gs, m, v)


N_REMOTE = 2 * (N_CHIPS - 1)


def reduce_rows(place, gs, t_sibs, t_rems):
    n = len(gs)
    nt = 2

    def body(place_ref, *refs):
        for a in range(n):
            g_ref, s_ref, t_ref, f_ref = refs[a], refs[n + a], refs[2 * n + a], refs[3 * n + a]
            acc = g_ref[...] + s_ref[...]
            for j in range(N_REMOTE):
                acc = acc + t_ref[j].astype(F32)
            f_ref[...] = acc

    tiles = [(g.shape[1] // 2 // nt, g.shape[2]) for g in gs]
    return pl.pallas_call(
        body, out_shape=tuple(_sds(g.shape[1:], F32) for g in gs),
        grid_spec=pltpu.PrefetchScalarGridSpec(
            num_scalar_prefetch=1, grid=(nt,),
            in_specs=[pl.BlockSpec((None, tr, c), lambda i, p: (p[0], p[1] * nt + i, 0)) for tr, c in tiles]
            + [pl.BlockSpec((tr, c), lambda i, p: (i, 0)) for tr, c in tiles]
            + [pl.BlockSpec((N_REMOTE, tr, c), lambda i, p: (0, i, 0)) for tr, c in tiles],
            out_specs=tuple(pl.BlockSpec((tr, c), lambda i, p: (p[1] * nt + i, 0)) for tr, c in tiles)),
        compiler_params=_cp("parallel"), name="reduce_rows")(place, *gs, *t_sibs, *t_rems)


_ANY = pl.BlockSpec(memory_space=pl.ANY)


def _place():
    x, y, c = lax.axis_index("x"), lax.axis_index("y"), lax.axis_index("c")
    chips = [(1 - x, y), (x, 1 - y), (1 - x, 1 - y)]
    return x, y, c, chips


def gather_sems(n):
    return [pltpu.SemaphoreType.DMA((n, N_REMOTE)), pltpu.SemaphoreType.DMA((n, N_REMOTE))]


def gather_stages(shapes, ins, outs, send, recv):
    n = len(shapes)
    x, y, c, chips = _place()
    me = 2 * x + y
    sib = (x, y, 1 - c)

    def rows(a, hl):
        r2 = shapes[a][0] // 2
        return pl.ds(hl * r2, r2)

    def remote(a, k, src, dst, dev):
        return pltpu.make_async_remote_copy(src, dst, send.at[a, k], recv.at[a, k], device_id=dev, device_id_type=MESH)

    def sent(a, k):
        cx, cy = chips[k]
        return remote(a, k, ins[a].at[rows(a, c)], outs[a].at[me, rows(a, c)], (cx, cy, c))

    def passed(a, k, hl):
        cx, cy = chips[k]
        got = outs[a].at[2 * cx + cy, rows(a, hl)]
        return remote(a, k, got, got, (cx, cy, c)), remote(a, 3 + k, got, got, sib)

    def start():
        for a in range(n):
            for k in range(3):
                sent(a, k).start()

    def forward():
        for k in range(3):
            for a in range(n):
                arrived, on = passed(a, k, c)
                arrived.wait_recv()
                on.start()

    def finish():
        for k in range(3):
            for a in range(n):
                passed(a, k, 1 - c)[1].wait_recv()
        for k in range(3):
            for a in range(n):
                sent(a, k).wait_send()
                passed(a, k, c)[1].wait_send()

    return start, forward, finish


def allgather_layer(shards):
    n = len(shards)

    def body(*refs):
        for stage in gather_stages([a.shape for a in shards], refs[:n], refs[n:2 * n], *refs[2 * n:]):
            stage()

    return pl.pallas_call(
        body, out_shape=tuple(_sds((N_CHIPS,) + a.shape, a.dtype) for a in shards),
        in_specs=[_ANY] * n, out_specs=(_ANY,) * n, scratch_shapes=gather_sems(n), name="allgather_layer")(*shards)


def scatter_sems(n):
    return [pltpu.SemaphoreType.DMA((n, N_REMOTE + 1)), pltpu.SemaphoreType.DMA((n, N_REMOTE + 1))]


def scatter_out_shapes(gs):
    return (tuple(_sds((g.shape[1] // 2, g.shape[2]), F32) for g in gs)
            + tuple(_sds((N_REMOTE, g.shape[1] // 2, g.shape[2]), BF16) for g in gs))


def scatter_stages(shapes, gf, gb, t_sib, t_rem, send, recv):
    n = len(shapes)
    x, y, c, chips = _place()
    me = 2 * x + y

    def copies():
        out = []
        for a in range(n):
            r2 = shapes[a][0] // 2
            out.append(pltpu.make_async_remote_copy(gf[a].at[me, pl.ds((1 - c) * r2, r2)], t_sib[a], send.at[a, N_REMOTE],
                                                    recv.at[a, N_REMOTE], device_id=(x, y, 1 - c), device_id_type=MESH))
            for k, (cx, cy) in enumerate(chips):
                for o in range(2):
                    tc = c if o == 0 else 1 - c
                    out.append(pltpu.make_async_remote_copy(gb[a].at[2 * cx + cy, pl.ds(tc * r2, r2)], t_rem[a].at[2 * k + o],
                                                            send.at[a, 2 * k + o], recv.at[a, 2 * k + o],
                                                            device_id=(cx, cy, tc), device_id_type=MESH))
        return out

    def start():
        for cp in copies():
            cp.start()

    def finish():
        for cp in copies():
            cp.wait()

    return start, finish


def finish_exchange(v, fs):
    n = len(fs)
    r, w = v.shape
    ndev = 2 * N_CHIPS

    def body(v_ref, *refs):
        out, sum_ref = refs[n:2 * n], refs[2 * n]
        all_ref, send, recv, loc, fsend, frecv = refs[2 * n + 1:]
        x, y, c, chips = _place()
        me, sib = (x, y, c), (x, y, 1 - c)
        swaps = []
        for a in range(n):
            r2 = fs[a].shape[0] // 2
            mine = out[a].at[pl.ds(c * r2, r2)]
            cp = pltpu.make_async_remote_copy(mine, mine, fsend.at[a], frecv.at[a], device_id=sib, device_id_type=MESH)
            cp.start()
            swaps.append(cp)

        def slab(px, py, pc):
            return all_ref.at[4 * px + 2 * py + pc]

        def copy(k, block, to, src=None):
            return pltpu.make_async_remote_copy(slab(*block) if src is None else src, slab(*block), send.at[k], recv.at[k],
                                                device_id=to, device_id_type=MESH)

        mine = pltpu.make_async_copy(v_ref, slab(*me), loc)
        mine.start()
        first = [copy(0, me, sib, src=v_ref)] + [copy(1 + j, me, (*chip, c), src=v_ref) for j, chip in enumerate(chips)]
        for cp in first:
            cp.start()
        passed = [copy(4 + j, (*chip, c), sib) for j, chip in enumerate(chips)]
        for j, chip in enumerate(chips):
            copy(1 + j, (*chip, c), me).wait_recv()
            passed[j].start()
        copy(0, sib, me).wait_recv()
        for j, chip in enumerate(chips):
            copy(4 + j, (*chip, 1 - c), me).wait_recv()
        for cp in first + passed:
            cp.wait_send()
        mine.wait()
        acc = all_ref[0]
        for i in range(1, ndev):
            acc = acc + all_ref[i]
        sum_ref[...] = acc
        for a, cp in enumerate(swaps):
            r2 = fs[a].shape[0] // 2
            theirs = out[a].at[pl.ds((1 - c) * r2, r2)]
            cp.wait_send()
            pltpu.make_async_remote_copy(theirs, theirs, fsend.at[a], frecv.at[a], device_id=sib, device_id_type=MESH).wait_recv()

    vm = pl.BlockSpec(memory_space=pltpu.VMEM)
    res = pl.pallas_call(
        body, out_shape=tuple(_sds(f.shape, F32) for f in fs) + (_sds((r, w), F32),),
        in_specs=[vm] + [_ANY] * n, out_specs=(_ANY,) * n + (vm,), input_output_aliases={a + 1: a for a in range(n)},
        scratch_shapes=[pltpu.VMEM((ndev, r, w), F32), pltpu.SemaphoreType.DMA((7,)), pltpu.SemaphoreType.DMA((7,)),
                        pltpu.SemaphoreType.DMA, pltpu.SemaphoreType.DMA((n,)), pltpu.SemaphoreType.DMA((n,))],
        compiler_params=pltpu.CompilerParams(vmem_limit_bytes=VMEM_LIMIT), name="finish_exchange")(v, *fs)
    return res[n], list(res[:n])


_SMALL = ("norm_g", "q_norm_g", "k_norm_g", "sg_ln_g", "sg_ln_b", "w_s", "b_s", "mem_norm_g", "final_g")
_WEIGHTS = ("norm_g", "w_in", "q_norm_g", "k_norm_g", "sg_ln_g", "sg_ln_b", "w_s", "b_s", "mem_norm_g", "w_mem_kv", "w_br",
            "w_out", "final_g")


def _pack(d):
    flat = jnp.concatenate([d[n].reshape(-1) for n in _SMALL])
    rows = -(-flat.shape[0] // (8 * LANES)) * 8
    return jnp.pad(flat, (0, rows * LANES - flat.shape[0])).reshape(rows, LANES)


def _unpack(p, like):
    flat, out, o = p.reshape(-1), {}, 0
    for n in _SMALL:
        out[n] = flat[o:o + like[n].size].reshape(like[n].shape)
        o += like[n].size
    return out


def kernel(x, mem, norm_g, w_in, q_norm_g, k_norm_g, sg_ln_g, sg_ln_b, w_s, b_s, mem_norm_g, w_mem_kv, w_br, w_out, final_g, loss_target, m_norm_g, m_w_in, m_q_norm_g, m_k_norm_g, m_sg_ln_g, m_sg_ln_b, m_w_s, m_b_s, m_mem_norm_g, m_w_mem_kv, m_w_br, m_w_out, m_final_g, v_norm_g, v_w_in, v_q_norm_g, v_k_norm_g, v_sg_ln_g, v_sg_ln_b, v_w_s, v_b_s, v_mem_norm_g, v_w_mem_kv, v_w_br, v_w_out, v_final_g):
    w = dict(norm_g=norm_g, w_in=w_in, q_norm_g=q_norm_g, k_norm_g=k_norm_g, sg_ln_g=sg_ln_g, sg_ln_b=sg_ln_b, w_s=w_s, b_s=b_s,
             mem_norm_g=mem_norm_g, w_mem_kv=w_mem_kv, w_br=w_br, w_out=w_out, final_g=final_g)
    m = dict(norm_g=m_norm_g, w_in=m_w_in, q_norm_g=m_q_norm_g, k_norm_g=m_k_norm_g, sg_ln_g=m_sg_ln_g, sg_ln_b=m_sg_ln_b,
             w_s=m_w_s, b_s=m_b_s, mem_norm_g=m_mem_norm_g, w_mem_kv=m_w_mem_kv, w_br=m_w_br, w_out=m_w_out, final_g=m_final_g)
    v = dict(norm_g=v_norm_g, w_in=v_w_in, q_norm_g=v_q_norm_g, k_norm_g=v_k_norm_g, sg_ln_g=v_sg_ln_g, sg_ln_b=v_sg_ln_b,
             w_s=v_w_s, b_s=v_b_s, mem_norm_g=v_mem_norm_g, w_mem_kv=v_w_mem_kv, w_br=v_w_br, w_out=v_w_out, final_g=v_final_g)
    depth, d = norm_g.shape
    nsh = N_CHIPS
    br_rows = N_BRANCH * A_WIDTH
    br_cols = d // nsh

    shards = [[jnp.swapaxes(w_in[l], 0, 1).astype(BF16), w_mem_kv[l].astype(BF16), w_br[l].astype(BF16).reshape(br_rows, br_cols),
               w_out[l].astype(BF16)] for l in range(depth)]
    place = jnp.stack([2 * lax.axis_index("x") + lax.axis_index("y"), lax.axis_index("c")]).astype(jnp.int32)
    small = {n: w[n] for n in _SMALL}

    sq, dx, grads, reduced = local_fwd_bwd(x[0], mem[0], loss_target[0], small, shards=shards, place=place)
    loss = (0.5 / d) * lax.psum(sq, ("x", "y", "c"))

    small_sum, finals = finish_exchange(_pack(grads), [g for layer in reduced for g in layer])
    big_grads = dict(zip(("w_in", "w_mem_kv", "w_br", "w_out"), [finals[a::len(_BIG)] for a in range(len(_BIG))]))
    small_grads = _unpack(small_sum, small)

    out_g, out_d, out_m, out_v = {}, {}, {}, {}
    _, sd, sm, sv = adamw(_pack(small), [small_sum], _pack({n: m[n] for n in _SMALL}), _pack({n: v[n] for n in _SMALL}))
    sd, sm, sv = _unpack(sd, small), _unpack(sm, small), _unpack(sv, small)
    for n in _SMALL:
        out_g[n], out_d[n], out_m[n], out_v[n] = small_grads[n], sd[n], sm[n], sv[n]
    for n, gs in big_grads.items():
        into = (lambda a: jnp.swapaxes(a, 1, 2)) if n == "w_in" else (lambda a: a)
        two_d = lambda a: a.reshape(-1, gs[0].shape[-1])
        res = adamw(two_d(into(w[n])), gs, two_d(into(m[n])), two_d(into(v[n])))
        out_g[n], out_d[n], out_m[n], out_v[n] = [into(t.reshape(into(w[n]).shape)) for t in res]
    return (loss, dx[None], *[out_g[n] for n in _WEIGHTS], *[out_d[n] for n in _WEIGHTS], *[out_m[n] for n in _WEIGHTS],
            *[out_v[n] for n in _WEIGHTS])
```

```python
import functools

import jax
import jax.numpy as jnp
from jax import lax
from jax.experimental import pallas as pl
from jax.experimental.pallas import tpu as pltpu

F32 = jnp.float32
BF16 = jnp.bfloat16

D_MODEL = 1024
GRID_W = 64
CHUNK = 128
ROPE_THETA = 10000.0
EPS = 1e-6
A_HEADS, A_KV_HEADS, A_HEAD_DIM = 8, 2, 64
A_WIDTH, A_KV_WIDTH = 512, 128
B_GROUPS, B_GROUP_DIM, B_WIDTH = 4, 128, 512
M_HEADS, M_HEAD_DIM, M_WIDTH = 4, 128, 512
N_BRANCH = 3
IN_WIDTH = 6912
O_QA, O_KA, O_VA, O_ZA, O_UB, O_VB, O_ZB, O_QM, O_ZM, O_LG = 0, 512, 640, 768, 1280, 1792, 2304, 2816, 3328, 3840
PBLK = 768
N_PBLK = IN_WIDTH // PBLK
MID_W = 3072
LG_W = 3072

LN2 = 0.6931471805599453
Q_SCALE = A_HEAD_DIM ** -0.5 / LN2
VTE_ROWS = A_HEAD_DIM + 16

ADAM_LR, ADAM_B1, ADAM_B2, ADAM_EPS, ADAM_WD, ADAM_STEP = 0.001, 0.9, 0.999, 1e-08, 0.01, 10

V7X_VMEM_BYTES = 64 * 2**20
VMEM_LIMIT = V7X_VMEM_BYTES - 4 * 2**20
LANES = 128
MESH = pl.DeviceIdType.MESH
N_CHIPS = 4


def _cp(*sem):
    return pltpu.CompilerParams(dimension_semantics=sem if sem else None, vmem_limit_bytes=VMEM_LIMIT)


def _dot(a, b):
    return jnp.dot(a, b, preferred_element_type=F32)


def _dot_nt(a, b):
    return lax.dot_general(a, b, (((1,), (1,)), ((), ())), preferred_element_type=F32)


def _dot_tn(a, b):
    return lax.dot_general(a, b, (((0,), (0,)), ((), ())), preferred_element_type=F32)


def _dot_hi(a, b):
    return jnp.dot(a, b, preferred_element_type=F32, precision=lax.Precision.HIGHEST)


def _group_sum(a, ones):
    hi = a.astype(BF16)
    lo = (a - hi.astype(F32)).astype(BF16)
    return _dot(hi, ones) + _dot(lo, ones)


def _dot_nt_hi(a, b):
    return lax.dot_general(a, b, (((1,), (1,)), ((), ())), preferred_element_type=F32, precision=lax.Precision.HIGHEST)


def _sig(z):
    return 1.0 / (1.0 + jnp.exp(-z))


def _full(shape, once=False):
    nd = len(shape)
    return pl.BlockSpec(shape, lambda *_: (0,) * nd, pipeline_mode=pl.Buffered(1) if once else None)


def _rows(tm, width):
    return pl.BlockSpec((tm, width), lambda i: (i, 0))


def _sds(shape, dtype):
    return jax.ShapeDtypeStruct(shape, dtype)


def rms_fwd(x, g):
    s, d = x.shape
    tm = min(s, 512)

    def body(x_ref, g_ref, h_ref):
        xf = x_ref[...]
        r = lax.rsqrt(jnp.mean(xf * xf, axis=-1, keepdims=True) + EPS)
        h_ref[...] = ((xf * r) * g_ref[...]).astype(BF16)

    return pl.pallas_call(
        body, out_shape=_sds((s, d), BF16), grid=(s // tm,),
        in_specs=[_rows(tm, d), _full((1, d))], out_specs=_rows(tm, d),
        compiler_params=_cp("parallel"), name="rms_fwd")(x, g)


def proj_fwd(h, w_t):
    s, d = h.shape
    n = w_t.shape[0]
    tm = min(s, 1024)
    tn = 2304

    def body(h_ref, w_ref, o_ref):
        o_ref[...] = _dot_nt(h_ref[...], w_ref[...]).astype(BF16)

    return pl.pallas_call(
        body, out_shape=_sds((s, n), BF16), grid=(n // tn, s // tm),
        in_specs=[pl.BlockSpec((tm, d), lambda j, i: (i, 0)), pl.BlockSpec((tn, d), lambda j, i: (j, 0))],
        out_specs=pl.BlockSpec((tm, tn), lambda j, i: (i, j)),
        compiler_params=_cp("parallel", "parallel"), name="proj_fwd")(h, w_t)


def rope_tables(seq):
    n_freq = A_HEAD_DIM // 4
    d = jnp.arange(LANES) % A_HEAD_DIM
    seg, half, freq = d // (2 * n_freq), (d % (2 * n_freq)) // n_freq, d % n_freq
    inv = ROPE_THETA ** (-freq.astype(F32) / n_freq)
    t = jnp.arange(seq)
    pos = jnp.where(seg[None, :] == 0, (t // GRID_W)[:, None], (t % GRID_W)[:, None]).astype(F32)
    ang = pos * inv[None, :]
    cos, sin = jnp.cos(ang), jnp.sin(ang)
    return cos, jnp.where(half[None, :] == 1, sin, 0.0), jnp.where(half[None, :] == 0, -sin, 0.0)


def _group_ones(width, group):
    i = jnp.arange(width)
    return (i[:, None] // group == i[None, :] // group).astype(F32)


def _rope(xn, c, sa, sb):
    w = xn.shape[1]
    return xn * c + pltpu.roll(xn, 16, 1) * sa + pltpu.roll(xn, w - 16, 1) * sb


def _rope_t(dy, c, sa, sb):
    w = dy.shape[1]
    return dy * c + pltpu.roll(dy * sa, w - 16, 1) + pltpu.roll(dy * sb, 16, 1)


def _tile4(t):
    return jnp.concatenate([t, t, t, t], axis=1)


def qk_prep(proj, tabs, qg, kg, gq, gk):
    s = proj.shape[0]
    tm = min(s, 1024)
    c, sa, sb = tabs

    def body(p_ref, c_ref, sa_ref, sb_ref, qg_ref, kg_ref, gq_ref, gk_ref, qt_ref, kr_ref, vb_ref, kt_ref, v0_ref, v1_ref):
        xq = p_ref[:, O_QA:O_QA + A_WIDTH].astype(F32)
        xk = p_ref[:, O_KA:O_KA + A_KV_WIDTH].astype(F32)
        xv = p_ref[:, O_VA:O_VA + A_KV_WIDTH].astype(F32)
        cc, ssa, ssb = c_ref[...], sa_ref[...], sb_ref[...]
        msq = _group_sum(xq * xq, gq_ref[...]) * (1.0 / A_HEAD_DIM)
        qn = (xq * lax.rsqrt(msq + EPS)) * qg_ref[...]
        qr = _rope(qn, _tile4(cc), _tile4(ssa), _tile4(ssb)) * Q_SCALE
        qt_ref[...] = qr.T.astype(BF16)
        msk = _group_sum(xk * xk, gk_ref[...]) * (1.0 / A_HEAD_DIM)
        kn = (xk * lax.rsqrt(msk + EPS)) * kg_ref[...]
        kr = _rope(kn, cc, ssa, ssb)
        kr_ref[...] = kr.astype(BF16)
        vb_ref[...] = xv.astype(BF16)
        kt_ref[...] = kr.T.astype(BF16)
        vt = xv.T.astype(BF16)
        one = jnp.ones((VTE_ROWS - A_HEAD_DIM, tm), BF16)
        v0_ref[...] = jnp.concatenate([vt[:A_HEAD_DIM], one], axis=0)
        v1_ref[...] = jnp.concatenate([vt[A_HEAD_DIM:], one], axis=0)

    tab = _rows(tm, LANES)
    colb = lambda w: pl.BlockSpec((w, tm), lambda i: (0, i))
    return pl.pallas_call(
        body,
        out_shape=(_sds((A_WIDTH, s), BF16), _sds((s, A_KV_WIDTH), BF16), _sds((s, A_KV_WIDTH), BF16),
                   _sds((A_KV_WIDTH, s), BF16), _sds((VTE_ROWS, s), BF16), _sds((VTE_ROWS, s), BF16)),
        grid=(s // tm,),
        in_specs=[_rows(tm, PBLK), tab, tab, tab, _full((1, A_WIDTH)), _full((1, A_KV_WIDTH)),
                  _full((A_WIDTH, A_WIDTH)), _full((A_KV_WIDTH, A_KV_WIDTH))],
        out_specs=(colb(A_WIDTH), _rows(tm, A_KV_WIDTH), _rows(tm, A_KV_WIDTH), colb(A_KV_WIDTH), colb(VTE_ROWS), colb(VTE_ROWS)),
        compiler_params=_cp("parallel"), name="qk_prep")(proj, c, sa, sb, qg, kg, gq, gk)


def _pad_head(q_h, kv):
    z = jnp.zeros_like(q_h)
    return jnp.concatenate([q_h, z], axis=0) if kv == 0 else jnp.concatenate([z, q_h], axis=0)


def attn_fwd(q_t, kr, vte0, vte1, gather=()):
    s = kr.shape[0]
    tq = min(s, 512)
    kc = min(s, 256)
    nkc = s // kc
    nq = s // tq
    grp = A_HEADS // A_KV_HEADS
    ng = len(gather)

    def body(qt_ref, kr_ref, v0_ref, v1_ref, *rest):
        g_in, (o_ref, lse_ref), g_out = rest[:ng], rest[ng:ng + 2], rest[ng + 2:2 * ng + 2]
        qp_ref, m_ref, acc_ref = rest[2 * ng + 2:2 * ng + 5]
        if ng:
            start, forward, finish = gather_stages([g.shape for g in gather], g_in, g_out, *rest[2 * ng + 5:])
            pl.when(pl.program_id(0) == 0)(start)
            pl.when(pl.program_id(0) == (3 * nq) // 4)(forward)

        for h in range(A_HEADS):
            qp_ref[h] = _pad_head(qt_ref[A_HEAD_DIM * h:A_HEAD_DIM * (h + 1), :], h // grp)
        m_ref[...] = jnp.full(m_ref.shape, -1e30, F32)
        acc_ref[...] = jnp.zeros_like(acc_ref)

        def step(ci, carry):
            ks = pl.ds(pl.multiple_of(ci * kc, kc), kc)
            kblk = kr_ref[ks, :]
            vts = (v0_ref[:, ks], v1_ref[:, ks])
            scs = [_dot(kblk, qp_ref[h]) for h in range(A_HEADS)]
            for h in range(A_HEADS):
                sc = scs[h]
                m_prev = m_ref[h:h + 1, :]
                m_new = jnp.maximum(m_prev, jnp.max(sc, axis=0, keepdims=True))
                p = jnp.exp2(sc - m_new)
                acc_ref[h] = acc_ref[h] * jnp.exp2(m_prev - m_new) + _dot(vts[h // grp], p.astype(BF16))
                m_ref[h:h + 1, :] = m_new
            return carry

        lax.fori_loop(0, nkc, step, 0)
        outs, lses = [], []
        for h in range(A_HEADS):
            acc = acc_ref[h]
            l = acc[A_HEAD_DIM:A_HEAD_DIM + 1, :]
            outs.append(acc[:A_HEAD_DIM, :] / l)
            lses.append(m_ref[h:h + 1, :] + jnp.log2(l))
        o_ref[...] = jnp.concatenate(outs, axis=0).T
        lse_ref[...] = jnp.concatenate(lses, axis=0)
        if ng:
            pl.when(pl.program_id(0) == nq - 1)(finish)

    out = pl.pallas_call(
        body,
        out_shape=(_sds((s, A_WIDTH), F32), _sds((A_HEADS, s), F32)) + tuple(_sds((N_CHIPS,) + g.shape, g.dtype) for g in gather),
        grid=(nq,),
        in_specs=[pl.BlockSpec((A_WIDTH, tq), lambda i: (0, i)), _full((s, A_KV_WIDTH)), _full((VTE_ROWS, s)),
                  _full((VTE_ROWS, s))] + [_ANY] * ng,
        out_specs=(_rows(tq, A_WIDTH), pl.BlockSpec((A_HEADS, tq), lambda i: (0, i))) + (_ANY,) * ng,
        scratch_shapes=[pltpu.VMEM((A_HEADS, A_KV_WIDTH, tq), BF16), pltpu.VMEM((A_HEADS, tq), F32),
                        pltpu.VMEM((A_HEADS, VTE_ROWS, tq), F32)] + (gather_sems(ng) if ng else []),
        compiler_params=_cp("arbitrary"), name="attn_fwd_gather" if ng else "attn_fwd")(q_t, kr, vte0, vte1, *gather)
    return out[0], out[1], list(out[2:])


def memkv_fwd(mem, g, w_kv):
    m, d = mem.shape

    def body(mem_ref, g_ref, w_ref, mn_ref, kv_ref):
        mf = mem_ref[...]
        r = lax.rsqrt(jnp.mean(mf * mf, axis=-1, keepdims=True) + EPS)
        mn = ((mf * r) * g_ref[...]).astype(BF16)
        mn_ref[...] = mn
        kv_ref[...] = _dot(mn, w_ref[...]).astype(BF16)

    return pl.pallas_call(
        body, out_shape=(_sds((m, d), BF16), _sds((m, 2 * M_WIDTH), BF16)),
        compiler_params=_cp(), name="memkv_fwd")(mem, g, w_kv)


def _layer_norm_stats(v):
    mu = jnp.mean(v, axis=-1, keepdims=True)
    xc = v - mu
    rstd = lax.rsqrt(jnp.mean(xc * xc, axis=-1, keepdims=True) + EPS)
    return xc * rstd, rstd


def _spatial_mix(vlb, ws_ref, bsb_ref, tm):
    rows = []
    for ci in range(tm // CHUNK):
        cols = []
        for g in range(B_GROUPS):
            blk = vlb[ci * CHUNK:(ci + 1) * CHUNK, g * B_GROUP_DIM:(g + 1) * B_GROUP_DIM]
            cols.append(_dot(ws_ref[g], blk) + bsb_ref[g])
        rows.append(jnp.concatenate(cols, axis=1))
    return jnp.concatenate(rows, axis=0)


def _mem_attn(qm, kv_ref):
    out = []
    for h in range(M_HEADS):
        qh = qm[:, h * M_HEAD_DIM:(h + 1) * M_HEAD_DIM].astype(BF16)
        kh = kv_ref[:, h * M_HEAD_DIM:(h + 1) * M_HEAD_DIM]
        vh = kv_ref[:, M_WIDTH + h * M_HEAD_DIM:M_WIDTH + (h + 1) * M_HEAD_DIM]
        sc = _dot_nt(qh, kh) * (M_HEAD_DIM ** -0.5)
        e = jnp.exp(sc - jnp.max(sc, axis=-1, keepdims=True))
        p = e / jnp.sum(e, axis=-1, keepdims=True)
        out.append((p, _dot(p.astype(BF16), vh)))
    return out


def branch_fwd(x, proj, o_a, kv, ws, bsb, ln_g, ln_b, w_br, w_out, next_g):
    s, d = x.shape
    tm = min(s, 512)

    def body(x_ref, p_ref, oa_ref, kv_ref, ws_ref, bsb_ref, lg_ref, lb_ref, wbr_ref, wo_ref, ng_ref,
             xn_ref, y_ref, up_ref, mg_ref, hn_ref):
        seg = lambda o, w: p_ref[:, o:o + w].astype(F32)
        z_a, u_b, v_b, z_b = seg(O_ZA, A_WIDTH), seg(O_UB, B_WIDTH), seg(O_VB, B_WIDTH), seg(O_ZB, B_WIDTH)
        q_m, z_m = seg(O_QM, M_WIDTH), seg(O_ZM, M_WIDTH)
        xhat, _ = _layer_norm_stats(v_b)
        vln = xhat * lg_ref[...] + lb_ref[...]
        mixed = _spatial_mix(vln.astype(BF16), ws_ref, bsb_ref, tm)
        y_b = (u_b * mixed) * (z_b * _sig(z_b))
        o_m = jnp.concatenate([o for _, o in _mem_attn(q_m, kv_ref)], axis=1)
        y_a = oa_ref[...] * (z_a * _sig(z_a))
        y_m = o_m * (z_m * _sig(z_m))
        merged = None
        for n, yy in enumerate((y_a, y_b, y_m)):
            yb = yy.astype(BF16)
            y_ref[n] = yb
            up = jnp.concatenate([_dot(yb, wbr_ref[c, n]) for c in range(N_CHIPS)], axis=1)
            up_ref[n] = up.astype(BF16)
            t = _sig(seg(O_LG + n * d, d)) * up
            merged = t if merged is None else merged + t
        mb = merged.astype(BF16)
        mg_ref[...] = mb
        xn = x_ref[...] + _dot(mb, wo_ref[...])
        xn_ref[...] = xn
        r = lax.rsqrt(jnp.mean(xn * xn, axis=-1, keepdims=True) + EPS)
        hn_ref[...] = ((xn * r) * ng_ref[...]).astype(BF16)

    return pl.pallas_call(
        body,
        out_shape=(_sds((s, d), F32), _sds((N_BRANCH, s, A_WIDTH), BF16), _sds((N_BRANCH, s, d), BF16), _sds((s, d), BF16),
                   _sds((s, d), BF16)),
        grid=(s // tm,),
        in_specs=[_rows(tm, d), _rows(tm, IN_WIDTH), _rows(tm, A_WIDTH), _full(kv.shape), _full(ws.shape), _full(bsb.shape),
                  _full((1, B_WIDTH)), _full((1, B_WIDTH)), _full(w_br.shape), _full(w_out.shape), _full((1, d))],
        out_specs=(_rows(tm, d), pl.BlockSpec((N_BRANCH, tm, A_WIDTH), lambda i: (0, i, 0)),
                   pl.BlockSpec((N_BRANCH, tm, d), lambda i: (0, i, 0)), _rows(tm, d), _rows(tm, d)),
        compiler_params=_cp("parallel"), name="branch_fwd")(x, proj, o_a, kv, ws, bsb, ln_g, ln_b, w_br, w_out, next_g)


def final_loss(x, fg, tgt):
    s, d = x.shape
    tm = min(s, 512)

    def body(x_ref, g_ref, t_ref, ls_ref, dx_ref, gg_ref):
        @pl.when(pl.program_id(0) == 0)
        def _():
            ls_ref[...] = jnp.zeros_like(ls_ref)
            gg_ref[...] = jnp.zeros_like(gg_ref)

        xf = x_ref[...]
        g = g_ref[...]
        r = lax.rsqrt(jnp.mean(xf * xf, axis=-1, keepdims=True) + EPS)
        xh = xf * r
        e = xh * g - t_ref[...]
        sq = jnp.sum(jnp.sum(e * e, axis=0, keepdims=True), axis=1, keepdims=True)
        ls_ref[...] += jnp.broadcast_to(sq, ls_ref.shape)
        dy = e * (1.0 / d)
        gg_ref[...] += jnp.sum(dy * xh, axis=0, keepdims=True)
        gy = dy * g
        dx_ref[...] = r * (gy - xh * jnp.mean(gy * xh, axis=-1, keepdims=True))

    return pl.pallas_call(
        body, out_shape=(_sds((1, LANES), F32), _sds((s, d), F32), _sds((1, d), F32)), grid=(s // tm,),
        in_specs=[_rows(tm, d), _full((1, d)), _rows(tm, d)],
        out_specs=(_full((1, LANES)), _rows(tm, d), _full((1, d))),
        compiler_params=_cp("arbitrary"), name="final_loss")(x, fg, tgt)


def _pblocks(tm, first, count):
    return [pl.BlockSpec((tm, PBLK), functools.partial(lambda i, b: (i, b), b=first + k)) for k in range(count)]


def merge_bwd(dx, proj, y, up, merged, w_br, w_out):
    s, d = dx.shape
    tm = min(s, 512)
    nlg = LG_W // PBLK
    cw = d // N_CHIPS

    def body(dx_ref, l0, l1, l2, l3, y_ref, up_ref, mg_ref, wbr_ref, wo_ref, dy_ref, dlg_ref, gwo_ref, gwb_ref, gwo16_ref, gwb16_ref):
        @pl.when(pl.program_id(0) == 0)
        def _():
            gwo_ref[...] = jnp.zeros_like(gwo_ref)
            gwb_ref[...] = jnp.zeros_like(gwb_ref)

        dxb = dx_ref[...].astype(BF16)
        dmg = _dot_nt(dxb, wo_ref[...])
        gwo_ref[...] += _dot_tn(mg_ref[...], dxb)
        lg = jnp.concatenate([l0[...], l1[...], l2[...], l3[...]], axis=1).astype(F32)
        for n in range(N_BRANCH):
            g = _sig(lg[:, n * d:(n + 1) * d])
            dup = dmg * g
            dlg_ref[:, n * d:(n + 1) * d] = ((dup * up_ref[n].astype(F32)) * (1.0 - g)).astype(BF16)
            dupb = dup.astype(BF16)
            dyn = None
            for c in range(N_CHIPS):
                blk = dupb[:, c * cw:(c + 1) * cw]
                gwb_ref[c, n] += _dot_tn(y_ref[n], blk)
                t = _dot_nt(blk, wbr_ref[c, n])
                dyn = t if dyn is None else dyn + t
            dy_ref[n] = dyn.astype(BF16)

        @pl.when(pl.program_id(0) == pl.num_programs(0) - 1)
        def _():
            gwo16_ref[...] = gwo_ref[...].astype(BF16)
            gwb16_ref[...] = gwb_ref[...].astype(BF16)

    return pl.pallas_call(
        body,
        out_shape=(_sds((N_BRANCH, s, A_WIDTH), BF16), _sds((s, LG_W), BF16), _sds((d, d), F32), _sds(w_br.shape, F32),
                   _sds((d, d), BF16), _sds(w_br.shape, BF16)),
        grid=(s // tm,),
        in_specs=[_rows(tm, d)] + _pblocks(tm, O_LG // PBLK, nlg) + [
            pl.BlockSpec((N_BRANCH, tm, A_WIDTH), lambda i: (0, i, 0)), pl.BlockSpec((N_BRANCH, tm, d), lambda i: (0, i, 0)),
            _rows(tm, d), _full(w_br.shape, once=True), _full(w_out.shape, once=True)],
        out_specs=(pl.BlockSpec((N_BRANCH, tm, A_WIDTH), lambda i: (0, i, 0)), _rows(tm, LG_W), _full((d, d)), _full(w_br.shape),
                   _full((d, d)), _full(w_br.shape)),
        compiler_params=_cp("arbitrary"), name="merge_bwd")(dx, proj, proj, proj, proj, y, up, merged, w_br, w_out)


def _dsilu(z, sg):
    return sg * (1.0 + z * (1.0 - sg))


def branch_bwd(dy, proj, o_a, kv, ws, ws_t, bsb, ln_g, ln_b, head_sel):
    s = proj.shape[0]
    tm = min(s, 512)
    nmid = MID_W // PBLK

    def body(dy_ref, m0, m1, m2, m3, oa_ref, kv_ref, ws_ref, wst_ref, bsb_ref, lg_ref, lb_ref, sel_ref,
             dmid_ref, dot_ref, dl_ref, gws_ref, gbs_ref, glg_ref, glb_ref, dkv_ref):
        @pl.when(pl.program_id(0) == 0)
        def _():
            for r in (gws_ref, gbs_ref, glg_ref, glb_ref, dkv_ref):
                r[...] = jnp.zeros_like(r)

        mid = jnp.concatenate([m0[...], m1[...], m2[...], m3[...]], axis=1).astype(F32)
        seg = lambda o, w: mid[:, o - O_ZA:o - O_ZA + w]
        z_a, u_b, v_b, z_b = seg(O_ZA, A_WIDTH), seg(O_UB, B_WIDTH), seg(O_VB, B_WIDTH), seg(O_ZB, B_WIDTH)
        q_m, z_m = seg(O_QM, M_WIDTH), seg(O_ZM, M_WIDTH)

        def put(o, v):
            dmid_ref[:, o - O_ZA:o - O_ZA + v.shape[1]] = v.astype(BF16)

        dy_a, dy_b, dy_m = dy_ref[0].astype(F32), dy_ref[1].astype(F32), dy_ref[2].astype(F32)

        o_a_ = oa_ref[...]
        sg = _sig(z_a)
        do_a = dy_a * (z_a * sg)
        put(O_ZA, (dy_a * o_a_) * _dsilu(z_a, sg))
        do_l = do_a * LN2
        dot_ref[...] = do_l.T.astype(BF16)
        dl_ref[...] = _dot_nt_hi(sel_ref[...], do_l * o_a_)

        xhat, rstd = _layer_norm_stats(v_b)
        lng = lg_ref[...]
        vln = xhat * lng + lb_ref[...]
        vlb = vln.astype(BF16)
        mixed = _spatial_mix(vlb, ws_ref, bsb_ref, tm)
        sg = _sig(z_b)
        sl = z_b * sg
        put(O_UB, (dy_b * mixed) * sl)
        put(O_ZB, ((dy_b * u_b) * mixed) * _dsilu(z_b, sg))
        dmix = (dy_b * u_b) * sl
        dmb = dmix.astype(BF16)
        rows = []
        for ci in range(tm // CHUNK):
            cols = []
            for g in range(B_GROUPS):
                rs, cs = slice(ci * CHUNK, (ci + 1) * CHUNK), slice(g * B_GROUP_DIM, (g + 1) * B_GROUP_DIM)
                gws_ref[g] += _dot_nt(dmb[rs, cs], vlb[rs, cs])
                gbs_ref[g] += jnp.broadcast_to(jnp.sum(dmix[rs, cs], axis=1, keepdims=True), (CHUNK, B_GROUP_DIM))
                cols.append(_dot(wst_ref[g], dmb[rs, cs]))
            rows.append(jnp.concatenate(cols, axis=1))
        dvln = jnp.concatenate(rows, axis=0)
        glg_ref[...] += jnp.sum(dvln * xhat, axis=0, keepdims=True)
        glb_ref[...] += jnp.sum(dvln, axis=0, keepdims=True)
        gy = dvln * lng
        put(O_VB, rstd * ((gy - jnp.mean(gy, axis=-1, keepdims=True)) - xhat * jnp.mean(gy * xhat, axis=-1, keepdims=True)))

        sg = _sig(z_m)
        sl = z_m * sg
        heads = _mem_attn(q_m, kv_ref)
        o_m = jnp.concatenate([o for _, o in heads], axis=1)
        put(O_ZM, (dy_m * o_m) * _dsilu(z_m, sg))
        do_m = dy_m * sl
        dqs = []
        for h, (p, o_h) in enumerate(heads):
            hs = slice(h * M_HEAD_DIM, (h + 1) * M_HEAD_DIM)
            vs = slice(M_WIDTH + h * M_HEAD_DIM, M_WIDTH + (h + 1) * M_HEAD_DIM)
            do_h = do_m[:, hs]
            dob = do_h.astype(BF16)
            dp = _dot_nt(dob, kv_ref[:, vs])
            dsc = (p * (dp - jnp.sum(do_h * o_h, axis=-1, keepdims=True))) * (M_HEAD_DIM ** -0.5)
            dsb = dsc.astype(BF16)
            dqs.append(_dot(dsb, kv_ref[:, hs]))
            dkv_ref[:, hs] += _dot_tn(dsb, q_m[:, hs].astype(BF16))
            dkv_ref[:, vs] += _dot_tn(p.astype(BF16), dob)
        put(O_QM, jnp.concatenate(dqs, axis=1))

    return pl.pallas_call(
        body,
        out_shape=(_sds((s, MID_W), BF16), _sds((A_WIDTH, s), BF16), _sds((A_HEADS, s), F32), _sds(ws.shape, F32),
                   _sds(ws.shape, F32), _sds((1, B_WIDTH), F32), _sds((1, B_WIDTH), F32), _sds(kv.shape, F32)),
        grid=(s // tm,),
        in_specs=[pl.BlockSpec((N_BRANCH, tm, A_WIDTH), lambda i: (0, i, 0))] + _pblocks(tm, O_ZA // PBLK, nmid) + [
            _rows(tm, A_WIDTH), _full(kv.shape), _full(ws.shape), _full(ws.shape), _full(bsb.shape),
            _full((1, B_WIDTH)), _full((1, B_WIDTH)), _full(head_sel.shape)],
        out_specs=(_rows(tm, MID_W), pl.BlockSpec((A_WIDTH, tm), lambda i: (0, i)), pl.BlockSpec((A_HEADS, tm), lambda i: (0, i)),
                   _full(ws.shape), _full(ws.shape), _full((1, B_WIDTH)), _full((1, B_WIDTH)), _full(kv.shape)),
        compiler_params=_cp("arbitrary"), name="branch_bwd")(dy, proj, proj, proj, proj, o_a, kv, ws, ws_t, bsb, ln_g, ln_b, head_sel)


def attn_bwd(q_t, do_t, kr, kr_t, vb, lse, delta, scatter=()):
    s = kr.shape[0]
    tq = min(s, 256)
    kc = min(s, 512)
    nkc = s // kc
    nq = s // tq
    grp = A_HEADS // A_KV_HEADS
    ns = len(scatter)
    na = ns // 2

    def body(qt_ref, dot_ref, kr_ref, krt_ref, vb_ref, lse_ref, dl_ref, *rest):
        s_in, (dqt_ref, dk_ref, dv_ref), s_out = rest[:ns], rest[ns:ns + 3], rest[ns + 3:2 * ns + 3]
        qp_ref, dop_ref, dq_ref = rest[2 * ns + 3:2 * ns + 6]
        if ns:
            start, finish = scatter_stages([g.shape[1:] for g in scatter[:na]], s_in[:na], s_in[na:], s_out[:na], s_out[na:],
                                           *rest[2 * ns + 6:])
            pl.when(pl.program_id(0) == 0)(start)

        @pl.when(pl.program_id(0) == 0)
        def _():
            dk_ref[...] = jnp.zeros_like(dk_ref)
            dv_ref[...] = jnp.zeros_like(dv_ref)

        for h in range(A_HEADS):
            hs = slice(A_HEAD_DIM * h, A_HEAD_DIM * (h + 1))
            qp_ref[h] = _pad_head(qt_ref[hs, :], h // grp)
            dop_ref[h] = _pad_head(dot_ref[hs, :], h // grp)
        dq_ref[...] = jnp.zeros_like(dq_ref)

        def step(ci, carry):
            ks = pl.ds(pl.multiple_of(ci * kc, kc), kc)
            kblk, vblk, ktb = kr_ref[ks, :], vb_ref[ks, :], krt_ref[:, ks]
            dv_acc = jnp.zeros((kc, A_KV_WIDTH), F32)
            dk_acc = jnp.zeros((kc, A_KV_WIDTH), F32)
            scs = [_dot(kblk, qp_ref[h]) for h in range(A_HEADS)]
            dps = [_dot(vblk, dop_ref[h]) for h in range(A_HEADS)]
            for h in range(A_HEADS):
                qpad, dopad = qp_ref[h], dop_ref[h]
                p = jnp.exp2(scs[h] - lse_ref[h:h + 1, :])
                dsb = (p * (dps[h] - dl_ref[h:h + 1, :])).astype(BF16)
                dv_acc = dv_acc + _dot_nt(p.astype(BF16), dopad)
                dk_acc = dk_acc + _dot_nt(dsb, qpad)
                dq_ref[h] += _dot(ktb, dsb)
            dv_ref[ks, :] += dv_acc
            dk_ref[ks, :] += dk_acc
            return carry

        lax.fori_loop(0, nkc, step, 0)
        dqt_ref[...] = jnp.concatenate(
            [dq_ref[h][A_HEAD_DIM * (h // grp):A_HEAD_DIM * (h // grp + 1), :] for h in range(A_HEADS)], axis=0)
        if ns:
            pl.when(pl.program_id(0) == nq - 1)(finish)

    colq = pl.BlockSpec((A_WIDTH, tq), lambda i: (0, i))
    colh = pl.BlockSpec((A_HEADS, tq), lambda i: (0, i))
    out = pl.pallas_call(
        body,
        out_shape=(_sds((A_WIDTH, s), F32), _sds((s, A_KV_WIDTH), F32), _sds((s, A_KV_WIDTH), F32)) + scatter_out_shapes(scatter[:na]),
        grid=(nq,),
        in_specs=[colq, colq, _full((s, A_KV_WIDTH)), _full((A_KV_WIDTH, s)), _full((s, A_KV_WIDTH)), colh, colh] + [_ANY] * ns,
        out_specs=(colq, _full((s, A_KV_WIDTH)), _full((s, A_KV_WIDTH))) + (_ANY,) * ns,
        scratch_shapes=[pltpu.VMEM((A_HEADS, A_KV_WIDTH, tq), BF16), pltpu.VMEM((A_HEADS, A_KV_WIDTH, tq), BF16),
                        pltpu.VMEM((A_HEADS, A_KV_WIDTH, tq), F32)] + (scatter_sems(na) if ns else []),
        compiler_params=_cp("arbitrary"), name="attn_bwd_scatter" if ns else "attn_bwd")(
            q_t, do_t, kr, kr_t, vb, lse, delta, *scatter)
    return out[0], out[1], out[2], list(out[3:3 + na]), list(out[3 + na:])


def qk_prep_bwd(proj, dq_t, dkr, dvb, tabs, qg, kg, gq, gk, fold_q, fold_k):
    s = proj.shape[0]
    tm = min(s, 1024)
    c, sa, sb = tabs

    def head_norm_bwd(x, dn, gain, gones, fold):
        ms = _group_sum(x * x, gones) * (1.0 / A_HEAD_DIM)
        r = lax.rsqrt(ms + EPS)
        xh = x * r
        gg = _dot_hi(jnp.sum(dn * xh, axis=0, keepdims=True), fold)
        u = dn * gain
        mean_u = _group_sum(u * xh, gones) * (1.0 / A_HEAD_DIM)
        return r * (u - xh * mean_u), gg

    def body(p_ref, dqt_ref, dk_ref, dv_ref, c_ref, sa_ref, sb_ref, qg_ref, kg_ref, gq_ref, gk_ref, fq_ref, fk_ref,
             dqkv_ref, gqg_ref, gkg_ref):
        @pl.when(pl.program_id(0) == 0)
        def _():
            gqg_ref[...] = jnp.zeros_like(gqg_ref)
            gkg_ref[...] = jnp.zeros_like(gkg_ref)

        cc, ssa, ssb = c_ref[...], sa_ref[...], sb_ref[...]
        dqr = dqt_ref[...].T * Q_SCALE
        dqn = _rope_t(dqr, _tile4(cc), _tile4(ssa), _tile4(ssb))
        dxq, gq_ = head_norm_bwd(p_ref[:, O_QA:O_QA + A_WIDTH].astype(F32), dqn, qg_ref[...], gq_ref[...], fq_ref[...])
        dkn = _rope_t(dk_ref[...], cc, ssa, ssb)
        dxk, gk_ = head_norm_bwd(p_ref[:, O_KA:O_KA + A_KV_WIDTH].astype(F32), dkn, kg_ref[...], gk_ref[...], fk_ref[...])
        gqg_ref[...] += gq_
        gkg_ref[...] += gk_
        dqkv_ref[:, O_QA:O_QA + A_WIDTH] = dxq.astype(BF16)
        dqkv_ref[:, O_KA:O_KA + A_KV_WIDTH] = dxk.astype(BF16)
        dqkv_ref[:, O_VA:O_VA + A_KV_WIDTH] = (dv_ref[...] * (1.0 / LN2)).astype(BF16)

    tab = _rows(tm, LANES)
    return pl.pallas_call(
        body, out_shape=(_sds((s, PBLK), BF16), _sds((1, LANES), F32), _sds((1, LANES), F32)), grid=(s // tm,),
        in_specs=[_rows(tm, PBLK), pl.BlockSpec((A_WIDTH, tm), lambda i: (0, i)), _rows(tm, A_KV_WIDTH), _rows(tm, A_KV_WIDTH),
                  tab, tab, tab, _full((1, A_WIDTH)), _full((1, A_KV_WIDTH)), _full((A_WIDTH, A_WIDTH)),
                  _full((A_KV_WIDTH, A_KV_WIDTH)), _full((A_WIDTH, LANES)), _full((A_KV_WIDTH, LANES))],
        out_specs=(_rows(tm, PBLK), _full((1, LANES)), _full((1, LANES))),
        compiler_params=_cp("arbitrary"), name="qk_prep_bwd")(proj, dq_t, dkr, dvb, c, sa, sb, qg, kg, gq, gk, fold_q, fold_k)


def _pick_dproj(b, d0, d1, d2, use):
    first_lg = 1 + MID_W // PBLK

    @pl.when(b == 0)
    def _():
        use(d0[...])

    @pl.when(jnp.logical_and(b >= 1, b < first_lg))
    def _():
        use(d1[...])

    @pl.when(b >= first_lg)
    def _():
        use(d2[...])


def win_grad(d0, d1, d2, h):
    s, d = h.shape
    tk = min(s, 4096)
    nk = s // tk

    def body(d0_ref, d1_ref, d2_ref, h_ref, o_ref, o16_ref):
        @pl.when(pl.program_id(1) == 0)
        def _():
            o_ref[...] = jnp.zeros_like(o_ref)

        def use(blk):
            o_ref[...] += _dot_tn(blk, h_ref[...])

        _pick_dproj(pl.program_id(0), d0_ref, d1_ref, d2_ref, use)

        @pl.when(pl.program_id(1) == nk - 1)
        def _():
            o16_ref[...] = o_ref[...].astype(BF16)

    def spec(first, count):
        def imap(j, k):
            used = jnp.logical_and(j >= first, j < first + count)
            return (jnp.where(used, k, 0), jnp.clip(j - first, 0, count - 1))
        return pl.BlockSpec((tk, PBLK), imap)

    nm = MID_W // PBLK
    oblk = pl.BlockSpec((PBLK, d), lambda j, k: (j, 0))
    return pl.pallas_call(
        body, out_shape=(_sds((IN_WIDTH, d), F32), _sds((IN_WIDTH, d), BF16)), grid=(N_PBLK, nk),
        in_specs=[spec(0, 1), spec(1, nm), spec(1 + nm, LG_W // PBLK),
                  pl.BlockSpec((tk, d), lambda j, k: (k, 0), pipeline_mode=pl.Buffered(1) if nk == 1 else None)],
        out_specs=(oblk, oblk),
        compiler_params=_cp("parallel", "arbitrary"), name="win_grad")(d0, d1, d2, h)


def h_bwd(d0, d1, d2, w_t, x, dx_out, g, scatter=()):
    s, d = x.shape
    tm = min(s, 512)
    nt = s // tm
    ns = len(scatter)
    na = ns // 2

    def body(d0_ref, d1_ref, d2_ref, w_ref, x_ref, dxo_ref, g_ref, *rest):
        s_in, (dx_ref, gg_ref), s_out = rest[:ns], rest[ns:ns + 2], rest[ns + 2:2 * ns + 2]
        if ns:
            start, finish = scatter_stages([a.shape[1:] for a in scatter[:na]], s_in[:na], s_in[na:], s_out[:na], s_out[na:],
                                           *rest[2 * ns + 2:])
            pl.when(pl.program_id(0) == 0)(start)

        @pl.when(pl.program_id(0) == 0)
        def _():
            gg_ref[...] = jnp.zeros_like(gg_ref)

        dh = (_dot(d0_ref[...], w_ref[0:PBLK, :]) + _dot(d1_ref[...], w_ref[PBLK:PBLK + MID_W, :])
              + _dot(d2_ref[...], w_ref[PBLK + MID_W:, :]))
        xf = x_ref[...]
        r = lax.rsqrt(jnp.mean(xf * xf, axis=-1, keepdims=True) + EPS)
        xh = xf * r
        gg_ref[...] += jnp.sum(dh * xh, axis=0, keepdims=True)
        u = dh * g_ref[...]
        dx_ref[...] = dxo_ref[...] + r * (u - xh * jnp.mean(u * xh, axis=-1, keepdims=True))
        if ns:
            pl.when(pl.program_id(0) == nt - 1)(finish)

    rowb = _rows(tm, d)
    out = pl.pallas_call(
        body, out_shape=(_sds((s, d), F32), _sds((1, d), F32)) + scatter_out_shapes(scatter[:na]), grid=(nt,),
        in_specs=[_rows(tm, PBLK), _rows(tm, MID_W), _rows(tm, LG_W),
                  pl.BlockSpec(w_t.shape, lambda i: (0, 0), pipeline_mode=pl.Buffered(1)), rowb, rowb, _full((1, d))] + [_ANY] * ns,
        out_specs=(rowb, _full((1, d))) + (_ANY,) * ns,
        scratch_shapes=scatter_sems(na) if ns else [],
        compiler_params=_cp("arbitrary"), name="h_bwd_scatter" if ns else "h_bwd")(d0, d1, d2, w_t, x, dx_out, g, *scatter)
    return out[0], out[1], list(out[2:2 + na]), list(out[2 + na:])


def memkv_bwd(mem, g, mem_n, w_kv, dkv):
    m, d = mem.shape

    def body(mem_ref, g_ref, mn_ref, w_ref, dkv_ref, gw_ref, gw16_ref, gg_ref):
        dkb = dkv_ref[...].astype(BF16)
        gw = _dot_tn(mn_ref[...], dkb)
        gw_ref[...] = gw
        gw16_ref[...] = gw.astype(BF16)
        dmn = _dot_nt(dkb, w_ref[...])
        mf = mem_ref[...]
        r = lax.rsqrt(jnp.mean(mf * mf, axis=-1, keepdims=True) + EPS)
        gg_ref[...] = jnp.sum(dmn * (mf * r), axis=0, keepdims=True)

    return pl.pallas_call(
        body, out_shape=(_sds(w_kv.shape, F32), _sds(w_kv.shape, BF16), _sds((1, d), F32)),
        compiler_params=_cp(), name="memkv_bwd")(mem, g, mem_n, w_kv, dkv)


def _layer_consts(seq):
    i = jnp.arange(A_WIDTH)
    return dict(
        tabs=rope_tables(seq),
        gq=_group_ones(A_WIDTH, A_HEAD_DIM).astype(BF16), gk=_group_ones(A_KV_WIDTH, A_HEAD_DIM).astype(BF16),
        fold_q=(i[:, None] % A_HEAD_DIM == jnp.arange(LANES)[None, :]).astype(F32),
        fold_k=(i[:A_KV_WIDTH, None] % A_HEAD_DIM == jnp.arange(LANES)[None, :]).astype(F32),
        head_sel=(jnp.arange(A_HEADS)[:, None] == i[None, :] // A_HEAD_DIM).astype(F32),
    )


_BIG = ("win_t", "wkv", "wbr", "wout")


def _with_own_part(names, gathered, shards, chip, d):
    shape = dict(win_t=(IN_WIDTH, d), wkv=(d, 2 * M_WIDTH), wbr=(N_CHIPS, N_BRANCH, A_WIDTH, d // N_CHIPS), wout=(d, d))
    return {n: lax.dynamic_update_slice(g, sh[None], (chip, 0, 0)).reshape(shape[n]) for n, g, sh in zip(names, gathered, shards)}


def local_fwd_bwd(x, mem, tgt, small, big=None, shards=None, place=None):
    s, d = x.shape
    depth = small["norm_g"].shape[0]
    k = _layer_consts(s)
    row = lambda v: v.reshape(1, -1)
    dist = shards is not None
    if dist:
        big = [_with_own_part(_BIG[:1], allgather_layer(shards[0][:1]), shards[0][:1], place[0], d)] + [None] * (depth - 1)
    saved = []
    for l in range(depth):
        ng = row(small["norm_g"][l])
        qg = row(jnp.tile(small["q_norm_g"][l], A_HEADS))
        kg = row(jnp.tile(small["k_norm_g"][l], A_KV_HEADS))
        ws = small["w_s"][l].astype(BF16)
        ws_t = jnp.swapaxes(small["w_s"][l], 1, 2).astype(BF16)
        bsb = jnp.broadcast_to(small["b_s"][l][:, :, None], (B_GROUPS, CHUNK, B_GROUP_DIM))
        lng, lnb = row(small["sg_ln_g"][l]), row(small["sg_ln_b"][l])
        mg = row(small["mem_norm_g"][l])
        w = big[l]
        h = rms_fwd(x, ng) if l == 0 else h_next
        proj = proj_fwd(h, w["win_t"])
        q_t, kr, vb, kr_t, vte0, vte1 = qk_prep(proj, k["tabs"], qg, kg, k["gq"], k["gk"])
        late = list(shards[0][1:]) if dist and l == 0 else []
        nxt = list(shards[l + 1]) if dist and l + 1 < depth else []
        o_a, lse, gathered = attn_fwd(q_t, kr, vte0, vte1, gather=tuple(late + nxt))
        if late:
            w.update(_with_own_part(_BIG[1:], gathered[:len(late)], late, place[0], d))
        if nxt:
            big[l + 1] = _with_own_part(_BIG, gathered[len(late):], nxt, place[0], d)
        mem_n, kv = memkv_fwd(mem, mg, w["wkv"])
        next_g = row(small["norm_g"][l + 1]) if l + 1 < depth else row(small["final_g"])
        x_next, y, up, merged, h_next = branch_fwd(x, proj, o_a, kv, ws, bsb, lng, lnb, w["wbr"], w["wout"], next_g)
        saved.append(dict(x=x, ng=ng, qg=qg, kg=kg, ws=ws, ws_t=ws_t, bsb=bsb, lng=lng, lnb=lnb, mg=mg, h=h, proj=proj,
                          q_t=q_t, kr=kr, kr_t=kr_t, vb=vb, o_a=o_a, lse=lse, mem_n=mem_n, kv=kv, y=y, up=up, merged=merged))
        x = x_next

    sq, dx, g_final = final_loss(x, row(small["final_g"]), tgt)
    grads = {n: [None] * depth for n in ("norm_g", "q_norm_g", "k_norm_g", "sg_ln_g", "sg_ln_b", "w_s", "b_s", "mem_norm_g")}
    parts = lambda g: g.reshape(N_CHIPS, -1, g.shape[-1])
    reduced = [[None] * len(_BIG) for _ in range(depth)]

    def reduce_all(items, t_sib, t_rem):
        if items:
            for (ll, a, _, _), f in zip(items, reduce_rows(place, [i[2] for i in items], t_sib, t_rem)):
                reduced[ll][a] = f

    as_scatter = lambda items: tuple(i[2] for i in items) + tuple(i[3] for i in items)
    pending = []
    for l in reversed(range(depth)):
        sv, w = saved[l], big[l]
        dy, dlg, g_wout, g_wbr, g_wout16, g_wbr16 = merge_bwd(dx, sv["proj"], sv["y"], sv["up"], sv["merged"], w["wbr"], w["wout"])
        dmid, do_t, delta, g_ws, g_bs, g_lng, g_lnb, dkv = branch_bwd(
            dy, sv["proj"], sv["o_a"], sv["kv"], sv["ws"], sv["ws_t"], sv["bsb"], sv["lng"], sv["lnb"], k["head_sel"])
        g_wkv, g_wkv16, g_mg = memkv_bwd(mem, sv["mg"], sv["mem_n"], w["wkv"], dkv)
        if dist:
            pending += [(l, 1, parts(g_wkv), parts(g_wkv16)), (l, 2, parts(g_wbr), parts(g_wbr16)), (l, 3, parts(g_wout), parts(g_wout16))]
        dq_t, dkr, dvb, t_sib, t_rem = attn_bwd(sv["q_t"], do_t, sv["kr"], sv["kr_t"], sv["vb"], sv["lse"], delta,
                                                scatter=as_scatter(pending))
        reduce_all(pending, t_sib, t_rem)
        dqkv, g_qg, g_kg = qk_prep_bwd(sv["proj"], dq_t, dkr, dvb, k["tabs"], sv["qg"], sv["kg"], k["gq"], k["gk"],
                                       k["fold_q"], k["fold_k"])
        g_win, g_win16 = win_grad(dqkv, dmid, dlg, sv["h"])
        pending = [(l, 0, parts(g_win), parts(g_win16))] if dist else []
        last = as_scatter(pending) if l == 0 else ()
        dx, g_ng, t_sib, t_rem = h_bwd(dqkv, dmid, dlg, w["win_t"], sv["x"], dx, sv["ng"], scatter=last)
        if last:
            reduce_all(pending, t_sib, t_rem)
        grads["norm_g"][l] = g_ng[0]
        grads["q_norm_g"][l] = g_qg[0, :A_HEAD_DIM]
        grads["k_norm_g"][l] = g_kg[0, :A_HEAD_DIM]
        grads["sg_ln_g"][l] = g_lng[0]
        grads["sg_ln_b"][l] = g_lnb[0]
        grads["w_s"][l] = g_ws
        grads["b_s"][l] = g_bs[:, :, 0]
        grads["mem_norm_g"][l] = g_mg[0]
        if not dist:
            reduced[l] = dict(zip(_BIG, (parts(g_win), parts(g_wkv), parts(g_wbr), parts(g_wout))))
    grads = {n: jnp.stack(v) for n, v in grads.items()}
    grads["final_g"] = g_final[0]
    return sq[0, 0], dx, grads, reduced


def _row_block(rows, width, cap_bytes=2 * 2**20):
    best = None
    for br in range(8, rows + 1, 8):
        if rows % br == 0 and br * width * 4 <= cap_bytes:
            best = br
    return best if best is not None else rows


def adamw(w, gs, m, v):
    r, c = w.shape
    n = len(gs)
    rs = r // n
    br = _row_block(rs, c)
    nb = rs // br

    def body(w_ref, *refs):
        g_refs, (m_ref, v_ref, og_ref, d_ref, nm_ref, nv_ref) = refs[:n], refs[n:]

        def update(gg):
            mm = ADAM_B1 * m_ref[...] + (1.0 - ADAM_B1) * gg
            vv = ADAM_B2 * v_ref[...] + (1.0 - ADAM_B2) * (gg * gg)
            m_hat = mm / (1.0 - ADAM_B1 ** ADAM_STEP)
            v_hat = vv / (1.0 - ADAM_B2 ** ADAM_STEP)
            og_ref[...] = gg
            d_ref[...] = -ADAM_LR * (m_hat / (jnp.sqrt(v_hat) + ADAM_EPS) + ADAM_WD * w_ref[...])
            nm_ref[...] = mm
            nv_ref[...] = vv

        for k in range(n):
            pl.when(pl.program_id(0) == k)(functools.partial(lambda k: update(g_refs[k][...]), k))

    blk = pl.BlockSpec((br, c), lambda l, i: (l * nb + i, 0))
    g_specs = [pl.BlockSpec((br, c), functools.partial(lambda l, i, k: (jnp.where(l == k, i, 0), 0), k=k)) for k in range(n)]
    return pl.pallas_call(
        body, out_shape=(_sds((r, c), F32),) * 4, grid=(n, nb), in_specs=[blk] + g_specs + [blk, blk], out_specs=(blk,) * 4,
        compiler_params=_cp("arbitrary", "arbitrary"), name="adamw")(w, *gs, m, v)


N_REMOTE = 2 * (N_CHIPS - 1)


def reduce_rows(place, gs, t_sibs, t_rems):
    n = len(gs)
    nt = 2

    def body(place_ref, *refs):
        for a in range(n):
            g_ref, s_ref, t_ref, f_ref = refs[a], refs[n + a], refs[2 * n + a], refs[3 * n + a]
            acc = g_ref[...] + s_ref[...]
            for j in range(N_REMOTE):
                acc = acc + t_ref[j].astype(F32)
            f_ref[...] = acc

    tiles = [(g.shape[1] // 2 // nt, g.shape[2]) for g in gs]
    return pl.pallas_call(
        body, out_shape=tuple(_sds(g.shape[1:], F32) for g in gs),
        grid_spec=pltpu.PrefetchScalarGridSpec(
            num_scalar_prefetch=1, grid=(nt,),
            in_specs=[pl.BlockSpec((None, tr, c), lambda i, p: (p[0], p[1] * nt + i, 0)) for tr, c in tiles]
            + [pl.BlockSpec((tr, c), lambda i, p: (i, 0)) for tr, c in tiles]
            + [pl.BlockSpec((N_REMOTE, tr, c), lambda i, p: (0, i, 0)) for tr, c in tiles],
            out_specs=tuple(pl.BlockSpec((tr, c), lambda i, p: (p[1] * nt + i, 0)) for tr, c in tiles)),
        compiler_params=_cp("parallel"), name="reduce_rows")(place, *gs, *t_sibs, *t_rems)


_ANY = pl.BlockSpec(memory_space=pl.ANY)


def _place():
    x, y, c = lax.axis_index("x"), lax.axis_index("y"), lax.axis_index("c")
    chips = [(1 - x, y), (x, 1 - y), (1 - x, 1 - y)]
    return x, y, c, chips


def gather_sems(n):
    return [pltpu.SemaphoreType.DMA((n, N_REMOTE)), pltpu.SemaphoreType.DMA((n, N_REMOTE))]


def gather_stages(shapes, ins, outs, send, recv):
    n = len(shapes)
    x, y, c, chips = _place()
    me = 2 * x + y
    sib = (x, y, 1 - c)

    def rows(a, hl):
        r2 = shapes[a][0] // 2
        return pl.ds(hl * r2, r2)

    def remote(a, k, src, dst, dev):
        return pltpu.make_async_remote_copy(src, dst, send.at[a, k], recv.at[a, k], device_id=dev, device_id_type=MESH)

    def sent(a, k):
        cx, cy = chips[k]
        return remote(a, k, ins[a].at[rows(a, c)], outs[a].at[me, rows(a, c)], (cx, cy, c))

    def passed(a, k, hl):
        cx, cy = chips[k]
        got = outs[a].at[2 * cx + cy, rows(a, hl)]
        return remote(a, k, got, got, (cx, cy, c)), remote(a, 3 + k, got, got, sib)

    def start():
        for a in range(n):
            for k in range(3):
                sent(a, k).start()

    def forward():
        for k in range(3):
            for a in range(n):
                arrived, on = passed(a, k, c)
                arrived.wait_recv()
                on.start()

    def finish():
        for k in range(3):
            for a in range(n):
                passed(a, k, 1 - c)[1].wait_recv()
        for k in range(3):
            for a in range(n):
                sent(a, k).wait_send()
                passed(a, k, c)[1].wait_send()

    return start, forward, finish


def allgather_layer(shards):
    n = len(shards)

    def body(*refs):
        for stage in gather_stages([a.shape for a in shards], refs[:n], refs[n:2 * n], *refs[2 * n:]):
            stage()

    return pl.pallas_call(
        body, out_shape=tuple(_sds((N_CHIPS,) + a.shape, a.dtype) for a in shards),
        in_specs=[_ANY] * n, out_specs=(_ANY,) * n, scratch_shapes=gather_sems(n), name="allgather_layer")(*shards)


def scatter_sems(n):
    return [pltpu.SemaphoreType.DMA((n, N_REMOTE + 1)), pltpu.SemaphoreType.DMA((n, N_REMOTE + 1))]


def scatter_out_shapes(gs):
    return (tuple(_sds((g.shape[1] // 2, g.shape[2]), F32) for g in gs)
            + tuple(_sds((N_REMOTE, g.shape[1] // 2, g.shape[2]), BF16) for g in gs))


def scatter_stages(shapes, gf, gb, t_sib, t_rem, send, recv):
    n = len(shapes)
    x, y, c, chips = _place()
    me = 2 * x + y

    def copies():
        out = []
        for a in range(n):
            r2 = shapes[a][0] // 2
            out.append(pltpu.make_async_remote_copy(gf[a].at[me, pl.ds((1 - c) * r2, r2)], t_sib[a], send.at[a, N_REMOTE],
                                                    recv.at[a, N_REMOTE], device_id=(x, y, 1 - c), device_id_type=MESH))
            for k, (cx, cy) in enumerate(chips):
                for o in range(2):
                    tc = c if o == 0 else 1 - c
                    out.append(pltpu.make_async_remote_copy(gb[a].at[2 * cx + cy, pl.ds(tc * r2, r2)], t_rem[a].at[2 * k + o],
                                                            send.at[a, 2 * k + o], recv.at[a, 2 * k + o],
                                                            device_id=(cx, cy, tc), device_id_type=MESH))
        return out

    def start():
        for cp in copies():
            cp.start()

    def finish():
        for cp in copies():
            cp.wait()

    return start, finish


def finish_exchange(v, fs):
    n = len(fs)
    r, w = v.shape
    ndev = 2 * N_CHIPS

    def body(v_ref, *refs):
        out, sum_ref = refs[n:2 * n], refs[2 * n]
        all_ref, send, recv, loc, fsend, frecv = refs[2 * n + 1:]
        x, y, c, chips = _place()
        me, sib = (x, y, c), (x, y, 1 - c)
        swaps = []
        for a in range(n):
            r2 = fs[a].shape[0] // 2
            mine = out[a].at[pl.ds(c * r2, r2)]
            cp = pltpu.make_async_remote_copy(mine, mine, fsend.at[a], frecv.at[a], device_id=sib, device_id_type=MESH)
            cp.start()
            swaps.append(cp)

        def slab(px, py, pc):
            return all_ref.at[4 * px + 2 * py + pc]

        def copy(k, block, to, src=None):
            return pltpu.make_async_remote_copy(slab(*block) if src is None else src, slab(*block), send.at[k], recv.at[k],
                                                device_id=to, device_id_type=MESH)

        mine = pltpu.make_async_copy(v_ref, slab(*me), loc)
        mine.start()
        first = [copy(0, me, sib, src=v_ref)] + [copy(1 + j, me, (*chip, c), src=v_ref) for j, chip in enumerate(chips)]
        for cp in first:
            cp.start()
        passed = [copy(4 + j, (*chip, c), sib) for j, chip in enumerate(chips)]
        for j, chip in enumerate(chips):
            copy(1 + j, (*chip, c), me).wait_recv()
            passed[j].start()
        copy(0, sib, me).wait_recv()
        for j, chip in enumerate(chips):
            copy(4 + j, (*chip, 1 - c), me).wait_recv()
        for cp in first + passed:
            cp.wait_send()
        mine.wait()
        acc = all_ref[0]
        for i in range(1, ndev):
            acc = acc + all_ref[i]
        sum_ref[...] = acc
        for a, cp in enumerate(swaps):
            r2 = fs[a].shape[0] // 2
            theirs = out[a].at[pl.ds((1 - c) * r2, r2)]
            cp.wait_send()
            pltpu.make_async_remote_copy(theirs, theirs, fsend.at[a], frecv.at[a], device_id=sib, device_id_type=MESH).wait_recv()

    vm = pl.BlockSpec(memory_space=pltpu.VMEM)
    res = pl.pallas_call(
        body, out_shape=tuple(_sds(f.shape, F32) for f in fs) + (_sds((r, w), F32),),
        in_specs=[vm] + [_ANY] * n, out_specs=(_ANY,) * n + (vm,), input_output_aliases={a + 1: a for a in range(n)},
        scratch_shapes=[pltpu.VMEM((ndev, r, w), F32), pltpu.SemaphoreType.DMA((7,)), pltpu.SemaphoreType.DMA((7,)),
                        pltpu.SemaphoreType.DMA, pltpu.SemaphoreType.DMA((n,)), pltpu.SemaphoreType.DMA((n,))],
        compiler_params=pltpu.CompilerParams(vmem_limit_bytes=VMEM_LIMIT), name="finish_exchange")(v, *fs)
    return res[n], list(res[:n])


_SMALL = ("norm_g", "q_norm_g", "k_norm_g", "sg_ln_g", "sg_ln_b", "w_s", "b_s", "mem_norm_g", "final_g")
_WEIGHTS = ("norm_g", "w_in", "q_norm_g", "k_norm_g", "sg_ln_g", "sg_ln_b", "w_s", "b_s", "mem_norm_g", "w_mem_kv", "w_br",
            "w_out", "final_g")


def _pack(d):
    flat = jnp.concatenate([d[n].reshape(-1) for n in _SMALL])
    rows = -(-flat.shape[0] // (8 * LANES)) * 8
    return jnp.pad(flat, (0, rows * LANES - flat.shape[0])).reshape(rows, LANES)


def _unpack(p, like):
    flat, out, o = p.reshape(-1), {}, 0
    for n in _SMALL:
        out[n] = flat[o:o + like[n].size].reshape(like[n].shape)
        o += like[n].size
    return out


def kernel(x, mem, norm_g, w_in, q_norm_g, k_norm_g, sg_ln_g, sg_ln_b, w_s, b_s, mem_norm_g, w_mem_kv, w_br, w_out, final_g, loss_target, m_norm_g, m_w_in, m_q_norm_g, m_k_norm_g, m_sg_ln_g, m_sg_ln_b, m_w_s, m_b_s, m_mem_norm_g, m_w_mem_kv, m_w_br, m_w_out, m_final_g, v_norm_g, v_w_in, v_q_norm_g, v_k_norm_g, v_sg_ln_g, v_sg_ln_b, v_w_s, v_b_s, v_mem_norm_g, v_w_mem_kv, v_w_br, v_w_out, v_final_g):
    w = dict(norm_g=norm_g, w_in=w_in, q_norm_g=q_norm_g, k_norm_g=k_norm_g, sg_ln_g=sg_ln_g, sg_ln_b=sg_ln_b, w_s=w_s, b_s=b_s,
             mem_norm_g=mem_norm_g, w_mem_kv=w_mem_kv, w_br=w_br, w_out=w_out, final_g=final_g)
    m = dict(norm_g=m_norm_g, w_in=m_w_in, q_norm_g=m_q_norm_g, k_norm_g=m_k_norm_g, sg_ln_g=m_sg_ln_g, sg_ln_b=m_sg_ln_b,
             w_s=m_w_s, b_s=m_b_s, mem_norm_g=m_mem_norm_g, w_mem_kv=m_w_mem_kv, w_br=m_w_br, w_out=m_w_out, final_g=m_final_g)
    v = dict(norm_g=v_norm_g, w_in=v_w_in, q_norm_g=v_q_norm_g, k_norm_g=v_k_norm_g, sg_ln_g=v_sg_ln_g, sg_ln_b=v_sg_ln_b,
             w_s=v_w_s, b_s=v_b_s, mem_norm_g=v_mem_norm_g, w_mem_kv=v_w_mem_kv, w_br=v_w_br, w_out=v_w_out, final_g=v_final_g)
    depth, d = norm_g.shape
    nsh = N_CHIPS
    br_rows = N_BRANCH * A_WIDTH
    br_cols = d // nsh

    shards = [[jnp.swapaxes(w_in[l], 0, 1).astype(BF16), w_mem_kv[l].astype(BF16), w_br[l].astype(BF16).reshape(br_rows, br_cols),
               w_out[l].astype(BF16)] for l in range(depth)]
    place = jnp.stack([2 * lax.axis_index("x") + lax.axis_index("y"), lax.axis_index("c")]).astype(jnp.int32)
    small = {n: w[n] for n in _SMALL}

    sq, dx, grads, reduced = local_fwd_bwd(x[0], mem[0], loss_target[0], small, shards=shards, place=place)
    loss = (0.5 / d) * lax.psum(sq, ("x", "y", "c"))

    small_sum, finals = finish_exchange(_pack(grads), [g for layer in reduced for g in layer])
    big_grads = dict(zip(("w_in", "w_mem_kv", "w_br", "w_out"), [finals[a::len(_BIG)] for a in range(len(_BIG))]))
    small_grads = _unpack(small_sum, small)

    out_g, out_d, out_m, out_v = {}, {}, {}, {}
    _, sd, sm, sv = adamw(_pack(small), [small_sum], _pack({n: m[n] for n in _SMALL}), _pack({n: v[n] for n in _SMALL}))
    sd, sm, sv = _unpack(sd, small), _unpack(sm, small), _unpack(sv, small)
    for n in _SMALL:
        out_g[n], out_d[n], out_m[n], out_v[n] = small_grads[n], sd[n], sm[n], sv[n]
    for n, gs in big_grads.items():
        into = (lambda a: jnp.swapaxes(a, 1, 2)) if n == "w_in" else (lambda a: a)
        two_d = lambda a: a.reshape(-1, gs[0].shape[-1])
        res = adamw(two_d(into(w[n])), gs, two_d(into(m[n])), two_d(into(v[n])))
        out_g[n], out_d[n], out_m[n], out_v[n] = [into(t.reshape(into(w[n]).shape)) for t in res]
    return (loss, dx[None], *[out_g[n] for n in _WEIGHTS], *[out_d[n] for n in _WEIGHTS], *[out_m[n] for n in _WEIGHTS],
            *[out_v[n] for n in _WEIGHTS])
```

```python
import functools

import jax
import jax.numpy as jnp
from jax import lax
from jax.experimental import pallas as pl
from jax.experimental.pallas import tpu as pltpu

F32 = jnp.float32
BF16 = jnp.bfloat16

GRID_W = 64
CHUNK = 128
ROPE_THETA = 10000.0
EPS = 1e-6
A_HEADS, A_KV_HEADS, A_HEAD_DIM = 8, 2, 64
A_WIDTH, A_KV_WIDTH = 512, 128
B_GROUPS, B_GROUP_DIM, B_WIDTH = 4, 128, 512
M_HEADS, M_HEAD_DIM, M_WIDTH = 4, 128, 512
N_BRANCH = 3
IN_WIDTH = 6912
O_QA, O_KA, O_VA, O_ZA, O_UB, O_VB, O_ZB, O_QM, O_ZM, O_LG = 0, 512, 640, 768, 1280, 1792, 2304, 2816, 3328, 3840
PBLK = 768
N_PBLK = IN_WIDTH // PBLK
MID_W = 3072
LG_W = 3072

LN2 = 0.6931471805599453
Q_SCALE = A_HEAD_DIM ** -0.5 / LN2
VTE_ROWS = A_HEAD_DIM + 16

ADAM_LR, ADAM_B1, ADAM_B2, ADAM_EPS, ADAM_WD, ADAM_STEP = 0.001, 0.9, 0.999, 1e-08, 0.01, 10

V7X_VMEM_BYTES = 64 * 2**20
VMEM_LIMIT = V7X_VMEM_BYTES - 4 * 2**20
LANES = 128
MESH = pl.DeviceIdType.MESH
N_CHIPS = 4


def _cp(*sem):
    return pltpu.CompilerParams(dimension_semantics=sem if sem else None, vmem_limit_bytes=VMEM_LIMIT)


def _dot(a, b):
    return jnp.dot(a, b, preferred_element_type=F32)


def _dot_nt(a, b):
    return lax.dot_general(a, b, (((1,), (1,)), ((), ())), preferred_element_type=F32)


def _dot_tn(a, b):
    return lax.dot_general(a, b, (((0,), (0,)), ((), ())), preferred_element_type=F32)


def _dot_hi(a, b):
    return jnp.dot(a, b, preferred_element_type=F32, precision=lax.Precision.HIGHEST)


def _group_sum(a, ones):
    hi = a.astype(BF16)
    lo = (a - hi.astype(F32)).astype(BF16)
    return _dot(hi, ones) + _dot(lo, ones)


def _dot_nt_hi(a, b):
    return lax.dot_general(a, b, (((1,), (1,)), ((), ())), preferred_element_type=F32, precision=lax.Precision.HIGHEST)


def _sig(z):
    return 1.0 / (1.0 + jnp.exp(-z))


def _full(shape, once=False):
    nd = len(shape)
    return pl.BlockSpec(shape, lambda *_: (0,) * nd, pipeline_mode=pl.Buffered(1) if once else None)


def _rows(tm, width):
    return pl.BlockSpec((tm, width), lambda i: (i, 0))


def _sds(shape, dtype):
    return jax.ShapeDtypeStruct(shape, dtype)


def rms_fwd(x, g, gather=()):
    s, d = x.shape
    tm = min(s, 512)
    nt = s // tm
    ng = len(gather)

    def body(x_ref, g_ref, *rest):
        g_in, h_ref, g_out = rest[:ng], rest[ng], rest[ng + 1:2 * ng + 1]
        if ng:
            start, forward, finish = gather_stages([a.shape for a in gather], g_in, g_out, *rest[2 * ng + 1:])
            pl.when(pl.program_id(0) == 0)(start)
        xf = x_ref[...]
        r = lax.rsqrt(jnp.mean(xf * xf, axis=-1, keepdims=True) + EPS)
        h_ref[...] = ((xf * r) * g_ref[...]).astype(BF16)
        if ng:
            @pl.when(pl.program_id(0) == nt - 1)
            def _():
                forward()
                finish()

    out = pl.pallas_call(
        body, out_shape=(_sds((s, d), BF16),) + tuple(_sds((N_CHIPS,) + a.shape, a.dtype) for a in gather), grid=(nt,),
        in_specs=[_rows(tm, d), _full((1, d))] + [_ANY] * ng, out_specs=(_rows(tm, d),) + (_ANY,) * ng,
        scratch_shapes=gather_sems(ng) if ng else [],
        compiler_params=_cp("arbitrary"), name="rms_fwd_gather" if ng else "rms_fwd")(x, g, *gather)
    return out[0], list(out[1:])


def proj_fwd(h, w_t):
    s, d = h.shape
    n = w_t.shape[0]
    tm = min(s, 1024)
    tn = 2304

    def body(h_ref, w_ref, o_ref):
        o_ref[...] = _dot_nt(h_ref[...], w_ref[...]).astype(BF16)

    return pl.pallas_call(
        body, out_shape=_sds((s, n), BF16), grid=(n // tn, s // tm),
        in_specs=[pl.BlockSpec((tm, d), lambda j, i: (i, 0)), pl.BlockSpec((tn, d), lambda j, i: (j, 0))],
        out_specs=pl.BlockSpec((tm, tn), lambda j, i: (i, j)),
        compiler_params=_cp("parallel", "parallel"), name="proj_fwd")(h, w_t)


def rope_tables(seq):
    n_freq = A_HEAD_DIM // 4
    d = jnp.arange(LANES) % A_HEAD_DIM
    seg, half, freq = d // (2 * n_freq), (d % (2 * n_freq)) // n_freq, d % n_freq
    inv = ROPE_THETA ** (-freq.astype(F32) / n_freq)
    t = jnp.arange(seq)
    pos = jnp.where(seg[None, :] == 0, (t // GRID_W)[:, None], (t % GRID_W)[:, None]).astype(F32)
    ang = pos * inv[None, :]
    cos, sin = jnp.cos(ang), jnp.sin(ang)
    return cos, jnp.where(half[None, :] == 1, sin, 0.0), jnp.where(half[None, :] == 0, -sin, 0.0)


def _group_ones(width, group):
    i = jnp.arange(width)
    return (i[:, None] // group == i[None, :] // group).astype(F32)


def _rope(xn, c, sa, sb):
    w = xn.shape[1]
    return xn * c + pltpu.roll(xn, 16, 1) * sa + pltpu.roll(xn, w - 16, 1) * sb


def _rope_t(dy, c, sa, sb):
    w = dy.shape[1]
    return dy * c + pltpu.roll(dy * sa, w - 16, 1) + pltpu.roll(dy * sb, 16, 1)


def _tile4(t):
    return jnp.concatenate([t, t, t, t], axis=1)


def qk_prep(proj, tabs, qg, kg, gq, gk):
    s = proj.shape[0]
    tm = min(s, 1024)
    c, sa, sb = tabs

    def body(p_ref, c_ref, sa_ref, sb_ref, qg_ref, kg_ref, gq_ref, gk_ref, qt_ref, kr_ref, vb_ref, kt_ref, v0_ref, v1_ref):
        xq = p_ref[:, O_QA:O_QA + A_WIDTH].astype(F32)
        xk = p_ref[:, O_KA:O_KA + A_KV_WIDTH].astype(F32)
        xv = p_ref[:, O_VA:O_VA + A_KV_WIDTH].astype(F32)
        cc, ssa, ssb = c_ref[...], sa_ref[...], sb_ref[...]
        msq = _group_sum(xq * xq, gq_ref[...]) * (1.0 / A_HEAD_DIM)
        qn = (xq * lax.rsqrt(msq + EPS)) * qg_ref[...]
        qr = _rope(qn, _tile4(cc), _tile4(ssa), _tile4(ssb)) * Q_SCALE
        qt_ref[...] = qr.T.astype(BF16)
        msk = _group_sum(xk * xk, gk_ref[...]) * (1.0 / A_HEAD_DIM)
        kn = (xk * lax.rsqrt(msk + EPS)) * kg_ref[...]
        kr = _rope(kn, cc, ssa, ssb)
        kr_ref[...] = kr.astype(BF16)
        vb_ref[...] = xv.astype(BF16)
        kt_ref[...] = kr.T.astype(BF16)
        vt = xv.T.astype(BF16)
        one = jnp.ones((VTE_ROWS - A_HEAD_DIM, tm), BF16)
        v0_ref[...] = jnp.concatenate([vt[:A_HEAD_DIM], one], axis=0)
        v1_ref[...] = jnp.concatenate([vt[A_HEAD_DIM:], one], axis=0)

    tab = _rows(tm, LANES)
    colb = lambda w: pl.BlockSpec((w, tm), lambda i: (0, i))
    return pl.pallas_call(
        body,
        out_shape=(_sds((A_WIDTH, s), BF16), _sds((s, A_KV_WIDTH), BF16), _sds((s, A_KV_WIDTH), BF16),
                   _sds((A_KV_WIDTH, s), BF16), _sds((VTE_ROWS, s), BF16), _sds((VTE_ROWS, s), BF16)),
        grid=(s // tm,),
        in_specs=[_rows(tm, PBLK), tab, tab, tab, _full((1, A_WIDTH)), _full((1, A_KV_WIDTH)),
                  _full((A_WIDTH, A_WIDTH)), _full((A_KV_WIDTH, A_KV_WIDTH))],
        out_specs=(colb(A_WIDTH), _rows(tm, A_KV_WIDTH), _rows(tm, A_KV_WIDTH), colb(A_KV_WIDTH), colb(VTE_ROWS), colb(VTE_ROWS)),
        compiler_params=_cp("parallel"), name="qk_prep")(proj, c, sa, sb, qg, kg, gq, gk)


def _pad_head(q_h, kv):
    z = jnp.zeros_like(q_h)
    return jnp.concatenate([q_h, z], axis=0) if kv == 0 else jnp.concatenate([z, q_h], axis=0)


def attn_fwd(q_t, kr, vte0, vte1, gather=()):
    s = kr.shape[0]
    tq = min(s, 512)
    kc = min(s, 256)
    nkc = s // kc
    nq = s // tq
    grp = A_HEADS // A_KV_HEADS
    ng = len(gather)

    def body(qt_ref, kr_ref, v0_ref, v1_ref, *rest):
        g_in, (o_ref, lse_ref), g_out = rest[:ng], rest[ng:ng + 2], rest[ng + 2:2 * ng + 2]
        qp_ref, m_ref, acc_ref = rest[2 * ng + 2:2 * ng + 5]
        if ng:
            start, forward, finish = gather_stages([g.shape for g in gather], g_in, g_out, *rest[2 * ng + 5:])
            pl.when(pl.program_id(0) == 0)(start)
            pl.when(pl.program_id(0) == (3 * nq) // 4)(forward)

        for h in range(A_HEADS):
            qp_ref[h] = _pad_head(qt_ref[A_HEAD_DIM * h:A_HEAD_DIM * (h + 1), :], h // grp)
        m_ref[...] = jnp.full(m_ref.shape, -1e30, F32)
        acc_ref[...] = jnp.zeros_like(acc_ref)

        def step(ci, carry):
            ks = pl.ds(pl.multiple_of(ci * kc, kc), kc)
            kblk = kr_ref[ks, :]
            vts = (v0_ref[:, ks], v1_ref[:, ks])
            scs = [_dot(kblk, qp_ref[h]) for h in range(A_HEADS)]
            for h in range(A_HEADS):
                sc = scs[h]
                m_prev = m_ref[h:h + 1, :]
                m_new = jnp.maximum(m_prev, jnp.max(sc, axis=0, keepdims=True))
                p = jnp.exp2(sc - m_new)
                acc_ref[h] = acc_ref[h] * jnp.exp2(m_prev - m_new) + _dot(vts[h // grp], p.astype(BF16))
                m_ref[h:h + 1, :] = m_new
            return carry

        lax.fori_loop(0, nkc, step, 0)
        outs, lses = [], []
        for h in range(A_HEADS):
            acc = acc_ref[h]
            l = acc[A_HEAD_DIM:A_HEAD_DIM + 1, :]
            outs.append(acc[:A_HEAD_DIM, :] / l)
            lses.append(m_ref[h:h + 1, :] + jnp.log2(l))
        o_ref[...] = jnp.concatenate(outs, axis=0).T
        lse_ref[...] = jnp.concatenate(lses, axis=0)
        if ng:
            pl.when(pl.program_id(0) == nq - 1)(finish)

    out = pl.pallas_call(
        body,
        out_shape=(_sds((s, A_WIDTH), F32), _sds((A_HEADS, s), F32)) + tuple(_sds((N_CHIPS,) + g.shape, g.dtype) for g in gather),
        grid=(nq,),
        in_specs=[pl.BlockSpec((A_WIDTH, tq), lambda i: (0, i)), _full((s, A_KV_WIDTH)), _full((VTE_ROWS, s)),
                  _full((VTE_ROWS, s))] + [_ANY] * ng,
        out_specs=(_rows(tq, A_WIDTH), pl.BlockSpec((A_HEADS, tq), lambda i: (0, i))) + (_ANY,) * ng,
        scratch_shapes=[pltpu.VMEM((A_HEADS, A_KV_WIDTH, tq), BF16), pltpu.VMEM((A_HEADS, tq), F32),
                        pltpu.VMEM((A_HEADS, VTE_ROWS, tq), F32)] + (gather_sems(ng) if ng else []),
        compiler_params=_cp("arbitrary"), name="attn_fwd_gather" if ng else "attn_fwd")(q_t, kr, vte0, vte1, *gather)
    return out[0], out[1], list(out[2:])


def memkv_fwd(mem, g, w_kv):
    m, d = mem.shape

    def body(mem_ref, g_ref, w_ref, mn_ref, kv_ref):
        mf = mem_ref[...]
        r = lax.rsqrt(jnp.mean(mf * mf, axis=-1, keepdims=True) + EPS)
        mn = ((mf * r) * g_ref[...]).astype(BF16)
        mn_ref[...] = mn
        kv_ref[...] = _dot(mn, w_ref[...]).astype(BF16)

    return pl.pallas_call(
        body, out_shape=(_sds((m, d), BF16), _sds((m, 2 * M_WIDTH), BF16)),
        compiler_params=_cp(), name="memkv_fwd")(mem, g, w_kv)


def _layer_norm_stats(v):
    mu = jnp.mean(v, axis=-1, keepdims=True)
    xc = v - mu
    rstd = lax.rsqrt(jnp.mean(xc * xc, axis=-1, keepdims=True) + EPS)
    return xc * rstd, rstd


def _spatial_mix(vlb, ws_ref, bsb_ref, tm):
    rows = []
    for ci in range(tm // CHUNK):
        cols = []
        for g in range(B_GROUPS):
            blk = vlb[ci * CHUNK:(ci + 1) * CHUNK, g * B_GROUP_DIM:(g + 1) * B_GROUP_DIM]
            cols.append(_dot(ws_ref[g], blk) + bsb_ref[g])
        rows.append(jnp.concatenate(cols, axis=1))
    return jnp.concatenate(rows, axis=0)


def _mem_attn(qm, kv_ref):
    out = []
    for h in range(M_HEADS):
        qh = qm[:, h * M_HEAD_DIM:(h + 1) * M_HEAD_DIM].astype(BF16)
        kh = kv_ref[:, h * M_HEAD_DIM:(h + 1) * M_HEAD_DIM]
        vh = kv_ref[:, M_WIDTH + h * M_HEAD_DIM:M_WIDTH + (h + 1) * M_HEAD_DIM]
        sc = _dot_nt(qh, kh) * (M_HEAD_DIM ** -0.5)
        e = jnp.exp(sc - jnp.max(sc, axis=-1, keepdims=True))
        p = e / jnp.sum(e, axis=-1, keepdims=True)
        out.append((p, _dot(p.astype(BF16), vh)))
    return out


def branch_fwd(x, proj, o_a, kv, ws, bsb, ln_g, ln_b, w_br, w_out, next_g):
    s, d = x.shape
    tm = min(s, 512)

    def body(x_ref, p_ref, oa_ref, kv_ref, ws_ref, bsb_ref, lg_ref, lb_ref, wbr_ref, wo_ref, ng_ref,
             xn_ref, y_ref, up_ref, mg_ref, hn_ref):
        seg = lambda o, w: p_ref[:, o:o + w].astype(F32)
        z_a, u_b, v_b, z_b = seg(O_ZA, A_WIDTH), seg(O_UB, B_WIDTH), seg(O_VB, B_WIDTH), seg(O_ZB, B_WIDTH)
        q_m, z_m = seg(O_QM, M_WIDTH), seg(O_ZM, M_WIDTH)
        xhat, _ = _layer_norm_stats(v_b)
        vln = xhat * lg_ref[...] + lb_ref[...]
        mixed = _spatial_mix(vln.astype(BF16), ws_ref, bsb_ref, tm)
        y_b = (u_b * mixed) * (z_b * _sig(z_b))
        o_m = jnp.concatenate([o for _, o in _mem_attn(q_m, kv_ref)], axis=1)
        y_a = oa_ref[...] * (z_a * _sig(z_a))
        y_m = o_m * (z_m * _sig(z_m))
        merged = None
        for n, yy in enumerate((y_a, y_b, y_m)):
            yb = yy.astype(BF16)
            y_ref[n] = yb
            up = jnp.concatenate([_dot(yb, wbr_ref[c, n]) for c in range(N_CHIPS)], axis=1)
            up_ref[n] = up.astype(BF16)
            t = _sig(seg(O_LG + n * d, d)) * up
            merged = t if merged is None else merged + t
        mb = merged.astype(BF16)
        mg_ref[...] = mb
        xn = x_ref[...] + _dot(mb, wo_ref[...])
        xn_ref[...] = xn
        r = lax.rsqrt(jnp.mean(xn * xn, axis=-1, keepdims=True) + EPS)
        hn_ref[...] = ((xn * r) * ng_ref[...]).astype(BF16)

    return pl.pallas_call(
        body,
        out_shape=(_sds((s, d), F32), _sds((N_BRANCH, s, A_WIDTH), BF16), _sds((N_BRANCH, s, d), BF16), _sds((s, d), BF16),
                   _sds((s, d), BF16)),
        grid=(s // tm,),
        in_specs=[_rows(tm, d), _rows(tm, IN_WIDTH), _rows(tm, A_WIDTH), _full(kv.shape), _full(ws.shape), _full(bsb.shape),
                  _full((1, B_WIDTH)), _full((1, B_WIDTH)), _full(w_br.shape), _full(w_out.shape), _full((1, d))],
        out_specs=(_rows(tm, d), pl.BlockSpec((N_BRANCH, tm, A_WIDTH), lambda i: (0, i, 0)),
                   pl.BlockSpec((N_BRANCH, tm, d), lambda i: (0, i, 0)), _rows(tm, d), _rows(tm, d)),
        compiler_params=_cp("parallel"), name="branch_fwd")(x, proj, o_a, kv, ws, bsb, ln_g, ln_b, w_br, w_out, next_g)


def final_loss(x, fg, tgt):
    s, d = x.shape
    tm = min(s, 512)

    def body(x_ref, g_ref, t_ref, ls_ref, dx_ref, gg_ref):
        @pl.when(pl.program_id(0) == 0)
        def _():
            ls_ref[...] = jnp.zeros_like(ls_ref)
            gg_ref[...] = jnp.zeros_like(gg_ref)

        xf = x_ref[...]
        g = g_ref[...]
        r = lax.rsqrt(jnp.mean(xf * xf, axis=-1, keepdims=True) + EPS)
        xh = xf * r
        e = xh * g - t_ref[...]
        sq = jnp.sum(jnp.sum(e * e, axis=0, keepdims=True), axis=1, keepdims=True)
        ls_ref[...] += jnp.broadcast_to(sq, ls_ref.shape)
        dy = e * (1.0 / d)
        gg_ref[...] += jnp.sum(dy * xh, axis=0, keepdims=True)
        gy = dy * g
        dx_ref[...] = r * (gy - xh * jnp.mean(gy * xh, axis=-1, keepdims=True))

    return pl.pallas_call(
        body, out_shape=(_sds((1, LANES), F32), _sds((s, d), F32), _sds((1, d), F32)), grid=(s // tm,),
        in_specs=[_rows(tm, d), _full((1, d)), _rows(tm, d)],
        out_specs=(_full((1, LANES)), _rows(tm, d), _full((1, d))),
        compiler_params=_cp("arbitrary"), name="final_loss")(x, fg, tgt)


def _pblocks(tm, first, count):
    return [pl.BlockSpec((tm, PBLK), functools.partial(lambda i, b: (i, b), b=first + k)) for k in range(count)]


def merge_bwd(dx, proj, y, up, merged, w_br, w_out):
    s, d = dx.shape
    tm = min(s, 512)
    nlg = LG_W // PBLK
    cw = d // N_CHIPS

    def body(dx_ref, l0, l1, l2, l3, y_ref, up_ref, mg_ref, wbr_ref, wo_ref, dy_ref, dlg_ref, gwo_ref, gwb_ref, gwo16_ref, gwb16_ref):
        @pl.when(pl.program_id(0) == 0)
        def _():
            gwo_ref[...] = jnp.zeros_like(gwo_ref)
            gwb_ref[...] = jnp.zeros_like(gwb_ref)

        dxb = dx_ref[...].astype(BF16)
        dmg = _dot_nt(dxb, wo_ref[...])
        gwo_ref[...] += _dot_tn(mg_ref[...], dxb)
        lg = jnp.concatenate([l0[...], l1[...], l2[...], l3[...]], axis=1).astype(F32)
        for n in range(N_BRANCH):
            g = _sig(lg[:, n * d:(n + 1) * d])
            dup = dmg * g
            dlg_ref[:, n * d:(n + 1) * d] = ((dup * up_ref[n].astype(F32)) * (1.0 - g)).astype(BF16)
            dupb = dup.astype(BF16)
            dyn = None
            for c in range(N_CHIPS):
                blk = dupb[:, c * cw:(c + 1) * cw]
                gwb_ref[c, n] += _dot_tn(y_ref[n], blk)
                t = _dot_nt(blk, wbr_ref[c, n])
                dyn = t if dyn is None else dyn + t
            dy_ref[n] = dyn.astype(BF16)

        @pl.when(pl.program_id(0) == pl.num_programs(0) - 1)
        def _():
            gwo16_ref[...] = gwo_ref[...].astype(BF16)
            gwb16_ref[...] = gwb_ref[...].astype(BF16)

    return pl.pallas_call(
        body,
        out_shape=(_sds((N_BRANCH, s, A_WIDTH), BF16), _sds((s, LG_W), BF16), _sds((d, d), F32), _sds(w_br.shape, F32),
                   _sds((d, d), BF16), _sds(w_br.shape, BF16)),
        grid=(s // tm,),
        in_specs=[_rows(tm, d)] + _pblocks(tm, O_LG // PBLK, nlg) + [
            pl.BlockSpec((N_BRANCH, tm, A_WIDTH), lambda i: (0, i, 0)), pl.BlockSpec((N_BRANCH, tm, d), lambda i: (0, i, 0)),
            _rows(tm, d), _full(w_br.shape, once=True), _full(w_out.shape, once=True)],
        out_specs=(pl.BlockSpec((N_BRANCH, tm, A_WIDTH), lambda i: (0, i, 0)), _rows(tm, LG_W), _full((d, d)), _full(w_br.shape),
                   _full((d, d)), _full(w_br.shape)),
        compiler_params=_cp("arbitrary"), name="merge_bwd")(dx, proj, proj, proj, proj, y, up, merged, w_br, w_out)


def _dsilu(z, sg):
    return sg * (1.0 + z * (1.0 - sg))


def branch_bwd(dy, proj, o_a, kv, ws, ws_t, bsb, ln_g, ln_b, head_sel):
    s = proj.shape[0]
    tm = min(s, 512)
    nmid = MID_W // PBLK

    def body(dy_ref, m0, m1, m2, m3, oa_ref, kv_ref, ws_ref, wst_ref, bsb_ref, lg_ref, lb_ref, sel_ref,
             dmid_ref, dot_ref, dl_ref, gws_ref, gbs_ref, glg_ref, glb_ref, dkv_ref):
        @pl.when(pl.program_id(0) == 0)
        def _():
            for r in (gws_ref, gbs_ref, glg_ref, glb_ref, dkv_ref):
                r[...] = jnp.zeros_like(r)

        mid = jnp.concatenate([m0[...], m1[...], m2[...], m3[...]], axis=1).astype(F32)
        seg = lambda o, w: mid[:, o - O_ZA:o - O_ZA + w]
        z_a, u_b, v_b, z_b = seg(O_ZA, A_WIDTH), seg(O_UB, B_WIDTH), seg(O_VB, B_WIDTH), seg(O_ZB, B_WIDTH)
        q_m, z_m = seg(O_QM, M_WIDTH), seg(O_ZM, M_WIDTH)

        def put(o, v):
            dmid_ref[:, o - O_ZA:o - O_ZA + v.shape[1]] = v.astype(BF16)

        dy_a, dy_b, dy_m = dy_ref[0].astype(F32), dy_ref[1].astype(F32), dy_ref[2].astype(F32)

        o_a_ = oa_ref[...]
        sg = _sig(z_a)
        do_a = dy_a * (z_a * sg)
        put(O_ZA, (dy_a * o_a_) * _dsilu(z_a, sg))
        do_l = do_a * LN2
        dot_ref[...] = do_l.T.astype(BF16)
        dl_ref[...] = _dot_nt_hi(sel_ref[...], do_l * o_a_)

        xhat, rstd = _layer_norm_stats(v_b)
        lng = lg_ref[...]
        vln = xhat * lng + lb_ref[...]
        vlb = vln.astype(BF16)
        mixed = _spatial_mix(vlb, ws_ref, bsb_ref, tm)
        sg = _sig(z_b)
        sl = z_b * sg
        put(O_UB, (dy_b * mixed) * sl)
        put(O_ZB, ((dy_b * u_b) * mixed) * _dsilu(z_b, sg))
        dmix = (dy_b * u_b) * sl
        dmb = dmix.astype(BF16)
        rows = []
        for ci in range(tm // CHUNK):
            cols = []
            for g in range(B_GROUPS):
                rs, cs = slice(ci * CHUNK, (ci + 1) * CHUNK), slice(g * B_GROUP_DIM, (g + 1) * B_GROUP_DIM)
                gws_ref[g] += _dot_nt(dmb[rs, cs], vlb[rs, cs])
                gbs_ref[g] += jnp.broadcast_to(jnp.sum(dmix[rs, cs], axis=1, keepdims=True), (CHUNK, B_GROUP_DIM))
                cols.append(_dot(wst_ref[g], dmb[rs, cs]))
            rows.append(jnp.concatenate(cols, axis=1))
        dvln = jnp.concatenate(rows, axis=0)
        glg_ref[...] += jnp.sum(dvln * xhat, axis=0, keepdims=True)
        glb_ref[...] += jnp.sum(dvln, axis=0, keepdims=True)
        gy = dvln * lng
        put(O_VB, rstd * ((gy - jnp.mean(gy, axis=-1, keepdims=True)) - xhat * jnp.mean(gy * xhat, axis=-1, keepdims=True)))

        sg = _sig(z_m)
        sl = z_m * sg
        heads = _mem_attn(q_m, kv_ref)
        o_m = jnp.concatenate([o for _, o in heads], axis=1)
        put(O_ZM, (dy_m * o_m) * _dsilu(z_m, sg))
        do_m = dy_m * sl
        dqs = []
        for h, (p, o_h) in enumerate(heads):
            hs = slice(h * M_HEAD_DIM, (h + 1) * M_HEAD_DIM)
            vs = slice(M_WIDTH + h * M_HEAD_DIM, M_WIDTH + (h + 1) * M_HEAD_DIM)
            do_h = do_m[:, hs]
            dob = do_h.astype(BF16)
            dp = _dot_nt(dob, kv_ref[:, vs])
            dsc = (p * (dp - jnp.sum(do_h * o_h, axis=-1, keepdims=True))) * (M_HEAD_DIM ** -0.5)
            dsb = dsc.astype(BF16)
            dqs.append(_dot(dsb, kv_ref[:, hs]))
            dkv_ref[:, hs] += _dot_tn(dsb, q_m[:, hs].astype(BF16))
            dkv_ref[:, vs] += _dot_tn(p.astype(BF16), dob)
        put(O_QM, jnp.concatenate(dqs, axis=1))

    return pl.pallas_call(
        body,
        out_shape=(_sds((s, MID_W), BF16), _sds((A_WIDTH, s), BF16), _sds((A_HEADS, s), F32), _sds(ws.shape, F32),
                   _sds(ws.shape, F32), _sds((1, B_WIDTH), F32), _sds((1, B_WIDTH), F32), _sds(kv.shape, F32)),
        grid=(s // tm,),
        in_specs=[pl.BlockSpec((N_BRANCH, tm, A_WIDTH), lambda i: (0, i, 0))] + _pblocks(tm, O_ZA // PBLK, nmid) + [
            _rows(tm, A_WIDTH), _full(kv.shape), _full(ws.shape), _full(ws.shape), _full(bsb.shape),
            _full((1, B_WIDTH)), _full((1, B_WIDTH)), _full(head_sel.shape)],
        out_specs=(_rows(tm, MID_W), pl.BlockSpec((A_WIDTH, tm), lambda i: (0, i)), pl.BlockSpec((A_HEADS, tm), lambda i: (0, i)),
                   _full(ws.shape), _full(ws.shape), _full((1, B_WIDTH)), _full((1, B_WIDTH)), _full(kv.shape)),
        compiler_params=_cp("arbitrary"), name="branch_bwd")(dy, proj, proj, proj, proj, o_a, kv, ws, ws_t, bsb, ln_g, ln_b, head_sel)


def attn_bwd(q_t, do_t, kr, kr_t, vb, lse, delta, scatter=()):
    s = kr.shape[0]
    tq = min(s, 256)
    kc = min(s, 512)
    nkc = s // kc
    nq = s // tq
    grp = A_HEADS // A_KV_HEADS
    ns = len(scatter)
    na = ns // 2

    def body(qt_ref, dot_ref, kr_ref, krt_ref, vb_ref, lse_ref, dl_ref, *rest):
        s_in, (dqt_ref, dk_ref, dv_ref), s_out = rest[:ns], rest[ns:ns + 3], rest[ns + 3:2 * ns + 3]
        qp_ref, dop_ref, dq_ref = rest[2 * ns + 3:2 * ns + 6]
        if ns:
            start, finish = scatter_stages([g.shape[1:] for g in scatter[:na]], s_in[:na], s_in[na:], s_out[:na], s_out[na:],
                                           *rest[2 * ns + 6:])
            pl.when(pl.program_id(0) == 0)(start)

        @pl.when(pl.program_id(0) == 0)
        def _():
            dk_ref[...] = jnp.zeros_like(dk_ref)
            dv_ref[...] = jnp.zeros_like(dv_ref)

        for h in range(A_HEADS):
            hs = slice(A_HEAD_DIM * h, A_HEAD_DIM * (h + 1))
            qp_ref[h] = _pad_head(qt_ref[hs, :], h // grp)
            dop_ref[h] = _pad_head(dot_ref[hs, :], h // grp)
        dq_ref[...] = jnp.zeros_like(dq_ref)

        def step(ci, carry):
            ks = pl.ds(pl.multiple_of(ci * kc, kc), kc)
            kblk, vblk, ktb = kr_ref[ks, :], vb_ref[ks, :], krt_ref[:, ks]
            dv_acc = jnp.zeros((kc, A_KV_WIDTH), F32)
            dk_acc = jnp.zeros((kc, A_KV_WIDTH), F32)
            scs = [_dot(kblk, qp_ref[h]) for h in range(A_HEADS)]
            dps = [_dot(vblk, dop_ref[h]) for h in range(A_HEADS)]
            for h in range(A_HEADS):
                qpad, dopad = qp_ref[h], dop_ref[h]
                p = jnp.exp2(scs[h] - lse_ref[h:h + 1, :])
                dsb = (p * (dps[h] - dl_ref[h:h + 1, :])).astype(BF16)
                dv_acc = dv_acc + _dot_nt(p.astype(BF16), dopad)
                dk_acc = dk_acc + _dot_nt(dsb, qpad)
                dq_ref[h] += _dot(ktb, dsb)
            dv_ref[ks, :] += dv_acc
            dk_ref[ks, :] += dk_acc
            return carry

        lax.fori_loop(0, nkc, step, 0)
        dqt_ref[...] = jnp.concatenate(
            [dq_ref[h][A_HEAD_DIM * (h // grp):A_HEAD_DIM * (h // grp + 1), :] for h in range(A_HEADS)], axis=0)
        if ns:
            pl.when(pl.program_id(0) == nq - 1)(finish)

    colq = pl.BlockSpec((A_WIDTH, tq), lambda i: (0, i))
    colh = pl.BlockSpec((A_HEADS, tq), lambda i: (0, i))
    out = pl.pallas_call(
        body,
        out_shape=(_sds((A_WIDTH, s), F32), _sds((s, A_KV_WIDTH), F32), _sds((s, A_KV_WIDTH), F32)) + scatter_out_shapes(scatter[:na]),
        grid=(nq,),
        in_specs=[colq, colq, _full((s, A_KV_WIDTH)), _full((A_KV_WIDTH, s)), _full((s, A_KV_WIDTH)), colh, colh] + [_ANY] * ns,
        out_specs=(colq, _full((s, A_KV_WIDTH)), _full((s, A_KV_WIDTH))) + (_ANY,) * ns,
        scratch_shapes=[pltpu.VMEM((A_HEADS, A_KV_WIDTH, tq), BF16), pltpu.VMEM((A_HEADS, A_KV_WIDTH, tq), BF16),
                        pltpu.VMEM((A_HEADS, A_KV_WIDTH, tq), F32)] + (scatter_sems(na) if ns else []),
        compiler_params=_cp("arbitrary"), name="attn_bwd_scatter" if ns else "attn_bwd")(
            q_t, do_t, kr, kr_t, vb, lse, delta, *scatter)
    return out[0], out[1], out[2], list(out[3:3 + na]), list(out[3 + na:])


def qk_prep_bwd(proj, dq_t, dkr, dvb, tabs, qg, kg, gq, gk, fold_q, fold_k):
    s = proj.shape[0]
    tm = min(s, 1024)
    c, sa, sb = tabs

    def head_norm_bwd(x, dn, gain, gones, fold):
        ms = _group_sum(x * x, gones) * (1.0 / A_HEAD_DIM)
        r = lax.rsqrt(ms + EPS)
        xh = x * r
        gg = _dot_hi(jnp.sum(dn * xh, axis=0, keepdims=True), fold)
        u = dn * gain
        mean_u = _group_sum(u * xh, gones) * (1.0 / A_HEAD_DIM)
        return r * (u - xh * mean_u), gg

    def body(p_ref, dqt_ref, dk_ref, dv_ref, c_ref, sa_ref, sb_ref, qg_ref, kg_ref, gq_ref, gk_ref, fq_ref, fk_ref,
             dqkv_ref, gqg_ref, gkg_ref):
        @pl.when(pl.program_id(0) == 0)
        def _():
            gqg_ref[...] = jnp.zeros_like(gqg_ref)
            gkg_ref[...] = jnp.zeros_like(gkg_ref)

        cc, ssa, ssb = c_ref[...], sa_ref[...], sb_ref[...]
        dqr = dqt_ref[...].T * Q_SCALE
        dqn = _rope_t(dqr, _tile4(cc), _tile4(ssa), _tile4(ssb))
        dxq, gq_ = head_norm_bwd(p_ref[:, O_QA:O_QA + A_WIDTH].astype(F32), dqn, qg_ref[...], gq_ref[...], fq_ref[...])
        dkn = _rope_t(dk_ref[...], cc, ssa, ssb)
        dxk, gk_ = head_norm_bwd(p_ref[:, O_KA:O_KA + A_KV_WIDTH].astype(F32), dkn, kg_ref[...], gk_ref[...], fk_ref[...])
        gqg_ref[...] += gq_
        gkg_ref[...] += gk_
        dqkv_ref[:, O_QA:O_QA + A_WIDTH] = dxq.astype(BF16)
        dqkv_ref[:, O_KA:O_KA + A_KV_WIDTH] = dxk.astype(BF16)
        dqkv_ref[:, O_VA:O_VA + A_KV_WIDTH] = (dv_ref[...] * (1.0 / LN2)).astype(BF16)

    tab = _rows(tm, LANES)
    return pl.pallas_call(
        body, out_shape=(_sds((s, PBLK), BF16), _sds((1, LANES), F32), _sds((1, LANES), F32)), grid=(s // tm,),
        in_specs=[_rows(tm, PBLK), pl.BlockSpec((A_WIDTH, tm), lambda i: (0, i)), _rows(tm, A_KV_WIDTH), _rows(tm, A_KV_WIDTH),
                  tab, tab, tab, _full((1, A_WIDTH)), _full((1, A_KV_WIDTH)), _full((A_WIDTH, A_WIDTH)),
                  _full((A_KV_WIDTH, A_KV_WIDTH)), _full((A_WIDTH, LANES)), _full((A_KV_WIDTH, LANES))],
        out_specs=(_rows(tm, PBLK), _full((1, LANES)), _full((1, LANES))),
        compiler_params=_cp("arbitrary"), name="qk_prep_bwd")(proj, dq_t, dkr, dvb, c, sa, sb, qg, kg, gq, gk, fold_q, fold_k)


def _pick_dproj(b, d0, d1, d2, use):
    first_lg = 1 + MID_W // PBLK

    @pl.when(b == 0)
    def _():
        use(d0[...])

    @pl.when(jnp.logical_and(b >= 1, b < first_lg))
    def _():
        use(d1[...])

    @pl.when(b >= first_lg)
    def _():
        use(d2[...])


def win_grad(d0, d1, d2, h):
    s, d = h.shape
    tk = min(s, 4096)
    nk = s // tk

    def body(d0_ref, d1_ref, d2_ref, h_ref, o_ref, o16_ref):
        @pl.when(pl.program_id(1) == 0)
        def _():
            o_ref[...] = jnp.zeros_like(o_ref)

        def use(blk):
            o_ref[...] += _dot_tn(blk, h_ref[...])

        _pick_dproj(pl.program_id(0), d0_ref, d1_ref, d2_ref, use)

        @pl.when(pl.program_id(1) == nk - 1)
        def _():
            o16_ref[...] = o_ref[...].astype(BF16)

    def spec(first, count):
        def imap(j, k):
            used = jnp.logical_and(j >= first, j < first + count)
            return (jnp.where(used, k, 0), jnp.clip(j - first, 0, count - 1))
        return pl.BlockSpec((tk, PBLK), imap)

    nm = MID_W // PBLK
    oblk = pl.BlockSpec((PBLK, d), lambda j, k: (j, 0))
    return pl.pallas_call(
        body, out_shape=(_sds((IN_WIDTH, d), F32), _sds((IN_WIDTH, d), BF16)), grid=(N_PBLK, nk),
        in_specs=[spec(0, 1), spec(1, nm), spec(1 + nm, LG_W // PBLK),
                  pl.BlockSpec((tk, d), lambda j, k: (k, 0), pipeline_mode=pl.Buffered(1) if nk == 1 else None)],
        out_specs=(oblk, oblk),
        compiler_params=_cp("parallel", "arbitrary"), name="win_grad")(d0, d1, d2, h)


def h_bwd(d0, d1, d2, w_t, x, dx_out, g, scatter=()):
    s, d = x.shape
    tm = min(s, 512)
    nt = s // tm
    ns = len(scatter)
    na = ns // 2

    def body(d0_ref, d1_ref, d2_ref, w_ref, x_ref, dxo_ref, g_ref, *rest):
        s_in, (dx_ref, gg_ref), s_out = rest[:ns], rest[ns:ns + 2], rest[ns + 2:2 * ns + 2]
        if ns:
            start, finish = scatter_stages([a.shape[1:] for a in scatter[:na]], s_in[:na], s_in[na:], s_out[:na], s_out[na:],
                                           *rest[2 * ns + 2:])
            pl.when(pl.program_id(0) == 0)(start)

        @pl.when(pl.program_id(0) == 0)
        def _():
            gg_ref[...] = jnp.zeros_like(gg_ref)

        dh = (_dot(d0_ref[...], w_ref[0:PBLK, :]) + _dot(d1_ref[...], w_ref[PBLK:PBLK + MID_W, :])
              + _dot(d2_ref[...], w_ref[PBLK + MID_W:, :]))
        xf = x_ref[...]
        r = lax.rsqrt(jnp.mean(xf * xf, axis=-1, keepdims=True) + EPS)
        xh = xf * r
        gg_ref[...] += jnp.sum(dh * xh, axis=0, keepdims=True)
        u = dh * g_ref[...]
        dx_ref[...] = dxo_ref[...] + r * (u - xh * jnp.mean(u * xh, axis=-1, keepdims=True))
        if ns:
            pl.when(pl.program_id(0) == nt - 1)(finish)

    rowb = _rows(tm, d)
    out = pl.pallas_call(
        body, out_shape=(_sds((s, d), F32), _sds((1, d), F32)) + scatter_out_shapes(scatter[:na]), grid=(nt,),
        in_specs=[_rows(tm, PBLK), _rows(tm, MID_W), _rows(tm, LG_W),
                  pl.BlockSpec(w_t.shape, lambda i: (0, 0), pipeline_mode=pl.Buffered(1)), rowb, rowb, _full((1, d))] + [_ANY] * ns,
        out_specs=(rowb, _full((1, d))) + (_ANY,) * ns,
        scratch_shapes=scatter_sems(na) if ns else [],
        compiler_params=_cp("arbitrary"), name="h_bwd_scatter" if ns else "h_bwd")(d0, d1, d2, w_t, x, dx_out, g, *scatter)
    return out[0], out[1], list(out[2:2 + na]), list(out[2 + na:])


def memkv_bwd(mem, g, mem_n, w_kv, dkv):
    m, d = mem.shape

    def body(mem_ref, g_ref, mn_ref, w_ref, dkv_ref, gw_ref, gw16_ref, gg_ref):
        dkb = dkv_ref[...].astype(BF16)
        gw = _dot_tn(mn_ref[...], dkb)
        gw_ref[...] = gw
        gw16_ref[...] = gw.astype(BF16)
        dmn = _dot_nt(dkb, w_ref[...])
        mf = mem_ref[...]
        r = lax.rsqrt(jnp.mean(mf * mf, axis=-1, keepdims=True) + EPS)
        gg_ref[...] = jnp.sum(dmn * (mf * r), axis=0, keepdims=True)

    return pl.pallas_call(
        body, out_shape=(_sds(w_kv.shape, F32), _sds(w_kv.shape, BF16), _sds((1, d), F32)),
        compiler_params=_cp(), name="memkv_bwd")(mem, g, mem_n, w_kv, dkv)


def _layer_consts(seq):
    i = jnp.arange(A_WIDTH)
    return dict(
        tabs=rope_tables(seq),
        gq=_group_ones(A_WIDTH, A_HEAD_DIM).astype(BF16), gk=_group_ones(A_KV_WIDTH, A_HEAD_DIM).astype(BF16),
        fold_q=(i[:, None] % A_HEAD_DIM == jnp.arange(LANES)[None, :]).astype(F32),
        fold_k=(i[:A_KV_WIDTH, None] % A_HEAD_DIM == jnp.arange(LANES)[None, :]).astype(F32),
        head_sel=(jnp.arange(A_HEADS)[:, None] == i[None, :] // A_HEAD_DIM).astype(F32),
    )


_BIG = ("win_t", "wkv", "wbr", "wout")


def _with_own_part(names, gathered, shards, chip, d):
    shape = dict(win_t=(IN_WIDTH, d), wkv=(d, 2 * M_WIDTH), wbr=(N_CHIPS, N_BRANCH, A_WIDTH, d // N_CHIPS), wout=(d, d))
    return {n: lax.dynamic_update_slice(g, sh[None], (chip, 0, 0)).reshape(shape[n]) for n, g, sh in zip(names, gathered, shards)}


def local_fwd_bwd(x, mem, tgt, small, big=None, shards=None, place=None):
    s, d = x.shape
    depth = small["norm_g"].shape[0]
    k = _layer_consts(s)
    row = lambda v: v.reshape(1, -1)
    dist = shards is not None
    if dist:
        big = [None] * depth
    saved = []
    for l in range(depth):
        ng = row(small["norm_g"][l])
        qg = row(jnp.tile(small["q_norm_g"][l], A_HEADS))
        kg = row(jnp.tile(small["k_norm_g"][l], A_KV_HEADS))
        ws = small["w_s"][l].astype(BF16)
        ws_t = jnp.swapaxes(small["w_s"][l], 1, 2).astype(BF16)
        bsb = jnp.broadcast_to(small["b_s"][l][:, :, None], (B_GROUPS, CHUNK, B_GROUP_DIM))
        lng, lnb = row(small["sg_ln_g"][l]), row(small["sg_ln_b"][l])
        mg = row(small["mem_norm_g"][l])
        if l == 0:
            first = tuple(shards[0][:1]) if dist else ()
            h, gathered = rms_fwd(x, ng, gather=first)
            if dist:
                big[0] = _with_own_part(_BIG[:1], gathered, first, place[0], d)
        else:
            h = h_next
        w = big[l]
        proj = proj_fwd(h, w["win_t"])
        q_t, kr, vb, kr_t, vte0, vte1 = qk_prep(proj, k["tabs"], qg, kg, k["gq"], k["gk"])
        late = list(shards[0][1:]) if dist and l == 0 else []
        nxt = list(shards[l + 1]) if dist and l + 1 < depth else []
        o_a, lse, gathered = attn_fwd(q_t, kr, vte0, vte1, gather=tuple(late + nxt))
        if late:
            w.update(_with_own_part(_BIG[1:], gathered[:len(late)], late, place[0], d))
        if nxt:
            big[l + 1] = _with_own_part(_BIG, gathered[len(late):], nxt, place[0], d)
        mem_n, kv = memkv_fwd(mem, mg, w["wkv"])
        next_g = row(small["norm_g"][l + 1]) if l + 1 < depth else row(small["final_g"])
        x_next, y, up, merged, h_next = branch_fwd(x, proj, o_a, kv, ws, bsb, lng, lnb, w["wbr"], w["wout"], next_g)
        saved.append(dict(x=x, ng=ng, qg=qg, kg=kg, ws=ws, ws_t=ws_t, bsb=bsb, lng=lng, lnb=lnb, mg=mg, h=h, proj=proj,
                          q_t=q_t, kr=kr, kr_t=kr_t, vb=vb, o_a=o_a, lse=lse, mem_n=mem_n, kv=kv, y=y, up=up, merged=merged))
        x = x_next

    sq, dx, g_final = final_loss(x, row(small["final_g"]), tgt)
    grads = {n: [None] * depth for n in ("norm_g", "q_norm_g", "k_norm_g", "sg_ln_g", "sg_ln_b", "w_s", "b_s", "mem_norm_g")}
    parts = lambda g: g.reshape(N_CHIPS, -1, g.shape[-1])
    reduced = [[None] * len(_BIG) for _ in range(depth)]

    def reduce_all(items, t_sib, t_rem):
        if items:
            for (ll, a, _, _), f in zip(items, reduce_rows(place, [i[2] for i in items], t_sib, t_rem)):
                reduced[ll][a] = f

    as_scatter = lambda items: tuple(i[2] for i in items) + tuple(i[3] for i in items)
    pending = []
    for l in reversed(range(depth)):
        sv, w = saved[l], big[l]
        dy, dlg, g_wout, g_wbr, g_wout16, g_wbr16 = merge_bwd(dx, sv["proj"], sv["y"], sv["up"], sv["merged"], w["wbr"], w["wout"])
        dmid, do_t, delta, g_ws, g_bs, g_lng, g_lnb, dkv = branch_bwd(
            dy, sv["proj"], sv["o_a"], sv["kv"], sv["ws"], sv["ws_t"], sv["bsb"], sv["lng"], sv["lnb"], k["head_sel"])
        g_wkv, g_wkv16, g_mg = memkv_bwd(mem, sv["mg"], sv["mem_n"], w["wkv"], dkv)
        if dist:
            pending += [(l, 1, parts(g_wkv), parts(g_wkv16)), (l, 2, parts(g_wbr), parts(g_wbr16)), (l, 3, parts(g_wout), parts(g_wout16))]
        dq_t, dkr, dvb, t_sib, t_rem = attn_bwd(sv["q_t"], do_t, sv["kr"], sv["kr_t"], sv["vb"], sv["lse"], delta,
                                                scatter=as_scatter(pending))
        reduce_all(pending, t_sib, t_rem)
        dqkv, g_qg, g_kg = qk_prep_bwd(sv["proj"], dq_t, dkr, dvb, k["tabs"], sv["qg"], sv["kg"], k["gq"], k["gk"],
                                       k["fold_q"], k["fold_k"])
        g_win, g_win16 = win_grad(dqkv, dmid, dlg, sv["h"])
        pending = [(l, 0, parts(g_win), parts(g_win16))] if dist else []
        last = as_scatter(pending) if l == 0 else ()
        dx, g_ng, t_sib, t_rem = h_bwd(dqkv, dmid, dlg, w["win_t"], sv["x"], dx, sv["ng"], scatter=last)
        if last:
            reduce_all(pending, t_sib, t_rem)
        grads["norm_g"][l] = g_ng[0]
        grads["q_norm_g"][l] = g_qg[0, :A_HEAD_DIM]
        grads["k_norm_g"][l] = g_kg[0, :A_HEAD_DIM]
        grads["sg_ln_g"][l] = g_lng[0]
        grads["sg_ln_b"][l] = g_lnb[0]
        grads["w_s"][l] = g_ws
        grads["b_s"][l] = g_bs[:, :, 0]
        grads["mem_norm_g"][l] = g_mg[0]
        if not dist:
            reduced[l] = dict(zip(_BIG, (parts(g_win), parts(g_wkv), parts(g_wbr), parts(g_wout))))
    grads = {n: jnp.stack(v) for n, v in grads.items()}
    grads["final_g"] = g_final[0]
    return sq[0, 0], dx, grads, reduced


def _row_block(rows, width, cap_bytes=2 * 2**20):
    best = None
    for br in range(8, rows + 1, 8):
        if rows % br == 0 and br * width * 4 <= cap_bytes:
            best = br
    return best if best is not None else rows


def adamw(w, gs, m, v):
    r, c = w.shape
    n = len(gs)
    rs = r // n
    br = _row_block(rs, c)
    nb = rs // br

    def body(w_ref, *refs):
        g_refs, (m_ref, v_ref, og_ref, d_ref, nm_ref, nv_ref) = refs[:n], refs[n:]

        def update(gg):
            mm = ADAM_B1 * m_ref[...] + (1.0 - ADAM_B1) * gg
            vv = ADAM_B2 * v_ref[...] + (1.0 - ADAM_B2) * (gg * gg)
            m_hat = mm / (1.0 - ADAM_B1 ** ADAM_STEP)
            v_hat = vv / (1.0 - ADAM_B2 ** ADAM_STEP)
            og_ref[...] = gg
            d_ref[...] = -ADAM_LR * (m_hat / (jnp.sqrt(v_hat) + ADAM_EPS) + ADAM_WD * w_ref[...])
            nm_ref[...] = mm
            nv_ref[...] = vv

        for k in range(n):
            pl.when(pl.program_id(0) == k)(functools.partial(lambda k: update(g_refs[k][...]), k))

    blk = pl.BlockSpec((br, c), lambda l, i: (l * nb + i, 0))
    g_specs = [pl.BlockSpec((br, c), functools.partial(lambda l, i, k: (jnp.where(l == k, i, 0), 0), k=k)) for k in range(n)]
    return pl.pallas_call(
        body, out_shape=(_sds((r, c), F32),) * 4, grid=(n, nb), in_specs=[blk] + g_specs + [blk, blk], out_specs=(blk,) * 4,
        compiler_params=_cp("arbitrary", "arbitrary"), name="adamw")(w, *gs, m, v)


N_REMOTE = 2 * (N_CHIPS - 1)


def reduce_rows(place, gs, t_sibs, t_rems):
    n = len(gs)
    nt = 2

    def body(place_ref, *refs):
        for a in range(n):
            g_ref, s_ref, t_ref, f_ref = refs[a], refs[n + a], refs[2 * n + a], refs[3 * n + a]
            acc = g_ref[...] + s_ref[...]
            for j in range(N_REMOTE):
                acc = acc + t_ref[j].astype(F32)
            f_ref[...] = acc

    tiles = [(g.shape[1] // 2 // nt, g.shape[2]) for g in gs]
    return pl.pallas_call(
        body, out_shape=tuple(_sds(g.shape[1:], F32) for g in gs),
        grid_spec=pltpu.PrefetchScalarGridSpec(
            num_scalar_prefetch=1, grid=(nt,),
            in_specs=[pl.BlockSpec((None, tr, c), lambda i, p: (p[0], p[1] * nt + i, 0)) for tr, c in tiles]
            + [pl.BlockSpec((tr, c), lambda i, p: (i, 0)) for tr, c in tiles]
            + [pl.BlockSpec((N_REMOTE, tr, c), lambda i, p: (0, i, 0)) for tr, c in tiles],
            out_specs=tuple(pl.BlockSpec((tr, c), lambda i, p: (p[1] * nt + i, 0)) for tr, c in tiles)),
        compiler_params=_cp("parallel"), name="reduce_rows")(place, *gs, *t_sibs, *t_rems)


_ANY = pl.BlockSpec(memory_space=pl.ANY)


def _place():
    x, y, c = lax.axis_index("x"), lax.axis_index("y"), lax.axis_index("c")
    chips = [(1 - x, y), (x, 1 - y), (1 - x, 1 - y)]
    return x, y, c, chips


def gather_sems(n):
    return [pltpu.SemaphoreType.DMA((n, N_REMOTE)), pltpu.SemaphoreType.DMA((n, N_REMOTE))]


def gather_stages(shapes, ins, outs, send, recv):
    n = len(shapes)
    x, y, c, chips = _place()
    me = 2 * x + y
    sib = (x, y, 1 - c)

    def rows(a, hl):
        r2 = shapes[a][0] // 2
        return pl.ds(hl * r2, r2)

    def remote(a, k, src, dst, dev):
        return pltpu.make_async_remote_copy(src, dst, send.at[a, k], recv.at[a, k], device_id=dev, device_id_type=MESH)

    def sent(a, k):
        cx, cy = chips[k]
        return remote(a, k, ins[a].at[rows(a, c)], outs[a].at[me, rows(a, c)], (cx, cy, c))

    def got(a, k, hl):
        cx, cy = chips[k]
        return outs[a].at[2 * cx + cy, rows(a, hl)]

    def arrived(a, k):
        return remote(a, k, got(a, k, c), got(a, k, c), (*chips[k], c))

    def passed(a, k, hl):
        return remote(a, 3 + k, got(a, k, hl), got(a, k, hl), sib)

    def start():
        for a in range(n):
            for k in range(3):
                sent(a, k).start()

    def forward():
        for k in range(3):
            for a in range(n):
                arrived(a, k).wait_recv()
                passed(a, k, c).start()

    def finish():
        for k in range(3):
            for a in range(n):
                passed(a, k, 1 - c).wait_recv()
        for k in range(3):
            for a in range(n):
                sent(a, k).wait_send()
                passed(a, k, c).wait_send()

    return start, forward, finish


def scatter_sems(n):
    return [pltpu.SemaphoreType.DMA((n, N_REMOTE + 1)), pltpu.SemaphoreType.DMA((n, N_REMOTE + 1))]


def scatter_out_shapes(gs):
    return (tuple(_sds((g.shape[1] // 2, g.shape[2]), F32) for g in gs)
            + tuple(_sds((N_REMOTE, g.shape[1] // 2, g.shape[2]), BF16) for g in gs))


def scatter_stages(shapes, gf, gb, t_sib, t_rem, send, recv):
    n = len(shapes)
    x, y, c, chips = _place()
    me = 2 * x + y

    def copies():
        out = []
        for a in range(n):
            r2 = shapes[a][0] // 2
            out.append(pltpu.make_async_remote_copy(gf[a].at[me, pl.ds((1 - c) * r2, r2)], t_sib[a], send.at[a, N_REMOTE],
                                                    recv.at[a, N_REMOTE], device_id=(x, y, 1 - c), device_id_type=MESH))
            for k, (cx, cy) in enumerate(chips):
                for o in range(2):
                    tc = c if o == 0 else 1 - c
                    out.append(pltpu.make_async_remote_copy(gb[a].at[2 * cx + cy, pl.ds(tc * r2, r2)], t_rem[a].at[2 * k + o],
                                                            send.at[a, 2 * k + o], recv.at[a, 2 * k + o],
                                                            device_id=(cx, cy, tc), device_id_type=MESH))
        return out

    def start():
        for cp in copies():
            cp.start()

    def finish():
        for cp in copies():
            cp.wait()

    return start, finish


def finish_exchange(v, fs):
    n = len(fs)
    r, w = v.shape
    ndev = 2 * N_CHIPS

    def body(v_ref, *refs):
        out, sum_ref = refs[n:2 * n], refs[2 * n]
        all_ref, send, recv, loc, fsend, frecv = refs[2 * n + 1:]
        x, y, c, chips = _place()
        me, sib = (x, y, c), (x, y, 1 - c)
        swaps = []
        for a in range(n):
            r2 = fs[a].shape[0] // 2
            mine = out[a].at[pl.ds(c * r2, r2)]
            cp = pltpu.make_async_remote_copy(mine, mine, fsend.at[a], frecv.at[a], device_id=sib, device_id_type=MESH)
            cp.start()
            swaps.append(cp)

        def slab(px, py, pc):
            return all_ref.at[4 * px + 2 * py + pc]

        def copy(k, block, to, src=None):
            return pltpu.make_async_remote_copy(slab(*block) if src is None else src, slab(*block), send.at[k], recv.at[k],
                                                device_id=to, device_id_type=MESH)

        mine = pltpu.make_async_copy(v_ref, slab(*me), loc)
        mine.start()
        first = [copy(0, me, sib, src=v_ref)] + [copy(1 + j, me, (*chip, c), src=v_ref) for j, chip in enumerate(chips)]
        for cp in first:
            cp.start()
        passed = [copy(4 + j, (*chip, c), sib) for j, chip in enumerate(chips)]
        for j, chip in enumerate(chips):
            copy(1 + j, (*chip, c), me).wait_recv()
            passed[j].start()
        copy(0, sib, me).wait_recv()
        for j, chip in enumerate(chips):
            copy(4 + j, (*chip, 1 - c), me).wait_recv()
        for cp in first + passed:
            cp.wait_send()
        mine.wait()
        acc = all_ref[0]
        for i in range(1, ndev):
            acc = acc + all_ref[i]
        sum_ref[...] = acc
        for a, cp in enumerate(swaps):
            r2 = fs[a].shape[0] // 2
            theirs = out[a].at[pl.ds((1 - c) * r2, r2)]
            cp.wait_send()
            pltpu.make_async_remote_copy(theirs, theirs, fsend.at[a], frecv.at[a], device_id=sib, device_id_type=MESH).wait_recv()

    vm = pl.BlockSpec(memory_space=pltpu.VMEM)
    res = pl.pallas_call(
        body, out_shape=tuple(_sds(f.shape, F32) for f in fs) + (_sds((r, w), F32),),
        in_specs=[vm] + [_ANY] * n, out_specs=(_ANY,) * n + (vm,), input_output_aliases={a + 1: a for a in range(n)},
        scratch_shapes=[pltpu.VMEM((ndev, r, w), F32), pltpu.SemaphoreType.DMA((7,)), pltpu.SemaphoreType.DMA((7,)),
                        pltpu.SemaphoreType.DMA, pltpu.SemaphoreType.DMA((n,)), pltpu.SemaphoreType.DMA((n,))],
        compiler_params=pltpu.CompilerParams(vmem_limit_bytes=VMEM_LIMIT), name="finish_exchange")(v, *fs)
    return res[n], list(res[:n])


_SMALL = ("norm_g", "q_norm_g", "k_norm_g", "sg_ln_g", "sg_ln_b", "w_s", "b_s", "mem_norm_g", "final_g")
_WEIGHTS = ("norm_g", "w_in", "q_norm_g", "k_norm_g", "sg_ln_g", "sg_ln_b", "w_s", "b_s", "mem_norm_g", "w_mem_kv", "w_br",
            "w_out", "final_g")


def _pack(d, tail=None):
    flat = jnp.concatenate([d[n].reshape(-1) for n in _SMALL] + ([tail.reshape(1)] if tail is not None else []))
    rows = -(-(sum(d[n].size for n in _SMALL) + 1) // (8 * LANES)) * 8
    return jnp.pad(flat, (0, rows * LANES - flat.shape[0])).reshape(rows, LANES)


def _unpack(p, like):
    flat, out, o = p.reshape(-1), {}, 0
    for n in _SMALL:
        out[n] = flat[o:o + like[n].size].reshape(like[n].shape)
        o += like[n].size
    return out


def kernel(x, mem, norm_g, w_in, q_norm_g, k_norm_g, sg_ln_g, sg_ln_b, w_s, b_s, mem_norm_g, w_mem_kv, w_br, w_out, final_g, loss_target, m_norm_g, m_w_in, m_q_norm_g, m_k_norm_g, m_sg_ln_g, m_sg_ln_b, m_w_s, m_b_s, m_mem_norm_g, m_w_mem_kv, m_w_br, m_w_out, m_final_g, v_norm_g, v_w_in, v_q_norm_g, v_k_norm_g, v_sg_ln_g, v_sg_ln_b, v_w_s, v_b_s, v_mem_norm_g, v_w_mem_kv, v_w_br, v_w_out, v_final_g):
    w = dict(norm_g=norm_g, w_in=w_in, q_norm_g=q_norm_g, k_norm_g=k_norm_g, sg_ln_g=sg_ln_g, sg_ln_b=sg_ln_b, w_s=w_s, b_s=b_s,
             mem_norm_g=mem_norm_g, w_mem_kv=w_mem_kv, w_br=w_br, w_out=w_out, final_g=final_g)
    m = dict(norm_g=m_norm_g, w_in=m_w_in, q_norm_g=m_q_norm_g, k_norm_g=m_k_norm_g, sg_ln_g=m_sg_ln_g, sg_ln_b=m_sg_ln_b,
             w_s=m_w_s, b_s=m_b_s, mem_norm_g=m_mem_norm_g, w_mem_kv=m_w_mem_kv, w_br=m_w_br, w_out=m_w_out, final_g=m_final_g)
    v = dict(norm_g=v_norm_g, w_in=v_w_in, q_norm_g=v_q_norm_g, k_norm_g=v_k_norm_g, sg_ln_g=v_sg_ln_g, sg_ln_b=v_sg_ln_b,
             w_s=v_w_s, b_s=v_b_s, mem_norm_g=v_mem_norm_g, w_mem_kv=v_w_mem_kv, w_br=v_w_br, w_out=v_w_out, final_g=v_final_g)
    depth, d = norm_g.shape
    nsh = N_CHIPS
    br_rows = N_BRANCH * A_WIDTH
    br_cols = d // nsh

    shards = [[jnp.swapaxes(w_in[l], 0, 1).astype(BF16), w_mem_kv[l].astype(BF16), w_br[l].astype(BF16).reshape(br_rows, br_cols),
               w_out[l].astype(BF16)] for l in range(depth)]
    place = jnp.stack([2 * lax.axis_index("x") + lax.axis_index("y"), lax.axis_index("c")]).astype(jnp.int32)
    small = {n: w[n] for n in _SMALL}

    sq, dx, grads, reduced = local_fwd_bwd(x[0], mem[0], loss_target[0], small, shards=shards, place=place)

    small_sum, finals = finish_exchange(_pack(grads, tail=sq), [g for layer in reduced for g in layer])
    loss = (0.5 / d) * small_sum.reshape(-1)[sum(small[n].size for n in _SMALL)]
    big_grads = dict(zip(("w_in", "w_mem_kv", "w_br", "w_out"), [finals[a::len(_BIG)] for a in range(len(_BIG))]))
    small_grads = _unpack(small_sum, small)

    out_g, out_d, out_m, out_v = {}, {}, {}, {}
    _, sd, sm, sv = adamw(_pack(small), [small_sum], _pack({n: m[n] for n in _SMALL}), _pack({n: v[n] for n in _SMALL}))
    sd, sm, sv = _unpack(sd, small), _unpack(sm, small), _unpack(sv, small)
    for n in _SMALL:
        out_g[n], out_d[n], out_m[n], out_v[n] = small_grads[n], sd[n], sm[n], sv[n]
    for n, gs in big_grads.items():
        into = (lambda a: jnp.swapaxes(a, 1, 2)) if n == "w_in" else (lambda a: a)
        two_d = lambda a: a.reshape(-1, gs[0].shape[-1])
        res = adamw(two_d(into(w[n])), gs, two_d(into(m[n])), two_d(into(v[n])))
        out_g[n], out_d[n], out_m[n], out_v[n] = [into(t.reshape(into(w[n]).shape)) for t in res]
    return (loss, dx[None], *[out_g[n] for n in _WEIGHTS], *[out_d[n] for n in _WEIGHTS], *[out_m[n] for n in _WEIGHTS],
            *[out_v[n] for n in _WEIGHTS])
```

```python
import functools

import jax
import jax.numpy as jnp
from jax import lax
from jax.experimental import pallas as pl
from jax.experimental.pallas import tpu as pltpu

F32 = jnp.float32
BF16 = jnp.bfloat16

GRID_W = 64
CHUNK = 128
ROPE_THETA = 10000.0
EPS = 1e-6
A_HEADS, A_KV_HEADS, A_HEAD_DIM = 8, 2, 64
A_WIDTH, A_KV_WIDTH = 512, 128
B_GROUPS, B_GROUP_DIM, B_WIDTH = 4, 128, 512
M_HEADS, M_HEAD_DIM, M_WIDTH = 4, 128, 512
N_BRANCH = 3
IN_WIDTH = 6912
O_QA, O_KA, O_VA, O_ZA, O_UB, O_VB, O_ZB, O_QM, O_ZM, O_LG = 0, 512, 640, 768, 1280, 1792, 2304, 2816, 3328, 3840
PBLK = 768
N_PBLK = IN_WIDTH // PBLK
MID_W = 3072
LG_W = 3072

LN2 = 0.6931471805599453
Q_SCALE = A_HEAD_DIM ** -0.5 / LN2
VTE_ROWS = A_HEAD_DIM + 16

ADAM_LR, ADAM_B1, ADAM_B2, ADAM_EPS, ADAM_WD, ADAM_STEP = 0.001, 0.9, 0.999, 1e-08, 0.01, 10

V7X_VMEM_BYTES = 64 * 2**20
VMEM_LIMIT = V7X_VMEM_BYTES - 4 * 2**20
LANES = 128
MESH = pl.DeviceIdType.MESH
N_CHIPS = 4


def _cp(*sem):
    return pltpu.CompilerParams(dimension_semantics=sem if sem else None, vmem_limit_bytes=VMEM_LIMIT)


def _dot(a, b):
    return jnp.dot(a, b, preferred_element_type=F32)


def _dot_nt(a, b):
    return lax.dot_general(a, b, (((1,), (1,)), ((), ())), preferred_element_type=F32)


def _dot_tn(a, b):
    return lax.dot_general(a, b, (((0,), (0,)), ((), ())), preferred_element_type=F32)


def _dot_hi(a, b):
    return jnp.dot(a, b, preferred_element_type=F32, precision=lax.Precision.HIGHEST)


def _group_sum(a, ones):
    hi = a.astype(BF16)
    lo = (a - hi.astype(F32)).astype(BF16)
    return _dot(hi, ones) + _dot(lo, ones)


def _dot_nt_hi(a, b):
    return lax.dot_general(a, b, (((1,), (1,)), ((), ())), preferred_element_type=F32, precision=lax.Precision.HIGHEST)


def _sig(z):
    return 1.0 / (1.0 + jnp.exp(-z))


def _full(shape, once=False):
    nd = len(shape)
    return pl.BlockSpec(shape, lambda *_: (0,) * nd, pipeline_mode=pl.Buffered(1) if once else None)


def _rows(tm, width):
    return pl.BlockSpec((tm, width), lambda i: (i, 0))


def _sds(shape, dtype):
    return jax.ShapeDtypeStruct(shape, dtype)


def rms_fwd(x, g, gather=()):
    s, d = x.shape
    tm = min(s, 512)
    nt = s // tm
    ng = len(gather)

    def body(x_ref, g_ref, *rest):
        g_in, h_ref, g_out = rest[:ng], rest[ng], rest[ng + 1:2 * ng + 1]
        if ng:
            start, forward, finish = gather_stages([a.shape for a in gather], g_in, g_out, *rest[2 * ng + 1:])
            pl.when(pl.program_id(0) == 0)(start)
        xf = x_ref[...]
        r = lax.rsqrt(jnp.mean(xf * xf, axis=-1, keepdims=True) + EPS)
        h_ref[...] = ((xf * r) * g_ref[...]).astype(BF16)
        if ng:
            @pl.when(pl.program_id(0) == nt - 1)
            def _():
                forward()
                finish()

    out = pl.pallas_call(
        body, out_shape=(_sds((s, d), BF16),) + tuple(_sds((N_CHIPS,) + a.shape, a.dtype) for a in gather), grid=(nt,),
        in_specs=[_rows(tm, d), _full((1, d))] + [_ANY] * ng, out_specs=(_rows(tm, d),) + (_ANY,) * ng,
        scratch_shapes=gather_sems(ng) if ng else [],
        compiler_params=_cp("arbitrary"), name="rms_fwd_gather" if ng else "rms_fwd")(x, g, *gather)
    return out[0], list(out[1:])


def proj_fwd(h, w_t, gather=()):
    s, d = h.shape
    n = w_t.shape[0]
    tm = min(s, 1024)
    tn = 2304
    nj, ni = n // tn, s // tm
    ng = len(gather)

    def body(h_ref, w_ref, *rest):
        g_in, o_ref, g_out = rest[:ng], rest[ng], rest[ng + 1:2 * ng + 1]
        step = pl.program_id(0) * ni + pl.program_id(1)
        if ng:
            start, forward, finish = gather_stages([a.shape for a in gather], g_in, g_out, *rest[2 * ng + 1:])
            pl.when(step == 0)(start)
        o_ref[...] = _dot_nt(h_ref[...], w_ref[...]).astype(BF16)
        if ng:
            @pl.when(step == nj * ni - 1)
            def _():
                forward()
                finish()

    out = pl.pallas_call(
        body, out_shape=(_sds((s, n), BF16),) + tuple(_sds((N_CHIPS,) + a.shape, a.dtype) for a in gather), grid=(nj, ni),
        in_specs=[pl.BlockSpec((tm, d), lambda j, i: (i, 0)), pl.BlockSpec((tn, d), lambda j, i: (j, 0))] + [_ANY] * ng,
        out_specs=(pl.BlockSpec((tm, tn), lambda j, i: (i, j)),) + (_ANY,) * ng,
        scratch_shapes=gather_sems(ng) if ng else [],
        compiler_params=_cp("arbitrary", "arbitrary") if ng else _cp("parallel", "parallel"),
        name="proj_fwd_gather" if ng else "proj_fwd")(h, w_t, *gather)
    return out[0], list(out[1:])


def rope_tables(seq):
    n_freq = A_HEAD_DIM // 4
    d = jnp.arange(LANES) % A_HEAD_DIM
    seg, half, freq = d // (2 * n_freq), (d % (2 * n_freq)) // n_freq, d % n_freq
    inv = ROPE_THETA ** (-freq.astype(F32) / n_freq)
    t = jnp.arange(seq)
    pos = jnp.where(seg[None, :] == 0, (t // GRID_W)[:, None], (t % GRID_W)[:, None]).astype(F32)
    ang = pos * inv[None, :]
    cos, sin = jnp.cos(ang), jnp.sin(ang)
    return cos, jnp.where(half[None, :] == 1, sin, 0.0), jnp.where(half[None, :] == 0, -sin, 0.0)


def _group_ones(width, group):
    i = jnp.arange(width)
    return (i[:, None] // group == i[None, :] // group).astype(F32)


def _rope(xn, c, sa, sb):
    w = xn.shape[1]
    return xn * c + pltpu.roll(xn, 16, 1) * sa + pltpu.roll(xn, w - 16, 1) * sb


def _rope_t(dy, c, sa, sb):
    w = dy.shape[1]
    return dy * c + pltpu.roll(dy * sa, w - 16, 1) + pltpu.roll(dy * sb, 16, 1)


def _tile4(t):
    return jnp.concatenate([t, t, t, t], axis=1)


def qk_prep(proj, tabs, qg, kg, gq, gk):
    s = proj.shape[0]
    tm = min(s, 1024)
    c, sa, sb = tabs

    def body(p_ref, c_ref, sa_ref, sb_ref, qg_ref, kg_ref, gq_ref, gk_ref, qt_ref, kr_ref, vb_ref, kt_ref, v0_ref, v1_ref):
        xq = p_ref[:, O_QA:O_QA + A_WIDTH].astype(F32)
        xk = p_ref[:, O_KA:O_KA + A_KV_WIDTH].astype(F32)
        xv = p_ref[:, O_VA:O_VA + A_KV_WIDTH].astype(F32)
        cc, ssa, ssb = c_ref[...], sa_ref[...], sb_ref[...]
        msq = _group_sum(xq * xq, gq_ref[...]) * (1.0 / A_HEAD_DIM)
        qn = (xq * lax.rsqrt(msq + EPS)) * qg_ref[...]
        qr = _rope(qn, _tile4(cc), _tile4(ssa), _tile4(ssb)) * Q_SCALE
        qt_ref[...] = qr.T.astype(BF16)
        msk = _group_sum(xk * xk, gk_ref[...]) * (1.0 / A_HEAD_DIM)
        kn = (xk * lax.rsqrt(msk + EPS)) * kg_ref[...]
        kr = _rope(kn, cc, ssa, ssb)
        kr_ref[...] = kr.astype(BF16)
        vb_ref[...] = xv.astype(BF16)
        kt_ref[...] = kr.T.astype(BF16)
        vt = xv.T.astype(BF16)
        one = jnp.ones((VTE_ROWS - A_HEAD_DIM, tm), BF16)
        v0_ref[...] = jnp.concatenate([vt[:A_HEAD_DIM], one], axis=0)
        v1_ref[...] = jnp.concatenate([vt[A_HEAD_DIM:], one], axis=0)

    tab = _rows(tm, LANES)
    colb = lambda w: pl.BlockSpec((w, tm), lambda i: (0, i))
    return pl.pallas_call(
        body,
        out_shape=(_sds((A_WIDTH, s), BF16), _sds((s, A_KV_WIDTH), BF16), _sds((s, A_KV_WIDTH), BF16),
                   _sds((A_KV_WIDTH, s), BF16), _sds((VTE_ROWS, s), BF16), _sds((VTE_ROWS, s), BF16)),
        grid=(s // tm,),
        in_specs=[_rows(tm, PBLK), tab, tab, tab, _full((1, A_WIDTH)), _full((1, A_KV_WIDTH)),
                  _full((A_WIDTH, A_WIDTH)), _full((A_KV_WIDTH, A_KV_WIDTH))],
        out_specs=(colb(A_WIDTH), _rows(tm, A_KV_WIDTH), _rows(tm, A_KV_WIDTH), colb(A_KV_WIDTH), colb(VTE_ROWS), colb(VTE_ROWS)),
        compiler_params=_cp("parallel"), name="qk_prep")(proj, c, sa, sb, qg, kg, gq, gk)


def _pad_head(q_h, kv):
    z = jnp.zeros_like(q_h)
    return jnp.concatenate([q_h, z], axis=0) if kv == 0 else jnp.concatenate([z, q_h], axis=0)


def attn_fwd(q_t, kr, vte0, vte1, gather=()):
    s = kr.shape[0]
    tq = min(s, 512)
    kc = min(s, 256)
    nkc = s // kc
    nq = s // tq
    grp = A_HEADS // A_KV_HEADS
    ng = len(gather)

    def body(qt_ref, kr_ref, v0_ref, v1_ref, *rest):
        g_in, (o_ref, lse_ref), g_out = rest[:ng], rest[ng:ng + 2], rest[ng + 2:2 * ng + 2]
        qp_ref, m_ref, acc_ref = rest[2 * ng + 2:2 * ng + 5]
        if ng:
            start, forward, finish = gather_stages([g.shape for g in gather], g_in, g_out, *rest[2 * ng + 5:])
            pl.when(pl.program_id(0) == 0)(start)
            pl.when(pl.program_id(0) == (3 * nq) // 4)(forward)

        for h in range(A_HEADS):
            qp_ref[h] = _pad_head(qt_ref[A_HEAD_DIM * h:A_HEAD_DIM * (h + 1), :], h // grp)
        m_ref[...] = jnp.full(m_ref.shape, -1e30, F32)
        acc_ref[...] = jnp.zeros_like(acc_ref)

        def step(ci, carry):
            ks = pl.ds(pl.multiple_of(ci * kc, kc), kc)
            kblk = kr_ref[ks, :]
            vts = (v0_ref[:, ks], v1_ref[:, ks])
            scs = [_dot(kblk, qp_ref[h]) for h in range(A_HEADS)]
            for h in range(A_HEADS):
                sc = scs[h]
                m_prev = m_ref[h:h + 1, :]
                m_new = jnp.maximum(m_prev, jnp.max(sc, axis=0, keepdims=True))
                p = jnp.exp2(sc - m_new)
                acc_ref[h] = acc_ref[h] * jnp.exp2(m_prev - m_new) + _dot(vts[h // grp], p.astype(BF16))
                m_ref[h:h + 1, :] = m_new
            return carry

        lax.fori_loop(0, nkc, step, 0)
        outs, lses = [], []
        for h in range(A_HEADS):
            acc = acc_ref[h]
            l = acc[A_HEAD_DIM:A_HEAD_DIM + 1, :]
            outs.append(acc[:A_HEAD_DIM, :] / l)
            lses.append(m_ref[h:h + 1, :] + jnp.log2(l))
        o_ref[...] = jnp.concatenate(outs, axis=0).T
        lse_ref[...] = jnp.concatenate(lses, axis=0)
        if ng:
            pl.when(pl.program_id(0) == nq - 1)(finish)

    out = pl.pallas_call(
        body,
        out_shape=(_sds((s, A_WIDTH), F32), _sds((A_HEADS, s), F32)) + tuple(_sds((N_CHIPS,) + g.shape, g.dtype) for g in gather),
        grid=(nq,),
        in_specs=[pl.BlockSpec((A_WIDTH, tq), lambda i: (0, i)), _full((s, A_KV_WIDTH)), _full((VTE_ROWS, s)),
                  _full((VTE_ROWS, s))] + [_ANY] * ng,
        out_specs=(_rows(tq, A_WIDTH), pl.BlockSpec((A_HEADS, tq), lambda i: (0, i))) + (_ANY,) * ng,
        scratch_shapes=[pltpu.VMEM((A_HEADS, A_KV_WIDTH, tq), BF16), pltpu.VMEM((A_HEADS, tq), F32),
                        pltpu.VMEM((A_HEADS, VTE_ROWS, tq), F32)] + (gather_sems(ng) if ng else []),
        compiler_params=_cp("arbitrary"), name="attn_fwd_gather" if ng else "attn_fwd")(q_t, kr, vte0, vte1, *gather)
    return out[0], out[1], list(out[2:])


def memkv_fwd(mem, g, w_kv):
    m, d = mem.shape

    def body(mem_ref, g_ref, w_ref, mn_ref, kv_ref):
        mf = mem_ref[...]
        r = lax.rsqrt(jnp.mean(mf * mf, axis=-1, keepdims=True) + EPS)
        mn = ((mf * r) * g_ref[...]).astype(BF16)
        mn_ref[...] = mn
        kv_ref[...] = _dot(mn, w_ref[...]).astype(BF16)

    return pl.pallas_call(
        body, out_shape=(_sds((m, d), BF16), _sds((m, 2 * M_WIDTH), BF16)),
        compiler_params=_cp(), name="memkv_fwd")(mem, g, w_kv)


def _layer_norm_stats(v):
    mu = jnp.mean(v, axis=-1, keepdims=True)
    xc = v - mu
    rstd = lax.rsqrt(jnp.mean(xc * xc, axis=-1, keepdims=True) + EPS)
    return xc * rstd, rstd


def _spatial_mix(vlb, ws_ref, bsb_ref, tm):
    rows = []
    for ci in range(tm // CHUNK):
        cols = []
        for g in range(B_GROUPS):
            blk = vlb[ci * CHUNK:(ci + 1) * CHUNK, g * B_GROUP_DIM:(g + 1) * B_GROUP_DIM]
            cols.append(_dot(ws_ref[g], blk) + bsb_ref[g])
        rows.append(jnp.concatenate(cols, axis=1))
    return jnp.concatenate(rows, axis=0)


def _mem_attn(qm, kv_ref):
    out = []
    for h in range(M_HEADS):
        qh = qm[:, h * M_HEAD_DIM:(h + 1) * M_HEAD_DIM].astype(BF16)
        kh = kv_ref[:, h * M_HEAD_DIM:(h + 1) * M_HEAD_DIM]
        vh = kv_ref[:, M_WIDTH + h * M_HEAD_DIM:M_WIDTH + (h + 1) * M_HEAD_DIM]
        sc = _dot_nt(qh, kh) * (M_HEAD_DIM ** -0.5)
        e = jnp.exp(sc - jnp.max(sc, axis=-1, keepdims=True))
        p = e / jnp.sum(e, axis=-1, keepdims=True)
        out.append((p, _dot(p.astype(BF16), vh)))
    return out


def branch_fwd(x, proj, o_a, kv, ws, bsb, ln_g, ln_b, w_br, w_out, next_g):
    s, d = x.shape
    tm = min(s, 512)

    def body(x_ref, p_ref, oa_ref, kv_ref, ws_ref, bsb_ref, lg_ref, lb_ref, wbr_ref, wo_ref, ng_ref,
             xn_ref, y_ref, up_ref, mg_ref, hn_ref):
        seg = lambda o, w: p_ref[:, o:o + w].astype(F32)
        z_a, u_b, v_b, z_b = seg(O_ZA, A_WIDTH), seg(O_UB, B_WIDTH), seg(O_VB, B_WIDTH), seg(O_ZB, B_WIDTH)
        q_m, z_m = seg(O_QM, M_WIDTH), seg(O_ZM, M_WIDTH)
        xhat, _ = _layer_norm_stats(v_b)
        vln = xhat * lg_ref[...] + lb_ref[...]
        mixed = _spatial_mix(vln.astype(BF16), ws_ref, bsb_ref, tm)
        y_b = (u_b * mixed) * (z_b * _sig(z_b))
        o_m = jnp.concatenate([o for _, o in _mem_attn(q_m, kv_ref)], axis=1)
        y_a = oa_ref[...] * (z_a * _sig(z_a))
        y_m = o_m * (z_m * _sig(z_m))
        merged = None
        for n, yy in enumerate((y_a, y_b, y_m)):
            yb = yy.astype(BF16)
            y_ref[n] = yb
            up = jnp.concatenate([_dot(yb, wbr_ref[c, n]) for c in range(N_CHIPS)], axis=1)
            up_ref[n] = up.astype(BF16)
            t = _sig(seg(O_LG + n * d, d)) * up
            merged = t if merged is None else merged + t
        mb = merged.astype(BF16)
        mg_ref[...] = mb
        xn = x_ref[...] + _dot(mb, wo_ref[...])
        xn_ref[...] = xn
        r = lax.rsqrt(jnp.mean(xn * xn, axis=-1, keepdims=True) + EPS)
        hn_ref[...] = ((xn * r) * ng_ref[...]).astype(BF16)

    return pl.pallas_call(
        body,
        out_shape=(_sds((s, d), F32), _sds((N_BRANCH, s, A_WIDTH), BF16), _sds((N_BRANCH, s, d), BF16), _sds((s, d), BF16),
                   _sds((s, d), BF16)),
        grid=(s // tm,),
        in_specs=[_rows(tm, d), _rows(tm, IN_WIDTH), _rows(tm, A_WIDTH), _full(kv.shape), _full(ws.shape), _full(bsb.shape),
                  _full((1, B_WIDTH)), _full((1, B_WIDTH)), _full(w_br.shape), _full(w_out.shape), _full((1, d))],
        out_specs=(_rows(tm, d), pl.BlockSpec((N_BRANCH, tm, A_WIDTH), lambda i: (0, i, 0)),
                   pl.BlockSpec((N_BRANCH, tm, d), lambda i: (0, i, 0)), _rows(tm, d), _rows(tm, d)),
        compiler_params=_cp("parallel"), name="branch_fwd")(x, proj, o_a, kv, ws, bsb, ln_g, ln_b, w_br, w_out, next_g)


def final_loss(x, fg, tgt):
    s, d = x.shape
    tm = min(s, 512)

    def body(x_ref, g_ref, t_ref, ls_ref, dx_ref, gg_ref):
        @pl.when(pl.program_id(0) == 0)
        def _():
            ls_ref[...] = jnp.zeros_like(ls_ref)
            gg_ref[...] = jnp.zeros_like(gg_ref)

        xf = x_ref[...]
        g = g_ref[...]
        r = lax.rsqrt(jnp.mean(xf * xf, axis=-1, keepdims=True) + EPS)
        xh = xf * r
        e = xh * g - t_ref[...]
        sq = jnp.sum(jnp.sum(e * e, axis=0, keepdims=True), axis=1, keepdims=True)
        ls_ref[...] += jnp.broadcast_to(sq, ls_ref.shape)
        dy = e * (1.0 / d)
        gg_ref[...] += jnp.sum(dy * xh, axis=0, keepdims=True)
        gy = dy * g
        dx_ref[...] = r * (gy - xh * jnp.mean(gy * xh, axis=-1, keepdims=True))

    return pl.pallas_call(
        body, out_shape=(_sds((1, LANES), F32), _sds((s, d), F32), _sds((1, d), F32)), grid=(s // tm,),
        in_specs=[_rows(tm, d), _full((1, d)), _rows(tm, d)],
        out_specs=(_full((1, LANES)), _rows(tm, d), _full((1, d))),
        compiler_params=_cp("arbitrary"), name="final_loss")(x, fg, tgt)


def _pblocks(tm, first, count):
    return [pl.BlockSpec((tm, PBLK), functools.partial(lambda i, b: (i, b), b=first + k)) for k in range(count)]


def merge_bwd(dx, proj, y, up, merged, w_br, w_out):
    s, d = dx.shape
    tm = min(s, 512)
    nlg = LG_W // PBLK
    cw = d // N_CHIPS

    def body(dx_ref, l0, l1, l2, l3, y_ref, up_ref, mg_ref, wbr_ref, wo_ref, dy_ref, dlg_ref, gwo_ref, gwb_ref, gwo16_ref, gwb16_ref):
        @pl.when(pl.program_id(0) == 0)
        def _():
            gwo_ref[...] = jnp.zeros_like(gwo_ref)
            gwb_ref[...] = jnp.zeros_like(gwb_ref)

        dxb = dx_ref[...].astype(BF16)
        dmg = _dot_nt(dxb, wo_ref[...])
        gwo_ref[...] += _dot_tn(mg_ref[...], dxb)
        lg = jnp.concatenate([l0[...], l1[...], l2[...], l3[...]], axis=1).astype(F32)
        for n in range(N_BRANCH):
            g = _sig(lg[:, n * d:(n + 1) * d])
            dup = dmg * g
            dlg_ref[:, n * d:(n + 1) * d] = ((dup * up_ref[n].astype(F32)) * (1.0 - g)).astype(BF16)
            dupb = dup.astype(BF16)
            dyn = None
            for c in range(N_CHIPS):
                blk = dupb[:, c * cw:(c + 1) * cw]
                gwb_ref[c, n] += _dot_tn(y_ref[n], blk)
                t = _dot_nt(blk, wbr_ref[c, n])
                dyn = t if dyn is None else dyn + t
            dy_ref[n] = dyn.astype(BF16)

        @pl.when(pl.program_id(0) == pl.num_programs(0) - 1)
        def _():
            gwo16_ref[...] = gwo_ref[...].astype(BF16)
            gwb16_ref[...] = gwb_ref[...].astype(BF16)

    return pl.pallas_call(
        body,
        out_shape=(_sds((N_BRANCH, s, A_WIDTH), BF16), _sds((s, LG_W), BF16), _sds((d, d), F32), _sds(w_br.shape, F32),
                   _sds((d, d), BF16), _sds(w_br.shape, BF16)),
        grid=(s // tm,),
        in_specs=[_rows(tm, d)] + _pblocks(tm, O_LG // PBLK, nlg) + [
            pl.BlockSpec((N_BRANCH, tm, A_WIDTH), lambda i: (0, i, 0)), pl.BlockSpec((N_BRANCH, tm, d), lambda i: (0, i, 0)),
            _rows(tm, d), _full(w_br.shape, once=True), _full(w_out.shape, once=True)],
        out_specs=(pl.BlockSpec((N_BRANCH, tm, A_WIDTH), lambda i: (0, i, 0)), _rows(tm, LG_W), _full((d, d)), _full(w_br.shape),
                   _full((d, d)), _full(w_br.shape)),
        compiler_params=_cp("arbitrary"), name="merge_bwd")(dx, proj, proj, proj, proj, y, up, merged, w_br, w_out)


def _dsilu(z, sg):
    return sg * (1.0 + z * (1.0 - sg))


def branch_bwd(dy, proj, o_a, kv, ws, ws_t, bsb, ln_g, ln_b, head_sel):
    s = proj.shape[0]
    tm = min(s, 512)
    nmid = MID_W // PBLK

    def body(dy_ref, m0, m1, m2, m3, oa_ref, kv_ref, ws_ref, wst_ref, bsb_ref, lg_ref, lb_ref, sel_ref,
             dmid_ref, dot_ref, dl_ref, gws_ref, gbs_ref, glg_ref, glb_ref, dkv_ref):
        @pl.when(pl.program_id(0) == 0)
        def _():
            for r in (gws_ref, gbs_ref, glg_ref, glb_ref, dkv_ref):
                r[...] = jnp.zeros_like(r)

        mid = jnp.concatenate([m0[...], m1[...], m2[...], m3[...]], axis=1).astype(F32)
        seg = lambda o, w: mid[:, o - O_ZA:o - O_ZA + w]
        z_a, u_b, v_b, z_b = seg(O_ZA, A_WIDTH), seg(O_UB, B_WIDTH), seg(O_VB, B_WIDTH), seg(O_ZB, B_WIDTH)
        q_m, z_m = seg(O_QM, M_WIDTH), seg(O_ZM, M_WIDTH)

        def put(o, v):
            dmid_ref[:, o - O_ZA:o - O_ZA + v.shape[1]] = v.astype(BF16)

        dy_a, dy_b, dy_m = dy_ref[0].astype(F32), dy_ref[1].astype(F32), dy_ref[2].astype(F32)

        o_a_ = oa_ref[...]
        sg = _sig(z_a)
        do_a = dy_a * (z_a * sg)
        put(O_ZA, (dy_a * o_a_) * _dsilu(z_a, sg))
        do_l = do_a * LN2
        dot_ref[...] = do_l.T.astype(BF16)
        dl_ref[...] = _dot_nt_hi(sel_ref[...], do_l * o_a_)

        xhat, rstd = _layer_norm_stats(v_b)
        lng = lg_ref[...]
        vln = xhat * lng + lb_ref[...]
        vlb = vln.astype(BF16)
        mixed = _spatial_mix(vlb, ws_ref, bsb_ref, tm)
        sg = _sig(z_b)
        sl = z_b * sg
        put(O_UB, (dy_b * mixed) * sl)
        put(O_ZB, ((dy_b * u_b) * mixed) * _dsilu(z_b, sg))
        dmix = (dy_b * u_b) * sl
        dmb = dmix.astype(BF16)
        rows = []
        for ci in range(tm // CHUNK):
            cols = []
            for g in range(B_GROUPS):
                rs, cs = slice(ci * CHUNK, (ci + 1) * CHUNK), slice(g * B_GROUP_DIM, (g + 1) * B_GROUP_DIM)
                gws_ref[g] += _dot_nt(dmb[rs, cs], vlb[rs, cs])
                gbs_ref[g] += jnp.broadcast_to(jnp.sum(dmix[rs, cs], axis=1, keepdims=True), (CHUNK, B_GROUP_DIM))
                cols.append(_dot(wst_ref[g], dmb[rs, cs]))
            rows.append(jnp.concatenate(cols, axis=1))
        dvln = jnp.concatenate(rows, axis=0)
        glg_ref[...] += jnp.sum(dvln * xhat, axis=0, keepdims=True)
        glb_ref[...] += jnp.sum(dvln, axis=0, keepdims=True)
        gy = dvln * lng
        put(O_VB, rstd * ((gy - jnp.mean(gy, axis=-1, keepdims=True)) - xhat * jnp.mean(gy * xhat, axis=-1, keepdims=True)))

        sg = _sig(z_m)
        sl = z_m * sg
        heads = _mem_attn(q_m, kv_ref)
        o_m = jnp.concatenate([o for _, o in heads], axis=1)
        put(O_ZM, (dy_m * o_m) * _dsilu(z_m, sg))
        do_m = dy_m * sl
        dqs = []
        for h, (p, o_h) in enumerate(heads):
            hs = slice(h * M_HEAD_DIM, (h + 1) * M_HEAD_DIM)
            vs = slice(M_WIDTH + h * M_HEAD_DIM, M_WIDTH + (h + 1) * M_HEAD_DIM)
            do_h = do_m[:, hs]
            dob = do_h.astype(BF16)
            dp = _dot_nt(dob, kv_ref[:, vs])
            dsc = (p * (dp - jnp.sum(do_h * o_h, axis=-1, keepdims=True))) * (M_HEAD_DIM ** -0.5)
            dsb = dsc.astype(BF16)
            dqs.append(_dot(dsb, kv_ref[:, hs]))
            dkv_ref[:, hs] += _dot_tn(dsb, q_m[:, hs].astype(BF16))
            dkv_ref[:, vs] += _dot_tn(p.astype(BF16), dob)
        put(O_QM, jnp.concatenate(dqs, axis=1))

    return pl.pallas_call(
        body,
        out_shape=(_sds((s, MID_W), BF16), _sds((A_WIDTH, s), BF16), _sds((A_HEADS, s), F32), _sds(ws.shape, F32),
                   _sds(ws.shape, F32), _sds((1, B_WIDTH), F32), _sds((1, B_WIDTH), F32), _sds(kv.shape, F32)),
        grid=(s // tm,),
        in_specs=[pl.BlockSpec((N_BRANCH, tm, A_WIDTH), lambda i: (0, i, 0))] + _pblocks(tm, O_ZA // PBLK, nmid) + [
            _rows(tm, A_WIDTH), _full(kv.shape), _full(ws.shape), _full(ws.shape), _full(bsb.shape),
            _full((1, B_WIDTH)), _full((1, B_WIDTH)), _full(head_sel.shape)],
        out_specs=(_rows(tm, MID_W), pl.BlockSpec((A_WIDTH, tm), lambda i: (0, i)), pl.BlockSpec((A_HEADS, tm), lambda i: (0, i)),
                   _full(ws.shape), _full(ws.shape), _full((1, B_WIDTH)), _full((1, B_WIDTH)), _full(kv.shape)),
        compiler_params=_cp("arbitrary"), name="branch_bwd")(dy, proj, proj, proj, proj, o_a, kv, ws, ws_t, bsb, ln_g, ln_b, head_sel)


def attn_bwd(q_t, do_t, kr, kr_t, vb, lse, delta, scatter=()):
    s = kr.shape[0]
    tq = min(s, 256)
    kc = min(s, 512)
    nkc = s // kc
    nq = s // tq
    grp = A_HEADS // A_KV_HEADS
    ns = len(scatter)
    na = ns // 2

    def body(qt_ref, dot_ref, kr_ref, krt_ref, vb_ref, lse_ref, dl_ref, *rest):
        s_in, (dqt_ref, dk_ref, dv_ref), s_out = rest[:ns], rest[ns:ns + 3], rest[ns + 3:2 * ns + 3]
        qp_ref, dop_ref, dq_ref = rest[2 * ns + 3:2 * ns + 6]
        if ns:
            start, finish = scatter_stages([g.shape[1:] for g in scatter[:na]], s_in[:na], s_in[na:], s_out[:na], s_out[na:],
                                           *rest[2 * ns + 6:])
            pl.when(pl.program_id(0) == 0)(start)

        @pl.when(pl.program_id(0) == 0)
        def _():
            dk_ref[...] = jnp.zeros_like(dk_ref)
            dv_ref[...] = jnp.zeros_like(dv_ref)

        for h in range(A_HEADS):
            hs = slice(A_HEAD_DIM * h, A_HEAD_DIM * (h + 1))
            qp_ref[h] = _pad_head(qt_ref[hs, :], h // grp)
            dop_ref[h] = _pad_head(dot_ref[hs, :], h // grp)
        dq_ref[...] = jnp.zeros_like(dq_ref)

        def step(ci, carry):
            ks = pl.ds(pl.multiple_of(ci * kc, kc), kc)
            kblk, vblk, ktb = kr_ref[ks, :], vb_ref[ks, :], krt_ref[:, ks]
            dv_acc = jnp.zeros((kc, A_KV_WIDTH), F32)
            dk_acc = jnp.zeros((kc, A_KV_WIDTH), F32)
            scs = [_dot(kblk, qp_ref[h]) for h in range(A_HEADS)]
            dps = [_dot(vblk, dop_ref[h]) for h in range(A_HEADS)]
            for h in range(A_HEADS):
                qpad, dopad = qp_ref[h], dop_ref[h]
                p = jnp.exp2(scs[h] - lse_ref[h:h + 1, :])
                dsb = (p * (dps[h] - dl_ref[h:h + 1, :])).astype(BF16)
                dv_acc = dv_acc + _dot_nt(p.astype(BF16), dopad)
                dk_acc = dk_acc + _dot_nt(dsb, qpad)
                dq_ref[h] += _dot(ktb, dsb)
            dv_ref[ks, :] += dv_acc
            dk_ref[ks, :] += dk_acc
            return carry

        lax.fori_loop(0, nkc, step, 0)
        dqt_ref[...] = jnp.concatenate(
            [dq_ref[h][A_HEAD_DIM * (h // grp):A_HEAD_DIM * (h // grp + 1), :] for h in range(A_HEADS)], axis=0)
        if ns:
            pl.when(pl.program_id(0) == nq - 1)(finish)

    colq = pl.BlockSpec((A_WIDTH, tq), lambda i: (0, i))
    colh = pl.BlockSpec((A_HEADS, tq), lambda i: (0, i))
    out = pl.pallas_call(
        body,
        out_shape=(_sds((A_WIDTH, s), F32), _sds((s, A_KV_WIDTH), F32), _sds((s, A_KV_WIDTH), F32)) + scatter_out_shapes(scatter[:na]),
        grid=(nq,),
        in_specs=[colq, colq, _full((s, A_KV_WIDTH)), _full((A_KV_WIDTH, s)), _full((s, A_KV_WIDTH)), colh, colh] + [_ANY] * ns,
        out_specs=(colq, _full((s, A_KV_WIDTH)), _full((s, A_KV_WIDTH))) + (_ANY,) * ns,
        scratch_shapes=[pltpu.VMEM((A_HEADS, A_KV_WIDTH, tq), BF16), pltpu.VMEM((A_HEADS, A_KV_WIDTH, tq), BF16),
                        pltpu.VMEM((A_HEADS, A_KV_WIDTH, tq), F32)] + (scatter_sems(na) if ns else []),
        compiler_params=_cp("arbitrary"), name="attn_bwd_scatter" if ns else "attn_bwd")(
            q_t, do_t, kr, kr_t, vb, lse, delta, *scatter)
    return out[0], out[1], out[2], list(out[3:3 + na]), list(out[3 + na:])


def qk_prep_bwd(proj, dq_t, dkr, dvb, tabs, qg, kg, gq, gk, fold_q, fold_k):
    s = proj.shape[0]
    tm = min(s, 1024)
    c, sa, sb = tabs

    def head_norm_bwd(x, dn, gain, gones, fold):
        ms = _group_sum(x * x, gones) * (1.0 / A_HEAD_DIM)
        r = lax.rsqrt(ms + EPS)
        xh = x * r
        gg = _dot_hi(jnp.sum(dn * xh, axis=0, keepdims=True), fold)
        u = dn * gain
        mean_u = _group_sum(u * xh, gones) * (1.0 / A_HEAD_DIM)
        return r * (u - xh * mean_u), gg

    def body(p_ref, dqt_ref, dk_ref, dv_ref, c_ref, sa_ref, sb_ref, qg_ref, kg_ref, gq_ref, gk_ref, fq_ref, fk_ref,
             dqkv_ref, gqg_ref, gkg_ref):
        @pl.when(pl.program_id(0) == 0)
        def _():
            gqg_ref[...] = jnp.zeros_like(gqg_ref)
            gkg_ref[...] = jnp.zeros_like(gkg_ref)

        cc, ssa, ssb = c_ref[...], sa_ref[...], sb_ref[...]
        dqr = dqt_ref[...].T * Q_SCALE
        dqn = _rope_t(dqr, _tile4(cc), _tile4(ssa), _tile4(ssb))
        dxq, gq_ = head_norm_bwd(p_ref[:, O_QA:O_QA + A_WIDTH].astype(F32), dqn, qg_ref[...], gq_ref[...], fq_ref[...])
        dkn = _rope_t(dk_ref[...], cc, ssa, ssb)
        dxk, gk_ = head_norm_bwd(p_ref[:, O_KA:O_KA + A_KV_WIDTH].astype(F32), dkn, kg_ref[...], gk_ref[...], fk_ref[...])
        gqg_ref[...] += gq_
        gkg_ref[...] += gk_
        dqkv_ref[:, O_QA:O_QA + A_WIDTH] = dxq.astype(BF16)
        dqkv_ref[:, O_KA:O_KA + A_KV_WIDTH] = dxk.astype(BF16)
        dqkv_ref[:, O_VA:O_VA + A_KV_WIDTH] = (dv_ref[...] * (1.0 / LN2)).astype(BF16)

    tab = _rows(tm, LANES)
    return pl.pallas_call(
        body, out_shape=(_sds((s, PBLK), BF16), _sds((1, LANES), F32), _sds((1, LANES), F32)), grid=(s // tm,),
        in_specs=[_rows(tm, PBLK), pl.BlockSpec((A_WIDTH, tm), lambda i: (0, i)), _rows(tm, A_KV_WIDTH), _rows(tm, A_KV_WIDTH),
                  tab, tab, tab, _full((1, A_WIDTH)), _full((1, A_KV_WIDTH)), _full((A_WIDTH, A_WIDTH)),
                  _full((A_KV_WIDTH, A_KV_WIDTH)), _full((A_WIDTH, LANES)), _full((A_KV_WIDTH, LANES))],
        out_specs=(_rows(tm, PBLK), _full((1, LANES)), _full((1, LANES))),
        compiler_params=_cp("arbitrary"), name="qk_prep_bwd")(proj, dq_t, dkr, dvb, c, sa, sb, qg, kg, gq, gk, fold_q, fold_k)


def _pick_dproj(b, d0, d1, d2, use):
    first_lg = 1 + MID_W // PBLK

    @pl.when(b == 0)
    def _():
        use(d0[...])

    @pl.when(jnp.logical_and(b >= 1, b < first_lg))
    def _():
        use(d1[...])

    @pl.when(b >= first_lg)
    def _():
        use(d2[...])


def win_grad(d0, d1, d2, h):
    s, d = h.shape
    tk = min(s, 4096)
    nk = s // tk

    def body(d0_ref, d1_ref, d2_ref, h_ref, o_ref, o16_ref):
        @pl.when(pl.program_id(1) == 0)
        def _():
            o_ref[...] = jnp.zeros_like(o_ref)

        def use(blk):
            o_ref[...] += _dot_tn(blk, h_ref[...])

        _pick_dproj(pl.program_id(0), d0_ref, d1_ref, d2_ref, use)

        @pl.when(pl.program_id(1) == nk - 1)
        def _():
            o16_ref[...] = o_ref[...].astype(BF16)

    def spec(first, count):
        def imap(j, k):
            used = jnp.logical_and(j >= first, j < first + count)
            return (jnp.where(used, k, 0), jnp.clip(j - first, 0, count - 1))
        return pl.BlockSpec((tk, PBLK), imap)

    nm = MID_W // PBLK
    oblk = pl.BlockSpec((PBLK, d), lambda j, k: (j, 0))
    return pl.pallas_call(
        body, out_shape=(_sds((IN_WIDTH, d), F32), _sds((IN_WIDTH, d), BF16)), grid=(N_PBLK, nk),
        in_specs=[spec(0, 1), spec(1, nm), spec(1 + nm, LG_W // PBLK),
                  pl.BlockSpec((tk, d), lambda j, k: (k, 0), pipeline_mode=pl.Buffered(1) if nk == 1 else None)],
        out_specs=(oblk, oblk),
        compiler_params=_cp("parallel", "arbitrary"), name="win_grad")(d0, d1, d2, h)


def h_bwd(d0, d1, d2, w_t, x, dx_out, g, scatter=()):
    s, d = x.shape
    tm = min(s, 512)
    nt = s // tm
    ns = len(scatter)
    na = ns // 2

    def body(d0_ref, d1_ref, d2_ref, w_ref, x_ref, dxo_ref, g_ref, *rest):
        s_in, (dx_ref, gg_ref), s_out = rest[:ns], rest[ns:ns + 2], rest[ns + 2:2 * ns + 2]
        if ns:
            start, finish = scatter_stages([a.shape[1:] for a in scatter[:na]], s_in[:na], s_in[na:], s_out[:na], s_out[na:],
                                           *rest[2 * ns + 2:])
            pl.when(pl.program_id(0) == 0)(start)

        @pl.when(pl.program_id(0) == 0)
        def _():
            gg_ref[...] = jnp.zeros_like(gg_ref)

        dh = (_dot(d0_ref[...], w_ref[0:PBLK, :]) + _dot(d1_ref[...], w_ref[PBLK:PBLK + MID_W, :])
              + _dot(d2_ref[...], w_ref[PBLK + MID_W:, :]))
        xf = x_ref[...]
        r = lax.rsqrt(jnp.mean(xf * xf, axis=-1, keepdims=True) + EPS)
        xh = xf * r
        gg_ref[...] += jnp.sum(dh * xh, axis=0, keepdims=True)
        u = dh * g_ref[...]
        dx_ref[...] = dxo_ref[...] + r * (u - xh * jnp.mean(u * xh, axis=-1, keepdims=True))
        if ns:
            pl.when(pl.program_id(0) == nt - 1)(finish)

    rowb = _rows(tm, d)
    out = pl.pallas_call(
        body, out_shape=(_sds((s, d), F32), _sds((1, d), F32)) + scatter_out_shapes(scatter[:na]), grid=(nt,),
        in_specs=[_rows(tm, PBLK), _rows(tm, MID_W), _rows(tm, LG_W),
                  pl.BlockSpec(w_t.shape, lambda i: (0, 0), pipeline_mode=pl.Buffered(1)), rowb, rowb, _full((1, d))] + [_ANY] * ns,
        out_specs=(rowb, _full((1, d))) + (_ANY,) * ns,
        scratch_shapes=scatter_sems(na) if ns else [],
        compiler_params=_cp("arbitrary"), name="h_bwd_scatter" if ns else "h_bwd")(d0, d1, d2, w_t, x, dx_out, g, *scatter)
    return out[0], out[1], list(out[2:2 + na]), list(out[2 + na:])


def memkv_bwd(mem, g, mem_n, w_kv, dkv):
    m, d = mem.shape

    def body(mem_ref, g_ref, mn_ref, w_ref, dkv_ref, gw_ref, gw16_ref, gg_ref):
        dkb = dkv_ref[...].astype(BF16)
        gw = _dot_tn(mn_ref[...], dkb)
        gw_ref[...] = gw
        gw16_ref[...] = gw.astype(BF16)
        dmn = _dot_nt(dkb, w_ref[...])
        mf = mem_ref[...]
        r = lax.rsqrt(jnp.mean(mf * mf, axis=-1, keepdims=True) + EPS)
        gg_ref[...] = jnp.sum(dmn * (mf * r), axis=0, keepdims=True)

    return pl.pallas_call(
        body, out_shape=(_sds(w_kv.shape, F32), _sds(w_kv.shape, BF16), _sds((1, d), F32)),
        compiler_params=_cp(), name="memkv_bwd")(mem, g, mem_n, w_kv, dkv)


def _layer_consts(seq):
    i = jnp.arange(A_WIDTH)
    return dict(
        tabs=rope_tables(seq),
        gq=_group_ones(A_WIDTH, A_HEAD_DIM).astype(BF16), gk=_group_ones(A_KV_WIDTH, A_HEAD_DIM).astype(BF16),
        fold_q=(i[:, None] % A_HEAD_DIM == jnp.arange(LANES)[None, :]).astype(F32),
        fold_k=(i[:A_KV_WIDTH, None] % A_HEAD_DIM == jnp.arange(LANES)[None, :]).astype(F32),
        head_sel=(jnp.arange(A_HEADS)[:, None] == i[None, :] // A_HEAD_DIM).astype(F32),
    )


_BIG = ("win_t", "wkv", "wbr", "wout")


def _with_own_part(names, gathered, shards, chip, d):
    shape = dict(win_t=(IN_WIDTH, d), wkv=(d, 2 * M_WIDTH), wbr=(N_CHIPS, N_BRANCH, A_WIDTH, d // N_CHIPS), wout=(d, d))
    return {n: lax.dynamic_update_slice(g, sh[None], (chip, 0, 0)).reshape(shape[n]) for n, g, sh in zip(names, gathered, shards)}


def local_fwd_bwd(x, mem, tgt, small, big=None, shards=None, place=None):
    s, d = x.shape
    depth = small["norm_g"].shape[0]
    k = _layer_consts(s)
    row = lambda v: v.reshape(1, -1)
    dist = shards is not None
    if dist:
        big = [None] * depth
    saved = []
    for l in range(depth):
        ng = row(small["norm_g"][l])
        qg = row(jnp.tile(small["q_norm_g"][l], A_HEADS))
        kg = row(jnp.tile(small["k_norm_g"][l], A_KV_HEADS))
        ws = small["w_s"][l].astype(BF16)
        ws_t = jnp.swapaxes(small["w_s"][l], 1, 2).astype(BF16)
        bsb = jnp.broadcast_to(small["b_s"][l][:, :, None], (B_GROUPS, CHUNK, B_GROUP_DIM))
        lng, lnb = row(small["sg_ln_g"][l]), row(small["sg_ln_b"][l])
        mg = row(small["mem_norm_g"][l])
        if l == 0:
            first = tuple(shards[0][:1]) if dist else ()
            h, gathered = rms_fwd(x, ng, gather=first)
            if dist:
                big[0] = _with_own_part(_BIG[:1], gathered, first, place[0], d)
        else:
            h = h_next
        w = big[l]
        late = tuple(shards[0][1:]) if dist and l == 0 else ()
        proj, gathered = proj_fwd(h, w["win_t"], gather=late)
        if late:
            w.update(_with_own_part(_BIG[1:], gathered, late, place[0], d))
        q_t, kr, vb, kr_t, vte0, vte1 = qk_prep(proj, k["tabs"], qg, kg, k["gq"], k["gk"])
        nxt = tuple(shards[l + 1]) if dist and l + 1 < depth else ()
        o_a, lse, gathered = attn_fwd(q_t, kr, vte0, vte1, gather=nxt)
        if nxt:
            big[l + 1] = _with_own_part(_BIG, gathered, nxt, place[0], d)
        mem_n, kv = memkv_fwd(mem, mg, w["wkv"])
        next_g = row(small["norm_g"][l + 1]) if l + 1 < depth else row(small["final_g"])
        x_next, y, up, merged, h_next = branch_fwd(x, proj, o_a, kv, ws, bsb, lng, lnb, w["wbr"], w["wout"], next_g)
        saved.append(dict(x=x, ng=ng, qg=qg, kg=kg, ws=ws, ws_t=ws_t, bsb=bsb, lng=lng, lnb=lnb, mg=mg, h=h, proj=proj,
                          q_t=q_t, kr=kr, kr_t=kr_t, vb=vb, o_a=o_a, lse=lse, mem_n=mem_n, kv=kv, y=y, up=up, merged=merged))
        x = x_next

    sq, dx, g_final = final_loss(x, row(small["final_g"]), tgt)
    grads = {n: [None] * depth for n in ("norm_g", "q_norm_g", "k_norm_g", "sg_ln_g", "sg_ln_b", "w_s", "b_s", "mem_norm_g")}
    parts = lambda g: g.reshape(N_CHIPS, -1, g.shape[-1])
    reduced = [[None] * len(_BIG) for _ in range(depth)]

    def reduce_all(items, t_sib, t_rem):
        if items:
            for (ll, a, _, _), f in zip(items, reduce_rows(place, [i[2] for i in items], t_sib, t_rem)):
                reduced[ll][a] = f

    as_scatter = lambda items: tuple(i[2] for i in items) + tuple(i[3] for i in items)
    pending = []
    for l in reversed(range(depth)):
        sv, w = saved[l], big[l]
        dy, dlg, g_wout, g_wbr, g_wout16, g_wbr16 = merge_bwd(dx, sv["proj"], sv["y"], sv["up"], sv["merged"], w["wbr"], w["wout"])
        dmid, do_t, delta, g_ws, g_bs, g_lng, g_lnb, dkv = branch_bwd(
            dy, sv["proj"], sv["o_a"], sv["kv"], sv["ws"], sv["ws_t"], sv["bsb"], sv["lng"], sv["lnb"], k["head_sel"])
        g_wkv, g_wkv16, g_mg = memkv_bwd(mem, sv["mg"], sv["mem_n"], w["wkv"], dkv)
        if dist:
            pending += [(l, 1, parts(g_wkv), parts(g_wkv16)), (l, 2, parts(g_wbr), parts(g_wbr16)), (l, 3, parts(g_wout), parts(g_wout16))]
        dq_t, dkr, dvb, t_sib, t_rem = attn_bwd(sv["q_t"], do_t, sv["kr"], sv["kr_t"], sv["vb"], sv["lse"], delta,
                                                scatter=as_scatter(pending))
        reduce_all(pending, t_sib, t_rem)
        dqkv, g_qg, g_kg = qk_prep_bwd(sv["proj"], dq_t, dkr, dvb, k["tabs"], sv["qg"], sv["kg"], k["gq"], k["gk"],
                                       k["fold_q"], k["fold_k"])
        g_win, g_win16 = win_grad(dqkv, dmid, dlg, sv["h"])
        pending = [(l, 0, parts(g_win), parts(g_win16))] if dist else []
        last = as_scatter(pending) if l == 0 else ()
        dx, g_ng, t_sib, t_rem = h_bwd(dqkv, dmid, dlg, w["win_t"], sv["x"], dx, sv["ng"], scatter=last)
        if last:
            reduce_all(pending, t_sib, t_rem)
        grads["norm_g"][l] = g_ng[0]
        grads["q_norm_g"][l] = g_qg[0, :A_HEAD_DIM]
        grads["k_norm_g"][l] = g_kg[0, :A_HEAD_DIM]
        grads["sg_ln_g"][l] = g_lng[0]
        grads["sg_ln_b"][l] = g_lnb[0]
        grads["w_s"][l] = g_ws
        grads["b_s"][l] = g_bs[:, :, 0]
        grads["mem_norm_g"][l] = g_mg[0]
        if not dist:
            reduced[l] = dict(zip(_BIG, (parts(g_win), parts(g_wkv), parts(g_wbr), parts(g_wout))))
    grads = {n: jnp.stack(v) for n, v in grads.items()}
    grads["final_g"] = g_final[0]
    return sq[0, 0], dx, grads, reduced


def _row_block(rows, width, cap_bytes=2 * 2**20):
    best = None
    for br in range(8, rows + 1, 8):
        if rows % br == 0 and br * width * 4 <= cap_bytes:
            best = br
    return best if best is not None else rows


def adamw(w, gs, m, v):
    r, c = w.shape
    n = len(gs)
    rs = r // n
    br = _row_block(rs, c)
    nb = rs // br

    def body(w_ref, *refs):
        g_refs, (m_ref, v_ref, og_ref, d_ref, nm_ref, nv_ref) = refs[:n], refs[n:]

        def update(gg):
            mm = ADAM_B1 * m_ref[...] + (1.0 - ADAM_B1) * gg
            vv = ADAM_B2 * v_ref[...] + (1.0 - ADAM_B2) * (gg * gg)
            m_hat = mm / (1.0 - ADAM_B1 ** ADAM_STEP)
            v_hat = vv / (1.0 - ADAM_B2 ** ADAM_STEP)
            og_ref[...] = gg
            d_ref[...] = -ADAM_LR * (m_hat / (jnp.sqrt(v_hat) + ADAM_EPS) + ADAM_WD * w_ref[...])
            nm_ref[...] = mm
            nv_ref[...] = vv

        for k in range(n):
            pl.when(pl.program_id(0) == k)(functools.partial(lambda k: update(g_refs[k][...]), k))

    blk = pl.BlockSpec((br, c), lambda l, i: (l * nb + i, 0))
    g_specs = [pl.BlockSpec((br, c), functools.partial(lambda l, i, k: (jnp.where(l == k, i, 0), 0), k=k)) for k in range(n)]
    return pl.pallas_call(
        body, out_shape=(_sds((r, c), F32),) * 4, grid=(n, nb), in_specs=[blk] + g_specs + [blk, blk], out_specs=(blk,) * 4,
        compiler_params=_cp("arbitrary", "arbitrary"), name="adamw")(w, *gs, m, v)


N_REMOTE = 2 * (N_CHIPS - 1)


def reduce_rows(place, gs, t_sibs, t_rems):
    n = len(gs)
    nt = 2

    def body(place_ref, *refs):
        for a in range(n):
            g_ref, s_ref, t_ref, f_ref = refs[a], refs[n + a], refs[2 * n + a], refs[3 * n + a]
            acc = g_ref[...] + s_ref[...]
            for j in range(N_REMOTE):
                acc = acc + t_ref[j].astype(F32)
            f_ref[...] = acc

    tiles = [(g.shape[1] // 2 // nt, g.shape[2]) for g in gs]
    return pl.pallas_call(
        body, out_shape=tuple(_sds(g.shape[1:], F32) for g in gs),
        grid_spec=pltpu.PrefetchScalarGridSpec(
            num_scalar_prefetch=1, grid=(nt,),
            in_specs=[pl.BlockSpec((None, tr, c), lambda i, p: (p[0], p[1] * nt + i, 0)) for tr, c in tiles]
            + [pl.BlockSpec((tr, c), lambda i, p: (i, 0)) for tr, c in tiles]
            + [pl.BlockSpec((N_REMOTE, tr, c), lambda i, p: (0, i, 0)) for tr, c in tiles],
            out_specs=tuple(pl.BlockSpec((tr, c), lambda i, p: (p[1] * nt + i, 0)) for tr, c in tiles)),
        compiler_params=_cp("parallel"), name="reduce_rows")(place, *gs, *t_sibs, *t_rems)


_ANY = pl.BlockSpec(memory_space=pl.ANY)


def _place():
    x, y, c = lax.axis_index("x"), lax.axis_index("y"), lax.axis_index("c")
    chips = [(1 - x, y), (x, 1 - y), (1 - x, 1 - y)]
    return x, y, c, chips


def gather_sems(n):
    return [pltpu.SemaphoreType.DMA((n, N_REMOTE)), pltpu.SemaphoreType.DMA((n, N_REMOTE))]


def gather_stages(shapes, ins, outs, send, recv):
    n = len(shapes)
    x, y, c, chips = _place()
    me = 2 * x + y
    sib = (x, y, 1 - c)

    def rows(a, hl):
        r2 = shapes[a][0] // 2
        return pl.ds(hl * r2, r2)

    def remote(a, k, src, dst, dev):
        return pltpu.make_async_remote_copy(src, dst, send.at[a, k], recv.at[a, k], device_id=dev, device_id_type=MESH)

    def sent(a, k):
        cx, cy = chips[k]
        return remote(a, k, ins[a].at[rows(a, c)], outs[a].at[me, rows(a, c)], (cx, cy, c))

    def got(a, k, hl):
        cx, cy = chips[k]
        return outs[a].at[2 * cx + cy, rows(a, hl)]

    def arrived(a, k):
        return remote(a, k, got(a, k, c), got(a, k, c), (*chips[k], c))

    def passed(a, k, hl):
        return remote(a, 3 + k, got(a, k, hl), got(a, k, hl), sib)

    def start():
        for a in range(n):
            for k in range(3):
                sent(a, k).start()

    def forward():
        for k in range(3):
            for a in range(n):
                arrived(a, k).wait_recv()
                passed(a, k, c).start()

    def finish():
        for k in range(3):
            for a in range(n):
                passed(a, k, 1 - c).wait_recv()
        for k in range(3):
            for a in range(n):
                sent(a, k).wait_send()
                passed(a, k, c).wait_send()

    return start, forward, finish


def scatter_sems(n):
    return [pltpu.SemaphoreType.DMA((n, N_REMOTE + 1)), pltpu.SemaphoreType.DMA((n, N_REMOTE + 1))]


def scatter_out_shapes(gs):
    return (tuple(_sds((g.shape[1] // 2, g.shape[2]), F32) for g in gs)
            + tuple(_sds((N_REMOTE, g.shape[1] // 2, g.shape[2]), BF16) for g in gs))


def scatter_stages(shapes, gf, gb, t_sib, t_rem, send, recv):
    n = len(shapes)
    x, y, c, chips = _place()
    me = 2 * x + y

    def copies():
        out = []
        for a in range(n):
            r2 = shapes[a][0] // 2
            out.append(pltpu.make_async_remote_copy(gf[a].at[me, pl.ds((1 - c) * r2, r2)], t_sib[a], send.at[a, N_REMOTE],
                                                    recv.at[a, N_REMOTE], device_id=(x, y, 1 - c), device_id_type=MESH))
            for k, (cx, cy) in enumerate(chips):
                for o in range(2):
                    tc = c if o == 0 else 1 - c
                    out.append(pltpu.make_async_remote_copy(gb[a].at[2 * cx + cy, pl.ds(tc * r2, r2)], t_rem[a].at[2 * k + o],
                                                            send.at[a, 2 * k + o], recv.at[a, 2 * k + o],
                                                            device_id=(cx, cy, tc), device_id_type=MESH))
        return out

    def start():
        for cp in copies():
            cp.start()

    def finish():
        for cp in copies():
            cp.wait()

    return start, finish


def finish_exchange(v, fs):
    n = len(fs)
    r, w = v.shape
    ndev = 2 * N_CHIPS

    def body(v_ref, *refs):
        out, sum_ref = refs[n:2 * n], refs[2 * n]
        all_ref, send, recv, loc, fsend, frecv = refs[2 * n + 1:]
        x, y, c, chips = _place()
        me, sib = (x, y, c), (x, y, 1 - c)
        swaps = []
        for a in range(n):
            r2 = fs[a].shape[0] // 2
            mine = out[a].at[pl.ds(c * r2, r2)]
            cp = pltpu.make_async_remote_copy(mine, mine, fsend.at[a], frecv.at[a], device_id=sib, device_id_type=MESH)
            cp.start()
            swaps.append(cp)

        def slab(px, py, pc):
            return all_ref.at[4 * px + 2 * py + pc]

        def copy(k, block, to, src=None):
            return pltpu.make_async_remote_copy(slab(*block) if src is None else src, slab(*block), send.at[k], recv.at[k],
                                                device_id=to, device_id_type=MESH)

        mine = pltpu.make_async_copy(v_ref, slab(*me), loc)
        mine.start()
        first = [copy(0, me, sib, src=v_ref)] + [copy(1 + j, me, (*chip, c), src=v_ref) for j, chip in enumerate(chips)]
        for cp in first:
            cp.start()
        passed = [copy(4 + j, (*chip, c), sib) for j, chip in enumerate(chips)]
        for j, chip in enumerate(chips):
            copy(1 + j, (*chip, c), me).wait_recv()
            passed[j].start()
        copy(0, sib, me).wait_recv()
        for j, chip in enumerate(chips):
            copy(4 + j, (*chip, 1 - c), me).wait_recv()
        for cp in first + passed:
            cp.wait_send()
        mine.wait()
        acc = all_ref[0]
        for i in range(1, ndev):
            acc = acc + all_ref[i]
        sum_ref[...] = acc
        for a, cp in enumerate(swaps):
            r2 = fs[a].shape[0] // 2
            theirs = out[a].at[pl.ds((1 - c) * r2, r2)]
            cp.wait_send()
            pltpu.make_async_remote_copy(theirs, theirs, fsend.at[a], frecv.at[a], device_id=sib, device_id_type=MESH).wait_recv()

    vm = pl.BlockSpec(memory_space=pltpu.VMEM)
    res = pl.pallas_call(
        body, out_shape=tuple(_sds(f.shape, F32) for f in fs) + (_sds((r, w), F32),),
        in_specs=[vm] + [_ANY] * n, out_specs=(_ANY,) * n + (vm,), input_output_aliases={a + 1: a for a in range(n)},
        scratch_shapes=[pltpu.VMEM((ndev, r, w), F32), pltpu.SemaphoreType.DMA((7,)), pltpu.SemaphoreType.DMA((7,)),
                        pltpu.SemaphoreType.DMA, pltpu.SemaphoreType.DMA((n,)), pltpu.SemaphoreType.DMA((n,))],
        compiler_params=pltpu.CompilerParams(vmem_limit_bytes=VMEM_LIMIT), name="finish_exchange")(v, *fs)
    return res[n], list(res[:n])


_SMALL = ("norm_g", "q_norm_g", "k_norm_g", "sg_ln_g", "sg_ln_b", "w_s", "b_s", "mem_norm_g", "final_g")
_WEIGHTS = ("norm_g", "w_in", "q_norm_g", "k_norm_g", "sg_ln_g", "sg_ln_b", "w_s", "b_s", "mem_norm_g", "w_mem_kv", "w_br",
            "w_out", "final_g")


def _pack(d, tail=None):
    flat = jnp.concatenate([d[n].reshape(-1) for n in _SMALL] + ([tail.reshape(1)] if tail is not None else []))
    rows = -(-(sum(d[n].size for n in _SMALL) + 1) // (8 * LANES)) * 8
    return jnp.pad(flat, (0, rows * LANES - flat.shape[0])).reshape(rows, LANES)


def _unpack(p, like):
    flat, out, o = p.reshape(-1), {}, 0
    for n in _SMALL:
        out[n] = flat[o:o + like[n].size].reshape(like[n].shape)
        o += like[n].size
    return out


def kernel(x, mem, norm_g, w_in, q_norm_g, k_norm_g, sg_ln_g, sg_ln_b, w_s, b_s, mem_norm_g, w_mem_kv, w_br, w_out, final_g, loss_target, m_norm_g, m_w_in, m_q_norm_g, m_k_norm_g, m_sg_ln_g, m_sg_ln_b, m_w_s, m_b_s, m_mem_norm_g, m_w_mem_kv, m_w_br, m_w_out, m_final_g, v_norm_g, v_w_in, v_q_norm_g, v_k_norm_g, v_sg_ln_g, v_sg_ln_b, v_w_s, v_b_s, v_mem_norm_g, v_w_mem_kv, v_w_br, v_w_out, v_final_g):
    w = dict(norm_g=norm_g, w_in=w_in, q_norm_g=q_norm_g, k_norm_g=k_norm_g, sg_ln_g=sg_ln_g, sg_ln_b=sg_ln_b, w_s=w_s, b_s=b_s,
             mem_norm_g=mem_norm_g, w_mem_kv=w_mem_kv, w_br=w_br, w_out=w_out, final_g=final_g)
    m = dict(norm_g=m_norm_g, w_in=m_w_in, q_norm_g=m_q_norm_g, k_norm_g=m_k_norm_g, sg_ln_g=m_sg_ln_g, sg_ln_b=m_sg_ln_b,
             w_s=m_w_s, b_s=m_b_s, mem_norm_g=m_mem_norm_g, w_mem_kv=m_w_mem_kv, w_br=m_w_br, w_out=m_w_out, final_g=m_final_g)
    v = dict(norm_g=v_norm_g, w_in=v_w_in, q_norm_g=v_q_norm_g, k_norm_g=v_k_norm_g, sg_ln_g=v_sg_ln_g, sg_ln_b=v_sg_ln_b,
             w_s=v_w_s, b_s=v_b_s, mem_norm_g=v_mem_norm_g, w_mem_kv=v_w_mem_kv, w_br=v_w_br, w_out=v_w_out, final_g=v_final_g)
    depth, d = norm_g.shape
    nsh = N_CHIPS
    br_rows = N_BRANCH * A_WIDTH
    br_cols = d // nsh

    shards = [[jnp.swapaxes(w_in[l], 0, 1).astype(BF16), w_mem_kv[l].astype(BF16), w_br[l].astype(BF16).reshape(br_rows, br_cols),
               w_out[l].astype(BF16)] for l in range(depth)]
    place = jnp.stack([2 * lax.axis_index("x") + lax.axis_index("y"), lax.axis_index("c")]).astype(jnp.int32)
    small = {n: w[n] for n in _SMALL}

    sq, dx, grads, reduced = local_fwd_bwd(x[0], mem[0], loss_target[0], small, shards=shards, place=place)

    small_sum, finals = finish_exchange(_pack(grads, tail=sq), [g for layer in reduced for g in layer])
    loss = (0.5 / d) * small_sum.reshape(-1)[sum(small[n].size for n in _SMALL)]
    big_grads = dict(zip(("w_in", "w_mem_kv", "w_br", "w_out"), [finals[a::len(_BIG)] for a in range(len(_BIG))]))
    small_grads = _unpack(small_sum, small)

    out_g, out_d, out_m, out_v = {}, {}, {}, {}
    _, sd, sm, sv = adamw(_pack(small), [small_sum], _pack({n: m[n] for n in _SMALL}), _pack({n: v[n] for n in _SMALL}))
    sd, sm, sv = _unpack(sd, small), _unpack(sm, small), _unpack(sv, small)
    for n in _SMALL:
        out_g[n], out_d[n], out_m[n], out_v[n] = small_grads[n], sd[n], sm[n], sv[n]
    for n, gs in big_grads.items():
        into = (lambda a: jnp.swapaxes(a, 1, 2)) if n == "w_in" else (lambda a: a)
        two_d = lambda a: a.reshape(-1, gs[0].shape[-1])
        res = adamw(two_d(into(w[n])), gs, two_d(into(m[n])), two_d(into(v[n])))
        out_g[n], out_d[n], out_m[n], out_v[n] = [into(t.reshape(into(w[n]).shape)) for t in res]
    return (loss, dx[None], *[out_g[n] for n in _WEIGHTS], *[out_d[n] for n in _WEIGHTS], *[out_m[n] for n in _WEIGHTS],
            *[out_v[n] for n in _WEIGHTS])
```

```python
import functools

import jax
import jax.numpy as jnp
from jax import lax
from jax.experimental import pallas as pl
from jax.experimental.pallas import tpu as pltpu

F32 = jnp.float32
BF16 = jnp.bfloat16

GRID_W = 64
CHUNK = 128
ROPE_THETA = 10000.0
EPS = 1e-6
A_HEADS, A_KV_HEADS, A_HEAD_DIM = 8, 2, 64
A_WIDTH, A_KV_WIDTH = 512, 128
B_GROUPS, B_GROUP_DIM, B_WIDTH = 4, 128, 512
M_HEADS, M_HEAD_DIM, M_WIDTH = 4, 128, 512
N_BRANCH = 3
IN_WIDTH = 6912
O_QA, O_KA, O_VA, O_ZA, O_UB, O_VB, O_ZB, O_QM, O_ZM, O_LG = 0, 512, 640, 768, 1280, 1792, 2304, 2816, 3328, 3840
PBLK = 768
N_PBLK = IN_WIDTH // PBLK
MID_W = 3072
LG_W = 3072

LN2 = 0.6931471805599453
Q_SCALE = A_HEAD_DIM ** -0.5 / LN2
VTE_ROWS = A_HEAD_DIM + 16

ADAM_LR, ADAM_B1, ADAM_B2, ADAM_EPS, ADAM_WD, ADAM_STEP = 0.001, 0.9, 0.999, 1e-08, 0.01, 10

V7X_VMEM_BYTES = 64 * 2**20
VMEM_LIMIT = V7X_VMEM_BYTES - 4 * 2**20
LANES = 128
MESH = pl.DeviceIdType.MESH
N_CHIPS = 4


def _cp(*sem):
    return pltpu.CompilerParams(dimension_semantics=sem if sem else None, vmem_limit_bytes=VMEM_LIMIT)


def _dot(a, b):
    return jnp.dot(a, b, preferred_element_type=F32)


def _dot_nt(a, b):
    return lax.dot_general(a, b, (((1,), (1,)), ((), ())), preferred_element_type=F32)


def _dot_tn(a, b):
    return lax.dot_general(a, b, (((0,), (0,)), ((), ())), preferred_element_type=F32)


def _dot_hi(a, b):
    return jnp.dot(a, b, preferred_element_type=F32, precision=lax.Precision.HIGHEST)


def _group_sum(a, ones):
    hi = a.astype(BF16)
    lo = (a - hi.astype(F32)).astype(BF16)
    return _dot(hi, ones) + _dot(lo, ones)


def _dot_nt_hi(a, b):
    return lax.dot_general(a, b, (((1,), (1,)), ((), ())), preferred_element_type=F32, precision=lax.Precision.HIGHEST)


def _sig(z):
    return 1.0 / (1.0 + jnp.exp(-z))


def _full(shape, once=False):
    nd = len(shape)
    return pl.BlockSpec(shape, lambda *_: (0,) * nd, pipeline_mode=pl.Buffered(1) if once else None)


def _rows(tm, width):
    return pl.BlockSpec((tm, width), lambda i: (i, 0))


def _sds(shape, dtype):
    return jax.ShapeDtypeStruct(shape, dtype)


def rms_fwd(x, g, gather=()):
    s, d = x.shape
    tm = min(s, 512)
    nt = s // tm
    ng = len(gather)

    def body(x_ref, g_ref, *rest):
        g_in, h_ref, g_out = rest[:ng], rest[ng], rest[ng + 1:2 * ng + 1]
        if ng:
            start, forward, finish = gather_stages([a.shape for a in gather], g_in, g_out, *rest[2 * ng + 1:])
            pl.when(pl.program_id(0) == 0)(start)
        xf = x_ref[...]
        r = lax.rsqrt(jnp.mean(xf * xf, axis=-1, keepdims=True) + EPS)
        h_ref[...] = ((xf * r) * g_ref[...]).astype(BF16)
        if ng:
            @pl.when(pl.program_id(0) == nt - 1)
            def _():
                forward()
                finish()

    out = pl.pallas_call(
        body, out_shape=(_sds((s, d), BF16),) + tuple(_sds((N_CHIPS,) + a.shape, a.dtype) for a in gather), grid=(nt,),
        in_specs=[_rows(tm, d), _full((1, d))] + [_ANY] * ng, out_specs=(_rows(tm, d),) + (_ANY,) * ng,
        scratch_shapes=gather_sems(ng) if ng else [],
        compiler_params=_cp("arbitrary"), name="rms_fwd_gather" if ng else "rms_fwd")(x, g, *gather)
    return out[0], list(out[1:])


def proj_fwd(h, w_t, gather=()):
    s, d = h.shape
    n = w_t.shape[0]
    tm = min(s, 1024)
    tn = 2304
    nj, ni = n // tn, s // tm
    ng = len(gather)

    def body(h_ref, w_ref, *rest):
        g_in, o_ref, g_out = rest[:ng], rest[ng], rest[ng + 1:2 * ng + 1]
        step = pl.program_id(0) * ni + pl.program_id(1)
        if ng:
            start, forward, finish = gather_stages([a.shape for a in gather], g_in, g_out, *rest[2 * ng + 1:])
            pl.when(step == 0)(start)
            pl.when(step == (3 * nj * ni) // 4)(forward)
        o_ref[...] = _dot_nt(h_ref[...], w_ref[...]).astype(BF16)
        if ng:
            pl.when(step == nj * ni - 1)(finish)

    out = pl.pallas_call(
        body, out_shape=(_sds((s, n), BF16),) + tuple(_sds((N_CHIPS,) + a.shape, a.dtype) for a in gather), grid=(nj, ni),
        in_specs=[pl.BlockSpec((tm, d), lambda j, i: (i, 0)), pl.BlockSpec((tn, d), lambda j, i: (j, 0))] + [_ANY] * ng,
        out_specs=(pl.BlockSpec((tm, tn), lambda j, i: (i, j)),) + (_ANY,) * ng,
        scratch_shapes=gather_sems(ng) if ng else [],
        compiler_params=_cp("arbitrary", "arbitrary") if ng else _cp("parallel", "parallel"),
        name="proj_fwd_gather" if ng else "proj_fwd")(h, w_t, *gather)
    return out[0], list(out[1:])


def rope_tables(seq):
    n_freq = A_HEAD_DIM // 4
    d = jnp.arange(LANES) % A_HEAD_DIM
    seg, half, freq = d // (2 * n_freq), (d % (2 * n_freq)) // n_freq, d % n_freq
    inv = ROPE_THETA ** (-freq.astype(F32) / n_freq)
    t = jnp.arange(seq)
    pos = jnp.where(seg[None, :] == 0, (t // GRID_W)[:, None], (t % GRID_W)[:, None]).astype(F32)
    ang = pos * inv[None, :]
    cos, sin = jnp.cos(ang), jnp.sin(ang)
    return cos, jnp.where(half[None, :] == 1, sin, 0.0), jnp.where(half[None, :] == 0, -sin, 0.0)


def _group_ones(width, group):
    i = jnp.arange(width)
    return (i[:, None] // group == i[None, :] // group).astype(F32)


def _rope(xn, c, sa, sb):
    w = xn.shape[1]
    return xn * c + pltpu.roll(xn, 16, 1) * sa + pltpu.roll(xn, w - 16, 1) * sb


def _rope_t(dy, c, sa, sb):
    w = dy.shape[1]
    return dy * c + pltpu.roll(dy * sa, w - 16, 1) + pltpu.roll(dy * sb, 16, 1)


def _tile4(t):
    return jnp.concatenate([t, t, t, t], axis=1)


def qk_prep(proj, tabs, qg, kg, gq, gk):
    s = proj.shape[0]
    tm = min(s, 1024)
    c, sa, sb = tabs

    def body(p_ref, c_ref, sa_ref, sb_ref, qg_ref, kg_ref, gq_ref, gk_ref, qt_ref, kr_ref, vb_ref, kt_ref, v0_ref, v1_ref):
        xq = p_ref[:, O_QA:O_QA + A_WIDTH].astype(F32)
        xk = p_ref[:, O_KA:O_KA + A_KV_WIDTH].astype(F32)
        xv = p_ref[:, O_VA:O_VA + A_KV_WIDTH].astype(F32)
        cc, ssa, ssb = c_ref[...], sa_ref[...], sb_ref[...]
        msq = _group_sum(xq * xq, gq_ref[...]) * (1.0 / A_HEAD_DIM)
        qn = (xq * lax.rsqrt(msq + EPS)) * qg_ref[...]
        qr = _rope(qn, _tile4(cc), _tile4(ssa), _tile4(ssb)) * Q_SCALE
        qt_ref[...] = qr.T.astype(BF16)
        msk = _group_sum(xk * xk, gk_ref[...]) * (1.0 / A_HEAD_DIM)
        kn = (xk * lax.rsqrt(msk + EPS)) * kg_ref[...]
        kr = _rope(kn, cc, ssa, ssb)
        kr_ref[...] = kr.astype(BF16)
        vb_ref[...] = xv.astype(BF16)
        kt_ref[...] = kr.T.astype(BF16)
        vt = xv.T.astype(BF16)
        one = jnp.ones((VTE_ROWS - A_HEAD_DIM, tm), BF16)
        v0_ref[...] = jnp.concatenate([vt[:A_HEAD_DIM], one], axis=0)
        v1_ref[...] = jnp.concatenate([vt[A_HEAD_DIM:], one], axis=0)

    tab = _rows(tm, LANES)
    colb = lambda w: pl.BlockSpec((w, tm), lambda i: (0, i))
    return pl.pallas_call(
        body,
        out_shape=(_sds((A_WIDTH, s), BF16), _sds((s, A_KV_WIDTH), BF16), _sds((s, A_KV_WIDTH), BF16),
                   _sds((A_KV_WIDTH, s), BF16), _sds((VTE_ROWS, s), BF16), _sds((VTE_ROWS, s), BF16)),
        grid=(s // tm,),
        in_specs=[_rows(tm, PBLK), tab, tab, tab, _full((1, A_WIDTH)), _full((1, A_KV_WIDTH)),
                  _full((A_WIDTH, A_WIDTH)), _full((A_KV_WIDTH, A_KV_WIDTH))],
        out_specs=(colb(A_WIDTH), _rows(tm, A_KV_WIDTH), _rows(tm, A_KV_WIDTH), colb(A_KV_WIDTH), colb(VTE_ROWS), colb(VTE_ROWS)),
        compiler_params=_cp("parallel"), name="qk_prep")(proj, c, sa, sb, qg, kg, gq, gk)


def _pad_head(q_h, kv):
    z = jnp.zeros_like(q_h)
    return jnp.concatenate([q_h, z], axis=0) if kv == 0 else jnp.concatenate([z, q_h], axis=0)


def attn_fwd(q_t, kr, vte0, vte1, gather=()):
    s = kr.shape[0]
    tq = min(s, 512)
    kc = min(s, 256)
    nkc = s // kc
    nq = s // tq
    grp = A_HEADS // A_KV_HEADS
    ng = len(gather)

    def body(qt_ref, kr_ref, v0_ref, v1_ref, *rest):
        g_in, (o_ref, lse_ref), g_out = rest[:ng], rest[ng:ng + 2], rest[ng + 2:2 * ng + 2]
        qp_ref, m_ref, acc_ref = rest[2 * ng + 2:2 * ng + 5]
        if ng:
            start, forward, finish = gather_stages([g.shape for g in gather], g_in, g_out, *rest[2 * ng + 5:])
            pl.when(pl.program_id(0) == 0)(start)
            pl.when(pl.program_id(0) == (3 * nq) // 4)(forward)

        for h in range(A_HEADS):
            qp_ref[h] = _pad_head(qt_ref[A_HEAD_DIM * h:A_HEAD_DIM * (h + 1), :], h // grp)
        m_ref[...] = jnp.full(m_ref.shape, -1e30, F32)
        acc_ref[...] = jnp.zeros_like(acc_ref)

        def step(ci, carry):
            ks = pl.ds(pl.multiple_of(ci * kc, kc), kc)
            kblk = kr_ref[ks, :]
            vts = (v0_ref[:, ks], v1_ref[:, ks])
            scs = [_dot(kblk, qp_ref[h]) for h in range(A_HEADS)]
            for h in range(A_HEADS):
                sc = scs[h]
                m_prev = m_ref[h:h + 1, :]
                m_new = jnp.maximum(m_prev, jnp.max(sc, axis=0, keepdims=True))
                p = jnp.exp2(sc - m_new)
                acc_ref[h] = acc_ref[h] * jnp.exp2(m_prev - m_new) + _dot(vts[h // grp], p.astype(BF16))
                m_ref[h:h + 1, :] = m_new
            return carry

        lax.fori_loop(0, nkc, step, 0)
        outs, lses = [], []
        for h in range(A_HEADS):
            acc = acc_ref[h]
            l = acc[A_HEAD_DIM:A_HEAD_DIM + 1, :]
            outs.append(acc[:A_HEAD_DIM, :] / l)
            lses.append(m_ref[h:h + 1, :] + jnp.log2(l))
        o_ref[...] = jnp.concatenate(outs, axis=0).T
        lse_ref[...] = jnp.concatenate(lses, axis=0)
        if ng:
            pl.when(pl.program_id(0) == nq - 1)(finish)

    out = pl.pallas_call(
        body,
        out_shape=(_sds((s, A_WIDTH), F32), _sds((A_HEADS, s), F32)) + tuple(_sds((N_CHIPS,) + g.shape, g.dtype) for g in gather),
        grid=(nq,),
        in_specs=[pl.BlockSpec((A_WIDTH, tq), lambda i: (0, i)), _full((s, A_KV_WIDTH)), _full((VTE_ROWS, s)),
                  _full((VTE_ROWS, s))] + [_ANY] * ng,
        out_specs=(_rows(tq, A_WIDTH), pl.BlockSpec((A_HEADS, tq), lambda i: (0, i))) + (_ANY,) * ng,
        scratch_shapes=[pltpu.VMEM((A_HEADS, A_KV_WIDTH, tq), BF16), pltpu.VMEM((A_HEADS, tq), F32),
                        pltpu.VMEM((A_HEADS, VTE_ROWS, tq), F32)] + (gather_sems(ng) if ng else []),
        compiler_params=_cp("arbitrary"), name="attn_fwd_gather" if ng else "attn_fwd")(q_t, kr, vte0, vte1, *gather)
    return out[0], out[1], list(out[2:])


def memkv_fwd(mem, g, w_kv):
    m, d = mem.shape

    def body(mem_ref, g_ref, w_ref, mn_ref, kv_ref):
        mf = mem_ref[...]
        r = lax.rsqrt(jnp.mean(mf * mf, axis=-1, keepdims=True) + EPS)
        mn = ((mf * r) * g_ref[...]).astype(BF16)
        mn_ref[...] = mn
        kv_ref[...] = _dot(mn, w_ref[...]).astype(BF16)

    return pl.pallas_call(
        body, out_shape=(_sds((m, d), BF16), _sds((m, 2 * M_WIDTH), BF16)),
        compiler_params=_cp(), name="memkv_fwd")(mem, g, w_kv)


def _layer_norm_stats(v):
    mu = jnp.mean(v, axis=-1, keepdims=True)
    xc = v - mu
    rstd = lax.rsqrt(jnp.mean(xc * xc, axis=-1, keepdims=True) + EPS)
    return xc * rstd, rstd


def _spatial_mix(vlb, ws_ref, bsb_ref, tm):
    rows = []
    for ci in range(tm // CHUNK):
        cols = []
        for g in range(B_GROUPS):
            blk = vlb[ci * CHUNK:(ci + 1) * CHUNK, g * B_GROUP_DIM:(g + 1) * B_GROUP_DIM]
            cols.append(_dot(ws_ref[g], blk) + bsb_ref[g])
        rows.append(jnp.concatenate(cols, axis=1))
    return jnp.concatenate(rows, axis=0)


def _mem_attn(qm, kv_ref):
    out = []
    for h in range(M_HEADS):
        qh = qm[:, h * M_HEAD_DIM:(h + 1) * M_HEAD_DIM].astype(BF16)
        kh = kv_ref[:, h * M_HEAD_DIM:(h + 1) * M_HEAD_DIM]
        vh = kv_ref[:, M_WIDTH + h * M_HEAD_DIM:M_WIDTH + (h + 1) * M_HEAD_DIM]
        sc = _dot_nt(qh, kh) * (M_HEAD_DIM ** -0.5)
        e = jnp.exp(sc - jnp.max(sc, axis=-1, keepdims=True))
        p = e / jnp.sum(e, axis=-1, keepdims=True)
        out.append((p, _dot(p.astype(BF16), vh)))
    return out


def branch_fwd(x, proj, o_a, kv, ws, bsb, ln_g, ln_b, w_br, w_out, next_g):
    s, d = x.shape
    tm = min(s, 512)

    def body(x_ref, p_ref, oa_ref, kv_ref, ws_ref, bsb_ref, lg_ref, lb_ref, wbr_ref, wo_ref, ng_ref,
             xn_ref, y_ref, up_ref, mg_ref, hn_ref):
        seg = lambda o, w: p_ref[:, o:o + w].astype(F32)
        z_a, u_b, v_b, z_b = seg(O_ZA, A_WIDTH), seg(O_UB, B_WIDTH), seg(O_VB, B_WIDTH), seg(O_ZB, B_WIDTH)
        q_m, z_m = seg(O_QM, M_WIDTH), seg(O_ZM, M_WIDTH)
        xhat, _ = _layer_norm_stats(v_b)
        vln = xhat * lg_ref[...] + lb_ref[...]
        mixed = _spatial_mix(vln.astype(BF16), ws_ref, bsb_ref, tm)
        y_b = (u_b * mixed) * (z_b * _sig(z_b))
        o_m = jnp.concatenate([o for _, o in _mem_attn(q_m, kv_ref)], axis=1)
        y_a = oa_ref[...] * (z_a * _sig(z_a))
        y_m = o_m * (z_m * _sig(z_m))
        merged = None
        for n, yy in enumerate((y_a, y_b, y_m)):
            yb = yy.astype(BF16)
            y_ref[n] = yb
            up = jnp.concatenate([_dot(yb, wbr_ref[c, n]) for c in range(N_CHIPS)], axis=1)
            up_ref[n] = up.astype(BF16)
            t = _sig(seg(O_LG + n * d, d)) * up
            merged = t if merged is None else merged + t
        mb = merged.astype(BF16)
        mg_ref[...] = mb
        xn = x_ref[...] + _dot(mb, wo_ref[...])
        xn_ref[...] = xn
        r = lax.rsqrt(jnp.mean(xn * xn, axis=-1, keepdims=True) + EPS)
        hn_ref[...] = ((xn * r) * ng_ref[...]).astype(BF16)

    return pl.pallas_call(
        body,
        out_shape=(_sds((s, d), F32), _sds((N_BRANCH, s, A_WIDTH), BF16), _sds((N_BRANCH, s, d), BF16), _sds((s, d), BF16),
                   _sds((s, d), BF16)),
        grid=(s // tm,),
        in_specs=[_rows(tm, d), _rows(tm, IN_WIDTH), _rows(tm, A_WIDTH), _full(kv.shape), _full(ws.shape), _full(bsb.shape),
                  _full((1, B_WIDTH)), _full((1, B_WIDTH)), _full(w_br.shape), _full(w_out.shape), _full((1, d))],
        out_specs=(_rows(tm, d), pl.BlockSpec((N_BRANCH, tm, A_WIDTH), lambda i: (0, i, 0)),
                   pl.BlockSpec((N_BRANCH, tm, d), lambda i: (0, i, 0)), _rows(tm, d), _rows(tm, d)),
        compiler_params=_cp("parallel"), name="branch_fwd")(x, proj, o_a, kv, ws, bsb, ln_g, ln_b, w_br, w_out, next_g)


def final_loss(x, fg, tgt):
    s, d = x.shape
    tm = min(s, 512)

    def body(x_ref, g_ref, t_ref, ls_ref, dx_ref, gg_ref):
        @pl.when(pl.program_id(0) == 0)
        def _():
            ls_ref[...] = jnp.zeros_like(ls_ref)
            gg_ref[...] = jnp.zeros_like(gg_ref)

        xf = x_ref[...]
        g = g_ref[...]
        r = lax.rsqrt(jnp.mean(xf * xf, axis=-1, keepdims=True) + EPS)
        xh = xf * r
        e = xh * g - t_ref[...]
        sq = jnp.sum(jnp.sum(e * e, axis=0, keepdims=True), axis=1, keepdims=True)
        ls_ref[...] += jnp.broadcast_to(sq, ls_ref.shape)
        dy = e * (1.0 / d)
        gg_ref[...] += jnp.sum(dy * xh, axis=0, keepdims=True)
        gy = dy * g
        dx_ref[...] = r * (gy - xh * jnp.mean(gy * xh, axis=-1, keepdims=True))

    return pl.pallas_call(
        body, out_shape=(_sds((1, LANES), F32), _sds((s, d), F32), _sds((1, d), F32)), grid=(s // tm,),
        in_specs=[_rows(tm, d), _full((1, d)), _rows(tm, d)],
        out_specs=(_full((1, LANES)), _rows(tm, d), _full((1, d))),
        compiler_params=_cp("arbitrary"), name="final_loss")(x, fg, tgt)


def _pblocks(tm, first, count):
    return [pl.BlockSpec((tm, PBLK), functools.partial(lambda i, b: (i, b), b=first + k)) for k in range(count)]


def merge_bwd(dx, proj, y, up, merged, w_br, w_out):
    s, d = dx.shape
    tm = min(s, 512)
    nlg = LG_W // PBLK
    cw = d // N_CHIPS

    def body(dx_ref, l0, l1, l2, l3, y_ref, up_ref, mg_ref, wbr_ref, wo_ref, dy_ref, dlg_ref, gwo_ref, gwb_ref, gwo16_ref, gwb16_ref):
        @pl.when(pl.program_id(0) == 0)
        def _():
            gwo_ref[...] = jnp.zeros_like(gwo_ref)
            gwb_ref[...] = jnp.zeros_like(gwb_ref)

        dxb = dx_ref[...].astype(BF16)
        dmg = _dot_nt(dxb, wo_ref[...])
        gwo_ref[...] += _dot_tn(mg_ref[...], dxb)
        lg = jnp.concatenate([l0[...], l1[...], l2[...], l3[...]], axis=1).astype(F32)
        for n in range(N_BRANCH):
            g = _sig(lg[:, n * d:(n + 1) * d])
            dup = dmg * g
            dlg_ref[:, n * d:(n + 1) * d] = ((dup * up_ref[n].astype(F32)) * (1.0 - g)).astype(BF16)
            dupb = dup.astype(BF16)
            dyn = None
            for c in range(N_CHIPS):
                blk = dupb[:, c * cw:(c + 1) * cw]
                gwb_ref[c, n] += _dot_tn(y_ref[n], blk)
                t = _dot_nt(blk, wbr_ref[c, n])
                dyn = t if dyn is None else dyn + t
            dy_ref[n] = dyn.astype(BF16)

        @pl.when(pl.program_id(0) == pl.num_programs(0) - 1)
        def _():
            gwo16_ref[...] = gwo_ref[...].astype(BF16)
            gwb16_ref[...] = gwb_ref[...].astype(BF16)

    return pl.pallas_call(
        body,
        out_shape=(_sds((N_BRANCH, s, A_WIDTH), BF16), _sds((s, LG_W), BF16), _sds((d, d), F32), _sds(w_br.shape, F32),
                   _sds((d, d), BF16), _sds(w_br.shape, BF16)),
        grid=(s // tm,),
        in_specs=[_rows(tm, d)] + _pblocks(tm, O_LG // PBLK, nlg) + [
            pl.BlockSpec((N_BRANCH, tm, A_WIDTH), lambda i: (0, i, 0)), pl.BlockSpec((N_BRANCH, tm, d), lambda i: (0, i, 0)),
            _rows(tm, d), _full(w_br.shape, once=True), _full(w_out.shape, once=True)],
        out_specs=(pl.BlockSpec((N_BRANCH, tm, A_WIDTH), lambda i: (0, i, 0)), _rows(tm, LG_W), _full((d, d)), _full(w_br.shape),
                   _full((d, d)), _full(w_br.shape)),
        compiler_params=_cp("arbitrary"), name="merge_bwd")(dx, proj, proj, proj, proj, y, up, merged, w_br, w_out)


def _dsilu(z, sg):
    return sg * (1.0 + z * (1.0 - sg))


def branch_bwd(dy, proj, o_a, kv, ws, ws_t, bsb, ln_g, ln_b, head_sel):
    s = proj.shape[0]
    tm = min(s, 512)
    nmid = MID_W // PBLK

    def body(dy_ref, m0, m1, m2, m3, oa_ref, kv_ref, ws_ref, wst_ref, bsb_ref, lg_ref, lb_ref, sel_ref,
             dmid_ref, dot_ref, dl_ref, gws_ref, gbs_ref, glg_ref, glb_ref, dkv_ref):
        @pl.when(pl.program_id(0) == 0)
        def _():
            for r in (gws_ref, gbs_ref, glg_ref, glb_ref, dkv_ref):
                r[...] = jnp.zeros_like(r)

        mid = jnp.concatenate([m0[...], m1[...], m2[...], m3[...]], axis=1).astype(F32)
        seg = lambda o, w: mid[:, o - O_ZA:o - O_ZA + w]
        z_a, u_b, v_b, z_b = seg(O_ZA, A_WIDTH), seg(O_UB, B_WIDTH), seg(O_VB, B_WIDTH), seg(O_ZB, B_WIDTH)
        q_m, z_m = seg(O_QM, M_WIDTH), seg(O_ZM, M_WIDTH)

        def put(o, v):
            dmid_ref[:, o - O_ZA:o - O_ZA + v.shape[1]] = v.astype(BF16)

        dy_a, dy_b, dy_m = dy_ref[0].astype(F32), dy_ref[1].astype(F32), dy_ref[2].astype(F32)

        o_a_ = oa_ref[...]
        sg = _sig(z_a)
        do_a = dy_a * (z_a * sg)
        put(O_ZA, (dy_a * o_a_) * _dsilu(z_a, sg))
        do_l = do_a * LN2
        dot_ref[...] = do_l.T.astype(BF16)
        dl_ref[...] = _dot_nt_hi(sel_ref[...], do_l * o_a_)

        xhat, rstd = _layer_norm_stats(v_b)
        lng = lg_ref[...]
        vln = xhat * lng + lb_ref[...]
        vlb = vln.astype(BF16)
        mixed = _spatial_mix(vlb, ws_ref, bsb_ref, tm)
        sg = _sig(z_b)
        sl = z_b * sg
        put(O_UB, (dy_b * mixed) * sl)
        put(O_ZB, ((dy_b * u_b) * mixed) * _dsilu(z_b, sg))
        dmix = (dy_b * u_b) * sl
        dmb = dmix.astype(BF16)
        rows = []
        for ci in range(tm // CHUNK):
            cols = []
            for g in range(B_GROUPS):
                rs, cs = slice(ci * CHUNK, (ci + 1) * CHUNK), slice(g * B_GROUP_DIM, (g + 1) * B_GROUP_DIM)
                gws_ref[g] += _dot_nt(dmb[rs, cs], vlb[rs, cs])
                gbs_ref[g] += jnp.broadcast_to(jnp.sum(dmix[rs, cs], axis=1, keepdims=True), (CHUNK, B_GROUP_DIM))
                cols.append(_dot(wst_ref[g], dmb[rs, cs]))
            rows.append(jnp.concatenate(cols, axis=1))
        dvln = jnp.concatenate(rows, axis=0)
        glg_ref[...] += jnp.sum(dvln * xhat, axis=0, keepdims=True)
        glb_ref[...] += jnp.sum(dvln, axis=0, keepdims=True)
        gy = dvln * lng
        put(O_VB, rstd * ((gy - jnp.mean(gy, axis=-1, keepdims=True)) - xhat * jnp.mean(gy * xhat, axis=-1, keepdims=True)))

        sg = _sig(z_m)
        sl = z_m * sg
        heads = _mem_attn(q_m, kv_ref)
        o_m = jnp.concatenate([o for _, o in heads], axis=1)
        put(O_ZM, (dy_m * o_m) * _dsilu(z_m, sg))
        do_m = dy_m * sl
        dqs = []
        for h, (p, o_h) in enumerate(heads):
            hs = slice(h * M_HEAD_DIM, (h + 1) * M_HEAD_DIM)
            vs = slice(M_WIDTH + h * M_HEAD_DIM, M_WIDTH + (h + 1) * M_HEAD_DIM)
            do_h = do_m[:, hs]
            dob = do_h.astype(BF16)
            dp = _dot_nt(dob, kv_ref[:, vs])
            dsc = (p * (dp - jnp.sum(do_h * o_h, axis=-1, keepdims=True))) * (M_HEAD_DIM ** -0.5)
            dsb = dsc.astype(BF16)
            dqs.append(_dot(dsb, kv_ref[:, hs]))
            dkv_ref[:, hs] += _dot_tn(dsb, q_m[:, hs].astype(BF16))
            dkv_ref[:, vs] += _dot_tn(p.astype(BF16), dob)
        put(O_QM, jnp.concatenate(dqs, axis=1))

    return pl.pallas_call(
        body,
        out_shape=(_sds((s, MID_W), BF16), _sds((A_WIDTH, s), BF16), _sds((A_HEADS, s), F32), _sds(ws.shape, F32),
                   _sds(ws.shape, F32), _sds((1, B_WIDTH), F32), _sds((1, B_WIDTH), F32), _sds(kv.shape, F32)),
        grid=(s // tm,),
        in_specs=[pl.BlockSpec((N_BRANCH, tm, A_WIDTH), lambda i: (0, i, 0))] + _pblocks(tm, O_ZA // PBLK, nmid) + [
            _rows(tm, A_WIDTH), _full(kv.shape), _full(ws.shape), _full(ws.shape), _full(bsb.shape),
            _full((1, B_WIDTH)), _full((1, B_WIDTH)), _full(head_sel.shape)],
        out_specs=(_rows(tm, MID_W), pl.BlockSpec((A_WIDTH, tm), lambda i: (0, i)), pl.BlockSpec((A_HEADS, tm), lambda i: (0, i)),
                   _full(ws.shape), _full(ws.shape), _full((1, B_WIDTH)), _full((1, B_WIDTH)), _full(kv.shape)),
        compiler_params=_cp("arbitrary"), name="branch_bwd")(dy, proj, proj, proj, proj, o_a, kv, ws, ws_t, bsb, ln_g, ln_b, head_sel)


def attn_bwd(q_t, do_t, kr, kr_t, vb, lse, delta, scatter=()):
    s = kr.shape[0]
    tq = min(s, 256)
    kc = min(s, 512)
    nkc = s // kc
    nq = s // tq
    grp = A_HEADS // A_KV_HEADS
    ns = len(scatter)
    na = ns // 2

    def body(qt_ref, dot_ref, kr_ref, krt_ref, vb_ref, lse_ref, dl_ref, *rest):
        s_in, (dqt_ref, dk_ref, dv_ref), s_out = rest[:ns], rest[ns:ns + 3], rest[ns + 3:2 * ns + 3]
        qp_ref, dop_ref, dq_ref = rest[2 * ns + 3:2 * ns + 6]
        if ns:
            start, finish = scatter_stages([g.shape[1:] for g in scatter[:na]], s_in[:na], s_in[na:], s_out[:na], s_out[na:],
                                           *rest[2 * ns + 6:])
            pl.when(pl.program_id(0) == 0)(start)

        @pl.when(pl.program_id(0) == 0)
        def _():
            dk_ref[...] = jnp.zeros_like(dk_ref)
            dv_ref[...] = jnp.zeros_like(dv_ref)

        for h in range(A_HEADS):
            hs = slice(A_HEAD_DIM * h, A_HEAD_DIM * (h + 1))
            qp_ref[h] = _pad_head(qt_ref[hs, :], h // grp)
            dop_ref[h] = _pad_head(dot_ref[hs, :], h // grp)
        dq_ref[...] = jnp.zeros_like(dq_ref)

        def step(ci, carry):
            ks = pl.ds(pl.multiple_of(ci * kc, kc), kc)
            kblk, vblk, ktb = kr_ref[ks, :], vb_ref[ks, :], krt_ref[:, ks]
            dv_acc = jnp.zeros((kc, A_KV_WIDTH), F32)
            dk_acc = jnp.zeros((kc, A_KV_WIDTH), F32)
            scs = [_dot(kblk, qp_ref[h]) for h in range(A_HEADS)]
            dps = [_dot(vblk, dop_ref[h]) for h in range(A_HEADS)]
            for h in range(A_HEADS):
                qpad, dopad = qp_ref[h], dop_ref[h]
                p = jnp.exp2(scs[h] - lse_ref[h:h + 1, :])
                dsb = (p * (dps[h] - dl_ref[h:h + 1, :])).astype(BF16)
                dv_acc = dv_acc + _dot_nt(p.astype(BF16), dopad)
                dk_acc = dk_acc + _dot_nt(dsb, qpad)
                dq_ref[h] += _dot(ktb, dsb)
            dv_ref[ks, :] += dv_acc
            dk_ref[ks, :] += dk_acc
            return carry

        lax.fori_loop(0, nkc, step, 0)
        dqt_ref[...] = jnp.concatenate(
            [dq_ref[h][A_HEAD_DIM * (h // grp):A_HEAD_DIM * (h // grp + 1), :] for h in range(A_HEADS)], axis=0)
        if ns:
            pl.when(pl.program_id(0) == nq - 1)(finish)

    colq = pl.BlockSpec((A_WIDTH, tq), lambda i: (0, i))
    colh = pl.BlockSpec((A_HEADS, tq), lambda i: (0, i))
    out = pl.pallas_call(
        body,
        out_shape=(_sds((A_WIDTH, s), F32), _sds((s, A_KV_WIDTH), F32), _sds((s, A_KV_WIDTH), F32)) + scatter_out_shapes(scatter[:na]),
        grid=(nq,),
        in_specs=[colq, colq, _full((s, A_KV_WIDTH)), _full((A_KV_WIDTH, s)), _full((s, A_KV_WIDTH)), colh, colh] + [_ANY] * ns,
        out_specs=(colq, _full((s, A_KV_WIDTH)), _full((s, A_KV_WIDTH))) + (_ANY,) * ns,
        scratch_shapes=[pltpu.VMEM((A_HEADS, A_KV_WIDTH, tq), BF16), pltpu.VMEM((A_HEADS, A_KV_WIDTH, tq), BF16),
                        pltpu.VMEM((A_HEADS, A_KV_WIDTH, tq), F32)] + (scatter_sems(na) if ns else []),
        compiler_params=_cp("arbitrary"), name="attn_bwd_scatter" if ns else "attn_bwd")(
            q_t, do_t, kr, kr_t, vb, lse, delta, *scatter)
    return out[0], out[1], out[2], list(out[3:3 + na]), list(out[3 + na:])


def qk_prep_bwd(proj, dq_t, dkr, dvb, tabs, qg, kg, gq, gk, fold_q, fold_k):
    s = proj.shape[0]
    tm = min(s, 1024)
    c, sa, sb = tabs

    def head_norm_bwd(x, dn, gain, gones, fold):
        ms = _group_sum(x * x, gones) * (1.0 / A_HEAD_DIM)
        r = lax.rsqrt(ms + EPS)
        xh = x * r
        gg = _dot_hi(jnp.sum(dn * xh, axis=0, keepdims=True), fold)
        u = dn * gain
        mean_u = _group_sum(u * xh, gones) * (1.0 / A_HEAD_DIM)
        return r * (u - xh * mean_u), gg

    def body(p_ref, dqt_ref, dk_ref, dv_ref, c_ref, sa_ref, sb_ref, qg_ref, kg_ref, gq_ref, gk_ref, fq_ref, fk_ref,
             dqkv_ref, gqg_ref, gkg_ref):
        @pl.when(pl.program_id(0) == 0)
        def _():
            gqg_ref[...] = jnp.zeros_like(gqg_ref)
            gkg_ref[...] = jnp.zeros_like(gkg_ref)

        cc, ssa, ssb = c_ref[...], sa_ref[...], sb_ref[...]
        dqr = dqt_ref[...].T * Q_SCALE
        dqn = _rope_t(dqr, _tile4(cc), _tile4(ssa), _tile4(ssb))
        dxq, gq_ = head_norm_bwd(p_ref[:, O_QA:O_QA + A_WIDTH].astype(F32), dqn, qg_ref[...], gq_ref[...], fq_ref[...])
        dkn = _rope_t(dk_ref[...], cc, ssa, ssb)
        dxk, gk_ = head_norm_bwd(p_ref[:, O_KA:O_KA + A_KV_WIDTH].astype(F32), dkn, kg_ref[...], gk_ref[...], fk_ref[...])
        gqg_ref[...] += gq_
        gkg_ref[...] += gk_
        dqkv_ref[:, O_QA:O_QA + A_WIDTH] = dxq.astype(BF16)
        dqkv_ref[:, O_KA:O_KA + A_KV_WIDTH] = dxk.astype(BF16)
        dqkv_ref[:, O_VA:O_VA + A_KV_WIDTH] = (dv_ref[...] * (1.0 / LN2)).astype(BF16)

    tab = _rows(tm, LANES)
    return pl.pallas_call(
        body, out_shape=(_sds((s, PBLK), BF16), _sds((1, LANES), F32), _sds((1, LANES), F32)), grid=(s // tm,),
        in_specs=[_rows(tm, PBLK), pl.BlockSpec((A_WIDTH, tm), lambda i: (0, i)), _rows(tm, A_KV_WIDTH), _rows(tm, A_KV_WIDTH),
                  tab, tab, tab, _full((1, A_WIDTH)), _full((1, A_KV_WIDTH)), _full((A_WIDTH, A_WIDTH)),
                  _full((A_KV_WIDTH, A_KV_WIDTH)), _full((A_WIDTH, LANES)), _full((A_KV_WIDTH, LANES))],
        out_specs=(_rows(tm, PBLK), _full((1, LANES)), _full((1, LANES))),
        compiler_params=_cp("arbitrary"), name="qk_prep_bwd")(proj, dq_t, dkr, dvb, c, sa, sb, qg, kg, gq, gk, fold_q, fold_k)


def _pick_dproj(b, d0, d1, d2, use):
    first_lg = 1 + MID_W // PBLK

    @pl.when(b == 0)
    def _():
        use(d0[...])

    @pl.when(jnp.logical_and(b >= 1, b < first_lg))
    def _():
        use(d1[...])

    @pl.when(b >= first_lg)
    def _():
        use(d2[...])


def win_grad(d0, d1, d2, h):
    s, d = h.shape
    tk = min(s, 4096)
    nk = s // tk

    def body(d0_ref, d1_ref, d2_ref, h_ref, o_ref, o16_ref):
        @pl.when(pl.program_id(1) == 0)
        def _():
            o_ref[...] = jnp.zeros_like(o_ref)

        def use(blk):
            o_ref[...] += _dot_tn(blk, h_ref[...])

        _pick_dproj(pl.program_id(0), d0_ref, d1_ref, d2_ref, use)

        @pl.when(pl.program_id(1) == nk - 1)
        def _():
            o16_ref[...] = o_ref[...].astype(BF16)

    def spec(first, count):
        def imap(j, k):
            used = jnp.logical_and(j >= first, j < first + count)
            return (jnp.where(used, k, 0), jnp.clip(j - first, 0, count - 1))
        return pl.BlockSpec((tk, PBLK), imap)

    nm = MID_W // PBLK
    oblk = pl.BlockSpec((PBLK, d), lambda j, k: (j, 0))
    return pl.pallas_call(
        body, out_shape=(_sds((IN_WIDTH, d), F32), _sds((IN_WIDTH, d), BF16)), grid=(N_PBLK, nk),
        in_specs=[spec(0, 1), spec(1, nm), spec(1 + nm, LG_W // PBLK),
                  pl.BlockSpec((tk, d), lambda j, k: (k, 0), pipeline_mode=pl.Buffered(1) if nk == 1 else None)],
        out_specs=(oblk, oblk),
        compiler_params=_cp("parallel", "arbitrary"), name="win_grad")(d0, d1, d2, h)


def h_bwd(d0, d1, d2, w_t, x, dx_out, g, scatter=()):
    s, d = x.shape
    tm = min(s, 512)
    nt = s // tm
    ns = len(scatter)
    na = ns // 2

    def body(d0_ref, d1_ref, d2_ref, w_ref, x_ref, dxo_ref, g_ref, *rest):
        s_in, (dx_ref, gg_ref), s_out = rest[:ns], rest[ns:ns + 2], rest[ns + 2:2 * ns + 2]
        if ns:
            start, finish = scatter_stages([a.shape[1:] for a in scatter[:na]], s_in[:na], s_in[na:], s_out[:na], s_out[na:],
                                           *rest[2 * ns + 2:])
            pl.when(pl.program_id(0) == 0)(start)

        @pl.when(pl.program_id(0) == 0)
        def _():
            gg_ref[...] = jnp.zeros_like(gg_ref)

        dh = (_dot(d0_ref[...], w_ref[0:PBLK, :]) + _dot(d1_ref[...], w_ref[PBLK:PBLK + MID_W, :])
              + _dot(d2_ref[...], w_ref[PBLK + MID_W:, :]))
        xf = x_ref[...]
        r = lax.rsqrt(jnp.mean(xf * xf, axis=-1, keepdims=True) + EPS)
        xh = xf * r
        gg_ref[...] += jnp.sum(dh * xh, axis=0, keepdims=True)
        u = dh * g_ref[...]
        dx_ref[...] = dxo_ref[...] + r * (u - xh * jnp.mean(u * xh, axis=-1, keepdims=True))
        if ns:
            pl.when(pl.program_id(0) == nt - 1)(finish)

    rowb = _rows(tm, d)
    out = pl.pallas_call(
        body, out_shape=(_sds((s, d), F32), _sds((1, d), F32)) + scatter_out_shapes(scatter[:na]), grid=(nt,),
        in_specs=[_rows(tm, PBLK), _rows(tm, MID_W), _rows(tm, LG_W),
                  pl.BlockSpec(w_t.shape, lambda i: (0, 0), pipeline_mode=pl.Buffered(1)), rowb, rowb, _full((1, d))] + [_ANY] * ns,
        out_specs=(rowb, _full((1, d))) + (_ANY,) * ns,
        scratch_shapes=scatter_sems(na) if ns else [],
        compiler_params=_cp("arbitrary"), name="h_bwd_scatter" if ns else "h_bwd")(d0, d1, d2, w_t, x, dx_out, g, *scatter)
    return out[0], out[1], list(out[2:2 + na]), list(out[2 + na:])


def memkv_bwd(mem, g, mem_n, w_kv, dkv):
    m, d = mem.shape

    def body(mem_ref, g_ref, mn_ref, w_ref, dkv_ref, gw_ref, gw16_ref, gg_ref):
        dkb = dkv_ref[...].astype(BF16)
        gw = _dot_tn(mn_ref[...], dkb)
        gw_ref[...] = gw
        gw16_ref[...] = gw.astype(BF16)
        dmn = _dot_nt(dkb, w_ref[...])
        mf = mem_ref[...]
        r = lax.rsqrt(jnp.mean(mf * mf, axis=-1, keepdims=True) + EPS)
        gg_ref[...] = jnp.sum(dmn * (mf * r), axis=0, keepdims=True)

    return pl.pallas_call(
        body, out_shape=(_sds(w_kv.shape, F32), _sds(w_kv.shape, BF16), _sds((1, d), F32)),
        compiler_params=_cp(), name="memkv_bwd")(mem, g, mem_n, w_kv, dkv)


def _layer_consts(seq):
    i = jnp.arange(A_WIDTH)
    return dict(
        tabs=rope_tables(seq),
        gq=_group_ones(A_WIDTH, A_HEAD_DIM).astype(BF16), gk=_group_ones(A_KV_WIDTH, A_HEAD_DIM).astype(BF16),
        fold_q=(i[:, None] % A_HEAD_DIM == jnp.arange(LANES)[None, :]).astype(F32),
        fold_k=(i[:A_KV_WIDTH, None] % A_HEAD_DIM == jnp.arange(LANES)[None, :]).astype(F32),
        head_sel=(jnp.arange(A_HEADS)[:, None] == i[None, :] // A_HEAD_DIM).astype(F32),
    )


_BIG = ("win_t", "wkv", "wbr", "wout")


def _with_own_part(names, gathered, shards, chip, d):
    shape = dict(win_t=(IN_WIDTH, d), wkv=(d, 2 * M_WIDTH), wbr=(N_CHIPS, N_BRANCH, A_WIDTH, d // N_CHIPS), wout=(d, d))
    return {n: lax.dynamic_update_slice(g, sh[None], (chip, 0, 0)).reshape(shape[n]) for n, g, sh in zip(names, gathered, shards)}


def local_fwd_bwd(x, mem, tgt, small, big=None, shards=None, place=None):
    s, d = x.shape
    depth = small["norm_g"].shape[0]
    k = _layer_consts(s)
    row = lambda v: v.reshape(1, -1)
    dist = shards is not None
    if dist:
        big = [None] * depth
    saved = []
    for l in range(depth):
        ng = row(small["norm_g"][l])
        qg = row(jnp.tile(small["q_norm_g"][l], A_HEADS))
        kg = row(jnp.tile(small["k_norm_g"][l], A_KV_HEADS))
        ws = small["w_s"][l].astype(BF16)
        ws_t = jnp.swapaxes(small["w_s"][l], 1, 2).astype(BF16)
        bsb = jnp.broadcast_to(small["b_s"][l][:, :, None], (B_GROUPS, CHUNK, B_GROUP_DIM))
        lng, lnb = row(small["sg_ln_g"][l]), row(small["sg_ln_b"][l])
        mg = row(small["mem_norm_g"][l])
        if l == 0:
            first = tuple(shards[0][:1]) if dist else ()
            h, gathered = rms_fwd(x, ng, gather=first)
            if dist:
                big[0] = _with_own_part(_BIG[:1], gathered, first, place[0], d)
        else:
            h = h_next
        w = big[l]
        late = tuple(shards[0][1:]) if dist and l == 0 else ()
        proj, gathered = proj_fwd(h, w["win_t"], gather=late)
        if late:
            w.update(_with_own_part(_BIG[1:], gathered, late, place[0], d))
        q_t, kr, vb, kr_t, vte0, vte1 = qk_prep(proj, k["tabs"], qg, kg, k["gq"], k["gk"])
        nxt = tuple(shards[l + 1]) if dist and l + 1 < depth else ()
        o_a, lse, gathered = attn_fwd(q_t, kr, vte0, vte1, gather=nxt)
        if nxt:
            big[l + 1] = _with_own_part(_BIG, gathered, nxt, place[0], d)
        mem_n, kv = memkv_fwd(mem, mg, w["wkv"])
        next_g = row(small["norm_g"][l + 1]) if l + 1 < depth else row(small["final_g"])
        x_next, y, up, merged, h_next = branch_fwd(x, proj, o_a, kv, ws, bsb, lng, lnb, w["wbr"], w["wout"], next_g)
        saved.append(dict(x=x, ng=ng, qg=qg, kg=kg, ws=ws, ws_t=ws_t, bsb=bsb, lng=lng, lnb=lnb, mg=mg, h=h, proj=proj,
                          q_t=q_t, kr=kr, kr_t=kr_t, vb=vb, o_a=o_a, lse=lse, mem_n=mem_n, kv=kv, y=y, up=up, merged=merged))
        x = x_next

    sq, dx, g_final = final_loss(x, row(small["final_g"]), tgt)
    grads = {n: [None] * depth for n in ("norm_g", "q_norm_g", "k_norm_g", "sg_ln_g", "sg_ln_b", "w_s", "b_s", "mem_norm_g")}
    parts = lambda g: g.reshape(N_CHIPS, -1, g.shape[-1])
    reduced = [[None] * len(_BIG) for _ in range(depth)]

    def reduce_all(items, t_sib, t_rem):
        if items:
            for (ll, a, _, _), f in zip(items, reduce_rows(place, [i[2] for i in items], t_sib, t_rem)):
                reduced[ll][a] = f

    as_scatter = lambda items: tuple(i[2] for i in items) + tuple(i[3] for i in items)
    pending = []
    for l in reversed(range(depth)):
        sv, w = saved[l], big[l]
        dy, dlg, g_wout, g_wbr, g_wout16, g_wbr16 = merge_bwd(dx, sv["proj"], sv["y"], sv["up"], sv["merged"], w["wbr"], w["wout"])
        dmid, do_t, delta, g_ws, g_bs, g_lng, g_lnb, dkv = branch_bwd(
            dy, sv["proj"], sv["o_a"], sv["kv"], sv["ws"], sv["ws_t"], sv["bsb"], sv["lng"], sv["lnb"], k["head_sel"])
        g_wkv, g_wkv16, g_mg = memkv_bwd(mem, sv["mg"], sv["mem_n"], w["wkv"], dkv)
        if dist:
            pending += [(l, 1, parts(g_wkv), parts(g_wkv16)), (l, 2, parts(g_wbr), parts(g_wbr16)), (l, 3, parts(g_wout), parts(g_wout16))]
        dq_t, dkr, dvb, t_sib, t_rem = attn_bwd(sv["q_t"], do_t, sv["kr"], sv["kr_t"], sv["vb"], sv["lse"], delta,
                                                scatter=as_scatter(pending))
        reduce_all(pending, t_sib, t_rem)
        dqkv, g_qg, g_kg = qk_prep_bwd(sv["proj"], dq_t, dkr, dvb, k["tabs"], sv["qg"], sv["kg"], k["gq"], k["gk"],
                                       k["fold_q"], k["fold_k"])
        g_win, g_win16 = win_grad(dqkv, dmid, dlg, sv["h"])
        pending = [(l, 0, parts(g_win), parts(g_win16))] if dist else []
        last = as_scatter(pending) if l == 0 else ()
        dx, g_ng, t_sib, t_rem = h_bwd(dqkv, dmid, dlg, w["win_t"], sv["x"], dx, sv["ng"], scatter=last)
        if last:
            reduce_all(pending, t_sib, t_rem)
        grads["norm_g"][l] = g_ng[0]
        grads["q_norm_g"][l] = g_qg[0, :A_HEAD_DIM]
        grads["k_norm_g"][l] = g_kg[0, :A_HEAD_DIM]
        grads["sg_ln_g"][l] = g_lng[0]
        grads["sg_ln_b"][l] = g_lnb[0]
        grads["w_s"][l] = g_ws
        grads["b_s"][l] = g_bs[:, :, 0]
        grads["mem_norm_g"][l] = g_mg[0]
        if not dist:
            reduced[l] = dict(zip(_BIG, (parts(g_win), parts(g_wkv), parts(g_wbr), parts(g_wout))))
    grads = {n: jnp.stack(v) for n, v in grads.items()}
    grads["final_g"] = g_final[0]
    return sq[0, 0], dx, grads, reduced


def _row_block(rows, width, cap_bytes=2 * 2**20):
    best = None
    for br in range(8, rows + 1, 8):
        if rows % br == 0 and br * width * 4 <= cap_bytes:
            best = br
    return best if best is not None else rows


def adamw(w, gs, m, v):
    r, c = w.shape
    n = len(gs)
    rs = r // n
    br = _row_block(rs, c)
    nb = rs // br

    def body(w_ref, *refs):
        g_refs, (m_ref, v_ref, og_ref, d_ref, nm_ref, nv_ref) = refs[:n], refs[n:]

        def update(gg):
            mm = ADAM_B1 * m_ref[...] + (1.0 - ADAM_B1) * gg
            vv = ADAM_B2 * v_ref[...] + (1.0 - ADAM_B2) * (gg * gg)
            m_hat = mm / (1.0 - ADAM_B1 ** ADAM_STEP)
            v_hat = vv / (1.0 - ADAM_B2 ** ADAM_STEP)
            og_ref[...] = gg
            d_ref[...] = -ADAM_LR * (m_hat / (jnp.sqrt(v_hat) + ADAM_EPS) + ADAM_WD * w_ref[...])
            nm_ref[...] = mm
            nv_ref[...] = vv

        for k in range(n):
            pl.when(pl.program_id(0) == k)(functools.partial(lambda k: update(g_refs[k][...]), k))

    blk = pl.BlockSpec((br, c), lambda l, i: (l * nb + i, 0))
    g_specs = [pl.BlockSpec((br, c), functools.partial(lambda l, i, k: (jnp.where(l == k, i, 0), 0), k=k)) for k in range(n)]
    return pl.pallas_call(
        body, out_shape=(_sds((r, c), F32),) * 4, grid=(n, nb), in_specs=[blk] + g_specs + [blk, blk], out_specs=(blk,) * 4,
        compiler_params=_cp("arbitrary", "arbitrary"), name="adamw")(w, *gs, m, v)


N_REMOTE = 2 * (N_CHIPS - 1)


def reduce_rows(place, gs, t_sibs, t_rems):
    n = len(gs)
    nt = 2

    def body(place_ref, *refs):
        for a in range(n):
            g_ref, s_ref, t_ref, f_ref = refs[a], refs[n + a], refs[2 * n + a], refs[3 * n + a]
            acc = g_ref[...] + s_ref[...]
            for j in range(N_REMOTE):
                acc = acc + t_ref[j].astype(F32)
            f_ref[...] = acc

    tiles = [(g.shape[1] // 2 // nt, g.shape[2]) for g in gs]
    return pl.pallas_call(
        body, out_shape=tuple(_sds(g.shape[1:], F32) for g in gs),
        grid_spec=pltpu.PrefetchScalarGridSpec(
            num_scalar_prefetch=1, grid=(nt,),
            in_specs=[pl.BlockSpec((None, tr, c), lambda i, p: (p[0], p[1] * nt + i, 0)) for tr, c in tiles]
            + [pl.BlockSpec((tr, c), lambda i, p: (i, 0)) for tr, c in tiles]
            + [pl.BlockSpec((N_REMOTE, tr, c), lambda i, p: (0, i, 0)) for tr, c in tiles],
            out_specs=tuple(pl.BlockSpec((tr, c), lambda i, p: (p[1] * nt + i, 0)) for tr, c in tiles)),
        compiler_params=_cp("parallel"), name="reduce_rows")(place, *gs, *t_sibs, *t_rems)


_ANY = pl.BlockSpec(memory_space=pl.ANY)


def _place():
    x, y, c = lax.axis_index("x"), lax.axis_index("y"), lax.axis_index("c")
    chips = [(1 - x, y), (x, 1 - y), (1 - x, 1 - y)]
    return x, y, c, chips


def gather_sems(n):
    return [pltpu.SemaphoreType.DMA((n, N_REMOTE)), pltpu.SemaphoreType.DMA((n, N_REMOTE))]


def gather_stages(shapes, ins, outs, send, recv):
    n = len(shapes)
    x, y, c, chips = _place()
    me = 2 * x + y
    sib = (x, y, 1 - c)

    def rows(a, hl):
        r2 = shapes[a][0] // 2
        return pl.ds(hl * r2, r2)

    def remote(a, k, src, dst, dev):
        return pltpu.make_async_remote_copy(src, dst, send.at[a, k], recv.at[a, k], device_id=dev, device_id_type=MESH)

    def sent(a, k):
        cx, cy = chips[k]
        return remote(a, k, ins[a].at[rows(a, c)], outs[a].at[me, rows(a, c)], (cx, cy, c))

    def got(a, k, hl):
        cx, cy = chips[k]
        return outs[a].at[2 * cx + cy, rows(a, hl)]

    def arrived(a, k):
        return remote(a, k, got(a, k, c), got(a, k, c), (*chips[k], c))

    def passed(a, k, hl):
        return remote(a, 3 + k, got(a, k, hl), got(a, k, hl), sib)

    def start():
        for a in range(n):
            for k in range(3):
                sent(a, k).start()

    def forward():
        for k in range(3):
            for a in range(n):
                arrived(a, k).wait_recv()
                passed(a, k, c).start()

    def finish():
        for k in range(3):
            for a in range(n):
                passed(a, k, 1 - c).wait_recv()
        for k in range(3):
            for a in range(n):
                sent(a, k).wait_send()
                passed(a, k, c).wait_send()

    return start, forward, finish


def scatter_sems(n):
    return [pltpu.SemaphoreType.DMA((n, N_REMOTE + 1)), pltpu.SemaphoreType.DMA((n, N_REMOTE + 1))]


def scatter_out_shapes(gs):
    return (tuple(_sds((g.shape[1] // 2, g.shape[2]), F32) for g in gs)
            + tuple(_sds((N_REMOTE, g.shape[1] // 2, g.shape[2]), BF16) for g in gs))


def scatter_stages(shapes, gf, gb, t_sib, t_rem, send, recv):
    n = len(shapes)
    x, y, c, chips = _place()
    me = 2 * x + y

    def copies():
        out = []
        for a in range(n):
            r2 = shapes[a][0] // 2
            out.append(pltpu.make_async_remote_copy(gf[a].at[me, pl.ds((1 - c) * r2, r2)], t_sib[a], send.at[a, N_REMOTE],
                                                    recv.at[a, N_REMOTE], device_id=(x, y, 1 - c), device_id_type=MESH))
            for k, (cx, cy) in enumerate(chips):
                for o in range(2):
                    tc = c if o == 0 else 1 - c
                    out.append(pltpu.make_async_remote_copy(gb[a].at[2 * cx + cy, pl.ds(tc * r2, r2)], t_rem[a].at[2 * k + o],
                                                            send.at[a, 2 * k + o], recv.at[a, 2 * k + o],
                                                            device_id=(cx, cy, tc), device_id_type=MESH))
        return out

    def start():
        for cp in copies():
            cp.start()

    def finish():
        for cp in copies():
            cp.wait()

    return start, finish


def finish_exchange(v, fs):
    n = len(fs)
    r, w = v.shape
    ndev = 2 * N_CHIPS

    def body(v_ref, *refs):
        out, sum_ref = refs[n:2 * n], refs[2 * n]
        all_ref, send, recv, loc, fsend, frecv = refs[2 * n + 1:]
        x, y, c, chips = _place()
        me, sib = (x, y, c), (x, y, 1 - c)

        def slab(px, py, pc):
            return all_ref.at[4 * px + 2 * py + pc]

        def copy(k, block, to, src=None):
            return pltpu.make_async_remote_copy(slab(*block) if src is None else src, slab(*block), send.at[k], recv.at[k],
                                                device_id=to, device_id_type=MESH)

        mine = pltpu.make_async_copy(v_ref, slab(*me), loc)
        mine.start()
        first = [copy(0, me, sib, src=v_ref)] + [copy(1 + j, me, (*chip, c), src=v_ref) for j, chip in enumerate(chips)]
        for cp in first:
            cp.start()
        passed = [copy(4 + j, (*chip, c), sib) for j, chip in enumerate(chips)]
        for j, chip in enumerate(chips):
            copy(1 + j, (*chip, c), me).wait_recv()
            passed[j].start()
        swaps = []
        for a in range(n):
            r2 = fs[a].shape[0] // 2
            half = out[a].at[pl.ds(c * r2, r2)]
            cp = pltpu.make_async_remote_copy(half, half, fsend.at[a], frecv.at[a], device_id=sib, device_id_type=MESH)
            cp.start()
            swaps.append(cp)
        copy(0, sib, me).wait_recv()
        for j, chip in enumerate(chips):
            copy(4 + j, (*chip, 1 - c), me).wait_recv()
        for cp in first + passed:
            cp.wait_send()
        mine.wait()
        acc = all_ref[0]
        for i in range(1, ndev):
            acc = acc + all_ref[i]
        sum_ref[...] = acc
        for a, cp in enumerate(swaps):
            r2 = fs[a].shape[0] // 2
            theirs = out[a].at[pl.ds((1 - c) * r2, r2)]
            cp.wait_send()
            pltpu.make_async_remote_copy(theirs, theirs, fsend.at[a], frecv.at[a], device_id=sib, device_id_type=MESH).wait_recv()

    vm = pl.BlockSpec(memory_space=pltpu.VMEM)
    res = pl.pallas_call(
        body, out_shape=tuple(_sds(f.shape, F32) for f in fs) + (_sds((r, w), F32),),
        in_specs=[vm] + [_ANY] * n, out_specs=(_ANY,) * n + (vm,), input_output_aliases={a + 1: a for a in range(n)},
        scratch_shapes=[pltpu.VMEM((ndev, r, w), F32), pltpu.SemaphoreType.DMA((7,)), pltpu.SemaphoreType.DMA((7,)),
                        pltpu.SemaphoreType.DMA, pltpu.SemaphoreType.DMA((n,)), pltpu.SemaphoreType.DMA((n,))],
        compiler_params=pltpu.CompilerParams(vmem_limit_bytes=VMEM_LIMIT), name="finish_exchange")(v, *fs)
    return res[n], list(res[:n])


_SMALL = ("norm_g", "q_norm_g", "k_norm_g", "sg_ln_g", "sg_ln_b", "w_s", "b_s", "mem_norm_g", "final_g")
_WEIGHTS = ("norm_g", "w_in", "q_norm_g", "k_norm_g", "sg_ln_g", "sg_ln_b", "w_s", "b_s", "mem_norm_g", "w_mem_kv", "w_br",
            "w_out", "final_g")


def _pack(d, tail=None):
    flat = jnp.concatenate([d[n].reshape(-1) for n in _SMALL] + ([tail.reshape(1)] if tail is not None else []))
    rows = -(-(sum(d[n].size for n in _SMALL) + 1) // (8 * LANES)) * 8
    return jnp.pad(flat, (0, rows * LANES - flat.shape[0])).reshape(rows, LANES)


def _unpack(p, like):
    flat, out, o = p.reshape(-1), {}, 0
    for n in _SMALL:
        out[n] = flat[o:o + like[n].size].reshape(like[n].shape)
        o += like[n].size
    return out


def kernel(x, mem, norm_g, w_in, q_norm_g, k_norm_g, sg_ln_g, sg_ln_b, w_s, b_s, mem_norm_g, w_mem_kv, w_br, w_out, final_g, loss_target, m_norm_g, m_w_in, m_q_norm_g, m_k_norm_g, m_sg_ln_g, m_sg_ln_b, m_w_s, m_b_s, m_mem_norm_g, m_w_mem_kv, m_w_br, m_w_out, m_final_g, v_norm_g, v_w_in, v_q_norm_g, v_k_norm_g, v_sg_ln_g, v_sg_ln_b, v_w_s, v_b_s, v_mem_norm_g, v_w_mem_kv, v_w_br, v_w_out, v_final_g):
    w = dict(norm_g=norm_g, w_in=w_in, q_norm_g=q_norm_g, k_norm_g=k_norm_g, sg_ln_g=sg_ln_g, sg_ln_b=sg_ln_b, w_s=w_s, b_s=b_s,
             mem_norm_g=mem_norm_g, w_mem_kv=w_mem_kv, w_br=w_br, w_out=w_out, final_g=final_g)
    m = dict(norm_g=m_norm_g, w_in=m_w_in, q_norm_g=m_q_norm_g, k_norm_g=m_k_norm_g, sg_ln_g=m_sg_ln_g, sg_ln_b=m_sg_ln_b,
             w_s=m_w_s, b_s=m_b_s, mem_norm_g=m_mem_norm_g, w_mem_kv=m_w_mem_kv, w_br=m_w_br, w_out=m_w_out, final_g=m_final_g)
    v = dict(norm_g=v_norm_g, w_in=v_w_in, q_norm_g=v_q_norm_g, k_norm_g=v_k_norm_g, sg_ln_g=v_sg_ln_g, sg_ln_b=v_sg_ln_b,
             w_s=v_w_s, b_s=v_b_s, mem_norm_g=v_mem_norm_g, w_mem_kv=v_w_mem_kv, w_br=v_w_br, w_out=v_w_out, final_g=v_final_g)
    depth, d = norm_g.shape
    nsh = N_CHIPS
    br_rows = N_BRANCH * A_WIDTH
    br_cols = d // nsh

    shards = [[jnp.swapaxes(w_in[l], 0, 1).astype(BF16), w_mem_kv[l].astype(BF16), w_br[l].astype(BF16).reshape(br_rows, br_cols),
               w_out[l].astype(BF16)] for l in range(depth)]
    place = jnp.stack([2 * lax.axis_index("x") + lax.axis_index("y"), lax.axis_index("c")]).astype(jnp.int32)
    small = {n: w[n] for n in _SMALL}

    sq, dx, grads, reduced = local_fwd_bwd(x[0], mem[0], loss_target[0], small, shards=shards, place=place)

    small_sum, finals = finish_exchange(_pack(grads, tail=sq), [g for layer in reduced for g in layer])
    loss = (0.5 / d) * small_sum.reshape(-1)[sum(small[n].size for n in _SMALL)]
    big_grads = dict(zip(("w_in", "w_mem_kv", "w_br", "w_out"), [finals[a::len(_BIG)] for a in range(len(_BIG))]))
    small_grads = _unpack(small_sum, small)

    out_g, out_d, out_m, out_v = {}, {}, {}, {}
    _, sd, sm, sv = adamw(_pack(small), [small_sum], _pack({n: m[n] for n in _SMALL}), _pack({n: v[n] for n in _SMALL}))
    sd, sm, sv = _unpack(sd, small), _unpack(sm, small), _unpack(sv, small)
    for n in _SMALL:
        out_g[n], out_d[n], out_m[n], out_v[n] = small_grads[n], sd[n], sm[n], sv[n]
    for n, gs in big_grads.items():
        into = (lambda a: jnp.swapaxes(a, 1, 2)) if n == "w_in" else (lambda a: a)
        two_d = lambda a: a.reshape(-1, gs[0].shape[-1])
        res = adamw(two_d(into(w[n])), gs, two_d(into(m[n])), two_d(into(v[n])))
        out_g[n], out_d[n], out_m[n], out_v[n] = [into(t.reshape(into(w[n]).shape)) for t in res]
    return (loss, dx[None], *[out_g[n] for n in _WEIGHTS], *[out_d[n] for n in _WEIGHTS], *[out_m[n] for n in _WEIGHTS],
            *[out_v[n] for n in _WEIGHTS])
```

```python
import functools

import jax
import jax.numpy as jnp
from jax import lax
from jax.experimental import pallas as pl
from jax.experimental.pallas import tpu as pltpu

F32 = jnp.float32
BF16 = jnp.bfloat16

GRID_W = 64
CHUNK = 128
ROPE_THETA = 10000.0
EPS = 1e-6
A_HEADS, A_KV_HEADS, A_HEAD_DIM = 8, 2, 64
A_WIDTH, A_KV_WIDTH = 512, 128
B_GROUPS, B_GROUP_DIM, B_WIDTH = 4, 128, 512
M_HEADS, M_HEAD_DIM, M_WIDTH = 4, 128, 512
N_BRANCH = 3
IN_WIDTH = 6912
O_QA, O_KA, O_VA, O_ZA, O_UB, O_VB, O_ZB, O_QM, O_ZM, O_LG = 0, 512, 640, 768, 1280, 1792, 2304, 2816, 3328, 3840
PBLK = 768
N_PBLK = IN_WIDTH // PBLK
MID_W = 3072
LG_W = 3072

LN2 = 0.6931471805599453
Q_SCALE = A_HEAD_DIM ** -0.5 / LN2
VTE_ROWS = A_HEAD_DIM + 16

ADAM_LR, ADAM_B1, ADAM_B2, ADAM_EPS, ADAM_WD, ADAM_STEP = 0.001, 0.9, 0.999, 1e-08, 0.01, 10

V7X_VMEM_BYTES = 64 * 2**20
VMEM_LIMIT = V7X_VMEM_BYTES - 4 * 2**20
LANES = 128
MESH = pl.DeviceIdType.MESH
N_CHIPS = 4


def _cp(*sem):
    return pltpu.CompilerParams(dimension_semantics=sem if sem else None, vmem_limit_bytes=VMEM_LIMIT)


def _dot(a, b):
    return jnp.dot(a, b, preferred_element_type=F32)


def _dot_nt(a, b):
    return lax.dot_general(a, b, (((1,), (1,)), ((), ())), preferred_element_type=F32)


def _dot_tn(a, b):
    return lax.dot_general(a, b, (((0,), (0,)), ((), ())), preferred_element_type=F32)


def _dot_hi(a, b):
    return jnp.dot(a, b, preferred_element_type=F32, precision=lax.Precision.HIGHEST)


def _group_sum(a, ones):
    hi = a.astype(BF16)
    lo = (a - hi.astype(F32)).astype(BF16)
    return _dot(hi, ones) + _dot(lo, ones)


def _dot_nt_hi(a, b):
    return lax.dot_general(a, b, (((1,), (1,)), ((), ())), preferred_element_type=F32, precision=lax.Precision.HIGHEST)


def _sig(z):
    return 1.0 / (1.0 + jnp.exp(-z))


def _full(shape, once=False):
    nd = len(shape)
    return pl.BlockSpec(shape, lambda *_: (0,) * nd, pipeline_mode=pl.Buffered(1) if once else None)


def _rows(tm, width):
    return pl.BlockSpec((tm, width), lambda i: (i, 0))


def _sds(shape, dtype):
    return jax.ShapeDtypeStruct(shape, dtype)


def rms_fwd(x, g, gather=()):
    s, d = x.shape
    tm = min(s, 512)
    nt = s // tm
    ng = len(gather)

    def body(x_ref, g_ref, *rest):
        g_in, h_ref, g_out = rest[:ng], rest[ng], rest[ng + 1:2 * ng + 1]
        if ng:
            start, forward, finish = gather_stages([a.shape for a in gather], g_in, g_out, *rest[2 * ng + 1:])
            pl.when(pl.program_id(0) == 0)(start)
        xf = x_ref[...]
        r = lax.rsqrt(jnp.mean(xf * xf, axis=-1, keepdims=True) + EPS)
        h_ref[...] = ((xf * r) * g_ref[...]).astype(BF16)
        if ng:
            @pl.when(pl.program_id(0) == nt - 1)
            def _():
                forward()
                finish()

    out = pl.pallas_call(
        body, out_shape=(_sds((s, d), BF16),) + tuple(_sds((N_CHIPS,) + a.shape, a.dtype) for a in gather), grid=(nt,),
        in_specs=[_rows(tm, d), _full((1, d))] + [_ANY] * ng, out_specs=(_rows(tm, d),) + (_ANY,) * ng,
        scratch_shapes=gather_sems(ng) if ng else [],
        compiler_params=_cp("arbitrary"), name="rms_fwd_gather" if ng else "rms_fwd")(x, g, *gather)
    return out[0], list(out[1:])


def proj_fwd(h, w_t, gather=()):
    s, d = h.shape
    n = w_t.shape[0]
    tm = min(s, 1024)
    tn = 2304
    nj, ni = n // tn, s // tm
    ng = len(gather)

    def body(h_ref, w_ref, *rest):
        g_in, o_ref, g_out = rest[:ng], rest[ng], rest[ng + 1:2 * ng + 1]
        step = pl.program_id(0) * ni + pl.program_id(1)
        if ng:
            start, forward, finish = gather_stages([a.shape for a in gather], g_in, g_out, *rest[2 * ng + 1:])
            pl.when(step == 0)(start)
            pl.when(step == (3 * nj * ni) // 4)(forward)
        o_ref[...] = _dot_nt(h_ref[...], w_ref[...]).astype(BF16)
        if ng:
            pl.when(step == nj * ni - 1)(finish)

    out = pl.pallas_call(
        body, out_shape=(_sds((s, n), BF16),) + tuple(_sds((N_CHIPS,) + a.shape, a.dtype) for a in gather), grid=(nj, ni),
        in_specs=[pl.BlockSpec((tm, d), lambda j, i: (i, 0)), pl.BlockSpec((tn, d), lambda j, i: (j, 0))] + [_ANY] * ng,
        out_specs=(pl.BlockSpec((tm, tn), lambda j, i: (i, j)),) + (_ANY,) * ng,
        scratch_shapes=gather_sems(ng) if ng else [],
        compiler_params=_cp("arbitrary", "arbitrary") if ng else _cp("parallel", "parallel"),
        name="proj_fwd_gather" if ng else "proj_fwd")(h, w_t, *gather)
    return out[0], list(out[1:])


def rope_tables(seq):
    n_freq = A_HEAD_DIM // 4
    d = jnp.arange(LANES) % A_HEAD_DIM
    seg, half, freq = d // (2 * n_freq), (d % (2 * n_freq)) // n_freq, d % n_freq
    inv = ROPE_THETA ** (-freq.astype(F32) / n_freq)
    t = jnp.arange(seq)
    pos = jnp.where(seg[None, :] == 0, (t // GRID_W)[:, None], (t % GRID_W)[:, None]).astype(F32)
    ang = pos * inv[None, :]
    cos, sin = jnp.cos(ang), jnp.sin(ang)
    return cos, jnp.where(half[None, :] == 1, sin, 0.0), jnp.where(half[None, :] == 0, -sin, 0.0)


def _group_ones(width, group):
    i = jnp.arange(width)
    return (i[:, None] // group == i[None, :] // group).astype(F32)


def _rope(xn, c, sa, sb):
    w = xn.shape[1]
    return xn * c + pltpu.roll(xn, 16, 1) * sa + pltpu.roll(xn, w - 16, 1) * sb


def _rope_t(dy, c, sa, sb):
    w = dy.shape[1]
    return dy * c + pltpu.roll(dy * sa, w - 16, 1) + pltpu.roll(dy * sb, 16, 1)


def _tile4(t):
    return jnp.concatenate([t, t, t, t], axis=1)


def qk_prep(proj, tabs, qg, kg, gq, gk):
    s = proj.shape[0]
    tm = min(s, 1024)
    c, sa, sb = tabs

    def body(p_ref, c_ref, sa_ref, sb_ref, qg_ref, kg_ref, gq_ref, gk_ref, qt_ref, kr_ref, vb_ref, kt_ref, v0_ref, v1_ref):
        xq = p_ref[:, O_QA:O_QA + A_WIDTH].astype(F32)
        xk = p_ref[:, O_KA:O_KA + A_KV_WIDTH].astype(F32)
        xv = p_ref[:, O_VA:O_VA + A_KV_WIDTH].astype(F32)
        cc, ssa, ssb = c_ref[...], sa_ref[...], sb_ref[...]
        msq = _group_sum(xq * xq, gq_ref[...]) * (1.0 / A_HEAD_DIM)
        qn = (xq * lax.rsqrt(msq + EPS)) * qg_ref[...]
        qr = _rope(qn, _tile4(cc), _tile4(ssa), _tile4(ssb)) * Q_SCALE
        qt_ref[...] = qr.T.astype(BF16)
        msk = _group_sum(xk * xk, gk_ref[...]) * (1.0 / A_HEAD_DIM)
        kn = (xk * lax.rsqrt(msk + EPS)) * kg_ref[...]
        kr = _rope(kn, cc, ssa, ssb)
        kr_ref[...] = kr.astype(BF16)
        vb_ref[...] = xv.astype(BF16)
        kt_ref[...] = kr.T.astype(BF16)
        vt = xv.T.astype(BF16)
        one = jnp.ones((VTE_ROWS - A_HEAD_DIM, tm), BF16)
        v0_ref[...] = jnp.concatenate([vt[:A_HEAD_DIM], one], axis=0)
        v1_ref[...] = jnp.concatenate([vt[A_HEAD_DIM:], one], axis=0)

    tab = _rows(tm, LANES)
    colb = lambda w: pl.BlockSpec((w, tm), lambda i: (0, i))
    return pl.pallas_call(
        body,
        out_shape=(_sds((A_WIDTH, s), BF16), _sds((s, A_KV_WIDTH), BF16), _sds((s, A_KV_WIDTH), BF16),
                   _sds((A_KV_WIDTH, s), BF16), _sds((VTE_ROWS, s), BF16), _sds((VTE_ROWS, s), BF16)),
        grid=(s // tm,),
        in_specs=[_rows(tm, PBLK), tab, tab, tab, _full((1, A_WIDTH)), _full((1, A_KV_WIDTH)),
                  _full((A_WIDTH, A_WIDTH)), _full((A_KV_WIDTH, A_KV_WIDTH))],
        out_specs=(colb(A_WIDTH), _rows(tm, A_KV_WIDTH), _rows(tm, A_KV_WIDTH), colb(A_KV_WIDTH), colb(VTE_ROWS), colb(VTE_ROWS)),
        compiler_params=_cp("parallel"), name="qk_prep")(proj, c, sa, sb, qg, kg, gq, gk)


def _pad_head(q_h, kv):
    z = jnp.zeros_like(q_h)
    return jnp.concatenate([q_h, z], axis=0) if kv == 0 else jnp.concatenate([z, q_h], axis=0)


def attn_fwd(q_t, kr, vte0, vte1, gather=()):
    s = kr.shape[0]
    tq = min(s, 512)
    kc = min(s, 256)
    nkc = s // kc
    nq = s // tq
    grp = A_HEADS // A_KV_HEADS
    ng = len(gather)

    def body(qt_ref, kr_ref, v0_ref, v1_ref, *rest):
        g_in, (o_ref, lse_ref), g_out = rest[:ng], rest[ng:ng + 2], rest[ng + 2:2 * ng + 2]
        qp_ref, m_ref, acc_ref = rest[2 * ng + 2:2 * ng + 5]
        if ng:
            start, forward, finish = gather_stages([g.shape for g in gather], g_in, g_out, *rest[2 * ng + 5:])
            pl.when(pl.program_id(0) == 0)(start)
            pl.when(pl.program_id(0) == (3 * nq) // 4)(forward)

        for h in range(A_HEADS):
            qp_ref[h] = _pad_head(qt_ref[A_HEAD_DIM * h:A_HEAD_DIM * (h + 1), :], h // grp)
        m_ref[...] = jnp.full(m_ref.shape, -1e30, F32)
        acc_ref[...] = jnp.zeros_like(acc_ref)

        def step(ci, carry):
            ks = pl.ds(pl.multiple_of(ci * kc, kc), kc)
            kblk = kr_ref[ks, :]
            vts = (v0_ref[:, ks], v1_ref[:, ks])
            scs = [_dot(kblk, qp_ref[h]) for h in range(A_HEADS)]
            for h in range(A_HEADS):
                sc = scs[h]
                m_prev = m_ref[h:h + 1, :]
                m_new = jnp.maximum(m_prev, jnp.max(sc, axis=0, keepdims=True))
                p = jnp.exp2(sc - m_new)
                acc_ref[h] = acc_ref[h] * jnp.exp2(m_prev - m_new) + _dot(vts[h // grp], p.astype(BF16))
                m_ref[h:h + 1, :] = m_new
            return carry

        lax.fori_loop(0, nkc, step, 0)
        outs, lses = [], []
        for h in range(A_HEADS):
            acc = acc_ref[h]
            l = acc[A_HEAD_DIM:A_HEAD_DIM + 1, :]
            outs.append(acc[:A_HEAD_DIM, :] / l)
            lses.append(m_ref[h:h + 1, :] + jnp.log2(l))
        o_ref[...] = jnp.concatenate(outs, axis=0).T
        lse_ref[...] = jnp.concatenate(lses, axis=0)
        if ng:
            pl.when(pl.program_id(0) == nq - 1)(finish)

    out = pl.pallas_call(
        body,
        out_shape=(_sds((s, A_WIDTH), F32), _sds((A_HEADS, s), F32)) + tuple(_sds((N_CHIPS,) + g.shape, g.dtype) for g in gather),
        grid=(nq,),
        in_specs=[pl.BlockSpec((A_WIDTH, tq), lambda i: (0, i)), _full((s, A_KV_WIDTH)), _full((VTE_ROWS, s)),
                  _full((VTE_ROWS, s))] + [_ANY] * ng,
        out_specs=(_rows(tq, A_WIDTH), pl.BlockSpec((A_HEADS, tq), lambda i: (0, i))) + (_ANY,) * ng,
        scratch_shapes=[pltpu.VMEM((A_HEADS, A_KV_WIDTH, tq), BF16), pltpu.VMEM((A_HEADS, tq), F32),
                        pltpu.VMEM((A_HEADS, VTE_ROWS, tq), F32)] + (gather_sems(ng) if ng else []),
        compiler_params=_cp("arbitrary"), name="attn_fwd_gather" if ng else "attn_fwd")(q_t, kr, vte0, vte1, *gather)
    return out[0], out[1], list(out[2:])


def memkv_fwd(mem, g, w_kv):
    m, d = mem.shape

    def body(mem_ref, g_ref, w_ref, mn_ref, kv_ref):
        mf = mem_ref[...]
        r = lax.rsqrt(jnp.mean(mf * mf, axis=-1, keepdims=True) + EPS)
        mn = ((mf * r) * g_ref[...]).astype(BF16)
        mn_ref[...] = mn
        kv_ref[...] = _dot(mn, w_ref[...]).astype(BF16)

    return pl.pallas_call(
        body, out_shape=(_sds((m, d), BF16), _sds((m, 2 * M_WIDTH), BF16)),
        compiler_params=_cp(), name="memkv_fwd")(mem, g, w_kv)


def _layer_norm_stats(v):
    mu = jnp.mean(v, axis=-1, keepdims=True)
    xc = v - mu
    rstd = lax.rsqrt(jnp.mean(xc * xc, axis=-1, keepdims=True) + EPS)
    return xc * rstd, rstd


def _spatial_mix(vlb, ws_ref, bsb_ref, tm):
    rows = []
    for ci in range(tm // CHUNK):
        cols = []
        for g in range(B_GROUPS):
            blk = vlb[ci * CHUNK:(ci + 1) * CHUNK, g * B_GROUP_DIM:(g + 1) * B_GROUP_DIM]
            cols.append(_dot(ws_ref[g], blk) + bsb_ref[g])
        rows.append(jnp.concatenate(cols, axis=1))
    return jnp.concatenate(rows, axis=0)


def _mem_attn(qm, kv_ref):
    out = []
    for h in range(M_HEADS):
        qh = qm[:, h * M_HEAD_DIM:(h + 1) * M_HEAD_DIM].astype(BF16)
        kh = kv_ref[:, h * M_HEAD_DIM:(h + 1) * M_HEAD_DIM]
        vh = kv_ref[:, M_WIDTH + h * M_HEAD_DIM:M_WIDTH + (h + 1) * M_HEAD_DIM]
        sc = _dot_nt(qh, kh) * (M_HEAD_DIM ** -0.5)
        e = jnp.exp(sc - jnp.max(sc, axis=-1, keepdims=True))
        p = e / jnp.sum(e, axis=-1, keepdims=True)
        out.append((p, _dot(p.astype(BF16), vh)))
    return out


def branch_fwd(x, proj, o_a, kv, ws, bsb, ln_g, ln_b, w_br, w_out, next_g):
    s, d = x.shape
    tm = min(s, 512)

    def body(x_ref, p_ref, oa_ref, kv_ref, ws_ref, bsb_ref, lg_ref, lb_ref, wbr_ref, wo_ref, ng_ref,
             xn_ref, y_ref, up_ref, mg_ref, hn_ref):
        seg = lambda o, w: p_ref[:, o:o + w].astype(F32)
        z_a, u_b, v_b, z_b = seg(O_ZA, A_WIDTH), seg(O_UB, B_WIDTH), seg(O_VB, B_WIDTH), seg(O_ZB, B_WIDTH)
        q_m, z_m = seg(O_QM, M_WIDTH), seg(O_ZM, M_WIDTH)
        xhat, _ = _layer_norm_stats(v_b)
        vln = xhat * lg_ref[...] + lb_ref[...]
        mixed = _spatial_mix(vln.astype(BF16), ws_ref, bsb_ref, tm)
        y_b = (u_b * mixed) * (z_b * _sig(z_b))
        o_m = jnp.concatenate([o for _, o in _mem_attn(q_m, kv_ref)], axis=1)
        y_a = oa_ref[...] * (z_a * _sig(z_a))
        y_m = o_m * (z_m * _sig(z_m))
        merged = None
        for n, yy in enumerate((y_a, y_b, y_m)):
            yb = yy.astype(BF16)
            y_ref[n] = yb
            up = jnp.concatenate([_dot(yb, wbr_ref[c, n]) for c in range(N_CHIPS)], axis=1)
            up_ref[n] = up.astype(BF16)
            t = _sig(seg(O_LG + n * d, d)) * up
            merged = t if merged is None else merged + t
        mb = merged.astype(BF16)
        mg_ref[...] = mb
        xn = x_ref[...] + _dot(mb, wo_ref[...])
        xn_ref[...] = xn
        r = lax.rsqrt(jnp.mean(xn * xn, axis=-1, keepdims=True) + EPS)
        hn_ref[...] = ((xn * r) * ng_ref[...]).astype(BF16)

    return pl.pallas_call(
        body,
        out_shape=(_sds((s, d), F32), _sds((N_BRANCH, s, A_WIDTH), BF16), _sds((N_BRANCH, s, d), BF16), _sds((s, d), BF16),
                   _sds((s, d), BF16)),
        grid=(s // tm,),
        in_specs=[_rows(tm, d), _rows(tm, IN_WIDTH), _rows(tm, A_WIDTH), _full(kv.shape), _full(ws.shape), _full(bsb.shape),
                  _full((1, B_WIDTH)), _full((1, B_WIDTH)), _full(w_br.shape), _full(w_out.shape), _full((1, d))],
        out_specs=(_rows(tm, d), pl.BlockSpec((N_BRANCH, tm, A_WIDTH), lambda i: (0, i, 0)),
                   pl.BlockSpec((N_BRANCH, tm, d), lambda i: (0, i, 0)), _rows(tm, d), _rows(tm, d)),
        compiler_params=_cp("parallel"), name="branch_fwd")(x, proj, o_a, kv, ws, bsb, ln_g, ln_b, w_br, w_out, next_g)


def final_loss(x, fg, tgt):
    s, d = x.shape
    tm = min(s, 512)

    def body(x_ref, g_ref, t_ref, ls_ref, dx_ref, gg_ref):
        @pl.when(pl.program_id(0) == 0)
        def _():
            ls_ref[...] = jnp.zeros_like(ls_ref)
            gg_ref[...] = jnp.zeros_like(gg_ref)

        xf = x_ref[...]
        g = g_ref[...]
        r = lax.rsqrt(jnp.mean(xf * xf, axis=-1, keepdims=True) + EPS)
        xh = xf * r
        e = xh * g - t_ref[...]
        sq = jnp.sum(jnp.sum(e * e, axis=0, keepdims=True), axis=1, keepdims=True)
        ls_ref[...] += jnp.broadcast_to(sq, ls_ref.shape)
        dy = e * (1.0 / d)
        gg_ref[...] += jnp.sum(dy * xh, axis=0, keepdims=True)
        gy = dy * g
        dx_ref[...] = r * (gy - xh * jnp.mean(gy * xh, axis=-1, keepdims=True))

    return pl.pallas_call(
        body, out_shape=(_sds((1, LANES), F32), _sds((s, d), F32), _sds((1, d), F32)), grid=(s // tm,),
        in_specs=[_rows(tm, d), _full((1, d)), _rows(tm, d)],
        out_specs=(_full((1, LANES)), _rows(tm, d), _full((1, d))),
        compiler_params=_cp("arbitrary"), name="final_loss")(x, fg, tgt)


def _pblocks(tm, first, count):
    return [pl.BlockSpec((tm, PBLK), functools.partial(lambda i, b: (i, b), b=first + k)) for k in range(count)]


def merge_bwd(dx, proj, y, up, merged, w_br, w_out):
    s, d = dx.shape
    tm = min(s, 512)
    nlg = LG_W // PBLK
    cw = d // N_CHIPS

    def body(dx_ref, l0, l1, l2, l3, y_ref, up_ref, mg_ref, wbr_ref, wo_ref, dy_ref, dlg_ref, gwo_ref, gwb_ref, gwo16_ref, gwb16_ref):
        @pl.when(pl.program_id(0) == 0)
        def _():
            gwo_ref[...] = jnp.zeros_like(gwo_ref)
            gwb_ref[...] = jnp.zeros_like(gwb_ref)

        dxb = dx_ref[...].astype(BF16)
        dmg = _dot_nt(dxb, wo_ref[...])
        gwo_ref[...] += _dot_tn(mg_ref[...], dxb)
        lg = jnp.concatenate([l0[...], l1[...], l2[...], l3[...]], axis=1).astype(F32)
        for n in range(N_BRANCH):
            g = _sig(lg[:, n * d:(n + 1) * d])
            dup = dmg * g
            dlg_ref[:, n * d:(n + 1) * d] = ((dup * up_ref[n].astype(F32)) * (1.0 - g)).astype(BF16)
            dupb = dup.astype(BF16)
            dyn = None
            for c in range(N_CHIPS):
                blk = dupb[:, c * cw:(c + 1) * cw]
                gwb_ref[c, n] += _dot_tn(y_ref[n], blk)
                t = _dot_nt(blk, wbr_ref[c, n])
                dyn = t if dyn is None else dyn + t
            dy_ref[n] = dyn.astype(BF16)

        @pl.when(pl.program_id(0) == pl.num_programs(0) - 1)
        def _():
            gwo16_ref[...] = gwo_ref[...].astype(BF16)
            gwb16_ref[...] = gwb_ref[...].astype(BF16)

    return pl.pallas_call(
        body,
        out_shape=(_sds((N_BRANCH, s, A_WIDTH), BF16), _sds((s, LG_W), BF16), _sds((d, d), F32), _sds(w_br.shape, F32),
                   _sds((d, d), BF16), _sds(w_br.shape, BF16)),
        grid=(s // tm,),
        in_specs=[_rows(tm, d)] + _pblocks(tm, O_LG // PBLK, nlg) + [
            pl.BlockSpec((N_BRANCH, tm, A_WIDTH), lambda i: (0, i, 0)), pl.BlockSpec((N_BRANCH, tm, d), lambda i: (0, i, 0)),
            _rows(tm, d), _full(w_br.shape, once=True), _full(w_out.shape, once=True)],
        out_specs=(pl.BlockSpec((N_BRANCH, tm, A_WIDTH), lambda i: (0, i, 0)), _rows(tm, LG_W), _full((d, d)), _full(w_br.shape),
                   _full((d, d)), _full(w_br.shape)),
        compiler_params=_cp("arbitrary"), name="merge_bwd")(dx, proj, proj, proj, proj, y, up, merged, w_br, w_out)


def _dsilu(z, sg):
    return sg * (1.0 + z * (1.0 - sg))


def branch_bwd(dy, proj, o_a, kv, ws, ws_t, bsb, ln_g, ln_b, head_sel):
    s = proj.shape[0]
    tm = min(s, 512)
    nmid = MID_W // PBLK

    def body(dy_ref, m0, m1, m2, m3, oa_ref, kv_ref, ws_ref, wst_ref, bsb_ref, lg_ref, lb_ref, sel_ref,
             dmid_ref, dot_ref, dl_ref, gws_ref, gbs_ref, glg_ref, glb_ref, dkv_ref):
        @pl.when(pl.program_id(0) == 0)
        def _():
            for r in (gws_ref, gbs_ref, glg_ref, glb_ref, dkv_ref):
                r[...] = jnp.zeros_like(r)

        mid = jnp.concatenate([m0[...], m1[...], m2[...], m3[...]], axis=1).astype(F32)
        seg = lambda o, w: mid[:, o - O_ZA:o - O_ZA + w]
        z_a, u_b, v_b, z_b = seg(O_ZA, A_WIDTH), seg(O_UB, B_WIDTH), seg(O_VB, B_WIDTH), seg(O_ZB, B_WIDTH)
        q_m, z_m = seg(O_QM, M_WIDTH), seg(O_ZM, M_WIDTH)

        def put(o, v):
            dmid_ref[:, o - O_ZA:o - O_ZA + v.shape[1]] = v.astype(BF16)

        dy_a, dy_b, dy_m = dy_ref[0].astype(F32), dy_ref[1].astype(F32), dy_ref[2].astype(F32)

        o_a_ = oa_ref[...]
        sg = _sig(z_a)
        do_a = dy_a * (z_a * sg)
        put(O_ZA, (dy_a * o_a_) * _dsilu(z_a, sg))
        do_l = do_a * LN2
        dot_ref[...] = do_l.T.astype(BF16)
        dl_ref[...] = _dot_nt_hi(sel_ref[...], do_l * o_a_)

        xhat, rstd = _layer_norm_stats(v_b)
        lng = lg_ref[...]
        vln = xhat * lng + lb_ref[...]
        vlb = vln.astype(BF16)
        mixed = _spatial_mix(vlb, ws_ref, bsb_ref, tm)
        sg = _sig(z_b)
        sl = z_b * sg
        put(O_UB, (dy_b * mixed) * sl)
        put(O_ZB, ((dy_b * u_b) * mixed) * _dsilu(z_b, sg))
        dmix = (dy_b * u_b) * sl
        dmb = dmix.astype(BF16)
        rows = []
        for ci in range(tm // CHUNK):
            cols = []
            for g in range(B_GROUPS):
                rs, cs = slice(ci * CHUNK, (ci + 1) * CHUNK), slice(g * B_GROUP_DIM, (g + 1) * B_GROUP_DIM)
                gws_ref[g] += _dot_nt(dmb[rs, cs], vlb[rs, cs])
                gbs_ref[g] += jnp.broadcast_to(jnp.sum(dmix[rs, cs], axis=1, keepdims=True), (CHUNK, B_GROUP_DIM))
                cols.append(_dot(wst_ref[g], dmb[rs, cs]))
            rows.append(jnp.concatenate(cols, axis=1))
        dvln = jnp.concatenate(rows, axis=0)
        glg_ref[...] += jnp.sum(dvln * xhat, axis=0, keepdims=True)
        glb_ref[...] += jnp.sum(dvln, axis=0, keepdims=True)
        gy = dvln * lng
        put(O_VB, rstd * ((gy - jnp.mean(gy, axis=-1, keepdims=True)) - xhat * jnp.mean(gy * xhat, axis=-1, keepdims=True)))

        sg = _sig(z_m)
        sl = z_m * sg
        heads = _mem_attn(q_m, kv_ref)
        o_m = jnp.concatenate([o for _, o in heads], axis=1)
        put(O_ZM, (dy_m * o_m) * _dsilu(z_m, sg))
        do_m = dy_m * sl
        dqs = []
        for h, (p, o_h) in enumerate(heads):
            hs = slice(h * M_HEAD_DIM, (h + 1) * M_HEAD_DIM)
            vs = slice(M_WIDTH + h * M_HEAD_DIM, M_WIDTH + (h + 1) * M_HEAD_DIM)
            do_h = do_m[:, hs]
            dob = do_h.astype(BF16)
            dp = _dot_nt(dob, kv_ref[:, vs])
            dsc = (p * (dp - jnp.sum(do_h * o_h, axis=-1, keepdims=True))) * (M_HEAD_DIM ** -0.5)
            dsb = dsc.astype(BF16)
            dqs.append(_dot(dsb, kv_ref[:, hs]))
            dkv_ref[:, hs] += _dot_tn(dsb, q_m[:, hs].astype(BF16))
            dkv_ref[:, vs] += _dot_tn(p.astype(BF16), dob)
        put(O_QM, jnp.concatenate(dqs, axis=1))

    return pl.pallas_call(
        body,
        out_shape=(_sds((s, MID_W), BF16), _sds((A_WIDTH, s), BF16), _sds((A_HEADS, s), F32), _sds(ws.shape, F32),
                   _sds(ws.shape, F32), _sds((1, B_WIDTH), F32), _sds((1, B_WIDTH), F32), _sds(kv.shape, F32)),
        grid=(s // tm,),
        in_specs=[pl.BlockSpec((N_BRANCH, tm, A_WIDTH), lambda i: (0, i, 0))] + _pblocks(tm, O_ZA // PBLK, nmid) + [
            _rows(tm, A_WIDTH), _full(kv.shape), _full(ws.shape), _full(ws.shape), _full(bsb.shape),
            _full((1, B_WIDTH)), _full((1, B_WIDTH)), _full(head_sel.shape)],
        out_specs=(_rows(tm, MID_W), pl.BlockSpec((A_WIDTH, tm), lambda i: (0, i)), pl.BlockSpec((A_HEADS, tm), lambda i: (0, i)),
                   _full(ws.shape), _full(ws.shape), _full((1, B_WIDTH)), _full((1, B_WIDTH)), _full(kv.shape)),
        compiler_params=_cp("arbitrary"), name="branch_bwd")(dy, proj, proj, proj, proj, o_a, kv, ws, ws_t, bsb, ln_g, ln_b, head_sel)


def attn_bwd(q_t, do_t, kr, kr_t, vb, lse, delta, scatter=()):
    s = kr.shape[0]
    tq = min(s, 256)
    kc = min(s, 512)
    nkc = s // kc
    nq = s // tq
    grp = A_HEADS // A_KV_HEADS
    ns = len(scatter)
    na = ns // 2

    def body(qt_ref, dot_ref, kr_ref, krt_ref, vb_ref, lse_ref, dl_ref, *rest):
        s_in, (dqt_ref, dk_ref, dv_ref), s_out = rest[:ns], rest[ns:ns + 3], rest[ns + 3:2 * ns + 3]
        qp_ref, dop_ref, dq_ref = rest[2 * ns + 3:2 * ns + 6]
        if ns:
            start, finish = scatter_stages([g.shape[1:] for g in scatter[:na]], s_in[:na], s_in[na:], s_out[:na], s_out[na:],
                                           *rest[2 * ns + 6:])
            pl.when(pl.program_id(0) == 0)(start)

        @pl.when(pl.program_id(0) == 0)
        def _():
            dk_ref[...] = jnp.zeros_like(dk_ref)
            dv_ref[...] = jnp.zeros_like(dv_ref)

        for h in range(A_HEADS):
            hs = slice(A_HEAD_DIM * h, A_HEAD_DIM * (h + 1))
            qp_ref[h] = _pad_head(qt_ref[hs, :], h // grp)
            dop_ref[h] = _pad_head(dot_ref[hs, :], h // grp)
        dq_ref[...] = jnp.zeros_like(dq_ref)

        def step(ci, carry):
            ks = pl.ds(pl.multiple_of(ci * kc, kc), kc)
            kblk, vblk, ktb = kr_ref[ks, :], vb_ref[ks, :], krt_ref[:, ks]
            dv_acc = jnp.zeros((kc, A_KV_WIDTH), F32)
            dk_acc = jnp.zeros((kc, A_KV_WIDTH), F32)
            scs = [_dot(kblk, qp_ref[h]) for h in range(A_HEADS)]
            dps = [_dot(vblk, dop_ref[h]) for h in range(A_HEADS)]
            for h in range(A_HEADS):
                qpad, dopad = qp_ref[h], dop_ref[h]
                p = jnp.exp2(scs[h] - lse_ref[h:h + 1, :])
                dsb = (p * (dps[h] - dl_ref[h:h + 1, :])).astype(BF16)
                dv_acc = dv_acc + _dot_nt(p.astype(BF16), dopad)
                dk_acc = dk_acc + _dot_nt(dsb, qpad)
                dq_ref[h] += _dot(ktb, dsb)
            dv_ref[ks, :] += dv_acc
            dk_ref[ks, :] += dk_acc
            return carry

        lax.fori_loop(0, nkc, step, 0)
        dqt_ref[...] = jnp.concatenate(
            [dq_ref[h][A_HEAD_DIM * (h // grp):A_HEAD_DIM * (h // grp + 1), :] for h in range(A_HEADS)], axis=0)
        if ns:
            pl.when(pl.program_id(0) == nq - 1)(finish)

    colq = pl.BlockSpec((A_WIDTH, tq), lambda i: (0, i))
    colh = pl.BlockSpec((A_HEADS, tq), lambda i: (0, i))
    out = pl.pallas_call(
        body,
        out_shape=(_sds((A_WIDTH, s), F32), _sds((s, A_KV_WIDTH), F32), _sds((s, A_KV_WIDTH), F32)) + scatter_out_shapes(scatter[:na]),
        grid=(nq,),
        in_specs=[colq, colq, _full((s, A_KV_WIDTH)), _full((A_KV_WIDTH, s)), _full((s, A_KV_WIDTH)), colh, colh] + [_ANY] * ns,
        out_specs=(colq, _full((s, A_KV_WIDTH)), _full((s, A_KV_WIDTH))) + (_ANY,) * ns,
        scratch_shapes=[pltpu.VMEM((A_HEADS, A_KV_WIDTH, tq), BF16), pltpu.VMEM((A_HEADS, A_KV_WIDTH, tq), BF16),
                        pltpu.VMEM((A_HEADS, A_KV_WIDTH, tq), F32)] + (scatter_sems(na) if ns else []),
        compiler_params=_cp("arbitrary"), name="attn_bwd_scatter" if ns else "attn_bwd")(
            q_t, do_t, kr, kr_t, vb, lse, delta, *scatter)
    return out[0], out[1], out[2], list(out[3:3 + na]), list(out[3 + na:])


def qk_prep_bwd(proj, dq_t, dkr, dvb, tabs, qg, kg, gq, gk, fold_q, fold_k):
    s = proj.shape[0]
    tm = min(s, 1024)
    c, sa, sb = tabs

    def head_norm_bwd(x, dn, gain, gones, fold):
        ms = _group_sum(x * x, gones) * (1.0 / A_HEAD_DIM)
        r = lax.rsqrt(ms + EPS)
        xh = x * r
        gg = _dot_hi(jnp.sum(dn * xh, axis=0, keepdims=True), fold)
        u = dn * gain
        mean_u = _group_sum(u * xh, gones) * (1.0 / A_HEAD_DIM)
        return r * (u - xh * mean_u), gg

    def body(p_ref, dqt_ref, dk_ref, dv_ref, c_ref, sa_ref, sb_ref, qg_ref, kg_ref, gq_ref, gk_ref, fq_ref, fk_ref,
             dqkv_ref, gqg_ref, gkg_ref):
        @pl.when(pl.program_id(0) == 0)
        def _():
            gqg_ref[...] = jnp.zeros_like(gqg_ref)
            gkg_ref[...] = jnp.zeros_like(gkg_ref)

        cc, ssa, ssb = c_ref[...], sa_ref[...], sb_ref[...]
        dqr = dqt_ref[...].T * Q_SCALE
        dqn = _rope_t(dqr, _tile4(cc), _tile4(ssa), _tile4(ssb))
        dxq, gq_ = head_norm_bwd(p_ref[:, O_QA:O_QA + A_WIDTH].astype(F32), dqn, qg_ref[...], gq_ref[...], fq_ref[...])
        dkn = _rope_t(dk_ref[...], cc, ssa, ssb)
        dxk, gk_ = head_norm_bwd(p_ref[:, O_KA:O_KA + A_KV_WIDTH].astype(F32), dkn, kg_ref[...], gk_ref[...], fk_ref[...])
        gqg_ref[...] += gq_
        gkg_ref[...] += gk_
        dqkv_ref[:, O_QA:O_QA + A_WIDTH] = dxq.astype(BF16)
        dqkv_ref[:, O_KA:O_KA + A_KV_WIDTH] = dxk.astype(BF16)
        dqkv_ref[:, O_VA:O_VA + A_KV_WIDTH] = (dv_ref[...] * (1.0 / LN2)).astype(BF16)

    tab = _rows(tm, LANES)
    return pl.pallas_call(
        body, out_shape=(_sds((s, PBLK), BF16), _sds((1, LANES), F32), _sds((1, LANES), F32)), grid=(s // tm,),
        in_specs=[_rows(tm, PBLK), pl.BlockSpec((A_WIDTH, tm), lambda i: (0, i)), _rows(tm, A_KV_WIDTH), _rows(tm, A_KV_WIDTH),
                  tab, tab, tab, _full((1, A_WIDTH)), _full((1, A_KV_WIDTH)), _full((A_WIDTH, A_WIDTH)),
                  _full((A_KV_WIDTH, A_KV_WIDTH)), _full((A_WIDTH, LANES)), _full((A_KV_WIDTH, LANES))],
        out_specs=(_rows(tm, PBLK), _full((1, LANES)), _full((1, LANES))),
        compiler_params=_cp("arbitrary"), name="qk_prep_bwd")(proj, dq_t, dkr, dvb, c, sa, sb, qg, kg, gq, gk, fold_q, fold_k)


def _pick_dproj(b, d0, d1, d2, use):
    first_lg = 1 + MID_W // PBLK

    @pl.when(b == 0)
    def _():
        use(d0[...])

    @pl.when(jnp.logical_and(b >= 1, b < first_lg))
    def _():
        use(d1[...])

    @pl.when(b >= first_lg)
    def _():
        use(d2[...])


def win_grad(d0, d1, d2, h):
    s, d = h.shape
    tk = min(s, 4096)
    nk = s // tk

    def body(d0_ref, d1_ref, d2_ref, h_ref, o_ref, o16_ref):
        @pl.when(pl.program_id(1) == 0)
        def _():
            o_ref[...] = jnp.zeros_like(o_ref)

        def use(blk):
            o_ref[...] += _dot_tn(blk, h_ref[...])

        _pick_dproj(pl.program_id(0), d0_ref, d1_ref, d2_ref, use)

        @pl.when(pl.program_id(1) == nk - 1)
        def _():
            o16_ref[...] = o_ref[...].astype(BF16)

    def spec(first, count):
        def imap(j, k):
            used = jnp.logical_and(j >= first, j < first + count)
            return (jnp.where(used, k, 0), jnp.clip(j - first, 0, count - 1))
        return pl.BlockSpec((tk, PBLK), imap)

    nm = MID_W // PBLK
    oblk = pl.BlockSpec((PBLK, d), lambda j, k: (j, 0))
    return pl.pallas_call(
        body, out_shape=(_sds((IN_WIDTH, d), F32), _sds((IN_WIDTH, d), BF16)), grid=(N_PBLK, nk),
        in_specs=[spec(0, 1), spec(1, nm), spec(1 + nm, LG_W // PBLK),
                  pl.BlockSpec((tk, d), lambda j, k: (k, 0), pipeline_mode=pl.Buffered(1) if nk == 1 else None)],
        out_specs=(oblk, oblk),
        compiler_params=_cp("parallel", "arbitrary"), name="win_grad")(d0, d1, d2, h)


def h_bwd(d0, d1, d2, w_t, x, dx_out, g, scatter=()):
    s, d = x.shape
    tm = min(s, 512)
    nt = s // tm
    ns = len(scatter)
    na = ns // 2

    def body(d0_ref, d1_ref, d2_ref, w_ref, x_ref, dxo_ref, g_ref, *rest):
        s_in, (dx_ref, gg_ref), s_out = rest[:ns], rest[ns:ns + 2], rest[ns + 2:2 * ns + 2]
        if ns:
            start, finish = scatter_stages([a.shape[1:] for a in scatter[:na]], s_in[:na], s_in[na:], s_out[:na], s_out[na:],
                                           *rest[2 * ns + 2:])
            pl.when(pl.program_id(0) == 0)(start)

        @pl.when(pl.program_id(0) == 0)
        def _():
            gg_ref[...] = jnp.zeros_like(gg_ref)

        dh = (_dot(d0_ref[...], w_ref[0:PBLK, :]) + _dot(d1_ref[...], w_ref[PBLK:PBLK + MID_W, :])
              + _dot(d2_ref[...], w_ref[PBLK + MID_W:, :]))
        xf = x_ref[...]
        r = lax.rsqrt(jnp.mean(xf * xf, axis=-1, keepdims=True) + EPS)
        xh = xf * r
        gg_ref[...] += jnp.sum(dh * xh, axis=0, keepdims=True)
        u = dh * g_ref[...]
        dx_ref[...] = dxo_ref[...] + r * (u - xh * jnp.mean(u * xh, axis=-1, keepdims=True))
        if ns:
            pl.when(pl.program_id(0) == nt - 1)(finish)

    rowb = _rows(tm, d)
    out = pl.pallas_call(
        body, out_shape=(_sds((s, d), F32), _sds((1, d), F32)) + scatter_out_shapes(scatter[:na]), grid=(nt,),
        in_specs=[_rows(tm, PBLK), _rows(tm, MID_W), _rows(tm, LG_W),
                  pl.BlockSpec(w_t.shape, lambda i: (0, 0), pipeline_mode=pl.Buffered(1)), rowb, rowb, _full((1, d))] + [_ANY] * ns,
        out_specs=(rowb, _full((1, d))) + (_ANY,) * ns,
        scratch_shapes=scatter_sems(na) if ns else [],
        compiler_params=_cp("arbitrary"), name="h_bwd_scatter" if ns else "h_bwd")(d0, d1, d2, w_t, x, dx_out, g, *scatter)
    return out[0], out[1], list(out[2:2 + na]), list(out[2 + na:])


def memkv_bwd(mem, g, mem_n, w_kv, dkv):
    m, d = mem.shape

    def body(mem_ref, g_ref, mn_ref, w_ref, dkv_ref, gw_ref, gw16_ref, gg_ref):
        dkb = dkv_ref[...].astype(BF16)
        gw = _dot_tn(mn_ref[...], dkb)
        gw_ref[...] = gw
        gw16_ref[...] = gw.astype(BF16)
        dmn = _dot_nt(dkb, w_ref[...])
        mf = mem_ref[...]
        r = lax.rsqrt(jnp.mean(mf * mf, axis=-1, keepdims=True) + EPS)
        gg_ref[...] = jnp.sum(dmn * (mf * r), axis=0, keepdims=True)

    return pl.pallas_call(
        body, out_shape=(_sds(w_kv.shape, F32), _sds(w_kv.shape, BF16), _sds((1, d), F32)),
        compiler_params=_cp(), name="memkv_bwd")(mem, g, mem_n, w_kv, dkv)


def _layer_consts(seq):
    i = jnp.arange(A_WIDTH)
    return dict(
        tabs=rope_tables(seq),
        gq=_group_ones(A_WIDTH, A_HEAD_DIM).astype(BF16), gk=_group_ones(A_KV_WIDTH, A_HEAD_DIM).astype(BF16),
        fold_q=(i[:, None] % A_HEAD_DIM == jnp.arange(LANES)[None, :]).astype(F32),
        fold_k=(i[:A_KV_WIDTH, None] % A_HEAD_DIM == jnp.arange(LANES)[None, :]).astype(F32),
        head_sel=(jnp.arange(A_HEADS)[:, None] == i[None, :] // A_HEAD_DIM).astype(F32),
    )


_BIG = ("win_t", "wkv", "wbr", "wout")


def _with_own_part(names, gathered, shards, chip, d):
    shape = dict(win_t=(IN_WIDTH, d), wkv=(d, 2 * M_WIDTH), wbr=(N_CHIPS, N_BRANCH, A_WIDTH, d // N_CHIPS), wout=(d, d))
    return {n: lax.dynamic_update_slice(g, sh[None], (chip, 0, 0)).reshape(shape[n]) for n, g, sh in zip(names, gathered, shards)}


def local_fwd_bwd(x, mem, tgt, small, big=None, shards=None, place=None):
    s, d = x.shape
    depth = small["norm_g"].shape[0]
    k = _layer_consts(s)
    row = lambda v: v.reshape(1, -1)
    dist = shards is not None
    if dist:
        big = [None] * depth
    saved = []
    for l in range(depth):
        ng = row(small["norm_g"][l])
        qg = row(jnp.tile(small["q_norm_g"][l], A_HEADS))
        kg = row(jnp.tile(small["k_norm_g"][l], A_KV_HEADS))
        ws = small["w_s"][l].astype(BF16)
        ws_t = jnp.swapaxes(small["w_s"][l], 1, 2).astype(BF16)
        bsb = jnp.broadcast_to(small["b_s"][l][:, :, None], (B_GROUPS, CHUNK, B_GROUP_DIM))
        lng, lnb = row(small["sg_ln_g"][l]), row(small["sg_ln_b"][l])
        mg = row(small["mem_norm_g"][l])
        if l == 0:
            first = tuple(shards[0][:1]) if dist else ()
            h, gathered = rms_fwd(x, ng, gather=first)
            if dist:
                big[0] = _with_own_part(_BIG[:1], gathered, first, place[0], d)
        else:
            h = h_next
        w = big[l]
        late = tuple(shards[0][1:]) if dist and l == 0 else ()
        proj, gathered = proj_fwd(h, w["win_t"], gather=late)
        if late:
            w.update(_with_own_part(_BIG[1:], gathered, late, place[0], d))
        q_t, kr, vb, kr_t, vte0, vte1 = qk_prep(proj, k["tabs"], qg, kg, k["gq"], k["gk"])
        nxt = tuple(shards[l + 1]) if dist and l + 1 < depth else ()
        o_a, lse, gathered = attn_fwd(q_t, kr, vte0, vte1, gather=nxt)
        if nxt:
            big[l + 1] = _with_own_part(_BIG, gathered, nxt, place[0], d)
        mem_n, kv = memkv_fwd(mem, mg, w["wkv"])
        next_g = row(small["norm_g"][l + 1]) if l + 1 < depth else row(small["final_g"])
        x_next, y, up, merged, h_next = branch_fwd(x, proj, o_a, kv, ws, bsb, lng, lnb, w["wbr"], w["wout"], next_g)
        saved.append(dict(x=x, ng=ng, qg=qg, kg=kg, ws=ws, ws_t=ws_t, bsb=bsb, lng=lng, lnb=lnb, mg=mg, h=h, proj=proj,
                          q_t=q_t, kr=kr, kr_t=kr_t, vb=vb, o_a=o_a, lse=lse, mem_n=mem_n, kv=kv, y=y, up=up, merged=merged))
        x = x_next

    sq, dx, g_final = final_loss(x, row(small["final_g"]), tgt)
    grads = {n: [None] * depth for n in ("norm_g", "q_norm_g", "k_norm_g", "sg_ln_g", "sg_ln_b", "w_s", "b_s", "mem_norm_g")}
    parts = lambda g: g.reshape(N_CHIPS, -1, g.shape[-1])
    reduced = [[None] * len(_BIG) for _ in range(depth)]

    def reduce_all(items, t_sib, t_rem):
        if items:
            for (ll, a, _, _), f in zip(items, reduce_rows(place, [i[2] for i in items], t_sib, t_rem)):
                reduced[ll][a] = f

    as_scatter = lambda items: tuple(i[2] for i in items) + tuple(i[3] for i in items)
    pending = []
    for l in reversed(range(depth)):
        sv, w = saved[l], big[l]
        dy, dlg, g_wout, g_wbr, g_wout16, g_wbr16 = merge_bwd(dx, sv["proj"], sv["y"], sv["up"], sv["merged"], w["wbr"], w["wout"])
        dmid, do_t, delta, g_ws, g_bs, g_lng, g_lnb, dkv = branch_bwd(
            dy, sv["proj"], sv["o_a"], sv["kv"], sv["ws"], sv["ws_t"], sv["bsb"], sv["lng"], sv["lnb"], k["head_sel"])
        g_wkv, g_wkv16, g_mg = memkv_bwd(mem, sv["mg"], sv["mem_n"], w["wkv"], dkv)
        if dist:
            pending += [(l, 1, parts(g_wkv), parts(g_wkv16)), (l, 2, parts(g_wbr), parts(g_wbr16)), (l, 3, parts(g_wout), parts(g_wout16))]
        dq_t, dkr, dvb, t_sib, t_rem = attn_bwd(sv["q_t"], do_t, sv["kr"], sv["kr_t"], sv["vb"], sv["lse"], delta,
                                                scatter=as_scatter(pending))
        reduce_all(pending, t_sib, t_rem)
        dqkv, g_qg, g_kg = qk_prep_bwd(sv["proj"], dq_t, dkr, dvb, k["tabs"], sv["qg"], sv["kg"], k["gq"], k["gk"],
                                       k["fold_q"], k["fold_k"])
        g_win, g_win16 = win_grad(dqkv, dmid, dlg, sv["h"])
        pending = [(l, 0, parts(g_win), parts(g_win16))] if dist else []
        last = as_scatter(pending) if l == 0 else ()
        dx, g_ng, t_sib, t_rem = h_bwd(dqkv, dmid, dlg, w["win_t"], sv["x"], dx, sv["ng"], scatter=last)
        if last:
            reduce_all(pending, t_sib, t_rem)
        grads["norm_g"][l] = g_ng[0]
        grads["q_norm_g"][l] = g_qg[0, :A_HEAD_DIM]
        grads["k_norm_g"][l] = g_kg[0, :A_HEAD_DIM]
        grads["sg_ln_g"][l] = g_lng[0]
        grads["sg_ln_b"][l] = g_lnb[0]
        grads["w_s"][l] = g_ws
        grads["b_s"][l] = g_bs[:, :, 0]
        grads["mem_norm_g"][l] = g_mg[0]
        if not dist:
            reduced[l] = dict(zip(_BIG, (parts(g_win), parts(g_wkv), parts(g_wbr), parts(g_wout))))
    grads = {n: jnp.stack(v) for n, v in grads.items()}
    grads["final_g"] = g_final[0]
    return sq[0, 0], dx, grads, reduced


def _row_block(rows, width, cap_bytes=2 * 2**20):
    best = None
    for br in range(8, rows + 1, 8):
        if rows % br == 0 and br * width * 4 <= cap_bytes:
            best = br
    return best if best is not None else rows


def adamw(w, gs, m, v):
    r, c = w.shape
    n = len(gs)
    rs = r // n
    br = _row_block(rs, c)
    nb = rs // br

    def body(w_ref, *refs):
        g_refs, (m_ref, v_ref, og_ref, d_ref, nm_ref, nv_ref) = refs[:n], refs[n:]

        def update(gg):
            mm = ADAM_B1 * m_ref[...] + (1.0 - ADAM_B1) * gg
            vv = ADAM_B2 * v_ref[...] + (1.0 - ADAM_B2) * (gg * gg)
            m_hat = mm / (1.0 - ADAM_B1 ** ADAM_STEP)
            v_hat = vv / (1.0 - ADAM_B2 ** ADAM_STEP)
            og_ref[...] = gg
            d_ref[...] = -ADAM_LR * (m_hat / (jnp.sqrt(v_hat) + ADAM_EPS) + ADAM_WD * w_ref[...])
            nm_ref[...] = mm
            nv_ref[...] = vv

        for k in range(n):
            pl.when(pl.program_id(0) == k)(functools.partial(lambda k: update(g_refs[k][...]), k))

    blk = pl.BlockSpec((br, c), lambda l, i: (l * nb + i, 0))
    g_specs = [pl.BlockSpec((br, c), functools.partial(lambda l, i, k: (jnp.where(l == k, i, 0), 0), k=k)) for k in range(n)]
    return pl.pallas_call(
        body, out_shape=(_sds((r, c), F32),) * 4, grid=(n, nb), in_specs=[blk] + g_specs + [blk, blk], out_specs=(blk,) * 4,
        compiler_params=_cp("arbitrary", "arbitrary"), name="adamw")(w, *gs, m, v)


N_REMOTE = 2 * (N_CHIPS - 1)


def reduce_rows(place, gs, t_sibs, t_rems):
    n = len(gs)
    nt = 2

    def body(place_ref, *refs):
        for a in range(n):
            g_ref, s_ref, t_ref, f_ref = refs[a], refs[n + a], refs[2 * n + a], refs[3 * n + a]
            acc = g_ref[...] + s_ref[...]
            for j in range(N_REMOTE):
                acc = acc + t_ref[j].astype(F32)
            f_ref[...] = acc

    tiles = [(g.shape[1] // 2 // nt, g.shape[2]) for g in gs]
    return pl.pallas_call(
        body, out_shape=tuple(_sds(g.shape[1:], F32) for g in gs),
        grid_spec=pltpu.PrefetchScalarGridSpec(
            num_scalar_prefetch=1, grid=(nt,),
            in_specs=[pl.BlockSpec((None, tr, c), lambda i, p: (p[0], p[1] * nt + i, 0)) for tr, c in tiles]
            + [pl.BlockSpec((tr, c), lambda i, p: (i, 0)) for tr, c in tiles]
            + [pl.BlockSpec((N_REMOTE, tr, c), lambda i, p: (0, i, 0)) for tr, c in tiles],
            out_specs=tuple(pl.BlockSpec((tr, c), lambda i, p: (p[1] * nt + i, 0)) for tr, c in tiles)),
        compiler_params=_cp("parallel"), name="reduce_rows")(place, *gs, *t_sibs, *t_rems)


_ANY = pl.BlockSpec(memory_space=pl.ANY)


def _place():
    x, y, c = lax.axis_index("x"), lax.axis_index("y"), lax.axis_index("c")
    chips = [(1 - x, y), (x, 1 - y), (1 - x, 1 - y)]
    return x, y, c, chips


def gather_sems(n):
    return [pltpu.SemaphoreType.DMA((n, N_REMOTE)), pltpu.SemaphoreType.DMA((n, N_REMOTE))]


def gather_stages(shapes, ins, outs, send, recv):
    n = len(shapes)
    x, y, c, chips = _place()
    me = 2 * x + y
    sib = (x, y, 1 - c)

    def rows(a, hl):
        r2 = shapes[a][0] // 2
        return pl.ds(hl * r2, r2)

    def remote(a, k, src, dst, dev):
        return pltpu.make_async_remote_copy(src, dst, send.at[a, k], recv.at[a, k], device_id=dev, device_id_type=MESH)

    def sent(a, k):
        cx, cy = chips[k]
        return remote(a, k, ins[a].at[rows(a, c)], outs[a].at[me, rows(a, c)], (cx, cy, c))

    def got(a, k, hl):
        cx, cy = chips[k]
        return outs[a].at[2 * cx + cy, rows(a, hl)]

    def arrived(a, k):
        return remote(a, k, got(a, k, c), got(a, k, c), (*chips[k], c))

    def passed(a, k, hl):
        return remote(a, 3 + k, got(a, k, hl), got(a, k, hl), sib)

    def start():
        for a in range(n):
            for k in range(3):
                sent(a, k).start()

    def forward():
        for k in range(3):
            for a in range(n):
                arrived(a, k).wait_recv()
                passed(a, k, c).start()

    def finish():
        for k in range(3):
            for a in range(n):
                passed(a, k, 1 - c).wait_recv()
        for k in range(3):
            for a in range(n):
                sent(a, k).wait_send()
                passed(a, k, c).wait_send()

    return start, forward, finish


def scatter_sems(n):
    return [pltpu.SemaphoreType.DMA((n, N_REMOTE + 1)), pltpu.SemaphoreType.DMA((n, N_REMOTE + 1))]


def scatter_out_shapes(gs):
    return (tuple(_sds((g.shape[1] // 2, g.shape[2]), F32) for g in gs)
            + tuple(_sds((N_REMOTE, g.shape[1] // 2, g.shape[2]), BF16) for g in gs))


def scatter_stages(shapes, gf, gb, t_sib, t_rem, send, recv):
    n = len(shapes)
    x, y, c, chips = _place()
    me = 2 * x + y

    def copies():
        out = []
        for a in range(n):
            r2 = shapes[a][0] // 2
            out.append(pltpu.make_async_remote_copy(gf[a].at[me, pl.ds((1 - c) * r2, r2)], t_sib[a], send.at[a, N_REMOTE],
                                                    recv.at[a, N_REMOTE], device_id=(x, y, 1 - c), device_id_type=MESH))
            for k, (cx, cy) in enumerate(chips):
                for o in range(2):
                    tc = c if o == 0 else 1 - c
                    out.append(pltpu.make_async_remote_copy(gb[a].at[2 * cx + cy, pl.ds(tc * r2, r2)], t_rem[a].at[2 * k + o],
                                                            send.at[a, 2 * k + o], recv.at[a, 2 * k + o],
                                                            device_id=(cx, cy, tc), device_id_type=MESH))
        return out

    def start():
        for cp in copies():
            cp.start()

    def finish():
        for cp in copies():
            cp.wait()

    return start, finish


def finish_exchange(v, fs):
    n = len(fs)
    r, w = v.shape
    ndev = 2 * N_CHIPS

    def body(v_ref, *refs):
        out, sum_ref = refs[n:2 * n], refs[2 * n]
        all_ref, send, recv, loc, fsend, frecv = refs[2 * n + 1:]
        x, y, c, chips = _place()
        me, sib = (x, y, c), (x, y, 1 - c)
        swaps = []
        for a in range(n):
            r2 = fs[a].shape[0] // 2
            half = out[a].at[pl.ds(c * r2, r2)]
            cp = pltpu.make_async_remote_copy(half, half, fsend.at[a], frecv.at[a], device_id=sib, device_id_type=MESH)
            cp.start()
            swaps.append(cp)

        def slab(px, py, pc):
            return all_ref.at[4 * px + 2 * py + pc]

        def copy(k, block, to, src=None):
            return pltpu.make_async_remote_copy(slab(*block) if src is None else src, slab(*block), send.at[k], recv.at[k],
                                                device_id=to, device_id_type=MESH)

        mine = pltpu.make_async_copy(v_ref, slab(*me), loc)
        mine.start()
        first = [copy(0, me, sib, src=v_ref)] + [copy(1 + j, me, (*chip, c), src=v_ref) for j, chip in enumerate(chips)]
        for cp in first:
            cp.start()
        passed = [copy(4 + j, (*chip, c), sib) for j, chip in enumerate(chips)]
        for j, chip in enumerate(chips):
            copy(1 + j, (*chip, c), me).wait_recv()
            passed[j].start()
        copy(0, sib, me).wait_recv()
        for j, chip in enumerate(chips):
            copy(4 + j, (*chip, 1 - c), me).wait_recv()
        for cp in first + passed:
            cp.wait_send()
        mine.wait()
        acc = all_ref[0]
        for i in range(1, ndev):
            acc = acc + all_ref[i]
        sum_ref[...] = acc
        for a, cp in enumerate(swaps):
            r2 = fs[a].shape[0] // 2
            theirs = out[a].at[pl.ds((1 - c) * r2, r2)]
            cp.wait_send()
            pltpu.make_async_remote_copy(theirs, theirs, fsend.at[a], frecv.at[a], device_id=sib, device_id_type=MESH).wait_recv()

    vm = pl.BlockSpec(memory_space=pltpu.VMEM)
    res = pl.pallas_call(
        body, out_shape=tuple(_sds(f.shape, F32) for f in fs) + (_sds((r, w), F32),),
        in_specs=[vm] + [_ANY] * n, out_specs=(_ANY,) * n + (vm,), input_output_aliases={a + 1: a for a in range(n)},
        scratch_shapes=[pltpu.VMEM((ndev, r, w), F32), pltpu.SemaphoreType.DMA((7,)), pltpu.SemaphoreType.DMA((7,)),
                        pltpu.SemaphoreType.DMA, pltpu.SemaphoreType.DMA((n,)), pltpu.SemaphoreType.DMA((n,))],
        compiler_params=pltpu.CompilerParams(vmem_limit_bytes=VMEM_LIMIT), name="finish_exchange")(v, *fs)
    return res[n], list(res[:n])


_SMALL = ("norm_g", "q_norm_g", "k_norm_g", "sg_ln_g", "sg_ln_b", "w_s", "b_s", "mem_norm_g", "final_g")
_WEIGHTS = ("norm_g", "w_in", "q_norm_g", "k_norm_g", "sg_ln_g", "sg_ln_b", "w_s", "b_s", "mem_norm_g", "w_mem_kv", "w_br",
            "w_out", "final_g")


def _pack(d, tail=None):
    flat = jnp.concatenate([d[n].reshape(-1) for n in _SMALL] + ([tail.reshape(1)] if tail is not None else []))
    rows = -(-(sum(d[n].size for n in _SMALL) + 1) // (8 * LANES)) * 8
    return jnp.pad(flat, (0, rows * LANES - flat.shape[0])).reshape(rows, LANES)


def _unpack(p, like):
    flat, out, o = p.reshape(-1), {}, 0
    for n in _SMALL:
        out[n] = flat[o:o + like[n].size].reshape(like[n].shape)
        o += like[n].size
    return out


def kernel(x, mem, norm_g, w_in, q_norm_g, k_norm_g, sg_ln_g, sg_ln_b, w_s, b_s, mem_norm_g, w_mem_kv, w_br, w_out, final_g, loss_target, m_norm_g, m_w_in, m_q_norm_g, m_k_norm_g, m_sg_ln_g, m_sg_ln_b, m_w_s, m_b_s, m_mem_norm_g, m_w_mem_kv, m_w_br, m_w_out, m_final_g, v_norm_g, v_w_in, v_q_norm_g, v_k_norm_g, v_sg_ln_g, v_sg_ln_b, v_w_s, v_b_s, v_mem_norm_g, v_w_mem_kv, v_w_br, v_w_out, v_final_g):
    w = dict(norm_g=norm_g, w_in=w_in, q_norm_g=q_norm_g, k_norm_g=k_norm_g, sg_ln_g=sg_ln_g, sg_ln_b=sg_ln_b, w_s=w_s, b_s=b_s,
             mem_norm_g=mem_norm_g, w_mem_kv=w_mem_kv, w_br=w_br, w_out=w_out, final_g=final_g)
    m = dict(norm_g=m_norm_g, w_in=m_w_in, q_norm_g=m_q_norm_g, k_norm_g=m_k_norm_g, sg_ln_g=m_sg_ln_g, sg_ln_b=m_sg_ln_b,
             w_s=m_w_s, b_s=m_b_s, mem_norm_g=m_mem_norm_g, w_mem_kv=m_w_mem_kv, w_br=m_w_br, w_out=m_w_out, final_g=m_final_g)
    v = dict(norm_g=v_norm_g, w_in=v_w_in, q_norm_g=v_q_norm_g, k_norm_g=v_k_norm_g, sg_ln_g=v_sg_ln_g, sg_ln_b=v_sg_ln_b,
             w_s=v_w_s, b_s=v_b_s, mem_norm_g=v_mem_norm_g, w_mem_kv=v_w_mem_kv, w_br=v_w_br, w_out=v_w_out, final_g=v_final_g)
    depth, d = norm_g.shape
    nsh = N_CHIPS
    br_rows = N_BRANCH * A_WIDTH
    br_cols = d // nsh

    shards = [[jnp.swapaxes(w_in[l], 0, 1).astype(BF16), w_mem_kv[l].astype(BF16), w_br[l].astype(BF16).reshape(br_rows, br_cols),
               w_out[l].astype(BF16)] for l in range(depth)]
    place = jnp.stack([2 * lax.axis_index("x") + lax.axis_index("y"), lax.axis_index("c")]).astype(jnp.int32)
    small = {n: w[n] for n in _SMALL}

    sq, dx, grads, reduced = local_fwd_bwd(x[0], mem[0], loss_target[0], small, shards=shards, place=place)

    small_sum, finals = finish_exchange(_pack(grads, tail=sq), [g for layer in reduced for g in layer])
    loss = (0.5 / d) * small_sum.reshape(-1)[sum(small[n].size for n in _SMALL)]
    big_grads = dict(zip(("w_in", "w_mem_kv", "w_br", "w_out"), [finals[a::len(_BIG)] for a in range(len(_BIG))]))
    small_grads = _unpack(small_sum, small)

    out_g, out_d, out_m, out_v = {}, {}, {}, {}
    _, sd, sm, sv = adamw(_pack(small), [small_sum], _pack({n: m[n] for n in _SMALL}), _pack({n: v[n] for n in _SMALL}))
    sd, sm, sv = _unpack(sd, small), _unpack(sm, small), _unpack(sv, small)
    for n in _SMALL:
        out_g[n], out_d[n], out_m[n], out_v[n] = small_grads[n], sd[n], sm[n], sv[n]
    for n, gs in big_grads.items():
        into = (lambda a: jnp.swapaxes(a, 1, 2)) if n == "w_in" else (lambda a: a)
        two_d = lambda a: a.reshape(-1, gs[0].shape[-1])
        res = adamw(two_d(into(w[n])), gs, two_d(into(m[n])), two_d(into(v[n])))
        out_g[n], out_d[n], out_m[n], out_v[n] = [into(t.reshape(into(w[n]).shape)) for t in res]
    return (loss, dx[None], *[out_g[n] for n in _WEIGHTS], *[out_d[n] for n in _WEIGHTS], *[out_m[n] for n in _WEIGHTS],
            *[out_v[n] for n in _WEIGHTS])
```

```python
import functools

import jax
import jax.numpy as jnp
from jax import lax
from jax.experimental import pallas as pl
from jax.experimental.pallas import tpu as pltpu

F32 = jnp.float32
BF16 = jnp.bfloat16

GRID_W = 64
CHUNK = 128
ROPE_THETA = 10000.0
EPS = 1e-6
A_HEADS, A_KV_HEADS, A_HEAD_DIM = 8, 2, 64
A_WIDTH, A_KV_WIDTH = 512, 128
B_GROUPS, B_GROUP_DIM, B_WIDTH = 4, 128, 512
M_HEADS, M_HEAD_DIM, M_WIDTH = 4, 128, 512
N_BRANCH = 3
IN_WIDTH = 6912
O_QA, O_KA, O_VA, O_ZA, O_UB, O_VB, O_ZB, O_QM, O_ZM, O_LG = 0, 512, 640, 768, 1280, 1792, 2304, 2816, 3328, 3840
PBLK = 768
N_PBLK = IN_WIDTH // PBLK
MID_W = 3072
LG_W = 3072

LN2 = 0.6931471805599453
Q_SCALE = A_HEAD_DIM ** -0.5 / LN2
VTE_ROWS = A_HEAD_DIM + 16

ADAM_LR, ADAM_B1, ADAM_B2, ADAM_EPS, ADAM_WD, ADAM_STEP = 0.001, 0.9, 0.999, 1e-08, 0.01, 10

V7X_VMEM_BYTES = 64 * 2**20
VMEM_LIMIT = V7X_VMEM_BYTES - 4 * 2**20
LANES = 128
MESH = pl.DeviceIdType.MESH
N_CHIPS = 4


def _cp(*sem):
    return pltpu.CompilerParams(dimension_semantics=sem if sem else None, vmem_limit_bytes=VMEM_LIMIT)


def _dot(a, b):
    return jnp.dot(a, b, preferred_element_type=F32)


def _dot_nt(a, b):
    return lax.dot_general(a, b, (((1,), (1,)), ((), ())), preferred_element_type=F32)


def _dot_tn(a, b):
    return lax.dot_general(a, b, (((0,), (0,)), ((), ())), preferred_element_type=F32)


def _dot_hi(a, b):
    return jnp.dot(a, b, preferred_element_type=F32, precision=lax.Precision.HIGHEST)


def _group_sum(a, ones):
    hi = a.astype(BF16)
    lo = (a - hi.astype(F32)).astype(BF16)
    return _dot(hi, ones) + _dot(lo, ones)


def _dot_nt_hi(a, b):
    return lax.dot_general(a, b, (((1,), (1,)), ((), ())), preferred_element_type=F32, precision=lax.Precision.HIGHEST)


def _sig(z):
    return 1.0 / (1.0 + jnp.exp(-z))


def _full(shape, once=False):
    nd = len(shape)
    return pl.BlockSpec(shape, lambda *_: (0,) * nd, pipeline_mode=pl.Buffered(1) if once else None)


def _rows(tm, width):
    return pl.BlockSpec((tm, width), lambda i: (i, 0))


def _sds(shape, dtype):
    return jax.ShapeDtypeStruct(shape, dtype)


def rms_fwd(x, g, gather=()):
    s, d = x.shape
    tm = min(s, 512)
    nt = s // tm
    ng = len(gather)

    def body(x_ref, g_ref, *rest):
        g_in, h_ref, g_out = rest[:ng], rest[ng], rest[ng + 1:2 * ng + 1]
        if ng:
            start, forward, finish = gather_stages([a.shape for a in gather], g_in, g_out, *rest[2 * ng + 1:])
            pl.when(pl.program_id(0) == 0)(start)
        xf = x_ref[...]
        r = lax.rsqrt(jnp.mean(xf * xf, axis=-1, keepdims=True) + EPS)
        h_ref[...] = ((xf * r) * g_ref[...]).astype(BF16)
        if ng:
            @pl.when(pl.program_id(0) == nt - 1)
            def _():
                forward()
                finish()

    out = pl.pallas_call(
        body, out_shape=(_sds((s, d), BF16),) + tuple(_sds((N_CHIPS,) + a.shape, a.dtype) for a in gather), grid=(nt,),
        in_specs=[_rows(tm, d), _full((1, d))] + [_ANY] * ng, out_specs=(_rows(tm, d),) + (_ANY,) * ng,
        scratch_shapes=gather_sems(ng) if ng else [],
        compiler_params=_cp("arbitrary"), name="rms_fwd_gather" if ng else "rms_fwd")(x, g, *gather)
    return out[0], list(out[1:])


def proj_fwd(h, w_t, gather=()):
    s, d = h.shape
    n = w_t.shape[0]
    tm = min(s, 1024)
    tn = 2304
    nj, ni = n // tn, s // tm
    ng = len(gather)

    def body(h_ref, w_ref, *rest):
        g_in, o_ref, g_out = rest[:ng], rest[ng], rest[ng + 1:2 * ng + 1]
        step = pl.program_id(0) * ni + pl.program_id(1)
        if ng:
            start, forward, finish = gather_stages([a.shape for a in gather], g_in, g_out, *rest[2 * ng + 1:])
            pl.when(step == 0)(start)
            pl.when(step == (3 * nj * ni) // 4)(forward)
        o_ref[...] = _dot_nt(h_ref[...], w_ref[...]).astype(BF16)
        if ng:
            pl.when(step == nj * ni - 1)(finish)

    out = pl.pallas_call(
        body, out_shape=(_sds((s, n), BF16),) + tuple(_sds((N_CHIPS,) + a.shape, a.dtype) for a in gather), grid=(nj, ni),
        in_specs=[pl.BlockSpec((tm, d), lambda j, i: (i, 0)), pl.BlockSpec((tn, d), lambda j, i: (j, 0))] + [_ANY] * ng,
        out_specs=(pl.BlockSpec((tm, tn), lambda j, i: (i, j)),) + (_ANY,) * ng,
        scratch_shapes=gather_sems(ng) if ng else [],
        compiler_params=_cp("arbitrary", "arbitrary") if ng else _cp("parallel", "parallel"),
        name="proj_fwd_gather" if ng else "proj_fwd")(h, w_t, *gather)
    return out[0], list(out[1:])


def rope_tables(seq):
    n_freq = A_HEAD_DIM // 4
    d = jnp.arange(LANES) % A_HEAD_DIM
    seg, half, freq = d // (2 * n_freq), (d % (2 * n_freq)) // n_freq, d % n_freq
    inv = ROPE_THETA ** (-freq.astype(F32) / n_freq)
    t = jnp.arange(seq)
    pos = jnp.where(seg[None, :] == 0, (t // GRID_W)[:, None], (t % GRID_W)[:, None]).astype(F32)
    ang = pos * inv[None, :]
    cos, sin = jnp.cos(ang), jnp.sin(ang)
    return cos, jnp.where(half[None, :] == 1, sin, 0.0), jnp.where(half[None, :] == 0, -sin, 0.0)


def _group_ones(width, group):
    i = jnp.arange(width)
    return (i[:, None] // group == i[None, :] // group).astype(F32)


def _rope(xn, c, sa, sb):
    w = xn.shape[1]
    return xn * c + pltpu.roll(xn, 16, 1) * sa + pltpu.roll(xn, w - 16, 1) * sb


def _rope_t(dy, c, sa, sb):
    w = dy.shape[1]
    return dy * c + pltpu.roll(dy * sa, w - 16, 1) + pltpu.roll(dy * sb, 16, 1)


def _tile4(t):
    return jnp.concatenate([t, t, t, t], axis=1)


def qk_prep(proj, tabs, qg, kg, gq, gk):
    s = proj.shape[0]
    tm = min(s, 1024)
    c, sa, sb = tabs

    def body(p_ref, c_ref, sa_ref, sb_ref, qg_ref, kg_ref, gq_ref, gk_ref, qt_ref, kr_ref, vb_ref, kt_ref, v0_ref, v1_ref):
        xq = p_ref[:, O_QA:O_QA + A_WIDTH].astype(F32)
        xk = p_ref[:, O_KA:O_KA + A_KV_WIDTH].astype(F32)
        xv = p_ref[:, O_VA:O_VA + A_KV_WIDTH].astype(F32)
        cc, ssa, ssb = c_ref[...], sa_ref[...], sb_ref[...]
        msq = _group_sum(xq * xq, gq_ref[...]) * (1.0 / A_HEAD_DIM)
        qn = (xq * lax.rsqrt(msq + EPS)) * qg_ref[...]
        qr = _rope(qn, _tile4(cc), _tile4(ssa), _tile4(ssb)) * Q_SCALE
        qt_ref[...] = qr.T.astype(BF16)
        msk = _group_sum(xk * xk, gk_ref[...]) * (1.0 / A_HEAD_DIM)
        kn = (xk * lax.rsqrt(msk + EPS)) * kg_ref[...]
        kr = _rope(kn, cc, ssa, ssb)
        kr_ref[...] = kr.astype(BF16)
        vb_ref[...] = xv.astype(BF16)
        kt_ref[...] = kr.T.astype(BF16)
        vt = xv.T.astype(BF16)
        one = jnp.ones((VTE_ROWS - A_HEAD_DIM, tm), BF16)
        v0_ref[...] = jnp.concatenate([vt[:A_HEAD_DIM], one], axis=0)
        v1_ref[...] = jnp.concatenate([vt[A_HEAD_DIM:], one], axis=0)

    tab = _rows(tm, LANES)
    colb = lambda w: pl.BlockSpec((w, tm), lambda i: (0, i))
    return pl.pallas_call(
        body,
        out_shape=(_sds((A_WIDTH, s), BF16), _sds((s, A_KV_WIDTH), BF16), _sds((s, A_KV_WIDTH), BF16),
                   _sds((A_KV_WIDTH, s), BF16), _sds((VTE_ROWS, s), BF16), _sds((VTE_ROWS, s), BF16)),
        grid=(s // tm,),
        in_specs=[_rows(tm, PBLK), tab, tab, tab, _full((1, A_WIDTH)), _full((1, A_KV_WIDTH)),
                  _full((A_WIDTH, A_WIDTH)), _full((A_KV_WIDTH, A_KV_WIDTH))],
        out_specs=(colb(A_WIDTH), _rows(tm, A_KV_WIDTH), _rows(tm, A_KV_WIDTH), colb(A_KV_WIDTH), colb(VTE_ROWS), colb(VTE_ROWS)),
        compiler_params=_cp("parallel"), name="qk_prep")(proj, c, sa, sb, qg, kg, gq, gk)


def _pad_head(q_h, kv):
    z = jnp.zeros_like(q_h)
    return jnp.concatenate([q_h, z], axis=0) if kv == 0 else jnp.concatenate([z, q_h], axis=0)


def attn_fwd(q_t, kr, vte0, vte1, gather=()):
    s = kr.shape[0]
    tq = min(s, 512)
    kc = min(s, 512)
    nkc = s // kc
    nq = s // tq
    grp = A_HEADS // A_KV_HEADS
    ng = len(gather)

    def body(qt_ref, kr_ref, v0_ref, v1_ref, *rest):
        g_in, (o_ref, lse_ref), g_out = rest[:ng], rest[ng:ng + 2], rest[ng + 2:2 * ng + 2]
        qp_ref, m_ref, acc_ref = rest[2 * ng + 2:2 * ng + 5]
        if ng:
            start, forward, finish = gather_stages([g.shape for g in gather], g_in, g_out, *rest[2 * ng + 5:])
            pl.when(pl.program_id(0) == 0)(start)
            pl.when(pl.program_id(0) == (3 * nq) // 4)(forward)

        for h in range(A_HEADS):
            qp_ref[h] = _pad_head(qt_ref[A_HEAD_DIM * h:A_HEAD_DIM * (h + 1), :], h // grp)
        m_ref[...] = jnp.full(m_ref.shape, -1e30, F32)
        acc_ref[...] = jnp.zeros_like(acc_ref)

        def step(ci, carry):
            ks = pl.ds(pl.multiple_of(ci * kc, kc), kc)
            kblk = kr_ref[ks, :]
            vts = (v0_ref[:, ks], v1_ref[:, ks])
            scs = [_dot(kblk, qp_ref[h]) for h in range(A_HEADS)]
            for h in range(A_HEADS):
                sc = scs[h]
                m_prev = m_ref[h:h + 1, :]
                m_new = jnp.maximum(m_prev, jnp.max(sc, axis=0, keepdims=True))
                p = jnp.exp2(sc - m_new)
                acc_ref[h] = acc_ref[h] * jnp.exp2(m_prev - m_new) + _dot(vts[h // grp], p.astype(BF16))
                m_ref[h:h + 1, :] = m_new
            return carry

        lax.fori_loop(0, nkc, step, 0)
        outs, lses = [], []
        for h in range(A_HEADS):
            acc = acc_ref[h]
            l = acc[A_HEAD_DIM:A_HEAD_DIM + 1, :]
            outs.append(acc[:A_HEAD_DIM, :] / l)
            lses.append(m_ref[h:h + 1, :] + jnp.log2(l))
        o_ref[...] = jnp.concatenate(outs, axis=0).T
        lse_ref[...] = jnp.concatenate(lses, axis=0)
        if ng:
            pl.when(pl.program_id(0) == nq - 1)(finish)

    out = pl.pallas_call(
        body,
        out_shape=(_sds((s, A_WIDTH), F32), _sds((A_HEADS, s), F32)) + tuple(_sds((N_CHIPS,) + g.shape, g.dtype) for g in gather),
        grid=(nq,),
        in_specs=[pl.BlockSpec((A_WIDTH, tq), lambda i: (0, i)), _full((s, A_KV_WIDTH)), _full((VTE_ROWS, s)),
                  _full((VTE_ROWS, s))] + [_ANY] * ng,
        out_specs=(_rows(tq, A_WIDTH), pl.BlockSpec((A_HEADS, tq), lambda i: (0, i))) + (_ANY,) * ng,
        scratch_shapes=[pltpu.VMEM((A_HEADS, A_KV_WIDTH, tq), BF16), pltpu.VMEM((A_HEADS, tq), F32),
                        pltpu.VMEM((A_HEADS, VTE_ROWS, tq), F32)] + (gather_sems(ng) if ng else []),
        compiler_params=_cp("arbitrary"), name="attn_fwd_gather" if ng else "attn_fwd")(q_t, kr, vte0, vte1, *gather)
    return out[0], out[1], list(out[2:])


def memkv_fwd(mem, g, w_kv):
    m, d = mem.shape

    def body(mem_ref, g_ref, w_ref, mn_ref, kv_ref):
        mf = mem_ref[...]
        r = lax.rsqrt(jnp.mean(mf * mf, axis=-1, keepdims=True) + EPS)
        mn = ((mf * r) * g_ref[...]).astype(BF16)
        mn_ref[...] = mn
        kv_ref[...] = _dot(mn, w_ref[...]).astype(BF16)

    return pl.pallas_call(
        body, out_shape=(_sds((m, d), BF16), _sds((m, 2 * M_WIDTH), BF16)),
        compiler_params=_cp(), name="memkv_fwd")(mem, g, w_kv)


def _layer_norm_stats(v):
    mu = jnp.mean(v, axis=-1, keepdims=True)
    xc = v - mu
    rstd = lax.rsqrt(jnp.mean(xc * xc, axis=-1, keepdims=True) + EPS)
    return xc * rstd, rstd


def _spatial_mix(vlb, ws_ref, bsb_ref, tm):
    rows = []
    for ci in range(tm // CHUNK):
        cols = []
        for g in range(B_GROUPS):
            blk = vlb[ci * CHUNK:(ci + 1) * CHUNK, g * B_GROUP_DIM:(g + 1) * B_GROUP_DIM]
            cols.append(_dot(ws_ref[g], blk) + bsb_ref[g])
        rows.append(jnp.concatenate(cols, axis=1))
    return jnp.concatenate(rows, axis=0)


def _mem_attn(qm, kv_ref):
    out = []
    for h in range(M_HEADS):
        qh = qm[:, h * M_HEAD_DIM:(h + 1) * M_HEAD_DIM].astype(BF16)
        kh = kv_ref[:, h * M_HEAD_DIM:(h + 1) * M_HEAD_DIM]
        vh = kv_ref[:, M_WIDTH + h * M_HEAD_DIM:M_WIDTH + (h + 1) * M_HEAD_DIM]
        sc = _dot_nt(qh, kh) * (M_HEAD_DIM ** -0.5)
        e = jnp.exp(sc - jnp.max(sc, axis=-1, keepdims=True))
        p = e / jnp.sum(e, axis=-1, keepdims=True)
        out.append((p, _dot(p.astype(BF16), vh)))
    return out


def branch_fwd(x, proj, o_a, kv, ws, bsb, ln_g, ln_b, w_br, w_out, next_g):
    s, d = x.shape
    tm = min(s, 512)

    def body(x_ref, p_ref, oa_ref, kv_ref, ws_ref, bsb_ref, lg_ref, lb_ref, wbr_ref, wo_ref, ng_ref,
             xn_ref, y_ref, up_ref, mg_ref, hn_ref):
        seg = lambda o, w: p_ref[:, o:o + w].astype(F32)
        z_a, u_b, v_b, z_b = seg(O_ZA, A_WIDTH), seg(O_UB, B_WIDTH), seg(O_VB, B_WIDTH), seg(O_ZB, B_WIDTH)
        q_m, z_m = seg(O_QM, M_WIDTH), seg(O_ZM, M_WIDTH)
        xhat, _ = _layer_norm_stats(v_b)
        vln = xhat * lg_ref[...] + lb_ref[...]
        mixed = _spatial_mix(vln.astype(BF16), ws_ref, bsb_ref, tm)
        y_b = (u_b * mixed) * (z_b * _sig(z_b))
        o_m = jnp.concatenate([o for _, o in _mem_attn(q_m, kv_ref)], axis=1)
        y_a = oa_ref[...] * (z_a * _sig(z_a))
        y_m = o_m * (z_m * _sig(z_m))
        merged = None
        for n, yy in enumerate((y_a, y_b, y_m)):
            yb = yy.astype(BF16)
            y_ref[n] = yb
            up = jnp.concatenate([_dot(yb, wbr_ref[c, n]) for c in range(N_CHIPS)], axis=1)
            up_ref[n] = up.astype(BF16)
            t = _sig(seg(O_LG + n * d, d)) * up
            merged = t if merged is None else merged + t
        mb = merged.astype(BF16)
        mg_ref[...] = mb
        xn = x_ref[...] + _dot(mb, wo_ref[...])
        xn_ref[...] = xn
        r = lax.rsqrt(jnp.mean(xn * xn, axis=-1, keepdims=True) + EPS)
        hn_ref[...] = ((xn * r) * ng_ref[...]).astype(BF16)

    return pl.pallas_call(
        body,
        out_shape=(_sds((s, d), F32), _sds((N_BRANCH, s, A_WIDTH), BF16), _sds((N_BRANCH, s, d), BF16), _sds((s, d), BF16),
                   _sds((s, d), BF16)),
        grid=(s // tm,),
        in_specs=[_rows(tm, d), _rows(tm, IN_WIDTH), _rows(tm, A_WIDTH), _full(kv.shape), _full(ws.shape), _full(bsb.shape),
                  _full((1, B_WIDTH)), _full((1, B_WIDTH)), _full(w_br.shape), _full(w_out.shape), _full((1, d))],
        out_specs=(_rows(tm, d), pl.BlockSpec((N_BRANCH, tm, A_WIDTH), lambda i: (0, i, 0)),
                   pl.BlockSpec((N_BRANCH, tm, d), lambda i: (0, i, 0)), _rows(tm, d), _rows(tm, d)),
        compiler_params=_cp("parallel"), name="branch_fwd")(x, proj, o_a, kv, ws, bsb, ln_g, ln_b, w_br, w_out, next_g)


def final_loss(x, fg, tgt):
    s, d = x.shape
    tm = min(s, 512)

    def body(x_ref, g_ref, t_ref, ls_ref, dx_ref, gg_ref):
        @pl.when(pl.program_id(0) == 0)
        def _():
            ls_ref[...] = jnp.zeros_like(ls_ref)
            gg_ref[...] = jnp.zeros_like(gg_ref)

        xf = x_ref[...]
        g = g_ref[...]
        r = lax.rsqrt(jnp.mean(xf * xf, axis=-1, keepdims=True) + EPS)
        xh = xf * r
        e = xh * g - t_ref[...]
        sq = jnp.sum(jnp.sum(e * e, axis=0, keepdims=True), axis=1, keepdims=True)
        ls_ref[...] += jnp.broadcast_to(sq, ls_ref.shape)
        dy = e * (1.0 / d)
        gg_ref[...] += jnp.sum(dy * xh, axis=0, keepdims=True)
        gy = dy * g
        dx_ref[...] = r * (gy - xh * jnp.mean(gy * xh, axis=-1, keepdims=True))

    return pl.pallas_call(
        body, out_shape=(_sds((1, LANES), F32), _sds((s, d), F32), _sds((1, d), F32)), grid=(s // tm,),
        in_specs=[_rows(tm, d), _full((1, d)), _rows(tm, d)],
        out_specs=(_full((1, LANES)), _rows(tm, d), _full((1, d))),
        compiler_params=_cp("arbitrary"), name="final_loss")(x, fg, tgt)


def _pblocks(tm, first, count):
    return [pl.BlockSpec((tm, PBLK), functools.partial(lambda i, b: (i, b), b=first + k)) for k in range(count)]


def merge_bwd(dx, proj, y, up, merged, w_br, w_out):
    s, d = dx.shape
    tm = min(s, 512)
    nlg = LG_W // PBLK
    cw = d // N_CHIPS

    def body(dx_ref, l0, l1, l2, l3, y_ref, up_ref, mg_ref, wbr_ref, wo_ref, dy_ref, dlg_ref, gwo_ref, gwb_ref, gwo16_ref, gwb16_ref):
        @pl.when(pl.program_id(0) == 0)
        def _():
            gwo_ref[...] = jnp.zeros_like(gwo_ref)
            gwb_ref[...] = jnp.zeros_like(gwb_ref)

        dxb = dx_ref[...].astype(BF16)
        dmg = _dot_nt(dxb, wo_ref[...])
        gwo_ref[...] += _dot_tn(mg_ref[...], dxb)
        lg = jnp.concatenate([l0[...], l1[...], l2[...], l3[...]], axis=1).astype(F32)
        for n in range(N_BRANCH):
            g = _sig(lg[:, n * d:(n + 1) * d])
            dup = dmg * g
            dlg_ref[:, n * d:(n + 1) * d] = ((dup * up_ref[n].astype(F32)) * (1.0 - g)).astype(BF16)
            dupb = dup.astype(BF16)
            dyn = None
            for c in range(N_CHIPS):
                blk = dupb[:, c * cw:(c + 1) * cw]
                gwb_ref[c, n] += _dot_tn(y_ref[n], blk)
                t = _dot_nt(blk, wbr_ref[c, n])
                dyn = t if dyn is None else dyn + t
            dy_ref[n] = dyn.astype(BF16)

        @pl.when(pl.program_id(0) == pl.num_programs(0) - 1)
        def _():
            gwo16_ref[...] = gwo_ref[...].astype(BF16)
            gwb16_ref[...] = gwb_ref[...].astype(BF16)

    return pl.pallas_call(
        body,
        out_shape=(_sds((N_BRANCH, s, A_WIDTH), BF16), _sds((s, LG_W), BF16), _sds((d, d), F32), _sds(w_br.shape, F32),
                   _sds((d, d), BF16), _sds(w_br.shape, BF16)),
        grid=(s // tm,),
        in_specs=[_rows(tm, d)] + _pblocks(tm, O_LG // PBLK, nlg) + [
            pl.BlockSpec((N_BRANCH, tm, A_WIDTH), lambda i: (0, i, 0)), pl.BlockSpec((N_BRANCH, tm, d), lambda i: (0, i, 0)),
            _rows(tm, d), _full(w_br.shape, once=True), _full(w_out.shape, once=True)],
        out_specs=(pl.BlockSpec((N_BRANCH, tm, A_WIDTH), lambda i: (0, i, 0)), _rows(tm, LG_W), _full((d, d)), _full(w_br.shape),
                   _full((d, d)), _full(w_br.shape)),
        compiler_params=_cp("arbitrary"), name="merge_bwd")(dx, proj, proj, proj, proj, y, up, merged, w_br, w_out)


def _dsilu(z, sg):
    return sg * (1.0 + z * (1.0 - sg))


def branch_bwd(dy, proj, o_a, kv, ws, ws_t, bsb, ln_g, ln_b, head_sel):
    s = proj.shape[0]
    tm = min(s, 512)
    nmid = MID_W // PBLK

    def body(dy_ref, m0, m1, m2, m3, oa_ref, kv_ref, ws_ref, wst_ref, bsb_ref, lg_ref, lb_ref, sel_ref,
             dmid_ref, dot_ref, dl_ref, gws_ref, gbs_ref, glg_ref, glb_ref, dkv_ref):
        @pl.when(pl.program_id(0) == 0)
        def _():
            for r in (gws_ref, gbs_ref, glg_ref, glb_ref, dkv_ref):
                r[...] = jnp.zeros_like(r)

        mid = jnp.concatenate([m0[...], m1[...], m2[...], m3[...]], axis=1).astype(F32)
        seg = lambda o, w: mid[:, o - O_ZA:o - O_ZA + w]
        z_a, u_b, v_b, z_b = seg(O_ZA, A_WIDTH), seg(O_UB, B_WIDTH), seg(O_VB, B_WIDTH), seg(O_ZB, B_WIDTH)
        q_m, z_m = seg(O_QM, M_WIDTH), seg(O_ZM, M_WIDTH)

        def put(o, v):
            dmid_ref[:, o - O_ZA:o - O_ZA + v.shape[1]] = v.astype(BF16)

        dy_a, dy_b, dy_m = dy_ref[0].astype(F32), dy_ref[1].astype(F32), dy_ref[2].astype(F32)

        o_a_ = oa_ref[...]
        sg = _sig(z_a)
        do_a = dy_a * (z_a * sg)
        put(O_ZA, (dy_a * o_a_) * _dsilu(z_a, sg))
        do_l = do_a * LN2
        dot_ref[...] = do_l.T.astype(BF16)
        dl_ref[...] = _dot_nt_hi(sel_ref[...], do_l * o_a_)

        xhat, rstd = _layer_norm_stats(v_b)
        lng = lg_ref[...]
        vln = xhat * lng + lb_ref[...]
        vlb = vln.astype(BF16)
        mixed = _spatial_mix(vlb, ws_ref, bsb_ref, tm)
        sg = _sig(z_b)
        sl = z_b * sg
        put(O_UB, (dy_b * mixed) * sl)
        put(O_ZB, ((dy_b * u_b) * mixed) * _dsilu(z_b, sg))
        dmix = (dy_b * u_b) * sl
        dmb = dmix.astype(BF16)
        rows = []
        for ci in range(tm // CHUNK):
            cols = []
            for g in range(B_GROUPS):
                rs, cs = slice(ci * CHUNK, (ci + 1) * CHUNK), slice(g * B_GROUP_DIM, (g + 1) * B_GROUP_DIM)
                gws_ref[g] += _dot_nt(dmb[rs, cs], vlb[rs, cs])
                gbs_ref[g] += jnp.broadcast_to(jnp.sum(dmix[rs, cs], axis=1, keepdims=True), (CHUNK, B_GROUP_DIM))
                cols.append(_dot(wst_ref[g], dmb[rs, cs]))
            rows.append(jnp.concatenate(cols, axis=1))
        dvln = jnp.concatenate(rows, axis=0)
        glg_ref[...] += jnp.sum(dvln * xhat, axis=0, keepdims=True)
        glb_ref[...] += jnp.sum(dvln, axis=0, keepdims=True)
        gy = dvln * lng
        put(O_VB, rstd * ((gy - jnp.mean(gy, axis=-1, keepdims=True)) - xhat * jnp.mean(gy * xhat, axis=-1, keepdims=True)))

        sg = _sig(z_m)
        sl = z_m * sg
        heads = _mem_attn(q_m, kv_ref)
        o_m = jnp.concatenate([o for _, o in heads], axis=1)
        put(O_ZM, (dy_m * o_m) * _dsilu(z_m, sg))
        do_m = dy_m * sl
        dqs = []
        for h, (p, o_h) in enumerate(heads):
            hs = slice(h * M_HEAD_DIM, (h + 1) * M_HEAD_DIM)
            vs = slice(M_WIDTH + h * M_HEAD_DIM, M_WIDTH + (h + 1) * M_HEAD_DIM)
            do_h = do_m[:, hs]
            dob = do_h.astype(BF16)
            dp = _dot_nt(dob, kv_ref[:, vs])
            dsc = (p * (dp - jnp.sum(do_h * o_h, axis=-1, keepdims=True))) * (M_HEAD_DIM ** -0.5)
            dsb = dsc.astype(BF16)
            dqs.append(_dot(dsb, kv_ref[:, hs]))
            dkv_ref[:, hs] += _dot_tn(dsb, q_m[:, hs].astype(BF16))
            dkv_ref[:, vs] += _dot_tn(p.astype(BF16), dob)
        put(O_QM, jnp.concatenate(dqs, axis=1))

    return pl.pallas_call(
        body,
        out_shape=(_sds((s, MID_W), BF16), _sds((A_WIDTH, s), BF16), _sds((A_HEADS, s), F32), _sds(ws.shape, F32),
                   _sds(ws.shape, F32), _sds((1, B_WIDTH), F32), _sds((1, B_WIDTH), F32), _sds(kv.shape, F32)),
        grid=(s // tm,),
        in_specs=[pl.BlockSpec((N_BRANCH, tm, A_WIDTH), lambda i: (0, i, 0))] + _pblocks(tm, O_ZA // PBLK, nmid) + [
            _rows(tm, A_WIDTH), _full(kv.shape), _full(ws.shape), _full(ws.shape), _full(bsb.shape),
            _full((1, B_WIDTH)), _full((1, B_WIDTH)), _full(head_sel.shape)],
        out_specs=(_rows(tm, MID_W), pl.BlockSpec((A_WIDTH, tm), lambda i: (0, i)), pl.BlockSpec((A_HEADS, tm), lambda i: (0, i)),
                   _full(ws.shape), _full(ws.shape), _full((1, B_WIDTH)), _full((1, B_WIDTH)), _full(kv.shape)),
        compiler_params=_cp("arbitrary"), name="branch_bwd")(dy, proj, proj, proj, proj, o_a, kv, ws, ws_t, bsb, ln_g, ln_b, head_sel)


def attn_bwd(q_t, do_t, kr, kr_t, vb, lse, delta, scatter=()):
    s = kr.shape[0]
    tq = min(s, 256)
    kc = min(s, 512)
    nkc = s // kc
    nq = s // tq
    grp = A_HEADS // A_KV_HEADS
    ns = len(scatter)
    na = ns // 2

    def body(qt_ref, dot_ref, kr_ref, krt_ref, vb_ref, lse_ref, dl_ref, *rest):
        s_in, (dqt_ref, dk_ref, dv_ref), s_out = rest[:ns], rest[ns:ns + 3], rest[ns + 3:2 * ns + 3]
        qp_ref, dop_ref, dq_ref = rest[2 * ns + 3:2 * ns + 6]
        if ns:
            start, finish = scatter_stages([g.shape[1:] for g in scatter[:na]], s_in[:na], s_in[na:], s_out[:na], s_out[na:],
                                           *rest[2 * ns + 6:])
            pl.when(pl.program_id(0) == 0)(start)

        @pl.when(pl.program_id(0) == 0)
        def _():
            dk_ref[...] = jnp.zeros_like(dk_ref)
            dv_ref[...] = jnp.zeros_like(dv_ref)

        for h in range(A_HEADS):
            hs = slice(A_HEAD_DIM * h, A_HEAD_DIM * (h + 1))
            qp_ref[h] = _pad_head(qt_ref[hs, :], h // grp)
            dop_ref[h] = _pad_head(dot_ref[hs, :], h // grp)
        dq_ref[...] = jnp.zeros_like(dq_ref)

        def step(ci, carry):
            ks = pl.ds(pl.multiple_of(ci * kc, kc), kc)
            kblk, vblk, ktb = kr_ref[ks, :], vb_ref[ks, :], krt_ref[:, ks]
            dv_acc = jnp.zeros((kc, A_KV_WIDTH), F32)
            dk_acc = jnp.zeros((kc, A_KV_WIDTH), F32)
            scs = [_dot(kblk, qp_ref[h]) for h in range(A_HEADS)]
            dps = [_dot(vblk, dop_ref[h]) for h in range(A_HEADS)]
            for h in range(A_HEADS):
                qpad, dopad = qp_ref[h], dop_ref[h]
                p = jnp.exp2(scs[h] - lse_ref[h:h + 1, :])
                dsb = (p * (dps[h] - dl_ref[h:h + 1, :])).astype(BF16)
                dv_acc = dv_acc + _dot_nt(p.astype(BF16), dopad)
                dk_acc = dk_acc + _dot_nt(dsb, qpad)
                dq_ref[h] += _dot(ktb, dsb)
            dv_ref[ks, :] += dv_acc
            dk_ref[ks, :] += dk_acc
            return carry

        lax.fori_loop(0, nkc, step, 0)
        dqt_ref[...] = jnp.concatenate(
            [dq_ref[h][A_HEAD_DIM * (h // grp):A_HEAD_DIM * (h // grp + 1), :] for h in range(A_HEADS)], axis=0)
        if ns:
            pl.when(pl.program_id(0) == nq - 1)(finish)

    colq = pl.BlockSpec((A_WIDTH, tq), lambda i: (0, i))
    colh = pl.BlockSpec((A_HEADS, tq), lambda i: (0, i))
    out = pl.pallas_call(
        body,
        out_shape=(_sds((A_WIDTH, s), F32), _sds((s, A_KV_WIDTH), F32), _sds((s, A_KV_WIDTH), F32)) + scatter_out_shapes(scatter[:na]),
        grid=(nq,),
        in_specs=[colq, colq, _full((s, A_KV_WIDTH)), _full((A_KV_WIDTH, s)), _full((s, A_KV_WIDTH)), colh, colh] + [_ANY] * ns,
        out_specs=(colq, _full((s, A_KV_WIDTH)), _full((s, A_KV_WIDTH))) + (_ANY,) * ns,
        scratch_shapes=[pltpu.VMEM((A_HEADS, A_KV_WIDTH, tq), BF16), pltpu.VMEM((A_HEADS, A_KV_WIDTH, tq), BF16),
                        pltpu.VMEM((A_HEADS, A_KV_WIDTH, tq), F32)] + (scatter_sems(na) if ns else []),
        compiler_params=_cp("arbitrary"), name="attn_bwd_scatter" if ns else "attn_bwd")(
            q_t, do_t, kr, kr_t, vb, lse, delta, *scatter)
    return out[0], out[1], out[2], list(out[3:3 + na]), list(out[3 + na:])


def qk_prep_bwd(proj, dq_t, dkr, dvb, tabs, qg, kg, gq, gk, fold_q, fold_k):
    s = proj.shape[0]
    tm = min(s, 1024)
    c, sa, sb = tabs

    def head_norm_bwd(x, dn, gain, gones, fold):
        ms = _group_sum(x * x, gones) * (1.0 / A_HEAD_DIM)
        r = lax.rsqrt(ms + EPS)
        xh = x * r
        gg = _dot_hi(jnp.sum(dn * xh, axis=0, keepdims=True), fold)
        u = dn * gain
        mean_u = _group_sum(u * xh, gones) * (1.0 / A_HEAD_DIM)
        return r * (u - xh * mean_u), gg

    def body(p_ref, dqt_ref, dk_ref, dv_ref, c_ref, sa_ref, sb_ref, qg_ref, kg_ref, gq_ref, gk_ref, fq_ref, fk_ref,
             dqkv_ref, gqg_ref, gkg_ref):
        @pl.when(pl.program_id(0) == 0)
        def _():
            gqg_ref[...] = jnp.zeros_like(gqg_ref)
            gkg_ref[...] = jnp.zeros_like(gkg_ref)

        cc, ssa, ssb = c_ref[...], sa_ref[...], sb_ref[...]
        dqr = dqt_ref[...].T * Q_SCALE
        dqn = _rope_t(dqr, _tile4(cc), _tile4(ssa), _tile4(ssb))
        dxq, gq_ = head_norm_bwd(p_ref[:, O_QA:O_QA + A_WIDTH].astype(F32), dqn, qg_ref[...], gq_ref[...], fq_ref[...])
        dkn = _rope_t(dk_ref[...], cc, ssa, ssb)
        dxk, gk_ = head_norm_bwd(p_ref[:, O_KA:O_KA + A_KV_WIDTH].astype(F32), dkn, kg_ref[...], gk_ref[...], fk_ref[...])
        gqg_ref[...] += gq_
        gkg_ref[...] += gk_
        dqkv_ref[:, O_QA:O_QA + A_WIDTH] = dxq.astype(BF16)
        dqkv_ref[:, O_KA:O_KA + A_KV_WIDTH] = dxk.astype(BF16)
        dqkv_ref[:, O_VA:O_VA + A_KV_WIDTH] = (dv_ref[...] * (1.0 / LN2)).astype(BF16)

    tab = _rows(tm, LANES)
    return pl.pallas_call(
        body, out_shape=(_sds((s, PBLK), BF16), _sds((1, LANES), F32), _sds((1, LANES), F32)), grid=(s // tm,),
        in_specs=[_rows(tm, PBLK), pl.BlockSpec((A_WIDTH, tm), lambda i: (0, i)), _rows(tm, A_KV_WIDTH), _rows(tm, A_KV_WIDTH),
                  tab, tab, tab, _full((1, A_WIDTH)), _full((1, A_KV_WIDTH)), _full((A_WIDTH, A_WIDTH)),
                  _full((A_KV_WIDTH, A_KV_WIDTH)), _full((A_WIDTH, LANES)), _full((A_KV_WIDTH, LANES))],
        out_specs=(_rows(tm, PBLK), _full((1, LANES)), _full((1, LANES))),
        compiler_params=_cp("arbitrary"), name="qk_prep_bwd")(proj, dq_t, dkr, dvb, c, sa, sb, qg, kg, gq, gk, fold_q, fold_k)


def _pick_dproj(b, d0, d1, d2, use):
    first_lg = 1 + MID_W // PBLK

    @pl.when(b == 0)
    def _():
        use(d0[...])

    @pl.when(jnp.logical_and(b >= 1, b < first_lg))
    def _():
        use(d1[...])

    @pl.when(b >= first_lg)
    def _():
        use(d2[...])


def win_grad(d0, d1, d2, h):
    s, d = h.shape
    tk = min(s, 4096)
    nk = s // tk

    def body(d0_ref, d1_ref, d2_ref, h_ref, o_ref, o16_ref):
        @pl.when(pl.program_id(1) == 0)
        def _():
            o_ref[...] = jnp.zeros_like(o_ref)

        def use(blk):
            o_ref[...] += _dot_tn(blk, h_ref[...])

        _pick_dproj(pl.program_id(0), d0_ref, d1_ref, d2_ref, use)

        @pl.when(pl.program_id(1) == nk - 1)
        def _():
            o16_ref[...] = o_ref[...].astype(BF16)

    def spec(first, count):
        def imap(j, k):
            used = jnp.logical_and(j >= first, j < first + count)
            return (jnp.where(used, k, 0), jnp.clip(j - first, 0, count - 1))
        return pl.BlockSpec((tk, PBLK), imap)

    nm = MID_W // PBLK
    oblk = pl.BlockSpec((PBLK, d), lambda j, k: (j, 0))
    return pl.pallas_call(
        body, out_shape=(_sds((IN_WIDTH, d), F32), _sds((IN_WIDTH, d), BF16)), grid=(N_PBLK, nk),
        in_specs=[spec(0, 1), spec(1, nm), spec(1 + nm, LG_W // PBLK),
                  pl.BlockSpec((tk, d), lambda j, k: (k, 0), pipeline_mode=pl.Buffered(1) if nk == 1 else None)],
        out_specs=(oblk, oblk),
        compiler_params=_cp("parallel", "arbitrary"), name="win_grad")(d0, d1, d2, h)


def h_bwd(d0, d1, d2, w_t, x, dx_out, g, scatter=()):
    s, d = x.shape
    tm = min(s, 512)
    nt = s // tm
    ns = len(scatter)
    na = ns // 2

    def body(d0_ref, d1_ref, d2_ref, w_ref, x_ref, dxo_ref, g_ref, *rest):
        s_in, (dx_ref, gg_ref), s_out = rest[:ns], rest[ns:ns + 2], rest[ns + 2:2 * ns + 2]
        if ns:
            start, finish = scatter_stages([a.shape[1:] for a in scatter[:na]], s_in[:na], s_in[na:], s_out[:na], s_out[na:],
                                           *rest[2 * ns + 2:])
            pl.when(pl.program_id(0) == 0)(start)

        @pl.when(pl.program_id(0) == 0)
        def _():
            gg_ref[...] = jnp.zeros_like(gg_ref)

        dh = (_dot(d0_ref[...], w_ref[0:PBLK, :]) + _dot(d1_ref[...], w_ref[PBLK:PBLK + MID_W, :])
              + _dot(d2_ref[...], w_ref[PBLK + MID_W:, :]))
        xf = x_ref[...]
        r = lax.rsqrt(jnp.mean(xf * xf, axis=-1, keepdims=True) + EPS)
        xh = xf * r
        gg_ref[...] += jnp.sum(dh * xh, axis=0, keepdims=True)
        u = dh * g_ref[...]
        dx_ref[...] = dxo_ref[...] + r * (u - xh * jnp.mean(u * xh, axis=-1, keepdims=True))
        if ns:
            pl.when(pl.program_id(0) == nt - 1)(finish)

    rowb = _rows(tm, d)
    out = pl.pallas_call(
        body, out_shape=(_sds((s, d), F32), _sds((1, d), F32)) + scatter_out_shapes(scatter[:na]), grid=(nt,),
        in_specs=[_rows(tm, PBLK), _rows(tm, MID_W), _rows(tm, LG_W),
                  pl.BlockSpec(w_t.shape, lambda i: (0, 0), pipeline_mode=pl.Buffered(1)), rowb, rowb, _full((1, d))] + [_ANY] * ns,
        out_specs=(rowb, _full((1, d))) + (_ANY,) * ns,
        scratch_shapes=scatter_sems(na) if ns else [],
        compiler_params=_cp("arbitrary"), name="h_bwd_scatter" if ns else "h_bwd")(d0, d1, d2, w_t, x, dx_out, g, *scatter)
    return out[0], out[1], list(out[2:2 + na]), list(out[2 + na:])


def memkv_bwd(mem, g, mem_n, w_kv, dkv):
    m, d = mem.shape

    def body(mem_ref, g_ref, mn_ref, w_ref, dkv_ref, gw_ref, gw16_ref, gg_ref):
        dkb = dkv_ref[...].astype(BF16)
        gw = _dot_tn(mn_ref[...], dkb)
        gw_ref[...] = gw
        gw16_ref[...] = gw.astype(BF16)
        dmn = _dot_nt(dkb, w_ref[...])
        mf = mem_ref[...]
        r = lax.rsqrt(jnp.mean(mf * mf, axis=-1, keepdims=True) + EPS)
        gg_ref[...] = jnp.sum(dmn * (mf * r), axis=0, keepdims=True)

    return pl.pallas_call(
        body, out_shape=(_sds(w_kv.shape, F32), _sds(w_kv.shape, BF16), _sds((1, d), F32)),
        compiler_params=_cp(), name="memkv_bwd")(mem, g, mem_n, w_kv, dkv)


def _layer_consts(seq):
    i = jnp.arange(A_WIDTH)
    return dict(
        tabs=rope_tables(seq),
        gq=_group_ones(A_WIDTH, A_HEAD_DIM).astype(BF16), gk=_group_ones(A_KV_WIDTH, A_HEAD_DIM).astype(BF16),
        fold_q=(i[:, None] % A_HEAD_DIM == jnp.arange(LANES)[None, :]).astype(F32),
        fold_k=(i[:A_KV_WIDTH, None] % A_HEAD_DIM == jnp.arange(LANES)[None, :]).astype(F32),
        head_sel=(jnp.arange(A_HEADS)[:, None] == i[None, :] // A_HEAD_DIM).astype(F32),
    )


_BIG = ("win_t", "wkv", "wbr", "wout")


def _with_own_part(names, gathered, shards, chip, d):
    shape = dict(win_t=(IN_WIDTH, d), wkv=(d, 2 * M_WIDTH), wbr=(N_CHIPS, N_BRANCH, A_WIDTH, d // N_CHIPS), wout=(d, d))
    return {n: lax.dynamic_update_slice(g, sh[None], (chip, 0, 0)).reshape(shape[n]) for n, g, sh in zip(names, gathered, shards)}


def local_fwd_bwd(x, mem, tgt, small, big=None, shards=None, place=None):
    s, d = x.shape
    depth = small["norm_g"].shape[0]
    k = _layer_consts(s)
    row = lambda v: v.reshape(1, -1)
    dist = shards is not None
    if dist:
        big = [None] * depth
    saved = []
    for l in range(depth):
        ng = row(small["norm_g"][l])
        qg = row(jnp.tile(small["q_norm_g"][l], A_HEADS))
        kg = row(jnp.tile(small["k_norm_g"][l], A_KV_HEADS))
        ws = small["w_s"][l].astype(BF16)
        ws_t = jnp.swapaxes(small["w_s"][l], 1, 2).astype(BF16)
        bsb = jnp.broadcast_to(small["b_s"][l][:, :, None], (B_GROUPS, CHUNK, B_GROUP_DIM))
        lng, lnb = row(small["sg_ln_g"][l]), row(small["sg_ln_b"][l])
        mg = row(small["mem_norm_g"][l])
        if l == 0:
            first = tuple(shards[0][:1]) if dist else ()
            h, gathered = rms_fwd(x, ng, gather=first)
            if dist:
                big[0] = _with_own_part(_BIG[:1], gathered, first, place[0], d)
        else:
            h = h_next
        w = big[l]
        late = tuple(shards[0][1:]) if dist and l == 0 else ()
        proj, gathered = proj_fwd(h, w["win_t"], gather=late)
        if late:
            w.update(_with_own_part(_BIG[1:], gathered, late, place[0], d))
        q_t, kr, vb, kr_t, vte0, vte1 = qk_prep(proj, k["tabs"], qg, kg, k["gq"], k["gk"])
        nxt = tuple(shards[l + 1]) if dist and l + 1 < depth else ()
        o_a, lse, gathered = attn_fwd(q_t, kr, vte0, vte1, gather=nxt)
        if nxt:
            big[l + 1] = _with_own_part(_BIG, gathered, nxt, place[0], d)
        mem_n, kv = memkv_fwd(mem, mg, w["wkv"])
        next_g = row(small["norm_g"][l + 1]) if l + 1 < depth else row(small["final_g"])
        x_next, y, up, merged, h_next = branch_fwd(x, proj, o_a, kv, ws, bsb, lng, lnb, w["wbr"], w["wout"], next_g)
        saved.append(dict(x=x, ng=ng, qg=qg, kg=kg, ws=ws, ws_t=ws_t, bsb=bsb, lng=lng, lnb=lnb, mg=mg, h=h, proj=proj,
                          q_t=q_t, kr=kr, kr_t=kr_t, vb=vb, o_a=o_a, lse=lse, mem_n=mem_n, kv=kv, y=y, up=up, merged=merged))
        x = x_next

    sq, dx, g_final = final_loss(x, row(small["final_g"]), tgt)
    grads = {n: [None] * depth for n in ("norm_g", "q_norm_g", "k_norm_g", "sg_ln_g", "sg_ln_b", "w_s", "b_s", "mem_norm_g")}
    parts = lambda g: g.reshape(N_CHIPS, -1, g.shape[-1])
    reduced = [[None] * len(_BIG) for _ in range(depth)]

    def reduce_all(items, t_sib, t_rem):
        if items:
            for (ll, a, _, _), f in zip(items, reduce_rows(place, [i[2] for i in items], t_sib, t_rem)):
                reduced[ll][a] = f

    as_scatter = lambda items: tuple(i[2] for i in items) + tuple(i[3] for i in items)
    pending = []
    for l in reversed(range(depth)):
        sv, w = saved[l], big[l]
        dy, dlg, g_wout, g_wbr, g_wout16, g_wbr16 = merge_bwd(dx, sv["proj"], sv["y"], sv["up"], sv["merged"], w["wbr"], w["wout"])
        dmid, do_t, delta, g_ws, g_bs, g_lng, g_lnb, dkv = branch_bwd(
            dy, sv["proj"], sv["o_a"], sv["kv"], sv["ws"], sv["ws_t"], sv["bsb"], sv["lng"], sv["lnb"], k["head_sel"])
        g_wkv, g_wkv16, g_mg = memkv_bwd(mem, sv["mg"], sv["mem_n"], w["wkv"], dkv)
        if dist:
            pending += [(l, 1, parts(g_wkv), parts(g_wkv16)), (l, 2, parts(g_wbr), parts(g_wbr16)), (l, 3, parts(g_wout), parts(g_wout16))]
        dq_t, dkr, dvb, t_sib, t_rem = attn_bwd(sv["q_t"], do_t, sv["kr"], sv["kr_t"], sv["vb"], sv["lse"], delta,
                                                scatter=as_scatter(pending))
        reduce_all(pending, t_sib, t_rem)
        dqkv, g_qg, g_kg = qk_prep_bwd(sv["proj"], dq_t, dkr, dvb, k["tabs"], sv["qg"], sv["kg"], k["gq"], k["gk"],
                                       k["fold_q"], k["fold_k"])
        g_win, g_win16 = win_grad(dqkv, dmid, dlg, sv["h"])
        pending = [(l, 0, parts(g_win), parts(g_win16))] if dist else []
        last = as_scatter(pending) if l == 0 else ()
        dx, g_ng, t_sib, t_rem = h_bwd(dqkv, dmid, dlg, w["win_t"], sv["x"], dx, sv["ng"], scatter=last)
        if last:
            reduce_all(pending, t_sib, t_rem)
        grads["norm_g"][l] = g_ng[0]
        grads["q_norm_g"][l] = g_qg[0, :A_HEAD_DIM]
        grads["k_norm_g"][l] = g_kg[0, :A_HEAD_DIM]
        grads["sg_ln_g"][l] = g_lng[0]
        grads["sg_ln_b"][l] = g_lnb[0]
        grads["w_s"][l] = g_ws
        grads["b_s"][l] = g_bs[:, :, 0]
        grads["mem_norm_g"][l] = g_mg[0]
        if not dist:
            reduced[l] = dict(zip(_BIG, (parts(g_win), parts(g_wkv), parts(g_wbr), parts(g_wout))))
    grads = {n: jnp.stack(v) for n, v in grads.items()}
    grads["final_g"] = g_final[0]
    return sq[0, 0], dx, grads, reduced


def _row_block(rows, width, cap_bytes=2 * 2**20):
    best = None
    for br in range(8, rows + 1, 8):
        if rows % br == 0 and br * width * 4 <= cap_bytes:
            best = br
    return best if best is not None else rows


def adamw(w, gs, m, v):
    r, c = w.shape
    n = len(gs)
    rs = r // n
    br = _row_block(rs, c)
    nb = rs // br

    def body(w_ref, *refs):
        g_refs, (m_ref, v_ref, og_ref, d_ref, nm_ref, nv_ref) = refs[:n], refs[n:]

        def update(gg):
            mm = ADAM_B1 * m_ref[...] + (1.0 - ADAM_B1) * gg
            vv = ADAM_B2 * v_ref[...] + (1.0 - ADAM_B2) * (gg * gg)
            m_hat = mm / (1.0 - ADAM_B1 ** ADAM_STEP)
            v_hat = vv / (1.0 - ADAM_B2 ** ADAM_STEP)
            og_ref[...] = gg
            d_ref[...] = -ADAM_LR * (m_hat / (jnp.sqrt(v_hat) + ADAM_EPS) + ADAM_WD * w_ref[...])
            nm_ref[...] = mm
            nv_ref[...] = vv

        for k in range(n):
            pl.when(pl.program_id(0) == k)(functools.partial(lambda k: update(g_refs[k][...]), k))

    blk = pl.BlockSpec((br, c), lambda l, i: (l * nb + i, 0))
    g_specs = [pl.BlockSpec((br, c), functools.partial(lambda l, i, k: (jnp.where(l == k, i, 0), 0), k=k)) for k in range(n)]
    return pl.pallas_call(
        body, out_shape=(_sds((r, c), F32),) * 4, grid=(n, nb), in_specs=[blk] + g_specs + [blk, blk], out_specs=(blk,) * 4,
        compiler_params=_cp("arbitrary", "arbitrary"), name="adamw")(w, *gs, m, v)


N_REMOTE = 2 * (N_CHIPS - 1)


def reduce_rows(place, gs, t_sibs, t_rems):
    n = len(gs)
    nt = 2

    def body(place_ref, *refs):
        for a in range(n):
            g_ref, s_ref, t_ref, f_ref = refs[a], refs[n + a], refs[2 * n + a], refs[3 * n + a]
            acc = g_ref[...] + s_ref[...]
            for j in range(N_REMOTE):
                acc = acc + t_ref[j].astype(F32)
            f_ref[...] = acc

    tiles = [(g.shape[1] // 2 // nt, g.shape[2]) for g in gs]
    return pl.pallas_call(
        body, out_shape=tuple(_sds(g.shape[1:], F32) for g in gs),
        grid_spec=pltpu.PrefetchScalarGridSpec(
            num_scalar_prefetch=1, grid=(nt,),
            in_specs=[pl.BlockSpec((None, tr, c), lambda i, p: (p[0], p[1] * nt + i, 0)) for tr, c in tiles]
            + [pl.BlockSpec((tr, c), lambda i, p: (i, 0)) for tr, c in tiles]
            + [pl.BlockSpec((N_REMOTE, tr, c), lambda i, p: (0, i, 0)) for tr, c in tiles],
            out_specs=tuple(pl.BlockSpec((tr, c), lambda i, p: (p[1] * nt + i, 0)) for tr, c in tiles)),
        compiler_params=_cp("parallel"), name="reduce_rows")(place, *gs, *t_sibs, *t_rems)


_ANY = pl.BlockSpec(memory_space=pl.ANY)


def _place():
    x, y, c = lax.axis_index("x"), lax.axis_index("y"), lax.axis_index("c")
    chips = [(1 - x, y), (x, 1 - y), (1 - x, 1 - y)]
    return x, y, c, chips


def gather_sems(n):
    return [pltpu.SemaphoreType.DMA((n, N_REMOTE)), pltpu.SemaphoreType.DMA((n, N_REMOTE))]


def gather_stages(shapes, ins, outs, send, recv):
    n = len(shapes)
    x, y, c, chips = _place()
    me = 2 * x + y
    sib = (x, y, 1 - c)

    def rows(a, hl):
        r2 = shapes[a][0] // 2
        return pl.ds(hl * r2, r2)

    def remote(a, k, src, dst, dev):
        return pltpu.make_async_remote_copy(src, dst, send.at[a, k], recv.at[a, k], device_id=dev, device_id_type=MESH)

    def sent(a, k):
        cx, cy = chips[k]
        return remote(a, k, ins[a].at[rows(a, c)], outs[a].at[me, rows(a, c)], (cx, cy, c))

    def got(a, k, hl):
        cx, cy = chips[k]
        return outs[a].at[2 * cx + cy, rows(a, hl)]

    def arrived(a, k):
        return remote(a, k, got(a, k, c), got(a, k, c), (*chips[k], c))

    def passed(a, k, hl):
        return remote(a, 3 + k, got(a, k, hl), got(a, k, hl), sib)

    def start():
        for a in range(n):
            for k in range(3):
                sent(a, k).start()

    def forward():
        for k in range(3):
            for a in range(n):
                arrived(a, k).wait_recv()
                passed(a, k, c).start()

    def finish():
        for k in range(3):
            for a in range(n):
                passed(a, k, 1 - c).wait_recv()
        for k in range(3):
            for a in range(n):
                sent(a, k).wait_send()
                passed(a, k, c).wait_send()

    return start, forward, finish


def scatter_sems(n):
    return [pltpu.SemaphoreType.DMA((n, N_REMOTE + 1)), pltpu.SemaphoreType.DMA((n, N_REMOTE + 1))]


def scatter_out_shapes(gs):
    return (tuple(_sds((g.shape[1] // 2, g.shape[2]), F32) for g in gs)
            + tuple(_sds((N_REMOTE, g.shape[1] // 2, g.shape[2]), BF16) for g in gs))


def scatter_stages(shapes, gf, gb, t_sib, t_rem, send, recv):
    n = len(shapes)
    x, y, c, chips = _place()
    me = 2 * x + y

    def copies():
        out = []
        for a in range(n):
            r2 = shapes[a][0] // 2
            out.append(pltpu.make_async_remote_copy(gf[a].at[me, pl.ds((1 - c) * r2, r2)], t_sib[a], send.at[a, N_REMOTE],
                                                    recv.at[a, N_REMOTE], device_id=(x, y, 1 - c), device_id_type=MESH))
            for k, (cx, cy) in enumerate(chips):
                for o in range(2):
                    tc = c if o == 0 else 1 - c
                    out.append(pltpu.make_async_remote_copy(gb[a].at[2 * cx + cy, pl.ds(tc * r2, r2)], t_rem[a].at[2 * k + o],
                                                            send.at[a, 2 * k + o], recv.at[a, 2 * k + o],
                                                            device_id=(cx, cy, tc), device_id_type=MESH))
        return out

    def start():
        for cp in copies():
            cp.start()

    def finish():
        for cp in copies():
            cp.wait()

    return start, finish


def finish_exchange(v, fs):
    n = len(fs)
    r, w = v.shape
    ndev = 2 * N_CHIPS

    def body(v_ref, *refs):
        out, sum_ref = refs[n:2 * n], refs[2 * n]
        all_ref, send, recv, loc, fsend, frecv = refs[2 * n + 1:]
        x, y, c, chips = _place()
        me, sib = (x, y, c), (x, y, 1 - c)
        swaps = []
        for a in range(n):
            r2 = fs[a].shape[0] // 2
            half = out[a].at[pl.ds(c * r2, r2)]
            cp = pltpu.make_async_remote_copy(half, half, fsend.at[a], frecv.at[a], device_id=sib, device_id_type=MESH)
            cp.start()
            swaps.append(cp)

        def slab(px, py, pc):
            return all_ref.at[4 * px + 2 * py + pc]

        def copy(k, block, to, src=None):
            return pltpu.make_async_remote_copy(slab(*block) if src is None else src, slab(*block), send.at[k], recv.at[k],
                                                device_id=to, device_id_type=MESH)

        mine = pltpu.make_async_copy(v_ref, slab(*me), loc)
        mine.start()
        first = [copy(0, me, sib, src=v_ref)] + [copy(1 + j, me, (*chip, c), src=v_ref) for j, chip in enumerate(chips)]
        for cp in first:
            cp.start()
        passed = [copy(4 + j, (*chip, c), sib) for j, chip in enumerate(chips)]
        for j, chip in enumerate(chips):
            copy(1 + j, (*chip, c), me).wait_recv()
            passed[j].start()
        copy(0, sib, me).wait_recv()
        for j, chip in enumerate(chips):
            copy(4 + j, (*chip, 1 - c), me).wait_recv()
        for cp in first + passed:
            cp.wait_send()
        mine.wait()
        acc = all_ref[0]
        for i in range(1, ndev):
            acc = acc + all_ref[i]
        sum_ref[...] = acc
        for a, cp in enumerate(swaps):
            r2 = fs[a].shape[0] // 2
            theirs = out[a].at[pl.ds((1 - c) * r2, r2)]
            cp.wait_send()
            pltpu.make_async_remote_copy(theirs, theirs, fsend.at[a], frecv.at[a], device_id=sib, device_id_type=MESH).wait_recv()

    vm = pl.BlockSpec(memory_space=pltpu.VMEM)
    res = pl.pallas_call(
        body, out_shape=tuple(_sds(f.shape, F32) for f in fs) + (_sds((r, w), F32),),
        in_specs=[vm] + [_ANY] * n, out_specs=(_ANY,) * n + (vm,), input_output_aliases={a + 1: a for a in range(n)},
        scratch_shapes=[pltpu.VMEM((ndev, r, w), F32), pltpu.SemaphoreType.DMA((7,)), pltpu.SemaphoreType.DMA((7,)),
                        pltpu.SemaphoreType.DMA, pltpu.SemaphoreType.DMA((n,)), pltpu.SemaphoreType.DMA((n,))],
        compiler_params=pltpu.CompilerParams(vmem_limit_bytes=VMEM_LIMIT), name="finish_exchange")(v, *fs)
    return res[n], list(res[:n])


_SMALL = ("norm_g", "q_norm_g", "k_norm_g", "sg_ln_g", "sg_ln_b", "w_s", "b_s", "mem_norm_g", "final_g")
_WEIGHTS = ("norm_g", "w_in", "q_norm_g", "k_norm_g", "sg_ln_g", "sg_ln_b", "w_s", "b_s", "mem_norm_g", "w_mem_kv", "w_br",
            "w_out", "final_g")


def _pack(d, tail=None):
    flat = jnp.concatenate([d[n].reshape(-1) for n in _SMALL] + ([tail.reshape(1)] if tail is not None else []))
    rows = -(-(sum(d[n].size for n in _SMALL) + 1) // (8 * LANES)) * 8
    return jnp.pad(flat, (0, rows * LANES - flat.shape[0])).reshape(rows, LANES)


def _unpack(p, like):
    flat, out, o = p.reshape(-1), {}, 0
    for n in _SMALL:
        out[n] = flat[o:o + like[n].size].reshape(like[n].shape)
        o += like[n].size
    return out


def kernel(x, mem, norm_g, w_in, q_norm_g, k_norm_g, sg_ln_g, sg_ln_b, w_s, b_s, mem_norm_g, w_mem_kv, w_br, w_out, final_g, loss_target, m_norm_g, m_w_in, m_q_norm_g, m_k_norm_g, m_sg_ln_g, m_sg_ln_b, m_w_s, m_b_s, m_mem_norm_g, m_w_mem_kv, m_w_br, m_w_out, m_final_g, v_norm_g, v_w_in, v_q_norm_g, v_k_norm_g, v_sg_ln_g, v_sg_ln_b, v_w_s, v_b_s, v_mem_norm_g, v_w_mem_kv, v_w_br, v_w_out, v_final_g):
    w = dict(norm_g=norm_g, w_in=w_in, q_norm_g=q_norm_g, k_norm_g=k_norm_g, sg_ln_g=sg_ln_g, sg_ln_b=sg_ln_b, w_s=w_s, b_s=b_s,
             mem_norm_g=mem_norm_g, w_mem_kv=w_mem_kv, w_br=w_br, w_out=w_out, final_g=final_g)
    m = dict(norm_g=m_norm_g, w_in=m_w_in, q_norm_g=m_q_norm_g, k_norm_g=m_k_norm_g, sg_ln_g=m_sg_ln_g, sg_ln_b=m_sg_ln_b,
             w_s=m_w_s, b_s=m_b_s, mem_norm_g=m_mem_norm_g, w_mem_kv=m_w_mem_kv, w_br=m_w_br, w_out=m_w_out, final_g=m_final_g)
    v = dict(norm_g=v_norm_g, w_in=v_w_in, q_norm_g=v_q_norm_g, k_norm_g=v_k_norm_g, sg_ln_g=v_sg_ln_g, sg_ln_b=v_sg_ln_b,
             w_s=v_w_s, b_s=v_b_s, mem_norm_g=v_mem_norm_g, w_mem_kv=v_w_mem_kv, w_br=v_w_br, w_out=v_w_out, final_g=v_final_g)
    depth, d = norm_g.shape
    nsh = N_CHIPS
    br_rows = N_BRANCH * A_WIDTH
    br_cols = d // nsh

    shards = [[jnp.swapaxes(w_in[l], 0, 1).astype(BF16), w_mem_kv[l].astype(BF16), w_br[l].astype(BF16).reshape(br_rows, br_cols),
               w_out[l].astype(BF16)] for l in range(depth)]
    place = jnp.stack([2 * lax.axis_index("x") + lax.axis_index("y"), lax.axis_index("c")]).astype(jnp.int32)
    small = {n: w[n] for n in _SMALL}

    sq, dx, grads, reduced = local_fwd_bwd(x[0], mem[0], loss_target[0], small, shards=shards, place=place)

    small_sum, finals = finish_exchange(_pack(grads, tail=sq), [g for layer in reduced for g in layer])
    loss = (0.5 / d) * small_sum.reshape(-1)[sum(small[n].size for n in _SMALL)]
    big_grads = dict(zip(("w_in", "w_mem_kv", "w_br", "w_out"), [finals[a::len(_BIG)] for a in range(len(_BIG))]))
    small_grads = _unpack(small_sum, small)

    out_g, out_d, out_m, out_v = {}, {}, {}, {}
    _, sd, sm, sv = adamw(_pack(small), [small_sum], _pack({n: m[n] for n in _SMALL}), _pack({n: v[n] for n in _SMALL}))
    sd, sm, sv = _unpack(sd, small), _unpack(sm, small), _unpack(sv, small)
    for n in _SMALL:
        out_g[n], out_d[n], out_m[n], out_v[n] = small_grads[n], sd[n], sm[n], sv[n]
    for n, gs in big_grads.items():
        into = (lambda a: jnp.swapaxes(a, 1, 2)) if n == "w_in" else (lambda a: a)
        two_d = lambda a: a.reshape(-1, gs[0].shape[-1])
        res = adamw(two_d(into(w[n])), gs, two_d(into(m[n])), two_d(into(v[n])))
        out_g[n], out_d[n], out_m[n], out_v[n] = [into(t.reshape(into(w[n]).shape)) for t in res]
    return (loss, dx[None], *[out_g[n] for n in _WEIGHTS], *[out_d[n] for n in _WEIGHTS], *[out_m[n] for n in _WEIGHTS],
            *[out_v[n] for n in _WEIGHTS])
```

```python
import functools

import jax
import jax.numpy as jnp
from jax import lax
from jax.experimental import pallas as pl
from jax.experimental.pallas import tpu as pltpu

F32 = jnp.float32
BF16 = jnp.bfloat16

GRID_W = 64
CHUNK = 128
ROPE_THETA = 10000.0
EPS = 1e-6
A_HEADS, A_KV_HEADS, A_HEAD_DIM = 8, 2, 64
A_WIDTH, A_KV_WIDTH = 512, 128
B_GROUPS, B_GROUP_DIM, B_WIDTH = 4, 128, 512
M_HEADS, M_HEAD_DIM, M_WIDTH = 4, 128, 512
N_BRANCH = 3
IN_WIDTH = 6912
O_QA, O_KA, O_VA, O_ZA, O_UB, O_VB, O_ZB, O_QM, O_ZM, O_LG = 0, 512, 640, 768, 1280, 1792, 2304, 2816, 3328, 3840
PBLK = 768
N_PBLK = IN_WIDTH // PBLK
MID_W = 3072
LG_W = 3072

LN2 = 0.6931471805599453
Q_SCALE = A_HEAD_DIM ** -0.5 / LN2
VTE_ROWS = A_HEAD_DIM + 16

ADAM_LR, ADAM_B1, ADAM_B2, ADAM_EPS, ADAM_WD, ADAM_STEP = 0.001, 0.9, 0.999, 1e-08, 0.01, 10

V7X_VMEM_BYTES = 64 * 2**20
VMEM_LIMIT = V7X_VMEM_BYTES - 4 * 2**20
LANES = 128
MESH = pl.DeviceIdType.MESH
N_CHIPS = 4


def _cp(*sem):
    return pltpu.CompilerParams(dimension_semantics=sem if sem else None, vmem_limit_bytes=VMEM_LIMIT)


def _dot(a, b):
    return jnp.dot(a, b, preferred_element_type=F32)


def _dot_nt(a, b):
    return lax.dot_general(a, b, (((1,), (1,)), ((), ())), preferred_element_type=F32)


def _dot_tn(a, b):
    return lax.dot_general(a, b, (((0,), (0,)), ((), ())), preferred_element_type=F32)


def _dot_hi(a, b):
    return jnp.dot(a, b, preferred_element_type=F32, precision=lax.Precision.HIGHEST)


def _group_sum(a, ones):
    hi = a.astype(BF16)
    lo = (a - hi.astype(F32)).astype(BF16)
    return _dot(hi, ones) + _dot(lo, ones)


def _dot_nt_hi(a, b):
    return lax.dot_general(a, b, (((1,), (1,)), ((), ())), preferred_element_type=F32, precision=lax.Precision.HIGHEST)


def _sig(z):
    return 1.0 / (1.0 + jnp.exp(-z))


def _full(shape, once=False):
    nd = len(shape)
    return pl.BlockSpec(shape, lambda *_: (0,) * nd, pipeline_mode=pl.Buffered(1) if once else None)


def _rows(tm, width):
    return pl.BlockSpec((tm, width), lambda i: (i, 0))


def _sds(shape, dtype):
    return jax.ShapeDtypeStruct(shape, dtype)


def rms_fwd(x, g, gather=()):
    s, d = x.shape
    tm = min(s, 512)
    nt = s // tm
    ng = len(gather)

    def body(x_ref, g_ref, *rest):
        g_in, h_ref, g_out = rest[:ng], rest[ng], rest[ng + 1:2 * ng + 1]
        if ng:
            start, forward, finish = gather_stages([a.shape for a in gather], g_in, g_out, *rest[2 * ng + 1:])
            pl.when(pl.program_id(0) == 0)(start)
        xf = x_ref[...]
        r = lax.rsqrt(jnp.mean(xf * xf, axis=-1, keepdims=True) + EPS)
        h_ref[...] = ((xf * r) * g_ref[...]).astype(BF16)
        if ng:
            @pl.when(pl.program_id(0) == nt - 1)
            def _():
                forward()
                finish()

    out = pl.pallas_call(
        body, out_shape=(_sds((s, d), BF16),) + tuple(_sds((N_CHIPS,) + a.shape, a.dtype) for a in gather), grid=(nt,),
        in_specs=[_rows(tm, d), _full((1, d))] + [_ANY] * ng, out_specs=(_rows(tm, d),) + (_ANY,) * ng,
        scratch_shapes=gather_sems(ng) if ng else [],
        compiler_params=_cp("arbitrary"), name="rms_fwd_gather" if ng else "rms_fwd")(x, g, *gather)
    return out[0], list(out[1:])


def proj_fwd(h, w_t, gather=()):
    s, d = h.shape
    n = w_t.shape[0]
    tm = min(s, 1024)
    tn = 2304
    nj, ni = n // tn, s // tm
    ng = len(gather)

    def body(h_ref, w_ref, *rest):
        g_in, o_ref, g_out = rest[:ng], rest[ng], rest[ng + 1:2 * ng + 1]
        step = pl.program_id(0) * ni + pl.program_id(1)
        if ng:
            start, forward, finish = gather_stages([a.shape for a in gather], g_in, g_out, *rest[2 * ng + 1:])
            pl.when(step == 0)(start)
            pl.when(step == (3 * nj * ni) // 4)(forward)
        o_ref[...] = _dot_nt(h_ref[...], w_ref[...]).astype(BF16)
        if ng:
            pl.when(step == nj * ni - 1)(finish)

    out = pl.pallas_call(
        body, out_shape=(_sds((s, n), BF16),) + tuple(_sds((N_CHIPS,) + a.shape, a.dtype) for a in gather), grid=(nj, ni),
        in_specs=[pl.BlockSpec((tm, d), lambda j, i: (i, 0)), pl.BlockSpec((tn, d), lambda j, i: (j, 0))] + [_ANY] * ng,
        out_specs=(pl.BlockSpec((tm, tn), lambda j, i: (i, j)),) + (_ANY,) * ng,
        scratch_shapes=gather_sems(ng) if ng else [],
        compiler_params=_cp("arbitrary", "arbitrary") if ng else _cp("parallel", "parallel"),
        name="proj_fwd_gather" if ng else "proj_fwd")(h, w_t, *gather)
    return out[0], list(out[1:])


def rope_tables(seq):
    n_freq = A_HEAD_DIM // 4
    d = jnp.arange(LANES) % A_HEAD_DIM
    seg, half, freq = d // (2 * n_freq), (d % (2 * n_freq)) // n_freq, d % n_freq
    inv = ROPE_THETA ** (-freq.astype(F32) / n_freq)
    t = jnp.arange(seq)
    pos = jnp.where(seg[None, :] == 0, (t // GRID_W)[:, None], (t % GRID_W)[:, None]).astype(F32)
    ang = pos * inv[None, :]
    cos, sin = jnp.cos(ang), jnp.sin(ang)
    return cos, jnp.where(half[None, :] == 1, sin, 0.0), jnp.where(half[None, :] == 0, -sin, 0.0)


def _group_ones(width, group):
    i = jnp.arange(width)
    return (i[:, None] // group == i[None, :] // group).astype(F32)


def _rope(xn, c, sa, sb):
    w = xn.shape[1]
    return xn * c + pltpu.roll(xn, 16, 1) * sa + pltpu.roll(xn, w - 16, 1) * sb


def _rope_t(dy, c, sa, sb):
    w = dy.shape[1]
    return dy * c + pltpu.roll(dy * sa, w - 16, 1) + pltpu.roll(dy * sb, 16, 1)


def _tile4(t):
    return jnp.concatenate([t, t, t, t], axis=1)


def qk_prep(proj, tabs, qg, kg, gq, gk):
    s = proj.shape[0]
    tm = min(s, 1024)
    c, sa, sb = tabs

    def body(p_ref, c_ref, sa_ref, sb_ref, qg_ref, kg_ref, gq_ref, gk_ref, qt_ref, kr_ref, vb_ref, kt_ref, v0_ref, v1_ref):
        xq = p_ref[:, O_QA:O_QA + A_WIDTH].astype(F32)
        xk = p_ref[:, O_KA:O_KA + A_KV_WIDTH].astype(F32)
        xv = p_ref[:, O_VA:O_VA + A_KV_WIDTH].astype(F32)
        cc, ssa, ssb = c_ref[...], sa_ref[...], sb_ref[...]
        msq = _group_sum(xq * xq, gq_ref[...]) * (1.0 / A_HEAD_DIM)
        qn = (xq * lax.rsqrt(msq + EPS)) * qg_ref[...]
        qr = _rope(qn, _tile4(cc), _tile4(ssa), _tile4(ssb)) * Q_SCALE
        qt_ref[...] = qr.T.astype(BF16)
        msk = _group_sum(xk * xk, gk_ref[...]) * (1.0 / A_HEAD_DIM)
        kn = (xk * lax.rsqrt(msk + EPS)) * kg_ref[...]
        kr = _rope(kn, cc, ssa, ssb)
        kr_ref[...] = kr.astype(BF16)
        vb_ref[...] = xv.astype(BF16)
        kt_ref[...] = kr.T.astype(BF16)
        vt = xv.T.astype(BF16)
        one = jnp.ones((VTE_ROWS - A_HEAD_DIM, tm), BF16)
        v0_ref[...] = jnp.concatenate([vt[:A_HEAD_DIM], one], axis=0)
        v1_ref[...] = jnp.concatenate([vt[A_HEAD_DIM:], one], axis=0)

    tab = _rows(tm, LANES)
    colb = lambda w: pl.BlockSpec((w, tm), lambda i: (0, i))
    return pl.pallas_call(
        body,
        out_shape=(_sds((A_WIDTH, s), BF16), _sds((s, A_KV_WIDTH), BF16), _sds((s, A_KV_WIDTH), BF16),
                   _sds((A_KV_WIDTH, s), BF16), _sds((VTE_ROWS, s), BF16), _sds((VTE_ROWS, s), BF16)),
        grid=(s // tm,),
        in_specs=[_rows(tm, PBLK), tab, tab, tab, _full((1, A_WIDTH)), _full((1, A_KV_WIDTH)),
                  _full((A_WIDTH, A_WIDTH)), _full((A_KV_WIDTH, A_KV_WIDTH))],
        out_specs=(colb(A_WIDTH), _rows(tm, A_KV_WIDTH), _rows(tm, A_KV_WIDTH), colb(A_KV_WIDTH), colb(VTE_ROWS), colb(VTE_ROWS)),
        compiler_params=_cp("parallel"), name="qk_prep")(proj, c, sa, sb, qg, kg, gq, gk)


def _pad_head(q_h, kv):
    z = jnp.zeros_like(q_h)
    return jnp.concatenate([q_h, z], axis=0) if kv == 0 else jnp.concatenate([z, q_h], axis=0)


def attn_fwd(q_t, kr, vte0, vte1, gather=()):
    s = kr.shape[0]
    tq = min(s, 512)
    kc = min(s, 256)
    nkc = s // kc
    nq = s // tq
    grp = A_HEADS // A_KV_HEADS
    ng = len(gather)

    def body(qt_ref, kr_ref, v0_ref, v1_ref, *rest):
        g_in, (o_ref, lse_ref), g_out = rest[:ng], rest[ng:ng + 2], rest[ng + 2:2 * ng + 2]
        qp_ref, m_ref, acc_ref = rest[2 * ng + 2:2 * ng + 5]
        if ng:
            start, forward, finish = gather_stages([g.shape for g in gather], g_in, g_out, *rest[2 * ng + 5:])
            pl.when(pl.program_id(0) == 0)(start)
            pl.when(pl.program_id(0) == (3 * nq) // 4)(forward)

        for h in range(A_HEADS):
            qp_ref[h] = _pad_head(qt_ref[A_HEAD_DIM * h:A_HEAD_DIM * (h + 1), :], h // grp)
        m_ref[...] = jnp.full(m_ref.shape, -1e30, F32)
        acc_ref[...] = jnp.zeros_like(acc_ref)

        def step(ci, carry):
            ks = pl.ds(pl.multiple_of(ci * kc, kc), kc)
            kblk = kr_ref[ks, :]
            vts = (v0_ref[:, ks], v1_ref[:, ks])
            scs = [_dot(kblk, qp_ref[h]) for h in range(A_HEADS)]
            for h in range(A_HEADS):
                sc = scs[h]
                m_prev = m_ref[h:h + 1, :]
                m_new = jnp.maximum(m_prev, jnp.max(sc, axis=0, keepdims=True))
                p = jnp.exp2(sc - m_new)
                acc_ref[h] = acc_ref[h] * jnp.exp2(m_prev - m_new) + _dot(vts[h // grp], p.astype(BF16))
                m_ref[h:h + 1, :] = m_new
            return carry

        lax.fori_loop(0, nkc, step, 0)
        outs, lses = [], []
        for h in range(A_HEADS):
            acc = acc_ref[h]
            l = acc[A_HEAD_DIM:A_HEAD_DIM + 1, :]
            outs.append(acc[:A_HEAD_DIM, :] / l)
            lses.append(m_ref[h:h + 1, :] + jnp.log2(l))
        o_ref[...] = jnp.concatenate(outs, axis=0).T
        lse_ref[...] = jnp.concatenate(lses, axis=0)
        if ng:
            pl.when(pl.program_id(0) == nq - 1)(finish)

    out = pl.pallas_call(
        body,
        out_shape=(_sds((s, A_WIDTH), F32), _sds((A_HEADS, s), F32)) + tuple(_sds((N_CHIPS,) + g.shape, g.dtype) for g in gather),
        grid=(nq,),
        in_specs=[pl.BlockSpec((A_WIDTH, tq), lambda i: (0, i)), _full((s, A_KV_WIDTH)), _full((VTE_ROWS, s)),
                  _full((VTE_ROWS, s))] + [_ANY] * ng,
        out_specs=(_rows(tq, A_WIDTH), pl.BlockSpec((A_HEADS, tq), lambda i: (0, i))) + (_ANY,) * ng,
        scratch_shapes=[pltpu.VMEM((A_HEADS, A_KV_WIDTH, tq), BF16), pltpu.VMEM((A_HEADS, tq), F32),
                        pltpu.VMEM((A_HEADS, VTE_ROWS, tq), F32)] + (gather_sems(ng) if ng else []),
        compiler_params=_cp("arbitrary"), name="attn_fwd_gather" if ng else "attn_fwd")(q_t, kr, vte0, vte1, *gather)
    return out[0], out[1], list(out[2:])


def memkv_fwd(mem, g, w_kv):
    m, d = mem.shape

    def body(mem_ref, g_ref, w_ref, mn_ref, kv_ref):
        mf = mem_ref[...]
        r = lax.rsqrt(jnp.mean(mf * mf, axis=-1, keepdims=True) + EPS)
        mn = ((mf * r) * g_ref[...]).astype(BF16)
        mn_ref[...] = mn
        kv_ref[...] = _dot(mn, w_ref[...]).astype(BF16)

    return pl.pallas_call(
        body, out_shape=(_sds((m, d), BF16), _sds((m, 2 * M_WIDTH), BF16)),
        compiler_params=_cp(), name="memkv_fwd")(mem, g, w_kv)


def _layer_norm_stats(v):
    mu = jnp.mean(v, axis=-1, keepdims=True)
    xc = v - mu
    rstd = lax.rsqrt(jnp.mean(xc * xc, axis=-1, keepdims=True) + EPS)
    return xc * rstd, rstd


def _spatial_mix(vlb, ws_ref, bsb_ref, tm):
    rows = []
    for ci in range(tm // CHUNK):
        cols = []
        for g in range(B_GROUPS):
            blk = vlb[ci * CHUNK:(ci + 1) * CHUNK, g * B_GROUP_DIM:(g + 1) * B_GROUP_DIM]
            cols.append(_dot(ws_ref[g], blk) + bsb_ref[g])
        rows.append(jnp.concatenate(cols, axis=1))
    return jnp.concatenate(rows, axis=0)


def _mem_attn(qm, kv_ref):
    out = []
    for h in range(M_HEADS):
        qh = qm[:, h * M_HEAD_DIM:(h + 1) * M_HEAD_DIM].astype(BF16)
        kh = kv_ref[:, h * M_HEAD_DIM:(h + 1) * M_HEAD_DIM]
        vh = kv_ref[:, M_WIDTH + h * M_HEAD_DIM:M_WIDTH + (h + 1) * M_HEAD_DIM]
        sc = _dot_nt(qh, kh) * (M_HEAD_DIM ** -0.5)
        e = jnp.exp(sc - jnp.max(sc, axis=-1, keepdims=True))
        p = e / jnp.sum(e, axis=-1, keepdims=True)
        out.append((p, _dot(p.astype(BF16), vh)))
    return out


def branch_fwd(x, proj, o_a, kv, ws, bsb, ln_g, ln_b, w_br, w_out, next_g, tgt=None):
    s, d = x.shape
    tm = min(s, 512)
    last = tgt is not None

    def body(x_ref, p_ref, oa_ref, kv_ref, ws_ref, bsb_ref, lg_ref, lb_ref, wbr_ref, wo_ref, ng_ref, *rest):
        y_ref, up_ref, mg_ref = rest[-5:-2] if not last else rest[-6:-3]
        seg = lambda o, w: p_ref[:, o:o + w].astype(F32)
        z_a, u_b, v_b, z_b = seg(O_ZA, A_WIDTH), seg(O_UB, B_WIDTH), seg(O_VB, B_WIDTH), seg(O_ZB, B_WIDTH)
        q_m, z_m = seg(O_QM, M_WIDTH), seg(O_ZM, M_WIDTH)
        xhat, _ = _layer_norm_stats(v_b)
        vln = xhat * lg_ref[...] + lb_ref[...]
        mixed = _spatial_mix(vln.astype(BF16), ws_ref, bsb_ref, tm)
        y_b = (u_b * mixed) * (z_b * _sig(z_b))
        o_m = jnp.concatenate([o for _, o in _mem_attn(q_m, kv_ref)], axis=1)
        y_a = oa_ref[...] * (z_a * _sig(z_a))
        y_m = o_m * (z_m * _sig(z_m))
        merged = None
        for n, yy in enumerate((y_a, y_b, y_m)):
            yb = yy.astype(BF16)
            y_ref[n] = yb
            up = jnp.concatenate([_dot(yb, wbr_ref[c, n]) for c in range(N_CHIPS)], axis=1)
            up_ref[n] = up.astype(BF16)
            t = _sig(seg(O_LG + n * d, d)) * up
            merged = t if merged is None else merged + t
        mb = merged.astype(BF16)
        mg_ref[...] = mb
        xn = x_ref[...] + _dot(mb, wo_ref[...])
        r = lax.rsqrt(jnp.mean(xn * xn, axis=-1, keepdims=True) + EPS)
        xh = xn * r
        g = ng_ref[...]
        if not last:
            xn_ref, hn_ref = rest[-2:]
            xn_ref[...] = xn
            hn_ref[...] = (xh * g).astype(BF16)
        else:
            t_ref, (ls_ref, dx_ref, gg_ref) = rest[0], rest[-3:]

            @pl.when(pl.program_id(0) == 0)
            def _():
                ls_ref[...] = jnp.zeros_like(ls_ref)
                gg_ref[...] = jnp.zeros_like(gg_ref)

            e = xh * g - t_ref[...]
            sq = jnp.sum(jnp.sum(e * e, axis=0, keepdims=True), axis=1, keepdims=True)
            ls_ref[...] += jnp.broadcast_to(sq, ls_ref.shape)
            dy = e * (1.0 / d)
            gg_ref[...] += jnp.sum(dy * xh, axis=0, keepdims=True)
            gy = dy * g
            dx_ref[...] = r * (gy - xh * jnp.mean(gy * xh, axis=-1, keepdims=True))

    saved_shapes = (_sds((N_BRANCH, s, A_WIDTH), BF16), _sds((N_BRANCH, s, d), BF16), _sds((s, d), BF16))
    saved_specs = (pl.BlockSpec((N_BRANCH, tm, A_WIDTH), lambda i: (0, i, 0)), pl.BlockSpec((N_BRANCH, tm, d), lambda i: (0, i, 0)),
                   _rows(tm, d))
    if last:
        tail_shapes, tail_specs = (_sds((1, LANES), F32), _sds((s, d), F32), _sds((1, d), F32)), (_full((1, LANES)), _rows(tm, d), _full((1, d)))
    else:
        tail_shapes, tail_specs = (_sds((s, d), F32), _sds((s, d), BF16)), (_rows(tm, d), _rows(tm, d))
    return pl.pallas_call(
        body, out_shape=saved_shapes + tail_shapes, grid=(s // tm,),
        in_specs=[_rows(tm, d), _rows(tm, IN_WIDTH), _rows(tm, A_WIDTH), _full(kv.shape), _full(ws.shape), _full(bsb.shape),
                  _full((1, B_WIDTH)), _full((1, B_WIDTH)), _full(w_br.shape), _full(w_out.shape), _full((1, d))]
        + ([_rows(tm, d)] if last else []),
        out_specs=saved_specs + tail_specs,
        compiler_params=_cp("arbitrary" if last else "parallel"), name="branch_fwd_loss" if last else "branch_fwd")(
            x, proj, o_a, kv, ws, bsb, ln_g, ln_b, w_br, w_out, next_g, *([tgt] if last else []))


def _pblocks(tm, first, count):
    return [pl.BlockSpec((tm, PBLK), functools.partial(lambda i, b: (i, b), b=first + k)) for k in range(count)]


def merge_bwd(dx, proj, y, up, merged, w_br, w_out):
    s, d = dx.shape
    tm = min(s, 512)
    nlg = LG_W // PBLK
    cw = d // N_CHIPS

    def body(dx_ref, l0, l1, l2, l3, y_ref, up_ref, mg_ref, wbr_ref, wo_ref, dy_ref, dlg_ref, gwo_ref, gwb_ref, gwo16_ref, gwb16_ref):
        @pl.when(pl.program_id(0) == 0)
        def _():
            gwo_ref[...] = jnp.zeros_like(gwo_ref)
            gwb_ref[...] = jnp.zeros_like(gwb_ref)

        dxb = dx_ref[...].astype(BF16)
        dmg = _dot_nt(dxb, wo_ref[...])
        gwo_ref[...] += _dot_tn(mg_ref[...], dxb)
        lg = jnp.concatenate([l0[...], l1[...], l2[...], l3[...]], axis=1).astype(F32)
        for n in range(N_BRANCH):
            g = _sig(lg[:, n * d:(n + 1) * d])
            dup = dmg * g
            dlg_ref[:, n * d:(n + 1) * d] = ((dup * up_ref[n].astype(F32)) * (1.0 - g)).astype(BF16)
            dupb = dup.astype(BF16)
            dyn = None
            for c in range(N_CHIPS):
                blk = dupb[:, c * cw:(c + 1) * cw]
                gwb_ref[c, n] += _dot_tn(y_ref[n], blk)
                t = _dot_nt(blk, wbr_ref[c, n])
                dyn = t if dyn is None else dyn + t
            dy_ref[n] = dyn.astype(BF16)

        @pl.when(pl.program_id(0) == pl.num_programs(0) - 1)
        def _():
            gwo16_ref[...] = gwo_ref[...].astype(BF16)
            gwb16_ref[...] = gwb_ref[...].astype(BF16)

    return pl.pallas_call(
        body,
        out_shape=(_sds((N_BRANCH, s, A_WIDTH), BF16), _sds((s, LG_W), BF16), _sds((d, d), F32), _sds(w_br.shape, F32),
                   _sds((d, d), BF16), _sds(w_br.shape, BF16)),
        grid=(s // tm,),
        in_specs=[_rows(tm, d)] + _pblocks(tm, O_LG // PBLK, nlg) + [
            pl.BlockSpec((N_BRANCH, tm, A_WIDTH), lambda i: (0, i, 0)), pl.BlockSpec((N_BRANCH, tm, d), lambda i: (0, i, 0)),
            _rows(tm, d), _full(w_br.shape, once=True), _full(w_out.shape, once=True)],
        out_specs=(pl.BlockSpec((N_BRANCH, tm, A_WIDTH), lambda i: (0, i, 0)), _rows(tm, LG_W), _full((d, d)), _full(w_br.shape),
                   _full((d, d)), _full(w_br.shape)),
        compiler_params=_cp("arbitrary"), name="merge_bwd")(dx, proj, proj, proj, proj, y, up, merged, w_br, w_out)


def _dsilu(z, sg):
    return sg * (1.0 + z * (1.0 - sg))


def branch_bwd(dy, proj, o_a, kv, ws, ws_t, bsb, ln_g, ln_b, head_sel):
    s = proj.shape[0]
    tm = min(s, 512)
    nmid = MID_W // PBLK

    def body(dy_ref, m0, m1, m2, m3, oa_ref, kv_ref, ws_ref, wst_ref, bsb_ref, lg_ref, lb_ref, sel_ref,
             dmid_ref, dot_ref, dl_ref, gws_ref, gbs_ref, glg_ref, glb_ref, dkv_ref):
        @pl.when(pl.program_id(0) == 0)
        def _():
            for r in (gws_ref, gbs_ref, glg_ref, glb_ref, dkv_ref):
                r[...] = jnp.zeros_like(r)

        mid = jnp.concatenate([m0[...], m1[...], m2[...], m3[...]], axis=1).astype(F32)
        seg = lambda o, w: mid[:, o - O_ZA:o - O_ZA + w]
        z_a, u_b, v_b, z_b = seg(O_ZA, A_WIDTH), seg(O_UB, B_WIDTH), seg(O_VB, B_WIDTH), seg(O_ZB, B_WIDTH)
        q_m, z_m = seg(O_QM, M_WIDTH), seg(O_ZM, M_WIDTH)

        def put(o, v):
            dmid_ref[:, o - O_ZA:o - O_ZA + v.shape[1]] = v.astype(BF16)

        dy_a, dy_b, dy_m = dy_ref[0].astype(F32), dy_ref[1].astype(F32), dy_ref[2].astype(F32)

        o_a_ = oa_ref[...]
        sg = _sig(z_a)
        do_a = dy_a * (z_a * sg)
        put(O_ZA, (dy_a * o_a_) * _dsilu(z_a, sg))
        do_l = do_a * LN2
        dot_ref[...] = do_l.T.astype(BF16)
        dl_ref[...] = _dot_nt_hi(sel_ref[...], do_l * o_a_)

        xhat, rstd = _layer_norm_stats(v_b)
        lng = lg_ref[...]
        vln = xhat * lng + lb_ref[...]
        vlb = vln.astype(BF16)
        mixed = _spatial_mix(vlb, ws_ref, bsb_ref, tm)
        sg = _sig(z_b)
        sl = z_b * sg
        put(O_UB, (dy_b * mixed) * sl)
        put(O_ZB, ((dy_b * u_b) * mixed) * _dsilu(z_b, sg))
        dmix = (dy_b * u_b) * sl
        dmb = dmix.astype(BF16)
        rows = []
        for ci in range(tm // CHUNK):
            cols = []
            for g in range(B_GROUPS):
                rs, cs = slice(ci * CHUNK, (ci + 1) * CHUNK), slice(g * B_GROUP_DIM, (g + 1) * B_GROUP_DIM)
                gws_ref[g] += _dot_nt(dmb[rs, cs], vlb[rs, cs])
                gbs_ref[g] += jnp.broadcast_to(jnp.sum(dmix[rs, cs], axis=1, keepdims=True), (CHUNK, B_GROUP_DIM))
                cols.append(_dot(wst_ref[g], dmb[rs, cs]))
            rows.append(jnp.concatenate(cols, axis=1))
        dvln = jnp.concatenate(rows, axis=0)
        glg_ref[...] += jnp.sum(dvln * xhat, axis=0, keepdims=True)
        glb_ref[...] += jnp.sum(dvln, axis=0, keepdims=True)
        gy = dvln * lng
        put(O_VB, rstd * ((gy - jnp.mean(gy, axis=-1, keepdims=True)) - xhat * jnp.mean(gy * xhat, axis=-1, keepdims=True)))

        sg = _sig(z_m)
        sl = z_m * sg
        heads = _mem_attn(q_m, kv_ref)
        o_m = jnp.concatenate([o for _, o in heads], axis=1)
        put(O_ZM, (dy_m * o_m) * _dsilu(z_m, sg))
        do_m = dy_m * sl
        dqs = []
        for h, (p, o_h) in enumerate(heads):
            hs = slice(h * M_HEAD_DIM, (h + 1) * M_HEAD_DIM)
            vs = slice(M_WIDTH + h * M_HEAD_DIM, M_WIDTH + (h + 1) * M_HEAD_DIM)
            do_h = do_m[:, hs]
            dob = do_h.astype(BF16)
            dp = _dot_nt(dob, kv_ref[:, vs])
            dsc = (p * (dp - jnp.sum(do_h * o_h, axis=-1, keepdims=True))) * (M_HEAD_DIM ** -0.5)
            dsb = dsc.astype(BF16)
            dqs.append(_dot(dsb, kv_ref[:, hs]))
            dkv_ref[:, hs] += _dot_tn(dsb, q_m[:, hs].astype(BF16))
            dkv_ref[:, vs] += _dot_tn(p.astype(BF16), dob)
        put(O_QM, jnp.concatenate(dqs, axis=1))

    return pl.pallas_call(
        body,
        out_shape=(_sds((s, MID_W), BF16), _sds((A_WIDTH, s), BF16), _sds((A_HEADS, s), F32), _sds(ws.shape, F32),
                   _sds(ws.shape, F32), _sds((1, B_WIDTH), F32), _sds((1, B_WIDTH), F32), _sds(kv.shape, F32)),
        grid=(s // tm,),
        in_specs=[pl.BlockSpec((N_BRANCH, tm, A_WIDTH), lambda i: (0, i, 0))] + _pblocks(tm, O_ZA // PBLK, nmid) + [
            _rows(tm, A_WIDTH), _full(kv.shape), _full(ws.shape), _full(ws.shape), _full(bsb.shape),
            _full((1, B_WIDTH)), _full((1, B_WIDTH)), _full(head_sel.shape)],
        out_specs=(_rows(tm, MID_W), pl.BlockSpec((A_WIDTH, tm), lambda i: (0, i)), pl.BlockSpec((A_HEADS, tm), lambda i: (0, i)),
                   _full(ws.shape), _full(ws.shape), _full((1, B_WIDTH)), _full((1, B_WIDTH)), _full(kv.shape)),
        compiler_params=_cp("arbitrary"), name="branch_bwd")(dy, proj, proj, proj, proj, o_a, kv, ws, ws_t, bsb, ln_g, ln_b, head_sel)


def attn_bwd(q_t, do_t, kr, kr_t, vb, lse, delta, scatter=()):
    s = kr.shape[0]
    tq = min(s, 256)
    kc = min(s, 512)
    nkc = s // kc
    nq = s // tq
    grp = A_HEADS // A_KV_HEADS
    ns = len(scatter)
    na = ns // 2

    def body(qt_ref, dot_ref, kr_ref, krt_ref, vb_ref, lse_ref, dl_ref, *rest):
        s_in, (dqt_ref, dk_ref, dv_ref), s_out = rest[:ns], rest[ns:ns + 3], rest[ns + 3:2 * ns + 3]
        qp_ref, dop_ref, dq_ref = rest[2 * ns + 3:2 * ns + 6]
        if ns:
            start, finish = scatter_stages([g.shape[1:] for g in scatter[:na]], s_in[:na], s_in[na:], s_out[:na], s_out[na:],
                                           *rest[2 * ns + 6:])
            pl.when(pl.program_id(0) == 0)(start)

        @pl.when(pl.program_id(0) == 0)
        def _():
            dk_ref[...] = jnp.zeros_like(dk_ref)
            dv_ref[...] = jnp.zeros_like(dv_ref)

        for h in range(A_HEADS):
            hs = slice(A_HEAD_DIM * h, A_HEAD_DIM * (h + 1))
            qp_ref[h] = _pad_head(qt_ref[hs, :], h // grp)
            dop_ref[h] = _pad_head(dot_ref[hs, :], h // grp)
        dq_ref[...] = jnp.zeros_like(dq_ref)

        def step(ci, carry):
            ks = pl.ds(pl.multiple_of(ci * kc, kc), kc)
            kblk, vblk, ktb = kr_ref[ks, :], vb_ref[ks, :], krt_ref[:, ks]
            dv_acc = jnp.zeros((kc, A_KV_WIDTH), F32)
            dk_acc = jnp.zeros((kc, A_KV_WIDTH), F32)
            scs = [_dot(kblk, qp_ref[h]) for h in range(A_HEADS)]
            dps = [_dot(vblk, dop_ref[h]) for h in range(A_HEADS)]
            for h in range(A_HEADS):
                qpad, dopad = qp_ref[h], dop_ref[h]
                p = jnp.exp2(scs[h] - lse_ref[h:h + 1, :])
                dsb = (p * (dps[h] - dl_ref[h:h + 1, :])).astype(BF16)
                dv_acc = dv_acc + _dot_nt(p.astype(BF16), dopad)
                dk_acc = dk_acc + _dot_nt(dsb, qpad)
                dq_ref[h] += _dot(ktb, dsb)
            dv_ref[ks, :] += dv_acc
            dk_ref[ks, :] += dk_acc
            return carry

        lax.fori_loop(0, nkc, step, 0)
        dqt_ref[...] = jnp.concatenate(
            [dq_ref[h][A_HEAD_DIM * (h // grp):A_HEAD_DIM * (h // grp + 1), :] for h in range(A_HEADS)], axis=0)
        if ns:
            pl.when(pl.program_id(0) == nq - 1)(finish)

    colq = pl.BlockSpec((A_WIDTH, tq), lambda i: (0, i))
    colh = pl.BlockSpec((A_HEADS, tq), lambda i: (0, i))
    out = pl.pallas_call(
        body,
        out_shape=(_sds((A_WIDTH, s), F32), _sds((s, A_KV_WIDTH), F32), _sds((s, A_KV_WIDTH), F32)) + scatter_out_shapes(scatter[:na]),
        grid=(nq,),
        in_specs=[colq, colq, _full((s, A_KV_WIDTH)), _full((A_KV_WIDTH, s)), _full((s, A_KV_WIDTH)), colh, colh] + [_ANY] * ns,
        out_specs=(colq, _full((s, A_KV_WIDTH)), _full((s, A_KV_WIDTH))) + (_ANY,) * ns,
        scratch_shapes=[pltpu.VMEM((A_HEADS, A_KV_WIDTH, tq), BF16), pltpu.VMEM((A_HEADS, A_KV_WIDTH, tq), BF16),
                        pltpu.VMEM((A_HEADS, A_KV_WIDTH, tq), F32)] + (scatter_sems(na) if ns else []),
        compiler_params=_cp("arbitrary"), name="attn_bwd_scatter" if ns else "attn_bwd")(
            q_t, do_t, kr, kr_t, vb, lse, delta, *scatter)
    return out[0], out[1], out[2], list(out[3:3 + na]), list(out[3 + na:])


def qk_prep_bwd(proj, dq_t, dkr, dvb, tabs, qg, kg, gq, gk, fold_q, fold_k):
    s = proj.shape[0]
    tm = min(s, 1024)
    c, sa, sb = tabs

    def head_norm_bwd(x, dn, gain, gones, fold):
        ms = _group_sum(x * x, gones) * (1.0 / A_HEAD_DIM)
        r = lax.rsqrt(ms + EPS)
        xh = x * r
        gg = _dot_hi(jnp.sum(dn * xh, axis=0, keepdims=True), fold)
        u = dn * gain
        mean_u = _group_sum(u * xh, gones) * (1.0 / A_HEAD_DIM)
        return r * (u - xh * mean_u), gg

    def body(p_ref, dqt_ref, dk_ref, dv_ref, c_ref, sa_ref, sb_ref, qg_ref, kg_ref, gq_ref, gk_ref, fq_ref, fk_ref,
             dqkv_ref, gqg_ref, gkg_ref):
        @pl.when(pl.program_id(0) == 0)
        def _():
            gqg_ref[...] = jnp.zeros_like(gqg_ref)
            gkg_ref[...] = jnp.zeros_like(gkg_ref)

        cc, ssa, ssb = c_ref[...], sa_ref[...], sb_ref[...]
        dqr = dqt_ref[...].T * Q_SCALE
        dqn = _rope_t(dqr, _tile4(cc), _tile4(ssa), _tile4(ssb))
        dxq, gq_ = head_norm_bwd(p_ref[:, O_QA:O_QA + A_WIDTH].astype(F32), dqn, qg_ref[...], gq_ref[...], fq_ref[...])
        dkn = _rope_t(dk_ref[...], cc, ssa, ssb)
        dxk, gk_ = head_norm_bwd(p_ref[:, O_KA:O_KA + A_KV_WIDTH].astype(F32), dkn, kg_ref[...], gk_ref[...], fk_ref[...])
        gqg_ref[...] += gq_
        gkg_ref[...] += gk_
        dqkv_ref[:, O_QA:O_QA + A_WIDTH] = dxq.astype(BF16)
        dqkv_ref[:, O_KA:O_KA + A_KV_WIDTH] = dxk.astype(BF16)
        dqkv_ref[:, O_VA:O_VA + A_KV_WIDTH] = (dv_ref[...] * (1.0 / LN2)).astype(BF16)

    tab = _rows(tm, LANES)
    return pl.pallas_call(
        body, out_shape=(_sds((s, PBLK), BF16), _sds((1, LANES), F32), _sds((1, LANES), F32)), grid=(s // tm,),
        in_specs=[_rows(tm, PBLK), pl.BlockSpec((A_WIDTH, tm), lambda i: (0, i)), _rows(tm, A_KV_WIDTH), _rows(tm, A_KV_WIDTH),
                  tab, tab, tab, _full((1, A_WIDTH)), _full((1, A_KV_WIDTH)), _full((A_WIDTH, A_WIDTH)),
                  _full((A_KV_WIDTH, A_KV_WIDTH)), _full((A_WIDTH, LANES)), _full((A_KV_WIDTH, LANES))],
        out_specs=(_rows(tm, PBLK), _full((1, LANES)), _full((1, LANES))),
        compiler_params=_cp("arbitrary"), name="qk_prep_bwd")(proj, dq_t, dkr, dvb, c, sa, sb, qg, kg, gq, gk, fold_q, fold_k)


def _pick_dproj(b, d0, d1, d2, use):
    first_lg = 1 + MID_W // PBLK

    @pl.when(b == 0)
    def _():
        use(d0[...])

    @pl.when(jnp.logical_and(b >= 1, b < first_lg))
    def _():
        use(d1[...])

    @pl.when(b >= first_lg)
    def _():
        use(d2[...])


def win_grad(d0, d1, d2, h):
    s, d = h.shape
    tk = min(s, 4096)
    nk = s // tk

    def body(d0_ref, d1_ref, d2_ref, h_ref, o_ref, o16_ref):
        @pl.when(pl.program_id(1) == 0)
        def _():
            o_ref[...] = jnp.zeros_like(o_ref)

        def use(blk):
            o_ref[...] += _dot_tn(blk, h_ref[...])

        _pick_dproj(pl.program_id(0), d0_ref, d1_ref, d2_ref, use)

        @pl.when(pl.program_id(1) == nk - 1)
        def _():
            o16_ref[...] = o_ref[...].astype(BF16)

    def spec(first, count):
        def imap(j, k):
            used = jnp.logical_and(j >= first, j < first + count)
            return (jnp.where(used, k, 0), jnp.clip(j - first, 0, count - 1))
        return pl.BlockSpec((tk, PBLK), imap)

    nm = MID_W // PBLK
    oblk = pl.BlockSpec((PBLK, d), lambda j, k: (j, 0))
    return pl.pallas_call(
        body, out_shape=(_sds((IN_WIDTH, d), F32), _sds((IN_WIDTH, d), BF16)), grid=(N_PBLK, nk),
        in_specs=[spec(0, 1), spec(1, nm), spec(1 + nm, LG_W // PBLK),
                  pl.BlockSpec((tk, d), lambda j, k: (k, 0), pipeline_mode=pl.Buffered(1) if nk == 1 else None)],
        out_specs=(oblk, oblk),
        compiler_params=_cp("parallel", "arbitrary"), name="win_grad")(d0, d1, d2, h)


def h_bwd(d0, d1, d2, w_t, x, dx_out, g, scatter=()):
    s, d = x.shape
    tm = min(s, 512)
    nt = s // tm
    ns = len(scatter)
    na = ns // 2

    def body(d0_ref, d1_ref, d2_ref, w_ref, x_ref, dxo_ref, g_ref, *rest):
        s_in, (dx_ref, gg_ref), s_out = rest[:ns], rest[ns:ns + 2], rest[ns + 2:2 * ns + 2]
        if ns:
            start, finish = scatter_stages([a.shape[1:] for a in scatter[:na]], s_in[:na], s_in[na:], s_out[:na], s_out[na:],
                                           *rest[2 * ns + 2:])
            pl.when(pl.program_id(0) == 0)(start)

        @pl.when(pl.program_id(0) == 0)
        def _():
            gg_ref[...] = jnp.zeros_like(gg_ref)

        dh = (_dot(d0_ref[...], w_ref[0:PBLK, :]) + _dot(d1_ref[...], w_ref[PBLK:PBLK + MID_W, :])
              + _dot(d2_ref[...], w_ref[PBLK + MID_W:, :]))
        xf = x_ref[...]
        r = lax.rsqrt(jnp.mean(xf * xf, axis=-1, keepdims=True) + EPS)
        xh = xf * r
        gg_ref[...] += jnp.sum(dh * xh, axis=0, keepdims=True)
        u = dh * g_ref[...]
        dx_ref[...] = dxo_ref[...] + r * (u - xh * jnp.mean(u * xh, axis=-1, keepdims=True))
        if ns:
            pl.when(pl.program_id(0) == nt - 1)(finish)

    rowb = _rows(tm, d)
    out = pl.pallas_call(
        body, out_shape=(_sds((s, d), F32), _sds((1, d), F32)) + scatter_out_shapes(scatter[:na]), grid=(nt,),
        in_specs=[_rows(tm, PBLK), _rows(tm, MID_W), _rows(tm, LG_W),
                  pl.BlockSpec(w_t.shape, lambda i: (0, 0), pipeline_mode=pl.Buffered(1)), rowb, rowb, _full((1, d))] + [_ANY] * ns,
        out_specs=(rowb, _full((1, d))) + (_ANY,) * ns,
        scratch_shapes=scatter_sems(na) if ns else [],
        compiler_params=_cp("arbitrary"), name="h_bwd_scatter" if ns else "h_bwd")(d0, d1, d2, w_t, x, dx_out, g, *scatter)
    return out[0], out[1], list(out[2:2 + na]), list(out[2 + na:])


def memkv_bwd(mem, g, mem_n, w_kv, dkv):
    m, d = mem.shape

    def body(mem_ref, g_ref, mn_ref, w_ref, dkv_ref, gw_ref, gw16_ref, gg_ref):
        dkb = dkv_ref[...].astype(BF16)
        gw = _dot_tn(mn_ref[...], dkb)
        gw_ref[...] = gw
        gw16_ref[...] = gw.astype(BF16)
        dmn = _dot_nt(dkb, w_ref[...])
        mf = mem_ref[...]
        r = lax.rsqrt(jnp.mean(mf * mf, axis=-1, keepdims=True) + EPS)
        gg_ref[...] = jnp.sum(dmn * (mf * r), axis=0, keepdims=True)

    return pl.pallas_call(
        body, out_shape=(_sds(w_kv.shape, F32), _sds(w_kv.shape, BF16), _sds((1, d), F32)),
        compiler_params=_cp(), name="memkv_bwd")(mem, g, mem_n, w_kv, dkv)


def _layer_consts(seq):
    i = jnp.arange(A_WIDTH)
    return dict(
        tabs=rope_tables(seq),
        gq=_group_ones(A_WIDTH, A_HEAD_DIM).astype(BF16), gk=_group_ones(A_KV_WIDTH, A_HEAD_DIM).astype(BF16),
        fold_q=(i[:, None] % A_HEAD_DIM == jnp.arange(LANES)[None, :]).astype(F32),
        fold_k=(i[:A_KV_WIDTH, None] % A_HEAD_DIM == jnp.arange(LANES)[None, :]).astype(F32),
        head_sel=(jnp.arange(A_HEADS)[:, None] == i[None, :] // A_HEAD_DIM).astype(F32),
    )


_BIG = ("win_t", "wkv", "wbr", "wout")


def _with_own_part(names, gathered, shards, chip, d):
    shape = dict(win_t=(IN_WIDTH, d), wkv=(d, 2 * M_WIDTH), wbr=(N_CHIPS, N_BRANCH, A_WIDTH, d // N_CHIPS), wout=(d, d))
    return {n: lax.dynamic_update_slice(g, sh[None], (chip, 0, 0)).reshape(shape[n]) for n, g, sh in zip(names, gathered, shards)}


def local_fwd_bwd(x, mem, tgt, small, big=None, shards=None, place=None):
    s, d = x.shape
    depth = small["norm_g"].shape[0]
    k = _layer_consts(s)
    row = lambda v: v.reshape(1, -1)
    dist = shards is not None
    if dist:
        big = [None] * depth
    saved = []
    for l in range(depth):
        ng = row(small["norm_g"][l])
        qg = row(jnp.tile(small["q_norm_g"][l], A_HEADS))
        kg = row(jnp.tile(small["k_norm_g"][l], A_KV_HEADS))
        ws = small["w_s"][l].astype(BF16)
        ws_t = jnp.swapaxes(small["w_s"][l], 1, 2).astype(BF16)
        bsb = jnp.broadcast_to(small["b_s"][l][:, :, None], (B_GROUPS, CHUNK, B_GROUP_DIM))
        lng, lnb = row(small["sg_ln_g"][l]), row(small["sg_ln_b"][l])
        mg = row(small["mem_norm_g"][l])
        if l == 0:
            first = tuple(shards[0][:1]) if dist else ()
            h, gathered = rms_fwd(x, ng, gather=first)
            if dist:
                big[0] = _with_own_part(_BIG[:1], gathered, first, place[0], d)
        else:
            h = h_next
        w = big[l]
        late = tuple(shards[0][1:]) if dist and l == 0 else ()
        proj, gathered = proj_fwd(h, w["win_t"], gather=late)
        if late:
            w.update(_with_own_part(_BIG[1:], gathered, late, place[0], d))
        q_t, kr, vb, kr_t, vte0, vte1 = qk_prep(proj, k["tabs"], qg, kg, k["gq"], k["gk"])
        nxt = tuple(shards[l + 1]) if dist and l + 1 < depth else ()
        o_a, lse, gathered = attn_fwd(q_t, kr, vte0, vte1, gather=nxt)
        if nxt:
            big[l + 1] = _with_own_part(_BIG, gathered, nxt, place[0], d)
        mem_n, kv = memkv_fwd(mem, mg, w["wkv"])
        x_in = x
        if l + 1 < depth:
            y, up, merged, x, h_next = branch_fwd(x, proj, o_a, kv, ws, bsb, lng, lnb, w["wbr"], w["wout"], row(small["norm_g"][l + 1]))
        else:
            y, up, merged, sq, dx, g_final = branch_fwd(x, proj, o_a, kv, ws, bsb, lng, lnb, w["wbr"], w["wout"],
                                                        row(small["final_g"]), tgt=tgt)
        saved.append(dict(x=x_in, ng=ng, qg=qg, kg=kg, ws=ws, ws_t=ws_t, bsb=bsb, lng=lng, lnb=lnb, mg=mg, h=h, proj=proj,
                          q_t=q_t, kr=kr, kr_t=kr_t, vb=vb, o_a=o_a, lse=lse, mem_n=mem_n, kv=kv, y=y, up=up, merged=merged))

    grads = {n: [None] * depth for n in ("norm_g", "q_norm_g", "k_norm_g", "sg_ln_g", "sg_ln_b", "w_s", "b_s", "mem_norm_g")}
    parts = lambda g: g.reshape(N_CHIPS, -1, g.shape[-1])
    reduced = [[None] * len(_BIG) for _ in range(depth)]

    def reduce_all(items, t_sib, t_rem):
        if items:
            for (ll, a, _, _), f in zip(items, reduce_rows(place, [i[2] for i in items], t_sib, t_rem)):
                reduced[ll][a] = f

    as_scatter = lambda items: tuple(i[2] for i in items) + tuple(i[3] for i in items)
    pending = []
    for l in reversed(range(depth)):
        sv, w = saved[l], big[l]
        dy, dlg, g_wout, g_wbr, g_wout16, g_wbr16 = merge_bwd(dx, sv["proj"], sv["y"], sv["up"], sv["merged"], w["wbr"], w["wout"])
        dmid, do_t, delta, g_ws, g_bs, g_lng, g_lnb, dkv = branch_bwd(
            dy, sv["proj"], sv["o_a"], sv["kv"], sv["ws"], sv["ws_t"], sv["bsb"], sv["lng"], sv["lnb"], k["head_sel"])
        g_wkv, g_wkv16, g_mg = memkv_bwd(mem, sv["mg"], sv["mem_n"], w["wkv"], dkv)
        if dist:
            pending += [(l, 1, parts(g_wkv), parts(g_wkv16)), (l, 2, parts(g_wbr), parts(g_wbr16)), (l, 3, parts(g_wout), parts(g_wout16))]
        dq_t, dkr, dvb, t_sib, t_rem = attn_bwd(sv["q_t"], do_t, sv["kr"], sv["kr_t"], sv["vb"], sv["lse"], delta,
                                                scatter=as_scatter(pending))
        reduce_all(pending, t_sib, t_rem)
        dqkv, g_qg, g_kg = qk_prep_bwd(sv["proj"], dq_t, dkr, dvb, k["tabs"], sv["qg"], sv["kg"], k["gq"], k["gk"],
                                       k["fold_q"], k["fold_k"])
        g_win, g_win16 = win_grad(dqkv, dmid, dlg, sv["h"])
        pending = [(l, 0, parts(g_win), parts(g_win16))] if dist else []
        last = as_scatter(pending) if l == 0 else ()
        dx, g_ng, t_sib, t_rem = h_bwd(dqkv, dmid, dlg, w["win_t"], sv["x"], dx, sv["ng"], scatter=last)
        if last:
            reduce_all(pending, t_sib, t_rem)
        grads["norm_g"][l] = g_ng[0]
        grads["q_norm_g"][l] = g_qg[0, :A_HEAD_DIM]
        grads["k_norm_g"][l] = g_kg[0, :A_HEAD_DIM]
        grads["sg_ln_g"][l] = g_lng[0]
        grads["sg_ln_b"][l] = g_lnb[0]
        grads["w_s"][l] = g_ws
        grads["b_s"][l] = g_bs[:, :, 0]
        grads["mem_norm_g"][l] = g_mg[0]
        if not dist:
            reduced[l] = dict(zip(_BIG, (parts(g_win), parts(g_wkv), parts(g_wbr), parts(g_wout))))
    grads = {n: jnp.stack(v) for n, v in grads.items()}
    grads["final_g"] = g_final[0]
    return sq[0, 0], dx, grads, reduced


def _row_block(rows, width, cap_bytes=2 * 2**20):
    best = None
    for br in range(8, rows + 1, 8):
        if rows % br == 0 and br * width * 4 <= cap_bytes:
            best = br
    return best if best is not None else rows


def adamw(w, gs, m, v):
    r, c = w.shape
    n = len(gs)
    rs = r // n
    br = _row_block(rs, c)
    nb = rs // br

    def body(w_ref, *refs):
        g_refs, (m_ref, v_ref, og_ref, d_ref, nm_ref, nv_ref) = refs[:n], refs[n:]

        def update(gg):
            mm = ADAM_B1 * m_ref[...] + (1.0 - ADAM_B1) * gg
            vv = ADAM_B2 * v_ref[...] + (1.0 - ADAM_B2) * (gg * gg)
            m_hat = mm / (1.0 - ADAM_B1 ** ADAM_STEP)
            v_hat = vv / (1.0 - ADAM_B2 ** ADAM_STEP)
            og_ref[...] = gg
            d_ref[...] = -ADAM_LR * (m_hat / (jnp.sqrt(v_hat) + ADAM_EPS) + ADAM_WD * w_ref[...])
            nm_ref[...] = mm
            nv_ref[...] = vv

        for k in range(n):
            pl.when(pl.program_id(0) == k)(functools.partial(lambda k: update(g_refs[k][...]), k))

    blk = pl.BlockSpec((br, c), lambda l, i: (l * nb + i, 0))
    g_specs = [pl.BlockSpec((br, c), functools.partial(lambda l, i, k: (jnp.where(l == k, i, 0), 0), k=k)) for k in range(n)]
    return pl.pallas_call(
        body, out_shape=(_sds((r, c), F32),) * 4, grid=(n, nb), in_specs=[blk] + g_specs + [blk, blk], out_specs=(blk,) * 4,
        compiler_params=_cp("arbitrary", "arbitrary"), name="adamw")(w, *gs, m, v)


N_REMOTE = 2 * (N_CHIPS - 1)


def reduce_rows(place, gs, t_sibs, t_rems):
    n = len(gs)
    nt = 2

    def body(place_ref, *refs):
        for a in range(n):
            g_ref, s_ref, t_ref, f_ref = refs[a], refs[n + a], refs[2 * n + a], refs[3 * n + a]
            acc = g_ref[...] + s_ref[...]
            for j in range(N_REMOTE):
                acc = acc + t_ref[j].astype(F32)
            f_ref[...] = acc

    tiles = [(g.shape[1] // 2 // nt, g.shape[2]) for g in gs]
    return pl.pallas_call(
        body, out_shape=tuple(_sds(g.shape[1:], F32) for g in gs),
        grid_spec=pltpu.PrefetchScalarGridSpec(
            num_scalar_prefetch=1, grid=(nt,),
            in_specs=[pl.BlockSpec((None, tr, c), lambda i, p: (p[0], p[1] * nt + i, 0)) for tr, c in tiles]
            + [pl.BlockSpec((tr, c), lambda i, p: (i, 0)) for tr, c in tiles]
            + [pl.BlockSpec((N_REMOTE, tr, c), lambda i, p: (0, i, 0)) for tr, c in tiles],
            out_specs=tuple(pl.BlockSpec((tr, c), lambda i, p: (p[1] * nt + i, 0)) for tr, c in tiles)),
        compiler_params=_cp("parallel"), name="reduce_rows")(place, *gs, *t_sibs, *t_rems)


_ANY = pl.BlockSpec(memory_space=pl.ANY)


def _place():
    x, y, c = lax.axis_index("x"), lax.axis_index("y"), lax.axis_index("c")
    chips = [(1 - x, y), (x, 1 - y), (1 - x, 1 - y)]
    return x, y, c, chips


def gather_sems(n):
    return [pltpu.SemaphoreType.DMA((n, N_REMOTE)), pltpu.SemaphoreType.DMA((n, N_REMOTE))]


def gather_stages(shapes, ins, outs, send, recv):
    n = len(shapes)
    x, y, c, chips = _place()
    me = 2 * x + y
    sib = (x, y, 1 - c)

    def rows(a, hl):
        r2 = shapes[a][0] // 2
        return pl.ds(hl * r2, r2)

    def remote(a, k, src, dst, dev):
        return pltpu.make_async_remote_copy(src, dst, send.at[a, k], recv.at[a, k], device_id=dev, device_id_type=MESH)

    def sent(a, k):
        cx, cy = chips[k]
        return remote(a, k, ins[a].at[rows(a, c)], outs[a].at[me, rows(a, c)], (cx, cy, c))

    def got(a, k, hl):
        cx, cy = chips[k]
        return outs[a].at[2 * cx + cy, rows(a, hl)]

    def arrived(a, k):
        return remote(a, k, got(a, k, c), got(a, k, c), (*chips[k], c))

    def passed(a, k, hl):
        return remote(a, 3 + k, got(a, k, hl), got(a, k, hl), sib)

    def start():
        for a in range(n):
            for k in range(3):
                sent(a, k).start()

    def forward():
        for k in range(3):
            for a in range(n):
                arrived(a, k).wait_recv()
                passed(a, k, c).start()

    def finish():
        for k in range(3):
            for a in range(n):
                passed(a, k, 1 - c).wait_recv()
        for k in range(3):
            for a in range(n):
                sent(a, k).wait_send()
                passed(a, k, c).wait_send()

    return start, forward, finish


def scatter_sems(n):
    return [pltpu.SemaphoreType.DMA((n, N_REMOTE + 1)), pltpu.SemaphoreType.DMA((n, N_REMOTE + 1))]


def scatter_out_shapes(gs):
    return (tuple(_sds((g.shape[1] // 2, g.shape[2]), F32) for g in gs)
            + tuple(_sds((N_REMOTE, g.shape[1] // 2, g.shape[2]), BF16) for g in gs))


def scatter_stages(shapes, gf, gb, t_sib, t_rem, send, recv):
    n = len(shapes)
    x, y, c, chips = _place()
    me = 2 * x + y

    def copies():
        out = []
        for a in range(n):
            r2 = shapes[a][0] // 2
            out.append(pltpu.make_async_remote_copy(gf[a].at[me, pl.ds((1 - c) * r2, r2)], t_sib[a], send.at[a, N_REMOTE],
                                                    recv.at[a, N_REMOTE], device_id=(x, y, 1 - c), device_id_type=MESH))
            for k, (cx, cy) in enumerate(chips):
                for o in range(2):
                    tc = c if o == 0 else 1 - c
                    out.append(pltpu.make_async_remote_copy(gb[a].at[2 * cx + cy, pl.ds(tc * r2, r2)], t_rem[a].at[2 * k + o],
                                                            send.at[a, 2 * k + o], recv.at[a, 2 * k + o],
                                                            device_id=(cx, cy, tc), device_id_type=MESH))
        return out

    def start():
        for cp in copies():
            cp.start()

    def finish():
        for cp in copies():
            cp.wait()

    return start, finish


def finish_exchange(v, fs):
    n = len(fs)
    r, w = v.shape
    ndev = 2 * N_CHIPS

    def body(v_ref, *refs):
        out, sum_ref = refs[n:2 * n], refs[2 * n]
        all_ref, send, recv, loc, fsend, frecv = refs[2 * n + 1:]
        x, y, c, chips = _place()
        me, sib = (x, y, c), (x, y, 1 - c)
        swaps = []
        for a in range(n):
            r2 = fs[a].shape[0] // 2
            half = out[a].at[pl.ds(c * r2, r2)]
            cp = pltpu.make_async_remote_copy(half, half, fsend.at[a], frecv.at[a], device_id=sib, device_id_type=MESH)
            cp.start()
            swaps.append(cp)

        def slab(px, py, pc):
            return all_ref.at[4 * px + 2 * py + pc]

        def copy(k, block, to, src=None):
            return pltpu.make_async_remote_copy(slab(*block) if src is None else src, slab(*block), send.at[k], recv.at[k],
                                                device_id=to, device_id_type=MESH)

        mine = pltpu.make_async_copy(v_ref, slab(*me), loc)
        mine.start()
        first = [copy(0, me, sib, src=v_ref)] + [copy(1 + j, me, (*chip, c), src=v_ref) for j, chip in enumerate(chips)]
        for cp in first:
            cp.start()
        passed = [copy(4 + j, (*chip, c), sib) for j, chip in enumerate(chips)]
        for j, chip in enumerate(chips):
            copy(1 + j, (*chip, c), me).wait_recv()
            passed[j].start()
        copy(0, sib, me).wait_recv()
        for j, chip in enumerate(chips):
            copy(4 + j, (*chip, 1 - c), me).wait_recv()
        for cp in first + passed:
            cp.wait_send()
        mine.wait()
        acc = all_ref[0]
        for i in range(1, ndev):
            acc = acc + all_ref[i]
        sum_ref[...] = acc
        for a, cp in enumerate(swaps):
            r2 = fs[a].shape[0] // 2
            theirs = out[a].at[pl.ds((1 - c) * r2, r2)]
            cp.wait_send()
            pltpu.make_async_remote_copy(theirs, theirs, fsend.at[a], frecv.at[a], device_id=sib, device_id_type=MESH).wait_recv()

    vm = pl.BlockSpec(memory_space=pltpu.VMEM)
    res = pl.pallas_call(
        body, out_shape=tuple(_sds(f.shape, F32) for f in fs) + (_sds((r, w), F32),),
        in_specs=[vm] + [_ANY] * n, out_specs=(_ANY,) * n + (vm,), input_output_aliases={a + 1: a for a in range(n)},
        scratch_shapes=[pltpu.VMEM((ndev, r, w), F32), pltpu.SemaphoreType.DMA((7,)), pltpu.SemaphoreType.DMA((7,)),
                        pltpu.SemaphoreType.DMA, pltpu.SemaphoreType.DMA((n,)), pltpu.SemaphoreType.DMA((n,))],
        compiler_params=pltpu.CompilerParams(vmem_limit_bytes=VMEM_LIMIT), name="finish_exchange")(v, *fs)
    return res[n], list(res[:n])


_SMALL = ("norm_g", "q_norm_g", "k_norm_g", "sg_ln_g", "sg_ln_b", "w_s", "b_s", "mem_norm_g", "final_g")
_WEIGHTS = ("norm_g", "w_in", "q_norm_g", "k_norm_g", "sg_ln_g", "sg_ln_b", "w_s", "b_s", "mem_norm_g", "w_mem_kv", "w_br",
            "w_out", "final_g")


def _pack(d, tail=None):
    flat = jnp.concatenate([d[n].reshape(-1) for n in _SMALL] + ([tail.reshape(1)] if tail is not None else []))
    rows = -(-(sum(d[n].size for n in _SMALL) + 1) // (8 * LANES)) * 8
    return jnp.pad(flat, (0, rows * LANES - flat.shape[0])).reshape(rows, LANES)


def _unpack(p, like):
    flat, out, o = p.reshape(-1), {}, 0
    for n in _SMALL:
        out[n] = flat[o:o + like[n].size].reshape(like[n].shape)
        o += like[n].size
    return out


def kernel(x, mem, norm_g, w_in, q_norm_g, k_norm_g, sg_ln_g, sg_ln_b, w_s, b_s, mem_norm_g, w_mem_kv, w_br, w_out, final_g, loss_target, m_norm_g, m_w_in, m_q_norm_g, m_k_norm_g, m_sg_ln_g, m_sg_ln_b, m_w_s, m_b_s, m_mem_norm_g, m_w_mem_kv, m_w_br, m_w_out, m_final_g, v_norm_g, v_w_in, v_q_norm_g, v_k_norm_g, v_sg_ln_g, v_sg_ln_b, v_w_s, v_b_s, v_mem_norm_g, v_w_mem_kv, v_w_br, v_w_out, v_final_g):
    w = dict(norm_g=norm_g, w_in=w_in, q_norm_g=q_norm_g, k_norm_g=k_norm_g, sg_ln_g=sg_ln_g, sg_ln_b=sg_ln_b, w_s=w_s, b_s=b_s,
             mem_norm_g=mem_norm_g, w_mem_kv=w_mem_kv, w_br=w_br, w_out=w_out, final_g=final_g)
    m = dict(norm_g=m_norm_g, w_in=m_w_in, q_norm_g=m_q_norm_g, k_norm_g=m_k_norm_g, sg_ln_g=m_sg_ln_g, sg_ln_b=m_sg_ln_b,
             w_s=m_w_s, b_s=m_b_s, mem_norm_g=m_mem_norm_g, w_mem_kv=m_w_mem_kv, w_br=m_w_br, w_out=m_w_out, final_g=m_final_g)
    v = dict(norm_g=v_norm_g, w_in=v_w_in, q_norm_g=v_q_norm_g, k_norm_g=v_k_norm_g, sg_ln_g=v_sg_ln_g, sg_ln_b=v_sg_ln_b,
             w_s=v_w_s, b_s=v_b_s, mem_norm_g=v_mem_norm_g, w_mem_kv=v_w_mem_kv, w_br=v_w_br, w_out=v_w_out, final_g=v_final_g)
    depth, d = norm_g.shape
    nsh = N_CHIPS
    br_rows = N_BRANCH * A_WIDTH
    br_cols = d // nsh

    shards = [[jnp.swapaxes(w_in[l], 0, 1).astype(BF16), w_mem_kv[l].astype(BF16), w_br[l].astype(BF16).reshape(br_rows, br_cols),
               w_out[l].astype(BF16)] for l in range(depth)]
    place = jnp.stack([2 * lax.axis_index("x") + lax.axis_index("y"), lax.axis_index("c")]).astype(jnp.int32)
    small = {n: w[n] for n in _SMALL}

    sq, dx, grads, reduced = local_fwd_bwd(x[0], mem[0], loss_target[0], small, shards=shards, place=place)

    small_sum, finals = finish_exchange(_pack(grads, tail=sq), [g for layer in reduced for g in layer])
    loss = (0.5 / d) * small_sum.reshape(-1)[sum(small[n].size for n in _SMALL)]
    big_grads = dict(zip(("w_in", "w_mem_kv", "w_br", "w_out"), [finals[a::len(_BIG)] for a in range(len(_BIG))]))
    small_grads = _unpack(small_sum, small)

    out_g, out_d, out_m, out_v = {}, {}, {}, {}
    _, sd, sm, sv = adamw(_pack(small), [small_sum], _pack({n: m[n] for n in _SMALL}), _pack({n: v[n] for n in _SMALL}))
    sd, sm, sv = _unpack(sd, small), _unpack(sm, small), _unpack(sv, small)
    for n in _SMALL:
        out_g[n], out_d[n], out_m[n], out_v[n] = small_grads[n], sd[n], sm[n], sv[n]
    for n, gs in big_grads.items():
        into = (lambda a: jnp.swapaxes(a, 1, 2)) if n == "w_in" else (lambda a: a)
        two_d = lambda a: a.reshape(-1, gs[0].shape[-1])
        res = adamw(two_d(into(w[n])), gs, two_d(into(m[n])), two_d(into(v[n])))
        out_g[n], out_d[n], out_m[n], out_v[n] = [into(t.reshape(into(w[n]).shape)) for t in res]
    return (loss, dx[None], *[out_g[n] for n in _WEIGHTS], *[out_d[n] for n in _WEIGHTS], *[out_m[n] for n in _WEIGHTS],
            *[out_v[n] for n in _WEIGHTS])
```

```python
import functools

import jax
import jax.numpy as jnp
from jax import lax
from jax.experimental import pallas as pl
from jax.experimental.pallas import tpu as pltpu

F32 = jnp.float32
BF16 = jnp.bfloat16

GRID_W = 64
CHUNK = 128
ROPE_THETA = 10000.0
EPS = 1e-6
A_HEADS, A_KV_HEADS, A_HEAD_DIM = 8, 2, 64
A_WIDTH, A_KV_WIDTH = 512, 128
B_GROUPS, B_GROUP_DIM, B_WIDTH = 4, 128, 512
M_HEADS, M_HEAD_DIM, M_WIDTH = 4, 128, 512
N_BRANCH = 3
IN_WIDTH = 6912
O_QA, O_KA, O_VA, O_ZA, O_UB, O_VB, O_ZB, O_QM, O_ZM, O_LG = 0, 512, 640, 768, 1280, 1792, 2304, 2816, 3328, 3840
PBLK = 768
N_PBLK = IN_WIDTH // PBLK
MID_W = 3072
LG_W = 3072

LN2 = 0.6931471805599453
Q_SCALE = A_HEAD_DIM ** -0.5 / LN2
VTE_ROWS = A_HEAD_DIM + 16

ADAM_LR, ADAM_B1, ADAM_B2, ADAM_EPS, ADAM_WD, ADAM_STEP = 0.001, 0.9, 0.999, 1e-08, 0.01, 10

V7X_VMEM_BYTES = 64 * 2**20
VMEM_LIMIT = V7X_VMEM_BYTES - 4 * 2**20
LANES = 128
MESH = pl.DeviceIdType.MESH
N_CHIPS = 4


def _cp(*sem):
    return pltpu.CompilerParams(dimension_semantics=sem if sem else None, vmem_limit_bytes=VMEM_LIMIT)


def _dot(a, b):
    return jnp.dot(a, b, preferred_element_type=F32)


def _dot_nt(a, b):
    return lax.dot_general(a, b, (((1,), (1,)), ((), ())), preferred_element_type=F32)


def _dot_tn(a, b):
    return lax.dot_general(a, b, (((0,), (0,)), ((), ())), preferred_element_type=F32)


def _dot_hi(a, b):
    return jnp.dot(a, b, preferred_element_type=F32, precision=lax.Precision.HIGHEST)


def _group_sum(a, ones):
    hi = a.astype(BF16)
    lo = (a - hi.astype(F32)).astype(BF16)
    return _dot(hi, ones) + _dot(lo, ones)


def _dot_nt_hi(a, b):
    return lax.dot_general(a, b, (((1,), (1,)), ((), ())), preferred_element_type=F32, precision=lax.Precision.HIGHEST)


def _sig(z):
    return 1.0 / (1.0 + jnp.exp(-z))


def _full(shape, once=False):
    nd = len(shape)
    return pl.BlockSpec(shape, lambda *_: (0,) * nd, pipeline_mode=pl.Buffered(1) if once else None)


def _rows(tm, width):
    return pl.BlockSpec((tm, width), lambda i: (i, 0))


def _sds(shape, dtype):
    return jax.ShapeDtypeStruct(shape, dtype)


def rms_fwd(x, g, gather=()):
    s, d = x.shape
    tm = min(s, 512)
    nt = s // tm
    ng = len(gather)

    def body(x_ref, g_ref, *rest):
        g_in, h_ref, g_out = rest[:ng], rest[ng], rest[ng + 1:2 * ng + 1]
        if ng:
            start, forward, finish = gather_stages([a.shape for a in gather], g_in, g_out, *rest[2 * ng + 1:])
            pl.when(pl.program_id(0) == 0)(start)
        xf = x_ref[...]
        r = lax.rsqrt(jnp.mean(xf * xf, axis=-1, keepdims=True) + EPS)
        h_ref[...] = ((xf * r) * g_ref[...]).astype(BF16)
        if ng:
            @pl.when(pl.program_id(0) == nt - 1)
            def _():
                forward()
                finish()

    out = pl.pallas_call(
        body, out_shape=(_sds((s, d), BF16),) + tuple(_sds((N_CHIPS,) + a.shape, a.dtype) for a in gather), grid=(nt,),
        in_specs=[_rows(tm, d), _full((1, d))] + [_ANY] * ng, out_specs=(_rows(tm, d),) + (_ANY,) * ng,
        scratch_shapes=gather_sems(ng) if ng else [],
        compiler_params=_cp("arbitrary"), name="rms_fwd_gather" if ng else "rms_fwd")(x, g, *gather)
    return out[0], list(out[1:])


def proj_fwd(h, w_t, gather=()):
    s, d = h.shape
    n = w_t.shape[0]
    tm = min(s, 1024)
    tn = 2304
    nj, ni = n // tn, s // tm
    ng = len(gather)

    def body(h_ref, w_ref, *rest):
        g_in, o_ref, g_out = rest[:ng], rest[ng], rest[ng + 1:2 * ng + 1]
        step = pl.program_id(0) * ni + pl.program_id(1)
        if ng:
            start, forward, finish = gather_stages([a.shape for a in gather], g_in, g_out, *rest[2 * ng + 1:])
            pl.when(step == 0)(start)
            pl.when(step == (3 * nj * ni) // 4)(forward)
        o_ref[...] = _dot_nt(h_ref[...], w_ref[...]).astype(BF16)
        if ng:
            pl.when(step == nj * ni - 1)(finish)

    out = pl.pallas_call(
        body, out_shape=(_sds((s, n), BF16),) + tuple(_sds((N_CHIPS,) + a.shape, a.dtype) for a in gather), grid=(nj, ni),
        in_specs=[pl.BlockSpec((tm, d), lambda j, i: (i, 0)), pl.BlockSpec((tn, d), lambda j, i: (j, 0))] + [_ANY] * ng,
        out_specs=(pl.BlockSpec((tm, tn), lambda j, i: (i, j)),) + (_ANY,) * ng,
        scratch_shapes=gather_sems(ng) if ng else [],
        compiler_params=_cp("arbitrary", "arbitrary") if ng else _cp("parallel", "parallel"),
        name="proj_fwd_gather" if ng else "proj_fwd")(h, w_t, *gather)
    return out[0], list(out[1:])


def rope_tables(seq):
    n_freq = A_HEAD_DIM // 4
    d = jnp.arange(LANES) % A_HEAD_DIM
    seg, half, freq = d // (2 * n_freq), (d % (2 * n_freq)) // n_freq, d % n_freq
    inv = ROPE_THETA ** (-freq.astype(F32) / n_freq)
    t = jnp.arange(seq)
    pos = jnp.where(seg[None, :] == 0, (t // GRID_W)[:, None], (t % GRID_W)[:, None]).astype(F32)
    ang = pos * inv[None, :]
    cos, sin = jnp.cos(ang), jnp.sin(ang)
    return cos, jnp.where(half[None, :] == 1, sin, 0.0), jnp.where(half[None, :] == 0, -sin, 0.0)


def _group_ones(width, group):
    i = jnp.arange(width)
    return (i[:, None] // group == i[None, :] // group).astype(F32)


def _rope(xn, c, sa, sb):
    w = xn.shape[1]
    return xn * c + pltpu.roll(xn, 16, 1) * sa + pltpu.roll(xn, w - 16, 1) * sb


def _rope_t(dy, c, sa, sb):
    w = dy.shape[1]
    return dy * c + pltpu.roll(dy * sa, w - 16, 1) + pltpu.roll(dy * sb, 16, 1)


def _tile4(t):
    return jnp.concatenate([t, t, t, t], axis=1)


def qk_prep(proj, tabs, qg, kg, gq, gk):
    s = proj.shape[0]
    tm = min(s, 1024)
    c, sa, sb = tabs

    def body(p_ref, c_ref, sa_ref, sb_ref, qg_ref, kg_ref, gq_ref, gk_ref, qt_ref, kr_ref, vb_ref, kt_ref, v0_ref, v1_ref):
        xq = p_ref[:, O_QA:O_QA + A_WIDTH].astype(F32)
        xk = p_ref[:, O_KA:O_KA + A_KV_WIDTH].astype(F32)
        xv = p_ref[:, O_VA:O_VA + A_KV_WIDTH].astype(F32)
        cc, ssa, ssb = c_ref[...], sa_ref[...], sb_ref[...]
        msq = _group_sum(xq * xq, gq_ref[...]) * (1.0 / A_HEAD_DIM)
        qn = (xq * lax.rsqrt(msq + EPS)) * qg_ref[...]
        qr = _rope(qn, _tile4(cc), _tile4(ssa), _tile4(ssb)) * Q_SCALE
        qt_ref[...] = qr.T.astype(BF16)
        msk = _group_sum(xk * xk, gk_ref[...]) * (1.0 / A_HEAD_DIM)
        kn = (xk * lax.rsqrt(msk + EPS)) * kg_ref[...]
        kr = _rope(kn, cc, ssa, ssb)
        kr_ref[...] = kr.astype(BF16)
        vb_ref[...] = xv.astype(BF16)
        kt_ref[...] = kr.T.astype(BF16)
        vt = xv.T.astype(BF16)
        one = jnp.ones((VTE_ROWS - A_HEAD_DIM, tm), BF16)
        v0_ref[...] = jnp.concatenate([vt[:A_HEAD_DIM], one], axis=0)
        v1_ref[...] = jnp.concatenate([vt[A_HEAD_DIM:], one], axis=0)

    tab = _rows(tm, LANES)
    colb = lambda w: pl.BlockSpec((w, tm), lambda i: (0, i))
    return pl.pallas_call(
        body,
        out_shape=(_sds((A_WIDTH, s), BF16), _sds((s, A_KV_WIDTH), BF16), _sds((s, A_KV_WIDTH), BF16),
                   _sds((A_KV_WIDTH, s), BF16), _sds((VTE_ROWS, s), BF16), _sds((VTE_ROWS, s), BF16)),
        grid=(s // tm,),
        in_specs=[_rows(tm, PBLK), tab, tab, tab, _full((1, A_WIDTH)), _full((1, A_KV_WIDTH)),
                  _full((A_WIDTH, A_WIDTH)), _full((A_KV_WIDTH, A_KV_WIDTH))],
        out_specs=(colb(A_WIDTH), _rows(tm, A_KV_WIDTH), _rows(tm, A_KV_WIDTH), colb(A_KV_WIDTH), colb(VTE_ROWS), colb(VTE_ROWS)),
        compiler_params=_cp("parallel"), name="qk_prep")(proj, c, sa, sb, qg, kg, gq, gk)


def _pad_head(q_h, kv):
    z = jnp.zeros_like(q_h)
    return jnp.concatenate([q_h, z], axis=0) if kv == 0 else jnp.concatenate([z, q_h], axis=0)


def attn_fwd(q_t, kr, vte0, vte1, gather=()):
    s = kr.shape[0]
    tq = min(s, 512)
    kc = min(s, 256)
    nkc = s // kc
    nq = s // tq
    grp = A_HEADS // A_KV_HEADS
    ng = len(gather)

    def body(qt_ref, kr_ref, v0_ref, v1_ref, *rest):
        g_in, (o_ref, lse_ref), g_out = rest[:ng], rest[ng:ng + 2], rest[ng + 2:2 * ng + 2]
        qp_ref, m_ref, acc_ref = rest[2 * ng + 2:2 * ng + 5]
        if ng:
            start, forward, finish = gather_stages([g.shape for g in gather], g_in, g_out, *rest[2 * ng + 5:])
            pl.when(pl.program_id(0) == 0)(start)
            pl.when(pl.program_id(0) == (3 * nq) // 4)(forward)

        for h in range(A_HEADS):
            qp_ref[h] = _pad_head(qt_ref[A_HEAD_DIM * h:A_HEAD_DIM * (h + 1), :], h // grp)
        m_ref[...] = jnp.full(m_ref.shape, -1e30, F32)
        acc_ref[...] = jnp.zeros_like(acc_ref)

        def step(ci, carry):
            ks = pl.ds(pl.multiple_of(ci * kc, kc), kc)
            kblk = kr_ref[ks, :]
            vts = (v0_ref[:, ks], v1_ref[:, ks])
            scs = [_dot(kblk, qp_ref[h]) for h in range(A_HEADS)]
            for h in range(A_HEADS):
                sc = scs[h]
                m_prev = m_ref[h:h + 1, :]
                m_new = jnp.maximum(m_prev, jnp.max(sc, axis=0, keepdims=True))
                p = jnp.exp2(sc - m_new)
                acc_ref[h] = acc_ref[h] * jnp.exp2(m_prev - m_new) + _dot(vts[h // grp], p.astype(BF16))
                m_ref[h:h + 1, :] = m_new
            return carry

        lax.fori_loop(0, nkc, step, 0)
        outs, lses = [], []
        for h in range(A_HEADS):
            acc = acc_ref[h]
            l = acc[A_HEAD_DIM:A_HEAD_DIM + 1, :]
            outs.append(acc[:A_HEAD_DIM, :] / l)
            lses.append(m_ref[h:h + 1, :] + jnp.log2(l))
        o_ref[...] = jnp.concatenate(outs, axis=0).T
        lse_ref[...] = jnp.concatenate(lses, axis=0)
        if ng:
            pl.when(pl.program_id(0) == nq - 1)(finish)

    out = pl.pallas_call(
        body,
        out_shape=(_sds((s, A_WIDTH), F32), _sds((A_HEADS, s), F32)) + tuple(_sds((N_CHIPS,) + g.shape, g.dtype) for g in gather),
        grid=(nq,),
        in_specs=[pl.BlockSpec((A_WIDTH, tq), lambda i: (0, i)), _full((s, A_KV_WIDTH)), _full((VTE_ROWS, s)),
                  _full((VTE_ROWS, s))] + [_ANY] * ng,
        out_specs=(_rows(tq, A_WIDTH), pl.BlockSpec((A_HEADS, tq), lambda i: (0, i))) + (_ANY,) * ng,
        scratch_shapes=[pltpu.VMEM((A_HEADS, A_KV_WIDTH, tq), BF16), pltpu.VMEM((A_HEADS, tq), F32),
                        pltpu.VMEM((A_HEADS, VTE_ROWS, tq), F32)] + (gather_sems(ng) if ng else []),
        compiler_params=_cp("arbitrary"), name="attn_fwd_gather" if ng else "attn_fwd")(q_t, kr, vte0, vte1, *gather)
    return out[0], out[1], list(out[2:])


def memkv_fwd(mem, g, w_kv):
    m, d = mem.shape

    def body(mem_ref, g_ref, w_ref, mn_ref, kv_ref):
        mf = mem_ref[...]
        r = lax.rsqrt(jnp.mean(mf * mf, axis=-1, keepdims=True) + EPS)
        mn = ((mf * r) * g_ref[...]).astype(BF16)
        mn_ref[...] = mn
        kv_ref[...] = _dot(mn, w_ref[...]).astype(BF16)

    return pl.pallas_call(
        body, out_shape=(_sds((m, d), BF16), _sds((m, 2 * M_WIDTH), BF16)),
        compiler_params=_cp(), name="memkv_fwd")(mem, g, w_kv)


def _layer_norm_stats(v):
    mu = jnp.mean(v, axis=-1, keepdims=True)
    xc = v - mu
    rstd = lax.rsqrt(jnp.mean(xc * xc, axis=-1, keepdims=True) + EPS)
    return xc * rstd, rstd


def _spatial_mix(vlb, ws_ref, bsb_ref, tm):
    rows = []
    for ci in range(tm // CHUNK):
        cols = []
        for g in range(B_GROUPS):
            blk = vlb[ci * CHUNK:(ci + 1) * CHUNK, g * B_GROUP_DIM:(g + 1) * B_GROUP_DIM]
            cols.append(_dot(ws_ref[g], blk) + bsb_ref[g])
        rows.append(jnp.concatenate(cols, axis=1))
    return jnp.concatenate(rows, axis=0)


def _mem_attn(qm, kv_ref):
    out = []
    for h in range(M_HEADS):
        qh = qm[:, h * M_HEAD_DIM:(h + 1) * M_HEAD_DIM].astype(BF16)
        kh = kv_ref[:, h * M_HEAD_DIM:(h + 1) * M_HEAD_DIM]
        vh = kv_ref[:, M_WIDTH + h * M_HEAD_DIM:M_WIDTH + (h + 1) * M_HEAD_DIM]
        sc = _dot_nt(qh, kh) * (M_HEAD_DIM ** -0.5)
        e = jnp.exp(sc - jnp.max(sc, axis=-1, keepdims=True))
        p = e / jnp.sum(e, axis=-1, keepdims=True)
        out.append((p, _dot(p.astype(BF16), vh)))
    return out


def branch_fwd(x, proj, o_a, kv, ws, bsb, ln_g, ln_b, w_br, w_out, next_g, tgt=None):
    s, d = x.shape
    tm = min(s, 512)
    last = tgt is not None

    def body(x_ref, p_ref, oa_ref, kv_ref, ws_ref, bsb_ref, lg_ref, lb_ref, wbr_ref, wo_ref, ng_ref, *rest):
        y_ref, up_ref, mg_ref = rest[-5:-2] if not last else rest[-6:-3]
        seg = lambda o, w: p_ref[:, o:o + w].astype(F32)
        z_a, u_b, v_b, z_b = seg(O_ZA, A_WIDTH), seg(O_UB, B_WIDTH), seg(O_VB, B_WIDTH), seg(O_ZB, B_WIDTH)
        q_m, z_m = seg(O_QM, M_WIDTH), seg(O_ZM, M_WIDTH)
        xhat, _ = _layer_norm_stats(v_b)
        vln = xhat * lg_ref[...] + lb_ref[...]
        mixed = _spatial_mix(vln.astype(BF16), ws_ref, bsb_ref, tm)
        y_b = (u_b * mixed) * (z_b * _sig(z_b))
        o_m = jnp.concatenate([o for _, o in _mem_attn(q_m, kv_ref)], axis=1)
        y_a = oa_ref[...] * (z_a * _sig(z_a))
        y_m = o_m * (z_m * _sig(z_m))
        merged = None
        for n, yy in enumerate((y_a, y_b, y_m)):
            yb = yy.astype(BF16)
            y_ref[n] = yb
            up = jnp.concatenate([_dot(yb, wbr_ref[c, n]) for c in range(N_CHIPS)], axis=1)
            up_ref[n] = up.astype(BF16)
            t = _sig(seg(O_LG + n * d, d)) * up
            merged = t if merged is None else merged + t
        mb = merged.astype(BF16)
        mg_ref[...] = mb
        xn = x_ref[...] + _dot(mb, wo_ref[...])
        r = lax.rsqrt(jnp.mean(xn * xn, axis=-1, keepdims=True) + EPS)
        xh = xn * r
        g = ng_ref[...]
        if not last:
            xn_ref, hn_ref = rest[-2:]
            xn_ref[...] = xn
            hn_ref[...] = (xh * g).astype(BF16)
        else:
            t_ref, (ls_ref, dx_ref, gg_ref) = rest[0], rest[-3:]

            @pl.when(pl.program_id(0) == 0)
            def _():
                ls_ref[...] = jnp.zeros_like(ls_ref)
                gg_ref[...] = jnp.zeros_like(gg_ref)

            e = xh * g - t_ref[...]
            sq = jnp.sum(jnp.sum(e * e, axis=0, keepdims=True), axis=1, keepdims=True)
            ls_ref[...] += jnp.broadcast_to(sq, ls_ref.shape)
            dy = e * (1.0 / d)
            gg_ref[...] += jnp.sum(dy * xh, axis=0, keepdims=True)
            gy = dy * g
            dx_ref[...] = r * (gy - xh * jnp.mean(gy * xh, axis=-1, keepdims=True))

    saved_shapes = (_sds((N_BRANCH, s, A_WIDTH), BF16), _sds((N_BRANCH, s, d), BF16), _sds((s, d), BF16))
    saved_specs = (pl.BlockSpec((N_BRANCH, tm, A_WIDTH), lambda i: (0, i, 0)), pl.BlockSpec((N_BRANCH, tm, d), lambda i: (0, i, 0)),
                   _rows(tm, d))
    if last:
        tail_shapes, tail_specs = (_sds((1, LANES), F32), _sds((s, d), F32), _sds((1, d), F32)), (_full((1, LANES)), _rows(tm, d), _full((1, d)))
    else:
        tail_shapes, tail_specs = (_sds((s, d), F32), _sds((s, d), BF16)), (_rows(tm, d), _rows(tm, d))
    return pl.pallas_call(
        body, out_shape=saved_shapes + tail_shapes, grid=(s // tm,),
        in_specs=[_rows(tm, d), _rows(tm, IN_WIDTH), _rows(tm, A_WIDTH), _full(kv.shape), _full(ws.shape), _full(bsb.shape),
                  _full((1, B_WIDTH)), _full((1, B_WIDTH)), _full(w_br.shape), _full(w_out.shape), _full((1, d))]
        + ([_rows(tm, d)] if last else []),
        out_specs=saved_specs + tail_specs,
        compiler_params=_cp("arbitrary" if last else "parallel"), name="branch_fwd_loss" if last else "branch_fwd")(
            x, proj, o_a, kv, ws, bsb, ln_g, ln_b, w_br, w_out, next_g, *([tgt] if last else []))


def _pblocks(tm, first, count):
    return [pl.BlockSpec((tm, PBLK), functools.partial(lambda i, b: (i, b), b=first + k)) for k in range(count)]


def merge_bwd(dx, proj, y, up, merged, w_br, w_out):
    s, d = dx.shape
    tm = min(s, 512)
    nlg = LG_W // PBLK
    cw = d // N_CHIPS

    def body(dx_ref, l0, l1, l2, l3, y_ref, up_ref, mg_ref, wbr_ref, wo_ref, dy_ref, dlg_ref, gwo_ref, gwb_ref, gwo16_ref, gwb16_ref):
        @pl.when(pl.program_id(0) == 0)
        def _():
            gwo_ref[...] = jnp.zeros_like(gwo_ref)
            gwb_ref[...] = jnp.zeros_like(gwb_ref)

        dxb = dx_ref[...].astype(BF16)
        dmg = _dot_nt(dxb, wo_ref[...])
        gwo_ref[...] += _dot_tn(mg_ref[...], dxb)
        lg = jnp.concatenate([l0[...], l1[...], l2[...], l3[...]], axis=1).astype(F32)
        for n in range(N_BRANCH):
            g = _sig(lg[:, n * d:(n + 1) * d])
            dup = dmg * g
            dlg_ref[:, n * d:(n + 1) * d] = ((dup * up_ref[n].astype(F32)) * (1.0 - g)).astype(BF16)
            dupb = dup.astype(BF16)
            dyn = None
            for c in range(N_CHIPS):
                blk = dupb[:, c * cw:(c + 1) * cw]
                gwb_ref[c, n] += _dot_tn(y_ref[n], blk)
                t = _dot_nt(blk, wbr_ref[c, n])
                dyn = t if dyn is None else dyn + t
            dy_ref[n] = dyn.astype(BF16)

        @pl.when(pl.program_id(0) == pl.num_programs(0) - 1)
        def _():
            gwo16_ref[...] = gwo_ref[...].astype(BF16)
            gwb16_ref[...] = gwb_ref[...].astype(BF16)

    return pl.pallas_call(
        body,
        out_shape=(_sds((N_BRANCH, s, A_WIDTH), BF16), _sds((s, LG_W), BF16), _sds((d, d), F32), _sds(w_br.shape, F32),
                   _sds((d, d), BF16), _sds(w_br.shape, BF16)),
        grid=(s // tm,),
        in_specs=[_rows(tm, d)] + _pblocks(tm, O_LG // PBLK, nlg) + [
            pl.BlockSpec((N_BRANCH, tm, A_WIDTH), lambda i: (0, i, 0)), pl.BlockSpec((N_BRANCH, tm, d), lambda i: (0, i, 0)),
            _rows(tm, d), _full(w_br.shape, once=True), _full(w_out.shape, once=True)],
        out_specs=(pl.BlockSpec((N_BRANCH, tm, A_WIDTH), lambda i: (0, i, 0)), _rows(tm, LG_W), _full((d, d)), _full(w_br.shape),
                   _full((d, d)), _full(w_br.shape)),
        compiler_params=_cp("arbitrary"), name="merge_bwd")(dx, proj, proj, proj, proj, y, up, merged, w_br, w_out)


def _dsilu(z, sg):
    return sg * (1.0 + z * (1.0 - sg))


def branch_bwd(dy, proj, o_a, kv, ws, ws_t, bsb, ln_g, ln_b, head_sel):
    s = proj.shape[0]
    tm = min(s, 512)
    nmid = MID_W // PBLK

    def body(dy_ref, m0, m1, m2, m3, oa_ref, kv_ref, ws_ref, wst_ref, bsb_ref, lg_ref, lb_ref, sel_ref,
             dmid_ref, dot_ref, dl_ref, gws_ref, gbs_ref, glg_ref, glb_ref, dkv_ref):
        @pl.when(pl.program_id(0) == 0)
        def _():
            for r in (gws_ref, gbs_ref, glg_ref, glb_ref, dkv_ref):
                r[...] = jnp.zeros_like(r)

        mid = jnp.concatenate([m0[...], m1[...], m2[...], m3[...]], axis=1).astype(F32)
        seg = lambda o, w: mid[:, o - O_ZA:o - O_ZA + w]
        z_a, u_b, v_b, z_b = seg(O_ZA, A_WIDTH), seg(O_UB, B_WIDTH), seg(O_VB, B_WIDTH), seg(O_ZB, B_WIDTH)
        q_m, z_m = seg(O_QM, M_WIDTH), seg(O_ZM, M_WIDTH)

        def put(o, v):
            dmid_ref[:, o - O_ZA:o - O_ZA + v.shape[1]] = v.astype(BF16)

        dy_a, dy_b, dy_m = dy_ref[0].astype(F32), dy_ref[1].astype(F32), dy_ref[2].astype(F32)

        o_a_ = oa_ref[...]
        sg = _sig(z_a)
        do_a = dy_a * (z_a * sg)
        put(O_ZA, (dy_a * o_a_) * _dsilu(z_a, sg))
        do_l = do_a * LN2
        dot_ref[...] = do_l.T.astype(BF16)
        dl_ref[...] = _dot_nt_hi(sel_ref[...], do_l * o_a_)

        xhat, rstd = _layer_norm_stats(v_b)
        lng = lg_ref[...]
        vln = xhat * lng + lb_ref[...]
        vlb = vln.astype(BF16)
        mixed = _spatial_mix(vlb, ws_ref, bsb_ref, tm)
        sg = _sig(z_b)
        sl = z_b * sg
        put(O_UB, (dy_b * mixed) * sl)
        put(O_ZB, ((dy_b * u_b) * mixed) * _dsilu(z_b, sg))
        dmix = (dy_b * u_b) * sl
        dmb = dmix.astype(BF16)
        rows = []
        for ci in range(tm // CHUNK):
            cols = []
            for g in range(B_GROUPS):
                rs, cs = slice(ci * CHUNK, (ci + 1) * CHUNK), slice(g * B_GROUP_DIM, (g + 1) * B_GROUP_DIM)
                gws_ref[g] += _dot_nt(dmb[rs, cs], vlb[rs, cs])
                gbs_ref[g] += jnp.broadcast_to(jnp.sum(dmix[rs, cs], axis=1, keepdims=True), (CHUNK, B_GROUP_DIM))
                cols.append(_dot(wst_ref[g], dmb[rs, cs]))
            rows.append(jnp.concatenate(cols, axis=1))
        dvln = jnp.concatenate(rows, axis=0)
        glg_ref[...] += jnp.sum(dvln * xhat, axis=0, keepdims=True)
        glb_ref[...] += jnp.sum(dvln, axis=0, keepdims=True)
        gy = dvln * lng
        put(O_VB, rstd * ((gy - jnp.mean(gy, axis=-1, keepdims=True)) - xhat * jnp.mean(gy * xhat, axis=-1, keepdims=True)))

        sg = _sig(z_m)
        sl = z_m * sg
        heads = _mem_attn(q_m, kv_ref)
        o_m = jnp.concatenate([o for _, o in heads], axis=1)
        put(O_ZM, (dy_m * o_m) * _dsilu(z_m, sg))
        do_m = dy_m * sl
        dqs = []
        for h, (p, o_h) in enumerate(heads):
            hs = slice(h * M_HEAD_DIM, (h + 1) * M_HEAD_DIM)
            vs = slice(M_WIDTH + h * M_HEAD_DIM, M_WIDTH + (h + 1) * M_HEAD_DIM)
            do_h = do_m[:, hs]
            dob = do_h.astype(BF16)
            dp = _dot_nt(dob, kv_ref[:, vs])
            dsc = (p * (dp - jnp.sum(do_h * o_h, axis=-1, keepdims=True))) * (M_HEAD_DIM ** -0.5)
            dsb = dsc.astype(BF16)
            dqs.append(_dot(dsb, kv_ref[:, hs]))
            dkv_ref[:, hs] += _dot_tn(dsb, q_m[:, hs].astype(BF16))
            dkv_ref[:, vs] += _dot_tn(p.astype(BF16), dob)
        put(O_QM, jnp.concatenate(dqs, axis=1))

    return pl.pallas_call(
        body,
        out_shape=(_sds((s, MID_W), BF16), _sds((A_WIDTH, s), BF16), _sds((A_HEADS, s), F32), _sds(ws.shape, F32),
                   _sds(ws.shape, F32), _sds((1, B_WIDTH), F32), _sds((1, B_WIDTH), F32), _sds(kv.shape, F32)),
        grid=(s // tm,),
        in_specs=[pl.BlockSpec((N_BRANCH, tm, A_WIDTH), lambda i: (0, i, 0))] + _pblocks(tm, O_ZA // PBLK, nmid) + [
            _rows(tm, A_WIDTH), _full(kv.shape), _full(ws.shape), _full(ws.shape), _full(bsb.shape),
            _full((1, B_WIDTH)), _full((1, B_WIDTH)), _full(head_sel.shape)],
        out_specs=(_rows(tm, MID_W), pl.BlockSpec((A_WIDTH, tm), lambda i: (0, i)), pl.BlockSpec((A_HEADS, tm), lambda i: (0, i)),
                   _full(ws.shape), _full(ws.shape), _full((1, B_WIDTH)), _full((1, B_WIDTH)), _full(kv.shape)),
        compiler_params=_cp("arbitrary"), name="branch_bwd")(dy, proj, proj, proj, proj, o_a, kv, ws, ws_t, bsb, ln_g, ln_b, head_sel)


def attn_bwd(q_t, do_t, kr, kr_t, vb, lse, delta, scatter=()):
    s = kr.shape[0]
    tq = min(s, 256)
    kc = min(s, 512)
    nkc = s // kc
    nq = s // tq
    grp = A_HEADS // A_KV_HEADS
    ns = len(scatter)
    na = ns // 2

    def body(qt_ref, dot_ref, kr_ref, krt_ref, vb_ref, lse_ref, dl_ref, *rest):
        s_in, (dqt_ref, dk_ref, dv_ref), s_out = rest[:ns], rest[ns:ns + 3], rest[ns + 3:2 * ns + 3]
        qp_ref, dop_ref, dq_ref = rest[2 * ns + 3:2 * ns + 6]
        if ns:
            start, finish = scatter_stages([g.shape[1:] for g in scatter[:na]], s_in[:na], s_in[na:], s_out[:na], s_out[na:],
                                           *rest[2 * ns + 6:])
            pl.when(pl.program_id(0) == 0)(start)

        @pl.when(pl.program_id(0) == 0)
        def _():
            dk_ref[...] = jnp.zeros_like(dk_ref)
            dv_ref[...] = jnp.zeros_like(dv_ref)

        for h in range(A_HEADS):
            hs = slice(A_HEAD_DIM * h, A_HEAD_DIM * (h + 1))
            qp_ref[h] = _pad_head(qt_ref[hs, :], h // grp)
            dop_ref[h] = _pad_head(dot_ref[hs, :], h // grp)
        dq_ref[...] = jnp.zeros_like(dq_ref)

        def step(ci, carry):
            ks = pl.ds(pl.multiple_of(ci * kc, kc), kc)
            kblk, vblk, ktb = kr_ref[ks, :], vb_ref[ks, :], krt_ref[:, ks]
            dv_acc = jnp.zeros((kc, A_KV_WIDTH), F32)
            dk_acc = jnp.zeros((kc, A_KV_WIDTH), F32)
            scs = [_dot(kblk, qp_ref[h]) for h in range(A_HEADS)]
            dps = [_dot(vblk, dop_ref[h]) for h in range(A_HEADS)]
            for h in range(A_HEADS):
                qpad, dopad = qp_ref[h], dop_ref[h]
                p = jnp.exp2(scs[h] - lse_ref[h:h + 1, :])
                dsb = (p * (dps[h] - dl_ref[h:h + 1, :])).astype(BF16)
                dv_acc = dv_acc + _dot_nt(p.astype(BF16), dopad)
                dk_acc = dk_acc + _dot_nt(dsb, qpad)
                dq_ref[h] += _dot(ktb, dsb)
            dv_ref[ks, :] += dv_acc
            dk_ref[ks, :] += dk_acc
            return carry

        lax.fori_loop(0, nkc, step, 0)
        dqt_ref[...] = jnp.concatenate(
            [dq_ref[h][A_HEAD_DIM * (h // grp):A_HEAD_DIM * (h // grp + 1), :] for h in range(A_HEADS)], axis=0)
        if ns:
            pl.when(pl.program_id(0) == nq - 1)(finish)

    colq = pl.BlockSpec((A_WIDTH, tq), lambda i: (0, i))
    colh = pl.BlockSpec((A_HEADS, tq), lambda i: (0, i))
    out = pl.pallas_call(
        body,
        out_shape=(_sds((A_WIDTH, s), F32), _sds((s, A_KV_WIDTH), F32), _sds((s, A_KV_WIDTH), F32)) + scatter_out_shapes(scatter[:na]),
        grid=(nq,),
        in_specs=[colq, colq, _full((s, A_KV_WIDTH)), _full((A_KV_WIDTH, s)), _full((s, A_KV_WIDTH)), colh, colh] + [_ANY] * ns,
        out_specs=(colq, _full((s, A_KV_WIDTH)), _full((s, A_KV_WIDTH))) + (_ANY,) * ns,
        scratch_shapes=[pltpu.VMEM((A_HEADS, A_KV_WIDTH, tq), BF16), pltpu.VMEM((A_HEADS, A_KV_WIDTH, tq), BF16),
                        pltpu.VMEM((A_HEADS, A_KV_WIDTH, tq), F32)] + (scatter_sems(na) if ns else []),
        compiler_params=_cp("arbitrary"), name="attn_bwd_scatter" if ns else "attn_bwd")(
            q_t, do_t, kr, kr_t, vb, lse, delta, *scatter)
    return out[0], out[1], out[2], list(out[3:3 + na]), list(out[3 + na:])


def qk_prep_bwd(proj, dq_t, dkr, dvb, tabs, qg, kg, gq, gk, fold_q, fold_k):
    s = proj.shape[0]
    tm = min(s, 1024)
    c, sa, sb = tabs

    def head_norm_bwd(x, dn, gain, gones, fold):
        ms = _group_sum(x * x, gones) * (1.0 / A_HEAD_DIM)
        r = lax.rsqrt(ms + EPS)
        xh = x * r
        gg = _dot_hi(jnp.sum(dn * xh, axis=0, keepdims=True), fold)
        u = dn * gain
        mean_u = _group_sum(u * xh, gones) * (1.0 / A_HEAD_DIM)
        return r * (u - xh * mean_u), gg

    def body(p_ref, dqt_ref, dk_ref, dv_ref, c_ref, sa_ref, sb_ref, qg_ref, kg_ref, gq_ref, gk_ref, fq_ref, fk_ref,
             dqkv_ref, gqg_ref, gkg_ref):
        @pl.when(pl.program_id(0) == 0)
        def _():
            gqg_ref[...] = jnp.zeros_like(gqg_ref)
            gkg_ref[...] = jnp.zeros_like(gkg_ref)

        cc, ssa, ssb = c_ref[...], sa_ref[...], sb_ref[...]
        dqr = dqt_ref[...].T * Q_SCALE
        dqn = _rope_t(dqr, _tile4(cc), _tile4(ssa), _tile4(ssb))
        dxq, gq_ = head_norm_bwd(p_ref[:, O_QA:O_QA + A_WIDTH].astype(F32), dqn, qg_ref[...], gq_ref[...], fq_ref[...])
        dkn = _rope_t(dk_ref[...], cc, ssa, ssb)
        dxk, gk_ = head_norm_bwd(p_ref[:, O_KA:O_KA + A_KV_WIDTH].astype(F32), dkn, kg_ref[...], gk_ref[...], fk_ref[...])
        gqg_ref[...] += gq_
        gkg_ref[...] += gk_
        dqkv_ref[:, O_QA:O_QA + A_WIDTH] = dxq.astype(BF16)
        dqkv_ref[:, O_KA:O_KA + A_KV_WIDTH] = dxk.astype(BF16)
        dqkv_ref[:, O_VA:O_VA + A_KV_WIDTH] = (dv_ref[...] * (1.0 / LN2)).astype(BF16)

    tab = _rows(tm, LANES)
    return pl.pallas_call(
        body, out_shape=(_sds((s, PBLK), BF16), _sds((1, LANES), F32), _sds((1, LANES), F32)), grid=(s // tm,),
        in_specs=[_rows(tm, PBLK), pl.BlockSpec((A_WIDTH, tm), lambda i: (0, i)), _rows(tm, A_KV_WIDTH), _rows(tm, A_KV_WIDTH),
                  tab, tab, tab, _full((1, A_WIDTH)), _full((1, A_KV_WIDTH)), _full((A_WIDTH, A_WIDTH)),
                  _full((A_KV_WIDTH, A_KV_WIDTH)), _full((A_WIDTH, LANES)), _full((A_KV_WIDTH, LANES))],
        out_specs=(_rows(tm, PBLK), _full((1, LANES)), _full((1, LANES))),
        compiler_params=_cp("arbitrary"), name="qk_prep_bwd")(proj, dq_t, dkr, dvb, c, sa, sb, qg, kg, gq, gk, fold_q, fold_k)


def _pick_dproj(b, d0, d1, d2, use):
    first_lg = 1 + MID_W // PBLK

    @pl.when(b == 0)
    def _():
        use(d0[...])

    @pl.when(jnp.logical_and(b >= 1, b < first_lg))
    def _():
        use(d1[...])

    @pl.when(b >= first_lg)
    def _():
        use(d2[...])


def win_grad(d0, d1, d2, h):
    s, d = h.shape
    tk = min(s, 4096)
    nk = s // tk

    def body(d0_ref, d1_ref, d2_ref, h_ref, o_ref, o16_ref):
        @pl.when(pl.program_id(1) == 0)
        def _():
            o_ref[...] = jnp.zeros_like(o_ref)

        def use(blk):
            o_ref[...] += _dot_tn(blk, h_ref[...])

        _pick_dproj(pl.program_id(0), d0_ref, d1_ref, d2_ref, use)

        @pl.when(pl.program_id(1) == nk - 1)
        def _():
            o16_ref[...] = o_ref[...].astype(BF16)

    def spec(first, count):
        def imap(j, k):
            used = jnp.logical_and(j >= first, j < first + count)
            return (jnp.where(used, k, 0), jnp.clip(j - first, 0, count - 1))
        return pl.BlockSpec((tk, PBLK), imap)

    nm = MID_W // PBLK
    oblk = pl.BlockSpec((PBLK, d), lambda j, k: (j, 0))
    return pl.pallas_call(
        body, out_shape=(_sds((IN_WIDTH, d), F32), _sds((IN_WIDTH, d), BF16)), grid=(N_PBLK, nk),
        in_specs=[spec(0, 1), spec(1, nm), spec(1 + nm, LG_W // PBLK),
                  pl.BlockSpec((tk, d), lambda j, k: (k, 0), pipeline_mode=pl.Buffered(1) if nk == 1 else None)],
        out_specs=(oblk, oblk),
        compiler_params=_cp("parallel", "arbitrary"), name="win_grad")(d0, d1, d2, h)


def h_bwd(d0, d1, d2, w_t, x, dx_out, g, scatter=(), swap=()):
    s, d = x.shape
    tm = min(s, 512)
    nt = s // tm
    ns = len(scatter)
    na = ns // 2
    nw = len(swap)

    def body(d0_ref, d1_ref, d2_ref, w_ref, x_ref, dxo_ref, g_ref, *rest):
        s_in, (dx_ref, gg_ref), s_out = rest[:ns], rest[ns + nw:ns + nw + 2], rest[ns + nw + 2:2 * ns + nw + 2]
        w_out, sems = rest[2 * ns + nw + 2:2 * (ns + nw) + 2], rest[2 * (ns + nw) + 2:]
        stages = []
        if ns:
            stages.append(scatter_stages([a.shape[1:] for a in scatter[:na]], s_in[:na], s_in[na:], s_out[:na], s_out[na:], *sems[:2]))
        if nw:
            stages.append(swap_stages([a.shape[0] for a in swap], w_out, *sems[-2:]))
        for start, _ in stages:
            pl.when(pl.program_id(0) == 0)(start)

        @pl.when(pl.program_id(0) == 0)
        def _():
            gg_ref[...] = jnp.zeros_like(gg_ref)

        dh = (_dot(d0_ref[...], w_ref[0:PBLK, :]) + _dot(d1_ref[...], w_ref[PBLK:PBLK + MID_W, :])
              + _dot(d2_ref[...], w_ref[PBLK + MID_W:, :]))
        xf = x_ref[...]
        r = lax.rsqrt(jnp.mean(xf * xf, axis=-1, keepdims=True) + EPS)
        xh = xf * r
        gg_ref[...] += jnp.sum(dh * xh, axis=0, keepdims=True)
        u = dh * g_ref[...]
        dx_ref[...] = dxo_ref[...] + r * (u - xh * jnp.mean(u * xh, axis=-1, keepdims=True))
        for _, finish in stages:
            pl.when(pl.program_id(0) == nt - 1)(finish)

    rowb = _rows(tm, d)
    out = pl.pallas_call(
        body, out_shape=(_sds((s, d), F32), _sds((1, d), F32)) + scatter_out_shapes(scatter[:na])
        + tuple(_sds(a.shape, F32) for a in swap), grid=(nt,),
        in_specs=[_rows(tm, PBLK), _rows(tm, MID_W), _rows(tm, LG_W),
                  pl.BlockSpec(w_t.shape, lambda i: (0, 0), pipeline_mode=pl.Buffered(1)), rowb, rowb, _full((1, d))]
        + [_ANY] * (ns + nw),
        out_specs=(rowb, _full((1, d))) + (_ANY,) * (ns + nw),
        input_output_aliases={7 + ns + a: 2 + ns + a for a in range(nw)},
        scratch_shapes=(scatter_sems(na) if ns else []) + (swap_sems(nw) if nw else []),
        compiler_params=_cp("arbitrary"), name="h_bwd_scatter" if ns else "h_bwd")(d0, d1, d2, w_t, x, dx_out, g, *scatter, *swap)
    return out[0], out[1], list(out[2:2 + na]), list(out[2 + na:2 + ns]), list(out[2 + ns:])


def memkv_bwd(mem, g, mem_n, w_kv, dkv):
    m, d = mem.shape

    def body(mem_ref, g_ref, mn_ref, w_ref, dkv_ref, gw_ref, gw16_ref, gg_ref):
        dkb = dkv_ref[...].astype(BF16)
        gw = _dot_tn(mn_ref[...], dkb)
        gw_ref[...] = gw
        gw16_ref[...] = gw.astype(BF16)
        dmn = _dot_nt(dkb, w_ref[...])
        mf = mem_ref[...]
        r = lax.rsqrt(jnp.mean(mf * mf, axis=-1, keepdims=True) + EPS)
        gg_ref[...] = jnp.sum(dmn * (mf * r), axis=0, keepdims=True)

    return pl.pallas_call(
        body, out_shape=(_sds(w_kv.shape, F32), _sds(w_kv.shape, BF16), _sds((1, d), F32)),
        compiler_params=_cp(), name="memkv_bwd")(mem, g, mem_n, w_kv, dkv)


def _layer_consts(seq):
    i = jnp.arange(A_WIDTH)
    return dict(
        tabs=rope_tables(seq),
        gq=_group_ones(A_WIDTH, A_HEAD_DIM).astype(BF16), gk=_group_ones(A_KV_WIDTH, A_HEAD_DIM).astype(BF16),
        fold_q=(i[:, None] % A_HEAD_DIM == jnp.arange(LANES)[None, :]).astype(F32),
        fold_k=(i[:A_KV_WIDTH, None] % A_HEAD_DIM == jnp.arange(LANES)[None, :]).astype(F32),
        head_sel=(jnp.arange(A_HEADS)[:, None] == i[None, :] // A_HEAD_DIM).astype(F32),
    )


_BIG = ("win_t", "wkv", "wbr", "wout")


def _with_own_part(names, gathered, shards, chip, d):
    shape = dict(win_t=(IN_WIDTH, d), wkv=(d, 2 * M_WIDTH), wbr=(N_CHIPS, N_BRANCH, A_WIDTH, d // N_CHIPS), wout=(d, d))
    return {n: lax.dynamic_update_slice(g, sh[None], (chip, 0, 0)).reshape(shape[n]) for n, g, sh in zip(names, gathered, shards)}


def local_fwd_bwd(x, mem, tgt, small, big=None, shards=None, place=None):
    s, d = x.shape
    depth = small["norm_g"].shape[0]
    k = _layer_consts(s)
    row = lambda v: v.reshape(1, -1)
    dist = shards is not None
    if dist:
        big = [None] * depth
    saved = []
    for l in range(depth):
        ng = row(small["norm_g"][l])
        qg = row(jnp.tile(small["q_norm_g"][l], A_HEADS))
        kg = row(jnp.tile(small["k_norm_g"][l], A_KV_HEADS))
        ws = small["w_s"][l].astype(BF16)
        ws_t = jnp.swapaxes(small["w_s"][l], 1, 2).astype(BF16)
        bsb = jnp.broadcast_to(small["b_s"][l][:, :, None], (B_GROUPS, CHUNK, B_GROUP_DIM))
        lng, lnb = row(small["sg_ln_g"][l]), row(small["sg_ln_b"][l])
        mg = row(small["mem_norm_g"][l])
        if l == 0:
            first = tuple(shards[0][:1]) if dist else ()
            h, gathered = rms_fwd(x, ng, gather=first)
            if dist:
                big[0] = _with_own_part(_BIG[:1], gathered, first, place[0], d)
        else:
            h = h_next
        w = big[l]
        late = tuple(shards[0][1:]) if dist and l == 0 else ()
        proj, gathered = proj_fwd(h, w["win_t"], gather=late)
        if late:
            w.update(_with_own_part(_BIG[1:], gathered, late, place[0], d))
        q_t, kr, vb, kr_t, vte0, vte1 = qk_prep(proj, k["tabs"], qg, kg, k["gq"], k["gk"])
        nxt = tuple(shards[l + 1]) if dist and l + 1 < depth else ()
        o_a, lse, gathered = attn_fwd(q_t, kr, vte0, vte1, gather=nxt)
        if nxt:
            big[l + 1] = _with_own_part(_BIG, gathered, nxt, place[0], d)
        mem_n, kv = memkv_fwd(mem, mg, w["wkv"])
        x_in = x
        if l + 1 < depth:
            y, up, merged, x, h_next = branch_fwd(x, proj, o_a, kv, ws, bsb, lng, lnb, w["wbr"], w["wout"], row(small["norm_g"][l + 1]))
        else:
            y, up, merged, sq, dx, g_final = branch_fwd(x, proj, o_a, kv, ws, bsb, lng, lnb, w["wbr"], w["wout"],
                                                        row(small["final_g"]), tgt=tgt)
        saved.append(dict(x=x_in, ng=ng, qg=qg, kg=kg, ws=ws, ws_t=ws_t, bsb=bsb, lng=lng, lnb=lnb, mg=mg, h=h, proj=proj,
                          q_t=q_t, kr=kr, kr_t=kr_t, vb=vb, o_a=o_a, lse=lse, mem_n=mem_n, kv=kv, y=y, up=up, merged=merged))

    grads = {n: [None] * depth for n in ("norm_g", "q_norm_g", "k_norm_g", "sg_ln_g", "sg_ln_b", "w_s", "b_s", "mem_norm_g")}
    parts = lambda g: g.reshape(N_CHIPS, -1, g.shape[-1])
    reduced = [[None] * len(_BIG) for _ in range(depth)]

    def reduce_all(items, t_sib, t_rem):
        if items:
            for (ll, a, _, _), f in zip(items, reduce_rows(place, [i[2] for i in items], t_sib, t_rem)):
                reduced[ll][a] = f

    as_scatter = lambda items: tuple(i[2] for i in items) + tuple(i[3] for i in items)
    pending = []
    for l in reversed(range(depth)):
        sv, w = saved[l], big[l]
        dy, dlg, g_wout, g_wbr, g_wout16, g_wbr16 = merge_bwd(dx, sv["proj"], sv["y"], sv["up"], sv["merged"], w["wbr"], w["wout"])
        dmid, do_t, delta, g_ws, g_bs, g_lng, g_lnb, dkv = branch_bwd(
            dy, sv["proj"], sv["o_a"], sv["kv"], sv["ws"], sv["ws_t"], sv["bsb"], sv["lng"], sv["lnb"], k["head_sel"])
        g_wkv, g_wkv16, g_mg = memkv_bwd(mem, sv["mg"], sv["mem_n"], w["wkv"], dkv)
        if dist:
            pending += [(l, 1, parts(g_wkv), parts(g_wkv16)), (l, 2, parts(g_wbr), parts(g_wbr16)), (l, 3, parts(g_wout), parts(g_wout16))]
        dq_t, dkr, dvb, t_sib, t_rem = attn_bwd(sv["q_t"], do_t, sv["kr"], sv["kr_t"], sv["vb"], sv["lse"], delta,
                                                scatter=as_scatter(pending))
        reduce_all(pending, t_sib, t_rem)
        dqkv, g_qg, g_kg = qk_prep_bwd(sv["proj"], dq_t, dkr, dvb, k["tabs"], sv["qg"], sv["kg"], k["gq"], k["gk"],
                                       k["fold_q"], k["fold_k"])
        g_win, g_win16 = win_grad(dqkv, dmid, dlg, sv["h"])
        pending = [(l, 0, parts(g_win), parts(g_win16))] if dist else []
        last = as_scatter(pending) if l == 0 else ()
        done = [(ll, a) for ll in range(depth) for a in range(len(_BIG)) if reduced[ll][a] is not None] if last else []
        dx, g_ng, t_sib, t_rem, swapped = h_bwd(dqkv, dmid, dlg, w["win_t"], sv["x"], dx, sv["ng"], scatter=last,
                                                swap=tuple(reduced[ll][a] for ll, a in done))
        for (ll, a), f in zip(done, swapped):
            reduced[ll][a] = f
        if last:
            reduce_all(pending, t_sib, t_rem)
        grads["norm_g"][l] = g_ng[0]
        grads["q_norm_g"][l] = g_qg[0, :A_HEAD_DIM]
        grads["k_norm_g"][l] = g_kg[0, :A_HEAD_DIM]
        grads["sg_ln_g"][l] = g_lng[0]
        grads["sg_ln_b"][l] = g_lnb[0]
        grads["w_s"][l] = g_ws
        grads["b_s"][l] = g_bs[:, :, 0]
        grads["mem_norm_g"][l] = g_mg[0]
        if not dist:
            reduced[l] = dict(zip(_BIG, (parts(g_win), parts(g_wkv), parts(g_wbr), parts(g_wout))))
    grads = {n: jnp.stack(v) for n, v in grads.items()}
    grads["final_g"] = g_final[0]
    return sq[0, 0], dx, grads, reduced


def _row_block(rows, width, cap_bytes=2 * 2**20):
    best = None
    for br in range(8, rows + 1, 8):
        if rows % br == 0 and br * width * 4 <= cap_bytes:
            best = br
    return best if best is not None else rows


def adamw(w, gs, m, v):
    r, c = w.shape
    n = len(gs)
    rs = r // n
    br = _row_block(rs, c)
    nb = rs // br

    def body(w_ref, *refs):
        g_refs, (m_ref, v_ref, og_ref, d_ref, nm_ref, nv_ref) = refs[:n], refs[n:]

        def update(gg):
            mm = ADAM_B1 * m_ref[...] + (1.0 - ADAM_B1) * gg
            vv = ADAM_B2 * v_ref[...] + (1.0 - ADAM_B2) * (gg * gg)
            m_hat = mm / (1.0 - ADAM_B1 ** ADAM_STEP)
            v_hat = vv / (1.0 - ADAM_B2 ** ADAM_STEP)
            og_ref[...] = gg
            d_ref[...] = -ADAM_LR * (m_hat / (jnp.sqrt(v_hat) + ADAM_EPS) + ADAM_WD * w_ref[...])
            nm_ref[...] = mm
            nv_ref[...] = vv

        for k in range(n):
            pl.when(pl.program_id(0) == k)(functools.partial(lambda k: update(g_refs[k][...]), k))

    blk = pl.BlockSpec((br, c), lambda l, i: (l * nb + i, 0))
    g_specs = [pl.BlockSpec((br, c), functools.partial(lambda l, i, k: (jnp.where(l == k, i, 0), 0), k=k)) for k in range(n)]
    return pl.pallas_call(
        body, out_shape=(_sds((r, c), F32),) * 4, grid=(n, nb), in_specs=[blk] + g_specs + [blk, blk], out_specs=(blk,) * 4,
        compiler_params=_cp("arbitrary", "arbitrary"), name="adamw")(w, *gs, m, v)


N_REMOTE = 2 * (N_CHIPS - 1)


def reduce_rows(place, gs, t_sibs, t_rems):
    n = len(gs)
    nt = 2

    def body(place_ref, *refs):
        for a in range(n):
            g_ref, s_ref, t_ref, f_ref = refs[a], refs[n + a], refs[2 * n + a], refs[3 * n + a]
            acc = g_ref[...] + s_ref[...]
            for j in range(N_REMOTE):
                acc = acc + t_ref[j].astype(F32)
            f_ref[...] = acc

    tiles = [(g.shape[1] // 2 // nt, g.shape[2]) for g in gs]
    return pl.pallas_call(
        body, out_shape=tuple(_sds(g.shape[1:], F32) for g in gs),
        grid_spec=pltpu.PrefetchScalarGridSpec(
            num_scalar_prefetch=1, grid=(nt,),
            in_specs=[pl.BlockSpec((None, tr, c), lambda i, p: (p[0], p[1] * nt + i, 0)) for tr, c in tiles]
            + [pl.BlockSpec((tr, c), lambda i, p: (i, 0)) for tr, c in tiles]
            + [pl.BlockSpec((N_REMOTE, tr, c), lambda i, p: (0, i, 0)) for tr, c in tiles],
            out_specs=tuple(pl.BlockSpec((tr, c), lambda i, p: (p[1] * nt + i, 0)) for tr, c in tiles)),
        compiler_params=_cp("parallel"), name="reduce_rows")(place, *gs, *t_sibs, *t_rems)


_ANY = pl.BlockSpec(memory_space=pl.ANY)


def _place():
    x, y, c = lax.axis_index("x"), lax.axis_index("y"), lax.axis_index("c")
    chips = [(1 - x, y), (x, 1 - y), (1 - x, 1 - y)]
    return x, y, c, chips


def gather_sems(n):
    return [pltpu.SemaphoreType.DMA((n, N_REMOTE)), pltpu.SemaphoreType.DMA((n, N_REMOTE))]


def gather_stages(shapes, ins, outs, send, recv):
    n = len(shapes)
    x, y, c, chips = _place()
    me = 2 * x + y
    sib = (x, y, 1 - c)

    def rows(a, hl):
        r2 = shapes[a][0] // 2
        return pl.ds(hl * r2, r2)

    def remote(a, k, src, dst, dev):
        return pltpu.make_async_remote_copy(src, dst, send.at[a, k], recv.at[a, k], device_id=dev, device_id_type=MESH)

    def sent(a, k):
        cx, cy = chips[k]
        return remote(a, k, ins[a].at[rows(a, c)], outs[a].at[me, rows(a, c)], (cx, cy, c))

    def got(a, k, hl):
        cx, cy = chips[k]
        return outs[a].at[2 * cx + cy, rows(a, hl)]

    def arrived(a, k):
        return remote(a, k, got(a, k, c), got(a, k, c), (*chips[k], c))

    def passed(a, k, hl):
        return remote(a, 3 + k, got(a, k, hl), got(a, k, hl), sib)

    def start():
        for a in range(n):
            for k in range(3):
                sent(a, k).start()

    def forward():
        for k in range(3):
            for a in range(n):
                arrived(a, k).wait_recv()
                passed(a, k, c).start()

    def finish():
        for k in range(3):
            for a in range(n):
                passed(a, k, 1 - c).wait_recv()
        for k in range(3):
            for a in range(n):
                sent(a, k).wait_send()
                passed(a, k, c).wait_send()

    return start, forward, finish


def scatter_sems(n):
    return [pltpu.SemaphoreType.DMA((n, N_REMOTE + 1)), pltpu.SemaphoreType.DMA((n, N_REMOTE + 1))]


def scatter_out_shapes(gs):
    return (tuple(_sds((g.shape[1] // 2, g.shape[2]), F32) for g in gs)
            + tuple(_sds((N_REMOTE, g.shape[1] // 2, g.shape[2]), BF16) for g in gs))


def scatter_stages(shapes, gf, gb, t_sib, t_rem, send, recv):
    n = len(shapes)
    x, y, c, chips = _place()
    me = 2 * x + y

    def copies():
        out = []
        for a in range(n):
            r2 = shapes[a][0] // 2
            out.append(pltpu.make_async_remote_copy(gf[a].at[me, pl.ds((1 - c) * r2, r2)], t_sib[a], send.at[a, N_REMOTE],
                                                    recv.at[a, N_REMOTE], device_id=(x, y, 1 - c), device_id_type=MESH))
            for k, (cx, cy) in enumerate(chips):
                for o in range(2):
                    tc = c if o == 0 else 1 - c
                    out.append(pltpu.make_async_remote_copy(gb[a].at[2 * cx + cy, pl.ds(tc * r2, r2)], t_rem[a].at[2 * k + o],
                                                            send.at[a, 2 * k + o], recv.at[a, 2 * k + o],
                                                            device_id=(cx, cy, tc), device_id_type=MESH))
        return out

    def start():
        for cp in copies():
            cp.start()

    def finish():
        for cp in copies():
            cp.wait()

    return start, finish


def swap_sems(n):
    return [pltpu.SemaphoreType.DMA((n,)), pltpu.SemaphoreType.DMA((n,))]


def swap_stages(rows, bufs, send, recv):
    x, y, c, _ = _place()

    def copies(core):
        return [pltpu.make_async_remote_copy(b.at[pl.ds(core * (r // 2), r // 2)], b.at[pl.ds(core * (r // 2), r // 2)],
                                             send.at[a], recv.at[a], device_id=(x, y, 1 - c), device_id_type=MESH)
                for a, (r, b) in enumerate(zip(rows, bufs))]

    def start():
        for cp in copies(c):
            cp.start()

    def finish():
        for mine, theirs in zip(copies(c), copies(1 - c)):
            mine.wait_send()
            theirs.wait_recv()

    return start, finish


def finish_exchange(v, fs):
    n = len(fs)
    r, w = v.shape
    ndev = 2 * N_CHIPS

    def body(v_ref, *refs):
        out, sum_ref = refs[n:2 * n], refs[2 * n]
        all_ref, send, recv, loc, fsend, frecv = refs[2 * n + 1:]
        x, y, c, chips = _place()
        me, sib = (x, y, c), (x, y, 1 - c)
        start_swaps, finish_swaps = swap_stages([f.shape[0] for f in fs], out, fsend, frecv)
        start_swaps()

        def slab(px, py, pc):
            return all_ref.at[4 * px + 2 * py + pc]

        def copy(k, block, to, src=None):
            return pltpu.make_async_remote_copy(slab(*block) if src is None else src, slab(*block), send.at[k], recv.at[k],
                                                device_id=to, device_id_type=MESH)

        mine = pltpu.make_async_copy(v_ref, slab(*me), loc)
        mine.start()
        first = [copy(0, me, sib, src=v_ref)] + [copy(1 + j, me, (*chip, c), src=v_ref) for j, chip in enumerate(chips)]
        for cp in first:
            cp.start()
        passed = [copy(4 + j, (*chip, c), sib) for j, chip in enumerate(chips)]
        for j, chip in enumerate(chips):
            copy(1 + j, (*chip, c), me).wait_recv()
            passed[j].start()
        copy(0, sib, me).wait_recv()
        for j, chip in enumerate(chips):
            copy(4 + j, (*chip, 1 - c), me).wait_recv()
        for cp in first + passed:
            cp.wait_send()
        mine.wait()
        acc = all_ref[0]
        for i in range(1, ndev):
            acc = acc + all_ref[i]
        sum_ref[...] = acc
        finish_swaps()

    vm = pl.BlockSpec(memory_space=pltpu.VMEM)
    res = pl.pallas_call(
        body, out_shape=tuple(_sds(f.shape, F32) for f in fs) + (_sds((r, w), F32),),
        in_specs=[vm] + [_ANY] * n, out_specs=(_ANY,) * n + (vm,), input_output_aliases={a + 1: a for a in range(n)},
        scratch_shapes=[pltpu.VMEM((ndev, r, w), F32), pltpu.SemaphoreType.DMA((7,)), pltpu.SemaphoreType.DMA((7,)),
                        pltpu.SemaphoreType.DMA, pltpu.SemaphoreType.DMA((n,)), pltpu.SemaphoreType.DMA((n,))],
        compiler_params=pltpu.CompilerParams(vmem_limit_bytes=VMEM_LIMIT), name="finish_exchange")(v, *fs)
    return res[n], list(res[:n])


_SMALL = ("norm_g", "q_norm_g", "k_norm_g", "sg_ln_g", "sg_ln_b", "w_s", "b_s", "mem_norm_g", "final_g")
_WEIGHTS = ("norm_g", "w_in", "q_norm_g", "k_norm_g", "sg_ln_g", "sg_ln_b", "w_s", "b_s", "mem_norm_g", "w_mem_kv", "w_br",
            "w_out", "final_g")


def _pack(d, tail=None):
    flat = jnp.concatenate([d[n].reshape(-1) for n in _SMALL] + ([tail.reshape(1)] if tail is not None else []))
    rows = -(-(sum(d[n].size for n in _SMALL) + 1) // (8 * LANES)) * 8
    return jnp.pad(flat, (0, rows * LANES - flat.shape[0])).reshape(rows, LANES)


def _unpack(p, like):
    flat, out, o = p.reshape(-1), {}, 0
    for n in _SMALL:
        out[n] = flat[o:o + like[n].size].reshape(like[n].shape)
        o += like[n].size
    return out


def kernel(x, mem, norm_g, w_in, q_norm_g, k_norm_g, sg_ln_g, sg_ln_b, w_s, b_s, mem_norm_g, w_mem_kv, w_br, w_out, final_g, loss_target, m_norm_g, m_w_in, m_q_norm_g, m_k_norm_g, m_sg_ln_g, m_sg_ln_b, m_w_s, m_b_s, m_mem_norm_g, m_w_mem_kv, m_w_br, m_w_out, m_final_g, v_norm_g, v_w_in, v_q_norm_g, v_k_norm_g, v_sg_ln_g, v_sg_ln_b, v_w_s, v_b_s, v_mem_norm_g, v_w_mem_kv, v_w_br, v_w_out, v_final_g):
    w = dict(norm_g=norm_g, w_in=w_in, q_norm_g=q_norm_g, k_norm_g=k_norm_g, sg_ln_g=sg_ln_g, sg_ln_b=sg_ln_b, w_s=w_s, b_s=b_s,
             mem_norm_g=mem_norm_g, w_mem_kv=w_mem_kv, w_br=w_br, w_out=w_out, final_g=final_g)
    m = dict(norm_g=m_norm_g, w_in=m_w_in, q_norm_g=m_q_norm_g, k_norm_g=m_k_norm_g, sg_ln_g=m_sg_ln_g, sg_ln_b=m_sg_ln_b,
             w_s=m_w_s, b_s=m_b_s, mem_norm_g=m_mem_norm_g, w_mem_kv=m_w_mem_kv, w_br=m_w_br, w_out=m_w_out, final_g=m_final_g)
    v = dict(norm_g=v_norm_g, w_in=v_w_in, q_norm_g=v_q_norm_g, k_norm_g=v_k_norm_g, sg_ln_g=v_sg_ln_g, sg_ln_b=v_sg_ln_b,
             w_s=v_w_s, b_s=v_b_s, mem_norm_g=v_mem_norm_g, w_mem_kv=v_w_mem_kv, w_br=v_w_br, w_out=v_w_out, final_g=v_final_g)
    depth, d = norm_g.shape
    nsh = N_CHIPS
    br_rows = N_BRANCH * A_WIDTH
    br_cols = d // nsh

    shards = [[jnp.swapaxes(w_in[l], 0, 1).astype(BF16), w_mem_kv[l].astype(BF16), w_br[l].astype(BF16).reshape(br_rows, br_cols),
               w_out[l].astype(BF16)] for l in range(depth)]
    place = jnp.stack([2 * lax.axis_index("x") + lax.axis_index("y"), lax.axis_index("c")]).astype(jnp.int32)
    small = {n: w[n] for n in _SMALL}

    sq, dx, grads, reduced = local_fwd_bwd(x[0], mem[0], loss_target[0], small, shards=shards, place=place)

    small_sum, reduced[0][:1] = finish_exchange(_pack(grads, tail=sq), reduced[0][:1])
    finals = [g for layer in reduced for g in layer]
    loss = (0.5 / d) * small_sum.reshape(-1)[sum(small[n].size for n in _SMALL)]
    big_grads = dict(zip(("w_in", "w_mem_kv", "w_br", "w_out"), [finals[a::len(_BIG)] for a in range(len(_BIG))]))
    small_grads = _unpack(small_sum, small)

    out_g, out_d, out_m, out_v = {}, {}, {}, {}
    _, sd, sm, sv = adamw(_pack(small), [small_sum], _pack({n: m[n] for n in _SMALL}), _pack({n: v[n] for n in _SMALL}))
    sd, sm, sv = _unpack(sd, small), _unpack(sm, small), _unpack(sv, small)
    for n in _SMALL:
        out_g[n], out_d[n], out_m[n], out_v[n] = small_grads[n], sd[n], sm[n], sv[n]
    for n, gs in big_grads.items():
        into = (lambda a: jnp.swapaxes(a, 1, 2)) if n == "w_in" else (lambda a: a)
        two_d = lambda a: a.reshape(-1, gs[0].shape[-1])
        res = adamw(two_d(into(w[n])), gs, two_d(into(m[n])), two_d(into(v[n])))
        out_g[n], out_d[n], out_m[n], out_v[n] = [into(t.reshape(into(w[n]).shape)) for t in res]
    return (loss, dx[None], *[out_g[n] for n in _WEIGHTS], *[out_d[n] for n in _WEIGHTS], *[out_m[n] for n in _WEIGHTS],
            *[out_v[n] for n in _WEIGHTS])
```

```python
import functools

import jax
import jax.numpy as jnp
from jax import lax
from jax.experimental import pallas as pl
from jax.experimental.pallas import tpu as pltpu

F32 = jnp.float32
BF16 = jnp.bfloat16

GRID_W = 64
CHUNK = 128
ROPE_THETA = 10000.0
EPS = 1e-6
A_HEADS, A_KV_HEADS, A_HEAD_DIM = 8, 2, 64
A_WIDTH, A_KV_WIDTH = 512, 128
B_GROUPS, B_GROUP_DIM, B_WIDTH = 4, 128, 512
M_HEADS, M_HEAD_DIM, M_WIDTH = 4, 128, 512
N_BRANCH = 3
IN_WIDTH = 6912
O_QA, O_KA, O_VA, O_ZA, O_UB, O_VB, O_ZB, O_QM, O_ZM, O_LG = 0, 512, 640, 768, 1280, 1792, 2304, 2816, 3328, 3840
PBLK = 768
N_PBLK = IN_WIDTH // PBLK
MID_W = 3072
LG_W = 3072

LN2 = 0.6931471805599453
Q_SCALE = A_HEAD_DIM ** -0.5 / LN2
VTE_ROWS = A_HEAD_DIM + 16

ADAM_LR, ADAM_B1, ADAM_B2, ADAM_EPS, ADAM_WD, ADAM_STEP = 0.001, 0.9, 0.999, 1e-08, 0.01, 10

V7X_VMEM_BYTES = 64 * 2**20
VMEM_LIMIT = V7X_VMEM_BYTES - 4 * 2**20
LANES = 128
MESH = pl.DeviceIdType.MESH
N_CHIPS = 4


def _cp(*sem):
    return pltpu.CompilerParams(dimension_semantics=sem if sem else None, vmem_limit_bytes=VMEM_LIMIT)


def _dot(a, b):
    return jnp.dot(a, b, preferred_element_type=F32)


def _dot_nt(a, b):
    return lax.dot_general(a, b, (((1,), (1,)), ((), ())), preferred_element_type=F32)


def _dot_tn(a, b):
    return lax.dot_general(a, b, (((0,), (0,)), ((), ())), preferred_element_type=F32)


def _dot_hi(a, b):
    return jnp.dot(a, b, preferred_element_type=F32, precision=lax.Precision.HIGHEST)


def _group_sum(a, ones):
    hi = a.astype(BF16)
    lo = (a - hi.astype(F32)).astype(BF16)
    return _dot(hi, ones) + _dot(lo, ones)


def _dot_nt_hi(a, b):
    return lax.dot_general(a, b, (((1,), (1,)), ((), ())), preferred_element_type=F32, precision=lax.Precision.HIGHEST)


def _sig(z):
    return 1.0 / (1.0 + jnp.exp(-z))


def _full(shape, once=False):
    nd = len(shape)
    return pl.BlockSpec(shape, lambda *_: (0,) * nd, pipeline_mode=pl.Buffered(1) if once else None)


def _rows(tm, width):
    return pl.BlockSpec((tm, width), lambda i: (i, 0))


def _sds(shape, dtype):
    return jax.ShapeDtypeStruct(shape, dtype)


def rms_fwd(x, g, gather=()):
    s, d = x.shape
    tm = min(s, 512)
    nt = s // tm
    ng = len(gather)

    def body(x_ref, g_ref, *rest):
        g_in, h_ref, g_out = rest[:ng], rest[ng], rest[ng + 1:2 * ng + 1]
        if ng:
            start, forward, finish = gather_stages([a.shape for a in gather], g_in, g_out, *rest[2 * ng + 1:])
            pl.when(pl.program_id(0) == 0)(start)
        xf = x_ref[...]
        r = lax.rsqrt(jnp.mean(xf * xf, axis=-1, keepdims=True) + EPS)
        h_ref[...] = ((xf * r) * g_ref[...]).astype(BF16)
        if ng:
            @pl.when(pl.program_id(0) == nt - 1)
            def _():
                forward()
                finish()

    out = pl.pallas_call(
        body, out_shape=(_sds((s, d), BF16),) + tuple(_sds((N_CHIPS,) + a.shape, a.dtype) for a in gather), grid=(nt,),
        in_specs=[_rows(tm, d), _full((1, d))] + [_ANY] * ng, out_specs=(_rows(tm, d),) + (_ANY,) * ng,
        scratch_shapes=gather_sems(ng) if ng else [],
        compiler_params=_cp("arbitrary"), name="rms_fwd_gather" if ng else "rms_fwd")(x, g, *gather)
    return out[0], list(out[1:])


def proj_fwd(h, w_t, gather=()):
    s, d = h.shape
    n = w_t.shape[0]
    tm = min(s, 1024)
    tn = 2304
    nj, ni = n // tn, s // tm
    ng = len(gather)

    def body(h_ref, w_ref, *rest):
        g_in, o_ref, g_out = rest[:ng], rest[ng], rest[ng + 1:2 * ng + 1]
        step = pl.program_id(0) * ni + pl.program_id(1)
        if ng:
            start, forward, finish = gather_stages([a.shape for a in gather], g_in, g_out, *rest[2 * ng + 1:])
            pl.when(step == 0)(start)
            pl.when(step == (3 * nj * ni) // 4)(forward)
        o_ref[...] = _dot_nt(h_ref[...], w_ref[...]).astype(BF16)
        if ng:
            pl.when(step == nj * ni - 1)(finish)

    out = pl.pallas_call(
        body, out_shape=(_sds((s, n), BF16),) + tuple(_sds((N_CHIPS,) + a.shape, a.dtype) for a in gather), grid=(nj, ni),
        in_specs=[pl.BlockSpec((tm, d), lambda j, i: (i, 0)), pl.BlockSpec((tn, d), lambda j, i: (j, 0))] + [_ANY] * ng,
        out_specs=(pl.BlockSpec((tm, tn), lambda j, i: (i, j)),) + (_ANY,) * ng,
        scratch_shapes=gather_sems(ng) if ng else [],
        compiler_params=_cp("arbitrary", "arbitrary") if ng else _cp("parallel", "parallel"),
        name="proj_fwd_gather" if ng else "proj_fwd")(h, w_t, *gather)
    return out[0], list(out[1:])


def rope_tables(seq):
    n_freq = A_HEAD_DIM // 4
    d = jnp.arange(LANES) % A_HEAD_DIM
    seg, half, freq = d // (2 * n_freq), (d % (2 * n_freq)) // n_freq, d % n_freq
    inv = ROPE_THETA ** (-freq.astype(F32) / n_freq)
    t = jnp.arange(seq)
    pos = jnp.where(seg[None, :] == 0, (t // GRID_W)[:, None], (t % GRID_W)[:, None]).astype(F32)
    ang = pos * inv[None, :]
    cos, sin = jnp.cos(ang), jnp.sin(ang)
    return cos, jnp.where(half[None, :] == 1, sin, 0.0), jnp.where(half[None, :] == 0, -sin, 0.0)


def _group_ones(width, group):
    i = jnp.arange(width)
    return (i[:, None] // group == i[None, :] // group).astype(F32)


def _rope(xn, c, sa, sb):
    w = xn.shape[1]
    return xn * c + pltpu.roll(xn, 16, 1) * sa + pltpu.roll(xn, w - 16, 1) * sb


def _rope_t(dy, c, sa, sb):
    w = dy.shape[1]
    return dy * c + pltpu.roll(dy * sa, w - 16, 1) + pltpu.roll(dy * sb, 16, 1)


def _tile4(t):
    return jnp.concatenate([t, t, t, t], axis=1)


def qk_prep(proj, tabs, qg, kg, gq, gk):
    s = proj.shape[0]
    tm = min(s, 1024)
    c, sa, sb = tabs

    def body(p_ref, c_ref, sa_ref, sb_ref, qg_ref, kg_ref, gq_ref, gk_ref, qt_ref, kr_ref, vb_ref, kt_ref, v0_ref, v1_ref):
        xq = p_ref[:, O_QA:O_QA + A_WIDTH].astype(F32)
        xk = p_ref[:, O_KA:O_KA + A_KV_WIDTH].astype(F32)
        xv = p_ref[:, O_VA:O_VA + A_KV_WIDTH].astype(F32)
        cc, ssa, ssb = c_ref[...], sa_ref[...], sb_ref[...]
        msq = _group_sum(xq * xq, gq_ref[...]) * (1.0 / A_HEAD_DIM)
        qn = (xq * lax.rsqrt(msq + EPS)) * qg_ref[...]
        qr = _rope(qn, _tile4(cc), _tile4(ssa), _tile4(ssb)) * Q_SCALE
        qt_ref[...] = qr.T.astype(BF16)
        msk = _group_sum(xk * xk, gk_ref[...]) * (1.0 / A_HEAD_DIM)
        kn = (xk * lax.rsqrt(msk + EPS)) * kg_ref[...]
        kr = _rope(kn, cc, ssa, ssb)
        kr_ref[...] = kr.astype(BF16)
        vb_ref[...] = xv.astype(BF16)
        kt_ref[...] = kr.T.astype(BF16)
        vt = xv.T.astype(BF16)
        one = jnp.ones((VTE_ROWS - A_HEAD_DIM, tm), BF16)
        v0_ref[...] = jnp.concatenate([vt[:A_HEAD_DIM], one], axis=0)
        v1_ref[...] = jnp.concatenate([vt[A_HEAD_DIM:], one], axis=0)

    tab = _rows(tm, LANES)
    colb = lambda w: pl.BlockSpec((w, tm), lambda i: (0, i))
    return pl.pallas_call(
        body,
        out_shape=(_sds((A_WIDTH, s), BF16), _sds((s, A_KV_WIDTH), BF16), _sds((s, A_KV_WIDTH), BF16),
                   _sds((A_KV_WIDTH, s), BF16), _sds((VTE_ROWS, s), BF16), _sds((VTE_ROWS, s), BF16)),
        grid=(s // tm,),
        in_specs=[_rows(tm, PBLK), tab, tab, tab, _full((1, A_WIDTH)), _full((1, A_KV_WIDTH)),
                  _full((A_WIDTH, A_WIDTH)), _full((A_KV_WIDTH, A_KV_WIDTH))],
        out_specs=(colb(A_WIDTH), _rows(tm, A_KV_WIDTH), _rows(tm, A_KV_WIDTH), colb(A_KV_WIDTH), colb(VTE_ROWS), colb(VTE_ROWS)),
        compiler_params=_cp("parallel"), name="qk_prep")(proj, c, sa, sb, qg, kg, gq, gk)


def _pad_head(q_h, kv):
    z = jnp.zeros_like(q_h)
    return jnp.concatenate([q_h, z], axis=0) if kv == 0 else jnp.concatenate([z, q_h], axis=0)


def attn_fwd(q_t, kr, vte0, vte1, gather=()):
    s = kr.shape[0]
    tq = min(s, 512)
    kc = min(s, 256)
    nkc = s // kc
    nq = s // tq
    grp = A_HEADS // A_KV_HEADS
    ng = len(gather)

    def body(qt_ref, kr_ref, v0_ref, v1_ref, *rest):
        g_in, (o_ref, lse_ref), g_out = rest[:ng], rest[ng:ng + 2], rest[ng + 2:2 * ng + 2]
        qp_ref, m_ref, acc_ref = rest[2 * ng + 2:2 * ng + 5]
        if ng:
            start, forward, finish = gather_stages([g.shape for g in gather], g_in, g_out, *rest[2 * ng + 5:])
            pl.when(pl.program_id(0) == 0)(start)
            pl.when(pl.program_id(0) == (3 * nq) // 4)(forward)

        for h in range(A_HEADS):
            qp_ref[h] = _pad_head(qt_ref[A_HEAD_DIM * h:A_HEAD_DIM * (h + 1), :], h // grp)
        m_ref[...] = jnp.full(m_ref.shape, -1e30, F32)
        acc_ref[...] = jnp.zeros_like(acc_ref)

        def step(ci, carry):
            ks = pl.ds(pl.multiple_of(ci * kc, kc), kc)
            kblk = kr_ref[ks, :]
            vts = (v0_ref[:, ks], v1_ref[:, ks])
            scs = [_dot(kblk, qp_ref[h]) for h in range(A_HEADS)]
            for h in range(A_HEADS):
                sc = scs[h]
                m_prev = m_ref[h:h + 1, :]
                m_new = jnp.maximum(m_prev, jnp.max(sc, axis=0, keepdims=True))
                p = jnp.exp2(sc - m_new)
                acc_ref[h] = acc_ref[h] * jnp.exp2(m_prev - m_new) + _dot(vts[h // grp], p.astype(BF16))
                m_ref[h:h + 1, :] = m_new
            return carry

        lax.fori_loop(0, nkc, step, 0)
        outs, lses = [], []
        for h in range(A_HEADS):
            acc = acc_ref[h]
            l = acc[A_HEAD_DIM:A_HEAD_DIM + 1, :]
            outs.append(acc[:A_HEAD_DIM, :] / l)
            lses.append(m_ref[h:h + 1, :] + jnp.log2(l))
        o_ref[...] = jnp.concatenate(outs, axis=0).T
        lse_ref[...] = jnp.concatenate(lses, axis=0)
        if ng:
            pl.when(pl.program_id(0) == nq - 1)(finish)

    out = pl.pallas_call(
        body,
        out_shape=(_sds((s, A_WIDTH), F32), _sds((A_HEADS, s), F32)) + tuple(_sds((N_CHIPS,) + g.shape, g.dtype) for g in gather),
        grid=(nq,),
        in_specs=[pl.BlockSpec((A_WIDTH, tq), lambda i: (0, i)), _full((s, A_KV_WIDTH)), _full((VTE_ROWS, s)),
                  _full((VTE_ROWS, s))] + [_ANY] * ng,
        out_specs=(_rows(tq, A_WIDTH), pl.BlockSpec((A_HEADS, tq), lambda i: (0, i))) + (_ANY,) * ng,
        scratch_shapes=[pltpu.VMEM((A_HEADS, A_KV_WIDTH, tq), BF16), pltpu.VMEM((A_HEADS, tq), F32),
                        pltpu.VMEM((A_HEADS, VTE_ROWS, tq), F32)] + (gather_sems(ng) if ng else []),
        compiler_params=_cp("arbitrary"), name="attn_fwd_gather" if ng else "attn_fwd")(q_t, kr, vte0, vte1, *gather)
    return out[0], out[1], list(out[2:])


def memkv_fwd(mem, g, w_kv):
    m, d = mem.shape

    def body(mem_ref, g_ref, w_ref, mn_ref, kv_ref):
        mf = mem_ref[...]
        r = lax.rsqrt(jnp.mean(mf * mf, axis=-1, keepdims=True) + EPS)
        mn = ((mf * r) * g_ref[...]).astype(BF16)
        mn_ref[...] = mn
        kv_ref[...] = _dot(mn, w_ref[...]).astype(BF16)

    return pl.pallas_call(
        body, out_shape=(_sds((m, d), BF16), _sds((m, 2 * M_WIDTH), BF16)),
        compiler_params=_cp(), name="memkv_fwd")(mem, g, w_kv)


def _layer_norm_stats(v):
    mu = jnp.mean(v, axis=-1, keepdims=True)
    xc = v - mu
    rstd = lax.rsqrt(jnp.mean(xc * xc, axis=-1, keepdims=True) + EPS)
    return xc * rstd, rstd


def _spatial_mix(vlb, ws_ref, bsb_ref, tm):
    rows = []
    for ci in range(tm // CHUNK):
        cols = []
        for g in range(B_GROUPS):
            blk = vlb[ci * CHUNK:(ci + 1) * CHUNK, g * B_GROUP_DIM:(g + 1) * B_GROUP_DIM]
            cols.append(_dot(ws_ref[g], blk) + bsb_ref[g])
        rows.append(jnp.concatenate(cols, axis=1))
    return jnp.concatenate(rows, axis=0)


def _mem_attn(qm, kv_ref):
    out = []
    for h in range(M_HEADS):
        qh = qm[:, h * M_HEAD_DIM:(h + 1) * M_HEAD_DIM].astype(BF16)
        kh = kv_ref[:, h * M_HEAD_DIM:(h + 1) * M_HEAD_DIM]
        vh = kv_ref[:, M_WIDTH + h * M_HEAD_DIM:M_WIDTH + (h + 1) * M_HEAD_DIM]
        sc = _dot_nt(qh, kh) * (M_HEAD_DIM ** -0.5)
        e = jnp.exp(sc - jnp.max(sc, axis=-1, keepdims=True))
        p = e / jnp.sum(e, axis=-1, keepdims=True)
        out.append((p, _dot(p.astype(BF16), vh)))
    return out


def branch_fwd(x, proj, o_a, kv, ws, bsb, ln_g, ln_b, w_br, w_out, next_g, tgt=None):
    s, d = x.shape
    tm = min(s, 512)
    last = tgt is not None

    def body(x_ref, p_ref, oa_ref, kv_ref, ws_ref, bsb_ref, lg_ref, lb_ref, wbr_ref, wo_ref, ng_ref, *rest):
        y_ref, up_ref, mg_ref = rest[-5:-2] if not last else rest[-6:-3]
        seg = lambda o, w: p_ref[:, o:o + w].astype(F32)
        z_a, u_b, v_b, z_b = seg(O_ZA, A_WIDTH), seg(O_UB, B_WIDTH), seg(O_VB, B_WIDTH), seg(O_ZB, B_WIDTH)
        q_m, z_m = seg(O_QM, M_WIDTH), seg(O_ZM, M_WIDTH)
        xhat, _ = _layer_norm_stats(v_b)
        vln = xhat * lg_ref[...] + lb_ref[...]
        mixed = _spatial_mix(vln.astype(BF16), ws_ref, bsb_ref, tm)
        y_b = (u_b * mixed) * (z_b * _sig(z_b))
        o_m = jnp.concatenate([o for _, o in _mem_attn(q_m, kv_ref)], axis=1)
        y_a = oa_ref[...] * (z_a * _sig(z_a))
        y_m = o_m * (z_m * _sig(z_m))
        merged = None
        for n, yy in enumerate((y_a, y_b, y_m)):
            yb = yy.astype(BF16)
            y_ref[n] = yb
            up = jnp.concatenate([_dot(yb, wbr_ref[c, n]) for c in range(N_CHIPS)], axis=1)
            up_ref[n] = up.astype(BF16)
            t = _sig(seg(O_LG + n * d, d)) * up
            merged = t if merged is None else merged + t
        mb = merged.astype(BF16)
        mg_ref[...] = mb
        xn = x_ref[...] + _dot(mb, wo_ref[...])
        r = lax.rsqrt(jnp.mean(xn * xn, axis=-1, keepdims=True) + EPS)
        xh = xn * r
        g = ng_ref[...]
        if not last:
            xn_ref, hn_ref = rest[-2:]
            xn_ref[...] = xn
            hn_ref[...] = (xh * g).astype(BF16)
        else:
            t_ref, (ls_ref, dx_ref, gg_ref) = rest[0], rest[-3:]

            @pl.when(pl.program_id(0) == 0)
            def _():
                ls_ref[...] = jnp.zeros_like(ls_ref)
                gg_ref[...] = jnp.zeros_like(gg_ref)

            e = xh * g - t_ref[...]
            sq = jnp.sum(jnp.sum(e * e, axis=0, keepdims=True), axis=1, keepdims=True)
            ls_ref[...] += jnp.broadcast_to(sq, ls_ref.shape)
            dy = e * (1.0 / d)
            gg_ref[...] += jnp.sum(dy * xh, axis=0, keepdims=True)
            gy = dy * g
            dx_ref[...] = r * (gy - xh * jnp.mean(gy * xh, axis=-1, keepdims=True))

    saved_shapes = (_sds((N_BRANCH, s, A_WIDTH), BF16), _sds((N_BRANCH, s, d), BF16), _sds((s, d), BF16))
    saved_specs = (pl.BlockSpec((N_BRANCH, tm, A_WIDTH), lambda i: (0, i, 0)), pl.BlockSpec((N_BRANCH, tm, d), lambda i: (0, i, 0)),
                   _rows(tm, d))
    if last:
        tail_shapes, tail_specs = (_sds((1, LANES), F32), _sds((s, d), F32), _sds((1, d), F32)), (_full((1, LANES)), _rows(tm, d), _full((1, d)))
    else:
        tail_shapes, tail_specs = (_sds((s, d), F32), _sds((s, d), BF16)), (_rows(tm, d), _rows(tm, d))
    return pl.pallas_call(
        body, out_shape=saved_shapes + tail_shapes, grid=(s // tm,),
        in_specs=[_rows(tm, d), _rows(tm, IN_WIDTH), _rows(tm, A_WIDTH), _full(kv.shape), _full(ws.shape), _full(bsb.shape),
                  _full((1, B_WIDTH)), _full((1, B_WIDTH)), _full(w_br.shape), _full(w_out.shape), _full((1, d))]
        + ([_rows(tm, d)] if last else []),
        out_specs=saved_specs + tail_specs,
        compiler_params=_cp("arbitrary" if last else "parallel"), name="branch_fwd_loss" if last else "branch_fwd")(
            x, proj, o_a, kv, ws, bsb, ln_g, ln_b, w_br, w_out, next_g, *([tgt] if last else []))


def _pblocks(tm, first, count):
    return [pl.BlockSpec((tm, PBLK), functools.partial(lambda i, b: (i, b), b=first + k)) for k in range(count)]


def merge_bwd(dx, proj, y, up, merged, w_br, w_out):
    s, d = dx.shape
    tm = min(s, 512)
    nlg = LG_W // PBLK
    cw = d // N_CHIPS

    def body(dx_ref, l0, l1, l2, l3, y_ref, up_ref, mg_ref, wbr_ref, wo_ref, dy_ref, dlg_ref, gwo_ref, gwb_ref, gwo16_ref, gwb16_ref):
        @pl.when(pl.program_id(0) == 0)
        def _():
            gwo_ref[...] = jnp.zeros_like(gwo_ref)
            gwb_ref[...] = jnp.zeros_like(gwb_ref)

        dxb = dx_ref[...].astype(BF16)
        dmg = _dot_nt(dxb, wo_ref[...])
        gwo_ref[...] += _dot_tn(mg_ref[...], dxb)
        lg = jnp.concatenate([l0[...], l1[...], l2[...], l3[...]], axis=1).astype(F32)
        for n in range(N_BRANCH):
            g = _sig(lg[:, n * d:(n + 1) * d])
            dup = dmg * g
            dlg_ref[:, n * d:(n + 1) * d] = ((dup * up_ref[n].astype(F32)) * (1.0 - g)).astype(BF16)
            dupb = dup.astype(BF16)
            dyn = None
            for c in range(N_CHIPS):
                blk = dupb[:, c * cw:(c + 1) * cw]
                gwb_ref[c, n] += _dot_tn(y_ref[n], blk)
                t = _dot_nt(blk, wbr_ref[c, n])
                dyn = t if dyn is None else dyn + t
            dy_ref[n] = dyn.astype(BF16)

        @pl.when(pl.program_id(0) == pl.num_programs(0) - 1)
        def _():
            gwo16_ref[...] = gwo_ref[...].astype(BF16)
            gwb16_ref[...] = gwb_ref[...].astype(BF16)

    return pl.pallas_call(
        body,
        out_shape=(_sds((N_BRANCH, s, A_WIDTH), BF16), _sds((s, LG_W), BF16), _sds((d, d), F32), _sds(w_br.shape, F32),
                   _sds((d, d), BF16), _sds(w_br.shape, BF16)),
        grid=(s // tm,),
        in_specs=[_rows(tm, d)] + _pblocks(tm, O_LG // PBLK, nlg) + [
            pl.BlockSpec((N_BRANCH, tm, A_WIDTH), lambda i: (0, i, 0)), pl.BlockSpec((N_BRANCH, tm, d), lambda i: (0, i, 0)),
            _rows(tm, d), _full(w_br.shape, once=True), _full(w_out.shape, once=True)],
        out_specs=(pl.BlockSpec((N_BRANCH, tm, A_WIDTH), lambda i: (0, i, 0)), _rows(tm, LG_W), _full((d, d)), _full(w_br.shape),
                   _full((d, d)), _full(w_br.shape)),
        compiler_params=_cp("arbitrary"), name="merge_bwd")(dx, proj, proj, proj, proj, y, up, merged, w_br, w_out)


def _dsilu(z, sg):
    return sg * (1.0 + z * (1.0 - sg))


def branch_bwd(dy, proj, o_a, kv, ws, ws_t, bsb, ln_g, ln_b, head_sel):
    s = proj.shape[0]
    tm = min(s, 512)
    nmid = MID_W // PBLK

    def body(dy_ref, m0, m1, m2, m3, oa_ref, kv_ref, ws_ref, wst_ref, bsb_ref, lg_ref, lb_ref, sel_ref,
             dmid_ref, dot_ref, dl_ref, gws_ref, gbs_ref, glg_ref, glb_ref, dkv_ref):
        @pl.when(pl.program_id(0) == 0)
        def _():
            for r in (gws_ref, gbs_ref, glg_ref, glb_ref, dkv_ref):
                r[...] = jnp.zeros_like(r)

        mid = jnp.concatenate([m0[...], m1[...], m2[...], m3[...]], axis=1).astype(F32)
        seg = lambda o, w: mid[:, o - O_ZA:o - O_ZA + w]
        z_a, u_b, v_b, z_b = seg(O_ZA, A_WIDTH), seg(O_UB, B_WIDTH), seg(O_VB, B_WIDTH), seg(O_ZB, B_WIDTH)
        q_m, z_m = seg(O_QM, M_WIDTH), seg(O_ZM, M_WIDTH)

        def put(o, v):
            dmid_ref[:, o - O_ZA:o - O_ZA + v.shape[1]] = v.astype(BF16)

        dy_a, dy_b, dy_m = dy_ref[0].astype(F32), dy_ref[1].astype(F32), dy_ref[2].astype(F32)

        o_a_ = oa_ref[...]
        sg = _sig(z_a)
        do_a = dy_a * (z_a * sg)
        put(O_ZA, (dy_a * o_a_) * _dsilu(z_a, sg))
        do_l = do_a * LN2
        dot_ref[...] = do_l.T.astype(BF16)
        dl_ref[...] = _dot_nt_hi(sel_ref[...], do_l * o_a_)

        xhat, rstd = _layer_norm_stats(v_b)
        lng = lg_ref[...]
        vln = xhat * lng + lb_ref[...]
        vlb = vln.astype(BF16)
        mixed = _spatial_mix(vlb, ws_ref, bsb_ref, tm)
        sg = _sig(z_b)
        sl = z_b * sg
        put(O_UB, (dy_b * mixed) * sl)
        put(O_ZB, ((dy_b * u_b) * mixed) * _dsilu(z_b, sg))
        dmix = (dy_b * u_b) * sl
        dmb = dmix.astype(BF16)
        rows = []
        for ci in range(tm // CHUNK):
            cols = []
            for g in range(B_GROUPS):
                rs, cs = slice(ci * CHUNK, (ci + 1) * CHUNK), slice(g * B_GROUP_DIM, (g + 1) * B_GROUP_DIM)
                gws_ref[g] += _dot_nt(dmb[rs, cs], vlb[rs, cs])
                gbs_ref[g] += jnp.broadcast_to(jnp.sum(dmix[rs, cs], axis=1, keepdims=True), (CHUNK, B_GROUP_DIM))
                cols.append(_dot(wst_ref[g], dmb[rs, cs]))
            rows.append(jnp.concatenate(cols, axis=1))
        dvln = jnp.concatenate(rows, axis=0)
        glg_ref[...] += jnp.sum(dvln * xhat, axis=0, keepdims=True)
        glb_ref[...] += jnp.sum(dvln, axis=0, keepdims=True)
        gy = dvln * lng
        put(O_VB, rstd * ((gy - jnp.mean(gy, axis=-1, keepdims=True)) - xhat * jnp.mean(gy * xhat, axis=-1, keepdims=True)))

        sg = _sig(z_m)
        sl = z_m * sg
        heads = _mem_attn(q_m, kv_ref)
        o_m = jnp.concatenate([o for _, o in heads], axis=1)
        put(O_ZM, (dy_m * o_m) * _dsilu(z_m, sg))
        do_m = dy_m * sl
        dqs = []
        for h, (p, o_h) in enumerate(heads):
            hs = slice(h * M_HEAD_DIM, (h + 1) * M_HEAD_DIM)
            vs = slice(M_WIDTH + h * M_HEAD_DIM, M_WIDTH + (h + 1) * M_HEAD_DIM)
            do_h = do_m[:, hs]
            dob = do_h.astype(BF16)
            dp = _dot_nt(dob, kv_ref[:, vs])
            dsc = (p * (dp - jnp.sum(do_h * o_h, axis=-1, keepdims=True))) * (M_HEAD_DIM ** -0.5)
            dsb = dsc.astype(BF16)
            dqs.append(_dot(dsb, kv_ref[:, hs]))
            dkv_ref[:, hs] += _dot_tn(dsb, q_m[:, hs].astype(BF16))
            dkv_ref[:, vs] += _dot_tn(p.astype(BF16), dob)
        put(O_QM, jnp.concatenate(dqs, axis=1))

    return pl.pallas_call(
        body,
        out_shape=(_sds((s, MID_W), BF16), _sds((A_WIDTH, s), BF16), _sds((A_HEADS, s), F32), _sds(ws.shape, F32),
                   _sds(ws.shape, F32), _sds((1, B_WIDTH), F32), _sds((1, B_WIDTH), F32), _sds(kv.shape, F32)),
        grid=(s // tm,),
        in_specs=[pl.BlockSpec((N_BRANCH, tm, A_WIDTH), lambda i: (0, i, 0))] + _pblocks(tm, O_ZA // PBLK, nmid) + [
            _rows(tm, A_WIDTH), _full(kv.shape), _full(ws.shape), _full(ws.shape), _full(bsb.shape),
            _full((1, B_WIDTH)), _full((1, B_WIDTH)), _full(head_sel.shape)],
        out_specs=(_rows(tm, MID_W), pl.BlockSpec((A_WIDTH, tm), lambda i: (0, i)), pl.BlockSpec((A_HEADS, tm), lambda i: (0, i)),
                   _full(ws.shape), _full(ws.shape), _full((1, B_WIDTH)), _full((1, B_WIDTH)), _full(kv.shape)),
        compiler_params=_cp("arbitrary"), name="branch_bwd")(dy, proj, proj, proj, proj, o_a, kv, ws, ws_t, bsb, ln_g, ln_b, head_sel)


def attn_bwd(q_t, do_t, kr, kr_t, vb, lse, delta, scatter=()):
    s = kr.shape[0]
    tq = min(s, 256)
    kc = min(s, 512)
    nkc = s // kc
    nq = s // tq
    grp = A_HEADS // A_KV_HEADS
    ns = len(scatter)
    na = ns // 2

    def body(qt_ref, dot_ref, kr_ref, krt_ref, vb_ref, lse_ref, dl_ref, *rest):
        s_in, (dqt_ref, dk_ref, dv_ref), s_out = rest[:ns], rest[ns:ns + 3], rest[ns + 3:2 * ns + 3]
        qp_ref, dop_ref, dq_ref = rest[2 * ns + 3:2 * ns + 6]
        if ns:
            start, finish = scatter_stages([g.shape[1:] for g in scatter[:na]], s_in[:na], s_in[na:], s_out[:na], s_out[na:],
                                           *rest[2 * ns + 6:])
            pl.when(pl.program_id(0) == 0)(start)

        @pl.when(pl.program_id(0) == 0)
        def _():
            dk_ref[...] = jnp.zeros_like(dk_ref)
            dv_ref[...] = jnp.zeros_like(dv_ref)

        for h in range(A_HEADS):
            hs = slice(A_HEAD_DIM * h, A_HEAD_DIM * (h + 1))
            qp_ref[h] = _pad_head(qt_ref[hs, :], h // grp)
            dop_ref[h] = _pad_head(dot_ref[hs, :], h // grp)
        dq_ref[...] = jnp.zeros_like(dq_ref)

        def step(ci, carry):
            ks = pl.ds(pl.multiple_of(ci * kc, kc), kc)
            kblk, vblk, ktb = kr_ref[ks, :], vb_ref[ks, :], krt_ref[:, ks]
            dv_acc = jnp.zeros((kc, A_KV_WIDTH), F32)
            dk_acc = jnp.zeros((kc, A_KV_WIDTH), F32)
            scs = [_dot(kblk, qp_ref[h]) for h in range(A_HEADS)]
            dps = [_dot(vblk, dop_ref[h]) for h in range(A_HEADS)]
            for h in range(A_HEADS):
                qpad, dopad = qp_ref[h], dop_ref[h]
                p = jnp.exp2(scs[h] - lse_ref[h:h + 1, :])
                dsb = (p * (dps[h] - dl_ref[h:h + 1, :])).astype(BF16)
                dv_acc = dv_acc + _dot_nt(p.astype(BF16), dopad)
                dk_acc = dk_acc + _dot_nt(dsb, qpad)
                dq_ref[h] += _dot(ktb, dsb)
            dv_ref[ks, :] += dv_acc
            dk_ref[ks, :] += dk_acc
            return carry

        lax.fori_loop(0, nkc, step, 0)
        dqt_ref[...] = jnp.concatenate(
            [dq_ref[h][A_HEAD_DIM * (h // grp):A_HEAD_DIM * (h // grp + 1), :] for h in range(A_HEADS)], axis=0)
        if ns:
            pl.when(pl.program_id(0) == nq - 1)(finish)

    colq = pl.BlockSpec((A_WIDTH, tq), lambda i: (0, i))
    colh = pl.BlockSpec((A_HEADS, tq), lambda i: (0, i))
    out = pl.pallas_call(
        body,
        out_shape=(_sds((A_WIDTH, s), F32), _sds((s, A_KV_WIDTH), F32), _sds((s, A_KV_WIDTH), F32)) + scatter_out_shapes(scatter[:na]),
        grid=(nq,),
        in_specs=[colq, colq, _full((s, A_KV_WIDTH)), _full((A_KV_WIDTH, s)), _full((s, A_KV_WIDTH)), colh, colh] + [_ANY] * ns,
        out_specs=(colq, _full((s, A_KV_WIDTH)), _full((s, A_KV_WIDTH))) + (_ANY,) * ns,
        scratch_shapes=[pltpu.VMEM((A_HEADS, A_KV_WIDTH, tq), BF16), pltpu.VMEM((A_HEADS, A_KV_WIDTH, tq), BF16),
                        pltpu.VMEM((A_HEADS, A_KV_WIDTH, tq), F32)] + (scatter_sems(na) if ns else []),
        compiler_params=_cp("arbitrary"), name="attn_bwd_scatter" if ns else "attn_bwd")(
            q_t, do_t, kr, kr_t, vb, lse, delta, *scatter)
    return out[0], out[1], out[2], list(out[3:3 + na]), list(out[3 + na:])


def qk_prep_bwd(proj, dq_t, dkr, dvb, tabs, qg, kg, gq, gk, fold_q, fold_k):
    s = proj.shape[0]
    tm = min(s, 1024)
    c, sa, sb = tabs

    def head_norm_bwd(x, dn, gain, gones, fold):
        ms = _group_sum(x * x, gones) * (1.0 / A_HEAD_DIM)
        r = lax.rsqrt(ms + EPS)
        xh = x * r
        gg = _dot_hi(jnp.sum(dn * xh, axis=0, keepdims=True), fold)
        u = dn * gain
        mean_u = _group_sum(u * xh, gones) * (1.0 / A_HEAD_DIM)
        return r * (u - xh * mean_u), gg

    def body(p_ref, dqt_ref, dk_ref, dv_ref, c_ref, sa_ref, sb_ref, qg_ref, kg_ref, gq_ref, gk_ref, fq_ref, fk_ref,
             dqkv_ref, gqg_ref, gkg_ref):
        @pl.when(pl.program_id(0) == 0)
        def _():
            gqg_ref[...] = jnp.zeros_like(gqg_ref)
            gkg_ref[...] = jnp.zeros_like(gkg_ref)

        cc, ssa, ssb = c_ref[...], sa_ref[...], sb_ref[...]
        dqr = dqt_ref[...].T * Q_SCALE
        dqn = _rope_t(dqr, _tile4(cc), _tile4(ssa), _tile4(ssb))
        dxq, gq_ = head_norm_bwd(p_ref[:, O_QA:O_QA + A_WIDTH].astype(F32), dqn, qg_ref[...], gq_ref[...], fq_ref[...])
        dkn = _rope_t(dk_ref[...], cc, ssa, ssb)
        dxk, gk_ = head_norm_bwd(p_ref[:, O_KA:O_KA + A_KV_WIDTH].astype(F32), dkn, kg_ref[...], gk_ref[...], fk_ref[...])
        gqg_ref[...] += gq_
        gkg_ref[...] += gk_
        dqkv_ref[:, O_QA:O_QA + A_WIDTH] = dxq.astype(BF16)
        dqkv_ref[:, O_KA:O_KA + A_KV_WIDTH] = dxk.astype(BF16)
        dqkv_ref[:, O_VA:O_VA + A_KV_WIDTH] = (dv_ref[...] * (1.0 / LN2)).astype(BF16)

    tab = _rows(tm, LANES)
    return pl.pallas_call(
        body, out_shape=(_sds((s, PBLK), BF16), _sds((1, LANES), F32), _sds((1, LANES), F32)), grid=(s // tm,),
        in_specs=[_rows(tm, PBLK), pl.BlockSpec((A_WIDTH, tm), lambda i: (0, i)), _rows(tm, A_KV_WIDTH), _rows(tm, A_KV_WIDTH),
                  tab, tab, tab, _full((1, A_WIDTH)), _full((1, A_KV_WIDTH)), _full((A_WIDTH, A_WIDTH)),
                  _full((A_KV_WIDTH, A_KV_WIDTH)), _full((A_WIDTH, LANES)), _full((A_KV_WIDTH, LANES))],
        out_specs=(_rows(tm, PBLK), _full((1, LANES)), _full((1, LANES))),
        compiler_params=_cp("arbitrary"), name="qk_prep_bwd")(proj, dq_t, dkr, dvb, c, sa, sb, qg, kg, gq, gk, fold_q, fold_k)


def _pick_dproj(b, d0, d1, d2, use):
    first_lg = 1 + MID_W // PBLK

    @pl.when(b == 0)
    def _():
        use(d0[...])

    @pl.when(jnp.logical_and(b >= 1, b < first_lg))
    def _():
        use(d1[...])

    @pl.when(b >= first_lg)
    def _():
        use(d2[...])


def win_grad(d0, d1, d2, h):
    s, d = h.shape
    tk = min(s, 4096)
    nk = s // tk

    def body(d0_ref, d1_ref, d2_ref, h_ref, o_ref, o16_ref):
        @pl.when(pl.program_id(1) == 0)
        def _():
            o_ref[...] = jnp.zeros_like(o_ref)

        def use(blk):
            o_ref[...] += _dot_tn(blk, h_ref[...])

        _pick_dproj(pl.program_id(0), d0_ref, d1_ref, d2_ref, use)

        @pl.when(pl.program_id(1) == nk - 1)
        def _():
            o16_ref[...] = o_ref[...].astype(BF16)

    def spec(first, count):
        def imap(j, k):
            used = jnp.logical_and(j >= first, j < first + count)
            return (jnp.where(used, k, 0), jnp.clip(j - first, 0, count - 1))
        return pl.BlockSpec((tk, PBLK), imap)

    nm = MID_W // PBLK
    oblk = pl.BlockSpec((PBLK, d), lambda j, k: (j, 0))
    return pl.pallas_call(
        body, out_shape=(_sds((IN_WIDTH, d), F32), _sds((IN_WIDTH, d), BF16)), grid=(N_PBLK, nk),
        in_specs=[spec(0, 1), spec(1, nm), spec(1 + nm, LG_W // PBLK),
                  pl.BlockSpec((tk, d), lambda j, k: (k, 0), pipeline_mode=pl.Buffered(1) if nk == 1 else None)],
        out_specs=(oblk, oblk),
        compiler_params=_cp("parallel", "arbitrary"), name="win_grad")(d0, d1, d2, h)


def h_bwd(d0, d1, d2, w_t, x, dx_out, g, scatter=(), swap=()):
    s, d = x.shape
    tm = min(s, 512)
    nt = s // tm
    ns = len(scatter)
    na = ns // 2
    nw = len(swap)

    def body(d0_ref, d1_ref, d2_ref, w_ref, x_ref, dxo_ref, g_ref, *rest):
        s_in, (dx_ref, gg_ref), s_out = rest[:ns], rest[ns + nw:ns + nw + 2], rest[ns + nw + 2:2 * ns + nw + 2]
        w_out, sems = rest[2 * ns + nw + 2:2 * (ns + nw) + 2], rest[2 * (ns + nw) + 2:]
        stages = []
        if ns:
            stages.append(scatter_stages([a.shape[1:] for a in scatter[:na]], s_in[:na], s_in[na:], s_out[:na], s_out[na:], *sems[:2]))
        if nw:
            stages.append(swap_stages([a.shape[0] for a in swap], w_out, *sems[-2:]))
        for start, _ in stages:
            pl.when(pl.program_id(0) == 0)(start)

        @pl.when(pl.program_id(0) == 0)
        def _():
            gg_ref[...] = jnp.zeros_like(gg_ref)

        dh = (_dot(d0_ref[...], w_ref[0:PBLK, :]) + _dot(d1_ref[...], w_ref[PBLK:PBLK + MID_W, :])
              + _dot(d2_ref[...], w_ref[PBLK + MID_W:, :]))
        xf = x_ref[...]
        r = lax.rsqrt(jnp.mean(xf * xf, axis=-1, keepdims=True) + EPS)
        xh = xf * r
        gg_ref[...] += jnp.sum(dh * xh, axis=0, keepdims=True)
        u = dh * g_ref[...]
        dx_ref[...] = dxo_ref[...] + r * (u - xh * jnp.mean(u * xh, axis=-1, keepdims=True))
        for _, finish in stages:
            pl.when(pl.program_id(0) == nt - 1)(finish)

    rowb = _rows(tm, d)
    out = pl.pallas_call(
        body, out_shape=(_sds((s, d), F32), _sds((1, d), F32)) + scatter_out_shapes(scatter[:na])
        + tuple(_sds(a.shape, F32) for a in swap), grid=(nt,),
        in_specs=[_rows(tm, PBLK), _rows(tm, MID_W), _rows(tm, LG_W),
                  pl.BlockSpec(w_t.shape, lambda i: (0, 0), pipeline_mode=pl.Buffered(1)), rowb, rowb, _full((1, d))]
        + [_ANY] * (ns + nw),
        out_specs=(rowb, _full((1, d))) + (_ANY,) * (ns + nw),
        input_output_aliases={7 + ns + a: 2 + ns + a for a in range(nw)},
        scratch_shapes=(scatter_sems(na) if ns else []) + (swap_sems(nw) if nw else []),
        compiler_params=_cp("arbitrary"), name="h_bwd_scatter" if ns else "h_bwd")(d0, d1, d2, w_t, x, dx_out, g, *scatter, *swap)
    return out[0], out[1], list(out[2:2 + na]), list(out[2 + na:2 + ns]), list(out[2 + ns:])


def memkv_bwd(mem, g, mem_n, w_kv, dkv):
    m, d = mem.shape

    def body(mem_ref, g_ref, mn_ref, w_ref, dkv_ref, gw_ref, gw16_ref, gg_ref):
        dkb = dkv_ref[...].astype(BF16)
        gw = _dot_tn(mn_ref[...], dkb)
        gw_ref[...] = gw
        gw16_ref[...] = gw.astype(BF16)
        dmn = _dot_nt(dkb, w_ref[...])
        mf = mem_ref[...]
        r = lax.rsqrt(jnp.mean(mf * mf, axis=-1, keepdims=True) + EPS)
        gg_ref[...] = jnp.sum(dmn * (mf * r), axis=0, keepdims=True)

    return pl.pallas_call(
        body, out_shape=(_sds(w_kv.shape, F32), _sds(w_kv.shape, BF16), _sds((1, d), F32)),
        compiler_params=_cp(), name="memkv_bwd")(mem, g, mem_n, w_kv, dkv)


def _layer_consts(seq):
    i = jnp.arange(A_WIDTH)
    return dict(
        tabs=rope_tables(seq),
        gq=_group_ones(A_WIDTH, A_HEAD_DIM).astype(BF16), gk=_group_ones(A_KV_WIDTH, A_HEAD_DIM).astype(BF16),
        fold_q=(i[:, None] % A_HEAD_DIM == jnp.arange(LANES)[None, :]).astype(F32),
        fold_k=(i[:A_KV_WIDTH, None] % A_HEAD_DIM == jnp.arange(LANES)[None, :]).astype(F32),
        head_sel=(jnp.arange(A_HEADS)[:, None] == i[None, :] // A_HEAD_DIM).astype(F32),
    )


_BIG = ("win_t", "wkv", "wbr", "wout")


def _with_own_part(names, gathered, shards, chip, d):
    shape = dict(win_t=(IN_WIDTH, d), wkv=(d, 2 * M_WIDTH), wbr=(N_CHIPS, N_BRANCH, A_WIDTH, d // N_CHIPS), wout=(d, d))
    return {n: lax.dynamic_update_slice(g, sh[None], (chip, 0, 0)).reshape(shape[n]) for n, g, sh in zip(names, gathered, shards)}


def local_fwd_bwd(x, mem, tgt, small, big=None, shards=None, place=None):
    s, d = x.shape
    depth = small["norm_g"].shape[0]
    k = _layer_consts(s)
    row = lambda v: v.reshape(1, -1)
    dist = shards is not None
    if dist:
        big = [None] * depth
    saved = []
    for l in range(depth):
        ng = row(small["norm_g"][l])
        qg = row(jnp.tile(small["q_norm_g"][l], A_HEADS))
        kg = row(jnp.tile(small["k_norm_g"][l], A_KV_HEADS))
        ws = small["w_s"][l].astype(BF16)
        ws_t = jnp.swapaxes(small["w_s"][l], 1, 2).astype(BF16)
        bsb = jnp.broadcast_to(small["b_s"][l][:, :, None], (B_GROUPS, CHUNK, B_GROUP_DIM))
        lng, lnb = row(small["sg_ln_g"][l]), row(small["sg_ln_b"][l])
        mg = row(small["mem_norm_g"][l])
        if l == 0:
            first = tuple(shards[0][:1]) if dist else ()
            h, gathered = rms_fwd(x, ng, gather=first)
            if dist:
                big[0] = _with_own_part(_BIG[:1], gathered, first, place[0], d)
        else:
            h = h_next
        w = big[l]
        late = tuple(shards[0][1:]) if dist and l == 0 else ()
        proj, gathered = proj_fwd(h, w["win_t"], gather=late)
        if late:
            w.update(_with_own_part(_BIG[1:], gathered, late, place[0], d))
        q_t, kr, vb, kr_t, vte0, vte1 = qk_prep(proj, k["tabs"], qg, kg, k["gq"], k["gk"])
        nxt = tuple(shards[l + 1]) if dist and l + 1 < depth else ()
        o_a, lse, gathered = attn_fwd(q_t, kr, vte0, vte1, gather=nxt)
        if nxt:
            big[l + 1] = _with_own_part(_BIG, gathered, nxt, place[0], d)
        mem_n, kv = memkv_fwd(mem, mg, w["wkv"])
        x_in = x
        if l + 1 < depth:
            y, up, merged, x, h_next = branch_fwd(x, proj, o_a, kv, ws, bsb, lng, lnb, w["wbr"], w["wout"], row(small["norm_g"][l + 1]))
        else:
            y, up, merged, sq, dx, g_final = branch_fwd(x, proj, o_a, kv, ws, bsb, lng, lnb, w["wbr"], w["wout"],
                                                        row(small["final_g"]), tgt=tgt)
        saved.append(dict(x=x_in, ng=ng, qg=qg, kg=kg, ws=ws, ws_t=ws_t, bsb=bsb, lng=lng, lnb=lnb, mg=mg, h=h, proj=proj,
                          q_t=q_t, kr=kr, kr_t=kr_t, vb=vb, o_a=o_a, lse=lse, mem_n=mem_n, kv=kv, y=y, up=up, merged=merged))

    grads = {n: [None] * depth for n in ("norm_g", "q_norm_g", "k_norm_g", "sg_ln_g", "sg_ln_b", "w_s", "b_s", "mem_norm_g")}
    parts = lambda g: g.reshape(N_CHIPS, -1, g.shape[-1])
    reduced = [[None] * len(_BIG) for _ in range(depth)]

    def reduce_all(items, t_sib, t_rem):
        if items:
            for (ll, a, _, _), f in zip(items, reduce_rows(place, [i[2] for i in items], t_sib, t_rem)):
                reduced[ll][a] = f

    as_scatter = lambda items: tuple(i[2] for i in items) + tuple(i[3] for i in items)
    pending = []
    for l in reversed(range(depth)):
        sv, w = saved[l], big[l]
        dy, dlg, g_wout, g_wbr, g_wout16, g_wbr16 = merge_bwd(dx, sv["proj"], sv["y"], sv["up"], sv["merged"], w["wbr"], w["wout"])
        dmid, do_t, delta, g_ws, g_bs, g_lng, g_lnb, dkv = branch_bwd(
            dy, sv["proj"], sv["o_a"], sv["kv"], sv["ws"], sv["ws_t"], sv["bsb"], sv["lng"], sv["lnb"], k["head_sel"])
        g_wkv, g_wkv16, g_mg = memkv_bwd(mem, sv["mg"], sv["mem_n"], w["wkv"], dkv)
        if dist:
            pending += [(l, 1, parts(g_wkv), parts(g_wkv16)), (l, 2, parts(g_wbr), parts(g_wbr16)), (l, 3, parts(g_wout), parts(g_wout16))]
        dq_t, dkr, dvb, t_sib, t_rem = attn_bwd(sv["q_t"], do_t, sv["kr"], sv["kr_t"], sv["vb"], sv["lse"], delta,
                                                scatter=as_scatter(pending))
        reduce_all(pending, t_sib, t_rem)
        dqkv, g_qg, g_kg = qk_prep_bwd(sv["proj"], dq_t, dkr, dvb, k["tabs"], sv["qg"], sv["kg"], k["gq"], k["gk"],
                                       k["fold_q"], k["fold_k"])
        g_win, g_win16 = win_grad(dqkv, dmid, dlg, sv["h"])
        pending = [(l, 0, parts(g_win), parts(g_win16))] if dist else []
        last = as_scatter(pending) if l == 0 else ()
        done = [(ll, a) for ll in range(depth) for a in range(len(_BIG)) if reduced[ll][a] is not None] if last else []
        dx, g_ng, t_sib, t_rem, swapped = h_bwd(dqkv, dmid, dlg, w["win_t"], sv["x"], dx, sv["ng"], scatter=last,
                                                swap=tuple(reduced[ll][a] for ll, a in done))
        for (ll, a), f in zip(done, swapped):
            reduced[ll][a] = f
        if last:
            reduce_all(pending, t_sib, t_rem)
        grads["norm_g"][l] = g_ng[0]
        grads["q_norm_g"][l] = g_qg[0, :A_HEAD_DIM]
        grads["k_norm_g"][l] = g_kg[0, :A_HEAD_DIM]
        grads["sg_ln_g"][l] = g_lng[0]
        grads["sg_ln_b"][l] = g_lnb[0]
        grads["w_s"][l] = g_ws
        grads["b_s"][l] = g_bs[:, :, 0]
        grads["mem_norm_g"][l] = g_mg[0]
        if not dist:
            reduced[l] = dict(zip(_BIG, (parts(g_win), parts(g_wkv), parts(g_wbr), parts(g_wout))))
    grads = {n: jnp.stack(v) for n, v in grads.items()}
    grads["final_g"] = g_final[0]
    return sq[0, 0], dx, grads, reduced


def _row_block(rows, width, cap_bytes=2 * 2**20):
    best = None
    for br in range(8, rows + 1, 8):
        if rows % br == 0 and br * width * 4 <= cap_bytes:
            best = br
    return best if best is not None else rows


def adamw(w, gs, m, v):
    r, c = w.shape
    n = len(gs)
    rs = r // n
    br = _row_block(rs, c)
    nb = rs // br

    def body(w_ref, *refs):
        g_refs, (m_ref, v_ref, og_ref, d_ref, nm_ref, nv_ref) = refs[:n], refs[n:]

        def update(gg):
            mm = ADAM_B1 * m_ref[...] + (1.0 - ADAM_B1) * gg
            vv = ADAM_B2 * v_ref[...] + (1.0 - ADAM_B2) * (gg * gg)
            m_hat = mm / (1.0 - ADAM_B1 ** ADAM_STEP)
            v_hat = vv / (1.0 - ADAM_B2 ** ADAM_STEP)
            og_ref[...] = gg
            d_ref[...] = -ADAM_LR * (m_hat / (jnp.sqrt(v_hat) + ADAM_EPS) + ADAM_WD * w_ref[...])
            nm_ref[...] = mm
            nv_ref[...] = vv

        for k in range(n):
            pl.when(pl.program_id(0) == k)(functools.partial(lambda k: update(g_refs[k][...]), k))

    blk = pl.BlockSpec((br, c), lambda l, i: (l * nb + i, 0))
    g_specs = [pl.BlockSpec((br, c), functools.partial(lambda l, i, k: (jnp.where(l == k, i, 0), 0), k=k)) for k in range(n)]
    return pl.pallas_call(
        body, out_shape=(_sds((r, c), F32),) * 4, grid=(n, nb), in_specs=[blk] + g_specs + [blk, blk], out_specs=(blk,) * 4,
        compiler_params=_cp("arbitrary", "arbitrary"), name="adamw")(w, *gs, m, v)


N_REMOTE = 2 * (N_CHIPS - 1)


def reduce_rows(place, gs, t_sibs, t_rems):
    n = len(gs)
    nt = 2

    def body(place_ref, *refs):
        for a in range(n):
            g_ref, s_ref, t_ref, f_ref = refs[a], refs[n + a], refs[2 * n + a], refs[3 * n + a]
            acc = g_ref[...] + s_ref[...]
            for j in range(N_REMOTE):
                acc = acc + t_ref[j].astype(F32)
            f_ref[...] = acc

    tiles = [(g.shape[1] // 2 // nt, g.shape[2]) for g in gs]
    return pl.pallas_call(
        body, out_shape=tuple(_sds(g.shape[1:], F32) for g in gs),
        grid_spec=pltpu.PrefetchScalarGridSpec(
            num_scalar_prefetch=1, grid=(nt,),
            in_specs=[pl.BlockSpec((None, tr, c), lambda i, p: (p[0], p[1] * nt + i, 0)) for tr, c in tiles]
            + [pl.BlockSpec((tr, c), lambda i, p: (i, 0)) for tr, c in tiles]
            + [pl.BlockSpec((N_REMOTE, tr, c), lambda i, p: (0, i, 0)) for tr, c in tiles],
            out_specs=tuple(pl.BlockSpec((tr, c), lambda i, p: (p[1] * nt + i, 0)) for tr, c in tiles)),
        compiler_params=_cp("parallel"), name="reduce_rows")(place, *gs, *t_sibs, *t_rems)


_ANY = pl.BlockSpec(memory_space=pl.ANY)


def _place():
    x, y, c = lax.axis_index("x"), lax.axis_index("y"), lax.axis_index("c")
    chips = [(1 - x, y), (x, 1 - y), (1 - x, 1 - y)]
    return x, y, c, chips


def gather_sems(n):
    return [pltpu.SemaphoreType.DMA((n, N_REMOTE)), pltpu.SemaphoreType.DMA((n, N_REMOTE))]


def gather_stages(shapes, ins, outs, send, recv):
    n = len(shapes)
    x, y, c, chips = _place()
    me = 2 * x + y
    sib = (x, y, 1 - c)

    def rows(a, hl):
        r2 = shapes[a][0] // 2
        return pl.ds(hl * r2, r2)

    def remote(a, k, src, dst, dev):
        return pltpu.make_async_remote_copy(src, dst, send.at[a, k], recv.at[a, k], device_id=dev, device_id_type=MESH)

    def sent(a, k):
        cx, cy = chips[k]
        return remote(a, k, ins[a].at[rows(a, c)], outs[a].at[me, rows(a, c)], (cx, cy, c))

    def got(a, k, hl):
        cx, cy = chips[k]
        return outs[a].at[2 * cx + cy, rows(a, hl)]

    def arrived(a, k):
        return remote(a, k, got(a, k, c), got(a, k, c), (*chips[k], c))

    def passed(a, k, hl):
        return remote(a, 3 + k, got(a, k, hl), got(a, k, hl), sib)

    def start():
        for a in range(n):
            for k in range(3):
                sent(a, k).start()

    def forward():
        for k in range(3):
            for a in range(n):
                arrived(a, k).wait_recv()
                passed(a, k, c).start()

    def finish():
        for k in range(3):
            for a in range(n):
                passed(a, k, 1 - c).wait_recv()
        for k in range(3):
            for a in range(n):
                sent(a, k).wait_send()
                passed(a, k, c).wait_send()

    return start, forward, finish


def scatter_sems(n):
    return [pltpu.SemaphoreType.DMA((n, N_REMOTE + 1)), pltpu.SemaphoreType.DMA((n, N_REMOTE + 1))]


def scatter_out_shapes(gs):
    return (tuple(_sds((g.shape[1] // 2, g.shape[2]), F32) for g in gs)
            + tuple(_sds((N_REMOTE, g.shape[1] // 2, g.shape[2]), BF16) for g in gs))


def scatter_stages(shapes, gf, gb, t_sib, t_rem, send, recv):
    n = len(shapes)
    x, y, c, chips = _place()
    me = 2 * x + y

    def copies():
        out = []
        for a in range(n):
            r2 = shapes[a][0] // 2
            out.append(pltpu.make_async_remote_copy(gf[a].at[me, pl.ds((1 - c) * r2, r2)], t_sib[a], send.at[a, N_REMOTE],
                                                    recv.at[a, N_REMOTE], device_id=(x, y, 1 - c), device_id_type=MESH))
            for k, (cx, cy) in enumerate(chips):
                for o in range(2):
                    tc = c if o == 0 else 1 - c
                    out.append(pltpu.make_async_remote_copy(gb[a].at[2 * cx + cy, pl.ds(tc * r2, r2)], t_rem[a].at[2 * k + o],
                                                            send.at[a, 2 * k + o], recv.at[a, 2 * k + o],
                                                            device_id=(cx, cy, tc), device_id_type=MESH))
        return out

    def start():
        for cp in copies():
            cp.start()

    def finish():
        for cp in copies():
            cp.wait()

    return start, finish


def swap_sems(n):
    return [pltpu.SemaphoreType.DMA((n,)), pltpu.SemaphoreType.DMA((n,))]


def swap_stages(rows, bufs, send, recv):
    x, y, c, _ = _place()

    def copies(core):
        return [pltpu.make_async_remote_copy(b.at[pl.ds(core * (r // 2), r // 2)], b.at[pl.ds(core * (r // 2), r // 2)],
                                             send.at[a], recv.at[a], device_id=(x, y, 1 - c), device_id_type=MESH)
                for a, (r, b) in enumerate(zip(rows, bufs))]

    def start():
        for cp in copies(c):
            cp.start()

    def finish():
        for mine, theirs in zip(copies(c), copies(1 - c)):
            mine.wait_send()
            theirs.wait_recv()

    return start, finish


def finish_exchange(vs, fs):
    n, nv = len(fs), len(vs)
    ndev = 2 * N_CHIPS

    def body(*refs):
        v_refs, out, sum_refs = refs[:nv], refs[nv + n:nv + 2 * n], refs[nv + 2 * n:2 * (nv + n)]
        all_refs, (send, recv, loc, fsend, frecv) = refs[2 * (nv + n):3 * nv + 2 * n], refs[3 * nv + 2 * n:]
        x, y, c, chips = _place()
        me, sib = (x, y, c), (x, y, 1 - c)
        start_swaps, finish_swaps = swap_stages([f.shape[0] for f in fs], out, fsend, frecv)
        start_swaps()

        def slab(b, px, py, pc):
            return all_refs[b].at[4 * px + 2 * py + pc]

        def copy(b, k, block, to, own=False):
            return pltpu.make_async_remote_copy(v_refs[b] if own else slab(b, *block), slab(b, *block), send.at[b, k],
                                                recv.at[b, k], device_id=to, device_id_type=MESH)

        each = range(nv)
        mine = [pltpu.make_async_copy(v_refs[b], slab(b, *me), loc.at[b]) for b in each]
        first = ([copy(b, 0, me, sib, own=True) for b in each]
                 + [copy(b, 1 + j, me, (*chip, c), own=True) for j, chip in enumerate(chips) for b in each])
        for cp in mine + first:
            cp.start()
        passed = [copy(b, 4 + j, (*chip, c), sib) for j, chip in enumerate(chips) for b in each]
        for j, chip in enumerate(chips):
            for b in each:
                copy(b, 1 + j, (*chip, c), me).wait_recv()
                passed[nv * j + b].start()
        for b in each:
            copy(b, 0, sib, me).wait_recv()
        for j, chip in enumerate(chips):
            for b in each:
                copy(b, 4 + j, (*chip, 1 - c), me).wait_recv()
        for cp in first + passed:
            cp.wait_send()
        for cp in mine:
            cp.wait()
        for b in each:
            acc = all_refs[b][0].astype(F32)
            for i in range(1, ndev):
                acc = acc + all_refs[b][i].astype(F32)
            sum_refs[b][...] = acc
        finish_swaps()

    vm = pl.BlockSpec(memory_space=pltpu.VMEM)
    res = pl.pallas_call(
        body, out_shape=tuple(_sds(f.shape, F32) for f in fs) + tuple(_sds(v.shape, F32) for v in vs),
        in_specs=[vm] * nv + [_ANY] * n, out_specs=(_ANY,) * n + (vm,) * nv,
        input_output_aliases={nv + a: a for a in range(n)},
        scratch_shapes=[pltpu.VMEM((ndev,) + v.shape, v.dtype) for v in vs]
        + [pltpu.SemaphoreType.DMA((nv, 7)), pltpu.SemaphoreType.DMA((nv, 7)), pltpu.SemaphoreType.DMA((nv,))] + swap_sems(n),
        compiler_params=pltpu.CompilerParams(vmem_limit_bytes=VMEM_LIMIT), name="finish_exchange")(*vs, *fs)
    return list(res[n:]), list(res[:n])


_SMALL = ("norm_g", "q_norm_g", "k_norm_g", "sg_ln_g", "sg_ln_b", "w_s", "b_s", "mem_norm_g", "final_g")
_WEIGHTS = ("norm_g", "w_in", "q_norm_g", "k_norm_g", "sg_ln_g", "sg_ln_b", "w_s", "b_s", "mem_norm_g", "w_mem_kv", "w_br",
            "w_out", "final_g")


def _pack(d, tail=None, names=_SMALL):
    flat = jnp.concatenate([d[n].reshape(-1) for n in names] + ([tail.reshape(1)] if tail is not None else []))
    rows = -(-(sum(d[n].size for n in names) + 1) // (8 * LANES)) * 8
    return jnp.pad(flat, (0, rows * LANES - flat.shape[0])).reshape(rows, LANES)


def _unpack(p, like, names=_SMALL):
    flat, out, o = p.reshape(-1), {}, 0
    for n in names:
        out[n] = flat[o:o + like[n].size].reshape(like[n].shape)
        o += like[n].size
    return out


def kernel(x, mem, norm_g, w_in, q_norm_g, k_norm_g, sg_ln_g, sg_ln_b, w_s, b_s, mem_norm_g, w_mem_kv, w_br, w_out, final_g, loss_target, m_norm_g, m_w_in, m_q_norm_g, m_k_norm_g, m_sg_ln_g, m_sg_ln_b, m_w_s, m_b_s, m_mem_norm_g, m_w_mem_kv, m_w_br, m_w_out, m_final_g, v_norm_g, v_w_in, v_q_norm_g, v_k_norm_g, v_sg_ln_g, v_sg_ln_b, v_w_s, v_b_s, v_mem_norm_g, v_w_mem_kv, v_w_br, v_w_out, v_final_g):
    w = dict(norm_g=norm_g, w_in=w_in, q_norm_g=q_norm_g, k_norm_g=k_norm_g, sg_ln_g=sg_ln_g, sg_ln_b=sg_ln_b, w_s=w_s, b_s=b_s,
             mem_norm_g=mem_norm_g, w_mem_kv=w_mem_kv, w_br=w_br, w_out=w_out, final_g=final_g)
    m = dict(norm_g=m_norm_g, w_in=m_w_in, q_norm_g=m_q_norm_g, k_norm_g=m_k_norm_g, sg_ln_g=m_sg_ln_g, sg_ln_b=m_sg_ln_b,
             w_s=m_w_s, b_s=m_b_s, mem_norm_g=m_mem_norm_g, w_mem_kv=m_w_mem_kv, w_br=m_w_br, w_out=m_w_out, final_g=m_final_g)
    v = dict(norm_g=v_norm_g, w_in=v_w_in, q_norm_g=v_q_norm_g, k_norm_g=v_k_norm_g, sg_ln_g=v_sg_ln_g, sg_ln_b=v_sg_ln_b,
             w_s=v_w_s, b_s=v_b_s, mem_norm_g=v_mem_norm_g, w_mem_kv=v_w_mem_kv, w_br=v_w_br, w_out=v_w_out, final_g=v_final_g)
    depth, d = norm_g.shape
    nsh = N_CHIPS
    br_rows = N_BRANCH * A_WIDTH
    br_cols = d // nsh

    shards = [[jnp.swapaxes(w_in[l], 0, 1).astype(BF16), w_mem_kv[l].astype(BF16), w_br[l].astype(BF16).reshape(br_rows, br_cols),
               w_out[l].astype(BF16)] for l in range(depth)]
    place = jnp.stack([2 * lax.axis_index("x") + lax.axis_index("y"), lax.axis_index("c")]).astype(jnp.int32)
    small = {n: w[n] for n in _SMALL}

    sq, dx, grads, reduced = local_fwd_bwd(x[0], mem[0], loss_target[0], small, shards=shards, place=place)

    narrow = tuple(n for n in _SMALL if n != "w_s")
    (sum_narrow, sum_ws), reduced[0][:1] = finish_exchange(
        [_pack(grads, tail=sq, names=narrow), grads["w_s"].astype(BF16).reshape(-1, LANES)], reduced[0][:1])
    finals = [g for layer in reduced for g in layer]
    loss = (0.5 / d) * sum_narrow.reshape(-1)[sum(small[n].size for n in narrow)]
    big_grads = dict(zip(("w_in", "w_mem_kv", "w_br", "w_out"), [finals[a::len(_BIG)] for a in range(len(_BIG))]))
    small_grads = dict(_unpack(sum_narrow, small, names=narrow), w_s=sum_ws.reshape(w_s.shape))
    small_sum = _pack(small_grads)

    out_g, out_d, out_m, out_v = {}, {}, {}, {}
    _, sd, sm, sv = adamw(_pack(small), [small_sum], _pack({n: m[n] for n in _SMALL}), _pack({n: v[n] for n in _SMALL}))
    sd, sm, sv = _unpack(sd, small), _unpack(sm, small), _unpack(sv, small)
    for n in _SMALL:
        out_g[n], out_d[n], out_m[n], out_v[n] = small_grads[n], sd[n], sm[n], sv[n]
    for n, gs in big_grads.items():
        into = (lambda a: jnp.swapaxes(a, 1, 2)) if n == "w_in" else (lambda a: a)
        two_d = lambda a: a.reshape(-1, gs[0].shape[-1])
        res = adamw(two_d(into(w[n])), gs, two_d(into(m[n])), two_d(into(v[n])))
        out_g[n], out_d[n], out_m[n], out_v[n] = [into(t.reshape(into(w[n]).shape)) for t in res]
    return (loss, dx[None], *[out_g[n] for n in _WEIGHTS], *[out_d[n] for n in _WEIGHTS], *[out_m[n] for n in _WEIGHTS],
            *[out_v[n] for n in _WEIGHTS])
```

```python
import functools

import jax
import jax.numpy as jnp
from jax import lax
from jax.experimental import pallas as pl
from jax.experimental.pallas import tpu as pltpu

F32 = jnp.float32
BF16 = jnp.bfloat16

GRID_W = 64
CHUNK = 128
ROPE_THETA = 10000.0
EPS = 1e-6
A_HEADS, A_KV_HEADS, A_HEAD_DIM = 8, 2, 64
A_WIDTH, A_KV_WIDTH = 512, 128
B_GROUPS, B_GROUP_DIM, B_WIDTH = 4, 128, 512
M_HEADS, M_HEAD_DIM, M_WIDTH = 4, 128, 512
N_BRANCH = 3
IN_WIDTH = 6912
O_QA, O_KA, O_VA, O_ZA, O_UB, O_VB, O_ZB, O_QM, O_ZM, O_LG = 0, 512, 640, 768, 1280, 1792, 2304, 2816, 3328, 3840
PBLK = 768
N_PBLK = IN_WIDTH // PBLK
MID_W = 3072
LG_W = 3072

LN2 = 0.6931471805599453
Q_SCALE = A_HEAD_DIM ** -0.5 / LN2
VTE_ROWS = A_HEAD_DIM + 16

ADAM_LR, ADAM_B1, ADAM_B2, ADAM_EPS, ADAM_WD, ADAM_STEP = 0.001, 0.9, 0.999, 1e-08, 0.01, 10

V7X_VMEM_BYTES = 64 * 2**20
VMEM_LIMIT = V7X_VMEM_BYTES - 4 * 2**20
LANES = 128
MESH = pl.DeviceIdType.MESH
N_CHIPS = 4


def _cp(*sem):
    return pltpu.CompilerParams(dimension_semantics=sem if sem else None, vmem_limit_bytes=VMEM_LIMIT)


def _dot(a, b):
    return jnp.dot(a, b, preferred_element_type=F32)


def _dot_nt(a, b):
    return lax.dot_general(a, b, (((1,), (1,)), ((), ())), preferred_element_type=F32)


def _dot_tn(a, b):
    return lax.dot_general(a, b, (((0,), (0,)), ((), ())), preferred_element_type=F32)


def _dot_hi(a, b):
    return jnp.dot(a, b, preferred_element_type=F32, precision=lax.Precision.HIGHEST)


def _group_sum(a, ones):
    hi = a.astype(BF16)
    lo = (a - hi.astype(F32)).astype(BF16)
    return _dot(hi, ones) + _dot(lo, ones)


def _dot_nt_hi(a, b):
    return lax.dot_general(a, b, (((1,), (1,)), ((), ())), preferred_element_type=F32, precision=lax.Precision.HIGHEST)


def _sig(z):
    return 1.0 / (1.0 + jnp.exp(-z))


def _full(shape, once=False):
    nd = len(shape)
    return pl.BlockSpec(shape, lambda *_: (0,) * nd, pipeline_mode=pl.Buffered(1) if once else None)


def _rows(tm, width):
    return pl.BlockSpec((tm, width), lambda i: (i, 0))


def _sds(shape, dtype):
    return jax.ShapeDtypeStruct(shape, dtype)


def rms_fwd(x, g, gather=()):
    s, d = x.shape
    tm = min(s, 512)
    nt = s // tm
    ng = len(gather)

    def body(x_ref, g_ref, *rest):
        g_in, h_ref, g_out = rest[:ng], rest[ng], rest[ng + 1:2 * ng + 1]
        if ng:
            start, forward, finish = gather_stages([a.shape for a in gather], g_in, g_out, *rest[2 * ng + 1:])
            pl.when(pl.program_id(0) == 0)(start)
        xf = x_ref[...]
        r = lax.rsqrt(jnp.mean(xf * xf, axis=-1, keepdims=True) + EPS)
        h_ref[...] = ((xf * r) * g_ref[...]).astype(BF16)
        if ng:
            @pl.when(pl.program_id(0) == nt - 1)
            def _():
                forward()
                finish()

    out = pl.pallas_call(
        body, out_shape=(_sds((s, d), BF16),) + tuple(_sds((N_CHIPS,) + a.shape, a.dtype) for a in gather), grid=(nt,),
        in_specs=[_rows(tm, d), _full((1, d))] + [_ANY] * ng, out_specs=(_rows(tm, d),) + (_ANY,) * ng,
        scratch_shapes=gather_sems(ng) if ng else [],
        compiler_params=_cp("arbitrary"), name="rms_fwd_gather" if ng else "rms_fwd")(x, g, *gather)
    return out[0], list(out[1:])


def proj_fwd(h, w_t, gather=()):
    s, d = h.shape
    n = w_t.shape[0]
    tm = min(s, 1024)
    tn = 2304
    nj, ni = n // tn, s // tm
    ng = len(gather)

    def body(h_ref, w_ref, *rest):
        g_in, o_ref, g_out = rest[:ng], rest[ng], rest[ng + 1:2 * ng + 1]
        step = pl.program_id(0) * ni + pl.program_id(1)
        if ng:
            start, forward, finish = gather_stages([a.shape for a in gather], g_in, g_out, *rest[2 * ng + 1:])
            pl.when(step == 0)(start)
            pl.when(step == (3 * nj * ni) // 4)(forward)
        o_ref[...] = _dot_nt(h_ref[...], w_ref[...]).astype(BF16)
        if ng:
            pl.when(step == nj * ni - 1)(finish)

    out = pl.pallas_call(
        body, out_shape=(_sds((s, n), BF16),) + tuple(_sds((N_CHIPS,) + a.shape, a.dtype) for a in gather), grid=(nj, ni),
        in_specs=[pl.BlockSpec((tm, d), lambda j, i: (i, 0)), pl.BlockSpec((tn, d), lambda j, i: (j, 0))] + [_ANY] * ng,
        out_specs=(pl.BlockSpec((tm, tn), lambda j, i: (i, j)),) + (_ANY,) * ng,
        scratch_shapes=gather_sems(ng) if ng else [],
        compiler_params=_cp("arbitrary", "arbitrary") if ng else _cp("parallel", "parallel"),
        name="proj_fwd_gather" if ng else "proj_fwd")(h, w_t, *gather)
    return out[0], list(out[1:])


def rope_tables(seq):
    n_freq = A_HEAD_DIM // 4
    d = jnp.arange(LANES) % A_HEAD_DIM
    seg, half, freq = d // (2 * n_freq), (d % (2 * n_freq)) // n_freq, d % n_freq
    inv = ROPE_THETA ** (-freq.astype(F32) / n_freq)
    t = jnp.arange(seq)
    pos = jnp.where(seg[None, :] == 0, (t // GRID_W)[:, None], (t % GRID_W)[:, None]).astype(F32)
    ang = pos * inv[None, :]
    cos, sin = jnp.cos(ang), jnp.sin(ang)
    return cos, jnp.where(half[None, :] == 1, sin, 0.0), jnp.where(half[None, :] == 0, -sin, 0.0)


def _group_ones(width, group):
    i = jnp.arange(width)
    return (i[:, None] // group == i[None, :] // group).astype(F32)


def _rope(xn, c, sa, sb):
    w = xn.shape[1]
    return xn * c + pltpu.roll(xn, 16, 1) * sa + pltpu.roll(xn, w - 16, 1) * sb


def _rope_t(dy, c, sa, sb):
    w = dy.shape[1]
    return dy * c + pltpu.roll(dy * sa, w - 16, 1) + pltpu.roll(dy * sb, 16, 1)


def _tile4(t):
    return jnp.concatenate([t, t, t, t], axis=1)


def qk_prep(proj, tabs, qg, kg, gq, gk):
    s = proj.shape[0]
    tm = min(s, 1024)
    c, sa, sb = tabs

    def body(p_ref, c_ref, sa_ref, sb_ref, qg_ref, kg_ref, gq_ref, gk_ref, qt_ref, kr_ref, vb_ref, kt_ref, v0_ref, v1_ref):
        xq = p_ref[:, O_QA:O_QA + A_WIDTH].astype(F32)
        xk = p_ref[:, O_KA:O_KA + A_KV_WIDTH].astype(F32)
        xv = p_ref[:, O_VA:O_VA + A_KV_WIDTH].astype(F32)
        cc, ssa, ssb = c_ref[...], sa_ref[...], sb_ref[...]
        msq = _group_sum(xq * xq, gq_ref[...]) * (1.0 / A_HEAD_DIM)
        qn = (xq * lax.rsqrt(msq + EPS)) * qg_ref[...]
        qr = _rope(qn, _tile4(cc), _tile4(ssa), _tile4(ssb)) * Q_SCALE
        qt_ref[...] = qr.T.astype(BF16)
        msk = _group_sum(xk * xk, gk_ref[...]) * (1.0 / A_HEAD_DIM)
        kn = (xk * lax.rsqrt(msk + EPS)) * kg_ref[...]
        kr = _rope(kn, cc, ssa, ssb)
        kr_ref[...] = kr.astype(BF16)
        vb_ref[...] = xv.astype(BF16)
        kt_ref[...] = kr.T.astype(BF16)
        vt = xv.T.astype(BF16)
        one = jnp.ones((VTE_ROWS - A_HEAD_DIM, tm), BF16)
        v0_ref[...] = jnp.concatenate([vt[:A_HEAD_DIM], one], axis=0)
        v1_ref[...] = jnp.concatenate([vt[A_HEAD_DIM:], one], axis=0)

    tab = _rows(tm, LANES)
    colb = lambda w: pl.BlockSpec((w, tm), lambda i: (0, i))
    return pl.pallas_call(
        body,
        out_shape=(_sds((A_WIDTH, s), BF16), _sds((s, A_KV_WIDTH), BF16), _sds((s, A_KV_WIDTH), BF16),
                   _sds((A_KV_WIDTH, s), BF16), _sds((VTE_ROWS, s), BF16), _sds((VTE_ROWS, s), BF16)),
        grid=(s // tm,),
        in_specs=[_rows(tm, PBLK), tab, tab, tab, _full((1, A_WIDTH)), _full((1, A_KV_WIDTH)),
                  _full((A_WIDTH, A_WIDTH)), _full((A_KV_WIDTH, A_KV_WIDTH))],
        out_specs=(colb(A_WIDTH), _rows(tm, A_KV_WIDTH), _rows(tm, A_KV_WIDTH), colb(A_KV_WIDTH), colb(VTE_ROWS), colb(VTE_ROWS)),
        compiler_params=_cp("parallel"), name="qk_prep")(proj, c, sa, sb, qg, kg, gq, gk)


def _pad_head(q_h, kv):
    z = jnp.zeros_like(q_h)
    return jnp.concatenate([q_h, z], axis=0) if kv == 0 else jnp.concatenate([z, q_h], axis=0)


def attn_fwd(q_t, kr, vte0, vte1, gather=()):
    s = kr.shape[0]
    tq = min(s, 512)
    kc = min(s, 256)
    nkc = s // kc
    nq = s // tq
    grp = A_HEADS // A_KV_HEADS
    ng = len(gather)

    def body(qt_ref, kr_ref, v0_ref, v1_ref, *rest):
        g_in, (o_ref, lse_ref), g_out = rest[:ng], rest[ng:ng + 2], rest[ng + 2:2 * ng + 2]
        qp_ref, m_ref, acc_ref = rest[2 * ng + 2:2 * ng + 5]
        if ng:
            start, forward, finish = gather_stages([g.shape for g in gather], g_in, g_out, *rest[2 * ng + 5:])
            pl.when(pl.program_id(0) == 0)(start)
            pl.when(pl.program_id(0) == (3 * nq) // 4)(forward)

        for h in range(A_HEADS):
            qp_ref[h] = _pad_head(qt_ref[A_HEAD_DIM * h:A_HEAD_DIM * (h + 1), :], h // grp)
        m_ref[...] = jnp.full(m_ref.shape, -1e30, F32)
        acc_ref[...] = jnp.zeros_like(acc_ref)

        def step(ci, carry):
            ks = pl.ds(pl.multiple_of(ci * kc, kc), kc)
            kblk = kr_ref[ks, :]
            vts = (v0_ref[:, ks], v1_ref[:, ks])
            scs = [_dot(kblk, qp_ref[h]) for h in range(A_HEADS)]
            for h in range(A_HEADS):
                sc = scs[h]
                m_prev = m_ref[h:h + 1, :]
                m_new = jnp.maximum(m_prev, jnp.max(sc, axis=0, keepdims=True))
                p = jnp.exp2(sc - m_new)
                acc_ref[h] = acc_ref[h] * jnp.exp2(m_prev - m_new) + _dot(vts[h // grp], p.astype(BF16))
                m_ref[h:h + 1, :] = m_new
            return carry

        lax.fori_loop(0, nkc, step, 0)
        outs, lses = [], []
        for h in range(A_HEADS):
            acc = acc_ref[h]
            l = acc[A_HEAD_DIM:A_HEAD_DIM + 1, :]
            outs.append(acc[:A_HEAD_DIM, :] / l)
            lses.append(m_ref[h:h + 1, :] + jnp.log2(l))
        o_ref[...] = jnp.concatenate(outs, axis=0).T
        lse_ref[...] = jnp.concatenate(lses, axis=0)
        if ng:
            pl.when(pl.program_id(0) == nq - 1)(finish)

    out = pl.pallas_call(
        body,
        out_shape=(_sds((s, A_WIDTH), F32), _sds((A_HEADS, s), F32)) + tuple(_sds((N_CHIPS,) + g.shape, g.dtype) for g in gather),
        grid=(nq,),
        in_specs=[pl.BlockSpec((A_WIDTH, tq), lambda i: (0, i)), _full((s, A_KV_WIDTH)), _full((VTE_ROWS, s)),
                  _full((VTE_ROWS, s))] + [_ANY] * ng,
        out_specs=(_rows(tq, A_WIDTH), pl.BlockSpec((A_HEADS, tq), lambda i: (0, i))) + (_ANY,) * ng,
        scratch_shapes=[pltpu.VMEM((A_HEADS, A_KV_WIDTH, tq), BF16), pltpu.VMEM((A_HEADS, tq), F32),
                        pltpu.VMEM((A_HEADS, VTE_ROWS, tq), F32)] + (gather_sems(ng) if ng else []),
        compiler_params=_cp("arbitrary"), name="attn_fwd_gather" if ng else "attn_fwd")(q_t, kr, vte0, vte1, *gather)
    return out[0], out[1], list(out[2:])


def memkv_fwd(mem, g, w_kv):
    m, d = mem.shape

    def body(mem_ref, g_ref, w_ref, mn_ref, kv_ref):
        mf = mem_ref[...]
        r = lax.rsqrt(jnp.mean(mf * mf, axis=-1, keepdims=True) + EPS)
        mn = ((mf * r) * g_ref[...]).astype(BF16)
        mn_ref[...] = mn
        kv_ref[...] = _dot(mn, w_ref[...]).astype(BF16)

    return pl.pallas_call(
        body, out_shape=(_sds((m, d), BF16), _sds((m, 2 * M_WIDTH), BF16)),
        compiler_params=_cp(), name="memkv_fwd")(mem, g, w_kv)


def _layer_norm_stats(v):
    mu = jnp.mean(v, axis=-1, keepdims=True)
    xc = v - mu
    rstd = lax.rsqrt(jnp.mean(xc * xc, axis=-1, keepdims=True) + EPS)
    return xc * rstd, rstd


def _spatial_mix(vlb, ws_ref, bsb_ref, tm):
    rows = []
    for ci in range(tm // CHUNK):
        cols = []
        for g in range(B_GROUPS):
            blk = vlb[ci * CHUNK:(ci + 1) * CHUNK, g * B_GROUP_DIM:(g + 1) * B_GROUP_DIM]
            cols.append(_dot(ws_ref[g], blk) + bsb_ref[g])
        rows.append(jnp.concatenate(cols, axis=1))
    return jnp.concatenate(rows, axis=0)


def _mem_attn(qm, kv_ref):
    out = []
    for h in range(M_HEADS):
        qh = qm[:, h * M_HEAD_DIM:(h + 1) * M_HEAD_DIM].astype(BF16)
        kh = kv_ref[:, h * M_HEAD_DIM:(h + 1) * M_HEAD_DIM]
        vh = kv_ref[:, M_WIDTH + h * M_HEAD_DIM:M_WIDTH + (h + 1) * M_HEAD_DIM]
        sc = _dot_nt(qh, kh) * (M_HEAD_DIM ** -0.5)
        e = jnp.exp(sc - jnp.max(sc, axis=-1, keepdims=True))
        p = e / jnp.sum(e, axis=-1, keepdims=True)
        out.append((p, _dot(p.astype(BF16), vh)))
    return out


def branch_fwd(x, proj, o_a, kv, ws, bsb, ln_g, ln_b, w_br, w_out, next_g, tgt=None):
    s, d = x.shape
    tm = min(s, 512)
    last = tgt is not None

    def body(x_ref, p_ref, oa_ref, kv_ref, ws_ref, bsb_ref, lg_ref, lb_ref, wbr_ref, wo_ref, ng_ref, *rest):
        y_ref, up_ref, mg_ref = rest[-5:-2] if not last else rest[-6:-3]
        seg = lambda o, w: p_ref[:, o:o + w].astype(F32)
        z_a, u_b, v_b, z_b = seg(O_ZA, A_WIDTH), seg(O_UB, B_WIDTH), seg(O_VB, B_WIDTH), seg(O_ZB, B_WIDTH)
        q_m, z_m = seg(O_QM, M_WIDTH), seg(O_ZM, M_WIDTH)
        xhat, _ = _layer_norm_stats(v_b)
        vln = xhat * lg_ref[...] + lb_ref[...]
        mixed = _spatial_mix(vln.astype(BF16), ws_ref, bsb_ref, tm)
        y_b = (u_b * mixed) * (z_b * _sig(z_b))
        o_m = jnp.concatenate([o for _, o in _mem_attn(q_m, kv_ref)], axis=1)
        y_a = oa_ref[...] * (z_a * _sig(z_a))
        y_m = o_m * (z_m * _sig(z_m))
        merged = None
        for n, yy in enumerate((y_a, y_b, y_m)):
            yb = yy.astype(BF16)
            y_ref[n] = yb
            up = jnp.concatenate([_dot(yb, wbr_ref[c, n]) for c in range(N_CHIPS)], axis=1)
            up_ref[n] = up.astype(BF16)
            t = _sig(seg(O_LG + n * d, d)) * up
            merged = t if merged is None else merged + t
        mb = merged.astype(BF16)
        mg_ref[...] = mb
        xn = x_ref[...] + _dot(mb, wo_ref[...])
        r = lax.rsqrt(jnp.mean(xn * xn, axis=-1, keepdims=True) + EPS)
        xh = xn * r
        g = ng_ref[...]
        if not last:
            xn_ref, hn_ref = rest[-2:]
            xn_ref[...] = xn
            hn_ref[...] = (xh * g).astype(BF16)
        else:
            t_ref, (ls_ref, dx_ref, gg_ref) = rest[0], rest[-3:]

            @pl.when(pl.program_id(0) == 0)
            def _():
                ls_ref[...] = jnp.zeros_like(ls_ref)
                gg_ref[...] = jnp.zeros_like(gg_ref)

            e = xh * g - t_ref[...]
            sq = jnp.sum(jnp.sum(e * e, axis=0, keepdims=True), axis=1, keepdims=True)
            ls_ref[...] += jnp.broadcast_to(sq, ls_ref.shape)
            dy = e * (1.0 / d)
            gg_ref[...] += jnp.sum(dy * xh, axis=0, keepdims=True)
            gy = dy * g
            dx_ref[...] = r * (gy - xh * jnp.mean(gy * xh, axis=-1, keepdims=True))

    saved_shapes = (_sds((N_BRANCH, s, A_WIDTH), BF16), _sds((N_BRANCH, s, d), BF16), _sds((s, d), BF16))
    saved_specs = (pl.BlockSpec((N_BRANCH, tm, A_WIDTH), lambda i: (0, i, 0)), pl.BlockSpec((N_BRANCH, tm, d), lambda i: (0, i, 0)),
                   _rows(tm, d))
    if last:
        tail_shapes, tail_specs = (_sds((1, LANES), F32), _sds((s, d), F32), _sds((1, d), F32)), (_full((1, LANES)), _rows(tm, d), _full((1, d)))
    else:
        tail_shapes, tail_specs = (_sds((s, d), F32), _sds((s, d), BF16)), (_rows(tm, d), _rows(tm, d))
    return pl.pallas_call(
        body, out_shape=saved_shapes + tail_shapes, grid=(s // tm,),
        in_specs=[_rows(tm, d), _rows(tm, IN_WIDTH), _rows(tm, A_WIDTH), _full(kv.shape), _full(ws.shape), _full(bsb.shape),
                  _full((1, B_WIDTH)), _full((1, B_WIDTH)), _full(w_br.shape), _full(w_out.shape), _full((1, d))]
        + ([_rows(tm, d)] if last else []),
        out_specs=saved_specs + tail_specs,
        compiler_params=_cp("arbitrary" if last else "parallel"), name="branch_fwd_loss" if last else "branch_fwd")(
            x, proj, o_a, kv, ws, bsb, ln_g, ln_b, w_br, w_out, next_g, *([tgt] if last else []))


def _pblocks(tm, first, count):
    return [pl.BlockSpec((tm, PBLK), functools.partial(lambda i, b: (i, b), b=first + k)) for k in range(count)]


def merge_bwd(dx, proj, y, up, merged, w_br, w_out):
    s, d = dx.shape
    tm = min(s, 512)
    nlg = LG_W // PBLK
    cw = d // N_CHIPS

    def body(dx_ref, l0, l1, l2, l3, y_ref, up_ref, mg_ref, wbr_ref, wo_ref, dy_ref, dlg_ref, gwo_ref, gwb_ref, gwo16_ref, gwb16_ref):
        @pl.when(pl.program_id(0) == 0)
        def _():
            gwo_ref[...] = jnp.zeros_like(gwo_ref)
            gwb_ref[...] = jnp.zeros_like(gwb_ref)

        dxb = dx_ref[...].astype(BF16)
        dmg = _dot_nt(dxb, wo_ref[...])
        gwo_ref[...] += _dot_tn(mg_ref[...], dxb)
        lg = jnp.concatenate([l0[...], l1[...], l2[...], l3[...]], axis=1).astype(F32)
        for n in range(N_BRANCH):
            g = _sig(lg[:, n * d:(n + 1) * d])
            dup = dmg * g
            dlg_ref[:, n * d:(n + 1) * d] = ((dup * up_ref[n].astype(F32)) * (1.0 - g)).astype(BF16)
            dupb = dup.astype(BF16)
            dyn = None
            for c in range(N_CHIPS):
                blk = dupb[:, c * cw:(c + 1) * cw]
                gwb_ref[c, n] += _dot_tn(y_ref[n], blk)
                t = _dot_nt(blk, wbr_ref[c, n])
                dyn = t if dyn is None else dyn + t
            dy_ref[n] = dyn.astype(BF16)

        @pl.when(pl.program_id(0) == pl.num_programs(0) - 1)
        def _():
            gwo16_ref[...] = gwo_ref[...].astype(BF16)
            gwb16_ref[...] = gwb_ref[...].astype(BF16)

    return pl.pallas_call(
        body,
        out_shape=(_sds((N_BRANCH, s, A_WIDTH), BF16), _sds((s, LG_W), BF16), _sds((d, d), F32), _sds(w_br.shape, F32),
                   _sds((d, d), BF16), _sds(w_br.shape, BF16)),
        grid=(s // tm,),
        in_specs=[_rows(tm, d)] + _pblocks(tm, O_LG // PBLK, nlg) + [
            pl.BlockSpec((N_BRANCH, tm, A_WIDTH), lambda i: (0, i, 0)), pl.BlockSpec((N_BRANCH, tm, d), lambda i: (0, i, 0)),
            _rows(tm, d), _full(w_br.shape, once=True), _full(w_out.shape, once=True)],
        out_specs=(pl.BlockSpec((N_BRANCH, tm, A_WIDTH), lambda i: (0, i, 0)), _rows(tm, LG_W), _full((d, d)), _full(w_br.shape),
                   _full((d, d)), _full(w_br.shape)),
        compiler_params=_cp("arbitrary"), name="merge_bwd")(dx, proj, proj, proj, proj, y, up, merged, w_br, w_out)


def _dsilu(z, sg):
    return sg * (1.0 + z * (1.0 - sg))


def branch_bwd(dy, proj, o_a, kv, ws, ws_t, bsb, ln_g, ln_b, head_sel):
    s = proj.shape[0]
    tm = min(s, 512)
    nmid = MID_W // PBLK

    def body(dy_ref, m0, m1, m2, m3, oa_ref, kv_ref, ws_ref, wst_ref, bsb_ref, lg_ref, lb_ref, sel_ref,
             dmid_ref, dot_ref, dl_ref, gws_ref, gbs_ref, glg_ref, glb_ref, dkv_ref):
        @pl.when(pl.program_id(0) == 0)
        def _():
            for r in (gws_ref, gbs_ref, glg_ref, glb_ref, dkv_ref):
                r[...] = jnp.zeros_like(r)

        mid = jnp.concatenate([m0[...], m1[...], m2[...], m3[...]], axis=1).astype(F32)
        seg = lambda o, w: mid[:, o - O_ZA:o - O_ZA + w]
        z_a, u_b, v_b, z_b = seg(O_ZA, A_WIDTH), seg(O_UB, B_WIDTH), seg(O_VB, B_WIDTH), seg(O_ZB, B_WIDTH)
        q_m, z_m = seg(O_QM, M_WIDTH), seg(O_ZM, M_WIDTH)

        def put(o, v):
            dmid_ref[:, o - O_ZA:o - O_ZA + v.shape[1]] = v.astype(BF16)

        dy_a, dy_b, dy_m = dy_ref[0].astype(F32), dy_ref[1].astype(F32), dy_ref[2].astype(F32)

        o_a_ = oa_ref[...]
        sg = _sig(z_a)
        do_a = dy_a * (z_a * sg)
        put(O_ZA, (dy_a * o_a_) * _dsilu(z_a, sg))
        do_l = do_a * LN2
        dot_ref[...] = do_l.T.astype(BF16)
        dl_ref[...] = _dot_nt_hi(sel_ref[...], do_l * o_a_)

        xhat, rstd = _layer_norm_stats(v_b)
        lng = lg_ref[...]
        vln = xhat * lng + lb_ref[...]
        vlb = vln.astype(BF16)
        mixed = _spatial_mix(vlb, ws_ref, bsb_ref, tm)
        sg = _sig(z_b)
        sl = z_b * sg
        put(O_UB, (dy_b * mixed) * sl)
        put(O_ZB, ((dy_b * u_b) * mixed) * _dsilu(z_b, sg))
        dmix = (dy_b * u_b) * sl
        dmb = dmix.astype(BF16)
        rows = []
        for ci in range(tm // CHUNK):
            cols = []
            for g in range(B_GROUPS):
                rs, cs = slice(ci * CHUNK, (ci + 1) * CHUNK), slice(g * B_GROUP_DIM, (g + 1) * B_GROUP_DIM)
                gws_ref[g] += _dot_nt(dmb[rs, cs], vlb[rs, cs])
                gbs_ref[g] += jnp.broadcast_to(jnp.sum(dmix[rs, cs], axis=1, keepdims=True), (CHUNK, B_GROUP_DIM))
                cols.append(_dot(wst_ref[g], dmb[rs, cs]))
            rows.append(jnp.concatenate(cols, axis=1))
        dvln = jnp.concatenate(rows, axis=0)
        glg_ref[...] += jnp.sum(dvln * xhat, axis=0, keepdims=True)
        glb_ref[...] += jnp.sum(dvln, axis=0, keepdims=True)
        gy = dvln * lng
        put(O_VB, rstd * ((gy - jnp.mean(gy, axis=-1, keepdims=True)) - xhat * jnp.mean(gy * xhat, axis=-1, keepdims=True)))

        sg = _sig(z_m)
        sl = z_m * sg
        heads = _mem_attn(q_m, kv_ref)
        o_m = jnp.concatenate([o for _, o in heads], axis=1)
        put(O_ZM, (dy_m * o_m) * _dsilu(z_m, sg))
        do_m = dy_m * sl
        dqs = []
        for h, (p, o_h) in enumerate(heads):
            hs = slice(h * M_HEAD_DIM, (h + 1) * M_HEAD_DIM)
            vs = slice(M_WIDTH + h * M_HEAD_DIM, M_WIDTH + (h + 1) * M_HEAD_DIM)
            do_h = do_m[:, hs]
            dob = do_h.astype(BF16)
            dp = _dot_nt(dob, kv_ref[:, vs])
            dsc = (p * (dp - jnp.sum(do_h * o_h, axis=-1, keepdims=True))) * (M_HEAD_DIM ** -0.5)
            dsb = dsc.astype(BF16)
            dqs.append(_dot(dsb, kv_ref[:, hs]))
            dkv_ref[:, hs] += _dot_tn(dsb, q_m[:, hs].astype(BF16))
            dkv_ref[:, vs] += _dot_tn(p.astype(BF16), dob)
        put(O_QM, jnp.concatenate(dqs, axis=1))

    return pl.pallas_call(
        body,
        out_shape=(_sds((s, MID_W), BF16), _sds((A_WIDTH, s), BF16), _sds((A_HEADS, s), F32), _sds(ws.shape, F32),
                   _sds(ws.shape, F32), _sds((1, B_WIDTH), F32), _sds((1, B_WIDTH), F32), _sds(kv.shape, F32)),
        grid=(s // tm,),
        in_specs=[pl.BlockSpec((N_BRANCH, tm, A_WIDTH), lambda i: (0, i, 0))] + _pblocks(tm, O_ZA // PBLK, nmid) + [
            _rows(tm, A_WIDTH), _full(kv.shape), _full(ws.shape), _full(ws.shape), _full(bsb.shape),
            _full((1, B_WIDTH)), _full((1, B_WIDTH)), _full(head_sel.shape)],
        out_specs=(_rows(tm, MID_W), pl.BlockSpec((A_WIDTH, tm), lambda i: (0, i)), pl.BlockSpec((A_HEADS, tm), lambda i: (0, i)),
                   _full(ws.shape), _full(ws.shape), _full((1, B_WIDTH)), _full((1, B_WIDTH)), _full(kv.shape)),
        compiler_params=_cp("arbitrary"), name="branch_bwd")(dy, proj, proj, proj, proj, o_a, kv, ws, ws_t, bsb, ln_g, ln_b, head_sel)


def attn_bwd(q_t, do_t, kr, kr_t, vb, lse, delta, scatter=()):
    s = kr.shape[0]
    tq = min(s, 256)
    kc = min(s, 512)
    nkc = s // kc
    nq = s // tq
    grp = A_HEADS // A_KV_HEADS
    ns = len(scatter)
    na = ns // 2

    def body(qt_ref, dot_ref, kr_ref, krt_ref, vb_ref, lse_ref, dl_ref, *rest):
        s_in, (dqt_ref, dk_ref, dv_ref), s_out = rest[:ns], rest[ns:ns + 3], rest[ns + 3:2 * ns + 3]
        qp_ref, dop_ref, dq_ref = rest[2 * ns + 3:2 * ns + 6]
        if ns:
            start, finish = scatter_stages([g.shape[1:] for g in scatter[:na]], s_in[:na], s_in[na:], s_out[:na], s_out[na:],
                                           *rest[2 * ns + 6:])
            pl.when(pl.program_id(0) == 0)(start)

        @pl.when(pl.program_id(0) == 0)
        def _():
            dk_ref[...] = jnp.zeros_like(dk_ref)
            dv_ref[...] = jnp.zeros_like(dv_ref)

        for h in range(A_HEADS):
            hs = slice(A_HEAD_DIM * h, A_HEAD_DIM * (h + 1))
            qp_ref[h] = _pad_head(qt_ref[hs, :], h // grp)
            dop_ref[h] = _pad_head(dot_ref[hs, :], h // grp)
        dq_ref[...] = jnp.zeros_like(dq_ref)

        def step(ci, carry):
            ks = pl.ds(pl.multiple_of(ci * kc, kc), kc)
            kblk, vblk, ktb = kr_ref[ks, :], vb_ref[ks, :], krt_ref[:, ks]
            dv_acc = jnp.zeros((kc, A_KV_WIDTH), F32)
            dk_acc = jnp.zeros((kc, A_KV_WIDTH), F32)
            scs = [_dot(kblk, qp_ref[h]) for h in range(A_HEADS)]
            dps = [_dot(vblk, dop_ref[h]) for h in range(A_HEADS)]
            for h in range(A_HEADS):
                qpad, dopad = qp_ref[h], dop_ref[h]
                p = jnp.exp2(scs[h] - lse_ref[h:h + 1, :])
                dsb = (p * (dps[h] - dl_ref[h:h + 1, :])).astype(BF16)
                dv_acc = dv_acc + _dot_nt(p.astype(BF16), dopad)
                dk_acc = dk_acc + _dot_nt(dsb, qpad)
                dq_ref[h] += _dot(ktb, dsb)
            dv_ref[ks, :] += dv_acc
            dk_ref[ks, :] += dk_acc
            return carry

        lax.fori_loop(0, nkc, step, 0)
        dqt_ref[...] = jnp.concatenate(
            [dq_ref[h][A_HEAD_DIM * (h // grp):A_HEAD_DIM * (h // grp + 1), :] for h in range(A_HEADS)], axis=0)
        if ns:
            pl.when(pl.program_id(0) == nq - 1)(finish)

    colq = pl.BlockSpec((A_WIDTH, tq), lambda i: (0, i))
    colh = pl.BlockSpec((A_HEADS, tq), lambda i: (0, i))
    out = pl.pallas_call(
        body,
        out_shape=(_sds((A_WIDTH, s), F32), _sds((s, A_KV_WIDTH), F32), _sds((s, A_KV_WIDTH), F32)) + scatter_out_shapes(scatter[:na]),
        grid=(nq,),
        in_specs=[colq, colq, _full((s, A_KV_WIDTH)), _full((A_KV_WIDTH, s)), _full((s, A_KV_WIDTH)), colh, colh] + [_ANY] * ns,
        out_specs=(colq, _full((s, A_KV_WIDTH)), _full((s, A_KV_WIDTH))) + (_ANY,) * ns,
        scratch_shapes=[pltpu.VMEM((A_HEADS, A_KV_WIDTH, tq), BF16), pltpu.VMEM((A_HEADS, A_KV_WIDTH, tq), BF16),
                        pltpu.VMEM((A_HEADS, A_KV_WIDTH, tq), F32)] + (scatter_sems(na) if ns else []),
        compiler_params=_cp("arbitrary"), name="attn_bwd_scatter" if ns else "attn_bwd")(
            q_t, do_t, kr, kr_t, vb, lse, delta, *scatter)
    return out[0], out[1], out[2], list(out[3:3 + na]), list(out[3 + na:])


def qk_prep_bwd(proj, dq_t, dkr, dvb, tabs, qg, kg, gq, gk, fold_q, fold_k):
    s = proj.shape[0]
    tm = min(s, 1024)
    c, sa, sb = tabs

    def head_norm_bwd(x, dn, gain, gones, fold):
        ms = _group_sum(x * x, gones) * (1.0 / A_HEAD_DIM)
        r = lax.rsqrt(ms + EPS)
        xh = x * r
        gg = _dot_hi(jnp.sum(dn * xh, axis=0, keepdims=True), fold)
        u = dn * gain
        mean_u = _group_sum(u * xh, gones) * (1.0 / A_HEAD_DIM)
        return r * (u - xh * mean_u), gg

    def body(p_ref, dqt_ref, dk_ref, dv_ref, c_ref, sa_ref, sb_ref, qg_ref, kg_ref, gq_ref, gk_ref, fq_ref, fk_ref,
             dqkv_ref, gqg_ref, gkg_ref):
        @pl.when(pl.program_id(0) == 0)
        def _():
            gqg_ref[...] = jnp.zeros_like(gqg_ref)
            gkg_ref[...] = jnp.zeros_like(gkg_ref)

        cc, ssa, ssb = c_ref[...], sa_ref[...], sb_ref[...]
        dqr = dqt_ref[...].T * Q_SCALE
        dqn = _rope_t(dqr, _tile4(cc), _tile4(ssa), _tile4(ssb))
        dxq, gq_ = head_norm_bwd(p_ref[:, O_QA:O_QA + A_WIDTH].astype(F32), dqn, qg_ref[...], gq_ref[...], fq_ref[...])
        dkn = _rope_t(dk_ref[...], cc, ssa, ssb)
        dxk, gk_ = head_norm_bwd(p_ref[:, O_KA:O_KA + A_KV_WIDTH].astype(F32), dkn, kg_ref[...], gk_ref[...], fk_ref[...])
        gqg_ref[...] += gq_
        gkg_ref[...] += gk_
        dqkv_ref[:, O_QA:O_QA + A_WIDTH] = dxq.astype(BF16)
        dqkv_ref[:, O_KA:O_KA + A_KV_WIDTH] = dxk.astype(BF16)
        dqkv_ref[:, O_VA:O_VA + A_KV_WIDTH] = (dv_ref[...] * (1.0 / LN2)).astype(BF16)

    tab = _rows(tm, LANES)
    return pl.pallas_call(
        body, out_shape=(_sds((s, PBLK), BF16), _sds((1, LANES), F32), _sds((1, LANES), F32)), grid=(s // tm,),
        in_specs=[_rows(tm, PBLK), pl.BlockSpec((A_WIDTH, tm), lambda i: (0, i)), _rows(tm, A_KV_WIDTH), _rows(tm, A_KV_WIDTH),
                  tab, tab, tab, _full((1, A_WIDTH)), _full((1, A_KV_WIDTH)), _full((A_WIDTH, A_WIDTH)),
                  _full((A_KV_WIDTH, A_KV_WIDTH)), _full((A_WIDTH, LANES)), _full((A_KV_WIDTH, LANES))],
        out_specs=(_rows(tm, PBLK), _full((1, LANES)), _full((1, LANES))),
        compiler_params=_cp("arbitrary"), name="qk_prep_bwd")(proj, dq_t, dkr, dvb, c, sa, sb, qg, kg, gq, gk, fold_q, fold_k)


def _pick_dproj(b, d0, d1, d2, use):
    first_lg = 1 + MID_W // PBLK

    @pl.when(b == 0)
    def _():
        use(d0[...])

    @pl.when(jnp.logical_and(b >= 1, b < first_lg))
    def _():
        use(d1[...])

    @pl.when(b >= first_lg)
    def _():
        use(d2[...])


def win_grad(d0, d1, d2, h):
    s, d = h.shape
    tk = min(s, 4096)
    nk = s // tk

    def body(d0_ref, d1_ref, d2_ref, h_ref, o_ref, o16_ref):
        @pl.when(pl.program_id(1) == 0)
        def _():
            o_ref[...] = jnp.zeros_like(o_ref)

        def use(blk):
            o_ref[...] += _dot_tn(blk, h_ref[...])

        _pick_dproj(pl.program_id(0), d0_ref, d1_ref, d2_ref, use)

        @pl.when(pl.program_id(1) == nk - 1)
        def _():
            o16_ref[...] = o_ref[...].astype(BF16)

    def spec(first, count):
        def imap(j, k):
            used = jnp.logical_and(j >= first, j < first + count)
            return (jnp.where(used, k, 0), jnp.clip(j - first, 0, count - 1))
        return pl.BlockSpec((tk, PBLK), imap)

    nm = MID_W // PBLK
    oblk = pl.BlockSpec((PBLK, d), lambda j, k: (j, 0))
    return pl.pallas_call(
        body, out_shape=(_sds((IN_WIDTH, d), F32), _sds((IN_WIDTH, d), BF16)), grid=(N_PBLK, nk),
        in_specs=[spec(0, 1), spec(1, nm), spec(1 + nm, LG_W // PBLK),
                  pl.BlockSpec((tk, d), lambda j, k: (k, 0), pipeline_mode=pl.Buffered(1) if nk == 1 else None)],
        out_specs=(oblk, oblk),
        compiler_params=_cp("parallel", "arbitrary"), name="win_grad")(d0, d1, d2, h)


def h_bwd(d0, d1, d2, w_t, x, dx_out, g, scatter=(), swap=()):
    s, d = x.shape
    tm = min(s, 512)
    nt = s // tm
    ns = len(scatter)
    na = ns // 2
    nw = len(swap)

    def body(d0_ref, d1_ref, d2_ref, w_ref, x_ref, dxo_ref, g_ref, *rest):
        s_in, (dx_ref, gg_ref), s_out = rest[:ns], rest[ns + nw:ns + nw + 2], rest[ns + nw + 2:2 * ns + nw + 2]
        w_out, sems = rest[2 * ns + nw + 2:2 * (ns + nw) + 2], rest[2 * (ns + nw) + 2:]
        stages = []
        if ns:
            stages.append(scatter_stages([a.shape[1:] for a in scatter[:na]], s_in[:na], s_in[na:], s_out[:na], s_out[na:], *sems[:2]))
        if nw:
            stages.append(swap_stages([a.shape[0] for a in swap], w_out, *sems[-2:]))
        for start, _ in stages:
            pl.when(pl.program_id(0) == 0)(start)

        @pl.when(pl.program_id(0) == 0)
        def _():
            gg_ref[...] = jnp.zeros_like(gg_ref)

        dh = (_dot(d0_ref[...], w_ref[0:PBLK, :]) + _dot(d1_ref[...], w_ref[PBLK:PBLK + MID_W, :])
              + _dot(d2_ref[...], w_ref[PBLK + MID_W:, :]))
        xf = x_ref[...]
        r = lax.rsqrt(jnp.mean(xf * xf, axis=-1, keepdims=True) + EPS)
        xh = xf * r
        gg_ref[...] += jnp.sum(dh * xh, axis=0, keepdims=True)
        u = dh * g_ref[...]
        dx_ref[...] = dxo_ref[...] + r * (u - xh * jnp.mean(u * xh, axis=-1, keepdims=True))
        for _, finish in stages:
            pl.when(pl.program_id(0) == nt - 1)(finish)

    rowb = _rows(tm, d)
    out = pl.pallas_call(
        body, out_shape=(_sds((s, d), F32), _sds((1, d), F32)) + scatter_out_shapes(scatter[:na])
        + tuple(_sds(a.shape, F32) for a in swap), grid=(nt,),
        in_specs=[_rows(tm, PBLK), _rows(tm, MID_W), _rows(tm, LG_W),
                  pl.BlockSpec(w_t.shape, lambda i: (0, 0), pipeline_mode=pl.Buffered(1)), rowb, rowb, _full((1, d))]
        + [_ANY] * (ns + nw),
        out_specs=(rowb, _full((1, d))) + (_ANY,) * (ns + nw),
        input_output_aliases={7 + ns + a: 2 + ns + a for a in range(nw)},
        scratch_shapes=(scatter_sems(na) if ns else []) + (swap_sems(nw) if nw else []),
        compiler_params=_cp("arbitrary"), name="h_bwd_scatter" if ns else "h_bwd")(d0, d1, d2, w_t, x, dx_out, g, *scatter, *swap)
    return out[0], out[1], list(out[2:2 + na]), list(out[2 + na:2 + ns]), list(out[2 + ns:])


def memkv_bwd(mem, g, mem_n, w_kv, dkv):
    m, d = mem.shape

    def body(mem_ref, g_ref, mn_ref, w_ref, dkv_ref, gw_ref, gw16_ref, gg_ref):
        dkb = dkv_ref[...].astype(BF16)
        gw = _dot_tn(mn_ref[...], dkb)
        gw_ref[...] = gw
        gw16_ref[...] = gw.astype(BF16)
        dmn = _dot_nt(dkb, w_ref[...])
        mf = mem_ref[...]
        r = lax.rsqrt(jnp.mean(mf * mf, axis=-1, keepdims=True) + EPS)
        gg_ref[...] = jnp.sum(dmn * (mf * r), axis=0, keepdims=True)

    return pl.pallas_call(
        body, out_shape=(_sds(w_kv.shape, F32), _sds(w_kv.shape, BF16), _sds((1, d), F32)),
        compiler_params=_cp(), name="memkv_bwd")(mem, g, mem_n, w_kv, dkv)


def _layer_consts(seq):
    i = jnp.arange(A_WIDTH)
    return dict(
        tabs=rope_tables(seq),
        gq=_group_ones(A_WIDTH, A_HEAD_DIM).astype(BF16), gk=_group_ones(A_KV_WIDTH, A_HEAD_DIM).astype(BF16),
        fold_q=(i[:, None] % A_HEAD_DIM == jnp.arange(LANES)[None, :]).astype(F32),
        fold_k=(i[:A_KV_WIDTH, None] % A_HEAD_DIM == jnp.arange(LANES)[None, :]).astype(F32),
        head_sel=(jnp.arange(A_HEADS)[:, None] == i[None, :] // A_HEAD_DIM).astype(F32),
    )


_BIG = ("win_t", "wkv", "wbr", "wout")


def _with_own_part(names, gathered, shards, chip, d):
    shape = dict(win_t=(IN_WIDTH, d), wkv=(d, 2 * M_WIDTH), wbr=(N_CHIPS, N_BRANCH, A_WIDTH, d // N_CHIPS), wout=(d, d))
    return {n: lax.dynamic_update_slice(g, sh[None], (chip, 0, 0)).reshape(shape[n]) for n, g, sh in zip(names, gathered, shards)}


def local_fwd_bwd(x, mem, tgt, small, big=None, shards=None, place=None):
    s, d = x.shape
    depth = small["norm_g"].shape[0]
    k = _layer_consts(s)
    row = lambda v: v.reshape(1, -1)
    dist = shards is not None
    if dist:
        big = [None] * depth
    saved = []
    for l in range(depth):
        ng = row(small["norm_g"][l])
        qg = row(jnp.tile(small["q_norm_g"][l], A_HEADS))
        kg = row(jnp.tile(small["k_norm_g"][l], A_KV_HEADS))
        ws = small["w_s"][l].astype(BF16)
        ws_t = jnp.swapaxes(small["w_s"][l], 1, 2).astype(BF16)
        bsb = jnp.broadcast_to(small["b_s"][l][:, :, None], (B_GROUPS, CHUNK, B_GROUP_DIM))
        lng, lnb = row(small["sg_ln_g"][l]), row(small["sg_ln_b"][l])
        mg = row(small["mem_norm_g"][l])
        if l == 0:
            first = tuple(shards[0][:1]) if dist else ()
            h, gathered = rms_fwd(x, ng, gather=first)
            if dist:
                big[0] = _with_own_part(_BIG[:1], gathered, first, place[0], d)
        else:
            h = h_next
        w = big[l]
        late = tuple(shards[0][1:]) if dist and l == 0 else ()
        proj, gathered = proj_fwd(h, w["win_t"], gather=late)
        if late:
            w.update(_with_own_part(_BIG[1:], gathered, late, place[0], d))
        q_t, kr, vb, kr_t, vte0, vte1 = qk_prep(proj, k["tabs"], qg, kg, k["gq"], k["gk"])
        nxt = tuple(shards[l + 1]) if dist and l + 1 < depth else ()
        o_a, lse, gathered = attn_fwd(q_t, kr, vte0, vte1, gather=nxt)
        if nxt:
            big[l + 1] = _with_own_part(_BIG, gathered, nxt, place[0], d)
        mem_n, kv = memkv_fwd(mem, mg, w["wkv"])
        x_in = x
        if l + 1 < depth:
            y, up, merged, x, h_next = branch_fwd(x, proj, o_a, kv, ws, bsb, lng, lnb, w["wbr"], w["wout"], row(small["norm_g"][l + 1]))
        else:
            y, up, merged, sq, dx, g_final = branch_fwd(x, proj, o_a, kv, ws, bsb, lng, lnb, w["wbr"], w["wout"],
                                                        row(small["final_g"]), tgt=tgt)
        saved.append(dict(x=x_in, ng=ng, qg=qg, kg=kg, ws=ws, ws_t=ws_t, bsb=bsb, lng=lng, lnb=lnb, mg=mg, h=h, proj=proj,
                          q_t=q_t, kr=kr, kr_t=kr_t, vb=vb, o_a=o_a, lse=lse, mem_n=mem_n, kv=kv, y=y, up=up, merged=merged))

    grads = {n: [None] * depth for n in ("norm_g", "q_norm_g", "k_norm_g", "sg_ln_g", "sg_ln_b", "w_s", "b_s", "mem_norm_g")}
    parts = lambda g: g.reshape(N_CHIPS, -1, g.shape[-1])
    reduced = [[None] * len(_BIG) for _ in range(depth)]

    def reduce_all(items, t_sib, t_rem):
        if items:
            for (ll, a, _, _), f in zip(items, reduce_rows(place, [i[2] for i in items], t_sib, t_rem)):
                reduced[ll][a] = f

    as_scatter = lambda items: tuple(i[2] for i in items) + tuple(i[3] for i in items)
    pending = []
    for l in reversed(range(depth)):
        sv, w = saved[l], big[l]
        dy, dlg, g_wout, g_wbr, g_wout16, g_wbr16 = merge_bwd(dx, sv["proj"], sv["y"], sv["up"], sv["merged"], w["wbr"], w["wout"])
        dmid, do_t, delta, g_ws, g_bs, g_lng, g_lnb, dkv = branch_bwd(
            dy, sv["proj"], sv["o_a"], sv["kv"], sv["ws"], sv["ws_t"], sv["bsb"], sv["lng"], sv["lnb"], k["head_sel"])
        g_wkv, g_wkv16, g_mg = memkv_bwd(mem, sv["mg"], sv["mem_n"], w["wkv"], dkv)
        if dist:
            pending += [(l, 1, parts(g_wkv), parts(g_wkv16)), (l, 2, parts(g_wbr), parts(g_wbr16)), (l, 3, parts(g_wout), parts(g_wout16))]
        dq_t, dkr, dvb, t_sib, t_rem = attn_bwd(sv["q_t"], do_t, sv["kr"], sv["kr_t"], sv["vb"], sv["lse"], delta,
                                                scatter=as_scatter(pending))
        reduce_all(pending, t_sib, t_rem)
        dqkv, g_qg, g_kg = qk_prep_bwd(sv["proj"], dq_t, dkr, dvb, k["tabs"], sv["qg"], sv["kg"], k["gq"], k["gk"],
                                       k["fold_q"], k["fold_k"])
        g_win, g_win16 = win_grad(dqkv, dmid, dlg, sv["h"])
        pending = [(l, 0, parts(g_win), parts(g_win16))] if dist else []
        last = as_scatter(pending) if l == 0 else ()
        done = [(ll, a) for ll in range(depth) for a in range(len(_BIG)) if reduced[ll][a] is not None] if last else []
        dx, g_ng, t_sib, t_rem, swapped = h_bwd(dqkv, dmid, dlg, w["win_t"], sv["x"], dx, sv["ng"], scatter=last,
                                                swap=tuple(reduced[ll][a] for ll, a in done))
        for (ll, a), f in zip(done, swapped):
            reduced[ll][a] = f
        if last:
            reduce_all(pending, t_sib, t_rem)
        grads["norm_g"][l] = g_ng[0]
        grads["q_norm_g"][l] = g_qg[0, :A_HEAD_DIM]
        grads["k_norm_g"][l] = g_kg[0, :A_HEAD_DIM]
        grads["sg_ln_g"][l] = g_lng[0]
        grads["sg_ln_b"][l] = g_lnb[0]
        grads["w_s"][l] = g_ws
        grads["b_s"][l] = g_bs[:, :, 0]
        grads["mem_norm_g"][l] = g_mg[0]
        if not dist:
            reduced[l] = dict(zip(_BIG, (parts(g_win), parts(g_wkv), parts(g_wbr), parts(g_wout))))
    grads = {n: jnp.stack(v) for n, v in grads.items()}
    grads["final_g"] = g_final[0]
    return sq[0, 0], dx, grads, reduced


def _row_block(rows, width, cap_bytes=2 * 2**20):
    best = None
    for br in range(8, rows + 1, 8):
        if rows % br == 0 and br * width * 4 <= cap_bytes:
            best = br
    return best if best is not None else rows


def adamw(w, gs, m, v):
    r, c = w.shape
    n = len(gs)
    rs = r // n
    br = _row_block(rs, c)
    nb = rs // br

    def body(w_ref, *refs):
        g_refs, (m_ref, v_ref, og_ref, d_ref, nm_ref, nv_ref) = refs[:n], refs[n:]

        def update(gg):
            mm = ADAM_B1 * m_ref[...] + (1.0 - ADAM_B1) * gg
            vv = ADAM_B2 * v_ref[...] + (1.0 - ADAM_B2) * (gg * gg)
            m_hat = mm / (1.0 - ADAM_B1 ** ADAM_STEP)
            v_hat = vv / (1.0 - ADAM_B2 ** ADAM_STEP)
            og_ref[...] = gg
            d_ref[...] = -ADAM_LR * (m_hat / (jnp.sqrt(v_hat) + ADAM_EPS) + ADAM_WD * w_ref[...])
            nm_ref[...] = mm
            nv_ref[...] = vv

        for k in range(n):
            pl.when(pl.program_id(0) == k)(functools.partial(lambda k: update(g_refs[k][...]), k))

    blk = pl.BlockSpec((br, c), lambda l, i: (l * nb + i, 0))
    g_specs = [pl.BlockSpec((br, c), functools.partial(lambda l, i, k: (jnp.where(l == k, i, 0), 0), k=k)) for k in range(n)]
    return pl.pallas_call(
        body, out_shape=(_sds((r, c), F32),) * 4, grid=(n, nb), in_specs=[blk] + g_specs + [blk, blk], out_specs=(blk,) * 4,
        compiler_params=_cp("arbitrary", "arbitrary"), name="adamw")(w, *gs, m, v)


N_REMOTE = 2 * (N_CHIPS - 1)


def reduce_rows(place, gs, t_sibs, t_rems):
    n = len(gs)
    nt = 2

    def body(place_ref, *refs):
        for a in range(n):
            g_ref, s_ref, t_ref, f_ref = refs[a], refs[n + a], refs[2 * n + a], refs[3 * n + a]
            acc = g_ref[...] + s_ref[...]
            for j in range(N_REMOTE):
                acc = acc + t_ref[j].astype(F32)
            f_ref[...] = acc

    tiles = [(g.shape[1] // 2 // nt, g.shape[2]) for g in gs]
    return pl.pallas_call(
        body, out_shape=tuple(_sds(g.shape[1:], F32) for g in gs),
        grid_spec=pltpu.PrefetchScalarGridSpec(
            num_scalar_prefetch=1, grid=(nt,),
            in_specs=[pl.BlockSpec((None, tr, c), lambda i, p: (p[0], p[1] * nt + i, 0)) for tr, c in tiles]
            + [pl.BlockSpec((tr, c), lambda i, p: (i, 0)) for tr, c in tiles]
            + [pl.BlockSpec((N_REMOTE, tr, c), lambda i, p: (0, i, 0)) for tr, c in tiles],
            out_specs=tuple(pl.BlockSpec((tr, c), lambda i, p: (p[1] * nt + i, 0)) for tr, c in tiles)),
        compiler_params=_cp("parallel"), name="reduce_rows")(place, *gs, *t_sibs, *t_rems)


_ANY = pl.BlockSpec(memory_space=pl.ANY)


def _place():
    x, y, c = lax.axis_index("x"), lax.axis_index("y"), lax.axis_index("c")
    chips = [(1 - x, y), (x, 1 - y), (1 - x, 1 - y)]
    return x, y, c, chips


def gather_sems(n):
    return [pltpu.SemaphoreType.DMA((n, N_REMOTE)), pltpu.SemaphoreType.DMA((n, N_REMOTE))]


def gather_stages(shapes, ins, outs, send, recv):
    n = len(shapes)
    x, y, c, chips = _place()
    me = 2 * x + y
    sib = (x, y, 1 - c)

    def rows(a, hl):
        r2 = shapes[a][0] // 2
        return pl.ds(hl * r2, r2)

    def remote(a, k, src, dst, dev):
        return pltpu.make_async_remote_copy(src, dst, send.at[a, k], recv.at[a, k], device_id=dev, device_id_type=MESH)

    def sent(a, k):
        cx, cy = chips[k]
        return remote(a, k, ins[a].at[rows(a, c)], outs[a].at[me, rows(a, c)], (cx, cy, c))

    def got(a, k, hl):
        cx, cy = chips[k]
        return outs[a].at[2 * cx + cy, rows(a, hl)]

    def arrived(a, k):
        return remote(a, k, got(a, k, c), got(a, k, c), (*chips[k], c))

    def passed(a, k, hl):
        return remote(a, 3 + k, got(a, k, hl), got(a, k, hl), sib)

    def start():
        for a in range(n):
            for k in range(3):
                sent(a, k).start()

    def forward():
        for k in range(3):
            for a in range(n):
                arrived(a, k).wait_recv()
                passed(a, k, c).start()

    def finish():
        for k in range(3):
            for a in range(n):
                passed(a, k, 1 - c).wait_recv()
        for k in range(3):
            for a in range(n):
                sent(a, k).wait_send()
                passed(a, k, c).wait_send()

    return start, forward, finish


def scatter_sems(n):
    return [pltpu.SemaphoreType.DMA((n, N_REMOTE + 1)), pltpu.SemaphoreType.DMA((n, N_REMOTE + 1))]


def scatter_out_shapes(gs):
    return (tuple(_sds((g.shape[1] // 2, g.shape[2]), F32) for g in gs)
            + tuple(_sds((N_REMOTE, g.shape[1] // 2, g.shape[2]), BF16) for g in gs))


def scatter_stages(shapes, gf, gb, t_sib, t_rem, send, recv):
    n = len(shapes)
    x, y, c, chips = _place()
    me = 2 * x + y

    def copies():
        out = []
        for a in range(n):
            r2 = shapes[a][0] // 2
            out.append(pltpu.make_async_remote_copy(gf[a].at[me, pl.ds((1 - c) * r2, r2)], t_sib[a], send.at[a, N_REMOTE],
                                                    recv.at[a, N_REMOTE], device_id=(x, y, 1 - c), device_id_type=MESH))
            for k, (cx, cy) in enumerate(chips):
                for o in range(2):
                    tc = c if o == 0 else 1 - c
                    out.append(pltpu.make_async_remote_copy(gb[a].at[2 * cx + cy, pl.ds(tc * r2, r2)], t_rem[a].at[2 * k + o],
                                                            send.at[a, 2 * k + o], recv.at[a, 2 * k + o],
                                                            device_id=(cx, cy, tc), device_id_type=MESH))
        return out

    def start():
        for cp in copies():
            cp.start()

    def finish():
        for cp in copies():
            cp.wait()

    return start, finish


def swap_sems(n):
    return [pltpu.SemaphoreType.DMA((n,)), pltpu.SemaphoreType.DMA((n,))]


def swap_stages(rows, bufs, send, recv):
    x, y, c, _ = _place()

    def copies(core):
        return [pltpu.make_async_remote_copy(b.at[pl.ds(core * (r // 2), r // 2)], b.at[pl.ds(core * (r // 2), r // 2)],
                                             send.at[a], recv.at[a], device_id=(x, y, 1 - c), device_id_type=MESH)
                for a, (r, b) in enumerate(zip(rows, bufs))]

    def start():
        for cp in copies(c):
            cp.start()

    def finish():
        for mine, theirs in zip(copies(c), copies(1 - c)):
            mine.wait_send()
            theirs.wait_recv()

    return start, finish


def finish_exchange(vs, fs):
    n, nv = len(fs), len(vs)
    ndev = 2 * N_CHIPS

    def body(*refs):
        v_refs, out, sum_refs = refs[:nv], refs[nv + n:nv + 2 * n], refs[nv + 2 * n:2 * (nv + n)]
        all_refs, (send, recv, loc, fsend, frecv) = refs[2 * (nv + n):3 * nv + 2 * n], refs[3 * nv + 2 * n:]
        x, y, c, chips = _place()
        me, sib = (x, y, c), (x, y, 1 - c)
        start_swaps, finish_swaps = swap_stages([f.shape[0] for f in fs], out, fsend, frecv)

        def slab(b, px, py, pc):
            return all_refs[b].at[4 * px + 2 * py + pc]

        def copy(b, k, block, to, own=False):
            return pltpu.make_async_remote_copy(v_refs[b] if own else slab(b, *block), slab(b, *block), send.at[b, k],
                                                recv.at[b, k], device_id=to, device_id_type=MESH)

        each = range(nv)
        mine = [pltpu.make_async_copy(v_refs[b], slab(b, *me), loc.at[b]) for b in each]
        first = ([copy(b, 0, me, sib, own=True) for b in each]
                 + [copy(b, 1 + j, me, (*chip, c), own=True) for j, chip in enumerate(chips) for b in each])
        for cp in mine + first:
            cp.start()
        start_swaps()
        passed =[copy(b, 4 + j, (*chip, c), sib) for j, chip in enumerate(chips) for b in each]
        for j, chip in enumerate(chips):
            for b in each:
                copy(b, 1 + j, (*chip, c), me).wait_recv()
                passed[nv * j + b].start()
        for b in each:
            copy(b, 0, sib, me).wait_recv()
        for j, chip in enumerate(chips):
            for b in each:
                copy(b, 4 + j, (*chip, 1 - c), me).wait_recv()
        for cp in first + passed:
            cp.wait_send()
        for cp in mine:
            cp.wait()
        for b in each:
            acc = all_refs[b][0].astype(F32)
            for i in range(1, ndev):
                acc = acc + all_refs[b][i].astype(F32)
            sum_refs[b][...] = acc
        finish_swaps()

    vm = pl.BlockSpec(memory_space=pltpu.VMEM)
    res = pl.pallas_call(
        body, out_shape=tuple(_sds(f.shape, F32) for f in fs) + tuple(_sds(v.shape, F32) for v in vs),
        in_specs=[vm] * nv + [_ANY] * n, out_specs=(_ANY,) * n + (vm,) * nv,
        input_output_aliases={nv + a: a for a in range(n)},
        scratch_shapes=[pltpu.VMEM((ndev,) + v.shape, v.dtype) for v in vs]
        + [pltpu.SemaphoreType.DMA((nv, 7)), pltpu.SemaphoreType.DMA((nv, 7)), pltpu.SemaphoreType.DMA((nv,))] + swap_sems(n),
        compiler_params=pltpu.CompilerParams(vmem_limit_bytes=VMEM_LIMIT), name="finish_exchange")(*vs, *fs)
    return list(res[n:]), list(res[:n])


_SMALL = ("norm_g", "q_norm_g", "k_norm_g", "sg_ln_g", "sg_ln_b", "w_s", "b_s", "mem_norm_g", "final_g")
_WEIGHTS = ("norm_g", "w_in", "q_norm_g", "k_norm_g", "sg_ln_g", "sg_ln_b", "w_s", "b_s", "mem_norm_g", "w_mem_kv", "w_br",
            "w_out", "final_g")


def _pack(d, tail=None, names=_SMALL):
    flat = jnp.concatenate([d[n].reshape(-1) for n in names] + ([tail.reshape(1)] if tail is not None else []))
    rows = -(-(sum(d[n].size for n in names) + 1) // (8 * LANES)) * 8
    return jnp.pad(flat, (0, rows * LANES - flat.shape[0])).reshape(rows, LANES)


def _unpack(p, like, names=_SMALL):
    flat, out, o = p.reshape(-1), {}, 0
    for n in names:
        out[n] = flat[o:o + like[n].size].reshape(like[n].shape)
        o += like[n].size
    return out


def kernel(x, mem, norm_g, w_in, q_norm_g, k_norm_g, sg_ln_g, sg_ln_b, w_s, b_s, mem_norm_g, w_mem_kv, w_br, w_out, final_g, loss_target, m_norm_g, m_w_in, m_q_norm_g, m_k_norm_g, m_sg_ln_g, m_sg_ln_b, m_w_s, m_b_s, m_mem_norm_g, m_w_mem_kv, m_w_br, m_w_out, m_final_g, v_norm_g, v_w_in, v_q_norm_g, v_k_norm_g, v_sg_ln_g, v_sg_ln_b, v_w_s, v_b_s, v_mem_norm_g, v_w_mem_kv, v_w_br, v_w_out, v_final_g):
    w = dict(norm_g=norm_g, w_in=w_in, q_norm_g=q_norm_g, k_norm_g=k_norm_g, sg_ln_g=sg_ln_g, sg_ln_b=sg_ln_b, w_s=w_s, b_s=b_s,
             mem_norm_g=mem_norm_g, w_mem_kv=w_mem_kv, w_br=w_br, w_out=w_out, final_g=final_g)
    m = dict(norm_g=m_norm_g, w_in=m_w_in, q_norm_g=m_q_norm_g, k_norm_g=m_k_norm_g, sg_ln_g=m_sg_ln_g, sg_ln_b=m_sg_ln_b,
             w_s=m_w_s, b_s=m_b_s, mem_norm_g=m_mem_norm_g, w_mem_kv=m_w_mem_kv, w_br=m_w_br, w_out=m_w_out, final_g=m_final_g)
    v = dict(norm_g=v_norm_g, w_in=v_w_in, q_norm_g=v_q_norm_g, k_norm_g=v_k_norm_g, sg_ln_g=v_sg_ln_g, sg_ln_b=v_sg_ln_b,
             w_s=v_w_s, b_s=v_b_s, mem_norm_g=v_mem_norm_g, w_mem_kv=v_w_mem_kv, w_br=v_w_br, w_out=v_w_out, final_g=v_final_g)
    depth, d = norm_g.shape
    nsh = N_CHIPS
    br_rows = N_BRANCH * A_WIDTH
    br_cols = d // nsh

    shards = [[jnp.swapaxes(w_in[l], 0, 1).astype(BF16), w_mem_kv[l].astype(BF16), w_br[l].astype(BF16).reshape(br_rows, br_cols),
               w_out[l].astype(BF16)] for l in range(depth)]
    place = jnp.stack([2 * lax.axis_index("x") + lax.axis_index("y"), lax.axis_index("c")]).astype(jnp.int32)
    small = {n: w[n] for n in _SMALL}

    sq, dx, grads, reduced = local_fwd_bwd(x[0], mem[0], loss_target[0], small, shards=shards, place=place)

    narrow = tuple(n for n in _SMALL if n != "w_s")
    (sum_narrow, sum_ws), reduced[0][:1] = finish_exchange(
        [_pack(grads, tail=sq, names=narrow), grads["w_s"].astype(BF16).reshape(-1, LANES)], reduced[0][:1])
    finals = [g for layer in reduced for g in layer]
    loss = (0.5 / d) * sum_narrow.reshape(-1)[sum(small[n].size for n in narrow)]
    big_grads = dict(zip(("w_in", "w_mem_kv", "w_br", "w_out"), [finals[a::len(_BIG)] for a in range(len(_BIG))]))
    small_grads = dict(_unpack(sum_narrow, small, names=narrow), w_s=sum_ws.reshape(w_s.shape))
    small_sum = _pack(small_grads)

    out_g, out_d, out_m, out_v = {}, {}, {}, {}
    _, sd, sm, sv = adamw(_pack(small), [small_sum], _pack({n: m[n] for n in _SMALL}), _pack({n: v[n] for n in _SMALL}))
    sd, sm, sv = _unpack(sd, small), _unpack(sm, small), _unpack(sv, small)
    for n in _SMALL:
        out_g[n], out_d[n], out_m[n], out_v[n] = small_grads[n], sd[n], sm[n], sv[n]
    for n, gs in big_grads.items():
        into = (lambda a: jnp.swapaxes(a, 1, 2)) if n == "w_in" else (lambda a: a)
        two_d = lambda a: a.reshape(-1, gs[0].shape[-1])
        res = adamw(two_d(into(w[n])), gs, two_d(into(m[n])), two_d(into(v[n])))
        out_g[n], out_d[n], out_m[n], out_v[n] = [into(t.reshape(into(w[n]).shape)) for t in res]
    return (loss, dx[None], *[out_g[n] for n in _WEIGHTS], *[out_d[n] for n in _WEIGHTS], *[out_m[n] for n in _WEIGHTS],
            *[out_v[n] for n in _WEIGHTS])
```
